```python
import math
import jax, jax.numpy as jnp
from jax import lax
import numpy as np

D_MODEL = 1024
BATCH = 8
SEQ = 4096
DEPTH = 2

N_MIXERS = 2
N_MAMBA_LAYERS = (DEPTH + 1) // 2
N_ATTN_LAYERS = DEPTH // 2

SSM_EXPAND = 2
D_INNER = SSM_EXPAND * D_MODEL
SSM_HEAD_DIM = 64
SSM_HEADS = D_INNER // SSM_HEAD_DIM
SSM_GROUPS = 4
SSM_HEADS_PER_GROUP = SSM_HEADS // SSM_GROUPS
SSM_STATE = 128
CONV_WIDTH = 4
SSD_CHUNK = 128
CONV_DIM = D_INNER + 2 * SSM_GROUPS * SSM_STATE
IN_PROJ_DIM = 2 * D_INNER + 2 * SSM_GROUPS * SSM_STATE + SSM_HEADS

ATT_HEAD_DIM = 64
ATT_HEADS = D_MODEL // ATT_HEAD_DIM
DIL_PATTERNS = ((128, 1), (512, 4), (2048, 16))
N_DIL_GROUPS = len(DIL_PATTERNS)
QKV_DIM = N_DIL_GROUPS * 3 * ATT_HEADS * ATT_HEAD_DIM

FFN_HIDDEN = ((-(-8 * D_MODEL // 3) + 255) // 256) * 256

PLE_DIM = 256

NORM_EPS = 1e-6

kernel_name = "hybrid_ssd_dilated_attn_trunk"


def rmsnorm(x, gain):
    xf = x.astype(jnp.float32)
    y = xf * lax.rsqrt(jnp.mean(xf * xf, axis=-1, keepdims=True) + NORM_EPS)
    return (y * gain.astype(jnp.float32)).astype(x.dtype)


def causal_depthwise_conv(u, w, bias):
    k_width, chans = w.shape
    out = lax.conv_general_dilated(
        u, w[:, None, :].astype(u.dtype), window_strides=(1,),
        padding=[(k_width - 1, 0)], dimension_numbers=("NWC", "WIO", "NWC"),
        feature_group_count=chans)
    return out + bias.astype(u.dtype)


def ssd_chunked(x, dt, a, bm, cm):
    f32 = jnp.float32
    b, t = x.shape[:2]
    nc, cl = t // SSD_CHUNK, SSD_CHUNK
    g, hg = SSM_GROUPS, SSM_HEADS_PER_GROUP
    xs = (x.astype(f32) * dt[..., None]).reshape(b, nc, cl, g, hg, SSM_HEAD_DIM)
    a_dt = (dt * a).reshape(b, nc, cl, g, hg).transpose(0, 1, 3, 4, 2)
    a_cs = jnp.cumsum(a_dt, axis=-1)
    bc = bm.astype(f32).reshape(b, nc, cl, g, SSM_STATE)
    cc = cm.astype(f32).reshape(b, nc, cl, g, SSM_STATE)
    causal = jnp.tril(jnp.ones((cl, cl), dtype=bool))
    seg = a_cs[..., :, None] - a_cs[..., None, :]
    lmat = jnp.exp(jnp.where(causal, seg, -jnp.inf))
    cb = jnp.einsum("bclgn,bcsgn->bcgls", cc, bc)
    y_diag = jnp.einsum("bcgls,bcghls,bcsghp->bclghp", cb, lmat, xs)
    decay = jnp.exp(a_cs[..., -1:] - a_cs)
    states = jnp.einsum("bclgn,bcghl,bclghp->bcghpn", bc, decay, xs)
    chunk_decay = jnp.exp(a_cs[..., -1])

    def step(carry, inp):
        st, dec = inp
        return carry * dec[..., None, None] + st, carry

    init = jnp.zeros((b, g, hg, SSM_HEAD_DIM, SSM_STATE), f32)
    _, prev = lax.scan(step, init, (jnp.moveaxis(states, 1, 0), jnp.moveaxis(chunk_decay, 1, 0)))
    prev = jnp.moveaxis(prev, 0, 1)
    y_off = jnp.einsum("bclgn,bcghpn,bcghl->bclghp", cc, prev, jnp.exp(a_cs))
    return (y_diag + y_off).reshape(b, t, SSM_HEADS, SSM_HEAD_DIM)


def mamba2_mixer(h, w_in, conv_w, conv_b, dt_bias, a_log, d_skip, norm_w, w_out):
    b, t, _ = h.shape
    zxbcdt = h @ w_in.astype(h.dtype)
    z = zxbcdt[..., :D_INNER]
    xbc = zxbcdt[..., D_INNER:D_INNER + CONV_DIM]
    dt_raw = zxbcdt[..., D_INNER + CONV_DIM:]
    xbc = jax.nn.silu(causal_depthwise_conv(xbc, conv_w, conv_b))
    xs = xbc[..., :D_INNER]
    bm = xbc[..., D_INNER:D_INNER + SSM_GROUPS * SSM_STATE].reshape(b, t, SSM_GROUPS, SSM_STATE)
    cm = xbc[..., D_INNER + SSM_GROUPS * SSM_STATE:].reshape(b, t, SSM_GROUPS, SSM_STATE)
    dt = jax.nn.softplus(dt_raw.astype(jnp.float32) + dt_bias.astype(jnp.float32))
    a = -jnp.exp(a_log.astype(jnp.float32))
    xh = xs.reshape(b, t, SSM_HEADS, SSM_HEAD_DIM)
    y = ssd_chunked(xh, dt, a, bm, cm)
    y = y + xh.astype(jnp.float32) * d_skip.astype(jnp.float32)[:, None]
    y = y.reshape(b, t, D_INNER) * jax.nn.silu(z.astype(jnp.float32))
    y = rmsnorm(y.reshape(b, t, SSM_GROUPS, -1), norm_w.reshape(SSM_GROUPS, -1)).reshape(b, t, D_INNER)
    return y.astype(h.dtype) @ w_out.astype(h.dtype)


def alibi_slopes(n_heads):
    return 2.0 ** (-8.0 * jnp.arange(1, n_heads + 1, dtype=jnp.float32) / n_heads)


def dilated_group_attention(q, k, v, window, dilation, slopes):
    f32 = jnp.float32
    b, t, nh, e = q.shape
    span = window // dilation
    blk = span
    lu = t // dilation
    nb = -(-lu // blk)
    lp = nb * blk

    def to_blocks(arr):
        arr = arr.reshape(b, lu, dilation, nh, e)
        arr = jnp.pad(arr, ((0, 0), (0, lp - lu), (0, 0), (0, 0), (0, 0)))
        return arr.reshape(b, nb, blk, dilation, nh, e)

    qb, kb, vb = to_blocks(q.astype(f32)), to_blocks(k.astype(f32)), to_blocks(v.astype(f32))
    pad_prev = ((0, 0), (1, 0), (0, 0), (0, 0), (0, 0), (0, 0))
    kcat = jnp.concatenate([jnp.pad(kb, pad_prev)[:, :nb], kb], axis=2)
    vcat = jnp.concatenate([jnp.pad(vb, pad_prev)[:, :nb], vb], axis=2)
    scores = jnp.einsum("bnqrhe,bnkrhe->bnrhqk", qb, kcat) * (1.0 / math.sqrt(e))
    qi = jnp.arange(blk)[:, None]
    ki = jnp.arange(2 * blk)[None, :]
    dist = qi + blk - ki
    in_band = (dist >= 0) & (dist <= span)
    key_u = jnp.arange(nb)[:, None] * blk - blk + jnp.arange(2 * blk)[None, :]
    valid = in_band[None] & (key_u >= 0)[:, None, :]
    bias = -slopes[:, None, None] * (dilation * dist).astype(f32)[None]
    logits = jnp.where(valid[None, :, None, None], scores + bias[None, None, None], -jnp.inf)
    lse = jax.nn.logsumexp(logits, axis=-1)
    probs = jnp.exp(logits - lse[..., None])
    out = jnp.einsum("bnrhqk,bnkrhe->bnqrhe", probs, vcat)
    out = out.reshape(b, lp, dilation, nh, e)[:, :lu].reshape(b, t, nh, e)
    lse = lse.transpose(0, 1, 4, 2, 3).reshape(b, lp, dilation, nh)[:, :lu].reshape(b, t, nh)
    return out, lse


def dilated_attention_mixer(h, w_qkv, q_gain, k_gain, w_o):
    b, t, _ = h.shape
    qkv = (h @ w_qkv.astype(h.dtype)).reshape(b, t, N_DIL_GROUPS, 3, ATT_HEADS, ATT_HEAD_DIM)
    q = rmsnorm(qkv[:, :, :, 0], q_gain)
    k = rmsnorm(qkv[:, :, :, 1], k_gain)
    v = qkv[:, :, :, 2]
    slopes = alibi_slopes(ATT_HEADS)
    outs, lses = [], []
    for g, (window, dilation) in enumerate(DIL_PATTERNS):
        o_g, l_g = dilated_group_attention(q[:, :, g], k[:, :, g], v[:, :, g], window, dilation, slopes)
        outs.append(o_g)
        lses.append(l_g)
    alpha = jax.nn.softmax(jnp.stack(lses), axis=0)
    o = jnp.einsum("gbth,gbthe->bthe", alpha, jnp.stack(outs))
    return o.reshape(b, t, ATT_HEADS * ATT_HEAD_DIM).astype(h.dtype) @ w_o.astype(h.dtype)


def swiglu(h, w_gate, w_up, w_down):
    return (jax.nn.silu(h @ w_gate.astype(h.dtype)) * (h @ w_up.astype(h.dtype))) @ w_down.astype(h.dtype)


def _fwd_setup_inputs(seed: int = 0) -> dict:
    key = jax.random.key(seed)
    ks = jax.random.split(key, 24)
    f32 = jnp.float32

    def nrm(k, shape, scale):
        return jax.random.normal(k, shape, f32) * scale

    nm, na = N_MAMBA_LAYERS, N_ATTN_LAYERS
    dt0 = jnp.exp(jax.random.uniform(ks[8], (nm, SSM_HEADS), f32, math.log(1e-3), math.log(1e-1)))
    return {
        "x": nrm(ks[0], (BATCH, SEQ, D_MODEL), 1.0),
        "p": nrm(ks[1], (DEPTH, BATCH, SEQ, PLE_DIM), 1.0),
        "norm_mix": 1.0 + nrm(ks[2], (DEPTH, D_MODEL), 0.02),
        "norm_ffn": 1.0 + nrm(ks[3], (DEPTH, D_MODEL), 0.02),
        "ssm_w_in": nrm(ks[4], (nm, D_MODEL, IN_PROJ_DIM), D_MODEL ** -0.5),
        "ssm_conv_w": nrm(ks[5], (nm, CONV_WIDTH, CONV_DIM), CONV_WIDTH ** -0.5),
        "ssm_conv_b": nrm(ks[6], (nm, CONV_DIM), 0.02),
        "ssm_dt_bias": dt0 + jnp.log(-jnp.expm1(-dt0)),
        "ssm_a_log": jnp.log(jax.random.uniform(ks[9], (nm, SSM_HEADS), f32, 1.0, 16.0)),
        "ssm_d_skip": 1.0 + nrm(ks[10], (nm, SSM_HEADS), 0.1),
        "ssm_norm_w": 1.0 + nrm(ks[11], (nm, D_INNER), 0.02),
        "ssm_w_out": nrm(ks[12], (nm, D_INNER, D_MODEL), D_INNER ** -0.5),
        "att_w_qkv": nrm(ks[13], (na, D_MODEL, QKV_DIM), D_MODEL ** -0.5),
        "att_q_norm": 1.0 + nrm(ks[14], (na, ATT_HEAD_DIM), 0.02),
        "att_k_norm": 1.0 + nrm(ks[15], (na, ATT_HEAD_DIM), 0.02),
        "att_w_o": nrm(ks[16], (na, ATT_HEADS * ATT_HEAD_DIM, D_MODEL), (ATT_HEADS * ATT_HEAD_DIM) ** -0.5),
        "ffn_w_gate": nrm(ks[17], (DEPTH, D_MODEL, FFN_HIDDEN), D_MODEL ** -0.5),
        "ffn_w_up": nrm(ks[18], (DEPTH, D_MODEL, FFN_HIDDEN), D_MODEL ** -0.5),
        "ffn_w_down": nrm(ks[19], (DEPTH, FFN_HIDDEN, D_MODEL), FFN_HIDDEN ** -0.5),
        "ple_w_proj": nrm(ks[20], (DEPTH, PLE_DIM, D_MODEL), PLE_DIM ** -0.5),
        "ple_w_gate": nrm(ks[21], (DEPTH, D_MODEL, D_MODEL), D_MODEL ** -0.5),
    }


def _fwd_reference(x, p, norm_mix, norm_ffn, ssm_w_in, ssm_conv_w, ssm_conv_b, ssm_dt_bias,
              ssm_a_log, ssm_d_skip, ssm_norm_w, ssm_w_out, att_w_qkv, att_q_norm,
              att_k_norm, att_w_o, ffn_w_gate, ffn_w_up, ffn_w_down, ple_w_proj, ple_w_gate):
    for i in range(DEPTH):
        j = i // N_MIXERS
        h = rmsnorm(x, norm_mix[i])
        if i % N_MIXERS == 0:
            mix = mamba2_mixer(h, ssm_w_in[j], ssm_conv_w[j], ssm_conv_b[j], ssm_dt_bias[j],
                               ssm_a_log[j], ssm_d_skip[j], ssm_norm_w[j], ssm_w_out[j])
        else:
            mix = dilated_attention_mixer(h, att_w_qkv[j], att_q_norm[j], att_k_norm[j], att_w_o[j])
        x = x + mix.astype(x.dtype)
        x = x + swiglu(rmsnorm(x, norm_ffn[i]), ffn_w_gate[i], ffn_w_up[i], ffn_w_down[i]).astype(x.dtype)
        gate = jax.nn.sigmoid((x @ ple_w_gate[i].astype(x.dtype)).astype(jnp.float32))
        ple = (p[i].astype(x.dtype) @ ple_w_proj[i].astype(x.dtype)).astype(jnp.float32)
        x = x + (gate * ple).astype(x.dtype)
    return x


import jax as _jax
import jax.numpy as _jnp

TWIN_FORMAT = 'train_step'
FWD_PARAMS = ['x', 'p', 'norm_mix', 'norm_ffn', 'ssm_w_in', 'ssm_conv_w', 'ssm_conv_b', 'ssm_dt_bias', 'ssm_a_log', 'ssm_d_skip', 'ssm_norm_w', 'ssm_w_out', 'att_w_qkv', 'att_q_norm', 'att_k_norm', 'att_w_o', 'ffn_w_gate', 'ffn_w_up', 'ffn_w_down', 'ple_w_proj', 'ple_w_gate']
TWIN_WEIGHTS = ['norm_mix', 'norm_ffn', 'ssm_w_in', 'ssm_conv_w', 'ssm_conv_b', 'ssm_dt_bias', 'ssm_a_log', 'ssm_d_skip', 'ssm_norm_w', 'ssm_w_out', 'att_w_qkv', 'att_q_norm', 'att_k_norm', 'att_w_o', 'ffn_w_gate', 'ffn_w_up', 'ffn_w_down', 'ple_w_proj', 'ple_w_gate']
TWIN_DIFF_INPUT = 'x'
TWIN_INPUTS = ['x', 'p', 'norm_mix', 'norm_ffn', 'ssm_w_in', 'ssm_conv_w', 'ssm_conv_b', 'ssm_dt_bias', 'ssm_a_log', 'ssm_d_skip', 'ssm_norm_w', 'ssm_w_out', 'att_w_qkv', 'att_q_norm', 'att_k_norm', 'att_w_o', 'ffn_w_gate', 'ffn_w_up', 'ffn_w_down', 'ple_w_proj', 'ple_w_gate', 'loss_target', 'm_norm_mix', 'm_norm_ffn', 'm_ssm_w_in', 'm_ssm_conv_w', 'm_ssm_conv_b', 'm_ssm_dt_bias', 'm_ssm_a_log', 'm_ssm_d_skip', 'm_ssm_norm_w', 'm_ssm_w_out', 'm_att_w_qkv', 'm_att_q_norm', 'm_att_k_norm', 'm_att_w_o', 'm_ffn_w_gate', 'm_ffn_w_up', 'm_ffn_w_down', 'm_ple_w_proj', 'm_ple_w_gate', 'v_norm_mix', 'v_norm_ffn', 'v_ssm_w_in', 'v_ssm_conv_w', 'v_ssm_conv_b', 'v_ssm_dt_bias', 'v_ssm_a_log', 'v_ssm_d_skip', 'v_ssm_norm_w', 'v_ssm_w_out', 'v_att_w_qkv', 'v_att_q_norm', 'v_att_k_norm', 'v_att_w_o', 'v_ffn_w_gate', 'v_ffn_w_up', 'v_ffn_w_down', 'v_ple_w_proj', 'v_ple_w_gate']
TWIN_OUTPUTS = ['loss', 'grad_x', 'grad_norm_mix', 'grad_norm_ffn', 'grad_ssm_w_in', 'grad_ssm_conv_w', 'grad_ssm_conv_b', 'grad_ssm_dt_bias', 'grad_ssm_a_log', 'grad_ssm_d_skip', 'grad_ssm_norm_w', 'grad_ssm_w_out', 'grad_att_w_qkv', 'grad_att_q_norm', 'grad_att_k_norm', 'grad_att_w_o', 'grad_ffn_w_gate', 'grad_ffn_w_up', 'grad_ffn_w_down', 'grad_ple_w_proj', 'grad_ple_w_gate', 'delta_norm_mix', 'delta_norm_ffn', 'delta_ssm_w_in', 'delta_ssm_conv_w', 'delta_ssm_conv_b', 'delta_ssm_dt_bias', 'delta_ssm_a_log', 'delta_ssm_d_skip', 'delta_ssm_norm_w', 'delta_ssm_w_out', 'delta_att_w_qkv', 'delta_att_q_norm', 'delta_att_k_norm', 'delta_att_w_o', 'delta_ffn_w_gate', 'delta_ffn_w_up', 'delta_ffn_w_down', 'delta_ple_w_proj', 'delta_ple_w_gate', 'new_m_norm_mix', 'new_m_norm_ffn', 'new_m_ssm_w_in', 'new_m_ssm_conv_w', 'new_m_ssm_conv_b', 'new_m_ssm_dt_bias', 'new_m_ssm_a_log', 'new_m_ssm_d_skip', 'new_m_ssm_norm_w', 'new_m_ssm_w_out', 'new_m_att_w_qkv', 'new_m_att_q_norm', 'new_m_att_k_norm', 'new_m_att_w_o', 'new_m_ffn_w_gate', 'new_m_ffn_w_up', 'new_m_ffn_w_down', 'new_m_ple_w_proj', 'new_m_ple_w_gate', 'new_v_norm_mix', 'new_v_norm_ffn', 'new_v_ssm_w_in', 'new_v_ssm_conv_w', 'new_v_ssm_conv_b', 'new_v_ssm_dt_bias', 'new_v_ssm_a_log', 'new_v_ssm_d_skip', 'new_v_ssm_norm_w', 'new_v_ssm_w_out', 'new_v_att_w_qkv', 'new_v_att_q_norm', 'new_v_att_k_norm', 'new_v_att_w_o', 'new_v_ffn_w_gate', 'new_v_ffn_w_up', 'new_v_ffn_w_down', 'new_v_ple_w_proj', 'new_v_ple_w_gate']
TWIN_LEAF_KINDS = {'loss': 'loss', 'grad_x': 'grad_x', 'grad_norm_mix': 'grad_w', 'grad_norm_ffn': 'grad_w', 'grad_ssm_w_in': 'grad_w', 'grad_ssm_conv_w': 'grad_w', 'grad_ssm_conv_b': 'grad_w', 'grad_ssm_dt_bias': 'grad_w', 'grad_ssm_a_log': 'grad_w', 'grad_ssm_d_skip': 'grad_w', 'grad_ssm_norm_w': 'grad_w', 'grad_ssm_w_out': 'grad_w', 'grad_att_w_qkv': 'grad_w', 'grad_att_q_norm': 'grad_w', 'grad_att_k_norm': 'grad_w', 'grad_att_w_o': 'grad_w', 'grad_ffn_w_gate': 'grad_w', 'grad_ffn_w_up': 'grad_w', 'grad_ffn_w_down': 'grad_w', 'grad_ple_w_proj': 'grad_w', 'grad_ple_w_gate': 'grad_w', 'delta_norm_mix': 'delta_w', 'delta_norm_ffn': 'delta_w', 'delta_ssm_w_in': 'delta_w', 'delta_ssm_conv_w': 'delta_w', 'delta_ssm_conv_b': 'delta_w', 'delta_ssm_dt_bias': 'delta_w', 'delta_ssm_a_log': 'delta_w', 'delta_ssm_d_skip': 'delta_w', 'delta_ssm_norm_w': 'delta_w', 'delta_ssm_w_out': 'delta_w', 'delta_att_w_qkv': 'delta_w', 'delta_att_q_norm': 'delta_w', 'delta_att_k_norm': 'delta_w', 'delta_att_w_o': 'delta_w', 'delta_ffn_w_gate': 'delta_w', 'delta_ffn_w_up': 'delta_w', 'delta_ffn_w_down': 'delta_w', 'delta_ple_w_proj': 'delta_w', 'delta_ple_w_gate': 'delta_w', 'new_m_norm_mix': 'new_m', 'new_m_norm_ffn': 'new_m', 'new_m_ssm_w_in': 'new_m', 'new_m_ssm_conv_w': 'new_m', 'new_m_ssm_conv_b': 'new_m', 'new_m_ssm_dt_bias': 'new_m', 'new_m_ssm_a_log': 'new_m', 'new_m_ssm_d_skip': 'new_m', 'new_m_ssm_norm_w': 'new_m', 'new_m_ssm_w_out': 'new_m', 'new_m_att_w_qkv': 'new_m', 'new_m_att_q_norm': 'new_m', 'new_m_att_k_norm': 'new_m', 'new_m_att_w_o': 'new_m', 'new_m_ffn_w_gate': 'new_m', 'new_m_ffn_w_up': 'new_m', 'new_m_ffn_w_down': 'new_m', 'new_m_ple_w_proj': 'new_m', 'new_m_ple_w_gate': 'new_m', 'new_v_norm_mix': 'new_v', 'new_v_norm_ffn': 'new_v', 'new_v_ssm_w_in': 'new_v', 'new_v_ssm_conv_w': 'new_v', 'new_v_ssm_conv_b': 'new_v', 'new_v_ssm_dt_bias': 'new_v', 'new_v_ssm_a_log': 'new_v', 'new_v_ssm_d_skip': 'new_v', 'new_v_ssm_norm_w': 'new_v', 'new_v_ssm_w_out': 'new_v', 'new_v_att_w_qkv': 'new_v', 'new_v_att_q_norm': 'new_v', 'new_v_att_k_norm': 'new_v', 'new_v_att_w_o': 'new_v', 'new_v_ffn_w_gate': 'new_v', 'new_v_ffn_w_up': 'new_v', 'new_v_ffn_w_down': 'new_v', 'new_v_ple_w_proj': 'new_v', 'new_v_ple_w_gate': 'new_v'}


def _forward(args):
    return _fwd_reference(*[args[k] for k in FWD_PARAMS])


def _output_shape():
    def fwd():
        inp = _fwd_setup_inputs(0)
        return _fwd_reference(*[inp[k] for k in FWD_PARAMS])
    out = _jax.eval_shape(fwd)
    return out.shape, out.dtype

N_MICROBATCH = 1
ADAM_LR = 0.001
ADAM_B1 = 0.9
ADAM_B2 = 0.999
ADAM_EPS = 1e-08
ADAM_WD = 0.01
ADAM_STEP = 10
PER_EXAMPLE_BATCH_AXIS = {'x': 0, 'p': 1, 'loss_target': 0}
SHARED_INPUTS = []
_WEIGHT_DTYPES = {'norm_mix': _jnp.float32, 'norm_ffn': _jnp.float32, 'ssm_w_in': _jnp.float32, 'ssm_conv_w': _jnp.float32, 'ssm_conv_b': _jnp.float32, 'ssm_dt_bias': _jnp.float32, 'ssm_a_log': _jnp.float32, 'ssm_d_skip': _jnp.float32, 'ssm_norm_w': _jnp.float32, 'ssm_w_out': _jnp.float32, 'att_w_qkv': _jnp.float32, 'att_q_norm': _jnp.float32, 'att_k_norm': _jnp.float32, 'att_w_o': _jnp.float32, 'ffn_w_gate': _jnp.float32, 'ffn_w_up': _jnp.float32, 'ffn_w_down': _jnp.float32, 'ple_w_proj': _jnp.float32, 'ple_w_gate': _jnp.float32}
MOMENT_SCALE = {'norm_mix': 2.198211e+00, 'norm_ffn': 2.530267e+01, 'ssm_w_in': 4.182957e-01, 'ssm_conv_w': 9.727939e-01, 'ssm_conv_b': 3.237719e+00, 'ssm_dt_bias': 1.910288e+00, 'ssm_a_log': 9.968449e+00, 'ssm_d_skip': 9.300198e+00, 'ssm_norm_w': 2.330067e+01, 'ssm_w_out': 2.497783e+00, 'att_w_qkv': 2.728200e-01, 'att_q_norm': 1.823197e+01, 'att_k_norm': 1.824708e+01, 'att_w_o': 8.293288e-01, 'ffn_w_gate': 4.686426e-01, 'ffn_w_up': 3.340103e-01, 'ffn_w_down': 5.261277e-01, 'ple_w_proj': 5.889339e-01, 'ple_w_gate': 5.389801e-01}


def _to_microbatches(a, axis):
    t = _jnp.moveaxis(a, axis, 0)
    t = t.reshape((N_MICROBATCH, t.shape[0] // N_MICROBATCH) + t.shape[1:])
    return _jnp.moveaxis(t, 1, axis + 1)


def setup_inputs(seed: int = 0) -> dict:
    inp = _fwd_setup_inputs(seed)
    key = _jax.random.fold_in(_jax.random.key(seed), 7919)
    shape, _ = _output_shape()
    out = dict(inp)
    out["loss_target"] = _jax.random.normal(_jax.random.fold_in(key, 0), shape, _jnp.float32)
    for i, name in enumerate(TWIN_WEIGHTS):
        w = inp[name].astype(_jnp.float32)
        if MOMENT_SCALE is None:
            s = _jnp.sqrt(_jnp.mean(_jnp.square(w)) + 1e-30)
        else:
            s = MOMENT_SCALE[name]
        km, kv = _jax.random.split(_jax.random.fold_in(key, i + 1))
        out[name] = w
        out["m_" + name] = s * _jax.random.normal(km, w.shape, _jnp.float32)
        out["v_" + name] = (s * s) * _jax.random.uniform(kv, w.shape, _jnp.float32, 0.5, 1.5)
    if N_MICROBATCH > 1:
        for name, axis in PER_EXAMPLE_BATCH_AXIS.items():
            out[name] = _to_microbatches(out[name], axis)
    return {'x': out['x'], 'p': out['p'], 'norm_mix': out['norm_mix'], 'norm_ffn': out['norm_ffn'], 'ssm_w_in': out['ssm_w_in'], 'ssm_conv_w': out['ssm_conv_w'], 'ssm_conv_b': out['ssm_conv_b'], 'ssm_dt_bias': out['ssm_dt_bias'], 'ssm_a_log': out['ssm_a_log'], 'ssm_d_skip': out['ssm_d_skip'], 'ssm_norm_w': out['ssm_norm_w'], 'ssm_w_out': out['ssm_w_out'], 'att_w_qkv': out['att_w_qkv'], 'att_q_norm': out['att_q_norm'], 'att_k_norm': out['att_k_norm'], 'att_w_o': out['att_w_o'], 'ffn_w_gate': out['ffn_w_gate'], 'ffn_w_up': out['ffn_w_up'], 'ffn_w_down': out['ffn_w_down'], 'ple_w_proj': out['ple_w_proj'], 'ple_w_gate': out['ple_w_gate'], 'loss_target': out['loss_target'], 'm_norm_mix': out['m_norm_mix'], 'm_norm_ffn': out['m_norm_ffn'], 'm_ssm_w_in': out['m_ssm_w_in'], 'm_ssm_conv_w': out['m_ssm_conv_w'], 'm_ssm_conv_b': out['m_ssm_conv_b'], 'm_ssm_dt_bias': out['m_ssm_dt_bias'], 'm_ssm_a_log': out['m_ssm_a_log'], 'm_ssm_d_skip': out['m_ssm_d_skip'], 'm_ssm_norm_w': out['m_ssm_norm_w'], 'm_ssm_w_out': out['m_ssm_w_out'], 'm_att_w_qkv': out['m_att_w_qkv'], 'm_att_q_norm': out['m_att_q_norm'], 'm_att_k_norm': out['m_att_k_norm'], 'm_att_w_o': out['m_att_w_o'], 'm_ffn_w_gate': out['m_ffn_w_gate'], 'm_ffn_w_up': out['m_ffn_w_up'], 'm_ffn_w_down': out['m_ffn_w_down'], 'm_ple_w_proj': out['m_ple_w_proj'], 'm_ple_w_gate': out['m_ple_w_gate'], 'v_norm_mix': out['v_norm_mix'], 'v_norm_ffn': out['v_norm_ffn'], 'v_ssm_w_in': out['v_ssm_w_in'], 'v_ssm_conv_w': out['v_ssm_conv_w'], 'v_ssm_conv_b': out['v_ssm_conv_b'], 'v_ssm_dt_bias': out['v_ssm_dt_bias'], 'v_ssm_a_log': out['v_ssm_a_log'], 'v_ssm_d_skip': out['v_ssm_d_skip'], 'v_ssm_norm_w': out['v_ssm_norm_w'], 'v_ssm_w_out': out['v_ssm_w_out'], 'v_att_w_qkv': out['v_att_w_qkv'], 'v_att_q_norm': out['v_att_q_norm'], 'v_att_k_norm': out['v_att_k_norm'], 'v_att_w_o': out['v_att_w_o'], 'v_ffn_w_gate': out['v_ffn_w_gate'], 'v_ffn_w_up': out['v_ffn_w_up'], 'v_ffn_w_down': out['v_ffn_w_down'], 'v_ple_w_proj': out['v_ple_w_proj'], 'v_ple_w_gate': out['v_ple_w_gate']}


def _loss(weights, diff, rest, loss_target):
    with _jax.named_scope("forward"):
        args = {**rest, TWIN_DIFF_INPUT: diff, **{k: w.astype(_WEIGHT_DTYPES[k]) for k, w in weights.items()}}
        y = _forward(args)
    with _jax.named_scope("loss_head"):
        err = _jnp.square(y.astype(_jnp.float32) - loss_target)
        return 0.5 * _jnp.sum(_jnp.mean(err, axis=-1)) if err.ndim else 0.5 * err


def _adamw(w, g, m, v):
    m = ADAM_B1 * m + (1.0 - ADAM_B1) * g
    v = ADAM_B2 * v + (1.0 - ADAM_B2) * _jnp.square(g)
    m_hat = m / (1.0 - ADAM_B1 ** ADAM_STEP)
    v_hat = v / (1.0 - ADAM_B2 ** ADAM_STEP)
    delta = -ADAM_LR * (m_hat / (_jnp.sqrt(v_hat) + ADAM_EPS) + ADAM_WD * w)
    return delta, m, v


def reference(x, p, norm_mix, norm_ffn, ssm_w_in, ssm_conv_w, ssm_conv_b, ssm_dt_bias, ssm_a_log, ssm_d_skip, ssm_norm_w, ssm_w_out, att_w_qkv, att_q_norm, att_k_norm, att_w_o, ffn_w_gate, ffn_w_up, ffn_w_down, ple_w_proj, ple_w_gate, loss_target, m_norm_mix, m_norm_ffn, m_ssm_w_in, m_ssm_conv_w, m_ssm_conv_b, m_ssm_dt_bias, m_ssm_a_log, m_ssm_d_skip, m_ssm_norm_w, m_ssm_w_out, m_att_w_qkv, m_att_q_norm, m_att_k_norm, m_att_w_o, m_ffn_w_gate, m_ffn_w_up, m_ffn_w_down, m_ple_w_proj, m_ple_w_gate, v_norm_mix, v_norm_ffn, v_ssm_w_in, v_ssm_conv_w, v_ssm_conv_b, v_ssm_dt_bias, v_ssm_a_log, v_ssm_d_skip, v_ssm_norm_w, v_ssm_w_out, v_att_w_qkv, v_att_q_norm, v_att_k_norm, v_att_w_o, v_ffn_w_gate, v_ffn_w_up, v_ffn_w_down, v_ple_w_proj, v_ple_w_gate):
    given = dict(x=x, p=p, norm_mix=norm_mix, norm_ffn=norm_ffn, ssm_w_in=ssm_w_in, ssm_conv_w=ssm_conv_w, ssm_conv_b=ssm_conv_b, ssm_dt_bias=ssm_dt_bias, ssm_a_log=ssm_a_log, ssm_d_skip=ssm_d_skip, ssm_norm_w=ssm_norm_w, ssm_w_out=ssm_w_out, att_w_qkv=att_w_qkv, att_q_norm=att_q_norm, att_k_norm=att_k_norm, att_w_o=att_w_o, ffn_w_gate=ffn_w_gate, ffn_w_up=ffn_w_up, ffn_w_down=ffn_w_down, ple_w_proj=ple_w_proj, ple_w_gate=ple_w_gate, loss_target=loss_target, m_norm_mix=m_norm_mix, m_norm_ffn=m_norm_ffn, m_ssm_w_in=m_ssm_w_in, m_ssm_conv_w=m_ssm_conv_w, m_ssm_conv_b=m_ssm_conv_b, m_ssm_dt_bias=m_ssm_dt_bias, m_ssm_a_log=m_ssm_a_log, m_ssm_d_skip=m_ssm_d_skip, m_ssm_norm_w=m_ssm_norm_w, m_ssm_w_out=m_ssm_w_out, m_att_w_qkv=m_att_w_qkv, m_att_q_norm=m_att_q_norm, m_att_k_norm=m_att_k_norm, m_att_w_o=m_att_w_o, m_ffn_w_gate=m_ffn_w_gate, m_ffn_w_up=m_ffn_w_up, m_ffn_w_down=m_ffn_w_down, m_ple_w_proj=m_ple_w_proj, m_ple_w_gate=m_ple_w_gate, v_norm_mix=v_norm_mix, v_norm_ffn=v_norm_ffn, v_ssm_w_in=v_ssm_w_in, v_ssm_conv_w=v_ssm_conv_w, v_ssm_conv_b=v_ssm_conv_b, v_ssm_dt_bias=v_ssm_dt_bias, v_ssm_a_log=v_ssm_a_log, v_ssm_d_skip=v_ssm_d_skip, v_ssm_norm_w=v_ssm_norm_w, v_ssm_w_out=v_ssm_w_out, v_att_w_qkv=v_att_w_qkv, v_att_q_norm=v_att_q_norm, v_att_k_norm=v_att_k_norm, v_att_w_o=v_att_w_o, v_ffn_w_gate=v_ffn_w_gate, v_ffn_w_up=v_ffn_w_up, v_ffn_w_down=v_ffn_w_down, v_ple_w_proj=v_ple_w_proj, v_ple_w_gate=v_ple_w_gate)
    weights = {n: given[n] for n in TWIN_WEIGHTS}
    shared = {n: given[n] for n in SHARED_INPUTS}
    per_example = {n: given[n] for n in ['x', 'p']}
    grad_fn = _jax.value_and_grad(_loss, argnums=(0, 1))

    def one_microbatch(ex, loss_target):
        ex = dict(ex)
        diff = ex.pop(TWIN_DIFF_INPUT)
        return grad_fn(weights, diff, {**shared, **ex}, loss_target)

    if N_MICROBATCH == 1:
        loss, (grad_w, grad_x) = one_microbatch(per_example, given["loss_target"])
    else:
        def body(carry, xs):
            loss_sum, grad_sum = carry
            l_k, (gw_k, gx_k) = one_microbatch(xs[0], xs[1])
            with _jax.named_scope("update"):
                return (loss_sum + l_k, _jax.tree.map(_jnp.add, grad_sum, gw_k)), gx_k

        init = (_jnp.zeros((), _jnp.float32), _jax.tree.map(_jnp.zeros_like, weights))
        (loss, grad_w), grad_x = _jax.lax.scan(body, init, (per_example, given["loss_target"]))
    with _jax.named_scope("update"):
        delta_w, new_m, new_v = {}, {}, {}
        for n in TWIN_WEIGHTS:
            delta_w[n], new_m[n], new_v[n] = _adamw(weights[n], grad_w[n], given["m_" + n], given["v_" + n])
    return (loss, grad_x, *[grad_w[n] for n in TWIN_WEIGHTS], *[delta_w[n] for n in TWIN_WEIGHTS],
            *[new_m[n] for n in TWIN_WEIGHTS], *[new_v[n] for n in TWIN_WEIGHTS])
```

```python
import functools
import math

import jax
import jax.numpy as jnp
from jax import lax
from jax.experimental import pallas as pl
from jax.experimental.pallas import tpu as pltpu

F32 = jnp.float32
BF16 = jnp.bfloat16
N_DEV = 8
MESH = pl.DeviceIdType.MESH

SSM_HEAD_DIM = 64
SSM_GROUPS = 4
SSM_STATE = 128
CONV_WIDTH = 4
SSD_CHUNK = 128
ATT_HEAD_DIM = 64
DIL_PATTERNS = ((128, 1), (512, 4), (2048, 16))
NORM_EPS = 1e-6
ADAM_LR = 0.001
ADAM_B1 = 0.9
ADAM_B2 = 0.999
ADAM_EPS = 1e-08
ADAM_WD = 0.01
ADAM_STEP = 10

BF16_ROWS = 16
LANES = 128
SUBLANES = 8

WEIGHTS = ['norm_mix', 'norm_ffn', 'ssm_w_in', 'ssm_conv_w', 'ssm_conv_b', 'ssm_dt_bias', 'ssm_a_log', 'ssm_d_skip',
           'ssm_norm_w', 'ssm_w_out', 'att_w_qkv', 'att_q_norm', 'att_k_norm', 'att_w_o', 'ffn_w_gate', 'ffn_w_up',
           'ffn_w_down', 'ple_w_proj', 'ple_w_gate']
COL_SHARDED = ('ssm_w_in', 'att_w_qkv', 'ffn_w_gate', 'ffn_w_up', 'ple_w_proj')
ROW_SHARDED = ('ssm_w_out', 'att_w_o', 'ffn_w_down', 'ple_w_gate')
BIG = COL_SHARDED + ROW_SHARDED
SMALL = ('norm_mix', 'norm_ffn', 'ssm_conv_w', 'ssm_conv_b', 'ssm_dt_bias', 'ssm_a_log', 'ssm_d_skip', 'ssm_norm_w',
         'att_q_norm', 'att_k_norm')


def _pick(n, cands):
    for c in cands:
        if n % c == 0:
            return c
    return n


def _round_up(n, m):
    return -(-n // m) * m


MM_TILES = (1024, 1408, 512, 256, 128)
MM_VMEM_BYTES = 48 * 1024 * 1024


def _mm(a, b, *, ta=False, tb=False, out_dtype=F32, name):
    k_dim, m_dim = (a.shape if ta else a.shape[::-1])
    n_dim = b.shape[0] if tb else b.shape[1]
    assert (b.shape[1] if tb else b.shape[0]) == k_dim, (a.shape, b.shape, ta, tb)
    tm = _pick(m_dim, MM_TILES)
    tn = _pick(n_dim, MM_TILES)
    tk = _pick(k_dim, MM_TILES)
    nk = k_dim // tk
    a_spec = pl.BlockSpec((tk, tm), lambda i, j, k: (k, i)) if ta else pl.BlockSpec((tm, tk), lambda i, j, k: (i, k))
    b_spec = pl.BlockSpec((tn, tk), lambda i, j, k: (j, k)) if tb else pl.BlockSpec((tk, tn), lambda i, j, k: (k, j))
    dims = (((0 if ta else 1,), (1 if tb else 0,)), ((), ()))

    def dot(a_ref, b_ref):
        return lax.dot_general(a_ref[...].astype(BF16), b_ref[...].astype(BF16), dims, preferred_element_type=F32)

    def body_once(a_ref, b_ref, o_ref):
        o_ref[...] = dot(a_ref, b_ref).astype(o_ref.dtype)

    def body_acc(a_ref, b_ref, o_ref, acc_ref):
        k = pl.program_id(2)

        @pl.when(k == 0)
        def _():
            acc_ref[...] = dot(a_ref, b_ref)

        @pl.when((k > 0) & (k < nk - 1))
        def _():
            acc_ref[...] += dot(a_ref, b_ref)

        @pl.when(k == nk - 1)
        def _():
            o_ref[...] = (acc_ref[...] + dot(a_ref, b_ref)).astype(o_ref.dtype)

    return pl.pallas_call(
        body_once if nk == 1 else body_acc, name=f"{name}_{m_dim}x{n_dim}x{k_dim}",
        out_shape=jax.ShapeDtypeStruct((m_dim, n_dim), out_dtype),
        grid=(m_dim // tm, n_dim // tn, nk),
        in_specs=[a_spec, b_spec],
        out_specs=pl.BlockSpec((tm, tn), lambda i, j, k: (i, j)),
        scratch_shapes=[] if nk == 1 else [pltpu.VMEM((tm, tn), F32)],
        compiler_params=pltpu.CompilerParams(dimension_semantics=("parallel", "parallel", "arbitrary"),
                                             vmem_limit_bytes=MM_VMEM_BYTES),
    )(a, b)


@functools.partial(jax.custom_vjp, nondiff_argnums=(2,))
def lin_t(a, wt, out_dtype=F32):
    return _mm(a, wt, tb=True, name="lin_t_fwd", out_dtype=out_dtype)


def _lin_t_fwd(a, wt, out_dtype):
    a16 = a.astype(BF16)
    return _mm(a16, wt, tb=True, name="lin_t_fwd", out_dtype=out_dtype), (a16, wt, jnp.zeros((0,), a.dtype))


def _lin_t_bwd(out_dtype, res, dy):
    a16, wt, like = res
    dy16 = dy.astype(BF16)
    da = _mm(dy16, wt, name="lin_t_da", out_dtype=like.dtype)
    dwt = _mm(dy16, a16, ta=True, name="lin_t_dw", out_dtype=wt.dtype)
    return da, dwt


lin_t.defvjp(_lin_t_fwd, _lin_t_bwd)


@jax.custom_vjp
def lin(a, w):
    return _mm(a, w, name="lin_fwd")


def _lin_fwd(a, w):
    a16 = a.astype(BF16)
    return _mm(a16, w, name="lin_fwd"), (a16, w, jnp.zeros((0,), a.dtype))


def _lin_bwd(res, dy):
    a16, w, like = res
    dy16 = dy.astype(BF16)
    da = _mm(dy16, w, tb=True, name="lin_da", out_dtype=like.dtype)
    dw = _mm(a16, dy16, ta=True, name="lin_dw", out_dtype=w.dtype)
    return da, dw


lin.defvjp(_lin_fwd, _lin_bwd)


def _me():
    return 4 * lax.axis_index("x") + 2 * lax.axis_index("y") + lax.axis_index("c")


def _peer(j):
    x, y, c = lax.axis_index("x"), lax.axis_index("y"), lax.axis_index("c")
    px = 1 - x if j & 4 else x
    py = 1 - y if j & 2 else y
    pc = 1 - c if j & 1 else c
    return (px, py, pc), 4 * px + 2 * py + pc


def _exchange_body(src_of, dst_ref, send_sems, recv_sems, local_sem):
    me = _me()
    mine = pltpu.make_async_copy(src_of(me), dst_ref.at[me], local_sem)
    mine.start()
    sends = []
    for j in range(1, N_DEV):
        peer, pidx = _peer(j)
        cp = pltpu.make_async_remote_copy(src_ref=src_of(pidx), dst_ref=dst_ref.at[me], send_sem=send_sems.at[j - 1],
                                          recv_sem=recv_sems.at[j - 1], device_id=peer, device_id_type=MESH)
        cp.start()
        sends.append(cp)
    for j in range(1, N_DEV):
        peer, pidx = _peer(j)
        pltpu.make_async_remote_copy(src_ref=src_of(pidx), dst_ref=dst_ref.at[pidx], send_sem=send_sems.at[j - 1],
                                     recv_sem=recv_sems.at[j - 1], device_id=peer, device_id_type=MESH).wait_recv()
    for cp in sends:
        cp.wait_send()
    mine.wait()


_EXCHANGE_SCRATCH = [pltpu.SemaphoreType.DMA((N_DEV - 1,)), pltpu.SemaphoreType.DMA((N_DEV - 1,)),
                     pltpu.SemaphoreType.DMA]


def all_gather_hbm(shard, name):
    def body(x_ref, out_ref, send_sems, recv_sems, local_sem):
        _exchange_body(lambda k: x_ref, out_ref, send_sems, recv_sems, local_sem)

    return pl.pallas_call(
        body, name=name,
        out_shape=jax.ShapeDtypeStruct((N_DEV,) + shard.shape, shard.dtype),
        in_specs=[pl.BlockSpec(memory_space=pl.ANY)],
        out_specs=pl.BlockSpec(memory_space=pl.ANY),
        scratch_shapes=list(_EXCHANGE_SCRATCH),
    )(shard)


def all_to_all_hbm(slots, name):
    def body(x_ref, out_ref, send_sems, recv_sems, local_sem):
        _exchange_body(lambda k: x_ref.at[k], out_ref, send_sems, recv_sems, local_sem)

    return pl.pallas_call(
        body, name=name,
        out_shape=jax.ShapeDtypeStruct(slots.shape, slots.dtype),
        in_specs=[pl.BlockSpec(memory_space=pl.ANY)],
        out_specs=pl.BlockSpec(memory_space=pl.ANY),
        scratch_shapes=list(_EXCHANGE_SCRATCH),
    )(slots)


def all_gather_sum_small(v, name):
    def body(x_ref, out_ref, sum_ref, send_sems, recv_sems, local_sem):
        _exchange_body(lambda k: x_ref, out_ref, send_sems, recv_sems, local_sem)
        acc = out_ref[0]
        for k in range(1, N_DEV):
            acc = acc + out_ref[k]
        sum_ref[...] = acc

    return pl.pallas_call(
        body, name=name,
        out_shape=(jax.ShapeDtypeStruct((N_DEV,) + v.shape, v.dtype), jax.ShapeDtypeStruct(v.shape, v.dtype)),
        in_specs=[pl.BlockSpec(memory_space=pltpu.VMEM)],
        out_specs=(pl.BlockSpec(memory_space=pltpu.VMEM), pl.BlockSpec(memory_space=pltpu.VMEM)),
        scratch_shapes=list(_EXCHANGE_SCRATCH),
    )(v)


def sum_slots(slots, name):
    _, p_dim, c_dim = slots.shape
    tp = _pick(p_dim, (256, 128, 64, 32, 16))

    def body(x_ref, o_ref):
        acc = x_ref[0].astype(F32)
        for k in range(1, N_DEV):
            acc = acc + x_ref[k].astype(F32)
        o_ref[...] = acc

    return pl.pallas_call(
        body, name=name,
        out_shape=jax.ShapeDtypeStruct((p_dim, c_dim), F32),
        grid=(p_dim // tp,),
        in_specs=[pl.BlockSpec((N_DEV, tp, c_dim), lambda i: (0, i, 0))],
        out_specs=pl.BlockSpec((tp, c_dim), lambda i: (i, 0)),
        compiler_params=pltpu.CompilerParams(dimension_semantics=("parallel",)),
    )(slots)


def adamw(w, g, m, v, name):
    rows, cols = w.shape
    tr = _pick(rows, (256, 128, 64, 32, 16, 8))

    def body(w_ref, g_ref, m_ref, v_ref, d_ref, nm_ref, nv_ref):
        gv = g_ref[...]
        nm = ADAM_B1 * m_ref[...] + (1.0 - ADAM_B1) * gv
        nv = ADAM_B2 * v_ref[...] + (1.0 - ADAM_B2) * (gv * gv)
        m_hat = nm / (1.0 - ADAM_B1 ** ADAM_STEP)
        v_hat = nv / (1.0 - ADAM_B2 ** ADAM_STEP)
        d_ref[...] = -ADAM_LR * (m_hat / (jnp.sqrt(v_hat) + ADAM_EPS) + ADAM_WD * w_ref[...])
        nm_ref[...] = nm
        nv_ref[...] = nv

    spec = pl.BlockSpec((tr, cols), lambda i: (i, 0))
    shp = jax.ShapeDtypeStruct((rows, cols), F32)
    return pl.pallas_call(
        body, name=name, out_shape=(shp, shp, shp), grid=(rows // tr,),
        in_specs=[spec] * 4, out_specs=(spec,) * 3,
        compiler_params=pltpu.CompilerParams(dimension_semantics=("parallel",)),
    )(w, g, m, v)


ATT_BLK = 128
NEG = -1e30


def _head_sums(v):
    li = lax.broadcasted_iota(jnp.int32, (LANES, LANES), 0) // ATT_HEAD_DIM
    lj = lax.broadcasted_iota(jnp.int32, (LANES, LANES), 1) // ATT_HEAD_DIM
    ones = (li == lj).astype(F32)
    return jnp.dot(v, ones, precision=lax.Precision.HIGHEST, preferred_element_type=F32)


def _head_col(v, hmask):
    return jnp.max(jnp.where(hmask, v, -jnp.inf), axis=-1, keepdims=True)


def _qk_norm(raw, gain2):
    rstd = lax.rsqrt(_head_sums(raw * raw) * (1.0 / ATT_HEAD_DIM) + NORM_EPS)
    xhat = raw * rstd
    return xhat * gain2, xhat, rstd


def _qk_norm_bwd(dn, xhat, rstd, gain2):
    dxh = dn * gain2
    return rstd * (dxh - xhat * (_head_sums(dxh * xhat) * (1.0 / ATT_HEAD_DIM))), dn * xhat


def _att_mask_bias(n, dilation):
    qi = lax.broadcasted_iota(jnp.int32, (ATT_BLK, 2 * ATT_BLK), 0)
    ki = lax.broadcasted_iota(jnp.int32, (ATT_BLK, 2 * ATT_BLK), 1)
    dist = qi + ATT_BLK - ki
    valid = (dist >= 0) & (dist <= ATT_BLK) & ((n > 0) | (ki >= ATT_BLK))
    return valid, (dilation * dist).astype(F32)


def _att_specs(g, dilation, hd, qkv_dim, nb):
    cpb = qkv_dim // LANES
    base = g * 3 * hd // LANES

    def spec(which, shift):
        def imap(r, hp, n):
            row = jnp.minimum(n, nb - 1) if shift == 0 else jnp.maximum(n - 1, 0)
            return (row, r * cpb + base + which * (hd // LANES) + hp)
        return pl.BlockSpec((ATT_BLK, LANES), imap)

    return [spec(0, 0), spec(1, 1), spec(1, 0), spec(2, 1), spec(2, 0)]


def _att_group_fwd(qkv, gq2, gk2, slopes, g, dilation):
    t, qkv_dim = qkv.shape
    hd = qkv_dim // (3 * len(DIL_PATTERNS))
    lu = t // dilation
    nb = lu // ATT_BLK
    assert nb * ATT_BLK == lu and hd % LANES == 0
    hpn = hd // LANES
    scale = 1.0 / math.sqrt(ATT_HEAD_DIM)

    def body(q_ref, kp_ref, kc_ref, vp_ref, vc_ref, gq_ref, gk_ref, sl_ref, o_ref, l_ref):
        n = pl.program_id(2)
        lane = lax.broadcasted_iota(jnp.int32, (1, LANES), 1)
        qn, _, _ = _qk_norm(q_ref[...].astype(F32), gq_ref[0:1, :])
        kn, _, _ = _qk_norm(jnp.concatenate([kp_ref[...], kc_ref[...]], axis=0).astype(F32), gk_ref[0:1, :])
        kn16 = kn.astype(BF16)
        v16 = jnp.concatenate([vp_ref[...], vc_ref[...]], axis=0)
        valid, dist = _att_mask_bias(n, dilation)
        outs, lses = [], []
        for hh in range(2):
            hmask = (lane // ATT_HEAD_DIM) == hh
            qh = jnp.where(hmask, qn, 0.0).astype(BF16)
            s = lax.dot_general(qh, kn16, (((1,), (1,)), ((), ())), preferred_element_type=F32) * scale
            slope = _head_col(sl_ref[0:1, :], hmask)
            logits = jnp.where(valid, s - slope * dist, NEG)
            mx = jnp.max(logits, axis=-1, keepdims=True)
            pexp = jnp.exp(logits - mx)
            den = jnp.sum(pexp, axis=-1, keepdims=True)
            outs.append(jnp.dot(pexp.astype(BF16), v16, preferred_element_type=F32) / den)
            lses.append(mx + jnp.log(den))
        first = (lane // ATT_HEAD_DIM) == 0
        o_ref[...] = jnp.where(first, outs[0], outs[1])
        l_ref[...] = jnp.where(first, lses[0], lses[1])

    out_spec = pl.BlockSpec((ATT_BLK, LANES), lambda r, hp, n: (n, r * hpn + hp))
    vec_spec = pl.BlockSpec((SUBLANES, LANES), lambda r, hp, n: (0, 0))
    shp = jax.ShapeDtypeStruct((lu, dilation * hd), F32)
    o, lse = pl.pallas_call(
        body, name=f"att_fwd_g{g}", out_shape=(shp, shp), grid=(dilation, hpn, nb),
        in_specs=_att_specs(g, dilation, hd, qkv_dim, nb) + [vec_spec, vec_spec,
                                                             pl.BlockSpec((None, SUBLANES, LANES), lambda r, hp, n: (hp, 0, 0))],
        out_specs=(out_spec, out_spec),
        compiler_params=pltpu.CompilerParams(dimension_semantics=("parallel", "parallel", "arbitrary")),
    )(*([qkv.reshape(lu, dilation * qkv_dim)] * 5), gq2, gk2, slopes)
    return o.reshape(t, hd), lse.reshape(t, hd)


def _att_merge(outs, lses):
    t, hd = outs[0].shape
    tr = _pick(t, (256, 128))
    ng = len(outs)

    def body(*refs):
        o_refs, l_refs, o_ref, lt_ref = refs[:ng], refs[ng:2 * ng], refs[2 * ng], refs[2 * ng + 1]
        ls = [r[...] for r in l_refs]
        mx = functools.reduce(jnp.maximum, ls)
        es = [jnp.exp(l - mx) for l in ls]
        den = functools.reduce(jnp.add, es)
        acc = es[0] * o_refs[0][...]
        for e, r in zip(es[1:], o_refs[1:]):
            acc = acc + e * r[...]
        o_ref[...] = acc / den
        lt_ref[...] = mx + jnp.log(den)

    spec = pl.BlockSpec((tr, hd), lambda i: (i, 0))
    shp = jax.ShapeDtypeStruct((t, hd), F32)
    return pl.pallas_call(
        body, name="att_merge", out_shape=(shp, shp), grid=(t // tr,),
        in_specs=[spec] * (2 * ng), out_specs=(spec, spec),
        compiler_params=pltpu.CompilerParams(dimension_semantics=("parallel",)),
    )(*outs, *lses)


def _att_group_bwd(qkv, gq2, gk2, slopes, o, lse_tot, do, g, dilation):
    t, qkv_dim = qkv.shape
    hd = qkv_dim // (3 * len(DIL_PATTERNS))
    lu = t // dilation
    nb = lu // ATT_BLK
    hpn = hd // LANES
    scale = 1.0 / math.sqrt(ATT_HEAD_DIM)

    def body(q_ref, kp_ref, kc_ref, vp_ref, vc_ref, gq_ref, gk_ref, sl_ref, o_ref, l_ref, do_ref,
             dq_ref, dk_ref, dv_ref, dgq_ref, dgk_ref, ck_ref, cv_ref):
        n = pl.program_id(2)
        lane = lax.broadcasted_iota(jnp.int32, (1, LANES), 1)
        gq, gk = gq_ref[0:1, :], gk_ref[0:1, :]

        @pl.when(n == 0)
        def _():
            ck_ref[...] = jnp.zeros_like(ck_ref)
            cv_ref[...] = jnp.zeros_like(cv_ref)
            dgq_ref[...] = jnp.zeros_like(dgq_ref)
            dgk_ref[...] = jnp.zeros_like(dgk_ref)

        qn, qhat, qr = _qk_norm(q_ref[...].astype(F32), gq)
        kn, khat, kr = _qk_norm(jnp.concatenate([kp_ref[...], kc_ref[...]], axis=0).astype(F32), gk)
        kn16 = kn.astype(BF16)
        v16 = jnp.concatenate([vp_ref[...], vc_ref[...]], axis=0)
        dov = do_ref[...]
        do_o = dov * o_ref[...]
        lse = l_ref[...]
        valid, dist = _att_mask_bias(n, dilation)
        valid = valid & (n < nb)
        dq_acc = jnp.zeros((ATT_BLK, LANES), F32)
        dk_acc = jnp.zeros((2 * ATT_BLK, LANES), F32)
        dv_acc = jnp.zeros((2 * ATT_BLK, LANES), F32)
        for hh in range(2):
            hmask = (lane // ATT_HEAD_DIM) == hh
            qh = jnp.where(hmask, qn, 0.0).astype(BF16)
            doh = jnp.where(hmask, dov, 0.0).astype(BF16)
            s = lax.dot_general(qh, kn16, (((1,), (1,)), ((), ())), preferred_element_type=F32) * scale
            slope = _head_col(sl_ref[0:1, :], hmask)
            pr = jnp.exp(jnp.where(valid, s - slope * dist - _head_col(lse, hmask), NEG))
            dp = lax.dot_general(doh, v16, (((1,), (1,)), ((), ())), preferred_element_type=F32)
            delta = jnp.sum(jnp.where(hmask, do_o, 0.0), axis=-1, keepdims=True)
            ds = (pr * (dp - delta) * scale).astype(BF16)
            dq_acc = dq_acc + jnp.where(hmask, jnp.dot(ds, kn16, preferred_element_type=F32), 0.0)
            dk_acc = dk_acc + lax.dot_general(ds, qh, (((0,), (0,)), ((), ())), preferred_element_type=F32)
            dv_acc = dv_acc + lax.dot_general(pr.astype(BF16), doh, (((0,), (0,)), ((), ())),
                                              preferred_element_type=F32)

        dq_raw, gq_part = _qk_norm_bwd(dq_acc, qhat, qr, gq)

        @pl.when(n < nb)
        def _():
            dq_ref[...] = dq_raw.astype(dq_ref.dtype)
            dgq_ref[...] += jnp.broadcast_to(jnp.sum(gq_part, axis=0, keepdims=True), dgq_ref.shape)

        dk_prev = ck_ref[...] + dk_acc[:ATT_BLK]
        dk_raw, gk_part = _qk_norm_bwd(dk_prev, khat[:ATT_BLK], kr[:ATT_BLK], gk)
        dk_ref[...] = dk_raw.astype(dk_ref.dtype)
        dv_ref[...] = (cv_ref[...] + dv_acc[:ATT_BLK]).astype(dv_ref.dtype)

        @pl.when(n > 0)
        def _():
            dgk_ref[...] += jnp.broadcast_to(jnp.sum(gk_part, axis=0, keepdims=True), dgk_ref.shape)

        ck_ref[...] = dk_acc[ATT_BLK:]
        cv_ref[...] = dv_acc[ATT_BLK:]

    q_out = pl.BlockSpec((ATT_BLK, LANES), lambda r, hp, n: (jnp.minimum(n, nb - 1), r * hpn + hp))
    kv_out = pl.BlockSpec((ATT_BLK, LANES), lambda r, hp, n: (jnp.maximum(n - 1, 0), r * hpn + hp))
    vec_spec = pl.BlockSpec((SUBLANES, LANES), lambda r, hp, n: (0, 0))
    g_out = pl.BlockSpec((None, SUBLANES, LANES), lambda r, hp, n: (r * hpn + hp, 0, 0))
    shp = jax.ShapeDtypeStruct((lu, dilation * hd), BF16)
    gshp = jax.ShapeDtypeStruct((dilation * hpn, SUBLANES, LANES), F32)
    view = lambda a: a.reshape(lu, dilation * hd)
    dq, dk, dv, dgq, dgk = pl.pallas_call(
        body, name=f"att_bwd_g{g}", out_shape=(shp, shp, shp, gshp, gshp), grid=(dilation, hpn, nb + 1),
        in_specs=_att_specs(g, dilation, hd, qkv_dim, nb) + [
            vec_spec, vec_spec, pl.BlockSpec((None, SUBLANES, LANES), lambda r, hp, n: (hp, 0, 0)), q_out, q_out, q_out],
        out_specs=(q_out, kv_out, kv_out, g_out, g_out),
        scratch_shapes=[pltpu.VMEM((ATT_BLK, LANES), F32), pltpu.VMEM((ATT_BLK, LANES), F32)],
        compiler_params=pltpu.CompilerParams(dimension_semantics=("parallel", "parallel", "arbitrary")),
    )(*([qkv.reshape(lu, dilation * qkv_dim)] * 5), gq2, gk2, slopes, view(o), view(lse_tot), view(do))
    return dq.reshape(t, hd), dk.reshape(t, hd), dv.reshape(t, hd), dgq, dgk


def _att_consts(q_gain, k_gain, hd):
    heads = hd // ATT_HEAD_DIM
    gq2 = jnp.broadcast_to(jnp.tile(q_gain, 2)[None], (SUBLANES, LANES))
    gk2 = jnp.broadcast_to(jnp.tile(k_gain, 2)[None], (SUBLANES, LANES))
    sl = 2.0 ** (-8.0 * jnp.arange(1, heads + 1, dtype=F32) / heads)
    slopes = jnp.broadcast_to(jnp.repeat(sl, ATT_HEAD_DIM).reshape(hd // LANES, 1, LANES), (hd // LANES, SUBLANES, LANES))
    return gq2, gk2, slopes


@jax.custom_vjp
def attention_core(qkv, q_gain, k_gain):
    return _attention_core_fwd(qkv, q_gain, k_gain)[0]


def _attention_core_fwd(qkv, q_gain, k_gain):
    hd = qkv.shape[1] // (3 * len(DIL_PATTERNS))
    gq2, gk2, slopes = _att_consts(q_gain, k_gain, hd)
    outs, lses = [], []
    for g, (_, dilation) in enumerate(DIL_PATTERNS):
        o_g, l_g = _att_group_fwd(qkv, gq2, gk2, slopes, g, dilation)
        outs.append(o_g)
        lses.append(l_g)
    o, lse_tot = _att_merge(outs, lses)
    return o, (qkv, q_gain, k_gain, o, lse_tot)


def _attention_core_bwd(res, do):
    qkv, q_gain, k_gain, o, lse_tot = res
    hd = o.shape[1]
    gq2, gk2, slopes = _att_consts(q_gain, k_gain, hd)
    parts, dgq, dgk = [], 0.0, 0.0
    for g, (_, dilation) in enumerate(DIL_PATTERNS):
        dq, dk, dv, a, b = _att_group_bwd(qkv, gq2, gk2, slopes, o, lse_tot, do, g, dilation)
        parts += [dq, dk, dv]
        dgq = dgq + a[:, 0, :].reshape(-1, ATT_HEAD_DIM).sum(0)
        dgk = dgk + b[:, 0, :].reshape(-1, ATT_HEAD_DIM).sum(0)
    return jnp.concatenate(parts, axis=1), dgq, dgk


attention_core.defvjp(_attention_core_fwd, _attention_core_bwd)


HALO = 8


def _silu(v):
    return v * jax.nn.sigmoid(v)


def _silu_grad(v):
    s = jax.nn.sigmoid(v)
    return s * (1.0 + v * (1.0 - s))


def _conv_fwd(zx, conv_w, conv_b, d_inner):
    t = zx.shape[0]
    conv_dim = conv_w.shape[1]
    cb = _pick(d_inner, (1024, 512, 256, 128))
    assert conv_dim % cb == 0
    tr = _pick(t, (256, 128))
    off = d_inner // cb

    def body(x_ref, h_ref, w_ref, b_ref, o_ref):
        i = pl.program_id(1)
        halo = jnp.where(i > 0, h_ref[...], 0.0)
        ext = jnp.concatenate([halo, x_ref[...]], axis=0)
        acc = jnp.broadcast_to(b_ref[...], (tr, cb))
        for k in range(CONV_WIDTH):
            s = CONV_WIDTH - 1 - k
            sh = ext if s == 0 else pltpu.roll(ext, shift=s, axis=0)
            acc = acc + w_ref[k:k + 1, :] * sh[HALO:HALO + tr]
        o_ref[...] = acc

    return pl.pallas_call(
        body, name="ssm_conv_fwd", out_shape=jax.ShapeDtypeStruct((t, conv_dim), F32),
        grid=(conv_dim // cb, t // tr),
        in_specs=[pl.BlockSpec((tr, cb), lambda j, i: (i, off + j)),
                  pl.BlockSpec((HALO, cb), lambda j, i: (jnp.maximum(i * (tr // HALO) - 1, 0), off + j)),
                  pl.BlockSpec((CONV_WIDTH, cb), lambda j, i: (0, j)),
                  pl.BlockSpec((1, cb), lambda j, i: (0, j))],
        out_specs=pl.BlockSpec((tr, cb), lambda j, i: (i, j)),
        compiler_params=pltpu.CompilerParams(dimension_semantics=("parallel", "parallel")),
    )(zx, zx, conv_w, conv_b.reshape(1, -1))


def _conv_bwd(zx, conv_w, dpre, dzx, d_inner):
    t, width = zx.shape
    conv_dim = conv_w.shape[1]
    cb = _pick(d_inner, (1024, 512, 256, 128))
    tr = _pick(t, (256, 128))
    off = d_inner // cb
    nr = t // tr

    def body(x_ref, h_ref, w_ref, d_ref, dn_ref, dzx_in, dx_ref, dw_ref, db_ref):
        i = pl.program_id(1)

        @pl.when(i == 0)
        def _():
            dw_ref[...] = jnp.zeros_like(dw_ref)
            db_ref[...] = jnp.zeros_like(db_ref)

        halo = jnp.where(i > 0, h_ref[...], 0.0)
        ext = jnp.concatenate([halo, x_ref[...]], axis=0)
        d = d_ref[...]
        dext = jnp.concatenate([d, jnp.where(i < nr - 1, dn_ref[...], 0.0)], axis=0)
        dx = jnp.zeros((tr, cb), F32)
        for k in range(CONV_WIDTH):
            s = CONV_WIDTH - 1 - k
            fut = dext if s == 0 else pltpu.roll(dext, shift=tr + HALO - s, axis=0)
            dx = dx + w_ref[k:k + 1, :] * fut[:tr]
            past = ext if s == 0 else pltpu.roll(ext, shift=s, axis=0)
            dw_ref[k:k + 1, :] += jnp.sum(d * past[HALO:HALO + tr], axis=0, keepdims=True)
        dx_ref[...] = dx.astype(dx_ref.dtype)
        db_ref[...] += jnp.sum(d, axis=0, keepdims=True)

    last_halo = t // HALO - 1
    return pl.pallas_call(
        body, name="ssm_conv_bwd",
        out_shape=(jax.ShapeDtypeStruct(dzx.shape, dzx.dtype), jax.ShapeDtypeStruct((CONV_WIDTH, conv_dim), F32),
                   jax.ShapeDtypeStruct((1, conv_dim), F32)),
        grid=(conv_dim // cb, nr),
        in_specs=[pl.BlockSpec((tr, cb), lambda j, i: (i, off + j)),
                  pl.BlockSpec((HALO, cb), lambda j, i: (jnp.maximum(i * (tr // HALO) - 1, 0), off + j)),
                  pl.BlockSpec((CONV_WIDTH, cb), lambda j, i: (0, j)),
                  pl.BlockSpec((tr, cb), lambda j, i: (i, j)),
                  pl.BlockSpec((HALO, cb), lambda j, i: (jnp.minimum((i + 1) * (tr // HALO), last_halo), j)),
                  pl.BlockSpec(memory_space=pl.ANY)],
        out_specs=(pl.BlockSpec((tr, cb), lambda j, i: (i, off + j)),
                   pl.BlockSpec((CONV_WIDTH, cb), lambda j, i: (0, j)),
                   pl.BlockSpec((1, cb), lambda j, i: (0, j))),
        input_output_aliases={5: 0},
        compiler_params=pltpu.CompilerParams(dimension_semantics=("parallel", "arbitrary")),
    )(zx, zx, conv_w, dpre, dpre, dzx)


def _eye(n):
    return lax.broadcasted_iota(jnp.int32, (n, n), 0) == lax.broadcasted_iota(jnp.int32, (n, n), 1)


def _row_to_col(row):
    n = row.shape[1]
    return jnp.sum(jnp.where(_eye(n), row, 0.0), axis=1, keepdims=True)


def _col_to_row(col):
    n = col.shape[0]
    return jnp.sum(jnp.where(_eye(n), col, 0.0), axis=0, keepdims=True)


def _pair_lanes(c0, c1):
    lane = lax.broadcasted_iota(jnp.int32, (1, LANES), 1)
    return jnp.where(lane < SSM_HEAD_DIM, c0, c1)


def _ssd_chunk_common(pre_x_ref, pre_b_ref, pre_c_ref, dtr_ref, bias_ref, alog_ref, cs_ref):
    cl = SSD_CHUNK
    hpg = dtr_ref.shape[0]
    x = _silu(pre_x_ref[...])
    b16 = _silu(pre_b_ref[...]).astype(BF16)
    c16 = _silu(pre_c_ref[...]).astype(BF16)
    dt = jax.nn.softplus(dtr_ref[...] + bias_ref[...])
    a = -jnp.exp(alog_ref[...])
    li = lax.broadcasted_iota(jnp.int32, (cl, cl), 0)
    si = lax.broadcasted_iota(jnp.int32, (cl, cl), 1)
    upper = (li <= si).astype(F32)
    cs_ref[0:hpg, :] = jnp.dot(dt * a, upper, precision=lax.Precision.HIGHEST, preferred_element_type=F32)
    cs_ref[hpg:2 * hpg, :] = dt
    g = lax.dot_general(c16, b16, (((1,), (1,)), ((), ())), preferred_element_type=F32)
    return x, b16, c16, dt, a, g, li >= si


def _ssd_fwd(pre, dtT, bias, alog, dskip_lanes, d_inner):
    t = pre.shape[0]
    cl = SSD_CHUNK
    nc = t // cl
    ng = SSM_GROUPS
    hpg = dtT.shape[1]
    gw = hpg * SSM_HEAD_DIM
    assert d_inner == ng * gw and hpg % 2 == 0
    bo = d_inner // SSM_STATE

    def body(px_ref, pb_ref, pc_ref, dtr_ref, bias_ref, alog_ref, dsk_ref, y_ref, st_ref, s_ref, cs_ref):
        c = pl.program_id(1)

        @pl.when(c == 0)
        def _():
            s_ref[...] = jnp.zeros_like(s_ref)

        x, b16, c16, dt, a, g, causal = _ssd_chunk_common(px_ref, pb_ref, pc_ref, dtr_ref, bias_ref, alog_ref, cs_ref)
        st_ref[...] = s_ref[...]
        yoff = lax.dot_general(c16, s_ref[...].astype(BF16), (((1,), (1,)), ((), ())), preferred_element_type=F32)
        xde_parts = []
        for j in range(hpg // 2):
            cols = slice(j * LANES, (j + 1) * LANES)
            xp = x[:, cols]
            dcol, ecol, ocol, ms = [], [], [], []
            for hh in range(2):
                h = 2 * j + hh
                cs_row = cs_ref[h:h + 1, :]
                cs_col = _row_to_col(cs_row)
                dcol.append(_row_to_col(cs_ref[hpg + h:hpg + h + 1, :]))
                ecol.append(jnp.exp(cs_ref[h:h + 1, cl - 1:cl] - cs_col))
                ocol.append(jnp.exp(cs_col))
                lm = jnp.where(causal, jnp.exp(jnp.minimum(cs_col - cs_row, 0.0)), 0.0)
                ms.append((g * lm).astype(BF16))
            xd = xp * _pair_lanes(dcol[0], dcol[1])
            xd16 = xd.astype(BF16)
            yd = _pair_lanes(1.0, 0.0) * jnp.dot(ms[0], xd16, preferred_element_type=F32) \
                + _pair_lanes(0.0, 1.0) * jnp.dot(ms[1], xd16, preferred_element_type=F32)
            y_ref[:, cols] = yd + yoff[:, cols] * _pair_lanes(ocol[0], ocol[1]) + xp * dsk_ref[0:1, cols]
            xde_parts.append((xd * _pair_lanes(ecol[0], ecol[1])).astype(BF16))
        new = lax.dot_general(jnp.concatenate(xde_parts, axis=1), b16, (((0,), (0,)), ((), ())),
                              preferred_element_type=F32)
        for h in range(hpg):
            rows = slice(h * SSM_HEAD_DIM, (h + 1) * SSM_HEAD_DIM)
            s_ref[rows, :] = s_ref[rows, :] * jnp.exp(cs_ref[h:h + 1, cl - 1:cl]) + new[rows, :]

    vec = lambda n: pl.BlockSpec((None, hpg, n), lambda gi, c: (gi, 0, 0))
    return pl.pallas_call(
        body, name="ssd_fwd",
        out_shape=(jax.ShapeDtypeStruct((t, d_inner), F32), jax.ShapeDtypeStruct((ng, nc, gw, SSM_STATE), F32)),
        grid=(ng, nc),
        in_specs=[pl.BlockSpec((cl, gw), lambda gi, c: (c, gi)),
                  pl.BlockSpec((cl, SSM_STATE), lambda gi, c: (c, bo + gi)),
                  pl.BlockSpec((cl, SSM_STATE), lambda gi, c: (c, bo + ng + gi)),
                  pl.BlockSpec((None, hpg, cl), lambda gi, c: (gi, 0, c)),
                  vec(1), vec(1),
                  pl.BlockSpec((1, gw), lambda gi, c: (0, gi))],
        out_specs=(pl.BlockSpec((cl, gw), lambda gi, c: (c, gi)),
                   pl.BlockSpec((None, None, gw, SSM_STATE), lambda gi, c: (gi, c, 0, 0))),
        scratch_shapes=[pltpu.VMEM((gw, SSM_STATE), F32), pltpu.VMEM((2 * hpg, cl), F32)],
        compiler_params=pltpu.CompilerParams(dimension_semantics=("parallel", "arbitrary")),
    )(pre, pre, pre, dtT, bias, alog, dskip_lanes)


def _ssd_bwd(pre, dtT, bias, alog, dskip_lanes, states, dy, d_inner):
    t, conv_dim = pre.shape
    cl = SSD_CHUNK
    nc = t // cl
    ng = SSM_GROUPS
    hpg = dtT.shape[1]
    gw = hpg * SSM_HEAD_DIM
    bo = d_inner // SSM_STATE

    def body(px_ref, pb_ref, pc_ref, dtr_ref, bias_ref, alog_ref, dsk_ref, st_ref, dy_ref,
             dx_ref, db_ref, dc_ref, ddt_ref, acc_ref, dsk_out, ds_ref, cs_ref, dcs_ref):
        c = pl.program_id(1)

        @pl.when(c == 0)
        def _():
            ds_ref[...] = jnp.zeros_like(ds_ref)
            acc_ref[...] = jnp.zeros_like(acc_ref)
            dsk_out[...] = jnp.zeros_like(dsk_out)

        x, b16, c16, dt, a, g, causal = _ssd_chunk_common(px_ref, pb_ref, pc_ref, dtr_ref, bias_ref, alog_ref, cs_ref)
        s_prev = st_ref[...]
        s16 = s_prev.astype(BF16)
        ds = ds_ref[...]
        ds16 = ds.astype(BF16)
        dyv = dy_ref[...]
        yoff = lax.dot_general(c16, s16, (((1,), (1,)), ((), ())), preferred_element_type=F32)
        bds = lax.dot_general(b16, ds16, (((1,), (1,)), ((), ())), preferred_element_type=F32)
        dg = jnp.zeros((cl, cl), F32)
        xde_parts, dye_parts = [], []
        lane = lax.broadcasted_iota(jnp.int32, (1, LANES), 1)
        for j in range(hpg // 2):
            cols = slice(j * LANES, (j + 1) * LANES)
            xp, dyp = x[:, cols], dyv[:, cols]
            dcol, ecol, ocol, lms = [], [], [], []
            for hh in range(2):
                h = 2 * j + hh
                cs_row = cs_ref[h:h + 1, :]
                cs_col = _row_to_col(cs_row)
                dcol.append(_row_to_col(cs_ref[hpg + h:hpg + h + 1, :]))
                ecol.append(jnp.exp(cs_ref[h:h + 1, cl - 1:cl] - cs_col))
                ocol.append(jnp.exp(cs_col))
                lms.append(jnp.where(causal, jnp.exp(jnp.minimum(cs_col - cs_row, 0.0)), 0.0))
            dlanes, elanes, olanes = _pair_lanes(*dcol), _pair_lanes(*ecol), _pair_lanes(*ocol)
            xd = xp * dlanes
            xd16 = xd.astype(BF16)
            xde = xd * elanes
            yoffp = yoff[:, cols] * olanes
            bdsp = bds[:, cols]
            dxd = bdsp * elanes
            for hh in range(2):
                h = 2 * j + hh
                hmask = (lane // SSM_HEAD_DIM) == hh
                dyh16 = jnp.where(hmask, dyp, 0.0).astype(BF16)
                m = g * lms[hh]
                dm = lax.dot_general(dyh16, xd16, (((1,), (1,)), ((), ())), preferred_element_type=F32)
                w = dm * m
                dg = dg + dm * lms[hh]
                dxd = dxd + lax.dot_general(m.astype(BF16), dyh16, (((0,), (0,)), ((), ())),
                                            preferred_element_type=F32)
                term = jnp.sum(jnp.where(hmask, xde * bdsp, 0.0), axis=1, keepdims=True)
                dcs_col = (jnp.sum(w, axis=1, keepdims=True)
                           + jnp.sum(jnp.where(hmask, dyp * yoffp, 0.0), axis=1, keepdims=True) - term)
                rows = slice(h * SSM_HEAD_DIM, (h + 1) * SSM_HEAD_DIM)
                dec = jnp.exp(cs_ref[h:h + 1, cl - 1:cl])
                tail = jnp.sum(term, axis=0, keepdims=True) + dec * jnp.sum(
                    jnp.sum(s_prev[rows, :] * ds[rows, :], axis=1, keepdims=True), axis=0, keepdims=True)
                last = lax.broadcasted_iota(jnp.int32, (1, cl), 1) == cl - 1
                dcs_ref[h:h + 1, :] = _col_to_row(dcs_col) - jnp.sum(w, axis=0, keepdims=True) + jnp.where(last, tail, 0.0)
                dcs_ref[hpg + h:hpg + h + 1, :] = _col_to_row(
                    jnp.sum(jnp.where(hmask, dxd * xp, 0.0), axis=1, keepdims=True))
            dx_act = dxd * dlanes + dyp * dsk_ref[0:1, cols]
            dx_ref[:, cols] = dx_act * _silu_grad(px_ref[:, cols])
            dsk_out[0:1, cols] += jnp.sum(dyp * xp, axis=0, keepdims=True)
            xde_parts.append(xde.astype(BF16))
            dye_parts.append((dyp * olanes).astype(BF16))
        xde16 = jnp.concatenate(xde_parts, axis=1)
        dye16 = jnp.concatenate(dye_parts, axis=1)
        dg16 = dg.astype(BF16)
        dc_act = jnp.dot(dg16, b16, preferred_element_type=F32) + jnp.dot(dye16, s16, preferred_element_type=F32)
        db_act = lax.dot_general(dg16, c16, (((0,), (0,)), ((), ())), preferred_element_type=F32) \
            + jnp.dot(xde16, ds16, preferred_element_type=F32)
        dc_ref[...] = dc_act * _silu_grad(pc_ref[...])
        db_ref[...] = db_act * _silu_grad(pb_ref[...])
        ds_new = lax.dot_general(dye16, c16, (((0,), (0,)), ((), ())), preferred_element_type=F32)
        for h in range(hpg):
            rows = slice(h * SSM_HEAD_DIM, (h + 1) * SSM_HEAD_DIM)
            ds_ref[rows, :] = ds[rows, :] * jnp.exp(cs_ref[h:h + 1, cl - 1:cl]) + ds_new[rows, :]
        li = lax.broadcasted_iota(jnp.int32, (cl, cl), 0)
        si = lax.broadcasted_iota(jnp.int32, (cl, cl), 1)
        d_adt = jnp.dot(dcs_ref[0:hpg, :], (li >= si).astype(F32), precision=lax.Precision.HIGHEST,
                        preferred_element_type=F32)
        ddt = d_adt * a + dcs_ref[hpg:2 * hpg, :]
        ddt_raw = ddt * jax.nn.sigmoid(dtr_ref[...] + bias_ref[...])
        ddt_ref[...] = ddt_raw
        acc_ref[0:hpg, :] += d_adt * dt
        acc_ref[hpg:2 * hpg, :] += ddt_raw

    rc = lambda c: nc - 1 - c
    vec = lambda n: pl.BlockSpec((None, hpg, n), lambda gi, c: (gi, 0, 0))
    x_spec = pl.BlockSpec((cl, gw), lambda gi, c: (rc(c), gi))
    b_spec = pl.BlockSpec((cl, SSM_STATE), lambda gi, c: (rc(c), bo + gi))
    c_spec = pl.BlockSpec((cl, SSM_STATE), lambda gi, c: (rc(c), bo + ng + gi))
    dt_spec = pl.BlockSpec((None, hpg, cl), lambda gi, c: (gi, 0, rc(c)))
    return pl.pallas_call(
        body, name="ssd_bwd",
        out_shape=(jax.ShapeDtypeStruct((t, d_inner), F32), jax.ShapeDtypeStruct((t, ng * SSM_STATE), F32),
                   jax.ShapeDtypeStruct((t, ng * SSM_STATE), F32), jax.ShapeDtypeStruct(dtT.shape, F32),
                   jax.ShapeDtypeStruct((ng, 2 * hpg, cl), F32), jax.ShapeDtypeStruct((1, d_inner), F32)),
        grid=(ng, nc),
        in_specs=[x_spec, b_spec, c_spec, dt_spec, vec(1), vec(1),
                  pl.BlockSpec((1, gw), lambda gi, c: (0, gi)),
                  pl.BlockSpec((None, None, gw, SSM_STATE), lambda gi, c: (gi, rc(c), 0, 0)),
                  x_spec],
        out_specs=(x_spec, pl.BlockSpec((cl, SSM_STATE), lambda gi, c: (rc(c), gi)),
                   pl.BlockSpec((cl, SSM_STATE), lambda gi, c: (rc(c), gi)), dt_spec,
                   pl.BlockSpec((None, 2 * hpg, cl), lambda gi, c: (gi, 0, 0)),
                   pl.BlockSpec((1, gw), lambda gi, c: (0, gi))),
        scratch_shapes=[pltpu.VMEM((gw, SSM_STATE), F32), pltpu.VMEM((2 * hpg, cl), F32),
                        pltpu.VMEM((2 * hpg, cl), F32)],
        compiler_params=pltpu.CompilerParams(dimension_semantics=("parallel", "arbitrary")),
    )(pre, pre, pre, dtT, bias, alog, dskip_lanes, states, dy)


def _gate_norm_fwd(y, zx, norm_w, d_inner):
    t = y.shape[0]
    tr = _pick(t, (256, 128))
    gs = d_inner // SSM_GROUPS

    def body(y_ref, z_ref, w_ref, o_ref):
        for gi in range(SSM_GROUPS):
            cols = slice(gi * gs, (gi + 1) * gs)
            v = y_ref[:, cols] * _silu(z_ref[:, cols])
            r = lax.rsqrt(jnp.mean(v * v, axis=-1, keepdims=True) + NORM_EPS)
            o_ref[:, cols] = v * r * w_ref[0:1, cols]

    spec = pl.BlockSpec((tr, d_inner), lambda i: (i, 0))
    return pl.pallas_call(
        body, name="ssm_gate_norm_fwd", out_shape=jax.ShapeDtypeStruct((t, d_inner), F32), grid=(t // tr,),
        in_specs=[spec, spec, pl.BlockSpec((1, d_inner), lambda i: (0, 0))], out_specs=spec,
        compiler_params=pltpu.CompilerParams(dimension_semantics=("parallel",)),
    )(y, zx, norm_w.reshape(1, -1))


def _gate_norm_bwd(y, zx, norm_w, dout, d_inner):
    t, width = zx.shape
    tr = _pick(t, (256, 128))
    gs = d_inner // SSM_GROUPS

    def body(y_ref, z_ref, w_ref, do_ref, dy_ref, dz_ref, dw_ref):
        @pl.when(pl.program_id(0) == 0)
        def _():
            dw_ref[...] = jnp.zeros_like(dw_ref)

        for gi in range(SSM_GROUPS):
            cols = slice(gi * gs, (gi + 1) * gs)
            yv, zv = y_ref[:, cols], z_ref[:, cols]
            sz = _silu(zv)
            v = yv * sz
            r = lax.rsqrt(jnp.mean(v * v, axis=-1, keepdims=True) + NORM_EPS)
            vhat = v * r
            dn = do_ref[:, cols]
            dw_ref[0:1, cols] += jnp.sum(dn * vhat, axis=0, keepdims=True)
            dvh = dn * w_ref[0:1, cols]
            dv = r * (dvh - vhat * jnp.mean(dvh * vhat, axis=-1, keepdims=True))
            dy_ref[:, cols] = dv * sz
            dz_ref[:, cols] = (dv * yv * _silu_grad(zv)).astype(dz_ref.dtype)

    spec = pl.BlockSpec((tr, d_inner), lambda i: (i, 0))
    wspec = pl.BlockSpec((1, d_inner), lambda i: (0, 0))
    return pl.pallas_call(
        body, name="ssm_gate_norm_bwd",
        out_shape=(jax.ShapeDtypeStruct((t, d_inner), F32), jax.ShapeDtypeStruct((t, width), zx.dtype),
                   jax.ShapeDtypeStruct((1, d_inner), F32)),
        grid=(t // tr,),
        in_specs=[spec, spec, wspec, spec], out_specs=(spec, spec, wspec),
        compiler_params=pltpu.CompilerParams(dimension_semantics=("arbitrary",)),
    )(y, zx, norm_w.reshape(1, -1), dout)


def _ssm_small(dt_raw, dt_bias, a_log, d_skip):
    heads = dt_raw.shape[1]
    hpg = heads // SSM_GROUPS
    dtT = dt_raw.T.reshape(SSM_GROUPS, hpg, -1)
    return (dtT, dt_bias.reshape(SSM_GROUPS, hpg, 1), a_log.reshape(SSM_GROUPS, hpg, 1),
            jnp.repeat(d_skip, SSM_HEAD_DIM).reshape(1, -1))


@jax.custom_vjp
def ssm_core(zx, dt_raw, conv_w, conv_b, dt_bias, a_log, d_skip, norm_w):
    return _ssm_core_fwd(zx, dt_raw, conv_w, conv_b, dt_bias, a_log, d_skip, norm_w)[0]


def _ssm_core_fwd(zx, dt_raw, conv_w, conv_b, dt_bias, a_log, d_skip, norm_w):
    d_inner = norm_w.shape[0]
    pre = _conv_fwd(zx, conv_w, conv_b, d_inner)
    dtT, bias, alog, dsk = _ssm_small(dt_raw, dt_bias, a_log, d_skip)
    y, states = _ssd_fwd(pre, dtT, bias, alog, dsk, d_inner)
    out = _gate_norm_fwd(y, zx, norm_w, d_inner)
    return out, (zx, dt_raw, conv_w, dt_bias, a_log, d_skip, norm_w, pre, y, states)


def _ssm_core_bwd(res, dout):
    zx, dt_raw, conv_w, dt_bias, a_log, d_skip, norm_w, pre, y, states = res
    d_inner = norm_w.shape[0]
    heads = dt_raw.shape[1]
    dy, dzx, dnorm = _gate_norm_bwd(y, zx, norm_w, dout, d_inner)
    dtT, bias, alog, dsk = _ssm_small(dt_raw, dt_bias, a_log, d_skip)
    dx, db, dc, ddtT, acc, dsk_l = _ssd_bwd(pre, dtT, bias, alog, dsk, states, dy, d_inner)
    dpre = jnp.concatenate([dx, db, dc], axis=1)
    dzx, dconv_w, dconv_b = _conv_bwd(zx, conv_w, dpre, dzx, d_inner)
    d_dt_raw = ddtT.reshape(heads, -1).T
    hpg = heads // SSM_GROUPS
    da = acc[:, :hpg].sum(-1).reshape(heads)
    d_bias = acc[:, hpg:].sum(-1).reshape(heads)
    d_alog = da * (-jnp.exp(a_log))
    d_dskip = dsk_l.reshape(heads, SSM_HEAD_DIM).sum(-1)
    return dzx, d_dt_raw, dconv_w, dconv_b.reshape(-1), d_bias, d_alog, d_dskip, dnorm.reshape(-1)


ssm_core.defvjp(_ssm_core_fwd, _ssm_core_bwd)


def _rows_call(body, name, ins, outs, acc_outs=(), rows=256):
    t = max(a.shape[0] for a in ins)
    tr = _pick(t, (rows, 128, 64, 32, 16, 8))

    def spec(a):
        if a.shape[0] == t:
            return pl.BlockSpec((tr, a.shape[1]), lambda i: (i, 0))
        return pl.BlockSpec(a.shape, lambda i: (0, 0))

    return pl.pallas_call(
        body, name=name, out_shape=tuple(outs) + tuple(acc_outs), grid=(t // tr,),
        in_specs=[spec(a) for a in ins],
        out_specs=tuple(spec(a) for a in outs) + tuple(pl.BlockSpec(a.shape, lambda i: (0, 0)) for a in acc_outs),
        compiler_params=pltpu.CompilerParams(dimension_semantics=("arbitrary" if acc_outs else "parallel",)),
    )(*ins)


@jax.custom_vjp
def rms_op(x, gain):
    return _rms_fwd(x, gain)[0]


def _rms_fwd(x, gain):
    def body(x_ref, g_ref, o_ref):
        v = x_ref[...]
        o_ref[...] = (v * lax.rsqrt(jnp.mean(v * v, axis=-1, keepdims=True) + NORM_EPS) * g_ref[...]).astype(BF16)

    (h,) = _rows_call(body, "rms_fwd", [x, gain.reshape(1, -1)], [jax.ShapeDtypeStruct(x.shape, BF16)])
    return h, (x, gain)


def _rms_bwd(res, dh):
    x, gain = res

    def body(x_ref, g_ref, dh_ref, dx_ref, dg_ref):
        @pl.when(pl.program_id(0) == 0)
        def _():
            dg_ref[...] = jnp.zeros_like(dg_ref)

        v = x_ref[...]
        r = lax.rsqrt(jnp.mean(v * v, axis=-1, keepdims=True) + NORM_EPS)
        vhat = v * r
        d = dh_ref[...].astype(F32)
        dg_ref[...] += jnp.sum(d * vhat, axis=0, keepdims=True)
        dvh = d * g_ref[...]
        dx_ref[...] = r * (dvh - vhat * jnp.mean(dvh * vhat, axis=-1, keepdims=True))

    dx, dg = _rows_call(body, "rms_bwd", [x, gain.reshape(1, -1), dh], [jax.ShapeDtypeStruct(x.shape, F32)],
                        [jax.ShapeDtypeStruct((1, x.shape[1]), F32)])
    return dx, dg.reshape(gain.shape)


rms_op.defvjp(_rms_fwd, _rms_bwd)


@jax.custom_vjp
def swiglu_op(gu):
    return _swiglu_fwd(gu)[0]


def _swiglu_fwd(gu):
    t, f2 = gu.shape
    f = f2 // 2

    def body(gu_ref, o_ref):
        o_ref[...] = (_silu(gu_ref[:, :f].astype(F32)) * gu_ref[:, f:].astype(F32)).astype(BF16)

    (act,) = _rows_call(body, "swiglu_fwd", [gu], [jax.ShapeDtypeStruct((t, f), BF16)])
    return act, gu


def _swiglu_bwd(gu, dact):
    t, f2 = gu.shape
    f = f2 // 2

    def body(gu_ref, d_ref, o_ref):
        g, u, d = gu_ref[:, :f].astype(F32), gu_ref[:, f:].astype(F32), d_ref[...].astype(F32)
        o_ref[:, :f] = (d * u * _silu_grad(g)).astype(BF16)
        o_ref[:, f:] = (d * _silu(g)).astype(BF16)

    (dgu,) = _rows_call(body, "swiglu_bwd", [gu, dact], [jax.ShapeDtypeStruct((t, f2), BF16)])
    return (dgu,)


swiglu_op.defvjp(_swiglu_fwd, _swiglu_bwd)


@jax.custom_vjp
def ple_op(x, gl, ple):
    return _ple_fwd(x, gl, ple)[0]


def _ple_fwd(x, gl, ple):
    def body(x_ref, g_ref, p_ref, o_ref):
        o_ref[...] = x_ref[...] + jax.nn.sigmoid(g_ref[...]) * p_ref[...]

    (out,) = _rows_call(body, "ple_fwd", [x, gl, ple], [jax.ShapeDtypeStruct(x.shape, F32)])
    return out, (gl, ple)


def _ple_bwd(res, dout):
    gl, ple = res

    def body(g_ref, p_ref, d_ref, dg_ref, dp_ref):
        s, d = jax.nn.sigmoid(g_ref[...]), d_ref[...]
        dg_ref[...] = d * p_ref[...] * s * (1.0 - s)
        dp_ref[...] = d * s

    shp = jax.ShapeDtypeStruct(gl.shape, F32)
    dgl, dple = _rows_call(body, "ple_bwd", [gl, ple, dout], [shp, shp])
    return dout, dgl, dple


ple_op.defvjp(_ple_fwd, _ple_bwd)


@jax.custom_vjp
def loss_op(y, target):
    return _loss_fwd(y, target)[0]


def _loss_fwd(y, target):
    inv = 1.0 / y.shape[1]

    def body(y_ref, t_ref, d_ref, l_ref):
        @pl.when(pl.program_id(0) == 0)
        def _():
            l_ref[...] = jnp.zeros_like(l_ref)

        e = y_ref[...] - t_ref[...]
        d_ref[...] = e * inv
        part = jnp.sum(jnp.sum(e * e, axis=1, keepdims=True), axis=0, keepdims=True) * (0.5 * inv)
        l_ref[...] += jnp.broadcast_to(part, l_ref.shape)

    dy, acc = _rows_call(body, "loss_fwd", [y, target], [jax.ShapeDtypeStruct(y.shape, F32)],
                         [jax.ShapeDtypeStruct((SUBLANES, LANES), F32)])
    return acc[0, 0], dy


def _loss_bwd(dy, g):
    return dy * g, jnp.zeros_like(dy)


loss_op.defvjp(_loss_fwd, _loss_bwd)


def rmsnorm(x, gain):
    y = x * lax.rsqrt(jnp.mean(x * x, axis=-1, keepdims=True) + NORM_EPS)
    return y * gain


def causal_depthwise_conv(u, w, bias):
    k_width, chans = w.shape
    out = lax.conv_general_dilated(u, w[:, None, :], window_strides=(1,), padding=[(k_width - 1, 0)],
                                   dimension_numbers=("NWC", "WIO", "NWC"), feature_group_count=chans)
    return out + bias


def ssd_chunked(x, dt, a, bm, cm):
    b, t, heads, _ = x.shape
    nc, cl = t // SSD_CHUNK, SSD_CHUNK
    g, hg = SSM_GROUPS, heads // SSM_GROUPS
    xs = (x * dt[..., None]).reshape(b, nc, cl, g, hg, SSM_HEAD_DIM)
    a_dt = (dt * a).reshape(b, nc, cl, g, hg).transpose(0, 1, 3, 4, 2)
    a_cs = jnp.cumsum(a_dt, axis=-1)
    bc = bm.reshape(b, nc, cl, g, SSM_STATE)
    cc = cm.reshape(b, nc, cl, g, SSM_STATE)
    causal = jnp.tril(jnp.ones((cl, cl), dtype=bool))
    seg = a_cs[..., :, None] - a_cs[..., None, :]
    lmat = jnp.exp(jnp.where(causal, seg, -jnp.inf))
    cb = jnp.einsum("bclgn,bcsgn->bcgls", cc, bc)
    y_diag = jnp.einsum("bcgls,bcghls,bcsghp->bclghp", cb, lmat, xs)
    decay = jnp.exp(a_cs[..., -1:] - a_cs)
    states = jnp.einsum("bclgn,bcghl,bclghp->bcghpn", bc, decay, xs)
    chunk_decay = jnp.exp(a_cs[..., -1])

    def step(carry, inp):
        st, dec = inp
        return carry * dec[..., None, None] + st, carry

    init = jnp.zeros((b, g, hg, SSM_HEAD_DIM, SSM_STATE), F32)
    _, prev = lax.scan(step, init, (jnp.moveaxis(states, 1, 0), jnp.moveaxis(chunk_decay, 1, 0)))
    prev = jnp.moveaxis(prev, 0, 1)
    y_off = jnp.einsum("bclgn,bcghpn,bcghl->bclghp", cc, prev, jnp.exp(a_cs))
    return (y_diag + y_off).reshape(b, t, heads, SSM_HEAD_DIM)


def mamba2_mixer(h, wt_in, conv_w, conv_b, dt_bias, a_log, d_skip, norm_w, w_out):
    t, d_model = h.shape
    d_inner = 2 * d_model
    heads = d_inner // SSM_HEAD_DIM
    gn = SSM_GROUPS * SSM_STATE
    conv_dim = d_inner + 2 * gn
    zx = lin_t(h, wt_in[:d_inner + conv_dim])
    dt_raw = lin_t(h, wt_in[d_inner + conv_dim:])
    return lin(ssm_core(zx, dt_raw, conv_w, conv_b, dt_bias, a_log, d_skip, norm_w), w_out)


def ssm_core_jnp(zx, dt_raw, conv_w, conv_b, dt_bias, a_log, d_skip, norm_w):
    t = zx.shape[0]
    d_inner = norm_w.shape[0]
    heads = d_inner // SSM_HEAD_DIM
    gn = SSM_GROUPS * SSM_STATE
    z = zx[:, :d_inner]
    xbc = zx[:, d_inner:]
    xbc = jax.nn.silu(causal_depthwise_conv(xbc[None], conv_w, conv_b))[0]
    xs = xbc[:, :d_inner]
    bm = xbc[:, d_inner:d_inner + gn].reshape(1, t, SSM_GROUPS, SSM_STATE)
    cm = xbc[:, d_inner + gn:].reshape(1, t, SSM_GROUPS, SSM_STATE)
    dt = jax.nn.softplus(dt_raw + dt_bias)[None]
    a = -jnp.exp(a_log)
    xh = xs.reshape(1, t, heads, SSM_HEAD_DIM)
    y = ssd_chunked(xh, dt, a, bm, cm)
    y = y + xh * d_skip[:, None]
    y = y.reshape(t, d_inner) * jax.nn.silu(z)
    return rmsnorm(y.reshape(t, SSM_GROUPS, -1), norm_w.reshape(SSM_GROUPS, -1)).reshape(t, d_inner)


def alibi_slopes(n_heads):
    return 2.0 ** (-8.0 * jnp.arange(1, n_heads + 1, dtype=F32) / n_heads)


def dilated_group_attention(q, k, v, window, dilation, slopes):
    b, t, nh, e = q.shape
    span = window // dilation
    blk = span
    lu = t // dilation
    nb = -(-lu // blk)
    lp = nb * blk

    def to_blocks(arr):
        arr = arr.reshape(b, lu, dilation, nh, e)
        arr = jnp.pad(arr, ((0, 0), (0, lp - lu), (0, 0), (0, 0), (0, 0)))
        return arr.reshape(b, nb, blk, dilation, nh, e)

    qb, kb, vb = to_blocks(q), to_blocks(k), to_blocks(v)
    pad_prev = ((0, 0), (1, 0), (0, 0), (0, 0), (0, 0), (0, 0))
    kcat = jnp.concatenate([jnp.pad(kb, pad_prev)[:, :nb], kb], axis=2)
    vcat = jnp.concatenate([jnp.pad(vb, pad_prev)[:, :nb], vb], axis=2)
    scores = jnp.einsum("bnqrhe,bnkrhe->bnrhqk", qb, kcat) * (1.0 / math.sqrt(e))
    qi = jnp.arange(blk)[:, None]
    ki = jnp.arange(2 * blk)[None, :]
    dist = qi + blk - ki
    in_band = (dist >= 0) & (dist <= span)
    key_u = jnp.arange(nb)[:, None] * blk - blk + jnp.arange(2 * blk)[None, :]
    valid = in_band[None] & (key_u >= 0)[:, None, :]
    bias = -slopes[:, None, None] * (dilation * dist).astype(F32)[None]
    logits = jnp.where(valid[None, :, None, None], scores + bias[None, None, None], -jnp.inf)
    lse = jax.nn.logsumexp(logits, axis=-1)
    probs = jnp.exp(logits - lse[..., None])
    out = jnp.einsum("bnrhqk,bnkrhe->bnqrhe", probs, vcat)
    out = out.reshape(b, lp, dilation, nh, e)[:, :lu].reshape(b, t, nh, e)
    lse = lse.transpose(0, 1, 4, 2, 3).reshape(b, lp, dilation, nh)[:, :lu].reshape(b, t, nh)
    return out, lse


def dilated_attention_mixer(h, wt_qkv, q_gain, k_gain, w_o):
    t, d_model = h.shape
    heads = d_model // ATT_HEAD_DIM
    ng = len(DIL_PATTERNS)
    return lin(attention_core(lin_t(h, wt_qkv, BF16), q_gain, k_gain), w_o)


def attention_core_jnp(qkv, q_gain, k_gain):
    t = qkv.shape[0]
    ng = len(DIL_PATTERNS)
    heads = qkv.shape[1] // (3 * ng * ATT_HEAD_DIM)
    qkv = qkv.astype(F32).reshape(1, t, ng, 3, heads, ATT_HEAD_DIM)
    q = rmsnorm(qkv[:, :, :, 0], q_gain)
    k = rmsnorm(qkv[:, :, :, 1], k_gain)
    v = qkv[:, :, :, 2]
    slopes = alibi_slopes(heads)
    outs, lses = [], []
    for g, (window, dilation) in enumerate(DIL_PATTERNS):
        o_g, l_g = dilated_group_attention(q[:, :, g], k[:, :, g], v[:, :, g], window, dilation, slopes)
        outs.append(o_g)
        lses.append(l_g)
    alpha = jax.nn.softmax(jnp.stack(lses), axis=0)
    o = jnp.einsum("gbth,gbthe->bthe", alpha, jnp.stack(outs))
    return o.reshape(t, heads * ATT_HEAD_DIM)


def local_loss(w, x, p, target):
    depth = w['norm_mix'].shape[0]
    for i in range(depth):
        j = i // 2
        h = rms_op(x, w['norm_mix'][i])
        if i % 2 == 0:
            mix = mamba2_mixer(h, w['ssm_w_in'][j], w['ssm_conv_w'][j], w['ssm_conv_b'][j], w['ssm_dt_bias'][j],
                               w['ssm_a_log'][j], w['ssm_d_skip'][j], w['ssm_norm_w'][j], w['ssm_w_out'][j])
        else:
            mix = dilated_attention_mixer(h, w['att_w_qkv'][j], w['att_q_norm'][j], w['att_k_norm'][j],
                                          w['att_w_o'][j])
        x = x + mix
        h2 = rms_op(x, w['norm_ffn'][i])
        act = swiglu_op(lin_t(h2, w['ffn_w_gu'][i], BF16))
        x = x + lin(act, w['ffn_w_down'][i])
        x = ple_op(x, lin(x, w['ple_w_gate'][i]), lin_t(p[i], w['ple_w_proj'][i]))
    return loss_op(x, target)


def _pack_plan(shapes, width):
    plan, off = [], 0
    for name in BIG:
        layers, r, c = shapes[name]
        if name in COL_SHARDED:
            r, c = c, r
        assert (r * c) % width == 0, (name, r, c)
        pr = r * c // width
        for layer in range(layers):
            plan.append((name, layer, r, c, pr, off))
            off += _round_up(pr, BF16_ROWS)
    return plan, off


def _small_plan(shapes):
    plan, off = [], 0
    for name in SMALL:
        n = math.prod(shapes[name])
        plan.append((name, n, off))
        off += n
    return plan, _round_up(off, SUBLANES * LANES)


def kernel(x, p, norm_mix, norm_ffn, ssm_w_in, ssm_conv_w, ssm_conv_b, ssm_dt_bias, ssm_a_log, ssm_d_skip, ssm_norm_w, ssm_w_out, att_w_qkv, att_q_norm, att_k_norm, att_w_o, ffn_w_gate, ffn_w_up, ffn_w_down, ple_w_proj, ple_w_gate, loss_target, m_norm_mix, m_norm_ffn, m_ssm_w_in, m_ssm_conv_w, m_ssm_conv_b, m_ssm_dt_bias, m_ssm_a_log, m_ssm_d_skip, m_ssm_norm_w, m_ssm_w_out, m_att_w_qkv, m_att_q_norm, m_att_k_norm, m_att_w_o, m_ffn_w_gate, m_ffn_w_up, m_ffn_w_down, m_ple_w_proj, m_ple_w_gate, v_norm_mix, v_norm_ffn, v_ssm_w_in, v_ssm_conv_w, v_ssm_conv_b, v_ssm_dt_bias, v_ssm_a_log, v_ssm_d_skip, v_ssm_norm_w, v_ssm_w_out, v_att_w_qkv, v_att_q_norm, v_att_k_norm, v_att_w_o, v_ffn_w_gate, v_ffn_w_up, v_ffn_w_down, v_ple_w_proj, v_ple_w_gate):
    given = dict(locals())
    w_in = {n: given[n] for n in WEIGHTS}
    m_in = {n: given["m_" + n] for n in WEIGHTS}
    v_in = {n: given["v_" + n] for n in WEIGHTS}
    width = x.shape[-1]

    plan, total = _pack_plan({n: w_in[n].shape for n in BIG}, width)
    pieces = []
    for name, layer, r, c, pr, off in plan:
        blk = w_in[name][layer]
        if name in COL_SHARDED:
            blk = blk.T
        blk = blk.astype(BF16).reshape(pr, width)
        pieces.append(jnp.pad(blk, ((0, _round_up(pr, BF16_ROWS) - pr), (0, 0))))
    gathered = all_gather_hbm(jnp.concatenate(pieces, axis=0), "gather_weights")
    full = {}
    for name, layer, r, c, pr, off in plan:
        full.setdefault(name, []).append(gathered[:, off:off + pr].reshape(N_DEV * r, c))
    full = {n: jnp.stack(v) for n, v in full.items()}

    conv_shard = w_in['ssm_conv_w']
    cl, cw, cs = conv_shard.shape
    conv_rows = _round_up(cl * cw * cs, SUBLANES * LANES) // LANES
    conv_vec = jnp.pad(conv_shard.reshape(-1), (0, conv_rows * LANES - cl * cw * cs)).reshape(conv_rows, LANES)
    conv_all, _ = all_gather_sum_small(conv_vec, "gather_conv_w")
    conv_full = conv_all.reshape(N_DEV, -1)[:, :cl * cw * cs].reshape(N_DEV, cl, cw, cs)
    conv_full = conv_full.transpose(1, 2, 0, 3).reshape(cl, cw, N_DEV * cs)

    wl = dict(full)
    for n in SMALL:
        wl[n] = w_in[n]
    wl['ssm_conv_w'] = conv_full
    hidden = wl['ffn_w_gate'].shape[1]
    wl['ffn_w_gu'] = jnp.concatenate([wl.pop('ffn_w_gate'), wl.pop('ffn_w_up')], axis=1)

    loss_local, (gw, gx) = jax.value_and_grad(local_loss, argnums=(0, 1))(wl, x[0], p[:, 0], loss_target[0])
    loss = lax.psum(loss_local, ("x", "y", "c"))
    gw['ffn_w_gate'], gw['ffn_w_up'] = gw['ffn_w_gu'][:, :hidden], gw['ffn_w_gu'][:, hidden:]

    gpieces = []
    for name, layer, r, c, pr, off in plan:
        g = gw[name][layer].reshape(N_DEV, pr, width)
        gpieces.append(jnp.pad(g, ((0, 0), (0, _round_up(pr, BF16_ROWS) - pr), (0, 0))))
    received = all_to_all_hbm(jnp.concatenate(gpieces, axis=1), "scatter_grads")
    gsum = sum_slots(received, "sum_grads")
    grads = {}
    for name, layer, r, c, pr, off in plan:
        g = gsum[off:off + pr].reshape(r, c)
        if name in COL_SHARDED:
            g = g.T
        grads.setdefault(name, []).append(g)
    grads = {n: jnp.stack(v) for n, v in grads.items()}

    splan, stotal = _small_plan({n: gw[n].shape for n in SMALL})
    svec = jnp.concatenate([gw[n].reshape(-1) for n, _, _ in splan])
    svec = jnp.pad(svec, (0, stotal - svec.shape[0])).reshape(stotal // LANES, LANES)
    _, ssum = all_gather_sum_small(svec, "sum_small_grads")
    ssum = ssum.reshape(-1)
    for name, n, off in splan:
        grads[name] = ssum[off:off + n].reshape(gw[name].shape)
    me = _me()
    grads['ssm_conv_w'] = lax.dynamic_slice_in_dim(grads['ssm_conv_w'], me * cs, cs, axis=2)

    delta, new_m, new_v = {}, {}, {}
    for name in BIG:
        shp = w_in[name].shape
        flat = lambda a: a.reshape(-1, shp[-1])
        d, nm, nv = adamw(flat(w_in[name]), flat(grads[name]), flat(m_in[name]), flat(v_in[name]), "adamw_" + name)
        delta[name], new_m[name], new_v[name] = d.reshape(shp), nm.reshape(shp), nv.reshape(shp)
    splan2, stotal2 = _small_plan({n: w_in[n].shape for n in SMALL})

    def pack_small(src):
        vec = jnp.concatenate([src[n].reshape(-1) for n, _, _ in splan2])
        return jnp.pad(vec, (0, stotal2 - vec.shape[0]), constant_values=1.0).reshape(stotal2 // LANES, LANES)

    sd, snm, snv = adamw(pack_small(w_in), pack_small(grads), pack_small(m_in), pack_small(v_in), "adamw_small")
    for name, n, off in splan2:
        shp = w_in[name].shape
        delta[name] = sd.reshape(-1)[off:off + n].reshape(shp)
        new_m[name] = snm.reshape(-1)[off:off + n].reshape(shp)
        new_v[name] = snv.reshape(-1)[off:off + n].reshape(shp)

    return (loss, gx[None], *[grads[n] for n in WEIGHTS], *[delta[n] for n in WEIGHTS],
            *[new_m[n] for n in WEIGHTS], *[new_v[n] for n in WEIGHTS])
```

```python
import functools
import math

import jax
import jax.numpy as jnp
from jax import lax
from jax.experimental import pallas as pl
from jax.experimental.pallas import tpu as pltpu

F32 = jnp.float32
BF16 = jnp.bfloat16
N_DEV = 8
MESH = pl.DeviceIdType.MESH

SSM_HEAD_DIM = 64
SSM_GROUPS = 4
SSM_STATE = 128
CONV_WIDTH = 4
SSD_CHUNK = 128
ATT_HEAD_DIM = 64
DIL_PATTERNS = ((128, 1), (512, 4), (2048, 16))
NORM_EPS = 1e-6
ADAM_LR = 0.001
ADAM_B1 = 0.9
ADAM_B2 = 0.999
ADAM_EPS = 1e-08
ADAM_WD = 0.01
ADAM_STEP = 10

BF16_ROWS = 16
LANES = 128
SUBLANES = 8

WEIGHTS = ['norm_mix', 'norm_ffn', 'ssm_w_in', 'ssm_conv_w', 'ssm_conv_b', 'ssm_dt_bias', 'ssm_a_log', 'ssm_d_skip',
           'ssm_norm_w', 'ssm_w_out', 'att_w_qkv', 'att_q_norm', 'att_k_norm', 'att_w_o', 'ffn_w_gate', 'ffn_w_up',
           'ffn_w_down', 'ple_w_proj', 'ple_w_gate']
COL_SHARDED = ('ssm_w_in', 'att_w_qkv', 'ffn_w_gate', 'ffn_w_up', 'ple_w_proj')
ROW_SHARDED = ('ssm_w_out', 'att_w_o', 'ffn_w_down', 'ple_w_gate')
BIG = COL_SHARDED + ROW_SHARDED
SMALL = ('norm_mix', 'norm_ffn', 'ssm_conv_w', 'ssm_conv_b', 'ssm_dt_bias', 'ssm_a_log', 'ssm_d_skip', 'ssm_norm_w',
         'att_q_norm', 'att_k_norm')


def _pick(n, cands):
    for c in cands:
        if n % c == 0:
            return c
    return n


def _round_up(n, m):
    return -(-n // m) * m


MM_TILES = (1024, 1408, 512, 256, 128)
MM_VMEM_BYTES = 48 * 1024 * 1024


def _mm(a, b, *, ta=False, tb=False, out_dtype=F32, add=None, name):
    k_dim, m_dim = (a.shape if ta else a.shape[::-1])
    n_dim = b.shape[0] if tb else b.shape[1]
    assert (b.shape[1] if tb else b.shape[0]) == k_dim, (a.shape, b.shape, ta, tb)
    tm = _pick(m_dim, MM_TILES)
    tn = _pick(n_dim, MM_TILES)
    tk = _pick(k_dim, MM_TILES)
    nk = k_dim // tk
    a_spec = pl.BlockSpec((tk, tm), lambda i, j, k: (k, i)) if ta else pl.BlockSpec((tm, tk), lambda i, j, k: (i, k))
    b_spec = pl.BlockSpec((tn, tk), lambda i, j, k: (j, k)) if tb else pl.BlockSpec((tk, tn), lambda i, j, k: (k, j))
    o_spec = pl.BlockSpec((tm, tn), lambda i, j, k: (i, j))
    dims = (((0 if ta else 1,), (1 if tb else 0,)), ((), ()))
    has_add = add is not None

    def body(*refs):
        a_ref, b_ref = refs[:2]
        o_ref = refs[2 + has_add]

        def dot():
            return lax.dot_general(a_ref[...].astype(BF16), b_ref[...].astype(BF16), dims,
                                   preferred_element_type=F32)

        def finish(acc):
            if has_add:
                acc = acc + refs[2][...].astype(F32)
            o_ref[...] = acc.astype(o_ref.dtype)

        if nk == 1:
            finish(dot())
            return
        acc_ref = refs[3 + has_add]
        k = pl.program_id(2)

        @pl.when(k == 0)
        def _():
            acc_ref[...] = dot()

        @pl.when((k > 0) & (k < nk - 1))
        def _():
            acc_ref[...] += dot()

        @pl.when(k == nk - 1)
        def _():
            finish(acc_ref[...] + dot())

    return pl.pallas_call(
        body, name=f"{name}_{m_dim}x{n_dim}x{k_dim}",
        out_shape=jax.ShapeDtypeStruct((m_dim, n_dim), out_dtype),
        grid=(m_dim // tm, n_dim // tn, nk),
        in_specs=[a_spec, b_spec] + ([o_spec] if has_add else []),
        out_specs=o_spec,
        scratch_shapes=[] if nk == 1 else [pltpu.VMEM((tm, tn), F32)],
        compiler_params=pltpu.CompilerParams(dimension_semantics=("parallel", "parallel", "arbitrary"),
                                             vmem_limit_bytes=MM_VMEM_BYTES),
    )(*((a, b) + ((add,) if has_add else ())))


def _me():
    return 4 * lax.axis_index("x") + 2 * lax.axis_index("y") + lax.axis_index("c")


def _peer(j):
    x, y, c = lax.axis_index("x"), lax.axis_index("y"), lax.axis_index("c")
    px = 1 - x if j & 4 else x
    py = 1 - y if j & 2 else y
    pc = 1 - c if j & 1 else c
    return (px, py, pc), 4 * px + 2 * py + pc


def _exchange_body(src_of, dst_ref, send_sems, recv_sems, local_sem):
    me = _me()
    mine = pltpu.make_async_copy(src_of(me), dst_ref.at[me], local_sem)
    mine.start()
    sends = []
    for j in range(1, N_DEV):
        peer, pidx = _peer(j)
        cp = pltpu.make_async_remote_copy(src_ref=src_of(pidx), dst_ref=dst_ref.at[me], send_sem=send_sems.at[j - 1],
                                          recv_sem=recv_sems.at[j - 1], device_id=peer, device_id_type=MESH)
        cp.start()
        sends.append(cp)
    for j in range(1, N_DEV):
        peer, pidx = _peer(j)
        pltpu.make_async_remote_copy(src_ref=src_of(pidx), dst_ref=dst_ref.at[pidx], send_sem=send_sems.at[j - 1],
                                     recv_sem=recv_sems.at[j - 1], device_id=peer, device_id_type=MESH).wait_recv()
    for cp in sends:
        cp.wait_send()
    mine.wait()


_EXCHANGE_SCRATCH = [pltpu.SemaphoreType.DMA((N_DEV - 1,)), pltpu.SemaphoreType.DMA((N_DEV - 1,)),
                     pltpu.SemaphoreType.DMA]


def all_gather_hbm(shard, name):
    def body(x_ref, out_ref, send_sems, recv_sems, local_sem):
        _exchange_body(lambda k: x_ref, out_ref, send_sems, recv_sems, local_sem)

    return pl.pallas_call(
        body, name=name,
        out_shape=jax.ShapeDtypeStruct((N_DEV,) + shard.shape, shard.dtype),
        in_specs=[pl.BlockSpec(memory_space=pl.ANY)],
        out_specs=pl.BlockSpec(memory_space=pl.ANY),
        scratch_shapes=list(_EXCHANGE_SCRATCH),
    )(shard)


def all_to_all_hbm(slots, name):
    def body(x_ref, out_ref, send_sems, recv_sems, local_sem):
        _exchange_body(lambda k: x_ref.at[k], out_ref, send_sems, recv_sems, local_sem)

    return pl.pallas_call(
        body, name=name,
        out_shape=jax.ShapeDtypeStruct(slots.shape, slots.dtype),
        in_specs=[pl.BlockSpec(memory_space=pl.ANY)],
        out_specs=pl.BlockSpec(memory_space=pl.ANY),
        scratch_shapes=list(_EXCHANGE_SCRATCH),
    )(slots)


def all_gather_sum_small(v, name):
    def body(x_ref, out_ref, sum_ref, send_sems, recv_sems, local_sem):
        _exchange_body(lambda k: x_ref, out_ref, send_sems, recv_sems, local_sem)
        acc = out_ref[0]
        for k in range(1, N_DEV):
            acc = acc + out_ref[k]
        sum_ref[...] = acc

    return pl.pallas_call(
        body, name=name,
        out_shape=(jax.ShapeDtypeStruct((N_DEV,) + v.shape, v.dtype), jax.ShapeDtypeStruct(v.shape, v.dtype)),
        in_specs=[pl.BlockSpec(memory_space=pltpu.VMEM)],
        out_specs=(pl.BlockSpec(memory_space=pltpu.VMEM), pl.BlockSpec(memory_space=pltpu.VMEM)),
        scratch_shapes=list(_EXCHANGE_SCRATCH),
    )(v)


def sum_slots(slots, name):
    _, p_dim, c_dim = slots.shape
    tp = _pick(p_dim, (256, 128, 64, 32, 16))

    def body(x_ref, o_ref):
        acc = x_ref[0].astype(F32)
        for k in range(1, N_DEV):
            acc = acc + x_ref[k].astype(F32)
        o_ref[...] = acc

    return pl.pallas_call(
        body, name=name,
        out_shape=jax.ShapeDtypeStruct((p_dim, c_dim), F32),
        grid=(p_dim // tp,),
        in_specs=[pl.BlockSpec((N_DEV, tp, c_dim), lambda i: (0, i, 0))],
        out_specs=pl.BlockSpec((tp, c_dim), lambda i: (i, 0)),
        compiler_params=pltpu.CompilerParams(dimension_semantics=("parallel",)),
    )(slots)


def adamw(w, g, m, v, name):
    rows, cols = w.shape
    tr = _pick(rows, (256, 128, 64, 32, 16, 8))

    def body(w_ref, g_ref, m_ref, v_ref, d_ref, nm_ref, nv_ref):
        gv = g_ref[...]
        nm = ADAM_B1 * m_ref[...] + (1.0 - ADAM_B1) * gv
        nv = ADAM_B2 * v_ref[...] + (1.0 - ADAM_B2) * (gv * gv)
        m_hat = nm / (1.0 - ADAM_B1 ** ADAM_STEP)
        v_hat = nv / (1.0 - ADAM_B2 ** ADAM_STEP)
        d_ref[...] = -ADAM_LR * (m_hat / (jnp.sqrt(v_hat) + ADAM_EPS) + ADAM_WD * w_ref[...])
        nm_ref[...] = nm
        nv_ref[...] = nv

    spec = pl.BlockSpec((tr, cols), lambda i: (i, 0))
    shp = jax.ShapeDtypeStruct((rows, cols), F32)
    return pl.pallas_call(
        body, name=name, out_shape=(shp, shp, shp), grid=(rows // tr,),
        in_specs=[spec] * 4, out_specs=(spec,) * 3,
        compiler_params=pltpu.CompilerParams(dimension_semantics=("parallel",)),
    )(w, g, m, v)


ATT_BLK = 128
NEG = -1e30


def _head_sums(v):
    first = lax.broadcasted_iota(jnp.int32, (1, LANES), 1) < ATT_HEAD_DIM
    s0 = jnp.sum(jnp.where(first, v, 0.0), axis=-1, keepdims=True)
    s1 = jnp.sum(jnp.where(first, 0.0, v), axis=-1, keepdims=True)
    return jnp.where(first, s0, s1)


def _head_col(v, hmask):
    return jnp.max(jnp.where(hmask, v, -jnp.inf), axis=-1, keepdims=True)


def _qk_norm(raw, gain2):
    rstd = lax.rsqrt(_head_sums(raw * raw) * (1.0 / ATT_HEAD_DIM) + NORM_EPS)
    xhat = raw * rstd
    return xhat * gain2, xhat, rstd


def _qk_norm_bwd(dn, xhat, rstd, gain2):
    dxh = dn * gain2
    return rstd * (dxh - xhat * (_head_sums(dxh * xhat) * (1.0 / ATT_HEAD_DIM))), dn * xhat


def _att_mask_bias(n, dilation):
    qi = lax.broadcasted_iota(jnp.int32, (ATT_BLK, 2 * ATT_BLK), 0)
    ki = lax.broadcasted_iota(jnp.int32, (ATT_BLK, 2 * ATT_BLK), 1)
    dist = qi + ATT_BLK - ki
    valid = (dist >= 0) & (dist <= ATT_BLK) & ((n > 0) | (ki >= ATT_BLK))
    return valid, (dilation * dist).astype(F32)


ATT_PAIRS = 4


def _att_specs(g, dilation, hd, qkv_dim, nb, pp):
    cpb = qkv_dim // LANES
    base = g * 3 * hd // LANES
    hpn = hd // LANES
    assert cpb % pp == 0 and base % pp == 0 and hpn % pp == 0

    def spec(which, shift):
        def imap(r, hp, n):
            row = jnp.minimum(n, nb - 1) if shift == 0 else jnp.maximum(n - 1, 0)
            return (row, (r * cpb + base + which * hpn) // pp + hp)
        return pl.BlockSpec((ATT_BLK, pp * LANES), imap)

    return [spec(0, 0), spec(1, 1), spec(1, 0), spec(2, 1), spec(2, 0)]


def _att_group_fwd(qkv, gq2, gk2, slopes, g, dilation):
    t, qkv_dim = qkv.shape
    hd = qkv_dim // (3 * len(DIL_PATTERNS))
    lu = t // dilation
    nb = lu // ATT_BLK
    assert nb * ATT_BLK == lu and hd % LANES == 0
    hpn = hd // LANES
    pp = math.gcd(ATT_PAIRS, hpn)
    scale = 1.0 / math.sqrt(ATT_HEAD_DIM)

    def body(q_ref, kp_ref, kc_ref, vp_ref, vc_ref, gq_ref, gk_ref, sl_ref, o_ref, l_ref):
        n = pl.program_id(2)
        lane = lax.broadcasted_iota(jnp.int32, (1, LANES), 1)
        first = (lane // ATT_HEAD_DIM) == 0
        valid, dist = _att_mask_bias(n, dilation)
        for pair in range(pp):
            cols = slice(pair * LANES, (pair + 1) * LANES)
            qn, _, _ = _qk_norm(q_ref[:, cols].astype(F32), gq_ref[0:1, :])
            kn, _, _ = _qk_norm(jnp.concatenate([kp_ref[:, cols], kc_ref[:, cols]], axis=0).astype(F32),
                                gk_ref[0:1, :])
            kn16 = kn.astype(BF16)
            v16 = jnp.concatenate([vp_ref[:, cols], vc_ref[:, cols]], axis=0)
            outs, lses = [], []
            for hh in range(2):
                hmask = (lane // ATT_HEAD_DIM) == hh
                qh = jnp.where(hmask, qn, 0.0).astype(BF16)
                s = lax.dot_general(qh, kn16, (((1,), (1,)), ((), ())), preferred_element_type=F32) * scale
                slope = _head_col(sl_ref[pair, 0:1, :], hmask)
                logits = jnp.where(valid, s - slope * dist, NEG)
                mx = jnp.max(logits, axis=-1, keepdims=True)
                pexp = jnp.exp(logits - mx)
                den = jnp.sum(pexp, axis=-1, keepdims=True)
                outs.append(jnp.dot(pexp.astype(BF16), v16, preferred_element_type=F32) / den)
                lses.append(mx + jnp.log(den))
            o_ref[:, cols] = jnp.where(first, outs[0], outs[1])
            l_ref[:, cols] = jnp.where(first, lses[0], lses[1])

    out_spec = pl.BlockSpec((ATT_BLK, pp * LANES), lambda r, hp, n: (n, r * (hpn // pp) + hp))
    vec_spec = pl.BlockSpec((SUBLANES, LANES), lambda r, hp, n: (0, 0))
    shp = jax.ShapeDtypeStruct((lu, dilation * hd), F32)
    o, lse = pl.pallas_call(
        body, name=f"att_fwd_g{g}", out_shape=(shp, shp), grid=(dilation, hpn // pp, nb),
        in_specs=_att_specs(g, dilation, hd, qkv_dim, nb, pp) + [
            vec_spec, vec_spec, pl.BlockSpec((pp, SUBLANES, LANES), lambda r, hp, n: (hp, 0, 0))],
        out_specs=(out_spec, out_spec),
        compiler_params=pltpu.CompilerParams(dimension_semantics=("parallel", "parallel", "arbitrary")),
    )(*([qkv.reshape(lu, dilation * qkv_dim)] * 5), gq2, gk2, slopes)
    return o.reshape(t, hd), lse.reshape(t, hd)


def _att_merge(outs, lses):
    t, hd = outs[0].shape
    tr = _pick(t, (256, 128))
    ng = len(outs)

    def body(*refs):
        o_refs, l_refs, o_ref, lt_ref, o16_ref = refs[:ng], refs[ng:2 * ng], refs[2 * ng], refs[2 * ng + 1], refs[2 * ng + 2]
        ls = [r[...] for r in l_refs]
        mx = functools.reduce(jnp.maximum, ls)
        es = [jnp.exp(l - mx) for l in ls]
        den = functools.reduce(jnp.add, es)
        acc = es[0] * o_refs[0][...]
        for e, r in zip(es[1:], o_refs[1:]):
            acc = acc + e * r[...]
        o = acc / den
        o_ref[...] = o
        o16_ref[...] = o.astype(BF16)
        lt_ref[...] = mx + jnp.log(den)

    spec = pl.BlockSpec((tr, hd), lambda i: (i, 0))
    shp = jax.ShapeDtypeStruct((t, hd), F32)
    return pl.pallas_call(
        body, name="att_merge", out_shape=(shp, shp, jax.ShapeDtypeStruct((t, hd), BF16)), grid=(t // tr,),
        in_specs=[spec] * (2 * ng), out_specs=(spec, spec, spec),
        compiler_params=pltpu.CompilerParams(dimension_semantics=("parallel",)),
    )(*outs, *lses)


def _att_group_bwd(qkv, gq2, gk2, slopes, o, lse_tot, do, g, dilation):
    t, qkv_dim = qkv.shape
    hd = qkv_dim // (3 * len(DIL_PATTERNS))
    lu = t // dilation
    nb = lu // ATT_BLK
    hpn = hd // LANES
    pp = math.gcd(ATT_PAIRS, hpn)
    hbn = hpn // pp
    scale = 1.0 / math.sqrt(ATT_HEAD_DIM)

    def body(q_ref, kp_ref, kc_ref, vp_ref, vc_ref, gq_ref, gk_ref, sl_ref, o_ref, l_ref, do_ref,
             dq_ref, dk_ref, dv_ref, dgq_ref, dgk_ref, ck_ref, cv_ref):
        n = pl.program_id(2)
        lane = lax.broadcasted_iota(jnp.int32, (1, LANES), 1)
        gq, gk = gq_ref[0:1, :], gk_ref[0:1, :]

        @pl.when(n == 0)
        def _():
            ck_ref[...] = jnp.zeros_like(ck_ref)
            cv_ref[...] = jnp.zeros_like(cv_ref)
            dgq_ref[...] = jnp.zeros_like(dgq_ref)
            dgk_ref[...] = jnp.zeros_like(dgk_ref)

        valid, dist = _att_mask_bias(n, dilation)
        valid = valid & (n < nb)
        gq_sum = jnp.zeros((1, LANES), F32)
        gk_sum = jnp.zeros((1, LANES), F32)
        for pair in range(pp):
            cols = slice(pair * LANES, (pair + 1) * LANES)
            qn, qhat, qr = _qk_norm(q_ref[:, cols].astype(F32), gq)
            kn, khat, kr = _qk_norm(jnp.concatenate([kp_ref[:, cols], kc_ref[:, cols]], axis=0).astype(F32), gk)
            kn16 = kn.astype(BF16)
            v16 = jnp.concatenate([vp_ref[:, cols], vc_ref[:, cols]], axis=0)
            dov = do_ref[:, cols]
            do_o = dov * o_ref[:, cols]
            lse = l_ref[:, cols]
            dq_acc = jnp.zeros((ATT_BLK, LANES), F32)
            dk_acc = jnp.zeros((2 * ATT_BLK, LANES), F32)
            dv_acc = jnp.zeros((2 * ATT_BLK, LANES), F32)
            for hh in range(2):
                hmask = (lane // ATT_HEAD_DIM) == hh
                qh = jnp.where(hmask, qn, 0.0).astype(BF16)
                doh = jnp.where(hmask, dov, 0.0).astype(BF16)
                s = lax.dot_general(qh, kn16, (((1,), (1,)), ((), ())), preferred_element_type=F32) * scale
                slope = _head_col(sl_ref[pair, 0:1, :], hmask)
                pr = jnp.exp(jnp.where(valid, s - slope * dist - _head_col(lse, hmask), NEG))
                dp = lax.dot_general(doh, v16, (((1,), (1,)), ((), ())), preferred_element_type=F32)
                delta = jnp.sum(jnp.where(hmask, do_o, 0.0), axis=-1, keepdims=True)
                ds = (pr * (dp - delta) * scale).astype(BF16)
                dq_acc = dq_acc + jnp.where(hmask, jnp.dot(ds, kn16, preferred_element_type=F32), 0.0)
                dk_acc = dk_acc + lax.dot_general(ds, qh, (((0,), (0,)), ((), ())), preferred_element_type=F32)
                dv_acc = dv_acc + lax.dot_general(pr.astype(BF16), doh, (((0,), (0,)), ((), ())),
                                                  preferred_element_type=F32)

            dq_raw, gq_part = _qk_norm_bwd(dq_acc, qhat, qr, gq)
            gq_sum = gq_sum + jnp.sum(gq_part, axis=0, keepdims=True)

            @pl.when(n < nb)
            def _():
                dq_ref[:, cols] = dq_raw.astype(dq_ref.dtype)

            dk_prev = ck_ref[:, cols] + dk_acc[:ATT_BLK]
            dk_raw, gk_part = _qk_norm_bwd(dk_prev, khat[:ATT_BLK], kr[:ATT_BLK], gk)
            gk_sum = gk_sum + jnp.sum(gk_part, axis=0, keepdims=True)
            dk_ref[:, cols] = dk_raw.astype(dk_ref.dtype)
            dv_ref[:, cols] = (cv_ref[:, cols] + dv_acc[:ATT_BLK]).astype(dv_ref.dtype)
            ck_ref[:, cols] = dk_acc[ATT_BLK:]
            cv_ref[:, cols] = dv_acc[ATT_BLK:]

        @pl.when(n < nb)
        def _():
            dgq_ref[...] += jnp.broadcast_to(gq_sum, dgq_ref.shape)

        @pl.when(n > 0)
        def _():
            dgk_ref[...] += jnp.broadcast_to(gk_sum, dgk_ref.shape)

    width = pp * LANES
    q_out = pl.BlockSpec((ATT_BLK, width), lambda r, hp, n: (jnp.minimum(n, nb - 1), r * hbn + hp))
    kv_out = pl.BlockSpec((ATT_BLK, width), lambda r, hp, n: (jnp.maximum(n - 1, 0), r * hbn + hp))
    vec_spec = pl.BlockSpec((SUBLANES, LANES), lambda r, hp, n: (0, 0))
    g_out = pl.BlockSpec((None, SUBLANES, LANES), lambda r, hp, n: (r * hbn + hp, 0, 0))
    shp = jax.ShapeDtypeStruct((lu, dilation * hd), BF16)
    gshp = jax.ShapeDtypeStruct((dilation * hbn, SUBLANES, LANES), F32)
    view = lambda a: a.reshape(lu, dilation * hd)
    dq, dk, dv, dgq, dgk = pl.pallas_call(
        body, name=f"att_bwd_g{g}", out_shape=(shp, shp, shp, gshp, gshp), grid=(dilation, hbn, nb + 1),
        in_specs=_att_specs(g, dilation, hd, qkv_dim, nb, pp) + [
            vec_spec, vec_spec, pl.BlockSpec((pp, SUBLANES, LANES), lambda r, hp, n: (hp, 0, 0)), q_out, q_out, q_out],
        out_specs=(q_out, kv_out, kv_out, g_out, g_out),
        scratch_shapes=[pltpu.VMEM((ATT_BLK, width), F32), pltpu.VMEM((ATT_BLK, width), F32)],
        compiler_params=pltpu.CompilerParams(dimension_semantics=("parallel", "parallel", "arbitrary")),
    )(*([qkv.reshape(lu, dilation * qkv_dim)] * 5), gq2, gk2, slopes, view(o), view(lse_tot), view(do))
    return dq.reshape(t, hd), dk.reshape(t, hd), dv.reshape(t, hd), dgq, dgk


def _att_consts(q_gain, k_gain, hd):
    heads = hd // ATT_HEAD_DIM
    gq2 = jnp.broadcast_to(jnp.tile(q_gain, 2)[None], (SUBLANES, LANES))
    gk2 = jnp.broadcast_to(jnp.tile(k_gain, 2)[None], (SUBLANES, LANES))
    sl = 2.0 ** (-8.0 * jnp.arange(1, heads + 1, dtype=F32) / heads)
    slopes = jnp.broadcast_to(jnp.repeat(sl, ATT_HEAD_DIM).reshape(hd // LANES, 1, LANES), (hd // LANES, SUBLANES, LANES))
    return gq2, gk2, slopes


def _attention_core_fwd(qkv, q_gain, k_gain):
    hd = qkv.shape[1] // (3 * len(DIL_PATTERNS))
    gq2, gk2, slopes = _att_consts(q_gain, k_gain, hd)
    outs, lses = [], []
    for g, (_, dilation) in enumerate(DIL_PATTERNS):
        o_g, l_g = _att_group_fwd(qkv, gq2, gk2, slopes, g, dilation)
        outs.append(o_g)
        lses.append(l_g)
    o, lse_tot, o16 = _att_merge(outs, lses)
    return o16, (qkv, q_gain, k_gain, o, lse_tot)


def _attention_core_bwd(res, do):
    qkv, q_gain, k_gain, o, lse_tot = res
    hd = o.shape[1]
    gq2, gk2, slopes = _att_consts(q_gain, k_gain, hd)
    parts, dgq, dgk = [], 0.0, 0.0
    for g, (_, dilation) in enumerate(DIL_PATTERNS):
        dq, dk, dv, a, b = _att_group_bwd(qkv, gq2, gk2, slopes, o, lse_tot, do, g, dilation)
        parts += [dq, dk, dv]
        dgq = dgq + a[:, 0, :].reshape(-1, ATT_HEAD_DIM).sum(0)
        dgk = dgk + b[:, 0, :].reshape(-1, ATT_HEAD_DIM).sum(0)
    return jnp.concatenate(parts, axis=1), dgq, dgk


HALO = 8


def _silu(v):
    return v * jax.nn.sigmoid(v)


def _silu_grad(v):
    s = jax.nn.sigmoid(v)
    return s * (1.0 + v * (1.0 - s))


def _conv_fwd(zx, conv_w, conv_b, d_inner):
    t = zx.shape[0]
    conv_dim = conv_w.shape[1]
    cb = _pick(d_inner, (1024, 512, 256, 128))
    assert conv_dim % cb == 0
    tr = _pick(t, (256, 128))
    off = d_inner // cb

    def body(x_ref, h_ref, w_ref, b_ref, o_ref):
        i = pl.program_id(1)
        halo = jnp.where(i > 0, h_ref[...], 0.0)
        ext = jnp.concatenate([halo, x_ref[...]], axis=0)
        acc = jnp.broadcast_to(b_ref[...], (tr, cb))
        for k in range(CONV_WIDTH):
            s = CONV_WIDTH - 1 - k
            sh = ext if s == 0 else pltpu.roll(ext, shift=s, axis=0)
            acc = acc + w_ref[k:k + 1, :] * sh[HALO:HALO + tr]
        o_ref[...] = acc

    return pl.pallas_call(
        body, name="ssm_conv_fwd", out_shape=jax.ShapeDtypeStruct((t, conv_dim), F32),
        grid=(conv_dim // cb, t // tr),
        in_specs=[pl.BlockSpec((tr, cb), lambda j, i: (i, off + j)),
                  pl.BlockSpec((HALO, cb), lambda j, i: (jnp.maximum(i * (tr // HALO) - 1, 0), off + j)),
                  pl.BlockSpec((CONV_WIDTH, cb), lambda j, i: (0, j)),
                  pl.BlockSpec((1, cb), lambda j, i: (0, j))],
        out_specs=pl.BlockSpec((tr, cb), lambda j, i: (i, j)),
        compiler_params=pltpu.CompilerParams(dimension_semantics=("parallel", "parallel")),
    )(zx, zx, conv_w, conv_b.reshape(1, -1))


def _conv_bwd(zx, conv_w, dpre, dzx, d_inner):
    t, width = zx.shape
    conv_dim = conv_w.shape[1]
    cb = _pick(d_inner, (1024, 512, 256, 128))
    tr = _pick(t, (256, 128))
    off = d_inner // cb
    nr = t // tr

    def body(x_ref, h_ref, w_ref, d_ref, dn_ref, dzx_in, dx_ref, dw_ref, db_ref):
        i = pl.program_id(1)

        @pl.when(i == 0)
        def _():
            dw_ref[...] = jnp.zeros_like(dw_ref)
            db_ref[...] = jnp.zeros_like(db_ref)

        halo = jnp.where(i > 0, h_ref[...], 0.0)
        ext = jnp.concatenate([halo, x_ref[...]], axis=0)
        d = d_ref[...]
        dext = jnp.concatenate([d, jnp.where(i < nr - 1, dn_ref[...], 0.0)], axis=0)
        dx = jnp.zeros((tr, cb), F32)
        for k in range(CONV_WIDTH):
            s = CONV_WIDTH - 1 - k
            fut = dext if s == 0 else pltpu.roll(dext, shift=tr + HALO - s, axis=0)
            dx = dx + w_ref[k:k + 1, :] * fut[:tr]
            past = ext if s == 0 else pltpu.roll(ext, shift=s, axis=0)
            dw_ref[k:k + 1, :] += jnp.sum(d * past[HALO:HALO + tr], axis=0, keepdims=True)
        dx_ref[...] = dx.astype(dx_ref.dtype)
        db_ref[...] += jnp.sum(d, axis=0, keepdims=True)

    last_halo = t // HALO - 1
    return pl.pallas_call(
        body, name="ssm_conv_bwd",
        out_shape=(jax.ShapeDtypeStruct(dzx.shape, dzx.dtype), jax.ShapeDtypeStruct((CONV_WIDTH, conv_dim), F32),
                   jax.ShapeDtypeStruct((1, conv_dim), F32)),
        grid=(conv_dim // cb, nr),
        in_specs=[pl.BlockSpec((tr, cb), lambda j, i: (i, off + j)),
                  pl.BlockSpec((HALO, cb), lambda j, i: (jnp.maximum(i * (tr // HALO) - 1, 0), off + j)),
                  pl.BlockSpec((CONV_WIDTH, cb), lambda j, i: (0, j)),
                  pl.BlockSpec((tr, cb), lambda j, i: (i, j)),
                  pl.BlockSpec((HALO, cb), lambda j, i: (jnp.minimum((i + 1) * (tr // HALO), last_halo), j)),
                  pl.BlockSpec(memory_space=pl.ANY)],
        out_specs=(pl.BlockSpec((tr, cb), lambda j, i: (i, off + j)),
                   pl.BlockSpec((CONV_WIDTH, cb), lambda j, i: (0, j)),
                   pl.BlockSpec((1, cb), lambda j, i: (0, j))),
        input_output_aliases={5: 0},
        compiler_params=pltpu.CompilerParams(dimension_semantics=("parallel", "arbitrary")),
    )(zx, zx, conv_w, dpre, dpre, dzx)


def _eye(n):
    return lax.broadcasted_iota(jnp.int32, (n, n), 0) == lax.broadcasted_iota(jnp.int32, (n, n), 1)


def _row_to_col(row):
    n = row.shape[1]
    return jnp.sum(jnp.where(_eye(n), row, 0.0), axis=1, keepdims=True)


def _col_to_row(col):
    n = col.shape[0]
    return jnp.sum(jnp.where(_eye(n), col, 0.0), axis=0, keepdims=True)


def _pair_lanes(c0, c1):
    lane = lax.broadcasted_iota(jnp.int32, (1, LANES), 1)
    return jnp.where(lane < SSM_HEAD_DIM, c0, c1)


def _ssd_chunk_common(pre_x_ref, pre_b_ref, pre_c_ref, dtr_ref, bias_ref, alog_ref, cs_ref):
    cl = SSD_CHUNK
    hpg = dtr_ref.shape[0]
    x = _silu(pre_x_ref[...])
    b16 = _silu(pre_b_ref[...]).astype(BF16)
    c16 = _silu(pre_c_ref[...]).astype(BF16)
    dt = jax.nn.softplus(dtr_ref[...] + bias_ref[...])
    a = -jnp.exp(alog_ref[...])
    li = lax.broadcasted_iota(jnp.int32, (cl, cl), 0)
    si = lax.broadcasted_iota(jnp.int32, (cl, cl), 1)
    upper = (li <= si).astype(F32)
    cs_ref[0:hpg, :] = jnp.dot(dt * a, upper, precision=lax.Precision.HIGHEST, preferred_element_type=F32)
    cs_ref[hpg:2 * hpg, :] = dt
    g = lax.dot_general(c16, b16, (((1,), (1,)), ((), ())), preferred_element_type=F32)
    return x, b16, c16, dt, a, g, li >= si


def _ssd_fwd(pre, dtT, bias, alog, dskip_lanes, d_inner):
    t = pre.shape[0]
    cl = SSD_CHUNK
    nc = t // cl
    ng = SSM_GROUPS
    hpg = dtT.shape[1]
    gw = hpg * SSM_HEAD_DIM
    assert d_inner == ng * gw and hpg % 2 == 0
    bo = d_inner // SSM_STATE

    def body(px_ref, pb_ref, pc_ref, dtr_ref, bias_ref, alog_ref, dsk_ref, y_ref, st_ref, s_ref, cs_ref):
        c = pl.program_id(1)

        @pl.when(c == 0)
        def _():
            s_ref[...] = jnp.zeros_like(s_ref)

        x, b16, c16, dt, a, g, causal = _ssd_chunk_common(px_ref, pb_ref, pc_ref, dtr_ref, bias_ref, alog_ref, cs_ref)
        st_ref[...] = s_ref[...]
        yoff = lax.dot_general(c16, s_ref[...].astype(BF16), (((1,), (1,)), ((), ())), preferred_element_type=F32)
        xde_parts = []
        for j in range(hpg // 2):
            cols = slice(j * LANES, (j + 1) * LANES)
            xp = x[:, cols]
            dcol, ecol, ocol, ms = [], [], [], []
            for hh in range(2):
                h = 2 * j + hh
                cs_row = cs_ref[h:h + 1, :]
                cs_col = _row_to_col(cs_row)
                dcol.append(_row_to_col(cs_ref[hpg + h:hpg + h + 1, :]))
                ecol.append(jnp.exp(cs_ref[h:h + 1, cl - 1:cl] - cs_col))
                ocol.append(jnp.exp(cs_col))
                lm = jnp.where(causal, jnp.exp(jnp.minimum(cs_col - cs_row, 0.0)), 0.0)
                ms.append((g * lm).astype(BF16))
            xd = xp * _pair_lanes(dcol[0], dcol[1])
            xd16 = xd.astype(BF16)
            yd = _pair_lanes(1.0, 0.0) * jnp.dot(ms[0], xd16, preferred_element_type=F32) \
                + _pair_lanes(0.0, 1.0) * jnp.dot(ms[1], xd16, preferred_element_type=F32)
            y_ref[:, cols] = yd + yoff[:, cols] * _pair_lanes(ocol[0], ocol[1]) + xp * dsk_ref[0:1, cols]
            xde_parts.append((xd * _pair_lanes(ecol[0], ecol[1])).astype(BF16))
        new = lax.dot_general(jnp.concatenate(xde_parts, axis=1), b16, (((0,), (0,)), ((), ())),
                              preferred_element_type=F32)
        for h in range(hpg):
            rows = slice(h * SSM_HEAD_DIM, (h + 1) * SSM_HEAD_DIM)
            s_ref[rows, :] = s_ref[rows, :] * jnp.exp(cs_ref[h:h + 1, cl - 1:cl]) + new[rows, :]

    vec = lambda n: pl.BlockSpec((None, hpg, n), lambda gi, c: (gi, 0, 0))
    return pl.pallas_call(
        body, name="ssd_fwd",
        out_shape=(jax.ShapeDtypeStruct((t, d_inner), F32), jax.ShapeDtypeStruct((ng, nc, gw, SSM_STATE), F32)),
        grid=(ng, nc),
        in_specs=[pl.BlockSpec((cl, gw), lambda gi, c: (c, gi)),
                  pl.BlockSpec((cl, SSM_STATE), lambda gi, c: (c, bo + gi)),
                  pl.BlockSpec((cl, SSM_STATE), lambda gi, c: (c, bo + ng + gi)),
                  pl.BlockSpec((None, hpg, cl), lambda gi, c: (gi, 0, c)),
                  vec(1), vec(1),
                  pl.BlockSpec((1, gw), lambda gi, c: (0, gi))],
        out_specs=(pl.BlockSpec((cl, gw), lambda gi, c: (c, gi)),
                   pl.BlockSpec((None, None, gw, SSM_STATE), lambda gi, c: (gi, c, 0, 0))),
        scratch_shapes=[pltpu.VMEM((gw, SSM_STATE), F32), pltpu.VMEM((2 * hpg, cl), F32)],
        compiler_params=pltpu.CompilerParams(dimension_semantics=("parallel", "arbitrary")),
    )(pre, pre, pre, dtT, bias, alog, dskip_lanes)


def _ssd_bwd(pre, dtT, bias, alog, dskip_lanes, states, dy, d_inner):
    t, conv_dim = pre.shape
    cl = SSD_CHUNK
    nc = t // cl
    ng = SSM_GROUPS
    hpg = dtT.shape[1]
    gw = hpg * SSM_HEAD_DIM
    bo = d_inner // SSM_STATE

    def body(px_ref, pb_ref, pc_ref, dtr_ref, bias_ref, alog_ref, dsk_ref, st_ref, dy_ref,
             dx_ref, db_ref, dc_ref, ddt_ref, acc_ref, dsk_out, ds_ref, cs_ref, dcs_ref):
        c = pl.program_id(1)

        @pl.when(c == 0)
        def _():
            ds_ref[...] = jnp.zeros_like(ds_ref)
            acc_ref[...] = jnp.zeros_like(acc_ref)
            dsk_out[...] = jnp.zeros_like(dsk_out)

        x, b16, c16, dt, a, g, causal = _ssd_chunk_common(px_ref, pb_ref, pc_ref, dtr_ref, bias_ref, alog_ref, cs_ref)
        s_prev = st_ref[...]
        s16 = s_prev.astype(BF16)
        ds = ds_ref[...]
        ds16 = ds.astype(BF16)
        dyv = dy_ref[...]
        yoff = lax.dot_general(c16, s16, (((1,), (1,)), ((), ())), preferred_element_type=F32)
        bds = lax.dot_general(b16, ds16, (((1,), (1,)), ((), ())), preferred_element_type=F32)
        dg = jnp.zeros((cl, cl), F32)
        xde_parts, dye_parts = [], []
        lane = lax.broadcasted_iota(jnp.int32, (1, LANES), 1)
        for j in range(hpg // 2):
            cols = slice(j * LANES, (j + 1) * LANES)
            xp, dyp = x[:, cols], dyv[:, cols]
            dcol, ecol, ocol, lms = [], [], [], []
            for hh in range(2):
                h = 2 * j + hh
                cs_row = cs_ref[h:h + 1, :]
                cs_col = _row_to_col(cs_row)
                dcol.append(_row_to_col(cs_ref[hpg + h:hpg + h + 1, :]))
                ecol.append(jnp.exp(cs_ref[h:h + 1, cl - 1:cl] - cs_col))
                ocol.append(jnp.exp(cs_col))
                lms.append(jnp.where(causal, jnp.exp(jnp.minimum(cs_col - cs_row, 0.0)), 0.0))
            dlanes, elanes, olanes = _pair_lanes(*dcol), _pair_lanes(*ecol), _pair_lanes(*ocol)
            xd = xp * dlanes
            xd16 = xd.astype(BF16)
            xde = xd * elanes
            yoffp = yoff[:, cols] * olanes
            bdsp = bds[:, cols]
            dxd = bdsp * elanes
            for hh in range(2):
                h = 2 * j + hh
                hmask = (lane // SSM_HEAD_DIM) == hh
                dyh16 = jnp.where(hmask, dyp, 0.0).astype(BF16)
                m = g * lms[hh]
                dm = lax.dot_general(dyh16, xd16, (((1,), (1,)), ((), ())), preferred_element_type=F32)
                w = dm * m
                dg = dg + dm * lms[hh]
                dxd = dxd + lax.dot_general(m.astype(BF16), dyh16, (((0,), (0,)), ((), ())),
                                            preferred_element_type=F32)
                term = jnp.sum(jnp.where(hmask, xde * bdsp, 0.0), axis=1, keepdims=True)
                dcs_col = (jnp.sum(w, axis=1, keepdims=True)
                           + jnp.sum(jnp.where(hmask, dyp * yoffp, 0.0), axis=1, keepdims=True) - term)
                rows = slice(h * SSM_HEAD_DIM, (h + 1) * SSM_HEAD_DIM)
                dec = jnp.exp(cs_ref[h:h + 1, cl - 1:cl])
                tail = jnp.sum(term, axis=0, keepdims=True) + dec * jnp.sum(
                    jnp.sum(s_prev[rows, :] * ds[rows, :], axis=1, keepdims=True), axis=0, keepdims=True)
                last = lax.broadcasted_iota(jnp.int32, (1, cl), 1) == cl - 1
                dcs_ref[h:h + 1, :] = _col_to_row(dcs_col) - jnp.sum(w, axis=0, keepdims=True) + jnp.where(last, tail, 0.0)
                dcs_ref[hpg + h:hpg + h + 1, :] = _col_to_row(
                    jnp.sum(jnp.where(hmask, dxd * xp, 0.0), axis=1, keepdims=True))
            dx_act = dxd * dlanes + dyp * dsk_ref[0:1, cols]
            dx_ref[:, cols] = dx_act * _silu_grad(px_ref[:, cols])
            dsk_out[0:1, cols] += jnp.sum(dyp * xp, axis=0, keepdims=True)
            xde_parts.append(xde.astype(BF16))
            dye_parts.append((dyp * olanes).astype(BF16))
        xde16 = jnp.concatenate(xde_parts, axis=1)
        dye16 = jnp.concatenate(dye_parts, axis=1)
        dg16 = dg.astype(BF16)
        dc_act = jnp.dot(dg16, b16, preferred_element_type=F32) + jnp.dot(dye16, s16, preferred_element_type=F32)
        db_act = lax.dot_general(dg16, c16, (((0,), (0,)), ((), ())), preferred_element_type=F32) \
            + jnp.dot(xde16, ds16, preferred_element_type=F32)
        dc_ref[...] = dc_act * _silu_grad(pc_ref[...])
        db_ref[...] = db_act * _silu_grad(pb_ref[...])
        ds_new = lax.dot_general(dye16, c16, (((0,), (0,)), ((), ())), preferred_element_type=F32)
        for h in range(hpg):
            rows = slice(h * SSM_HEAD_DIM, (h + 1) * SSM_HEAD_DIM)
            ds_ref[rows, :] = ds[rows, :] * jnp.exp(cs_ref[h:h + 1, cl - 1:cl]) + ds_new[rows, :]
        li = lax.broadcasted_iota(jnp.int32, (cl, cl), 0)
        si = lax.broadcasted_iota(jnp.int32, (cl, cl), 1)
        d_adt = jnp.dot(dcs_ref[0:hpg, :], (li >= si).astype(F32), precision=lax.Precision.HIGHEST,
                        preferred_element_type=F32)
        ddt = d_adt * a + dcs_ref[hpg:2 * hpg, :]
        ddt_raw = ddt * jax.nn.sigmoid(dtr_ref[...] + bias_ref[...])
        ddt_ref[...] = ddt_raw
        acc_ref[0:hpg, :] += d_adt * dt
        acc_ref[hpg:2 * hpg, :] += ddt_raw

    rc = lambda c: nc - 1 - c
    vec = lambda n: pl.BlockSpec((None, hpg, n), lambda gi, c: (gi, 0, 0))
    x_spec = pl.BlockSpec((cl, gw), lambda gi, c: (rc(c), gi))
    b_spec = pl.BlockSpec((cl, SSM_STATE), lambda gi, c: (rc(c), bo + gi))
    c_spec = pl.BlockSpec((cl, SSM_STATE), lambda gi, c: (rc(c), bo + ng + gi))
    dt_spec = pl.BlockSpec((None, hpg, cl), lambda gi, c: (gi, 0, rc(c)))
    return pl.pallas_call(
        body, name="ssd_bwd",
        out_shape=(jax.ShapeDtypeStruct((t, d_inner), F32), jax.ShapeDtypeStruct((t, ng * SSM_STATE), F32),
                   jax.ShapeDtypeStruct((t, ng * SSM_STATE), F32), jax.ShapeDtypeStruct(dtT.shape, F32),
                   jax.ShapeDtypeStruct((ng, 2 * hpg, cl), F32), jax.ShapeDtypeStruct((1, d_inner), F32)),
        grid=(ng, nc),
        in_specs=[x_spec, b_spec, c_spec, dt_spec, vec(1), vec(1),
                  pl.BlockSpec((1, gw), lambda gi, c: (0, gi)),
                  pl.BlockSpec((None, None, gw, SSM_STATE), lambda gi, c: (gi, rc(c), 0, 0)),
                  x_spec],
        out_specs=(x_spec, pl.BlockSpec((cl, SSM_STATE), lambda gi, c: (rc(c), gi)),
                   pl.BlockSpec((cl, SSM_STATE), lambda gi, c: (rc(c), gi)), dt_spec,
                   pl.BlockSpec((None, 2 * hpg, cl), lambda gi, c: (gi, 0, 0)),
                   pl.BlockSpec((1, gw), lambda gi, c: (0, gi))),
        scratch_shapes=[pltpu.VMEM((gw, SSM_STATE), F32), pltpu.VMEM((2 * hpg, cl), F32),
                        pltpu.VMEM((2 * hpg, cl), F32)],
        compiler_params=pltpu.CompilerParams(dimension_semantics=("parallel", "arbitrary")),
    )(pre, pre, pre, dtT, bias, alog, dskip_lanes, states, dy)


def _gate_norm_fwd(y, zx, norm_w, d_inner):
    t = y.shape[0]
    tr = _pick(t, (256, 128))
    gs = d_inner // SSM_GROUPS

    def body(y_ref, z_ref, w_ref, o_ref):
        for gi in range(SSM_GROUPS):
            cols = slice(gi * gs, (gi + 1) * gs)
            v = y_ref[:, cols] * _silu(z_ref[:, cols])
            r = lax.rsqrt(jnp.mean(v * v, axis=-1, keepdims=True) + NORM_EPS)
            o_ref[:, cols] = (v * r * w_ref[0:1, cols]).astype(BF16)

    spec = pl.BlockSpec((tr, d_inner), lambda i: (i, 0))
    return pl.pallas_call(
        body, name="ssm_gate_norm_fwd", out_shape=jax.ShapeDtypeStruct((t, d_inner), BF16), grid=(t // tr,),
        in_specs=[spec, spec, pl.BlockSpec((1, d_inner), lambda i: (0, 0))], out_specs=spec,
        compiler_params=pltpu.CompilerParams(dimension_semantics=("parallel",)),
    )(y, zx, norm_w.reshape(1, -1))


def _gate_norm_bwd(y, zx, norm_w, dout, d_inner):
    t, width = zx.shape
    tr = _pick(t, (256, 128))
    gs = d_inner // SSM_GROUPS

    def body(y_ref, z_ref, w_ref, do_ref, dy_ref, dz_ref, dw_ref):
        @pl.when(pl.program_id(0) == 0)
        def _():
            dw_ref[...] = jnp.zeros_like(dw_ref)

        for gi in range(SSM_GROUPS):
            cols = slice(gi * gs, (gi + 1) * gs)
            yv, zv = y_ref[:, cols], z_ref[:, cols]
            sz = _silu(zv)
            v = yv * sz
            r = lax.rsqrt(jnp.mean(v * v, axis=-1, keepdims=True) + NORM_EPS)
            vhat = v * r
            dn = do_ref[:, cols].astype(F32)
            dw_ref[0:1, cols] += jnp.sum(dn * vhat, axis=0, keepdims=True)
            dvh = dn * w_ref[0:1, cols]
            dv = r * (dvh - vhat * jnp.mean(dvh * vhat, axis=-1, keepdims=True))
            dy_ref[:, cols] = dv * sz
            dz_ref[:, cols] = (dv * yv * _silu_grad(zv)).astype(dz_ref.dtype)

    spec = pl.BlockSpec((tr, d_inner), lambda i: (i, 0))
    wspec = pl.BlockSpec((1, d_inner), lambda i: (0, 0))
    return pl.pallas_call(
        body, name="ssm_gate_norm_bwd",
        out_shape=(jax.ShapeDtypeStruct((t, d_inner), F32), jax.ShapeDtypeStruct((t, width), zx.dtype),
                   jax.ShapeDtypeStruct((1, d_inner), F32)),
        grid=(t // tr,),
        in_specs=[spec, spec, wspec, spec], out_specs=(spec, spec, wspec),
        compiler_params=pltpu.CompilerParams(dimension_semantics=("arbitrary",)),
    )(y, zx, norm_w.reshape(1, -1), dout)


def _ssm_small(dt_raw, dt_bias, a_log, d_skip):
    heads = dt_raw.shape[1]
    hpg = heads // SSM_GROUPS
    dtT = dt_raw.T.reshape(SSM_GROUPS, hpg, -1)
    return (dtT, dt_bias.reshape(SSM_GROUPS, hpg, 1), a_log.reshape(SSM_GROUPS, hpg, 1),
            jnp.repeat(d_skip, SSM_HEAD_DIM).reshape(1, -1))


def _ssm_core_fwd(zx, dt_raw, conv_w, conv_b, dt_bias, a_log, d_skip, norm_w):
    d_inner = norm_w.shape[0]
    pre = _conv_fwd(zx, conv_w, conv_b, d_inner)
    dtT, bias, alog, dsk = _ssm_small(dt_raw, dt_bias, a_log, d_skip)
    y, states = _ssd_fwd(pre, dtT, bias, alog, dsk, d_inner)
    out = _gate_norm_fwd(y, zx, norm_w, d_inner)
    return out, (zx, dt_raw, conv_w, dt_bias, a_log, d_skip, norm_w, pre, y, states)


def _ssm_core_bwd(res, dout):
    zx, dt_raw, conv_w, dt_bias, a_log, d_skip, norm_w, pre, y, states = res
    d_inner = norm_w.shape[0]
    heads = dt_raw.shape[1]
    dy, dzx, dnorm = _gate_norm_bwd(y, zx, norm_w, dout, d_inner)
    dtT, bias, alog, dsk = _ssm_small(dt_raw, dt_bias, a_log, d_skip)
    dx, db, dc, ddtT, acc, dsk_l = _ssd_bwd(pre, dtT, bias, alog, dsk, states, dy, d_inner)
    dpre = jnp.concatenate([dx, db, dc], axis=1)
    dzx, dconv_w, dconv_b = _conv_bwd(zx, conv_w, dpre, dzx, d_inner)
    d_dt_raw = ddtT.reshape(heads, -1).T
    hpg = heads // SSM_GROUPS
    da = acc[:, :hpg].sum(-1).reshape(heads)
    d_bias = acc[:, hpg:].sum(-1).reshape(heads)
    d_alog = da * (-jnp.exp(a_log))
    d_dskip = dsk_l.reshape(heads, SSM_HEAD_DIM).sum(-1)
    return dzx, d_dt_raw, dconv_w, dconv_b.reshape(-1), d_bias, d_alog, d_dskip, dnorm.reshape(-1)


def _rows_call(body, name, ins, outs, acc_outs=(), rows=256):
    t = max(a.shape[0] for a in ins)
    tr = _pick(t, (rows, 128, 64, 32, 16, 8))

    def spec(a):
        if a.shape[0] == t:
            return pl.BlockSpec((tr, a.shape[1]), lambda i: (i, 0))
        return pl.BlockSpec(a.shape, lambda i: (0, 0))

    return pl.pallas_call(
        body, name=name, out_shape=tuple(outs) + tuple(acc_outs), grid=(t // tr,),
        in_specs=[spec(a) for a in ins],
        out_specs=tuple(spec(a) for a in outs) + tuple(pl.BlockSpec(a.shape, lambda i: (0, 0)) for a in acc_outs),
        compiler_params=pltpu.CompilerParams(dimension_semantics=("arbitrary" if acc_outs else "parallel",)),
    )(*ins)


def _rms_fwd(x, gain):
    def body(x_ref, g_ref, o_ref):
        v = x_ref[...]
        o_ref[...] = (v * lax.rsqrt(jnp.mean(v * v, axis=-1, keepdims=True) + NORM_EPS) * g_ref[...]).astype(BF16)

    (h,) = _rows_call(body, "rms_fwd", [x, gain.reshape(1, -1)], [jax.ShapeDtypeStruct(x.shape, BF16)])
    return h


def _rms_bwd(x, gain, dh, dres):
    def body(x_ref, g_ref, dh_ref, dr_ref, dx_ref, dg_ref):
        @pl.when(pl.program_id(0) == 0)
        def _():
            dg_ref[...] = jnp.zeros_like(dg_ref)

        v = x_ref[...]
        r = lax.rsqrt(jnp.mean(v * v, axis=-1, keepdims=True) + NORM_EPS)
        vhat = v * r
        d = dh_ref[...].astype(F32)
        dg_ref[...] += jnp.sum(d * vhat, axis=0, keepdims=True)
        dvh = d * g_ref[...]
        dx_ref[...] = dr_ref[...] + r * (dvh - vhat * jnp.mean(dvh * vhat, axis=-1, keepdims=True))

    dx, dg = _rows_call(body, "rms_bwd", [x, gain.reshape(1, -1), dh, dres], [jax.ShapeDtypeStruct(x.shape, F32)],
                        [jax.ShapeDtypeStruct((1, x.shape[1]), F32)])
    return dx, dg.reshape(gain.shape)


def _swiglu_fwd(gu):
    t, f2 = gu.shape
    f = f2 // 2

    def body(gu_ref, o_ref):
        o_ref[...] = (_silu(gu_ref[:, :f].astype(F32)) * gu_ref[:, f:].astype(F32)).astype(BF16)

    (act,) = _rows_call(body, "swiglu_fwd", [gu], [jax.ShapeDtypeStruct((t, f), BF16)])
    return act


def _swiglu_bwd(gu, dact):
    t, f2 = gu.shape
    f = f2 // 2

    def body(gu_ref, d_ref, o_ref):
        g, u, d = gu_ref[:, :f].astype(F32), gu_ref[:, f:].astype(F32), d_ref[...].astype(F32)
        o_ref[:, :f] = (d * u * _silu_grad(g)).astype(BF16)
        o_ref[:, f:] = (d * _silu(g)).astype(BF16)

    (dgu,) = _rows_call(body, "swiglu_bwd", [gu, dact], [jax.ShapeDtypeStruct((t, f2), BF16)])
    return dgu


def _ple_fwd(x, gl, ple):
    def body(x_ref, g_ref, p_ref, o_ref):
        o_ref[...] = x_ref[...] + jax.nn.sigmoid(g_ref[...]) * p_ref[...]

    (out,) = _rows_call(body, "ple_fwd", [x, gl, ple], [jax.ShapeDtypeStruct(x.shape, F32)])
    return out


def _ple_bwd(gl, ple, dout):
    def body(g_ref, p_ref, d_ref, dg_ref, dp_ref):
        s, d = jax.nn.sigmoid(g_ref[...]), d_ref[...]
        dg_ref[...] = (d * p_ref[...] * s * (1.0 - s)).astype(BF16)
        dp_ref[...] = (d * s).astype(BF16)

    shp = jax.ShapeDtypeStruct(gl.shape, BF16)
    return _rows_call(body, "ple_bwd", [gl, ple, dout], [shp, shp])


def _loss_fwd(y, target):
    inv = 1.0 / y.shape[1]

    def body(y_ref, t_ref, d_ref, l_ref):
        @pl.when(pl.program_id(0) == 0)
        def _():
            l_ref[...] = jnp.zeros_like(l_ref)

        e = y_ref[...] - t_ref[...]
        d_ref[...] = e * inv
        part = jnp.sum(jnp.sum(e * e, axis=1, keepdims=True), axis=0, keepdims=True) * (0.5 * inv)
        l_ref[...] += jnp.broadcast_to(part, l_ref.shape)

    dy, acc = _rows_call(body, "loss_fwd", [y, target], [jax.ShapeDtypeStruct(y.shape, F32)],
                         [jax.ShapeDtypeStruct((SUBLANES, LANES), F32)])
    return acc[0, 0], dy


def rmsnorm(x, gain):
    y = x * lax.rsqrt(jnp.mean(x * x, axis=-1, keepdims=True) + NORM_EPS)
    return y * gain


def causal_depthwise_conv(u, w, bias):
    k_width, chans = w.shape
    out = lax.conv_general_dilated(u, w[:, None, :], window_strides=(1,), padding=[(k_width - 1, 0)],
                                   dimension_numbers=("NWC", "WIO", "NWC"), feature_group_count=chans)
    return out + bias


def ssd_chunked(x, dt, a, bm, cm):
    b, t, heads, _ = x.shape
    nc, cl = t // SSD_CHUNK, SSD_CHUNK
    g, hg = SSM_GROUPS, heads // SSM_GROUPS
    xs = (x * dt[..., None]).reshape(b, nc, cl, g, hg, SSM_HEAD_DIM)
    a_dt = (dt * a).reshape(b, nc, cl, g, hg).transpose(0, 1, 3, 4, 2)
    a_cs = jnp.cumsum(a_dt, axis=-1)
    bc = bm.reshape(b, nc, cl, g, SSM_STATE)
    cc = cm.reshape(b, nc, cl, g, SSM_STATE)
    causal = jnp.tril(jnp.ones((cl, cl), dtype=bool))
    seg = a_cs[..., :, None] - a_cs[..., None, :]
    lmat = jnp.exp(jnp.where(causal, seg, -jnp.inf))
    cb = jnp.einsum("bclgn,bcsgn->bcgls", cc, bc)
    y_diag = jnp.einsum("bcgls,bcghls,bcsghp->bclghp", cb, lmat, xs)
    decay = jnp.exp(a_cs[..., -1:] - a_cs)
    states = jnp.einsum("bclgn,bcghl,bclghp->bcghpn", bc, decay, xs)
    chunk_decay = jnp.exp(a_cs[..., -1])

    def step(carry, inp):
        st, dec = inp
        return carry * dec[..., None, None] + st, carry

    init = jnp.zeros((b, g, hg, SSM_HEAD_DIM, SSM_STATE), F32)
    _, prev = lax.scan(step, init, (jnp.moveaxis(states, 1, 0), jnp.moveaxis(chunk_decay, 1, 0)))
    prev = jnp.moveaxis(prev, 0, 1)
    y_off = jnp.einsum("bclgn,bcghpn,bcghl->bclghp", cc, prev, jnp.exp(a_cs))
    return (y_diag + y_off).reshape(b, t, heads, SSM_HEAD_DIM)


def mamba2_mixer(h, wt_in, conv_w, conv_b, dt_bias, a_log, d_skip, norm_w, w_out):
    t, d_model = h.shape
    d_inner = 2 * d_model
    heads = d_inner // SSM_HEAD_DIM
    gn = SSM_GROUPS * SSM_STATE
    conv_dim = d_inner + 2 * gn
    zx = lin_t(h, wt_in[:d_inner + conv_dim])
    dt_raw = lin_t(h, wt_in[d_inner + conv_dim:])
    return lin(ssm_core(zx, dt_raw, conv_w, conv_b, dt_bias, a_log, d_skip, norm_w), w_out)


def ssm_core_jnp(zx, dt_raw, conv_w, conv_b, dt_bias, a_log, d_skip, norm_w):
    t = zx.shape[0]
    d_inner = norm_w.shape[0]
    heads = d_inner // SSM_HEAD_DIM
    gn = SSM_GROUPS * SSM_STATE
    z = zx[:, :d_inner]
    xbc = zx[:, d_inner:]
    xbc = jax.nn.silu(causal_depthwise_conv(xbc[None], conv_w, conv_b))[0]
    xs = xbc[:, :d_inner]
    bm = xbc[:, d_inner:d_inner + gn].reshape(1, t, SSM_GROUPS, SSM_STATE)
    cm = xbc[:, d_inner + gn:].reshape(1, t, SSM_GROUPS, SSM_STATE)
    dt = jax.nn.softplus(dt_raw + dt_bias)[None]
    a = -jnp.exp(a_log)
    xh = xs.reshape(1, t, heads, SSM_HEAD_DIM)
    y = ssd_chunked(xh, dt, a, bm, cm)
    y = y + xh * d_skip[:, None]
    y = y.reshape(t, d_inner) * jax.nn.silu(z)
    return rmsnorm(y.reshape(t, SSM_GROUPS, -1), norm_w.reshape(SSM_GROUPS, -1)).reshape(t, d_inner)


def alibi_slopes(n_heads):
    return 2.0 ** (-8.0 * jnp.arange(1, n_heads + 1, dtype=F32) / n_heads)


def dilated_group_attention(q, k, v, window, dilation, slopes):
    b, t, nh, e = q.shape
    span = window // dilation
    blk = span
    lu = t // dilation
    nb = -(-lu // blk)
    lp = nb * blk

    def to_blocks(arr):
        arr = arr.reshape(b, lu, dilation, nh, e)
        arr = jnp.pad(arr, ((0, 0), (0, lp - lu), (0, 0), (0, 0), (0, 0)))
        return arr.reshape(b, nb, blk, dilation, nh, e)

    qb, kb, vb = to_blocks(q), to_blocks(k), to_blocks(v)
    pad_prev = ((0, 0), (1, 0), (0, 0), (0, 0), (0, 0), (0, 0))
    kcat = jnp.concatenate([jnp.pad(kb, pad_prev)[:, :nb], kb], axis=2)
    vcat = jnp.concatenate([jnp.pad(vb, pad_prev)[:, :nb], vb], axis=2)
    scores = jnp.einsum("bnqrhe,bnkrhe->bnrhqk", qb, kcat) * (1.0 / math.sqrt(e))
    qi = jnp.arange(blk)[:, None]
    ki = jnp.arange(2 * blk)[None, :]
    dist = qi + blk - ki
    in_band = (dist >= 0) & (dist <= span)
    key_u = jnp.arange(nb)[:, None] * blk - blk + jnp.arange(2 * blk)[None, :]
    valid = in_band[None] & (key_u >= 0)[:, None, :]
    bias = -slopes[:, None, None] * (dilation * dist).astype(F32)[None]
    logits = jnp.where(valid[None, :, None, None], scores + bias[None, None, None], -jnp.inf)
    lse = jax.nn.logsumexp(logits, axis=-1)
    probs = jnp.exp(logits - lse[..., None])
    out = jnp.einsum("bnrhqk,bnkrhe->bnqrhe", probs, vcat)
    out = out.reshape(b, lp, dilation, nh, e)[:, :lu].reshape(b, t, nh, e)
    lse = lse.transpose(0, 1, 4, 2, 3).reshape(b, lp, dilation, nh)[:, :lu].reshape(b, t, nh)
    return out, lse


def dilated_attention_mixer(h, wt_qkv, q_gain, k_gain, w_o):
    t, d_model = h.shape
    heads = d_model // ATT_HEAD_DIM
    ng = len(DIL_PATTERNS)
    return lin(attention_core(lin_t(h, wt_qkv, BF16), q_gain, k_gain), w_o)


def attention_core_jnp(qkv, q_gain, k_gain):
    t = qkv.shape[0]
    ng = len(DIL_PATTERNS)
    heads = qkv.shape[1] // (3 * ng * ATT_HEAD_DIM)
    qkv = qkv.astype(F32).reshape(1, t, ng, 3, heads, ATT_HEAD_DIM)
    q = rmsnorm(qkv[:, :, :, 0], q_gain)
    k = rmsnorm(qkv[:, :, :, 1], k_gain)
    v = qkv[:, :, :, 2]
    slopes = alibi_slopes(heads)
    outs, lses = [], []
    for g, (window, dilation) in enumerate(DIL_PATTERNS):
        o_g, l_g = dilated_group_attention(q[:, :, g], k[:, :, g], v[:, :, g], window, dilation, slopes)
        outs.append(o_g)
        lses.append(l_g)
    alpha = jax.nn.softmax(jnp.stack(lses), axis=0)
    o = jnp.einsum("gbth,gbthe->bthe", alpha, jnp.stack(outs))
    return o.reshape(t, heads * ATT_HEAD_DIM)


def local_step(w, x, p, target):
    depth = w['norm_mix'].shape[0]
    ssm_small = ('ssm_conv_w', 'ssm_conv_b', 'ssm_dt_bias', 'ssm_a_log', 'ssm_d_skip', 'ssm_norm_w')
    saved = []
    for i in range(depth):
        j = i // 2
        s = {'x': x}
        h = s['h'] = _rms_fwd(x, w['norm_mix'][i])
        if i % 2 == 0:
            wt_in = w['ssm_w_in'][j]
            n_main = wt_in.shape[0] - w['ssm_dt_bias'].shape[1]
            zx = _mm(h, wt_in[:n_main], tb=True, name="ssm_in_fwd")
            dt_raw = _mm(h, wt_in[n_main:], tb=True, name="ssm_dt_fwd")
            y, s['mix'] = _ssm_core_fwd(zx, dt_raw, *[w[n][j] for n in ssm_small])
            x = _mm(y, w['ssm_w_out'][j], add=x, name="ssm_out_fwd")
        else:
            qkv = _mm(h, w['att_w_qkv'][j], tb=True, out_dtype=BF16, name="att_qkv_fwd")
            y, s['mix'] = _attention_core_fwd(qkv, w['att_q_norm'][j], w['att_k_norm'][j])
            x = _mm(y, w['att_w_o'][j], add=x, name="att_o_fwd")
        s['y'], s['x1'] = y, x
        h2 = s['h2'] = _rms_fwd(x, w['norm_ffn'][i])
        gu = s['gu'] = _mm(h2, w['ffn_w_gu'][i], tb=True, out_dtype=BF16, name="ffn_gu_fwd")
        act = s['act'] = _swiglu_fwd(gu)
        x = s['x2'] = _mm(act, w['ffn_w_down'][i], add=x, name="ffn_down_fwd")
        gl = s['gl'] = _mm(x, w['ple_w_gate'][i], name="ple_gate_fwd")
        ple = s['ple'] = _mm(p[i], w['ple_w_proj'][i], tb=True, name="ple_proj_fwd")
        x = _ple_fwd(x, gl, ple)
        saved.append(s)
    loss, dx = _loss_fwd(x, target)

    g = {n: [None] * w[n].shape[0] for n in w}
    for i in reversed(range(depth)):
        j = i // 2
        s = saved[i]
        dgl, dple = _ple_bwd(s['gl'], s['ple'], dx)
        g['ple_w_proj'][i] = _mm(dple, p[i], ta=True, out_dtype=BF16, name="ple_proj_dw")
        g['ple_w_gate'][i] = _mm(s['x2'], dgl, ta=True, out_dtype=BF16, name="ple_gate_dw")
        dx = _mm(dgl, w['ple_w_gate'][i], tb=True, add=dx, name="ple_gate_da")
        dact = _mm(dx, w['ffn_w_down'][i], tb=True, out_dtype=BF16, name="ffn_down_da")
        g['ffn_w_down'][i] = _mm(s['act'], dx, ta=True, out_dtype=BF16, name="ffn_down_dw")
        dgu = _swiglu_bwd(s['gu'], dact)
        dh2 = _mm(dgu, w['ffn_w_gu'][i], out_dtype=BF16, name="ffn_gu_da")
        g['ffn_w_gu'][i] = _mm(dgu, s['h2'], ta=True, out_dtype=BF16, name="ffn_gu_dw")
        dx, g['norm_ffn'][i] = _rms_bwd(s['x1'], w['norm_ffn'][i], dh2, dx)
        if i % 2 == 0:
            wt_in = w['ssm_w_in'][j]
            n_main = wt_in.shape[0] - w['ssm_dt_bias'].shape[1]
            dyn = _mm(dx, w['ssm_w_out'][j], tb=True, out_dtype=BF16, name="ssm_out_da")
            g['ssm_w_out'][j] = _mm(s['y'], dx, ta=True, out_dtype=BF16, name="ssm_out_dw")
            dzx, d_dt, *small = _ssm_core_bwd(s['mix'], dyn)
            for n, v in zip(ssm_small, small):
                g[n][j] = v
            dh = _mm(d_dt, wt_in[n_main:], name="ssm_dt_da")
            dh = _mm(dzx, wt_in[:n_main], add=dh, out_dtype=BF16, name="ssm_in_da")
            g['ssm_w_in'][j] = jnp.concatenate([_mm(dzx, s['h'], ta=True, out_dtype=BF16, name="ssm_in_dw"),
                                                _mm(d_dt, s['h'], ta=True, out_dtype=BF16, name="ssm_dt_dw")], axis=0)
        else:
            do = _mm(dx, w['att_w_o'][j], tb=True, name="att_o_da")
            g['att_w_o'][j] = _mm(s['y'], dx, ta=True, out_dtype=BF16, name="att_o_dw")
            dqkv, g['att_q_norm'][j], g['att_k_norm'][j] = _attention_core_bwd(s['mix'], do)
            dh = _mm(dqkv, w['att_w_qkv'][j], out_dtype=BF16, name="att_qkv_da")
            g['att_w_qkv'][j] = _mm(dqkv, s['h'], ta=True, out_dtype=BF16, name="att_qkv_dw")
        dx, g['norm_mix'][i] = _rms_bwd(s['x'], w['norm_mix'][i], dh, dx)
    return loss, dx, {n: jnp.stack(v) for n, v in g.items()}


def _pack_plan(shapes, width):
    plan, off = [], 0
    for name in BIG:
        layers, r, c = shapes[name]
        if name in COL_SHARDED:
            r, c = c, r
        assert (r * c) % width == 0, (name, r, c)
        pr = r * c // width
        for layer in range(layers):
            plan.append((name, layer, r, c, pr, off))
            off += _round_up(pr, BF16_ROWS)
    return plan, off


def _small_plan(shapes):
    plan, off = [], 0
    for name in SMALL:
        n = math.prod(shapes[name])
        plan.append((name, n, off))
        off += n
    return plan, _round_up(off, SUBLANES * LANES)


def kernel(x, p, norm_mix, norm_ffn, ssm_w_in, ssm_conv_w, ssm_conv_b, ssm_dt_bias, ssm_a_log, ssm_d_skip, ssm_norm_w, ssm_w_out, att_w_qkv, att_q_norm, att_k_norm, att_w_o, ffn_w_gate, ffn_w_up, ffn_w_down, ple_w_proj, ple_w_gate, loss_target, m_norm_mix, m_norm_ffn, m_ssm_w_in, m_ssm_conv_w, m_ssm_conv_b, m_ssm_dt_bias, m_ssm_a_log, m_ssm_d_skip, m_ssm_norm_w, m_ssm_w_out, m_att_w_qkv, m_att_q_norm, m_att_k_norm, m_att_w_o, m_ffn_w_gate, m_ffn_w_up, m_ffn_w_down, m_ple_w_proj, m_ple_w_gate, v_norm_mix, v_norm_ffn, v_ssm_w_in, v_ssm_conv_w, v_ssm_conv_b, v_ssm_dt_bias, v_ssm_a_log, v_ssm_d_skip, v_ssm_norm_w, v_ssm_w_out, v_att_w_qkv, v_att_q_norm, v_att_k_norm, v_att_w_o, v_ffn_w_gate, v_ffn_w_up, v_ffn_w_down, v_ple_w_proj, v_ple_w_gate):
    given = dict(locals())
    w_in = {n: given[n] for n in WEIGHTS}
    m_in = {n: given["m_" + n] for n in WEIGHTS}
    v_in = {n: given["v_" + n] for n in WEIGHTS}
    width = x.shape[-1]

    plan, total = _pack_plan({n: w_in[n].shape for n in BIG}, width)
    pieces = []
    for name, layer, r, c, pr, off in plan:
        blk = w_in[name][layer]
        if name in COL_SHARDED:
            blk = blk.T
        blk = blk.astype(BF16).reshape(pr, width)
        pieces.append(jnp.pad(blk, ((0, _round_up(pr, BF16_ROWS) - pr), (0, 0))))
    gathered = all_gather_hbm(jnp.concatenate(pieces, axis=0), "gather_weights")
    full = {}
    for name, layer, r, c, pr, off in plan:
        full.setdefault(name, []).append(gathered[:, off:off + pr].reshape(N_DEV * r, c))
    full = {n: jnp.stack(v) for n, v in full.items()}

    conv_shard = w_in['ssm_conv_w']
    cl, cw, cs = conv_shard.shape
    conv_rows = _round_up(cl * cw * cs, SUBLANES * LANES) // LANES
    conv_vec = jnp.pad(conv_shard.reshape(-1), (0, conv_rows * LANES - cl * cw * cs)).reshape(conv_rows, LANES)
    conv_all, _ = all_gather_sum_small(conv_vec, "gather_conv_w")
    conv_full = conv_all.reshape(N_DEV, -1)[:, :cl * cw * cs].reshape(N_DEV, cl, cw, cs)
    conv_full = conv_full.transpose(1, 2, 0, 3).reshape(cl, cw, N_DEV * cs)

    wl = dict(full)
    for n in SMALL:
        wl[n] = w_in[n]
    wl['ssm_conv_w'] = conv_full
    hidden = wl['ffn_w_gate'].shape[1]
    wl['ffn_w_gu'] = jnp.concatenate([wl.pop('ffn_w_gate'), wl.pop('ffn_w_up')], axis=1)

    loss_local, gx, gw = local_step(wl, x[0], p[:, 0], loss_target[0])
    loss = lax.psum(loss_local, ("x", "y", "c"))
    gw['ffn_w_gate'], gw['ffn_w_up'] = gw['ffn_w_gu'][:, :hidden], gw['ffn_w_gu'][:, hidden:]

    gpieces = []
    for name, layer, r, c, pr, off in plan:
        g = gw[name][layer].reshape(N_DEV, pr, width)
        gpieces.append(jnp.pad(g, ((0, 0), (0, _round_up(pr, BF16_ROWS) - pr), (0, 0))))
    received = all_to_all_hbm(jnp.concatenate(gpieces, axis=1), "scatter_grads")
    gsum = sum_slots(received, "sum_grads")
    grads = {}
    for name, layer, r, c, pr, off in plan:
        g = gsum[off:off + pr].reshape(r, c)
        if name in COL_SHARDED:
            g = g.T
        grads.setdefault(name, []).append(g)
    grads = {n: jnp.stack(v) for n, v in grads.items()}

    splan, stotal = _small_plan({n: gw[n].shape for n in SMALL})
    svec = jnp.concatenate([gw[n].reshape(-1) for n, _, _ in splan])
    svec = jnp.pad(svec, (0, stotal - svec.shape[0])).reshape(stotal // LANES, LANES)
    _, ssum = all_gather_sum_small(svec, "sum_small_grads")
    ssum = ssum.reshape(-1)
    for name, n, off in splan:
        grads[name] = ssum[off:off + n].reshape(gw[name].shape)
    me = _me()
    grads['ssm_conv_w'] = lax.dynamic_slice_in_dim(grads['ssm_conv_w'], me * cs, cs, axis=2)

    delta, new_m, new_v = {}, {}, {}
    for name in BIG:
        shp = w_in[name].shape
        flat = lambda a: a.reshape(-1, shp[-1])
        d, nm, nv = adamw(flat(w_in[name]), flat(grads[name]), flat(m_in[name]), flat(v_in[name]), "adamw_" + name)
        delta[name], new_m[name], new_v[name] = d.reshape(shp), nm.reshape(shp), nv.reshape(shp)
    splan2, stotal2 = _small_plan({n: w_in[n].shape for n in SMALL})

    def pack_small(src):
        vec = jnp.concatenate([src[n].reshape(-1) for n, _, _ in splan2])
        return jnp.pad(vec, (0, stotal2 - vec.shape[0]), constant_values=1.0).reshape(stotal2 // LANES, LANES)

    sd, snm, snv = adamw(pack_small(w_in), pack_small(grads), pack_small(m_in), pack_small(v_in), "adamw_small")
    for name, n, off in splan2:
        shp = w_in[name].shape
        delta[name] = sd.reshape(-1)[off:off + n].reshape(shp)
        new_m[name] = snm.reshape(-1)[off:off + n].reshape(shp)
        new_v[name] = snv.reshape(-1)[off:off + n].reshape(shp)

    return (loss, gx[None], *[grads[n] for n in WEIGHTS], *[delta[n] for n in WEIGHTS],
            *[new_m[n] for n in WEIGHTS], *[new_v[n] for n in WEIGHTS])
```

```python
import functools
import math

import jax
import jax.numpy as jnp
from jax import lax
from jax.experimental import pallas as pl
from jax.experimental.pallas import tpu as pltpu

F32 = jnp.float32
BF16 = jnp.bfloat16
N_DEV = 8
MESH = pl.DeviceIdType.MESH

SSM_HEAD_DIM = 64
SSM_GROUPS = 4
SSM_STATE = 128
CONV_WIDTH = 4
SSD_CHUNK = 128
ATT_HEAD_DIM = 64
DIL_PATTERNS = ((128, 1), (512, 4), (2048, 16))
NORM_EPS = 1e-6
ADAM_LR = 0.001
ADAM_B1 = 0.9
ADAM_B2 = 0.999
ADAM_EPS = 1e-08
ADAM_WD = 0.01
ADAM_STEP = 10

BF16_ROWS = 16
LANES = 128
SUBLANES = 8

WEIGHTS = ['norm_mix', 'norm_ffn', 'ssm_w_in', 'ssm_conv_w', 'ssm_conv_b', 'ssm_dt_bias', 'ssm_a_log', 'ssm_d_skip',
           'ssm_norm_w', 'ssm_w_out', 'att_w_qkv', 'att_q_norm', 'att_k_norm', 'att_w_o', 'ffn_w_gate', 'ffn_w_up',
           'ffn_w_down', 'ple_w_proj', 'ple_w_gate']
COL_SHARDED = ('ssm_w_in', 'att_w_qkv', 'ffn_w_gate', 'ffn_w_up', 'ple_w_proj')
ROW_SHARDED = ('ssm_w_out', 'att_w_o', 'ffn_w_down', 'ple_w_gate')
BIG = COL_SHARDED + ROW_SHARDED
SMALL = ('norm_mix', 'norm_ffn', 'ssm_conv_w', 'ssm_conv_b', 'ssm_dt_bias', 'ssm_a_log', 'ssm_d_skip', 'ssm_norm_w',
         'att_q_norm', 'att_k_norm')


def _pick(n, cands):
    for c in cands:
        if n % c == 0:
            return c
    return n


def _round_up(n, m):
    return -(-n // m) * m


MM_TILES = (1024, 1408, 512, 256, 128)
MM_VMEM_BYTES = 48 * 1024 * 1024


def _mm(a, b, *, ta=False, tb=False, out_dtype=F32, add=None, name):
    k_dim, m_dim = (a.shape if ta else a.shape[::-1])
    n_dim = b.shape[0] if tb else b.shape[1]
    assert (b.shape[1] if tb else b.shape[0]) == k_dim, (a.shape, b.shape, ta, tb)
    tm = _pick(m_dim, MM_TILES)
    tn = _pick(n_dim, MM_TILES)
    tk = _pick(k_dim, MM_TILES)
    nk = k_dim // tk
    a_spec = pl.BlockSpec((tk, tm), lambda i, j, k: (k, i)) if ta else pl.BlockSpec((tm, tk), lambda i, j, k: (i, k))
    b_spec = pl.BlockSpec((tn, tk), lambda i, j, k: (j, k)) if tb else pl.BlockSpec((tk, tn), lambda i, j, k: (k, j))
    o_spec = pl.BlockSpec((tm, tn), lambda i, j, k: (i, j))
    dims = (((0 if ta else 1,), (1 if tb else 0,)), ((), ()))
    has_add = add is not None

    def body(*refs):
        a_ref, b_ref = refs[:2]
        o_ref = refs[2 + has_add]

        def dot():
            return lax.dot_general(a_ref[...].astype(BF16), b_ref[...].astype(BF16), dims,
                                   preferred_element_type=F32)

        def finish(acc):
            if has_add:
                acc = acc + refs[2][...].astype(F32)
            o_ref[...] = acc.astype(o_ref.dtype)

        if nk == 1:
            finish(dot())
            return
        acc_ref = refs[3 + has_add]
        k = pl.program_id(2)

        @pl.when(k == 0)
        def _():
            acc_ref[...] = dot()

        @pl.when((k > 0) & (k < nk - 1))
        def _():
            acc_ref[...] += dot()

        @pl.when(k == nk - 1)
        def _():
            finish(acc_ref[...] + dot())

    return pl.pallas_call(
        body, name=f"{name}_{m_dim}x{n_dim}x{k_dim}",
        out_shape=jax.ShapeDtypeStruct((m_dim, n_dim), out_dtype),
        grid=(m_dim // tm, n_dim // tn, nk),
        in_specs=[a_spec, b_spec] + ([o_spec] if has_add else []),
        out_specs=o_spec,
        scratch_shapes=[] if nk == 1 else [pltpu.VMEM((tm, tn), F32)],
        compiler_params=pltpu.CompilerParams(dimension_semantics=("parallel", "parallel", "arbitrary"),
                                             vmem_limit_bytes=MM_VMEM_BYTES),
    )(*((a, b) + ((add,) if has_add else ())))


def _me():
    return 4 * lax.axis_index("x") + 2 * lax.axis_index("y") + lax.axis_index("c")


def _peer(j):
    x, y, c = lax.axis_index("x"), lax.axis_index("y"), lax.axis_index("c")
    px = 1 - x if j & 4 else x
    py = 1 - y if j & 2 else y
    pc = 1 - c if j & 1 else c
    return (px, py, pc), 4 * px + 2 * py + pc


def _exchange_body(src_of, dst_ref, send_sems, recv_sems, local_sem):
    me = _me()
    mine = pltpu.make_async_copy(src_of(me), dst_ref.at[me], local_sem)
    mine.start()
    sends = []
    for j in range(1, N_DEV):
        peer, pidx = _peer(j)
        cp = pltpu.make_async_remote_copy(src_ref=src_of(pidx), dst_ref=dst_ref.at[me], send_sem=send_sems.at[j - 1],
                                          recv_sem=recv_sems.at[j - 1], device_id=peer, device_id_type=MESH)
        cp.start()
        sends.append(cp)
    for j in range(1, N_DEV):
        peer, pidx = _peer(j)
        pltpu.make_async_remote_copy(src_ref=src_of(pidx), dst_ref=dst_ref.at[pidx], send_sem=send_sems.at[j - 1],
                                     recv_sem=recv_sems.at[j - 1], device_id=peer, device_id_type=MESH).wait_recv()
    for cp in sends:
        cp.wait_send()
    mine.wait()


_EXCHANGE_SCRATCH = [pltpu.SemaphoreType.DMA((N_DEV - 1,)), pltpu.SemaphoreType.DMA((N_DEV - 1,)),
                     pltpu.SemaphoreType.DMA]


def all_gather_hbm(shard, name):
    def body(x_ref, out_ref, send_sems, recv_sems, local_sem):
        _exchange_body(lambda k: x_ref, out_ref, send_sems, recv_sems, local_sem)

    return pl.pallas_call(
        body, name=name,
        out_shape=jax.ShapeDtypeStruct((N_DEV,) + shard.shape, shard.dtype),
        in_specs=[pl.BlockSpec(memory_space=pl.ANY)],
        out_specs=pl.BlockSpec(memory_space=pl.ANY),
        scratch_shapes=list(_EXCHANGE_SCRATCH),
    )(shard)


def all_to_all_hbm(slots, name):
    def body(x_ref, out_ref, send_sems, recv_sems, local_sem):
        _exchange_body(lambda k: x_ref.at[k], out_ref, send_sems, recv_sems, local_sem)

    return pl.pallas_call(
        body, name=name,
        out_shape=jax.ShapeDtypeStruct(slots.shape, slots.dtype),
        in_specs=[pl.BlockSpec(memory_space=pl.ANY)],
        out_specs=pl.BlockSpec(memory_space=pl.ANY),
        scratch_shapes=list(_EXCHANGE_SCRATCH),
    )(slots)


_HBM = pl.BlockSpec(memory_space=pltpu.HBM)
_SEM = pl.BlockSpec(memory_space=pltpu.SEMAPHORE)


def _split_copies(src_ref, gather, land_ref, send_sems, recv_sems):
    me = _me()
    pairs = []
    for j in range(1, N_DEV):
        peer, pidx = _peer(j)

        def make(slot, peer=peer, pidx=pidx, j=j):
            return pltpu.make_async_remote_copy(
                src_ref=src_ref if gather else src_ref.at[pidx], dst_ref=land_ref.at[slot],
                send_sem=send_sems.at[j - 1], recv_sem=recv_sems.at[j - 1], device_id=peer, device_id_type=MESH)

        pairs.append((make(me), make(pidx)))
    return pairs


def exchange_start(src, gather, name):
    land_shape = ((N_DEV,) + src.shape) if gather else src.shape

    def body(src_ref, land_ref, send_sems, recv_sems, src_thru, land_thru, token):
        for send, _ in _split_copies(src_ref, gather, land_ref, send_sems, recv_sems):
            send.start()
        token[...] = jnp.zeros_like(token)

    sem = pltpu.SemaphoreType.DMA((N_DEV - 1,))
    send_sems, recv_sems, src_thru, land, token = pl.pallas_call(
        body, name=name,
        out_shape=(sem, sem, pltpu.HBM(src.shape, src.dtype), pltpu.HBM(land_shape, src.dtype),
                   jax.ShapeDtypeStruct((SUBLANES, LANES), F32)),
        in_specs=(_HBM, _HBM), out_specs=(_SEM, _SEM, _HBM, _HBM, pl.BlockSpec(memory_space=pltpu.VMEM)),
        input_output_aliases={0: 2, 1: 3},
        compiler_params=pltpu.CompilerParams(has_side_effects=pltpu.SideEffectType.DATAFLOW_SIDE_EFFECTING),
    )(pltpu.with_memory_space_constraint(src, pltpu.HBM),
      pltpu.with_memory_space_constraint(lax.empty(land_shape, src.dtype), pltpu.HBM))
    return (send_sems, recv_sems, src_thru, land), token


def exchange_wait(handle, after, gather, name):
    send_sems, recv_sems, src_thru, land = handle

    def body(src_ref, land_ref, send_sems, recv_sems, after_ref, src_dead, got_ref):
        for _, arrival in _split_copies(src_ref, gather, land_ref, send_sems, recv_sems):
            arrival.wait_send()
            arrival.wait_recv()

    src_done, got = pl.pallas_call(
        body, name=name,
        out_shape=(pltpu.HBM(src_thru.shape, src_thru.dtype), pltpu.HBM(land.shape, land.dtype)),
        in_specs=(_HBM, _HBM, _SEM, _SEM, pl.BlockSpec(memory_space=pl.ANY)), out_specs=(_HBM, _HBM),
        input_output_aliases={0: 0, 1: 1},
        compiler_params=pltpu.CompilerParams(has_side_effects=pltpu.SideEffectType.DATAFLOW_SIDE_EFFECTING),
    )(src_thru, land, send_sems, recv_sems, after)
    mine = src_done if gather else lax.dynamic_index_in_dim(src_done, _me(), 0, keepdims=False)
    return lax.dynamic_update_index_in_dim(got, mine, _me(), 0)


def all_gather_sum_small(v, name):
    def body(x_ref, out_ref, sum_ref, send_sems, recv_sems, local_sem):
        _exchange_body(lambda k: x_ref, out_ref, send_sems, recv_sems, local_sem)
        acc = out_ref[0]
        for k in range(1, N_DEV):
            acc = acc + out_ref[k]
        sum_ref[...] = acc

    return pl.pallas_call(
        body, name=name,
        out_shape=(jax.ShapeDtypeStruct((N_DEV,) + v.shape, v.dtype), jax.ShapeDtypeStruct(v.shape, v.dtype)),
        in_specs=[pl.BlockSpec(memory_space=pltpu.VMEM)],
        out_specs=(pl.BlockSpec(memory_space=pltpu.VMEM), pl.BlockSpec(memory_space=pltpu.VMEM)),
        scratch_shapes=list(_EXCHANGE_SCRATCH),
    )(v)


def sum_slots(slots, name):
    _, p_dim, c_dim = slots.shape
    tp = _pick(p_dim, (256, 128, 64, 32, 16))

    def body(x_ref, o_ref):
        acc = x_ref[0].astype(F32)
        for k in range(1, N_DEV):
            acc = acc + x_ref[k].astype(F32)
        o_ref[...] = acc

    return pl.pallas_call(
        body, name=name,
        out_shape=jax.ShapeDtypeStruct((p_dim, c_dim), F32),
        grid=(p_dim // tp,),
        in_specs=[pl.BlockSpec((N_DEV, tp, c_dim), lambda i: (0, i, 0))],
        out_specs=pl.BlockSpec((tp, c_dim), lambda i: (i, 0)),
        compiler_params=pltpu.CompilerParams(dimension_semantics=("parallel",)),
    )(slots)


def adamw(w, g, m, v, name):
    rows, cols = w.shape
    tr = _pick(rows, (256, 128, 64, 32, 16, 8))

    def body(w_ref, g_ref, m_ref, v_ref, d_ref, nm_ref, nv_ref):
        gv = g_ref[...]
        nm = ADAM_B1 * m_ref[...] + (1.0 - ADAM_B1) * gv
        nv = ADAM_B2 * v_ref[...] + (1.0 - ADAM_B2) * (gv * gv)
        m_hat = nm / (1.0 - ADAM_B1 ** ADAM_STEP)
        v_hat = nv / (1.0 - ADAM_B2 ** ADAM_STEP)
        d_ref[...] = -ADAM_LR * (m_hat / (jnp.sqrt(v_hat) + ADAM_EPS) + ADAM_WD * w_ref[...])
        nm_ref[...] = nm
        nv_ref[...] = nv

    spec = pl.BlockSpec((tr, cols), lambda i: (i, 0))
    shp = jax.ShapeDtypeStruct((rows, cols), F32)
    return pl.pallas_call(
        body, name=name, out_shape=(shp, shp, shp), grid=(rows // tr,),
        in_specs=[spec] * 4, out_specs=(spec,) * 3,
        compiler_params=pltpu.CompilerParams(dimension_semantics=("parallel",)),
    )(w, g, m, v)


ATT_BLK = 128
NEG = -1e30


def _head_sums(v):
    first = lax.broadcasted_iota(jnp.int32, (1, LANES), 1) < ATT_HEAD_DIM
    s0 = jnp.sum(jnp.where(first, v, 0.0), axis=-1, keepdims=True)
    s1 = jnp.sum(jnp.where(first, 0.0, v), axis=-1, keepdims=True)
    return jnp.where(first, s0, s1)


def _head_col(v, hmask):
    return jnp.max(jnp.where(hmask, v, -jnp.inf), axis=-1, keepdims=True)


def _qk_norm(raw, gain2):
    rstd = lax.rsqrt(_head_sums(raw * raw) * (1.0 / ATT_HEAD_DIM) + NORM_EPS)
    xhat = raw * rstd
    return xhat * gain2, xhat, rstd


def _qk_norm_bwd(dn, xhat, rstd, gain2):
    dxh = dn * gain2
    return rstd * (dxh - xhat * (_head_sums(dxh * xhat) * (1.0 / ATT_HEAD_DIM))), dn * xhat


def _att_mask_bias(n, dilation):
    qi = lax.broadcasted_iota(jnp.int32, (ATT_BLK, 2 * ATT_BLK), 0)
    ki = lax.broadcasted_iota(jnp.int32, (ATT_BLK, 2 * ATT_BLK), 1)
    dist = qi + ATT_BLK - ki
    valid = (dist >= 0) & (dist <= ATT_BLK) & ((n > 0) | (ki >= ATT_BLK))
    return valid, (dilation * dist).astype(F32)


ATT_PAIRS = 4


def _att_specs(g, dilation, hd, qkv_dim, nb, pp):
    cpb = qkv_dim // LANES
    base = g * 3 * hd // LANES
    hpn = hd // LANES
    assert cpb % pp == 0 and base % pp == 0 and hpn % pp == 0

    def spec(which, shift):
        def imap(r, hp, n):
            row = jnp.minimum(n, nb - 1) if shift == 0 else jnp.maximum(n - 1, 0)
            return (row, (r * cpb + base + which * hpn) // pp + hp)
        return pl.BlockSpec((ATT_BLK, pp * LANES), imap)

    return [spec(0, 0), spec(1, 1), spec(1, 0), spec(2, 1), spec(2, 0)]


def _att_group_fwd(qkv, gq2, gk2, slopes, g, dilation):
    t, qkv_dim = qkv.shape
    hd = qkv_dim // (3 * len(DIL_PATTERNS))
    lu = t // dilation
    nb = lu // ATT_BLK
    assert nb * ATT_BLK == lu and hd % LANES == 0
    hpn = hd // LANES
    pp = math.gcd(ATT_PAIRS, hpn)
    scale = 1.0 / math.sqrt(ATT_HEAD_DIM)

    def body(q_ref, kp_ref, kc_ref, vp_ref, vc_ref, gq_ref, gk_ref, sl_ref, o_ref, l_ref):
        n = pl.program_id(2)
        lane = lax.broadcasted_iota(jnp.int32, (1, LANES), 1)
        first = (lane // ATT_HEAD_DIM) == 0
        valid, dist = _att_mask_bias(n, dilation)
        for pair in range(pp):
            cols = slice(pair * LANES, (pair + 1) * LANES)
            qn, _, _ = _qk_norm(q_ref[:, cols].astype(F32), gq_ref[0:1, :])
            kn, _, _ = _qk_norm(jnp.concatenate([kp_ref[:, cols], kc_ref[:, cols]], axis=0).astype(F32),
                                gk_ref[0:1, :])
            kn16 = kn.astype(BF16)
            v16 = jnp.concatenate([vp_ref[:, cols], vc_ref[:, cols]], axis=0)
            outs, lses = [], []
            for hh in range(2):
                hmask = (lane // ATT_HEAD_DIM) == hh
                qh = jnp.where(hmask, qn, 0.0).astype(BF16)
                s = lax.dot_general(qh, kn16, (((1,), (1,)), ((), ())), preferred_element_type=F32) * scale
                slope = _head_col(sl_ref[pair, 0:1, :], hmask)
                logits = jnp.where(valid, s - slope * dist, NEG)
                mx = jnp.max(logits, axis=-1, keepdims=True)
                pexp = jnp.exp(logits - mx)
                den = jnp.sum(pexp, axis=-1, keepdims=True)
                outs.append(jnp.dot(pexp.astype(BF16), v16, preferred_element_type=F32) / den)
                lses.append(mx + jnp.log(den))
            o_ref[:, cols] = jnp.where(first, outs[0], outs[1])
            l_ref[:, cols] = jnp.where(first, lses[0], lses[1])

    out_spec = pl.BlockSpec((ATT_BLK, pp * LANES), lambda r, hp, n: (n, r * (hpn // pp) + hp))
    vec_spec = pl.BlockSpec((SUBLANES, LANES), lambda r, hp, n: (0, 0))
    shp = jax.ShapeDtypeStruct((lu, dilation * hd), F32)
    o, lse = pl.pallas_call(
        body, name=f"att_fwd_g{g}", out_shape=(shp, shp), grid=(dilation, hpn // pp, nb),
        in_specs=_att_specs(g, dilation, hd, qkv_dim, nb, pp) + [
            vec_spec, vec_spec, pl.BlockSpec((pp, SUBLANES, LANES), lambda r, hp, n: (hp, 0, 0))],
        out_specs=(out_spec, out_spec),
        compiler_params=pltpu.CompilerParams(dimension_semantics=("parallel", "parallel", "arbitrary")),
    )(*([qkv.reshape(lu, dilation * qkv_dim)] * 5), gq2, gk2, slopes)
    return o.reshape(t, hd), lse.reshape(t, hd)


def _att_merge(outs, lses):
    t, hd = outs[0].shape
    tr = _pick(t, (256, 128))
    ng = len(outs)

    def body(*refs):
        o_refs, l_refs, o_ref, lt_ref, o16_ref = refs[:ng], refs[ng:2 * ng], refs[2 * ng], refs[2 * ng + 1], refs[2 * ng + 2]
        ls = [r[...] for r in l_refs]
        mx = functools.reduce(jnp.maximum, ls)
        es = [jnp.exp(l - mx) for l in ls]
        den = functools.reduce(jnp.add, es)
        acc = es[0] * o_refs[0][...]
        for e, r in zip(es[1:], o_refs[1:]):
            acc = acc + e * r[...]
        o = acc / den
        o_ref[...] = o
        o16_ref[...] = o.astype(BF16)
        lt_ref[...] = mx + jnp.log(den)

    spec = pl.BlockSpec((tr, hd), lambda i: (i, 0))
    shp = jax.ShapeDtypeStruct((t, hd), F32)
    return pl.pallas_call(
        body, name="att_merge", out_shape=(shp, shp, jax.ShapeDtypeStruct((t, hd), BF16)), grid=(t // tr,),
        in_specs=[spec] * (2 * ng), out_specs=(spec, spec, spec),
        compiler_params=pltpu.CompilerParams(dimension_semantics=("parallel",)),
    )(*outs, *lses)


def _att_group_bwd(qkv, gq2, gk2, slopes, o, lse_tot, do, g, dilation):
    t, qkv_dim = qkv.shape
    hd = qkv_dim // (3 * len(DIL_PATTERNS))
    lu = t // dilation
    nb = lu // ATT_BLK
    hpn = hd // LANES
    pp = math.gcd(ATT_PAIRS, hpn)
    hbn = hpn // pp
    scale = 1.0 / math.sqrt(ATT_HEAD_DIM)

    def body(q_ref, kp_ref, kc_ref, vp_ref, vc_ref, gq_ref, gk_ref, sl_ref, o_ref, l_ref, do_ref,
             dq_ref, dk_ref, dv_ref, dgq_ref, dgk_ref, ck_ref, cv_ref):
        n = pl.program_id(2)
        lane = lax.broadcasted_iota(jnp.int32, (1, LANES), 1)
        gq, gk = gq_ref[0:1, :], gk_ref[0:1, :]

        @pl.when(n == 0)
        def _():
            ck_ref[...] = jnp.zeros_like(ck_ref)
            cv_ref[...] = jnp.zeros_like(cv_ref)
            dgq_ref[...] = jnp.zeros_like(dgq_ref)
            dgk_ref[...] = jnp.zeros_like(dgk_ref)

        valid, dist = _att_mask_bias(n, dilation)
        valid = valid & (n < nb)
        gq_sum = jnp.zeros((1, LANES), F32)
        gk_sum = jnp.zeros((1, LANES), F32)
        for pair in range(pp):
            cols = slice(pair * LANES, (pair + 1) * LANES)
            qn, qhat, qr = _qk_norm(q_ref[:, cols].astype(F32), gq)
            kn, khat, kr = _qk_norm(jnp.concatenate([kp_ref[:, cols], kc_ref[:, cols]], axis=0).astype(F32), gk)
            kn16 = kn.astype(BF16)
            v16 = jnp.concatenate([vp_ref[:, cols], vc_ref[:, cols]], axis=0)
            dov = do_ref[:, cols]
            do_o = dov * o_ref[:, cols]
            lse = l_ref[:, cols]
            dq_acc = jnp.zeros((ATT_BLK, LANES), F32)
            dk_acc = jnp.zeros((2 * ATT_BLK, LANES), F32)
            dv_acc = jnp.zeros((2 * ATT_BLK, LANES), F32)
            for hh in range(2):
                hmask = (lane // ATT_HEAD_DIM) == hh
                qh = jnp.where(hmask, qn, 0.0).astype(BF16)
                doh = jnp.where(hmask, dov, 0.0).astype(BF16)
                s = lax.dot_general(qh, kn16, (((1,), (1,)), ((), ())), preferred_element_type=F32) * scale
                slope = _head_col(sl_ref[pair, 0:1, :], hmask)
                pr = jnp.exp(jnp.where(valid, s - slope * dist - _head_col(lse, hmask), NEG))
                dp = lax.dot_general(doh, v16, (((1,), (1,)), ((), ())), preferred_element_type=F32)
                delta = jnp.sum(jnp.where(hmask, do_o, 0.0), axis=-1, keepdims=True)
                ds = (pr * (dp - delta) * scale).astype(BF16)
                dq_acc = dq_acc + jnp.where(hmask, jnp.dot(ds, kn16, preferred_element_type=F32), 0.0)
                dk_acc = dk_acc + lax.dot_general(ds, qh, (((0,), (0,)), ((), ())), preferred_element_type=F32)
                dv_acc = dv_acc + lax.dot_general(pr.astype(BF16), doh, (((0,), (0,)), ((), ())),
                                                  preferred_element_type=F32)

            dq_raw, gq_part = _qk_norm_bwd(dq_acc, qhat, qr, gq)
            gq_sum = gq_sum + jnp.sum(gq_part, axis=0, keepdims=True)

            @pl.when(n < nb)
            def _():
                dq_ref[:, cols] = dq_raw.astype(dq_ref.dtype)

            dk_prev = ck_ref[:, cols] + dk_acc[:ATT_BLK]
            dk_raw, gk_part = _qk_norm_bwd(dk_prev, khat[:ATT_BLK], kr[:ATT_BLK], gk)
            gk_sum = gk_sum + jnp.sum(gk_part, axis=0, keepdims=True)
            dk_ref[:, cols] = dk_raw.astype(dk_ref.dtype)
            dv_ref[:, cols] = (cv_ref[:, cols] + dv_acc[:ATT_BLK]).astype(dv_ref.dtype)
            ck_ref[:, cols] = dk_acc[ATT_BLK:]
            cv_ref[:, cols] = dv_acc[ATT_BLK:]

        @pl.when(n < nb)
        def _():
            dgq_ref[...] += jnp.broadcast_to(gq_sum, dgq_ref.shape)

        @pl.when(n > 0)
        def _():
            dgk_ref[...] += jnp.broadcast_to(gk_sum, dgk_ref.shape)

    width = pp * LANES
    q_out = pl.BlockSpec((ATT_BLK, width), lambda r, hp, n: (jnp.minimum(n, nb - 1), r * hbn + hp))
    kv_out = pl.BlockSpec((ATT_BLK, width), lambda r, hp, n: (jnp.maximum(n - 1, 0), r * hbn + hp))
    vec_spec = pl.BlockSpec((SUBLANES, LANES), lambda r, hp, n: (0, 0))
    g_out = pl.BlockSpec((None, SUBLANES, LANES), lambda r, hp, n: (r * hbn + hp, 0, 0))
    shp = jax.ShapeDtypeStruct((lu, dilation * hd), BF16)
    gshp = jax.ShapeDtypeStruct((dilation * hbn, SUBLANES, LANES), F32)
    view = lambda a: a.reshape(lu, dilation * hd)
    dq, dk, dv, dgq, dgk = pl.pallas_call(
        body, name=f"att_bwd_g{g}", out_shape=(shp, shp, shp, gshp, gshp), grid=(dilation, hbn, nb + 1),
        in_specs=_att_specs(g, dilation, hd, qkv_dim, nb, pp) + [
            vec_spec, vec_spec, pl.BlockSpec((pp, SUBLANES, LANES), lambda r, hp, n: (hp, 0, 0)), q_out, q_out, q_out],
        out_specs=(q_out, kv_out, kv_out, g_out, g_out),
        scratch_shapes=[pltpu.VMEM((ATT_BLK, width), F32), pltpu.VMEM((ATT_BLK, width), F32)],
        compiler_params=pltpu.CompilerParams(dimension_semantics=("parallel", "parallel", "arbitrary")),
    )(*([qkv.reshape(lu, dilation * qkv_dim)] * 5), gq2, gk2, slopes, view(o), view(lse_tot), view(do))
    return dq.reshape(t, hd), dk.reshape(t, hd), dv.reshape(t, hd), dgq, dgk


def _att_consts(q_gain, k_gain, hd):
    heads = hd // ATT_HEAD_DIM
    gq2 = jnp.broadcast_to(jnp.tile(q_gain, 2)[None], (SUBLANES, LANES))
    gk2 = jnp.broadcast_to(jnp.tile(k_gain, 2)[None], (SUBLANES, LANES))
    sl = 2.0 ** (-8.0 * jnp.arange(1, heads + 1, dtype=F32) / heads)
    slopes = jnp.broadcast_to(jnp.repeat(sl, ATT_HEAD_DIM).reshape(hd // LANES, 1, LANES), (hd // LANES, SUBLANES, LANES))
    return gq2, gk2, slopes


def _attention_core_fwd(qkv, q_gain, k_gain):
    hd = qkv.shape[1] // (3 * len(DIL_PATTERNS))
    gq2, gk2, slopes = _att_consts(q_gain, k_gain, hd)
    outs, lses = [], []
    for g, (_, dilation) in enumerate(DIL_PATTERNS):
        o_g, l_g = _att_group_fwd(qkv, gq2, gk2, slopes, g, dilation)
        outs.append(o_g)
        lses.append(l_g)
    o, lse_tot, o16 = _att_merge(outs, lses)
    return o16, (qkv, q_gain, k_gain, o, lse_tot)


def _attention_core_bwd(res, do):
    qkv, q_gain, k_gain, o, lse_tot = res
    hd = o.shape[1]
    gq2, gk2, slopes = _att_consts(q_gain, k_gain, hd)
    parts, dgq, dgk = [], 0.0, 0.0
    for g, (_, dilation) in enumerate(DIL_PATTERNS):
        dq, dk, dv, a, b = _att_group_bwd(qkv, gq2, gk2, slopes, o, lse_tot, do, g, dilation)
        parts += [dq, dk, dv]
        dgq = dgq + a[:, 0, :].reshape(-1, ATT_HEAD_DIM).sum(0)
        dgk = dgk + b[:, 0, :].reshape(-1, ATT_HEAD_DIM).sum(0)
    return jnp.concatenate(parts, axis=1), dgq, dgk


HALO = 8


def _silu(v):
    return v * jax.nn.sigmoid(v)


def _silu_grad(v):
    s = jax.nn.sigmoid(v)
    return s * (1.0 + v * (1.0 - s))


def _conv_fwd(zx, conv_w, conv_b, d_inner):
    t = zx.shape[0]
    conv_dim = conv_w.shape[1]
    cb = _pick(d_inner, (1024, 512, 256, 128))
    assert conv_dim % cb == 0
    tr = _pick(t, (256, 128))
    off = d_inner // cb

    def body(x_ref, h_ref, w_ref, b_ref, o_ref):
        i = pl.program_id(1)
        halo = jnp.where(i > 0, h_ref[...], 0.0)
        ext = jnp.concatenate([halo, x_ref[...]], axis=0)
        acc = jnp.broadcast_to(b_ref[...], (tr, cb))
        for k in range(CONV_WIDTH):
            s = CONV_WIDTH - 1 - k
            sh = ext if s == 0 else pltpu.roll(ext, shift=s, axis=0)
            acc = acc + w_ref[k:k + 1, :] * sh[HALO:HALO + tr]
        o_ref[...] = acc

    return pl.pallas_call(
        body, name="ssm_conv_fwd", out_shape=jax.ShapeDtypeStruct((t, conv_dim), F32),
        grid=(conv_dim // cb, t // tr),
        in_specs=[pl.BlockSpec((tr, cb), lambda j, i: (i, off + j)),
                  pl.BlockSpec((HALO, cb), lambda j, i: (jnp.maximum(i * (tr // HALO) - 1, 0), off + j)),
                  pl.BlockSpec((CONV_WIDTH, cb), lambda j, i: (0, j)),
                  pl.BlockSpec((1, cb), lambda j, i: (0, j))],
        out_specs=pl.BlockSpec((tr, cb), lambda j, i: (i, j)),
        compiler_params=pltpu.CompilerParams(dimension_semantics=("parallel", "parallel")),
    )(zx, zx, conv_w, conv_b.reshape(1, -1))


def _conv_bwd(zx, conv_w, dpre, dzx, d_inner):
    t, width = zx.shape
    conv_dim = conv_w.shape[1]
    cb = _pick(d_inner, (1024, 512, 256, 128))
    tr = _pick(t, (256, 128))
    off = d_inner // cb
    nr = t // tr

    def body(x_ref, h_ref, w_ref, d_ref, dn_ref, dzx_in, dx_ref, dw_ref, db_ref):
        i = pl.program_id(1)

        @pl.when(i == 0)
        def _():
            dw_ref[...] = jnp.zeros_like(dw_ref)
            db_ref[...] = jnp.zeros_like(db_ref)

        halo = jnp.where(i > 0, h_ref[...], 0.0)
        ext = jnp.concatenate([halo, x_ref[...]], axis=0)
        d = d_ref[...]
        dext = jnp.concatenate([d, jnp.where(i < nr - 1, dn_ref[...], 0.0)], axis=0)
        dx = jnp.zeros((tr, cb), F32)
        for k in range(CONV_WIDTH):
            s = CONV_WIDTH - 1 - k
            fut = dext if s == 0 else pltpu.roll(dext, shift=tr + HALO - s, axis=0)
            dx = dx + w_ref[k:k + 1, :] * fut[:tr]
            past = ext if s == 0 else pltpu.roll(ext, shift=s, axis=0)
            dw_ref[k:k + 1, :] += jnp.sum(d * past[HALO:HALO + tr], axis=0, keepdims=True)
        dx_ref[...] = dx.astype(dx_ref.dtype)
        db_ref[...] += jnp.sum(d, axis=0, keepdims=True)

    last_halo = t // HALO - 1
    return pl.pallas_call(
        body, name="ssm_conv_bwd",
        out_shape=(jax.ShapeDtypeStruct(dzx.shape, dzx.dtype), jax.ShapeDtypeStruct((CONV_WIDTH, conv_dim), F32),
                   jax.ShapeDtypeStruct((1, conv_dim), F32)),
        grid=(conv_dim // cb, nr),
        in_specs=[pl.BlockSpec((tr, cb), lambda j, i: (i, off + j)),
                  pl.BlockSpec((HALO, cb), lambda j, i: (jnp.maximum(i * (tr // HALO) - 1, 0), off + j)),
                  pl.BlockSpec((CONV_WIDTH, cb), lambda j, i: (0, j)),
                  pl.BlockSpec((tr, cb), lambda j, i: (i, j)),
                  pl.BlockSpec((HALO, cb), lambda j, i: (jnp.minimum((i + 1) * (tr // HALO), last_halo), j)),
                  pl.BlockSpec(memory_space=pl.ANY)],
        out_specs=(pl.BlockSpec((tr, cb), lambda j, i: (i, off + j)),
                   pl.BlockSpec((CONV_WIDTH, cb), lambda j, i: (0, j)),
                   pl.BlockSpec((1, cb), lambda j, i: (0, j))),
        input_output_aliases={5: 0},
        compiler_params=pltpu.CompilerParams(dimension_semantics=("parallel", "arbitrary")),
    )(zx, zx, conv_w, dpre, dpre, dzx)


def _eye(n):
    return lax.broadcasted_iota(jnp.int32, (n, n), 0) == lax.broadcasted_iota(jnp.int32, (n, n), 1)


def _row_to_col(row):
    n = row.shape[1]
    return jnp.sum(jnp.where(_eye(n), row, 0.0), axis=1, keepdims=True)


def _col_to_row(col):
    n = col.shape[0]
    return jnp.sum(jnp.where(_eye(n), col, 0.0), axis=0, keepdims=True)


def _pair_lanes(c0, c1):
    lane = lax.broadcasted_iota(jnp.int32, (1, LANES), 1)
    return jnp.where(lane < SSM_HEAD_DIM, c0, c1)


def _ssd_chunk_common(pre_x_ref, pre_b_ref, pre_c_ref, dtr_ref, bias_ref, alog_ref, cs_ref):
    cl = SSD_CHUNK
    hpg = dtr_ref.shape[0]
    x = _silu(pre_x_ref[...])
    b16 = _silu(pre_b_ref[...]).astype(BF16)
    c16 = _silu(pre_c_ref[...]).astype(BF16)
    dt = jax.nn.softplus(dtr_ref[...] + bias_ref[...])
    a = -jnp.exp(alog_ref[...])
    li = lax.broadcasted_iota(jnp.int32, (cl, cl), 0)
    si = lax.broadcasted_iota(jnp.int32, (cl, cl), 1)
    upper = (li <= si).astype(F32)
    cs_ref[0:hpg, :] = jnp.dot(dt * a, upper, precision=lax.Precision.HIGHEST, preferred_element_type=F32)
    cs_ref[hpg:2 * hpg, :] = dt
    g = lax.dot_general(c16, b16, (((1,), (1,)), ((), ())), preferred_element_type=F32)
    return x, b16, c16, dt, a, g, li >= si


def _ssd_fwd(pre, dtT, bias, alog, dskip_lanes, d_inner):
    t = pre.shape[0]
    cl = SSD_CHUNK
    nc = t // cl
    ng = SSM_GROUPS
    hpg = dtT.shape[1]
    gw = hpg * SSM_HEAD_DIM
    assert d_inner == ng * gw and hpg % 2 == 0
    bo = d_inner // SSM_STATE

    def body(px_ref, pb_ref, pc_ref, dtr_ref, bias_ref, alog_ref, dsk_ref, y_ref, st_ref, s_ref, cs_ref):
        c = pl.program_id(1)

        @pl.when(c == 0)
        def _():
            s_ref[...] = jnp.zeros_like(s_ref)

        x, b16, c16, dt, a, g, causal = _ssd_chunk_common(px_ref, pb_ref, pc_ref, dtr_ref, bias_ref, alog_ref, cs_ref)
        st_ref[...] = s_ref[...]
        yoff = lax.dot_general(c16, s_ref[...].astype(BF16), (((1,), (1,)), ((), ())), preferred_element_type=F32)
        xde_parts = []
        for j in range(hpg // 2):
            cols = slice(j * LANES, (j + 1) * LANES)
            xp = x[:, cols]
            dcol, ecol, ocol, ms = [], [], [], []
            for hh in range(2):
                h = 2 * j + hh
                cs_row = cs_ref[h:h + 1, :]
                cs_col = _row_to_col(cs_row)
                dcol.append(_row_to_col(cs_ref[hpg + h:hpg + h + 1, :]))
                ecol.append(jnp.exp(cs_ref[h:h + 1, cl - 1:cl] - cs_col))
                ocol.append(jnp.exp(cs_col))
                lm = jnp.where(causal, jnp.exp(jnp.minimum(cs_col - cs_row, 0.0)), 0.0)
                ms.append((g * lm).astype(BF16))
            xd = xp * _pair_lanes(dcol[0], dcol[1])
            xd16 = xd.astype(BF16)
            yd = _pair_lanes(1.0, 0.0) * jnp.dot(ms[0], xd16, preferred_element_type=F32) \
                + _pair_lanes(0.0, 1.0) * jnp.dot(ms[1], xd16, preferred_element_type=F32)
            y_ref[:, cols] = yd + yoff[:, cols] * _pair_lanes(ocol[0], ocol[1]) + xp * dsk_ref[0:1, cols]
            xde_parts.append((xd * _pair_lanes(ecol[0], ecol[1])).astype(BF16))
        new = lax.dot_general(jnp.concatenate(xde_parts, axis=1), b16, (((0,), (0,)), ((), ())),
                              preferred_element_type=F32)
        for h in range(hpg):
            rows = slice(h * SSM_HEAD_DIM, (h + 1) * SSM_HEAD_DIM)
            s_ref[rows, :] = s_ref[rows, :] * jnp.exp(cs_ref[h:h + 1, cl - 1:cl]) + new[rows, :]

    vec = lambda n: pl.BlockSpec((None, hpg, n), lambda gi, c: (gi, 0, 0))
    return pl.pallas_call(
        body, name="ssd_fwd",
        out_shape=(jax.ShapeDtypeStruct((t, d_inner), F32), jax.ShapeDtypeStruct((ng, nc, gw, SSM_STATE), F32)),
        grid=(ng, nc),
        in_specs=[pl.BlockSpec((cl, gw), lambda gi, c: (c, gi)),
                  pl.BlockSpec((cl, SSM_STATE), lambda gi, c: (c, bo + gi)),
                  pl.BlockSpec((cl, SSM_STATE), lambda gi, c: (c, bo + ng + gi)),
                  pl.BlockSpec((None, hpg, cl), lambda gi, c: (gi, 0, c)),
                  vec(1), vec(1),
                  pl.BlockSpec((1, gw), lambda gi, c: (0, gi))],
        out_specs=(pl.BlockSpec((cl, gw), lambda gi, c: (c, gi)),
                   pl.BlockSpec((None, None, gw, SSM_STATE), lambda gi, c: (gi, c, 0, 0))),
        scratch_shapes=[pltpu.VMEM((gw, SSM_STATE), F32), pltpu.VMEM((2 * hpg, cl), F32)],
        compiler_params=pltpu.CompilerParams(dimension_semantics=("parallel", "arbitrary")),
    )(pre, pre, pre, dtT, bias, alog, dskip_lanes)


def _ssd_bwd(pre, dtT, bias, alog, dskip_lanes, states, dy, d_inner):
    t, conv_dim = pre.shape
    cl = SSD_CHUNK
    nc = t // cl
    ng = SSM_GROUPS
    hpg = dtT.shape[1]
    gw = hpg * SSM_HEAD_DIM
    bo = d_inner // SSM_STATE

    def body(px_ref, pb_ref, pc_ref, dtr_ref, bias_ref, alog_ref, dsk_ref, st_ref, dy_ref,
             dx_ref, db_ref, dc_ref, ddt_ref, acc_ref, dsk_out, ds_ref, cs_ref, dcs_ref):
        c = pl.program_id(1)

        @pl.when(c == 0)
        def _():
            ds_ref[...] = jnp.zeros_like(ds_ref)
            acc_ref[...] = jnp.zeros_like(acc_ref)
            dsk_out[...] = jnp.zeros_like(dsk_out)

        x, b16, c16, dt, a, g, causal = _ssd_chunk_common(px_ref, pb_ref, pc_ref, dtr_ref, bias_ref, alog_ref, cs_ref)
        s_prev = st_ref[...]
        s16 = s_prev.astype(BF16)
        ds = ds_ref[...]
        ds16 = ds.astype(BF16)
        dyv = dy_ref[...]
        yoff = lax.dot_general(c16, s16, (((1,), (1,)), ((), ())), preferred_element_type=F32)
        bds = lax.dot_general(b16, ds16, (((1,), (1,)), ((), ())), preferred_element_type=F32)
        dg = jnp.zeros((cl, cl), F32)
        xde_parts, dye_parts = [], []
        lane = lax.broadcasted_iota(jnp.int32, (1, LANES), 1)
        for j in range(hpg // 2):
            cols = slice(j * LANES, (j + 1) * LANES)
            xp, dyp = x[:, cols], dyv[:, cols]
            dcol, ecol, ocol, lms = [], [], [], []
            for hh in range(2):
                h = 2 * j + hh
                cs_row = cs_ref[h:h + 1, :]
                cs_col = _row_to_col(cs_row)
                dcol.append(_row_to_col(cs_ref[hpg + h:hpg + h + 1, :]))
                ecol.append(jnp.exp(cs_ref[h:h + 1, cl - 1:cl] - cs_col))
                ocol.append(jnp.exp(cs_col))
                lms.append(jnp.where(causal, jnp.exp(jnp.minimum(cs_col - cs_row, 0.0)), 0.0))
            dlanes, elanes, olanes = _pair_lanes(*dcol), _pair_lanes(*ecol), _pair_lanes(*ocol)
            xd = xp * dlanes
            xd16 = xd.astype(BF16)
            xde = xd * elanes
            yoffp = yoff[:, cols] * olanes
            bdsp = bds[:, cols]
            dxd = bdsp * elanes
            for hh in range(2):
                h = 2 * j + hh
                hmask = (lane // SSM_HEAD_DIM) == hh
                dyh16 = jnp.where(hmask, dyp, 0.0).astype(BF16)
                m = g * lms[hh]
                dm = lax.dot_general(dyh16, xd16, (((1,), (1,)), ((), ())), preferred_element_type=F32)
                w = dm * m
                dg = dg + dm * lms[hh]
                dxd = dxd + lax.dot_general(m.astype(BF16), dyh16, (((0,), (0,)), ((), ())),
                                            preferred_element_type=F32)
                term = jnp.sum(jnp.where(hmask, xde * bdsp, 0.0), axis=1, keepdims=True)
                dcs_col = (jnp.sum(w, axis=1, keepdims=True)
                           + jnp.sum(jnp.where(hmask, dyp * yoffp, 0.0), axis=1, keepdims=True) - term)
                rows = slice(h * SSM_HEAD_DIM, (h + 1) * SSM_HEAD_DIM)
                dec = jnp.exp(cs_ref[h:h + 1, cl - 1:cl])
                tail = jnp.sum(term, axis=0, keepdims=True) + dec * jnp.sum(
                    jnp.sum(s_prev[rows, :] * ds[rows, :], axis=1, keepdims=True), axis=0, keepdims=True)
                last = lax.broadcasted_iota(jnp.int32, (1, cl), 1) == cl - 1
                dcs_ref[h:h + 1, :] = _col_to_row(dcs_col) - jnp.sum(w, axis=0, keepdims=True) + jnp.where(last, tail, 0.0)
                dcs_ref[hpg + h:hpg + h + 1, :] = _col_to_row(
                    jnp.sum(jnp.where(hmask, dxd * xp, 0.0), axis=1, keepdims=True))
            dx_act = dxd * dlanes + dyp * dsk_ref[0:1, cols]
            dx_ref[:, cols] = dx_act * _silu_grad(px_ref[:, cols])
            dsk_out[0:1, cols] += jnp.sum(dyp * xp, axis=0, keepdims=True)
            xde_parts.append(xde.astype(BF16))
            dye_parts.append((dyp * olanes).astype(BF16))
        xde16 = jnp.concatenate(xde_parts, axis=1)
        dye16 = jnp.concatenate(dye_parts, axis=1)
        dg16 = dg.astype(BF16)
        dc_act = jnp.dot(dg16, b16, preferred_element_type=F32) + jnp.dot(dye16, s16, preferred_element_type=F32)
        db_act = lax.dot_general(dg16, c16, (((0,), (0,)), ((), ())), preferred_element_type=F32) \
            + jnp.dot(xde16, ds16, preferred_element_type=F32)
        dc_ref[...] = dc_act * _silu_grad(pc_ref[...])
        db_ref[...] = db_act * _silu_grad(pb_ref[...])
        ds_new = lax.dot_general(dye16, c16, (((0,), (0,)), ((), ())), preferred_element_type=F32)
        for h in range(hpg):
            rows = slice(h * SSM_HEAD_DIM, (h + 1) * SSM_HEAD_DIM)
            ds_ref[rows, :] = ds[rows, :] * jnp.exp(cs_ref[h:h + 1, cl - 1:cl]) + ds_new[rows, :]
        li = lax.broadcasted_iota(jnp.int32, (cl, cl), 0)
        si = lax.broadcasted_iota(jnp.int32, (cl, cl), 1)
        d_adt = jnp.dot(dcs_ref[0:hpg, :], (li >= si).astype(F32), precision=lax.Precision.HIGHEST,
                        preferred_element_type=F32)
        ddt = d_adt * a + dcs_ref[hpg:2 * hpg, :]
        ddt_raw = ddt * jax.nn.sigmoid(dtr_ref[...] + bias_ref[...])
        ddt_ref[...] = ddt_raw
        acc_ref[0:hpg, :] += d_adt * dt
        acc_ref[hpg:2 * hpg, :] += ddt_raw

    rc = lambda c: nc - 1 - c
    vec = lambda n: pl.BlockSpec((None, hpg, n), lambda gi, c: (gi, 0, 0))
    x_spec = pl.BlockSpec((cl, gw), lambda gi, c: (rc(c), gi))
    b_spec = pl.BlockSpec((cl, SSM_STATE), lambda gi, c: (rc(c), bo + gi))
    c_spec = pl.BlockSpec((cl, SSM_STATE), lambda gi, c: (rc(c), bo + ng + gi))
    dt_spec = pl.BlockSpec((None, hpg, cl), lambda gi, c: (gi, 0, rc(c)))
    return pl.pallas_call(
        body, name="ssd_bwd",
        out_shape=(jax.ShapeDtypeStruct((t, d_inner), F32), jax.ShapeDtypeStruct((t, ng * SSM_STATE), F32),
                   jax.ShapeDtypeStruct((t, ng * SSM_STATE), F32), jax.ShapeDtypeStruct(dtT.shape, F32),
                   jax.ShapeDtypeStruct((ng, 2 * hpg, cl), F32), jax.ShapeDtypeStruct((1, d_inner), F32)),
        grid=(ng, nc),
        in_specs=[x_spec, b_spec, c_spec, dt_spec, vec(1), vec(1),
                  pl.BlockSpec((1, gw), lambda gi, c: (0, gi)),
                  pl.BlockSpec((None, None, gw, SSM_STATE), lambda gi, c: (gi, rc(c), 0, 0)),
                  x_spec],
        out_specs=(x_spec, pl.BlockSpec((cl, SSM_STATE), lambda gi, c: (rc(c), gi)),
                   pl.BlockSpec((cl, SSM_STATE), lambda gi, c: (rc(c), gi)), dt_spec,
                   pl.BlockSpec((None, 2 * hpg, cl), lambda gi, c: (gi, 0, 0)),
                   pl.BlockSpec((1, gw), lambda gi, c: (0, gi))),
        scratch_shapes=[pltpu.VMEM((gw, SSM_STATE), F32), pltpu.VMEM((2 * hpg, cl), F32),
                        pltpu.VMEM((2 * hpg, cl), F32)],
        compiler_params=pltpu.CompilerParams(dimension_semantics=("parallel", "arbitrary")),
    )(pre, pre, pre, dtT, bias, alog, dskip_lanes, states, dy)


def _gate_norm_fwd(y, zx, norm_w, d_inner):
    t = y.shape[0]
    tr = _pick(t, (256, 128))
    gs = d_inner // SSM_GROUPS

    def body(y_ref, z_ref, w_ref, o_ref):
        for gi in range(SSM_GROUPS):
            cols = slice(gi * gs, (gi + 1) * gs)
            v = y_ref[:, cols] * _silu(z_ref[:, cols])
            r = lax.rsqrt(jnp.mean(v * v, axis=-1, keepdims=True) + NORM_EPS)
            o_ref[:, cols] = (v * r * w_ref[0:1, cols]).astype(BF16)

    spec = pl.BlockSpec((tr, d_inner), lambda i: (i, 0))
    return pl.pallas_call(
        body, name="ssm_gate_norm_fwd", out_shape=jax.ShapeDtypeStruct((t, d_inner), BF16), grid=(t // tr,),
        in_specs=[spec, spec, pl.BlockSpec((1, d_inner), lambda i: (0, 0))], out_specs=spec,
        compiler_params=pltpu.CompilerParams(dimension_semantics=("parallel",)),
    )(y, zx, norm_w.reshape(1, -1))


def _gate_norm_bwd(y, zx, norm_w, dout, d_inner):
    t, width = zx.shape
    tr = _pick(t, (256, 128))
    gs = d_inner // SSM_GROUPS

    def body(y_ref, z_ref, w_ref, do_ref, dy_ref, dz_ref, dw_ref):
        @pl.when(pl.program_id(0) == 0)
        def _():
            dw_ref[...] = jnp.zeros_like(dw_ref)

        for gi in range(SSM_GROUPS):
            cols = slice(gi * gs, (gi + 1) * gs)
            yv, zv = y_ref[:, cols], z_ref[:, cols]
            sz = _silu(zv)
            v = yv * sz
            r = lax.rsqrt(jnp.mean(v * v, axis=-1, keepdims=True) + NORM_EPS)
            vhat = v * r
            dn = do_ref[:, cols].astype(F32)
            dw_ref[0:1, cols] += jnp.sum(dn * vhat, axis=0, keepdims=True)
            dvh = dn * w_ref[0:1, cols]
            dv = r * (dvh - vhat * jnp.mean(dvh * vhat, axis=-1, keepdims=True))
            dy_ref[:, cols] = dv * sz
            dz_ref[:, cols] = (dv * yv * _silu_grad(zv)).astype(dz_ref.dtype)

    spec = pl.BlockSpec((tr, d_inner), lambda i: (i, 0))
    wspec = pl.BlockSpec((1, d_inner), lambda i: (0, 0))
    return pl.pallas_call(
        body, name="ssm_gate_norm_bwd",
        out_shape=(jax.ShapeDtypeStruct((t, d_inner), F32), jax.ShapeDtypeStruct((t, width), zx.dtype),
                   jax.ShapeDtypeStruct((1, d_inner), F32)),
        grid=(t // tr,),
        in_specs=[spec, spec, wspec, spec], out_specs=(spec, spec, wspec),
        compiler_params=pltpu.CompilerParams(dimension_semantics=("arbitrary",)),
    )(y, zx, norm_w.reshape(1, -1), dout)


def _ssm_small(dt_raw, dt_bias, a_log, d_skip):
    heads = dt_raw.shape[1]
    hpg = heads // SSM_GROUPS
    dtT = dt_raw.T.reshape(SSM_GROUPS, hpg, -1)
    return (dtT, dt_bias.reshape(SSM_GROUPS, hpg, 1), a_log.reshape(SSM_GROUPS, hpg, 1),
            jnp.repeat(d_skip, SSM_HEAD_DIM).reshape(1, -1))


def _ssm_core_fwd(zx, dt_raw, conv_w, conv_b, dt_bias, a_log, d_skip, norm_w):
    d_inner = norm_w.shape[0]
    pre = _conv_fwd(zx, conv_w, conv_b, d_inner)
    dtT, bias, alog, dsk = _ssm_small(dt_raw, dt_bias, a_log, d_skip)
    y, states = _ssd_fwd(pre, dtT, bias, alog, dsk, d_inner)
    out = _gate_norm_fwd(y, zx, norm_w, d_inner)
    return out, (zx, dt_raw, conv_w, dt_bias, a_log, d_skip, norm_w, pre, y, states)


def _ssm_core_bwd(res, dout):
    zx, dt_raw, conv_w, dt_bias, a_log, d_skip, norm_w, pre, y, states = res
    d_inner = norm_w.shape[0]
    heads = dt_raw.shape[1]
    dy, dzx, dnorm = _gate_norm_bwd(y, zx, norm_w, dout, d_inner)
    dtT, bias, alog, dsk = _ssm_small(dt_raw, dt_bias, a_log, d_skip)
    dx, db, dc, ddtT, acc, dsk_l = _ssd_bwd(pre, dtT, bias, alog, dsk, states, dy, d_inner)
    dpre = jnp.concatenate([dx, db, dc], axis=1)
    dzx, dconv_w, dconv_b = _conv_bwd(zx, conv_w, dpre, dzx, d_inner)
    d_dt_raw = ddtT.reshape(heads, -1).T
    hpg = heads // SSM_GROUPS
    da = acc[:, :hpg].sum(-1).reshape(heads)
    d_bias = acc[:, hpg:].sum(-1).reshape(heads)
    d_alog = da * (-jnp.exp(a_log))
    d_dskip = dsk_l.reshape(heads, SSM_HEAD_DIM).sum(-1)
    return dzx, d_dt_raw, dconv_w, dconv_b.reshape(-1), d_bias, d_alog, d_dskip, dnorm.reshape(-1)


def _rows_call(body, name, ins, outs, acc_outs=(), rows=256):
    t = max(a.shape[0] for a in ins)
    tr = _pick(t, (rows, 128, 64, 32, 16, 8))

    def spec(a):
        if a.shape[0] == t:
            return pl.BlockSpec((tr, a.shape[1]), lambda i: (i, 0))
        return pl.BlockSpec(a.shape, lambda i: (0, 0))

    return pl.pallas_call(
        body, name=name, out_shape=tuple(outs) + tuple(acc_outs), grid=(t // tr,),
        in_specs=[spec(a) for a in ins],
        out_specs=tuple(spec(a) for a in outs) + tuple(pl.BlockSpec(a.shape, lambda i: (0, 0)) for a in acc_outs),
        compiler_params=pltpu.CompilerParams(dimension_semantics=("arbitrary" if acc_outs else "parallel",)),
    )(*ins)


def _rms_fwd(x, gain):
    def body(x_ref, g_ref, o_ref):
        v = x_ref[...]
        o_ref[...] = (v * lax.rsqrt(jnp.mean(v * v, axis=-1, keepdims=True) + NORM_EPS) * g_ref[...]).astype(BF16)

    (h,) = _rows_call(body, "rms_fwd", [x, gain.reshape(1, -1)], [jax.ShapeDtypeStruct(x.shape, BF16)])
    return h


def _rms_bwd(x, gain, dh, dres, after):
    def body(x_ref, g_ref, dh_ref, dr_ref, *rest):
        dx_ref, dg_ref = rest[-2:]

        @pl.when(pl.program_id(0) == 0)
        def _():
            dg_ref[...] = jnp.zeros_like(dg_ref)

        v = x_ref[...]
        r = lax.rsqrt(jnp.mean(v * v, axis=-1, keepdims=True) + NORM_EPS)
        vhat = v * r
        d = dh_ref[...].astype(F32)
        dg_ref[...] += jnp.sum(d * vhat, axis=0, keepdims=True)
        dvh = d * g_ref[...]
        dx_ref[...] = dr_ref[...] + r * (dvh - vhat * jnp.mean(dvh * vhat, axis=-1, keepdims=True))

    ins = [x, gain.reshape(1, -1), dh, dres] + ([] if after is None else [after])
    dx, dg = _rows_call(body, "rms_bwd", ins, [jax.ShapeDtypeStruct(x.shape, F32)],
                        [jax.ShapeDtypeStruct((1, x.shape[1]), F32)])
    return dx, dg.reshape(gain.shape)


def _swiglu_fwd(gu):
    t, f2 = gu.shape
    f = f2 // 2

    def body(gu_ref, o_ref):
        o_ref[...] = (_silu(gu_ref[:, :f].astype(F32)) * gu_ref[:, f:].astype(F32)).astype(BF16)

    (act,) = _rows_call(body, "swiglu_fwd", [gu], [jax.ShapeDtypeStruct((t, f), BF16)])
    return act


def _swiglu_bwd(gu, dact):
    t, f2 = gu.shape
    f = f2 // 2

    def body(gu_ref, d_ref, o_ref):
        g, u, d = gu_ref[:, :f].astype(F32), gu_ref[:, f:].astype(F32), d_ref[...].astype(F32)
        o_ref[:, :f] = (d * u * _silu_grad(g)).astype(BF16)
        o_ref[:, f:] = (d * _silu(g)).astype(BF16)

    (dgu,) = _rows_call(body, "swiglu_bwd", [gu, dact], [jax.ShapeDtypeStruct((t, f2), BF16)])
    return dgu


def _ple_fwd(x, gl, ple):
    def body(x_ref, g_ref, p_ref, o_ref):
        o_ref[...] = x_ref[...] + jax.nn.sigmoid(g_ref[...]) * p_ref[...]

    (out,) = _rows_call(body, "ple_fwd", [x, gl, ple], [jax.ShapeDtypeStruct(x.shape, F32)])
    return out


def _ple_bwd(gl, ple, dout):
    def body(g_ref, p_ref, d_ref, dg_ref, dp_ref):
        s, d = jax.nn.sigmoid(g_ref[...]), d_ref[...]
        dg_ref[...] = (d * p_ref[...] * s * (1.0 - s)).astype(BF16)
        dp_ref[...] = (d * s).astype(BF16)

    shp = jax.ShapeDtypeStruct(gl.shape, BF16)
    return _rows_call(body, "ple_bwd", [gl, ple, dout], [shp, shp])


def _loss_fwd(y, target):
    inv = 1.0 / y.shape[1]

    def body(y_ref, t_ref, d_ref, l_ref):
        @pl.when(pl.program_id(0) == 0)
        def _():
            l_ref[...] = jnp.zeros_like(l_ref)

        e = y_ref[...] - t_ref[...]
        d_ref[...] = e * inv
        part = jnp.sum(jnp.sum(e * e, axis=1, keepdims=True), axis=0, keepdims=True) * (0.5 * inv)
        l_ref[...] += jnp.broadcast_to(part, l_ref.shape)

    dy, acc = _rows_call(body, "loss_fwd", [y, target], [jax.ShapeDtypeStruct(y.shape, F32)],
                         [jax.ShapeDtypeStruct((SUBLANES, LANES), F32)])
    return acc[0, 0], dy


def rmsnorm(x, gain):
    y = x * lax.rsqrt(jnp.mean(x * x, axis=-1, keepdims=True) + NORM_EPS)
    return y * gain


def causal_depthwise_conv(u, w, bias):
    k_width, chans = w.shape
    out = lax.conv_general_dilated(u, w[:, None, :], window_strides=(1,), padding=[(k_width - 1, 0)],
                                   dimension_numbers=("NWC", "WIO", "NWC"), feature_group_count=chans)
    return out + bias


def ssd_chunked(x, dt, a, bm, cm):
    b, t, heads, _ = x.shape
    nc, cl = t // SSD_CHUNK, SSD_CHUNK
    g, hg = SSM_GROUPS, heads // SSM_GROUPS
    xs = (x * dt[..., None]).reshape(b, nc, cl, g, hg, SSM_HEAD_DIM)
    a_dt = (dt * a).reshape(b, nc, cl, g, hg).transpose(0, 1, 3, 4, 2)
    a_cs = jnp.cumsum(a_dt, axis=-1)
    bc = bm.reshape(b, nc, cl, g, SSM_STATE)
    cc = cm.reshape(b, nc, cl, g, SSM_STATE)
    causal = jnp.tril(jnp.ones((cl, cl), dtype=bool))
    seg = a_cs[..., :, None] - a_cs[..., None, :]
    lmat = jnp.exp(jnp.where(causal, seg, -jnp.inf))
    cb = jnp.einsum("bclgn,bcsgn->bcgls", cc, bc)
    y_diag = jnp.einsum("bcgls,bcghls,bcsghp->bclghp", cb, lmat, xs)
    decay = jnp.exp(a_cs[..., -1:] - a_cs)
    states = jnp.einsum("bclgn,bcghl,bclghp->bcghpn", bc, decay, xs)
    chunk_decay = jnp.exp(a_cs[..., -1])

    def step(carry, inp):
        st, dec = inp
        return carry * dec[..., None, None] + st, carry

    init = jnp.zeros((b, g, hg, SSM_HEAD_DIM, SSM_STATE), F32)
    _, prev = lax.scan(step, init, (jnp.moveaxis(states, 1, 0), jnp.moveaxis(chunk_decay, 1, 0)))
    prev = jnp.moveaxis(prev, 0, 1)
    y_off = jnp.einsum("bclgn,bcghpn,bcghl->bclghp", cc, prev, jnp.exp(a_cs))
    return (y_diag + y_off).reshape(b, t, heads, SSM_HEAD_DIM)


def mamba2_mixer(h, wt_in, conv_w, conv_b, dt_bias, a_log, d_skip, norm_w, w_out):
    t, d_model = h.shape
    d_inner = 2 * d_model
    heads = d_inner // SSM_HEAD_DIM
    gn = SSM_GROUPS * SSM_STATE
    conv_dim = d_inner + 2 * gn
    zx = lin_t(h, wt_in[:d_inner + conv_dim])
    dt_raw = lin_t(h, wt_in[d_inner + conv_dim:])
    return lin(ssm_core(zx, dt_raw, conv_w, conv_b, dt_bias, a_log, d_skip, norm_w), w_out)


def ssm_core_jnp(zx, dt_raw, conv_w, conv_b, dt_bias, a_log, d_skip, norm_w):
    t = zx.shape[0]
    d_inner = norm_w.shape[0]
    heads = d_inner // SSM_HEAD_DIM
    gn = SSM_GROUPS * SSM_STATE
    z = zx[:, :d_inner]
    xbc = zx[:, d_inner:]
    xbc = jax.nn.silu(causal_depthwise_conv(xbc[None], conv_w, conv_b))[0]
    xs = xbc[:, :d_inner]
    bm = xbc[:, d_inner:d_inner + gn].reshape(1, t, SSM_GROUPS, SSM_STATE)
    cm = xbc[:, d_inner + gn:].reshape(1, t, SSM_GROUPS, SSM_STATE)
    dt = jax.nn.softplus(dt_raw + dt_bias)[None]
    a = -jnp.exp(a_log)
    xh = xs.reshape(1, t, heads, SSM_HEAD_DIM)
    y = ssd_chunked(xh, dt, a, bm, cm)
    y = y + xh * d_skip[:, None]
    y = y.reshape(t, d_inner) * jax.nn.silu(z)
    return rmsnorm(y.reshape(t, SSM_GROUPS, -1), norm_w.reshape(SSM_GROUPS, -1)).reshape(t, d_inner)


def alibi_slopes(n_heads):
    return 2.0 ** (-8.0 * jnp.arange(1, n_heads + 1, dtype=F32) / n_heads)


def dilated_group_attention(q, k, v, window, dilation, slopes):
    b, t, nh, e = q.shape
    span = window // dilation
    blk = span
    lu = t // dilation
    nb = -(-lu // blk)
    lp = nb * blk

    def to_blocks(arr):
        arr = arr.reshape(b, lu, dilation, nh, e)
        arr = jnp.pad(arr, ((0, 0), (0, lp - lu), (0, 0), (0, 0), (0, 0)))
        return arr.reshape(b, nb, blk, dilation, nh, e)

    qb, kb, vb = to_blocks(q), to_blocks(k), to_blocks(v)
    pad_prev = ((0, 0), (1, 0), (0, 0), (0, 0), (0, 0), (0, 0))
    kcat = jnp.concatenate([jnp.pad(kb, pad_prev)[:, :nb], kb], axis=2)
    vcat = jnp.concatenate([jnp.pad(vb, pad_prev)[:, :nb], vb], axis=2)
    scores = jnp.einsum("bnqrhe,bnkrhe->bnrhqk", qb, kcat) * (1.0 / math.sqrt(e))
    qi = jnp.arange(blk)[:, None]
    ki = jnp.arange(2 * blk)[None, :]
    dist = qi + blk - ki
    in_band = (dist >= 0) & (dist <= span)
    key_u = jnp.arange(nb)[:, None] * blk - blk + jnp.arange(2 * blk)[None, :]
    valid = in_band[None] & (key_u >= 0)[:, None, :]
    bias = -slopes[:, None, None] * (dilation * dist).astype(F32)[None]
    logits = jnp.where(valid[None, :, None, None], scores + bias[None, None, None], -jnp.inf)
    lse = jax.nn.logsumexp(logits, axis=-1)
    probs = jnp.exp(logits - lse[..., None])
    out = jnp.einsum("bnrhqk,bnkrhe->bnqrhe", probs, vcat)
    out = out.reshape(b, lp, dilation, nh, e)[:, :lu].reshape(b, t, nh, e)
    lse = lse.transpose(0, 1, 4, 2, 3).reshape(b, lp, dilation, nh)[:, :lu].reshape(b, t, nh)
    return out, lse


def dilated_attention_mixer(h, wt_qkv, q_gain, k_gain, w_o):
    t, d_model = h.shape
    heads = d_model // ATT_HEAD_DIM
    ng = len(DIL_PATTERNS)
    return lin(attention_core(lin_t(h, wt_qkv, BF16), q_gain, k_gain), w_o)


def attention_core_jnp(qkv, q_gain, k_gain):
    t = qkv.shape[0]
    ng = len(DIL_PATTERNS)
    heads = qkv.shape[1] // (3 * ng * ATT_HEAD_DIM)
    qkv = qkv.astype(F32).reshape(1, t, ng, 3, heads, ATT_HEAD_DIM)
    q = rmsnorm(qkv[:, :, :, 0], q_gain)
    k = rmsnorm(qkv[:, :, :, 1], k_gain)
    v = qkv[:, :, :, 2]
    slopes = alibi_slopes(heads)
    outs, lses = [], []
    for g, (window, dilation) in enumerate(DIL_PATTERNS):
        o_g, l_g = dilated_group_attention(q[:, :, g], k[:, :, g], v[:, :, g], window, dilation, slopes)
        outs.append(o_g)
        lses.append(l_g)
    alpha = jax.nn.softmax(jnp.stack(lses), axis=0)
    o = jnp.einsum("gbth,gbthe->bthe", alpha, jnp.stack(outs))
    return o.reshape(t, heads * ATT_HEAD_DIM)


def local_step(small, fetch, emit, x, p, target):
    depth = small['norm_mix'].shape[0]
    ssm_small = ('ssm_conv_w', 'ssm_conv_b', 'ssm_dt_bias', 'ssm_a_log', 'ssm_d_skip', 'ssm_norm_w')
    saved = []
    for i in range(depth):
        j = i // 2
        s = {'x': x}
        wm = s['wm'] = fetch(2 * i, x)
        h = s['h'] = _rms_fwd(x, small['norm_mix'][i])
        if i % 2 == 0:
            n_main = wm['ssm_w_in'].shape[0] - small['ssm_dt_bias'].shape[1]
            zx = _mm(h, wm['ssm_w_in'][:n_main], tb=True, name="ssm_in_fwd")
            dt_raw = _mm(h, wm['ssm_w_in'][n_main:], tb=True, name="ssm_dt_fwd")
            y, s['mix'] = _ssm_core_fwd(zx, dt_raw, *[small[n][j] for n in ssm_small])
            x = _mm(y, wm['ssm_w_out'], add=x, name="ssm_out_fwd")
        else:
            qkv = _mm(h, wm['att_w_qkv'], tb=True, out_dtype=BF16, name="att_qkv_fwd")
            y, s['mix'] = _attention_core_fwd(qkv, small['att_q_norm'][j], small['att_k_norm'][j])
            x = _mm(y, wm['att_w_o'], add=x, name="att_o_fwd")
        s['y'], s['x1'] = y, x
        wf = s['wf'] = fetch(2 * i + 1, x)
        h2 = s['h2'] = _rms_fwd(x, small['norm_ffn'][i])
        gu = s['gu'] = _mm(h2, wf['ffn_w_gu'], tb=True, out_dtype=BF16, name="ffn_gu_fwd")
        act = s['act'] = _swiglu_fwd(gu)
        x = s['x2'] = _mm(act, wf['ffn_w_down'], add=x, name="ffn_down_fwd")
        gl = s['gl'] = _mm(x, wf['ple_w_gate'], name="ple_gate_fwd")
        ple = s['ple'] = _mm(p[i], wf['ple_w_proj'], tb=True, name="ple_proj_fwd")
        x = _ple_fwd(x, gl, ple)
        saved.append(s)
    loss, dx = _loss_fwd(x, target)

    g = {n: [None] * small[n].shape[0] for n in small}
    for i in reversed(range(depth)):
        j = i // 2
        s = saved[i]
        wm, wf = s['wm'], s['wf']
        gf = {}
        dgl, dple = _ple_bwd(s['gl'], s['ple'], dx)
        gf['ple_w_proj'] = _mm(dple, p[i], ta=True, out_dtype=BF16, name="ple_proj_dw")
        gf['ple_w_gate'] = _mm(s['x2'], dgl, ta=True, out_dtype=BF16, name="ple_gate_dw")
        dx = _mm(dgl, wf['ple_w_gate'], tb=True, add=dx, name="ple_gate_da")
        dact = _mm(dx, wf['ffn_w_down'], tb=True, out_dtype=BF16, name="ffn_down_da")
        gf['ffn_w_down'] = _mm(s['act'], dx, ta=True, out_dtype=BF16, name="ffn_down_dw")
        dgu = _swiglu_bwd(s['gu'], dact)
        dh2 = _mm(dgu, wf['ffn_w_gu'], out_dtype=BF16, name="ffn_gu_da")
        gf['ffn_w_gu'] = _mm(dgu, s['h2'], ta=True, out_dtype=BF16, name="ffn_gu_dw")
        dx, g['norm_ffn'][i] = _rms_bwd(s['x1'], small['norm_ffn'][i], dh2, dx, emit(2 * i + 1, gf))
        gm = {}
        if i % 2 == 0:
            n_main = wm['ssm_w_in'].shape[0] - small['ssm_dt_bias'].shape[1]
            dyn = _mm(dx, wm['ssm_w_out'], tb=True, out_dtype=BF16, name="ssm_out_da")
            gm['ssm_w_out'] = _mm(s['y'], dx, ta=True, out_dtype=BF16, name="ssm_out_dw")
            dzx, d_dt, *sg = _ssm_core_bwd(s['mix'], dyn)
            for n, v in zip(ssm_small, sg):
                g[n][j] = v
            dh = _mm(d_dt, wm['ssm_w_in'][n_main:], name="ssm_dt_da")
            dh = _mm(dzx, wm['ssm_w_in'][:n_main], add=dh, out_dtype=BF16, name="ssm_in_da")
            gm['ssm_w_in'] = jnp.concatenate([_mm(dzx, s['h'], ta=True, out_dtype=BF16, name="ssm_in_dw"),
                                              _mm(d_dt, s['h'], ta=True, out_dtype=BF16, name="ssm_dt_dw")], axis=0)
        else:
            do = _mm(dx, wm['att_w_o'], tb=True, name="att_o_da")
            gm['att_w_o'] = _mm(s['y'], dx, ta=True, out_dtype=BF16, name="att_o_dw")
            dqkv, g['att_q_norm'][j], g['att_k_norm'][j] = _attention_core_bwd(s['mix'], do)
            dh = _mm(dqkv, wm['att_w_qkv'], out_dtype=BF16, name="att_qkv_da")
            gm['att_w_qkv'] = _mm(dqkv, s['h'], ta=True, out_dtype=BF16, name="att_qkv_dw")
        dx, g['norm_mix'][i] = _rms_bwd(s['x'], small['norm_mix'][i], dh, dx, emit(2 * i, gm))
    return loss, dx, {n: jnp.stack(v) for n, v in g.items()}


MIXER_WEIGHTS = (('ssm_w_in', 'ssm_w_out'), ('att_w_qkv', 'att_w_o'))
CHANNEL_WEIGHTS = ('ffn_w_gate', 'ffn_w_up', 'ffn_w_down', 'ple_w_proj', 'ple_w_gate')


def _pack_plan(shapes, width, stage):
    layer = stage // 2
    if stage % 2:
        members = [(n, layer) for n in CHANNEL_WEIGHTS]
    else:
        members = [(n, layer // 2) for n in MIXER_WEIGHTS[layer % 2]]
    plan, off = [], 0
    for name, lyr in members:
        _, r, c = shapes[name]
        if name in COL_SHARDED:
            r, c = c, r
        assert (r * c) % width == 0, (name, r, c)
        pr = r * c // width
        plan.append((name, lyr, r, c, pr, off))
        off += _round_up(pr, BF16_ROWS)
    return plan, off


def _small_plan(shapes):
    plan, off = [], 0
    for name in SMALL:
        n = math.prod(shapes[name])
        plan.append((name, n, off))
        off += n
    return plan, _round_up(off, SUBLANES * LANES)


def kernel(x, p, norm_mix, norm_ffn, ssm_w_in, ssm_conv_w, ssm_conv_b, ssm_dt_bias, ssm_a_log, ssm_d_skip, ssm_norm_w, ssm_w_out, att_w_qkv, att_q_norm, att_k_norm, att_w_o, ffn_w_gate, ffn_w_up, ffn_w_down, ple_w_proj, ple_w_gate, loss_target, m_norm_mix, m_norm_ffn, m_ssm_w_in, m_ssm_conv_w, m_ssm_conv_b, m_ssm_dt_bias, m_ssm_a_log, m_ssm_d_skip, m_ssm_norm_w, m_ssm_w_out, m_att_w_qkv, m_att_q_norm, m_att_k_norm, m_att_w_o, m_ffn_w_gate, m_ffn_w_up, m_ffn_w_down, m_ple_w_proj, m_ple_w_gate, v_norm_mix, v_norm_ffn, v_ssm_w_in, v_ssm_conv_w, v_ssm_conv_b, v_ssm_dt_bias, v_ssm_a_log, v_ssm_d_skip, v_ssm_norm_w, v_ssm_w_out, v_att_w_qkv, v_att_q_norm, v_att_k_norm, v_att_w_o, v_ffn_w_gate, v_ffn_w_up, v_ffn_w_down, v_ple_w_proj, v_ple_w_gate):
    given = dict(locals())
    w_in = {n: given[n] for n in WEIGHTS}
    m_in = {n: given["m_" + n] for n in WEIGHTS}
    v_in = {n: given["v_" + n] for n in WEIGHTS}
    width = x.shape[-1]
    depth = norm_mix.shape[0]
    n_stages = 2 * depth

    plans = [_pack_plan({n: w_in[n].shape for n in BIG}, width, stage) for stage in range(n_stages)]
    gathers, tokens = [], []
    for stage, (plan, total) in enumerate(plans):
        pieces = []
        for name, layer, r, c, pr, off in plan:
            blk = w_in[name][layer]
            if name in COL_SHARDED:
                blk = blk.T
            blk = blk.astype(BF16).reshape(pr, width)
            pieces.append(jnp.pad(blk, ((0, _round_up(pr, BF16_ROWS) - pr), (0, 0))))
        handle, token = exchange_start(jnp.concatenate(pieces, axis=0), True, f"gather_start_{stage}")
        gathers.append(handle)
        tokens.append(token)
    all_started = functools.reduce(jnp.add, tokens)

    def fetch(stage, after):
        land = exchange_wait(gathers[stage], all_started if stage == 0 else after, True, f"gather_wait_{stage}")
        got = {name: land[:, off:off + pr].reshape(N_DEV * r, c) for name, layer, r, c, pr, off in plans[stage][0]}
        if 'ffn_w_gate' in got:
            got['ffn_w_gu'] = jnp.concatenate([got.pop('ffn_w_gate'), got.pop('ffn_w_up')], axis=0)
        return got

    scatters = [None] * n_stages

    def emit(stage, grads):
        grads = dict(grads)
        if 'ffn_w_gu' in grads:
            hidden = grads['ffn_w_gu'].shape[0] // 2
            grads['ffn_w_gate'], grads['ffn_w_up'] = grads['ffn_w_gu'][:hidden], grads['ffn_w_gu'][hidden:]
        pieces = []
        for name, layer, r, c, pr, off in plans[stage][0]:
            g = grads[name].reshape(N_DEV, pr, width)
            pieces.append(jnp.pad(g, ((0, 0), (0, _round_up(pr, BF16_ROWS) - pr), (0, 0))))
        scatters[stage], token = exchange_start(jnp.concatenate(pieces, axis=1), False, f"scatter_start_{stage}")
        return token

    conv_shard = w_in['ssm_conv_w']
    cl, cw, cs = conv_shard.shape
    conv_rows = _round_up(cl * cw * cs, SUBLANES * LANES) // LANES
    conv_vec = jnp.pad(conv_shard.reshape(-1), (0, conv_rows * LANES - cl * cw * cs)).reshape(conv_rows, LANES)
    conv_all, _ = all_gather_sum_small(conv_vec, "gather_conv_w")
    conv_full = conv_all.reshape(N_DEV, -1)[:, :cl * cw * cs].reshape(N_DEV, cl, cw, cs)
    conv_full = conv_full.transpose(1, 2, 0, 3).reshape(cl, cw, N_DEV * cs)

    small = {n: w_in[n] for n in SMALL}
    small['ssm_conv_w'] = conv_full
    loss_local, gx, gw = local_step(small, fetch, emit, x[0], p[:, 0], loss_target[0])
    loss = lax.psum(loss_local, ("x", "y", "c"))

    parts = {}
    for stage in reversed(range(n_stages)):
        received = exchange_wait(scatters[stage], gx, False, f"scatter_wait_{stage}")
        gsum = sum_slots(received, f"sum_grads_{stage}")
        for name, layer, r, c, pr, off in plans[stage][0]:
            g = gsum[off:off + pr].reshape(r, c)
            parts[name, layer] = g.T if name in COL_SHARDED else g
    grads = {n: jnp.stack([parts[n, layer] for layer in range(w_in[n].shape[0])]) for n in BIG}

    splan, stotal = _small_plan({n: gw[n].shape for n in SMALL})
    svec = jnp.concatenate([gw[n].reshape(-1) for n, _, _ in splan])
    svec = jnp.pad(svec, (0, stotal - svec.shape[0])).reshape(stotal // LANES, LANES)
    _, ssum = all_gather_sum_small(svec, "sum_small_grads")
    ssum = ssum.reshape(-1)
    for name, n, off in splan:
        grads[name] = ssum[off:off + n].reshape(gw[name].shape)
    me = _me()
    grads['ssm_conv_w'] = lax.dynamic_slice_in_dim(grads['ssm_conv_w'], me * cs, cs, axis=2)

    delta, new_m, new_v = {}, {}, {}
    for name in BIG:
        shp = w_in[name].shape
        flat = lambda a: a.reshape(-1, shp[-1])
        d, nm, nv = adamw(flat(w_in[name]), flat(grads[name]), flat(m_in[name]), flat(v_in[name]), "adamw_" + name)
        delta[name], new_m[name], new_v[name] = d.reshape(shp), nm.reshape(shp), nv.reshape(shp)
    splan2, stotal2 = _small_plan({n: w_in[n].shape for n in SMALL})

    def pack_small(src):
        vec = jnp.concatenate([src[n].reshape(-1) for n, _, _ in splan2])
        return jnp.pad(vec, (0, stotal2 - vec.shape[0]), constant_values=1.0).reshape(stotal2 // LANES, LANES)

    sd, snm, snv = adamw(pack_small(w_in), pack_small(grads), pack_small(m_in), pack_small(v_in), "adamw_small")
    for name, n, off in splan2:
        shp = w_in[name].shape
        delta[name] = sd.reshape(-1)[off:off + n].reshape(shp)
        new_m[name] = snm.reshape(-1)[off:off + n].reshape(shp)
        new_v[name] = snv.reshape(-1)[off:off + n].reshape(shp)

    return (loss, gx[None], *[grads[n] for n in WEIGHTS], *[delta[n] for n in WEIGHTS],
            *[new_m[n] for n in WEIGHTS], *[new_v[n] for n in WEIGHTS])
```

```python
import functools
import math

import jax
import jax.numpy as jnp
from jax import lax
from jax.experimental import pallas as pl
from jax.experimental.pallas import tpu as pltpu

F32 = jnp.float32
BF16 = jnp.bfloat16
N_DEV = 8
MESH = pl.DeviceIdType.MESH

SSM_HEAD_DIM = 64
SSM_GROUPS = 4
SSM_STATE = 128
CONV_WIDTH = 4
SSD_CHUNK = 128
ATT_HEAD_DIM = 64
DIL_PATTERNS = ((128, 1), (512, 4), (2048, 16))
NORM_EPS = 1e-6
ADAM_LR = 0.001
ADAM_B1 = 0.9
ADAM_B2 = 0.999
ADAM_EPS = 1e-08
ADAM_WD = 0.01
ADAM_STEP = 10

BF16_ROWS = 16
LANES = 128
SUBLANES = 8

WEIGHTS = ['norm_mix', 'norm_ffn', 'ssm_w_in', 'ssm_conv_w', 'ssm_conv_b', 'ssm_dt_bias', 'ssm_a_log', 'ssm_d_skip',
           'ssm_norm_w', 'ssm_w_out', 'att_w_qkv', 'att_q_norm', 'att_k_norm', 'att_w_o', 'ffn_w_gate', 'ffn_w_up',
           'ffn_w_down', 'ple_w_proj', 'ple_w_gate']
COL_SHARDED = ('ssm_w_in', 'att_w_qkv', 'ffn_w_gate', 'ffn_w_up', 'ple_w_proj')
ROW_SHARDED = ('ssm_w_out', 'att_w_o', 'ffn_w_down', 'ple_w_gate')
BIG = COL_SHARDED + ROW_SHARDED
SMALL = ('norm_mix', 'norm_ffn', 'ssm_conv_w', 'ssm_conv_b', 'ssm_dt_bias', 'ssm_a_log', 'ssm_d_skip', 'ssm_norm_w',
         'att_q_norm', 'att_k_norm')


def _pick(n, cands):
    for c in cands:
        if n % c == 0:
            return c
    return n


def _round_up(n, m):
    return -(-n // m) * m


MM_TILES = (1024, 1408, 512, 256, 128)
MM_VMEM_BYTES = 48 * 1024 * 1024


def _mm(a, b, *, ta=False, tb=False, out_dtype=F32, add=None, name):
    k_dim, m_dim = (a.shape if ta else a.shape[::-1])
    n_dim = b.shape[0] if tb else b.shape[1]
    assert (b.shape[1] if tb else b.shape[0]) == k_dim, (a.shape, b.shape, ta, tb)
    tm = _pick(m_dim, MM_TILES)
    tn = _pick(n_dim, MM_TILES)
    tk = _pick(k_dim, MM_TILES)
    nk = k_dim // tk
    a_spec = pl.BlockSpec((tk, tm), lambda i, j, k: (k, i)) if ta else pl.BlockSpec((tm, tk), lambda i, j, k: (i, k))
    b_spec = pl.BlockSpec((tn, tk), lambda i, j, k: (j, k)) if tb else pl.BlockSpec((tk, tn), lambda i, j, k: (k, j))
    o_spec = pl.BlockSpec((tm, tn), lambda i, j, k: (i, j))
    dims = (((0 if ta else 1,), (1 if tb else 0,)), ((), ()))
    has_add = add is not None

    def body(*refs):
        a_ref, b_ref = refs[:2]
        o_ref = refs[2 + has_add]

        def dot():
            return lax.dot_general(a_ref[...].astype(BF16), b_ref[...].astype(BF16), dims,
                                   preferred_element_type=F32)

        def finish(acc):
            if has_add:
                acc = acc + refs[2][...].astype(F32)
            o_ref[...] = acc.astype(o_ref.dtype)

        if nk == 1:
            finish(dot())
            return
        acc_ref = refs[3 + has_add]
        k = pl.program_id(2)

        @pl.when(k == 0)
        def _():
            acc_ref[...] = dot()

        @pl.when((k > 0) & (k < nk - 1))
        def _():
            acc_ref[...] += dot()

        @pl.when(k == nk - 1)
        def _():
            finish(acc_ref[...] + dot())

    return pl.pallas_call(
        body, name=f"{name}_{m_dim}x{n_dim}x{k_dim}",
        out_shape=jax.ShapeDtypeStruct((m_dim, n_dim), out_dtype),
        grid=(m_dim // tm, n_dim // tn, nk),
        in_specs=[a_spec, b_spec] + ([o_spec] if has_add else []),
        out_specs=o_spec,
        scratch_shapes=[] if nk == 1 else [pltpu.VMEM((tm, tn), F32)],
        compiler_params=pltpu.CompilerParams(dimension_semantics=("parallel", "parallel", "arbitrary"),
                                             vmem_limit_bytes=MM_VMEM_BYTES),
    )(*((a, b) + ((add,) if has_add else ())))


def _me():
    return 4 * lax.axis_index("x") + 2 * lax.axis_index("y") + lax.axis_index("c")


def _peer(j):
    x, y, c = lax.axis_index("x"), lax.axis_index("y"), lax.axis_index("c")
    px = 1 - x if j & 4 else x
    py = 1 - y if j & 2 else y
    pc = 1 - c if j & 1 else c
    return (px, py, pc), 4 * px + 2 * py + pc


def _exchange_body(src_of, dst_ref, send_sems, recv_sems, local_sem):
    me = _me()
    mine = pltpu.make_async_copy(src_of(me), dst_ref.at[me], local_sem)
    mine.start()
    sends = []
    for j in range(1, N_DEV):
        peer, pidx = _peer(j)
        cp = pltpu.make_async_remote_copy(src_ref=src_of(pidx), dst_ref=dst_ref.at[me], send_sem=send_sems.at[j - 1],
                                          recv_sem=recv_sems.at[j - 1], device_id=peer, device_id_type=MESH)
        cp.start()
        sends.append(cp)
    for j in range(1, N_DEV):
        peer, pidx = _peer(j)
        pltpu.make_async_remote_copy(src_ref=src_of(pidx), dst_ref=dst_ref.at[pidx], send_sem=send_sems.at[j - 1],
                                     recv_sem=recv_sems.at[j - 1], device_id=peer, device_id_type=MESH).wait_recv()
    for cp in sends:
        cp.wait_send()
    mine.wait()


_EXCHANGE_SCRATCH = [pltpu.SemaphoreType.DMA((N_DEV - 1,)), pltpu.SemaphoreType.DMA((N_DEV - 1,)),
                     pltpu.SemaphoreType.DMA]


def all_gather_hbm(shard, name):
    def body(x_ref, out_ref, send_sems, recv_sems, local_sem):
        _exchange_body(lambda k: x_ref, out_ref, send_sems, recv_sems, local_sem)

    return pl.pallas_call(
        body, name=name,
        out_shape=jax.ShapeDtypeStruct((N_DEV,) + shard.shape, shard.dtype),
        in_specs=[pl.BlockSpec(memory_space=pl.ANY)],
        out_specs=pl.BlockSpec(memory_space=pl.ANY),
        scratch_shapes=list(_EXCHANGE_SCRATCH),
    )(shard)


def all_to_all_hbm(slots, name):
    def body(x_ref, out_ref, send_sems, recv_sems, local_sem):
        _exchange_body(lambda k: x_ref.at[k], out_ref, send_sems, recv_sems, local_sem)

    return pl.pallas_call(
        body, name=name,
        out_shape=jax.ShapeDtypeStruct(slots.shape, slots.dtype),
        in_specs=[pl.BlockSpec(memory_space=pl.ANY)],
        out_specs=pl.BlockSpec(memory_space=pl.ANY),
        scratch_shapes=list(_EXCHANGE_SCRATCH),
    )(slots)


_HBM = pl.BlockSpec(memory_space=pltpu.HBM)
_SEM = pl.BlockSpec(memory_space=pltpu.SEMAPHORE)


def _split_copies(src_ref, gather, land_ref, send_sems, recv_sems):
    me = _me()
    pairs = []
    for j in range(1, N_DEV):
        peer, pidx = _peer(j)

        def make(slot, peer=peer, pidx=pidx, j=j):
            return pltpu.make_async_remote_copy(
                src_ref=src_ref if gather else src_ref.at[pidx], dst_ref=land_ref.at[slot],
                send_sem=send_sems.at[j - 1], recv_sem=recv_sems.at[j - 1], device_id=peer, device_id_type=MESH)

        pairs.append((make(me), make(pidx)))
    return pairs


def exchange_start(src, gather, name, after=None):
    land_shape = ((N_DEV,) + src.shape) if gather else src.shape
    has_after = after is not None

    def body(*refs):
        src_ref, land_ref = refs[:2]
        send_sems, recv_sems = refs[2 + has_after:4 + has_after]
        for send, _ in _split_copies(src_ref, gather, land_ref, send_sems, recv_sems):
            send.start()
        refs[-1][...] = jnp.zeros_like(refs[-1])

    sem = pltpu.SemaphoreType.DMA((N_DEV - 1,))
    send_sems, recv_sems, src_thru, land, token = pl.pallas_call(
        body, name=name,
        out_shape=(sem, sem, pltpu.HBM(src.shape, src.dtype), pltpu.HBM(land_shape, src.dtype),
                   jax.ShapeDtypeStruct((SUBLANES, LANES), F32)),
        in_specs=(_HBM, _HBM) + ((pl.BlockSpec(memory_space=pl.ANY),) if has_after else ()),
        out_specs=(_SEM, _SEM, _HBM, _HBM, pl.BlockSpec(memory_space=pltpu.VMEM)),
        input_output_aliases={0: 2, 1: 3},
        compiler_params=pltpu.CompilerParams(has_side_effects=pltpu.SideEffectType.DATAFLOW_SIDE_EFFECTING),
    )(pltpu.with_memory_space_constraint(src, pltpu.HBM),
      pltpu.with_memory_space_constraint(lax.empty(land_shape, src.dtype), pltpu.HBM),
      *((after,) if has_after else ()))
    return (send_sems, recv_sems, src_thru, land), token


def exchange_wait(handle, after, gather, name):
    send_sems, recv_sems, src_thru, land = handle

    def body(src_ref, land_ref, send_sems, recv_sems, after_ref, src_dead, got_ref):
        for _, arrival in _split_copies(src_ref, gather, land_ref, send_sems, recv_sems):
            arrival.wait_send()
            arrival.wait_recv()

    src_done, got = pl.pallas_call(
        body, name=name,
        out_shape=(pltpu.HBM(src_thru.shape, src_thru.dtype), pltpu.HBM(land.shape, land.dtype)),
        in_specs=(_HBM, _HBM, _SEM, _SEM, pl.BlockSpec(memory_space=pl.ANY)), out_specs=(_HBM, _HBM),
        input_output_aliases={0: 0, 1: 1},
        compiler_params=pltpu.CompilerParams(has_side_effects=pltpu.SideEffectType.DATAFLOW_SIDE_EFFECTING),
    )(src_thru, land, send_sems, recv_sems, after)
    mine = src_done if gather else lax.dynamic_index_in_dim(src_done, _me(), 0, keepdims=False)
    return lax.dynamic_update_index_in_dim(got, mine, _me(), 0)


def all_gather_sum_small(v, name):
    def body(x_ref, out_ref, sum_ref, send_sems, recv_sems, local_sem):
        _exchange_body(lambda k: x_ref, out_ref, send_sems, recv_sems, local_sem)
        acc = out_ref[0]
        for k in range(1, N_DEV):
            acc = acc + out_ref[k]
        sum_ref[...] = acc

    return pl.pallas_call(
        body, name=name,
        out_shape=(jax.ShapeDtypeStruct((N_DEV,) + v.shape, v.dtype), jax.ShapeDtypeStruct(v.shape, v.dtype)),
        in_specs=[pl.BlockSpec(memory_space=pltpu.VMEM)],
        out_specs=(pl.BlockSpec(memory_space=pltpu.VMEM), pl.BlockSpec(memory_space=pltpu.VMEM)),
        scratch_shapes=list(_EXCHANGE_SCRATCH),
    )(v)


def sum_slots(slots, name):
    _, p_dim, c_dim = slots.shape
    tp = _pick(p_dim, (256, 128, 64, 32, 16))

    def body(x_ref, o_ref):
        acc = x_ref[0].astype(F32)
        for k in range(1, N_DEV):
            acc = acc + x_ref[k].astype(F32)
        o_ref[...] = acc

    return pl.pallas_call(
        body, name=name,
        out_shape=jax.ShapeDtypeStruct((p_dim, c_dim), F32),
        grid=(p_dim // tp,),
        in_specs=[pl.BlockSpec((N_DEV, tp, c_dim), lambda i: (0, i, 0))],
        out_specs=pl.BlockSpec((tp, c_dim), lambda i: (i, 0)),
        compiler_params=pltpu.CompilerParams(dimension_semantics=("parallel",)),
    )(slots)


def adamw(w, g, m, v, name):
    rows, cols = w.shape
    tr = _pick(rows, (256, 128, 64, 32, 16, 8))

    def body(w_ref, g_ref, m_ref, v_ref, d_ref, nm_ref, nv_ref):
        gv = g_ref[...]
        nm = ADAM_B1 * m_ref[...] + (1.0 - ADAM_B1) * gv
        nv = ADAM_B2 * v_ref[...] + (1.0 - ADAM_B2) * (gv * gv)
        m_hat = nm / (1.0 - ADAM_B1 ** ADAM_STEP)
        v_hat = nv / (1.0 - ADAM_B2 ** ADAM_STEP)
        d_ref[...] = -ADAM_LR * (m_hat / (jnp.sqrt(v_hat) + ADAM_EPS) + ADAM_WD * w_ref[...])
        nm_ref[...] = nm
        nv_ref[...] = nv

    spec = pl.BlockSpec((tr, cols), lambda i: (i, 0))
    shp = jax.ShapeDtypeStruct((rows, cols), F32)
    return pl.pallas_call(
        body, name=name, out_shape=(shp, shp, shp), grid=(rows // tr,),
        in_specs=[spec] * 4, out_specs=(spec,) * 3,
        compiler_params=pltpu.CompilerParams(dimension_semantics=("parallel",)),
    )(w, g, m, v)


ATT_BLK = 128
NEG = -1e30


def _head_sums(v):
    first = lax.broadcasted_iota(jnp.int32, (1, LANES), 1) < ATT_HEAD_DIM
    s0 = jnp.sum(jnp.where(first, v, 0.0), axis=-1, keepdims=True)
    s1 = jnp.sum(jnp.where(first, 0.0, v), axis=-1, keepdims=True)
    return jnp.where(first, s0, s1)


def _head_col(v, hmask):
    return jnp.max(jnp.where(hmask, v, -jnp.inf), axis=-1, keepdims=True)


def _qk_norm(raw, gain2):
    rstd = lax.rsqrt(_head_sums(raw * raw) * (1.0 / ATT_HEAD_DIM) + NORM_EPS)
    xhat = raw * rstd
    return xhat * gain2, xhat, rstd


def _qk_norm_bwd(dn, xhat, rstd, gain2):
    dxh = dn * gain2
    return rstd * (dxh - xhat * (_head_sums(dxh * xhat) * (1.0 / ATT_HEAD_DIM))), dn * xhat


def _att_mask_bias(n, dilation):
    qi = lax.broadcasted_iota(jnp.int32, (ATT_BLK, 2 * ATT_BLK), 0)
    ki = lax.broadcasted_iota(jnp.int32, (ATT_BLK, 2 * ATT_BLK), 1)
    dist = qi + ATT_BLK - ki
    valid = (dist >= 0) & (dist <= ATT_BLK) & ((n > 0) | (ki >= ATT_BLK))
    return valid, (dilation * dist).astype(F32)


ATT_PAIRS = 4
RELAYOUT_ROWS = 512
RELAYOUT_COLS = 512


def _to_residues(x, dilation, col0=0, cols=None):
    t = x.shape[0]
    cols = x.shape[1] if cols is None else cols
    if dilation == 1 and col0 == 0 and cols == x.shape[1]:
        return x.reshape(1, t, cols)
    tr = _pick(t, (RELAYOUT_ROWS,))
    tc = _pick(cols, (RELAYOUT_COLS, 256, 128))
    per = tr // dilation
    assert tr % dilation == 0 and col0 % tc == 0

    def body(x_ref, o_ref, s_ref):
        for c in range(tc // LANES):
            lanes = slice(c * LANES, (c + 1) * LANES)
            s_ref[c] = x_ref[:, lanes].astype(F32)
            for r in range(dilation):
                o_ref[r, :, lanes] = s_ref[c, pl.ds(r, per, stride=dilation), :].astype(o_ref.dtype)

    return pl.pallas_call(
        body, name=f"to_residues_{dilation}", out_shape=jax.ShapeDtypeStruct((dilation, t // dilation, cols), x.dtype),
        grid=(t // tr, cols // tc),
        in_specs=[pl.BlockSpec((tr, tc), lambda i, j: (i, col0 // tc + j))],
        out_specs=pl.BlockSpec((dilation, per, tc), lambda i, j: (0, i, j)),
        scratch_shapes=[pltpu.VMEM((tc // LANES, tr, LANES), F32)],
        compiler_params=pltpu.CompilerParams(dimension_semantics=("parallel", "parallel")),
    )(x)


def _from_residues(y):
    dilation, lu, cols = y.shape
    t = dilation * lu
    if dilation == 1:
        return y.reshape(t, cols)
    tr = _pick(t, (RELAYOUT_ROWS,))
    tc = _pick(cols, (RELAYOUT_COLS, 256, 128))
    per = tr // dilation

    def body(y_ref, o_ref, s_ref):
        for c in range(tc // LANES):
            lanes = slice(c * LANES, (c + 1) * LANES)
            for r in range(dilation):
                s_ref[c, pl.ds(r, per, stride=dilation), :] = y_ref[r, :, lanes].astype(F32)
            o_ref[:, lanes] = s_ref[c].astype(o_ref.dtype)

    return pl.pallas_call(
        body, name=f"from_residues_{dilation}", out_shape=jax.ShapeDtypeStruct((t, cols), y.dtype),
        grid=(t // tr, cols // tc),
        in_specs=[pl.BlockSpec((dilation, per, tc), lambda i, j: (0, i, j))],
        out_specs=pl.BlockSpec((tr, tc), lambda i, j: (i, j)),
        scratch_shapes=[pltpu.VMEM((tc // LANES, tr, LANES), F32)],
        compiler_params=pltpu.CompilerParams(dimension_semantics=("parallel", "parallel")),
    )(y)


def _att_specs(base, hd, nb, pp):
    hpn = hd // LANES
    assert base % (pp * LANES) == 0 and hpn % pp == 0

    def spec(which, shift):
        def imap(r, hp, n):
            row = jnp.minimum(n, nb - 1) if shift == 0 else jnp.maximum(n - 1, 0)
            return (r, row, (base // LANES + which * hpn) // pp + hp)
        return pl.BlockSpec((None, ATT_BLK, pp * LANES), imap)

    return [spec(0, 0), spec(1, 1), spec(1, 0), spec(2, 1), spec(2, 0)]


def _att_group_fwd(qkv_r, base, hd, gq2, gk2, slopes, g):
    dilation, lu, _ = qkv_r.shape
    nb = lu // ATT_BLK
    assert nb * ATT_BLK == lu and hd % LANES == 0
    hpn = hd // LANES
    pp = math.gcd(ATT_PAIRS, hpn)
    scale = 1.0 / math.sqrt(ATT_HEAD_DIM)

    def body(q_ref, kp_ref, kc_ref, vp_ref, vc_ref, gq_ref, gk_ref, sl_ref, o_ref, l_ref):
        n = pl.program_id(2)
        lane = lax.broadcasted_iota(jnp.int32, (1, LANES), 1)
        first = (lane // ATT_HEAD_DIM) == 0
        valid, dist = _att_mask_bias(n, dilation)
        for pair in range(pp):
            cols = slice(pair * LANES, (pair + 1) * LANES)
            qn, _, _ = _qk_norm(q_ref[:, cols].astype(F32), gq_ref[0:1, :])
            kn, _, _ = _qk_norm(jnp.concatenate([kp_ref[:, cols], kc_ref[:, cols]], axis=0).astype(F32),
                                gk_ref[0:1, :])
            kn16 = kn.astype(BF16)
            v16 = jnp.concatenate([vp_ref[:, cols], vc_ref[:, cols]], axis=0)
            outs, lses = [], []
            for hh in range(2):
                hmask = (lane // ATT_HEAD_DIM) == hh
                qh = jnp.where(hmask, qn, 0.0).astype(BF16)
                s = lax.dot_general(qh, kn16, (((1,), (1,)), ((), ())), preferred_element_type=F32) * scale
                slope = _head_col(sl_ref[pair, 0:1, :], hmask)
                logits = jnp.where(valid, s - slope * dist, NEG)
                mx = jnp.max(logits, axis=-1, keepdims=True)
                pexp = jnp.exp(logits - mx)
                den = jnp.sum(pexp, axis=-1, keepdims=True)
                outs.append(jnp.dot(pexp.astype(BF16), v16, preferred_element_type=F32) / den)
                lses.append(mx + jnp.log(den))
            o_ref[:, cols] = jnp.where(first, outs[0], outs[1])
            l_ref[:, cols] = jnp.where(first, lses[0], lses[1])

    out_spec = pl.BlockSpec((None, ATT_BLK, pp * LANES), lambda r, hp, n: (r, n, hp))
    vec_spec = pl.BlockSpec((SUBLANES, LANES), lambda r, hp, n: (0, 0))
    shp = jax.ShapeDtypeStruct((dilation, lu, hd), F32)
    o, lse = pl.pallas_call(
        body, name=f"att_fwd_g{g}", out_shape=(shp, shp), grid=(dilation, hpn // pp, nb),
        in_specs=_att_specs(base, hd, nb, pp) + [
            vec_spec, vec_spec, pl.BlockSpec((pp, SUBLANES, LANES), lambda r, hp, n: (hp, 0, 0))],
        out_specs=(out_spec, out_spec),
        compiler_params=pltpu.CompilerParams(dimension_semantics=("parallel", "parallel", "arbitrary")),
    )(*([qkv_r] * 5), gq2, gk2, slopes)
    return _from_residues(o), _from_residues(lse)


def _att_merge(outs, lses):
    t, hd = outs[0].shape
    tr = _pick(t, (256, 128))
    ng = len(outs)

    def body(*refs):
        o_refs, l_refs, o_ref, lt_ref, o16_ref = refs[:ng], refs[ng:2 * ng], refs[2 * ng], refs[2 * ng + 1], refs[2 * ng + 2]
        ls = [r[...] for r in l_refs]
        mx = functools.reduce(jnp.maximum, ls)
        es = [jnp.exp(l - mx) for l in ls]
        den = functools.reduce(jnp.add, es)
        acc = es[0] * o_refs[0][...]
        for e, r in zip(es[1:], o_refs[1:]):
            acc = acc + e * r[...]
        o = acc / den
        o_ref[...] = o
        o16_ref[...] = o.astype(BF16)
        lt_ref[...] = mx + jnp.log(den)

    spec = pl.BlockSpec((tr, hd), lambda i: (i, 0))
    shp = jax.ShapeDtypeStruct((t, hd), F32)
    return pl.pallas_call(
        body, name="att_merge", out_shape=(shp, shp, jax.ShapeDtypeStruct((t, hd), BF16)), grid=(t // tr,),
        in_specs=[spec] * (2 * ng), out_specs=(spec, spec, spec),
        compiler_params=pltpu.CompilerParams(dimension_semantics=("parallel",)),
    )(*outs, *lses)


def _att_group_bwd(qkv_r, base, hd, gq2, gk2, slopes, o, lse_tot, do, g):
    dilation, lu, _ = qkv_r.shape
    nb = lu // ATT_BLK
    hpn = hd // LANES
    pp = math.gcd(ATT_PAIRS, hpn)
    hbn = hpn // pp
    scale = 1.0 / math.sqrt(ATT_HEAD_DIM)

    def body(q_ref, kp_ref, kc_ref, vp_ref, vc_ref, gq_ref, gk_ref, sl_ref, o_ref, l_ref, do_ref,
             dq_ref, dk_ref, dv_ref, dgq_ref, dgk_ref, ck_ref, cv_ref):
        n = pl.program_id(2)
        lane = lax.broadcasted_iota(jnp.int32, (1, LANES), 1)
        gq, gk = gq_ref[0:1, :], gk_ref[0:1, :]

        @pl.when(n == 0)
        def _():
            ck_ref[...] = jnp.zeros_like(ck_ref)
            cv_ref[...] = jnp.zeros_like(cv_ref)
            dgq_ref[...] = jnp.zeros_like(dgq_ref)
            dgk_ref[...] = jnp.zeros_like(dgk_ref)

        valid, dist = _att_mask_bias(n, dilation)
        valid = valid & (n < nb)
        gq_sum = jnp.zeros((1, LANES), F32)
        gk_sum = jnp.zeros((1, LANES), F32)
        for pair in range(pp):
            cols = slice(pair * LANES, (pair + 1) * LANES)
            qn, qhat, qr = _qk_norm(q_ref[:, cols].astype(F32), gq)
            kn, khat, kr = _qk_norm(jnp.concatenate([kp_ref[:, cols], kc_ref[:, cols]], axis=0).astype(F32), gk)
            kn16 = kn.astype(BF16)
            v16 = jnp.concatenate([vp_ref[:, cols], vc_ref[:, cols]], axis=0)
            dov = do_ref[:, cols]
            do_o = dov * o_ref[:, cols]
            lse = l_ref[:, cols]
            dq_acc = jnp.zeros((ATT_BLK, LANES), F32)
            dk_acc = jnp.zeros((2 * ATT_BLK, LANES), F32)
            dv_acc = jnp.zeros((2 * ATT_BLK, LANES), F32)
            for hh in range(2):
                hmask = (lane // ATT_HEAD_DIM) == hh
                qh = jnp.where(hmask, qn, 0.0).astype(BF16)
                doh = jnp.where(hmask, dov, 0.0).astype(BF16)
                s = lax.dot_general(qh, kn16, (((1,), (1,)), ((), ())), preferred_element_type=F32) * scale
                slope = _head_col(sl_ref[pair, 0:1, :], hmask)
                pr = jnp.exp(jnp.where(valid, s - slope * dist - _head_col(lse, hmask), NEG))
                dp = lax.dot_general(doh, v16, (((1,), (1,)), ((), ())), preferred_element_type=F32)
                delta = jnp.sum(jnp.where(hmask, do_o, 0.0), axis=-1, keepdims=True)
                ds = (pr * (dp - delta) * scale).astype(BF16)
                dq_acc = dq_acc + jnp.where(hmask, jnp.dot(ds, kn16, preferred_element_type=F32), 0.0)
                dk_acc = dk_acc + lax.dot_general(ds, qh, (((0,), (0,)), ((), ())), preferred_element_type=F32)
                dv_acc = dv_acc + lax.dot_general(pr.astype(BF16), doh, (((0,), (0,)), ((), ())),
                                                  preferred_element_type=F32)

            dq_raw, gq_part = _qk_norm_bwd(dq_acc, qhat, qr, gq)
            gq_sum = gq_sum + jnp.sum(gq_part, axis=0, keepdims=True)

            @pl.when(n < nb)
            def _():
                dq_ref[:, cols] = dq_raw.astype(dq_ref.dtype)

            dk_prev = ck_ref[:, cols] + dk_acc[:ATT_BLK]
            dk_raw, gk_part = _qk_norm_bwd(dk_prev, khat[:ATT_BLK], kr[:ATT_BLK], gk)
            gk_sum = gk_sum + jnp.sum(gk_part, axis=0, keepdims=True)
            dk_ref[:, cols] = dk_raw.astype(dk_ref.dtype)
            dv_ref[:, cols] = (cv_ref[:, cols] + dv_acc[:ATT_BLK]).astype(dv_ref.dtype)
            ck_ref[:, cols] = dk_acc[ATT_BLK:]
            cv_ref[:, cols] = dv_acc[ATT_BLK:]

        @pl.when(n < nb)
        def _():
            dgq_ref[...] += jnp.broadcast_to(gq_sum, dgq_ref.shape)

        @pl.when(n > 0)
        def _():
            dgk_ref[...] += jnp.broadcast_to(gk_sum, dgk_ref.shape)

    width = pp * LANES
    q_out = pl.BlockSpec((None, ATT_BLK, width), lambda r, hp, n: (r, jnp.minimum(n, nb - 1), hp))
    kv_out = pl.BlockSpec((None, ATT_BLK, width), lambda r, hp, n: (r, jnp.maximum(n - 1, 0), hp))
    vec_spec = pl.BlockSpec((SUBLANES, LANES), lambda r, hp, n: (0, 0))
    g_out = pl.BlockSpec((None, SUBLANES, LANES), lambda r, hp, n: (r * hbn + hp, 0, 0))
    shp = jax.ShapeDtypeStruct((dilation, lu, hd), BF16)
    gshp = jax.ShapeDtypeStruct((dilation * hbn, SUBLANES, LANES), F32)
    view = lambda a: _to_residues(a, dilation)
    dq, dk, dv, dgq, dgk = pl.pallas_call(
        body, name=f"att_bwd_g{g}", out_shape=(shp, shp, shp, gshp, gshp), grid=(dilation, hbn, nb + 1),
        in_specs=_att_specs(base, hd, nb, pp) + [
            vec_spec, vec_spec, pl.BlockSpec((pp, SUBLANES, LANES), lambda r, hp, n: (hp, 0, 0)), q_out, q_out, q_out],
        out_specs=(q_out, kv_out, kv_out, g_out, g_out),
        scratch_shapes=[pltpu.VMEM((ATT_BLK, width), F32), pltpu.VMEM((ATT_BLK, width), F32)],
        compiler_params=pltpu.CompilerParams(dimension_semantics=("parallel", "parallel", "arbitrary")),
    )(*([qkv_r] * 5), gq2, gk2, slopes, view(o), view(lse_tot), view(do))
    return _from_residues(dq), _from_residues(dk), _from_residues(dv), dgq, dgk


def _att_consts(q_gain, k_gain, hd):
    heads = hd // ATT_HEAD_DIM
    gq2 = jnp.broadcast_to(jnp.tile(q_gain, 2)[None], (SUBLANES, LANES))
    gk2 = jnp.broadcast_to(jnp.tile(k_gain, 2)[None], (SUBLANES, LANES))
    sl = 2.0 ** (-8.0 * jnp.arange(1, heads + 1, dtype=F32) / heads)
    slopes = jnp.broadcast_to(jnp.repeat(sl, ATT_HEAD_DIM).reshape(hd // LANES, 1, LANES), (hd // LANES, SUBLANES, LANES))
    return gq2, gk2, slopes


def _attention_core_fwd(qkv, q_gain, k_gain):
    hd = qkv.shape[1] // (3 * len(DIL_PATTERNS))
    gq2, gk2, slopes = _att_consts(q_gain, k_gain, hd)
    outs, lses, views = [], [], []
    for g, (_, dilation) in enumerate(DIL_PATTERNS):
        base = g * 3 * hd if dilation == 1 else 0
        qkv_r = _to_residues(qkv, dilation) if dilation == 1 else _to_residues(qkv, dilation, g * 3 * hd, 3 * hd)
        o_g, l_g = _att_group_fwd(qkv_r, base, hd, gq2, gk2, slopes, g)
        outs.append(o_g)
        lses.append(l_g)
        views.append((qkv_r, base))
    o, lse_tot, o16 = _att_merge(outs, lses)
    return o16, (views, q_gain, k_gain, o, lse_tot)


def _attention_core_bwd(res, do):
    views, q_gain, k_gain, o, lse_tot = res
    hd = o.shape[1]
    gq2, gk2, slopes = _att_consts(q_gain, k_gain, hd)
    parts, dgq, dgk = [], 0.0, 0.0
    for g, (qkv_r, base) in enumerate(views):
        dq, dk, dv, a, b = _att_group_bwd(qkv_r, base, hd, gq2, gk2, slopes, o, lse_tot, do, g)
        parts += [dq, dk, dv]
        dgq = dgq + a[:, 0, :].reshape(-1, ATT_HEAD_DIM).sum(0)
        dgk = dgk + b[:, 0, :].reshape(-1, ATT_HEAD_DIM).sum(0)
    return jnp.concatenate(parts, axis=1), dgq, dgk


HALO = 8


def _silu(v):
    return v * jax.nn.sigmoid(v)


def _silu_grad(v):
    s = jax.nn.sigmoid(v)
    return s * (1.0 + v * (1.0 - s))


def _conv_fwd(zx, conv_w, conv_b, d_inner):
    t = zx.shape[0]
    conv_dim = conv_w.shape[1]
    cb = _pick(d_inner, (1024, 512, 256, 128))
    assert conv_dim % cb == 0
    tr = _pick(t, (256, 128))
    off = d_inner // cb

    def body(x_ref, h_ref, w_ref, b_ref, o_ref):
        i = pl.program_id(1)
        halo = jnp.where(i > 0, h_ref[...], 0.0)
        ext = jnp.concatenate([halo, x_ref[...]], axis=0)
        acc = jnp.broadcast_to(b_ref[...], (tr, cb))
        for k in range(CONV_WIDTH):
            s = CONV_WIDTH - 1 - k
            sh = ext if s == 0 else pltpu.roll(ext, shift=s, axis=0)
            acc = acc + w_ref[k:k + 1, :] * sh[HALO:HALO + tr]
        o_ref[...] = acc

    return pl.pallas_call(
        body, name="ssm_conv_fwd", out_shape=jax.ShapeDtypeStruct((t, conv_dim), F32),
        grid=(conv_dim // cb, t // tr),
        in_specs=[pl.BlockSpec((tr, cb), lambda j, i: (i, off + j)),
                  pl.BlockSpec((HALO, cb), lambda j, i: (jnp.maximum(i * (tr // HALO) - 1, 0), off + j)),
                  pl.BlockSpec((CONV_WIDTH, cb), lambda j, i: (0, j)),
                  pl.BlockSpec((1, cb), lambda j, i: (0, j))],
        out_specs=pl.BlockSpec((tr, cb), lambda j, i: (i, j)),
        compiler_params=pltpu.CompilerParams(dimension_semantics=("parallel", "parallel")),
    )(zx, zx, conv_w, conv_b.reshape(1, -1))


def _conv_bwd(zx, conv_w, dpre, dzx, d_inner):
    t, width = zx.shape
    conv_dim = conv_w.shape[1]
    cb = _pick(d_inner, (1024, 512, 256, 128))
    tr = _pick(t, (256, 128))
    off = d_inner // cb
    nr = t // tr

    def body(x_ref, h_ref, w_ref, d_ref, dn_ref, dzx_in, dx_ref, dw_ref, db_ref):
        i = pl.program_id(1)

        @pl.when(i == 0)
        def _():
            dw_ref[...] = jnp.zeros_like(dw_ref)
            db_ref[...] = jnp.zeros_like(db_ref)

        halo = jnp.where(i > 0, h_ref[...], 0.0)
        ext = jnp.concatenate([halo, x_ref[...]], axis=0)
        d = d_ref[...]
        dext = jnp.concatenate([d, jnp.where(i < nr - 1, dn_ref[...], 0.0)], axis=0)
        dx = jnp.zeros((tr, cb), F32)
        for k in range(CONV_WIDTH):
            s = CONV_WIDTH - 1 - k
            fut = dext if s == 0 else pltpu.roll(dext, shift=tr + HALO - s, axis=0)
            dx = dx + w_ref[k:k + 1, :] * fut[:tr]
            past = ext if s == 0 else pltpu.roll(ext, shift=s, axis=0)
            dw_ref[k:k + 1, :] += jnp.sum(d * past[HALO:HALO + tr], axis=0, keepdims=True)
        dx_ref[...] = dx.astype(dx_ref.dtype)
        db_ref[...] += jnp.sum(d, axis=0, keepdims=True)

    last_halo = t // HALO - 1
    return pl.pallas_call(
        body, name="ssm_conv_bwd",
        out_shape=(jax.ShapeDtypeStruct(dzx.shape, dzx.dtype), jax.ShapeDtypeStruct((CONV_WIDTH, conv_dim), F32),
                   jax.ShapeDtypeStruct((1, conv_dim), F32)),
        grid=(conv_dim // cb, nr),
        in_specs=[pl.BlockSpec((tr, cb), lambda j, i: (i, off + j)),
                  pl.BlockSpec((HALO, cb), lambda j, i: (jnp.maximum(i * (tr // HALO) - 1, 0), off + j)),
                  pl.BlockSpec((CONV_WIDTH, cb), lambda j, i: (0, j)),
                  pl.BlockSpec((tr, cb), lambda j, i: (i, j)),
                  pl.BlockSpec((HALO, cb), lambda j, i: (jnp.minimum((i + 1) * (tr // HALO), last_halo), j)),
                  pl.BlockSpec(memory_space=pl.ANY)],
        out_specs=(pl.BlockSpec((tr, cb), lambda j, i: (i, off + j)),
                   pl.BlockSpec((CONV_WIDTH, cb), lambda j, i: (0, j)),
                   pl.BlockSpec((1, cb), lambda j, i: (0, j))),
        input_output_aliases={5: 0},
        compiler_params=pltpu.CompilerParams(dimension_semantics=("parallel", "arbitrary")),
    )(zx, zx, conv_w, dpre, dpre, dzx)


def _eye(n):
    return lax.broadcasted_iota(jnp.int32, (n, n), 0) == lax.broadcasted_iota(jnp.int32, (n, n), 1)


def _row_to_col(row):
    n = row.shape[1]
    return jnp.sum(jnp.where(_eye(n), row, 0.0), axis=1, keepdims=True)


def _col_to_row(col):
    n = col.shape[0]
    return jnp.sum(jnp.where(_eye(n), col, 0.0), axis=0, keepdims=True)


def _pair_lanes(c0, c1):
    lane = lax.broadcasted_iota(jnp.int32, (1, LANES), 1)
    return jnp.where(lane < SSM_HEAD_DIM, c0, c1)


def _ssd_chunk_common(pre_x_ref, pre_b_ref, pre_c_ref, dtr_ref, bias_ref, alog_ref, cs_ref):
    cl = SSD_CHUNK
    hpg = dtr_ref.shape[0]
    x = _silu(pre_x_ref[...])
    b16 = _silu(pre_b_ref[...]).astype(BF16)
    c16 = _silu(pre_c_ref[...]).astype(BF16)
    dt = jax.nn.softplus(dtr_ref[...] + bias_ref[...])
    a = -jnp.exp(alog_ref[...])
    li = lax.broadcasted_iota(jnp.int32, (cl, cl), 0)
    si = lax.broadcasted_iota(jnp.int32, (cl, cl), 1)
    upper = (li <= si).astype(F32)
    cs_ref[0:hpg, :] = jnp.dot(dt * a, upper, precision=lax.Precision.HIGHEST, preferred_element_type=F32)
    cs_ref[hpg:2 * hpg, :] = dt
    g = lax.dot_general(c16, b16, (((1,), (1,)), ((), ())), preferred_element_type=F32)
    return x, b16, c16, dt, a, g, li >= si


def _ssd_fwd(pre, dtT, bias, alog, dskip_lanes, d_inner):
    t = pre.shape[0]
    cl = SSD_CHUNK
    nc = t // cl
    ng = SSM_GROUPS
    hpg = dtT.shape[1]
    gw = hpg * SSM_HEAD_DIM
    assert d_inner == ng * gw and hpg % 2 == 0
    bo = d_inner // SSM_STATE

    def body(px_ref, pb_ref, pc_ref, dtr_ref, bias_ref, alog_ref, dsk_ref, y_ref, st_ref, s_ref, cs_ref):
        c = pl.program_id(1)

        @pl.when(c == 0)
        def _():
            s_ref[...] = jnp.zeros_like(s_ref)

        x, b16, c16, dt, a, g, causal = _ssd_chunk_common(px_ref, pb_ref, pc_ref, dtr_ref, bias_ref, alog_ref, cs_ref)
        st_ref[...] = s_ref[...]
        yoff = lax.dot_general(c16, s_ref[...].astype(BF16), (((1,), (1,)), ((), ())), preferred_element_type=F32)
        xde_parts = []
        for j in range(hpg // 2):
            cols = slice(j * LANES, (j + 1) * LANES)
            xp = x[:, cols]
            dcol, ecol, ocol, ms = [], [], [], []
            for hh in range(2):
                h = 2 * j + hh
                cs_row = cs_ref[h:h + 1, :]
                cs_col = _row_to_col(cs_row)
                dcol.append(_row_to_col(cs_ref[hpg + h:hpg + h + 1, :]))
                ecol.append(jnp.exp(cs_ref[h:h + 1, cl - 1:cl] - cs_col))
                ocol.append(jnp.exp(cs_col))
                lm = jnp.where(causal, jnp.exp(jnp.minimum(cs_col - cs_row, 0.0)), 0.0)
                ms.append((g * lm).astype(BF16))
            xd = xp * _pair_lanes(dcol[0], dcol[1])
            xd16 = xd.astype(BF16)
            yd = _pair_lanes(1.0, 0.0) * jnp.dot(ms[0], xd16, preferred_element_type=F32) \
                + _pair_lanes(0.0, 1.0) * jnp.dot(ms[1], xd16, preferred_element_type=F32)
            y_ref[:, cols] = yd + yoff[:, cols] * _pair_lanes(ocol[0], ocol[1]) + xp * dsk_ref[0:1, cols]
            xde_parts.append((xd * _pair_lanes(ecol[0], ecol[1])).astype(BF16))
        new = lax.dot_general(jnp.concatenate(xde_parts, axis=1), b16, (((0,), (0,)), ((), ())),
                              preferred_element_type=F32)
        for h in range(hpg):
            rows = slice(h * SSM_HEAD_DIM, (h + 1) * SSM_HEAD_DIM)
            s_ref[rows, :] = s_ref[rows, :] * jnp.exp(cs_ref[h:h + 1, cl - 1:cl]) + new[rows, :]

    vec = lambda n: pl.BlockSpec((None, hpg, n), lambda gi, c: (gi, 0, 0))
    return pl.pallas_call(
        body, name="ssd_fwd",
        out_shape=(jax.ShapeDtypeStruct((t, d_inner), F32), jax.ShapeDtypeStruct((ng, nc, gw, SSM_STATE), F32)),
        grid=(ng, nc),
        in_specs=[pl.BlockSpec((cl, gw), lambda gi, c: (c, gi)),
                  pl.BlockSpec((cl, SSM_STATE), lambda gi, c: (c, bo + gi)),
                  pl.BlockSpec((cl, SSM_STATE), lambda gi, c: (c, bo + ng + gi)),
                  pl.BlockSpec((None, hpg, cl), lambda gi, c: (gi, 0, c)),
                  vec(1), vec(1),
                  pl.BlockSpec((1, gw), lambda gi, c: (0, gi))],
        out_specs=(pl.BlockSpec((cl, gw), lambda gi, c: (c, gi)),
                   pl.BlockSpec((None, None, gw, SSM_STATE), lambda gi, c: (gi, c, 0, 0))),
        scratch_shapes=[pltpu.VMEM((gw, SSM_STATE), F32), pltpu.VMEM((2 * hpg, cl), F32)],
        compiler_params=pltpu.CompilerParams(dimension_semantics=("parallel", "arbitrary")),
    )(pre, pre, pre, dtT, bias, alog, dskip_lanes)


def _ssd_bwd(pre, dtT, bias, alog, dskip_lanes, states, dy, d_inner):
    t, conv_dim = pre.shape
    cl = SSD_CHUNK
    nc = t // cl
    ng = SSM_GROUPS
    hpg = dtT.shape[1]
    gw = hpg * SSM_HEAD_DIM
    bo = d_inner // SSM_STATE

    def body(px_ref, pb_ref, pc_ref, dtr_ref, bias_ref, alog_ref, dsk_ref, st_ref, dy_ref,
             dx_ref, db_ref, dc_ref, ddt_ref, acc_ref, dsk_out, ds_ref, cs_ref, dcs_ref):
        c = pl.program_id(1)

        @pl.when(c == 0)
        def _():
            ds_ref[...] = jnp.zeros_like(ds_ref)
            acc_ref[...] = jnp.zeros_like(acc_ref)
            dsk_out[...] = jnp.zeros_like(dsk_out)

        x, b16, c16, dt, a, g, causal = _ssd_chunk_common(px_ref, pb_ref, pc_ref, dtr_ref, bias_ref, alog_ref, cs_ref)
        s_prev = st_ref[...]
        s16 = s_prev.astype(BF16)
        ds = ds_ref[...]
        ds16 = ds.astype(BF16)
        dyv = dy_ref[...]
        yoff = lax.dot_general(c16, s16, (((1,), (1,)), ((), ())), preferred_element_type=F32)
        bds = lax.dot_general(b16, ds16, (((1,), (1,)), ((), ())), preferred_element_type=F32)
        dg = jnp.zeros((cl, cl), F32)
        xde_parts, dye_parts = [], []
        lane = lax.broadcasted_iota(jnp.int32, (1, LANES), 1)
        for j in range(hpg // 2):
            cols = slice(j * LANES, (j + 1) * LANES)
            xp, dyp = x[:, cols], dyv[:, cols]
            dcol, ecol, ocol, lms = [], [], [], []
            for hh in range(2):
                h = 2 * j + hh
                cs_row = cs_ref[h:h + 1, :]
                cs_col = _row_to_col(cs_row)
                dcol.append(_row_to_col(cs_ref[hpg + h:hpg + h + 1, :]))
                ecol.append(jnp.exp(cs_ref[h:h + 1, cl - 1:cl] - cs_col))
                ocol.append(jnp.exp(cs_col))
                lms.append(jnp.where(causal, jnp.exp(jnp.minimum(cs_col - cs_row, 0.0)), 0.0))
            dlanes, elanes, olanes = _pair_lanes(*dcol), _pair_lanes(*ecol), _pair_lanes(*ocol)
            xd = xp * dlanes
            xd16 = xd.astype(BF16)
            xde = xd * elanes
            yoffp = yoff[:, cols] * olanes
            bdsp = bds[:, cols]
            dxd = bdsp * elanes
            for hh in range(2):
                h = 2 * j + hh
                hmask = (lane // SSM_HEAD_DIM) == hh
                dyh16 = jnp.where(hmask, dyp, 0.0).astype(BF16)
                m = g * lms[hh]
                dm = lax.dot_general(dyh16, xd16, (((1,), (1,)), ((), ())), preferred_element_type=F32)
                w = dm * m
                dg = dg + dm * lms[hh]
                dxd = dxd + lax.dot_general(m.astype(BF16), dyh16, (((0,), (0,)), ((), ())),
                                            preferred_element_type=F32)
                term = jnp.sum(jnp.where(hmask, xde * bdsp, 0.0), axis=1, keepdims=True)
                dcs_col = (jnp.sum(w, axis=1, keepdims=True)
                           + jnp.sum(jnp.where(hmask, dyp * yoffp, 0.0), axis=1, keepdims=True) - term)
                rows = slice(h * SSM_HEAD_DIM, (h + 1) * SSM_HEAD_DIM)
                dec = jnp.exp(cs_ref[h:h + 1, cl - 1:cl])
                tail = jnp.sum(term, axis=0, keepdims=True) + dec * jnp.sum(
                    jnp.sum(s_prev[rows, :] * ds[rows, :], axis=1, keepdims=True), axis=0, keepdims=True)
                last = lax.broadcasted_iota(jnp.int32, (1, cl), 1) == cl - 1
                dcs_ref[h:h + 1, :] = _col_to_row(dcs_col) - jnp.sum(w, axis=0, keepdims=True) + jnp.where(last, tail, 0.0)
                dcs_ref[hpg + h:hpg + h + 1, :] = _col_to_row(
                    jnp.sum(jnp.where(hmask, dxd * xp, 0.0), axis=1, keepdims=True))
            dx_act = dxd * dlanes + dyp * dsk_ref[0:1, cols]
            dx_ref[:, cols] = dx_act * _silu_grad(px_ref[:, cols])
            dsk_out[0:1, cols] += jnp.sum(dyp * xp, axis=0, keepdims=True)
            xde_parts.append(xde.astype(BF16))
            dye_parts.append((dyp * olanes).astype(BF16))
        xde16 = jnp.concatenate(xde_parts, axis=1)
        dye16 = jnp.concatenate(dye_parts, axis=1)
        dg16 = dg.astype(BF16)
        dc_act = jnp.dot(dg16, b16, preferred_element_type=F32) + jnp.dot(dye16, s16, preferred_element_type=F32)
        db_act = lax.dot_general(dg16, c16, (((0,), (0,)), ((), ())), preferred_element_type=F32) \
            + jnp.dot(xde16, ds16, preferred_element_type=F32)
        dc_ref[...] = dc_act * _silu_grad(pc_ref[...])
        db_ref[...] = db_act * _silu_grad(pb_ref[...])
        ds_new = lax.dot_general(dye16, c16, (((0,), (0,)), ((), ())), preferred_element_type=F32)
        for h in range(hpg):
            rows = slice(h * SSM_HEAD_DIM, (h + 1) * SSM_HEAD_DIM)
            ds_ref[rows, :] = ds[rows, :] * jnp.exp(cs_ref[h:h + 1, cl - 1:cl]) + ds_new[rows, :]
        li = lax.broadcasted_iota(jnp.int32, (cl, cl), 0)
        si = lax.broadcasted_iota(jnp.int32, (cl, cl), 1)
        d_adt = jnp.dot(dcs_ref[0:hpg, :], (li >= si).astype(F32), precision=lax.Precision.HIGHEST,
                        preferred_element_type=F32)
        ddt = d_adt * a + dcs_ref[hpg:2 * hpg, :]
        ddt_raw = ddt * jax.nn.sigmoid(dtr_ref[...] + bias_ref[...])
        ddt_ref[...] = ddt_raw
        acc_ref[0:hpg, :] += d_adt * dt
        acc_ref[hpg:2 * hpg, :] += ddt_raw

    rc = lambda c: nc - 1 - c
    vec = lambda n: pl.BlockSpec((None, hpg, n), lambda gi, c: (gi, 0, 0))
    x_spec = pl.BlockSpec((cl, gw), lambda gi, c: (rc(c), gi))
    b_spec = pl.BlockSpec((cl, SSM_STATE), lambda gi, c: (rc(c), bo + gi))
    c_spec = pl.BlockSpec((cl, SSM_STATE), lambda gi, c: (rc(c), bo + ng + gi))
    dt_spec = pl.BlockSpec((None, hpg, cl), lambda gi, c: (gi, 0, rc(c)))
    return pl.pallas_call(
        body, name="ssd_bwd",
        out_shape=(jax.ShapeDtypeStruct((t, d_inner), F32), jax.ShapeDtypeStruct((t, ng * SSM_STATE), F32),
                   jax.ShapeDtypeStruct((t, ng * SSM_STATE), F32), jax.ShapeDtypeStruct(dtT.shape, F32),
                   jax.ShapeDtypeStruct((ng, 2 * hpg, cl), F32), jax.ShapeDtypeStruct((1, d_inner), F32)),
        grid=(ng, nc),
        in_specs=[x_spec, b_spec, c_spec, dt_spec, vec(1), vec(1),
                  pl.BlockSpec((1, gw), lambda gi, c: (0, gi)),
                  pl.BlockSpec((None, None, gw, SSM_STATE), lambda gi, c: (gi, rc(c), 0, 0)),
                  x_spec],
        out_specs=(x_spec, pl.BlockSpec((cl, SSM_STATE), lambda gi, c: (rc(c), gi)),
                   pl.BlockSpec((cl, SSM_STATE), lambda gi, c: (rc(c), gi)), dt_spec,
                   pl.BlockSpec((None, 2 * hpg, cl), lambda gi, c: (gi, 0, 0)),
                   pl.BlockSpec((1, gw), lambda gi, c: (0, gi))),
        scratch_shapes=[pltpu.VMEM((gw, SSM_STATE), F32), pltpu.VMEM((2 * hpg, cl), F32),
                        pltpu.VMEM((2 * hpg, cl), F32)],
        compiler_params=pltpu.CompilerParams(dimension_semantics=("parallel", "arbitrary")),
    )(pre, pre, pre, dtT, bias, alog, dskip_lanes, states, dy)


def _gate_norm_fwd(y, zx, norm_w, d_inner):
    t = y.shape[0]
    tr = _pick(t, (256, 128))
    gs = d_inner // SSM_GROUPS

    def body(y_ref, z_ref, w_ref, o_ref):
        for gi in range(SSM_GROUPS):
            cols = slice(gi * gs, (gi + 1) * gs)
            v = y_ref[:, cols] * _silu(z_ref[:, cols])
            r = lax.rsqrt(jnp.mean(v * v, axis=-1, keepdims=True) + NORM_EPS)
            o_ref[:, cols] = (v * r * w_ref[0:1, cols]).astype(BF16)

    spec = pl.BlockSpec((tr, d_inner), lambda i: (i, 0))
    return pl.pallas_call(
        body, name="ssm_gate_norm_fwd", out_shape=jax.ShapeDtypeStruct((t, d_inner), BF16), grid=(t // tr,),
        in_specs=[spec, spec, pl.BlockSpec((1, d_inner), lambda i: (0, 0))], out_specs=spec,
        compiler_params=pltpu.CompilerParams(dimension_semantics=("parallel",)),
    )(y, zx, norm_w.reshape(1, -1))


def _gate_norm_bwd(y, zx, norm_w, dout, d_inner):
    t, width = zx.shape
    tr = _pick(t, (256, 128))
    gs = d_inner // SSM_GROUPS

    def body(y_ref, z_ref, w_ref, do_ref, dy_ref, dz_ref, dw_ref):
        @pl.when(pl.program_id(0) == 0)
        def _():
            dw_ref[...] = jnp.zeros_like(dw_ref)

        for gi in range(SSM_GROUPS):
            cols = slice(gi * gs, (gi + 1) * gs)
            yv, zv = y_ref[:, cols], z_ref[:, cols]
            sz = _silu(zv)
            v = yv * sz
            r = lax.rsqrt(jnp.mean(v * v, axis=-1, keepdims=True) + NORM_EPS)
            vhat = v * r
            dn = do_ref[:, cols].astype(F32)
            dw_ref[0:1, cols] += jnp.sum(dn * vhat, axis=0, keepdims=True)
            dvh = dn * w_ref[0:1, cols]
            dv = r * (dvh - vhat * jnp.mean(dvh * vhat, axis=-1, keepdims=True))
            dy_ref[:, cols] = dv * sz
            dz_ref[:, cols] = (dv * yv * _silu_grad(zv)).astype(dz_ref.dtype)

    spec = pl.BlockSpec((tr, d_inner), lambda i: (i, 0))
    wspec = pl.BlockSpec((1, d_inner), lambda i: (0, 0))
    return pl.pallas_call(
        body, name="ssm_gate_norm_bwd",
        out_shape=(jax.ShapeDtypeStruct((t, d_inner), F32), jax.ShapeDtypeStruct((t, width), zx.dtype),
                   jax.ShapeDtypeStruct((1, d_inner), F32)),
        grid=(t // tr,),
        in_specs=[spec, spec, wspec, spec], out_specs=(spec, spec, wspec),
        compiler_params=pltpu.CompilerParams(dimension_semantics=("arbitrary",)),
    )(y, zx, norm_w.reshape(1, -1), dout)


def _ssm_small(dt_raw, dt_bias, a_log, d_skip):
    heads = dt_raw.shape[1]
    hpg = heads // SSM_GROUPS
    dtT = dt_raw.T.reshape(SSM_GROUPS, hpg, -1)
    return (dtT, dt_bias.reshape(SSM_GROUPS, hpg, 1), a_log.reshape(SSM_GROUPS, hpg, 1),
            jnp.repeat(d_skip, SSM_HEAD_DIM).reshape(1, -1))


def _ssm_core_fwd(zx, dt_raw, conv_w, conv_b, dt_bias, a_log, d_skip, norm_w):
    d_inner = norm_w.shape[0]
    pre = _conv_fwd(zx, conv_w, conv_b, d_inner)
    dtT, bias, alog, dsk = _ssm_small(dt_raw, dt_bias, a_log, d_skip)
    y, states = _ssd_fwd(pre, dtT, bias, alog, dsk, d_inner)
    out = _gate_norm_fwd(y, zx, norm_w, d_inner)
    return out, (zx, dt_raw, conv_w, dt_bias, a_log, d_skip, norm_w, pre, y, states)


def _ssm_core_bwd(res, dout):
    zx, dt_raw, conv_w, dt_bias, a_log, d_skip, norm_w, pre, y, states = res
    d_inner = norm_w.shape[0]
    heads = dt_raw.shape[1]
    dy, dzx, dnorm = _gate_norm_bwd(y, zx, norm_w, dout, d_inner)
    dtT, bias, alog, dsk = _ssm_small(dt_raw, dt_bias, a_log, d_skip)
    dx, db, dc, ddtT, acc, dsk_l = _ssd_bwd(pre, dtT, bias, alog, dsk, states, dy, d_inner)
    dpre = jnp.concatenate([dx, db, dc], axis=1)
    dzx, dconv_w, dconv_b = _conv_bwd(zx, conv_w, dpre, dzx, d_inner)
    d_dt_raw = ddtT.reshape(heads, -1).T
    hpg = heads // SSM_GROUPS
    da = acc[:, :hpg].sum(-1).reshape(heads)
    d_bias = acc[:, hpg:].sum(-1).reshape(heads)
    d_alog = da * (-jnp.exp(a_log))
    d_dskip = dsk_l.reshape(heads, SSM_HEAD_DIM).sum(-1)
    return dzx, d_dt_raw, dconv_w, dconv_b.reshape(-1), d_bias, d_alog, d_dskip, dnorm.reshape(-1)


def _rows_call(body, name, ins, outs, acc_outs=(), rows=256):
    t = max(a.shape[0] for a in ins)
    tr = _pick(t, (rows, 128, 64, 32, 16, 8))

    def spec(a):
        if a.shape[0] == t:
            return pl.BlockSpec((tr, a.shape[1]), lambda i: (i, 0))
        return pl.BlockSpec(a.shape, lambda i: (0, 0))

    return pl.pallas_call(
        body, name=name, out_shape=tuple(outs) + tuple(acc_outs), grid=(t // tr,),
        in_specs=[spec(a) for a in ins],
        out_specs=tuple(spec(a) for a in outs) + tuple(pl.BlockSpec(a.shape, lambda i: (0, 0)) for a in acc_outs),
        compiler_params=pltpu.CompilerParams(dimension_semantics=("arbitrary" if acc_outs else "parallel",)),
    )(*ins)


def _rms_fwd(x, gain, after=None):
    def body(x_ref, g_ref, *rest):
        v = x_ref[...]
        rest[-1][...] = (v * lax.rsqrt(jnp.mean(v * v, axis=-1, keepdims=True) + NORM_EPS) * g_ref[...]).astype(BF16)

    ins = [x, gain.reshape(1, -1)] + ([] if after is None else [after])
    (h,) = _rows_call(body, "rms_fwd", ins, [jax.ShapeDtypeStruct(x.shape, BF16)])
    return h


def _rms_bwd(x, gain, dh, dres, after):
    def body(x_ref, g_ref, dh_ref, dr_ref, *rest):
        dx_ref, dg_ref = rest[-2:]

        @pl.when(pl.program_id(0) == 0)
        def _():
            dg_ref[...] = jnp.zeros_like(dg_ref)

        v = x_ref[...]
        r = lax.rsqrt(jnp.mean(v * v, axis=-1, keepdims=True) + NORM_EPS)
        vhat = v * r
        d = dh_ref[...].astype(F32)
        dg_ref[...] += jnp.sum(d * vhat, axis=0, keepdims=True)
        dvh = d * g_ref[...]
        dx_ref[...] = dr_ref[...] + r * (dvh - vhat * jnp.mean(dvh * vhat, axis=-1, keepdims=True))

    ins = [x, gain.reshape(1, -1), dh, dres] + ([] if after is None else [after])
    dx, dg = _rows_call(body, "rms_bwd", ins, [jax.ShapeDtypeStruct(x.shape, F32)],
                        [jax.ShapeDtypeStruct((1, x.shape[1]), F32)])
    return dx, dg.reshape(gain.shape)


def _swiglu_fwd(gu):
    t, f2 = gu.shape
    f = f2 // 2

    def body(gu_ref, o_ref):
        o_ref[...] = (_silu(gu_ref[:, :f].astype(F32)) * gu_ref[:, f:].astype(F32)).astype(BF16)

    (act,) = _rows_call(body, "swiglu_fwd", [gu], [jax.ShapeDtypeStruct((t, f), BF16)])
    return act


def _swiglu_bwd(gu, dact):
    t, f2 = gu.shape
    f = f2 // 2

    def body(gu_ref, d_ref, o_ref):
        g, u, d = gu_ref[:, :f].astype(F32), gu_ref[:, f:].astype(F32), d_ref[...].astype(F32)
        o_ref[:, :f] = (d * u * _silu_grad(g)).astype(BF16)
        o_ref[:, f:] = (d * _silu(g)).astype(BF16)

    (dgu,) = _rows_call(body, "swiglu_bwd", [gu, dact], [jax.ShapeDtypeStruct((t, f2), BF16)])
    return dgu


def _ple_fwd(x, gl, ple):
    def body(x_ref, g_ref, p_ref, o_ref):
        o_ref[...] = x_ref[...] + jax.nn.sigmoid(g_ref[...]) * p_ref[...]

    (out,) = _rows_call(body, "ple_fwd", [x, gl, ple], [jax.ShapeDtypeStruct(x.shape, F32)])
    return out


def _ple_bwd(gl, ple, dout):
    def body(g_ref, p_ref, d_ref, dg_ref, dp_ref):
        s, d = jax.nn.sigmoid(g_ref[...]), d_ref[...]
        dg_ref[...] = (d * p_ref[...] * s * (1.0 - s)).astype(BF16)
        dp_ref[...] = (d * s).astype(BF16)

    shp = jax.ShapeDtypeStruct(gl.shape, BF16)
    return _rows_call(body, "ple_bwd", [gl, ple, dout], [shp, shp])


def _loss_fwd(y, target):
    inv = 1.0 / y.shape[1]

    def body(y_ref, t_ref, d_ref, l_ref):
        @pl.when(pl.program_id(0) == 0)
        def _():
            l_ref[...] = jnp.zeros_like(l_ref)

        e = y_ref[...] - t_ref[...]
        d_ref[...] = e * inv
        part = jnp.sum(jnp.sum(e * e, axis=1, keepdims=True), axis=0, keepdims=True) * (0.5 * inv)
        l_ref[...] += jnp.broadcast_to(part, l_ref.shape)

    dy, acc = _rows_call(body, "loss_fwd", [y, target], [jax.ShapeDtypeStruct(y.shape, F32)],
                         [jax.ShapeDtypeStruct((SUBLANES, LANES), F32)])
    return acc[0, 0], dy


def rmsnorm(x, gain):
    y = x * lax.rsqrt(jnp.mean(x * x, axis=-1, keepdims=True) + NORM_EPS)
    return y * gain


def causal_depthwise_conv(u, w, bias):
    k_width, chans = w.shape
    out = lax.conv_general_dilated(u, w[:, None, :], window_strides=(1,), padding=[(k_width - 1, 0)],
                                   dimension_numbers=("NWC", "WIO", "NWC"), feature_group_count=chans)
    return out + bias


def ssd_chunked(x, dt, a, bm, cm):
    b, t, heads, _ = x.shape
    nc, cl = t // SSD_CHUNK, SSD_CHUNK
    g, hg = SSM_GROUPS, heads // SSM_GROUPS
    xs = (x * dt[..., None]).reshape(b, nc, cl, g, hg, SSM_HEAD_DIM)
    a_dt = (dt * a).reshape(b, nc, cl, g, hg).transpose(0, 1, 3, 4, 2)
    a_cs = jnp.cumsum(a_dt, axis=-1)
    bc = bm.reshape(b, nc, cl, g, SSM_STATE)
    cc = cm.reshape(b, nc, cl, g, SSM_STATE)
    causal = jnp.tril(jnp.ones((cl, cl), dtype=bool))
    seg = a_cs[..., :, None] - a_cs[..., None, :]
    lmat = jnp.exp(jnp.where(causal, seg, -jnp.inf))
    cb = jnp.einsum("bclgn,bcsgn->bcgls", cc, bc)
    y_diag = jnp.einsum("bcgls,bcghls,bcsghp->bclghp", cb, lmat, xs)
    decay = jnp.exp(a_cs[..., -1:] - a_cs)
    states = jnp.einsum("bclgn,bcghl,bclghp->bcghpn", bc, decay, xs)
    chunk_decay = jnp.exp(a_cs[..., -1])

    def step(carry, inp):
        st, dec = inp
        return carry * dec[..., None, None] + st, carry

    init = jnp.zeros((b, g, hg, SSM_HEAD_DIM, SSM_STATE), F32)
    _, prev = lax.scan(step, init, (jnp.moveaxis(states, 1, 0), jnp.moveaxis(chunk_decay, 1, 0)))
    prev = jnp.moveaxis(prev, 0, 1)
    y_off = jnp.einsum("bclgn,bcghpn,bcghl->bclghp", cc, prev, jnp.exp(a_cs))
    return (y_diag + y_off).reshape(b, t, heads, SSM_HEAD_DIM)


def mamba2_mixer(h, wt_in, conv_w, conv_b, dt_bias, a_log, d_skip, norm_w, w_out):
    t, d_model = h.shape
    d_inner = 2 * d_model
    heads = d_inner // SSM_HEAD_DIM
    gn = SSM_GROUPS * SSM_STATE
    conv_dim = d_inner + 2 * gn
    zx = lin_t(h, wt_in[:d_inner + conv_dim])
    dt_raw = lin_t(h, wt_in[d_inner + conv_dim:])
    return lin(ssm_core(zx, dt_raw, conv_w, conv_b, dt_bias, a_log, d_skip, norm_w), w_out)


def ssm_core_jnp(zx, dt_raw, conv_w, conv_b, dt_bias, a_log, d_skip, norm_w):
    t = zx.shape[0]
    d_inner = norm_w.shape[0]
    heads = d_inner // SSM_HEAD_DIM
    gn = SSM_GROUPS * SSM_STATE
    z = zx[:, :d_inner]
    xbc = zx[:, d_inner:]
    xbc = jax.nn.silu(causal_depthwise_conv(xbc[None], conv_w, conv_b))[0]
    xs = xbc[:, :d_inner]
    bm = xbc[:, d_inner:d_inner + gn].reshape(1, t, SSM_GROUPS, SSM_STATE)
    cm = xbc[:, d_inner + gn:].reshape(1, t, SSM_GROUPS, SSM_STATE)
    dt = jax.nn.softplus(dt_raw + dt_bias)[None]
    a = -jnp.exp(a_log)
    xh = xs.reshape(1, t, heads, SSM_HEAD_DIM)
    y = ssd_chunked(xh, dt, a, bm, cm)
    y = y + xh * d_skip[:, None]
    y = y.reshape(t, d_inner) * jax.nn.silu(z)
    return rmsnorm(y.reshape(t, SSM_GROUPS, -1), norm_w.reshape(SSM_GROUPS, -1)).reshape(t, d_inner)


def alibi_slopes(n_heads):
    return 2.0 ** (-8.0 * jnp.arange(1, n_heads + 1, dtype=F32) / n_heads)


def dilated_group_attention(q, k, v, window, dilation, slopes):
    b, t, nh, e = q.shape
    span = window // dilation
    blk = span
    lu = t // dilation
    nb = -(-lu // blk)
    lp = nb * blk

    def to_blocks(arr):
        arr = arr.reshape(b, lu, dilation, nh, e)
        arr = jnp.pad(arr, ((0, 0), (0, lp - lu), (0, 0), (0, 0), (0, 0)))
        return arr.reshape(b, nb, blk, dilation, nh, e)

    qb, kb, vb = to_blocks(q), to_blocks(k), to_blocks(v)
    pad_prev = ((0, 0), (1, 0), (0, 0), (0, 0), (0, 0), (0, 0))
    kcat = jnp.concatenate([jnp.pad(kb, pad_prev)[:, :nb], kb], axis=2)
    vcat = jnp.concatenate([jnp.pad(vb, pad_prev)[:, :nb], vb], axis=2)
    scores = jnp.einsum("bnqrhe,bnkrhe->bnrhqk", qb, kcat) * (1.0 / math.sqrt(e))
    qi = jnp.arange(blk)[:, None]
    ki = jnp.arange(2 * blk)[None, :]
    dist = qi + blk - ki
    in_band = (dist >= 0) & (dist <= span)
    key_u = jnp.arange(nb)[:, None] * blk - blk + jnp.arange(2 * blk)[None, :]
    valid = in_band[None] & (key_u >= 0)[:, None, :]
    bias = -slopes[:, None, None] * (dilation * dist).astype(F32)[None]
    logits = jnp.where(valid[None, :, None, None], scores + bias[None, None, None], -jnp.inf)
    lse = jax.nn.logsumexp(logits, axis=-1)
    probs = jnp.exp(logits - lse[..., None])
    out = jnp.einsum("bnrhqk,bnkrhe->bnqrhe", probs, vcat)
    out = out.reshape(b, lp, dilation, nh, e)[:, :lu].reshape(b, t, nh, e)
    lse = lse.transpose(0, 1, 4, 2, 3).reshape(b, lp, dilation, nh)[:, :lu].reshape(b, t, nh)
    return out, lse


def dilated_attention_mixer(h, wt_qkv, q_gain, k_gain, w_o):
    t, d_model = h.shape
    heads = d_model // ATT_HEAD_DIM
    ng = len(DIL_PATTERNS)
    return lin(attention_core(lin_t(h, wt_qkv, BF16), q_gain, k_gain), w_o)


def attention_core_jnp(qkv, q_gain, k_gain):
    t = qkv.shape[0]
    ng = len(DIL_PATTERNS)
    heads = qkv.shape[1] // (3 * ng * ATT_HEAD_DIM)
    qkv = qkv.astype(F32).reshape(1, t, ng, 3, heads, ATT_HEAD_DIM)
    q = rmsnorm(qkv[:, :, :, 0], q_gain)
    k = rmsnorm(qkv[:, :, :, 1], k_gain)
    v = qkv[:, :, :, 2]
    slopes = alibi_slopes(heads)
    outs, lses = [], []
    for g, (window, dilation) in enumerate(DIL_PATTERNS):
        o_g, l_g = dilated_group_attention(q[:, :, g], k[:, :, g], v[:, :, g], window, dilation, slopes)
        outs.append(o_g)
        lses.append(l_g)
    alpha = jax.nn.softmax(jnp.stack(lses), axis=0)
    o = jnp.einsum("gbth,gbthe->bthe", alpha, jnp.stack(outs))
    return o.reshape(t, heads * ATT_HEAD_DIM)


def local_step(small, fetch, emit, x, p, target):
    depth = small['norm_mix'].shape[0]
    ssm_small = ('ssm_conv_w', 'ssm_conv_b', 'ssm_dt_bias', 'ssm_a_log', 'ssm_d_skip', 'ssm_norm_w')
    saved = []
    for i in range(depth):
        j = i // 2
        s = {'x': x}
        wm, token = fetch(2 * i, x)
        s['wm'] = wm
        h = s['h'] = _rms_fwd(x, small['norm_mix'][i], token)
        if i % 2 == 0:
            n_main = wm['ssm_w_in'].shape[0] - small['ssm_dt_bias'].shape[1]
            zx = _mm(h, wm['ssm_w_in'][:n_main], tb=True, name="ssm_in_fwd")
            dt_raw = _mm(h, wm['ssm_w_in'][n_main:], tb=True, name="ssm_dt_fwd")
            y, s['mix'] = _ssm_core_fwd(zx, dt_raw, wm['ssm_conv_w'], *[small[n][j] for n in ssm_small[1:]])
            x = _mm(y, wm['ssm_w_out'], add=x, name="ssm_out_fwd")
        else:
            qkv = _mm(h, wm['att_w_qkv'], tb=True, out_dtype=BF16, name="att_qkv_fwd")
            y, s['mix'] = _attention_core_fwd(qkv, small['att_q_norm'][j], small['att_k_norm'][j])
            x = _mm(y, wm['att_w_o'], add=x, name="att_o_fwd")
        s['y'], s['x1'] = y, x
        wf, token = fetch(2 * i + 1, x)
        s['wf'] = wf
        h2 = s['h2'] = _rms_fwd(x, small['norm_ffn'][i], token)
        gu = s['gu'] = _mm(h2, wf['ffn_w_gu'], tb=True, out_dtype=BF16, name="ffn_gu_fwd")
        act = s['act'] = _swiglu_fwd(gu)
        x = s['x2'] = _mm(act, wf['ffn_w_down'], add=x, name="ffn_down_fwd")
        gl = s['gl'] = _mm(x, wf['ple_w_gate'], name="ple_gate_fwd")
        ple = s['ple'] = _mm(p[i], wf['ple_w_proj'], tb=True, name="ple_proj_fwd")
        x = _ple_fwd(x, gl, ple)
        saved.append(s)
    loss, dx = _loss_fwd(x, target)

    g = {n: [None] * small[n].shape[0] for n in small}
    for i in reversed(range(depth)):
        j = i // 2
        s = saved[i]
        wm, wf = s['wm'], s['wf']
        gf = {}
        dgl, dple = _ple_bwd(s['gl'], s['ple'], dx)
        gf['ple_w_proj'] = _mm(dple, p[i], ta=True, out_dtype=BF16, name="ple_proj_dw")
        gf['ple_w_gate'] = _mm(s['x2'], dgl, ta=True, out_dtype=BF16, name="ple_gate_dw")
        dx = _mm(dgl, wf['ple_w_gate'], tb=True, add=dx, name="ple_gate_da")
        dact = _mm(dx, wf['ffn_w_down'], tb=True, out_dtype=BF16, name="ffn_down_da")
        gf['ffn_w_down'] = _mm(s['act'], dx, ta=True, out_dtype=BF16, name="ffn_down_dw")
        dgu = _swiglu_bwd(s['gu'], dact)
        dh2 = _mm(dgu, wf['ffn_w_gu'], out_dtype=BF16, name="ffn_gu_da")
        gf['ffn_w_gu'] = _mm(dgu, s['h2'], ta=True, out_dtype=BF16, name="ffn_gu_dw")
        dx, g['norm_ffn'][i] = _rms_bwd(s['x1'], small['norm_ffn'][i], dh2, dx, emit(2 * i + 1, gf))
        gm = {}
        if i % 2 == 0:
            n_main = wm['ssm_w_in'].shape[0] - small['ssm_dt_bias'].shape[1]
            dyn = _mm(dx, wm['ssm_w_out'], tb=True, out_dtype=BF16, name="ssm_out_da")
            gm['ssm_w_out'] = _mm(s['y'], dx, ta=True, out_dtype=BF16, name="ssm_out_dw")
            dzx, d_dt, *sg = _ssm_core_bwd(s['mix'], dyn)
            for n, v in zip(ssm_small, sg):
                g[n][j] = v
            dh = _mm(d_dt, wm['ssm_w_in'][n_main:], name="ssm_dt_da")
            dh = _mm(dzx, wm['ssm_w_in'][:n_main], add=dh, out_dtype=BF16, name="ssm_in_da")
            gm['ssm_w_in'] = jnp.concatenate([_mm(dzx, s['h'], ta=True, out_dtype=BF16, name="ssm_in_dw"),
                                              _mm(d_dt, s['h'], ta=True, out_dtype=BF16, name="ssm_dt_dw")], axis=0)
        else:
            do = _mm(dx, wm['att_w_o'], tb=True, name="att_o_da")
            gm['att_w_o'] = _mm(s['y'], dx, ta=True, out_dtype=BF16, name="att_o_dw")
            dqkv, g['att_q_norm'][j], g['att_k_norm'][j] = _attention_core_bwd(s['mix'], do)
            dh = _mm(dqkv, wm['att_w_qkv'], out_dtype=BF16, name="att_qkv_da")
            gm['att_w_qkv'] = _mm(dqkv, s['h'], ta=True, out_dtype=BF16, name="att_qkv_dw")
        dx, g['norm_mix'][i] = _rms_bwd(s['x'], small['norm_mix'][i], dh, dx, emit(2 * i, gm))
    return loss, dx, {n: jnp.stack(v) for n, v in g.items()}


MIXER_WEIGHTS = (('ssm_w_in', 'ssm_w_out', 'ssm_conv_w'), ('att_w_qkv', 'att_w_o'))
CHANNEL_WEIGHTS = ('ffn_w_gate', 'ffn_w_up', 'ffn_w_down', 'ple_w_proj', 'ple_w_gate')


def _pack_plan(shapes, width, stage):
    layer = stage // 2
    if stage % 2:
        members = [(n, layer) for n in CHANNEL_WEIGHTS]
    else:
        members = [(n, layer // 2) for n in MIXER_WEIGHTS[layer % 2]]
    plan, off = [], 0
    for name, lyr in members:
        _, r, c = shapes[name]
        if name in COL_SHARDED:
            r, c = c, r
        if name == 'ssm_conv_w':
            pr = -(-2 * r * c // width)
        else:
            assert (r * c) % width == 0, (name, r, c)
            pr = r * c // width
        plan.append((name, lyr, r, c, pr, off))
        off += _round_up(pr, BF16_ROWS)
    return plan, off


def _small_plan(shapes):
    plan, off = [], 0
    for name in SMALL:
        n = math.prod(shapes[name])
        plan.append((name, n, off))
        off += n
    return plan, _round_up(off, SUBLANES * LANES)


def kernel(x, p, norm_mix, norm_ffn, ssm_w_in, ssm_conv_w, ssm_conv_b, ssm_dt_bias, ssm_a_log, ssm_d_skip, ssm_norm_w, ssm_w_out, att_w_qkv, att_q_norm, att_k_norm, att_w_o, ffn_w_gate, ffn_w_up, ffn_w_down, ple_w_proj, ple_w_gate, loss_target, m_norm_mix, m_norm_ffn, m_ssm_w_in, m_ssm_conv_w, m_ssm_conv_b, m_ssm_dt_bias, m_ssm_a_log, m_ssm_d_skip, m_ssm_norm_w, m_ssm_w_out, m_att_w_qkv, m_att_q_norm, m_att_k_norm, m_att_w_o, m_ffn_w_gate, m_ffn_w_up, m_ffn_w_down, m_ple_w_proj, m_ple_w_gate, v_norm_mix, v_norm_ffn, v_ssm_w_in, v_ssm_conv_w, v_ssm_conv_b, v_ssm_dt_bias, v_ssm_a_log, v_ssm_d_skip, v_ssm_norm_w, v_ssm_w_out, v_att_w_qkv, v_att_q_norm, v_att_k_norm, v_att_w_o, v_ffn_w_gate, v_ffn_w_up, v_ffn_w_down, v_ple_w_proj, v_ple_w_gate):
    given = dict(locals())
    w_in = {n: given[n] for n in WEIGHTS}
    m_in = {n: given["m_" + n] for n in WEIGHTS}
    v_in = {n: given["v_" + n] for n in WEIGHTS}
    width = x.shape[-1]
    depth = norm_mix.shape[0]
    n_stages = 2 * depth

    plans = [_pack_plan({n: w_in[n].shape for n in BIG + ('ssm_conv_w',)}, width, stage) for stage in range(n_stages)]

    def start_gather(stage, after):
        pieces = []
        for name, layer, r, c, pr, off in plans[stage][0]:
            blk = w_in[name][layer]
            if name == 'ssm_conv_w':
                blk = lax.bitcast_convert_type(blk.reshape(-1), BF16).reshape(-1)
                blk = jnp.pad(blk, (0, pr * width - blk.shape[0]))
            elif name in COL_SHARDED:
                blk = blk.T
            blk = blk.astype(BF16).reshape(pr, width)
            pieces.append(jnp.pad(blk, ((0, _round_up(pr, BF16_ROWS) - pr), (0, 0))))
        return exchange_start(jnp.concatenate(pieces, axis=0), True, f"gather_start_{stage}", after)

    pending = [start_gather(0, None)]

    def fetch(stage, after):
        handle, token = pending[stage]
        land = exchange_wait(handle, token if stage == 0 else after, True, f"gather_wait_{stage}")
        token = None
        if stage + 1 < n_stages:
            pending.append(start_gather(stage + 1, land))
            token = pending[-1][1]
        got = {}
        for name, layer, r, c, pr, off in plans[stage][0]:
            piece = land[:, off:off + pr]
            if name == 'ssm_conv_w':
                taps, chans = w_in[name].shape[1:]
                bits = piece.reshape(N_DEV, -1)[:, :2 * taps * chans].reshape(N_DEV, taps * chans, 2)
                piece = lax.bitcast_convert_type(bits, F32).reshape(N_DEV, taps, chans)
                got[name] = piece.transpose(1, 0, 2).reshape(taps, N_DEV * chans)
            else:
                got[name] = piece.reshape(N_DEV * r, c)
        if 'ffn_w_gate' in got:
            got['ffn_w_gu'] = jnp.concatenate([got.pop('ffn_w_gate'), got.pop('ffn_w_up')], axis=0)
        return got, token

    scatters = [None] * n_stages

    def emit(stage, grads):
        grads = dict(grads)
        if 'ffn_w_gu' in grads:
            hidden = grads['ffn_w_gu'].shape[0] // 2
            grads['ffn_w_gate'], grads['ffn_w_up'] = grads['ffn_w_gu'][:hidden], grads['ffn_w_gu'][hidden:]
        pieces = []
        for name, layer, r, c, pr, off in plans[stage][0]:
            if name == 'ssm_conv_w':
                continue
            g = grads[name].reshape(N_DEV, pr, width)
            pieces.append(jnp.pad(g, ((0, 0), (0, _round_up(pr, BF16_ROWS) - pr), (0, 0))))
        scatters[stage], token = exchange_start(jnp.concatenate(pieces, axis=1), False, f"scatter_start_{stage}")
        return token

    small = {n: w_in[n] for n in SMALL}
    cs = w_in['ssm_conv_w'].shape[2]
    loss_local, gx, gw = local_step(small, fetch, emit, x[0], p[:, 0], loss_target[0])
    loss = lax.psum(loss_local, ("x", "y", "c"))

    parts = {}
    for stage in reversed(range(n_stages)):
        received = exchange_wait(scatters[stage], gx, False, f"scatter_wait_{stage}")
        gsum = sum_slots(received, f"sum_grads_{stage}")
        for name, layer, r, c, pr, off in plans[stage][0]:
            if name == 'ssm_conv_w':
                continue
            g = gsum[off:off + pr].reshape(r, c)
            parts[name, layer] = g.T if name in COL_SHARDED else g
    grads = {n: jnp.stack([parts[n, layer] for layer in range(w_in[n].shape[0])]) for n in BIG}

    splan, stotal = _small_plan({n: gw[n].shape for n in SMALL})
    svec = jnp.concatenate([gw[n].reshape(-1) for n, _, _ in splan])
    svec = jnp.pad(svec, (0, stotal - svec.shape[0])).reshape(stotal // LANES, LANES)
    _, ssum = all_gather_sum_small(svec, "sum_small_grads")
    ssum = ssum.reshape(-1)
    for name, n, off in splan:
        grads[name] = ssum[off:off + n].reshape(gw[name].shape)
    me = _me()
    grads['ssm_conv_w'] = lax.dynamic_slice_in_dim(grads['ssm_conv_w'], me * cs, cs, axis=2)

    delta, new_m, new_v = {}, {}, {}
    for name in BIG:
        shp = w_in[name].shape
        flat = lambda a: a.reshape(-1, shp[-1])
        d, nm, nv = adamw(flat(w_in[name]), flat(grads[name]), flat(m_in[name]), flat(v_in[name]), "adamw_" + name)
        delta[name], new_m[name], new_v[name] = d.reshape(shp), nm.reshape(shp), nv.reshape(shp)
    splan2, stotal2 = _small_plan({n: w_in[n].shape for n in SMALL})

    def pack_small(src):
        vec = jnp.concatenate([src[n].reshape(-1) for n, _, _ in splan2])
        return jnp.pad(vec, (0, stotal2 - vec.shape[0]), constant_values=1.0).reshape(stotal2 // LANES, LANES)

    sd, snm, snv = adamw(pack_small(w_in), pack_small(grads), pack_small(m_in), pack_small(v_in), "adamw_small")
    for name, n, off in splan2:
        shp = w_in[name].shape
        delta[name] = sd.reshape(-1)[off:off + n].reshape(shp)
        new_m[name] = snm.reshape(-1)[off:off + n].reshape(shp)
        new_v[name] = snv.reshape(-1)[off:off + n].reshape(shp)

    return (loss, gx[None], *[grads[n] for n in WEIGHTS], *[delta[n] for n in WEIGHTS],
            *[new_m[n] for n in WEIGHTS], *[new_v[n] for n in WEIGHTS])
```

```python
import functools
import math

import jax
import jax.numpy as jnp
from jax import lax
from jax.experimental import pallas as pl
from jax.experimental.pallas import tpu as pltpu

F32 = jnp.float32
BF16 = jnp.bfloat16
N_DEV = 8
MESH = pl.DeviceIdType.MESH

SSM_HEAD_DIM = 64
SSM_GROUPS = 4
SSM_STATE = 128
CONV_WIDTH = 4
SSD_CHUNK = 128
ATT_HEAD_DIM = 64
DIL_PATTERNS = ((128, 1), (512, 4), (2048, 16))
NORM_EPS = 1e-6
ADAM_LR = 0.001
ADAM_B1 = 0.9
ADAM_B2 = 0.999
ADAM_EPS = 1e-08
ADAM_WD = 0.01
ADAM_STEP = 10

BF16_ROWS = 16
LANES = 128
SUBLANES = 8

WEIGHTS = ['norm_mix', 'norm_ffn', 'ssm_w_in', 'ssm_conv_w', 'ssm_conv_b', 'ssm_dt_bias', 'ssm_a_log', 'ssm_d_skip',
           'ssm_norm_w', 'ssm_w_out', 'att_w_qkv', 'att_q_norm', 'att_k_norm', 'att_w_o', 'ffn_w_gate', 'ffn_w_up',
           'ffn_w_down', 'ple_w_proj', 'ple_w_gate']
COL_SHARDED = ('ssm_w_in', 'att_w_qkv', 'ffn_w_gate', 'ffn_w_up', 'ple_w_proj')
ROW_SHARDED = ('ssm_w_out', 'att_w_o', 'ffn_w_down', 'ple_w_gate')
BIG = COL_SHARDED + ROW_SHARDED
SMALL = ('norm_mix', 'norm_ffn', 'ssm_conv_w', 'ssm_conv_b', 'ssm_dt_bias', 'ssm_a_log', 'ssm_d_skip', 'ssm_norm_w',
         'att_q_norm', 'att_k_norm')


def _pick(n, cands):
    for c in cands:
        if n % c == 0:
            return c
    return n


def _round_up(n, m):
    return -(-n // m) * m


MM_TILES = (1024, 1408, 512, 256, 128)
MM_VMEM_BYTES = 48 * 1024 * 1024


def _mm(a, b, *, ta=False, tb=False, out_dtype=F32, add=None, name):
    k_dim, m_dim = (a.shape if ta else a.shape[::-1])
    n_dim = b.shape[0] if tb else b.shape[1]
    assert (b.shape[1] if tb else b.shape[0]) == k_dim, (a.shape, b.shape, ta, tb)
    tm = _pick(m_dim, MM_TILES)
    tn = _pick(n_dim, MM_TILES)
    tk = _pick(k_dim, MM_TILES)
    nk = k_dim // tk
    a_spec = pl.BlockSpec((tk, tm), lambda i, j, k: (k, i)) if ta else pl.BlockSpec((tm, tk), lambda i, j, k: (i, k))
    b_spec = pl.BlockSpec((tn, tk), lambda i, j, k: (j, k)) if tb else pl.BlockSpec((tk, tn), lambda i, j, k: (k, j))
    o_spec = pl.BlockSpec((tm, tn), lambda i, j, k: (i, j))
    dims = (((0 if ta else 1,), (1 if tb else 0,)), ((), ()))
    has_add = add is not None

    def body(*refs):
        a_ref, b_ref = refs[:2]
        o_ref = refs[2 + has_add]

        def dot():
            return lax.dot_general(a_ref[...].astype(BF16), b_ref[...].astype(BF16), dims,
                                   preferred_element_type=F32)

        def finish(acc):
            if has_add:
                acc = acc + refs[2][...].astype(F32)
            o_ref[...] = acc.astype(o_ref.dtype)

        if nk == 1:
            finish(dot())
            return
        acc_ref = refs[3 + has_add]
        k = pl.program_id(2)

        @pl.when(k == 0)
        def _():
            acc_ref[...] = dot()

        @pl.when((k > 0) & (k < nk - 1))
        def _():
            acc_ref[...] += dot()

        @pl.when(k == nk - 1)
        def _():
            finish(acc_ref[...] + dot())

    return pl.pallas_call(
        body, name=f"{name}_{m_dim}x{n_dim}x{k_dim}",
        out_shape=jax.ShapeDtypeStruct((m_dim, n_dim), out_dtype),
        grid=(m_dim // tm, n_dim // tn, nk),
        in_specs=[a_spec, b_spec] + ([o_spec] if has_add else []),
        out_specs=o_spec,
        scratch_shapes=[] if nk == 1 else [pltpu.VMEM((tm, tn), F32)],
        compiler_params=pltpu.CompilerParams(dimension_semantics=("parallel", "parallel", "arbitrary"),
                                             vmem_limit_bytes=MM_VMEM_BYTES),
    )(*((a, b) + ((add,) if has_add else ())))


def _me():
    return 4 * lax.axis_index("x") + 2 * lax.axis_index("y") + lax.axis_index("c")


def _peer(j):
    x, y, c = lax.axis_index("x"), lax.axis_index("y"), lax.axis_index("c")
    px = 1 - x if j & 4 else x
    py = 1 - y if j & 2 else y
    pc = 1 - c if j & 1 else c
    return (px, py, pc), 4 * px + 2 * py + pc


def _exchange_body(src_of, dst_ref, send_sems, recv_sems, local_sem):
    me = _me()
    mine = pltpu.make_async_copy(src_of(me), dst_ref.at[me], local_sem)
    mine.start()
    sends = []
    for j in range(1, N_DEV):
        peer, pidx = _peer(j)
        cp = pltpu.make_async_remote_copy(src_ref=src_of(pidx), dst_ref=dst_ref.at[me], send_sem=send_sems.at[j - 1],
                                          recv_sem=recv_sems.at[j - 1], device_id=peer, device_id_type=MESH)
        cp.start()
        sends.append(cp)
    for j in range(1, N_DEV):
        peer, pidx = _peer(j)
        pltpu.make_async_remote_copy(src_ref=src_of(pidx), dst_ref=dst_ref.at[pidx], send_sem=send_sems.at[j - 1],
                                     recv_sem=recv_sems.at[j - 1], device_id=peer, device_id_type=MESH).wait_recv()
    for cp in sends:
        cp.wait_send()
    mine.wait()


_EXCHANGE_SCRATCH = [pltpu.SemaphoreType.DMA((N_DEV - 1,)), pltpu.SemaphoreType.DMA((N_DEV - 1,)),
                     pltpu.SemaphoreType.DMA]


def all_gather_hbm(shard, name):
    def body(x_ref, out_ref, send_sems, recv_sems, local_sem):
        _exchange_body(lambda k: x_ref, out_ref, send_sems, recv_sems, local_sem)

    return pl.pallas_call(
        body, name=name,
        out_shape=jax.ShapeDtypeStruct((N_DEV,) + shard.shape, shard.dtype),
        in_specs=[pl.BlockSpec(memory_space=pl.ANY)],
        out_specs=pl.BlockSpec(memory_space=pl.ANY),
        scratch_shapes=list(_EXCHANGE_SCRATCH),
    )(shard)


def all_to_all_hbm(slots, name):
    def body(x_ref, out_ref, send_sems, recv_sems, local_sem):
        _exchange_body(lambda k: x_ref.at[k], out_ref, send_sems, recv_sems, local_sem)

    return pl.pallas_call(
        body, name=name,
        out_shape=jax.ShapeDtypeStruct(slots.shape, slots.dtype),
        in_specs=[pl.BlockSpec(memory_space=pl.ANY)],
        out_specs=pl.BlockSpec(memory_space=pl.ANY),
        scratch_shapes=list(_EXCHANGE_SCRATCH),
    )(slots)


_HBM = pl.BlockSpec(memory_space=pltpu.HBM)
_SEM = pl.BlockSpec(memory_space=pltpu.SEMAPHORE)


def _split_copies(src_ref, gather, land_ref, send_sems, recv_sems):
    me = _me()
    pairs = []
    for j in range(1, N_DEV):
        peer, pidx = _peer(j)

        def make(slot, peer=peer, pidx=pidx, j=j):
            return pltpu.make_async_remote_copy(
                src_ref=src_ref if gather else src_ref.at[pidx], dst_ref=land_ref.at[slot],
                send_sem=send_sems.at[j - 1], recv_sem=recv_sems.at[j - 1], device_id=peer, device_id_type=MESH)

        pairs.append((make(me), make(pidx)))
    return pairs


def exchange_start(src, gather, name, after=None):
    land_shape = ((N_DEV,) + src.shape) if gather else src.shape
    has_after = after is not None

    def body(*refs):
        src_ref, land_ref = refs[:2]
        send_sems, recv_sems = refs[2 + has_after:4 + has_after]
        for send, _ in _split_copies(src_ref, gather, land_ref, send_sems, recv_sems):
            send.start()
        refs[-1][...] = jnp.zeros_like(refs[-1])

    sem = pltpu.SemaphoreType.DMA((N_DEV - 1,))
    send_sems, recv_sems, src_thru, land, token = pl.pallas_call(
        body, name=name,
        out_shape=(sem, sem, pltpu.HBM(src.shape, src.dtype), pltpu.HBM(land_shape, src.dtype),
                   jax.ShapeDtypeStruct((SUBLANES, LANES), F32)),
        in_specs=(_HBM, _HBM) + ((pl.BlockSpec(memory_space=pl.ANY),) if has_after else ()),
        out_specs=(_SEM, _SEM, _HBM, _HBM, pl.BlockSpec(memory_space=pltpu.VMEM)),
        input_output_aliases={0: 2, 1: 3},
        compiler_params=pltpu.CompilerParams(has_side_effects=pltpu.SideEffectType.DATAFLOW_SIDE_EFFECTING),
    )(pltpu.with_memory_space_constraint(src, pltpu.HBM),
      pltpu.with_memory_space_constraint(lax.empty(land_shape, src.dtype), pltpu.HBM),
      *((after,) if has_after else ()))
    return (send_sems, recv_sems, src_thru, land), token


def exchange_wait(handle, after, gather, name):
    send_sems, recv_sems, src_thru, land = handle

    def body(src_ref, land_ref, send_sems, recv_sems, after_ref, src_dead, got_ref):
        for _, arrival in _split_copies(src_ref, gather, land_ref, send_sems, recv_sems):
            arrival.wait_send()
            arrival.wait_recv()

    src_done, got = pl.pallas_call(
        body, name=name,
        out_shape=(pltpu.HBM(src_thru.shape, src_thru.dtype), pltpu.HBM(land.shape, land.dtype)),
        in_specs=(_HBM, _HBM, _SEM, _SEM, pl.BlockSpec(memory_space=pl.ANY)), out_specs=(_HBM, _HBM),
        input_output_aliases={0: 0, 1: 1},
        compiler_params=pltpu.CompilerParams(has_side_effects=pltpu.SideEffectType.DATAFLOW_SIDE_EFFECTING),
    )(src_thru, land, send_sems, recv_sems, after)
    mine = src_done if gather else lax.dynamic_index_in_dim(src_done, _me(), 0, keepdims=False)
    return lax.dynamic_update_index_in_dim(got, mine, _me(), 0)


def all_gather_sum_small(v, name):
    def body(x_ref, out_ref, sum_ref, send_sems, recv_sems, local_sem):
        _exchange_body(lambda k: x_ref, out_ref, send_sems, recv_sems, local_sem)
        acc = out_ref[0]
        for k in range(1, N_DEV):
            acc = acc + out_ref[k]
        sum_ref[...] = acc

    return pl.pallas_call(
        body, name=name,
        out_shape=(jax.ShapeDtypeStruct((N_DEV,) + v.shape, v.dtype), jax.ShapeDtypeStruct(v.shape, v.dtype)),
        in_specs=[pl.BlockSpec(memory_space=pltpu.VMEM)],
        out_specs=(pl.BlockSpec(memory_space=pltpu.VMEM), pl.BlockSpec(memory_space=pltpu.VMEM)),
        scratch_shapes=list(_EXCHANGE_SCRATCH),
    )(v)


def sum_slots(slots, name):
    _, p_dim, c_dim = slots.shape
    tp = _pick(p_dim, (256, 128, 64, 32, 16))

    def body(x_ref, o_ref):
        acc = x_ref[0].astype(F32)
        for k in range(1, N_DEV):
            acc = acc + x_ref[k].astype(F32)
        o_ref[...] = acc

    return pl.pallas_call(
        body, name=name,
        out_shape=jax.ShapeDtypeStruct((p_dim, c_dim), F32),
        grid=(p_dim // tp,),
        in_specs=[pl.BlockSpec((N_DEV, tp, c_dim), lambda i: (0, i, 0))],
        out_specs=pl.BlockSpec((tp, c_dim), lambda i: (i, 0)),
        compiler_params=pltpu.CompilerParams(dimension_semantics=("parallel",)),
    )(slots)


def adamw(w, g, m, v, name):
    rows, cols = w.shape
    tr = _pick(rows, (256, 128, 64, 32, 16, 8))

    def body(w_ref, g_ref, m_ref, v_ref, d_ref, nm_ref, nv_ref):
        gv = g_ref[...]
        nm = ADAM_B1 * m_ref[...] + (1.0 - ADAM_B1) * gv
        nv = ADAM_B2 * v_ref[...] + (1.0 - ADAM_B2) * (gv * gv)
        m_hat = nm / (1.0 - ADAM_B1 ** ADAM_STEP)
        v_hat = nv / (1.0 - ADAM_B2 ** ADAM_STEP)
        d_ref[...] = -ADAM_LR * (m_hat / (jnp.sqrt(v_hat) + ADAM_EPS) + ADAM_WD * w_ref[...])
        nm_ref[...] = nm
        nv_ref[...] = nv

    spec = pl.BlockSpec((tr, cols), lambda i: (i, 0))
    shp = jax.ShapeDtypeStruct((rows, cols), F32)
    return pl.pallas_call(
        body, name=name, out_shape=(shp, shp, shp), grid=(rows // tr,),
        in_specs=[spec] * 4, out_specs=(spec,) * 3,
        compiler_params=pltpu.CompilerParams(dimension_semantics=("parallel",)),
    )(w, g, m, v)


ATT_BLK = 128
NEG = -1e30


def _head_sums(v):
    first = lax.broadcasted_iota(jnp.int32, (1, LANES), 1) < ATT_HEAD_DIM
    s0 = jnp.sum(jnp.where(first, v, 0.0), axis=-1, keepdims=True)
    s1 = jnp.sum(jnp.where(first, 0.0, v), axis=-1, keepdims=True)
    return jnp.where(first, s0, s1)


def _head_col(v, hmask):
    return jnp.max(jnp.where(hmask, v, -jnp.inf), axis=-1, keepdims=True)


def _qk_norm(raw, gain2):
    rstd = lax.rsqrt(_head_sums(raw * raw) * (1.0 / ATT_HEAD_DIM) + NORM_EPS)
    xhat = raw * rstd
    return xhat * gain2, xhat, rstd


def _qk_norm_bwd(dn, xhat, rstd, gain2):
    dxh = dn * gain2
    return rstd * (dxh - xhat * (_head_sums(dxh * xhat) * (1.0 / ATT_HEAD_DIM))), dn * xhat


def _att_mask_bias(n, dilation):
    qi = lax.broadcasted_iota(jnp.int32, (ATT_BLK, 2 * ATT_BLK), 0)
    ki = lax.broadcasted_iota(jnp.int32, (ATT_BLK, 2 * ATT_BLK), 1)
    dist = qi + ATT_BLK - ki
    valid = (dist >= 0) & (dist <= ATT_BLK) & ((n > 0) | (ki >= ATT_BLK))
    return valid, (dilation * dist).astype(F32)


ATT_PAIRS = 4
RELAYOUT_ROWS = 512
RELAYOUT_COLS = 512


def _to_residues(x, dilation, col0=0, cols=None):
    t = x.shape[0]
    cols = x.shape[1] if cols is None else cols
    if dilation == 1 and col0 == 0 and cols == x.shape[1]:
        return x.reshape(1, t, cols)
    tr = _pick(t, (RELAYOUT_ROWS,))
    tc = _pick(cols, (RELAYOUT_COLS, 256, 128))
    per = tr // dilation
    assert tr % dilation == 0 and col0 % tc == 0

    def body(x_ref, o_ref, s_ref):
        for c in range(tc // LANES):
            lanes = slice(c * LANES, (c + 1) * LANES)
            s_ref[c] = x_ref[:, lanes].astype(F32)
            for r in range(dilation):
                o_ref[r, :, lanes] = s_ref[c, pl.ds(r, per, stride=dilation), :].astype(o_ref.dtype)

    return pl.pallas_call(
        body, name=f"to_residues_{dilation}", out_shape=jax.ShapeDtypeStruct((dilation, t // dilation, cols), x.dtype),
        grid=(t // tr, cols // tc),
        in_specs=[pl.BlockSpec((tr, tc), lambda i, j: (i, col0 // tc + j))],
        out_specs=pl.BlockSpec((dilation, per, tc), lambda i, j: (0, i, j)),
        scratch_shapes=[pltpu.VMEM((tc // LANES, tr, LANES), F32)],
        compiler_params=pltpu.CompilerParams(dimension_semantics=("parallel", "parallel")),
    )(x)


def _from_residues(y):
    dilation, lu, cols = y.shape
    t = dilation * lu
    if dilation == 1:
        return y.reshape(t, cols)
    tr = _pick(t, (RELAYOUT_ROWS,))
    tc = _pick(cols, (RELAYOUT_COLS, 256, 128))
    per = tr // dilation

    def body(y_ref, o_ref, s_ref):
        for c in range(tc // LANES):
            lanes = slice(c * LANES, (c + 1) * LANES)
            for r in range(dilation):
                s_ref[c, pl.ds(r, per, stride=dilation), :] = y_ref[r, :, lanes].astype(F32)
            o_ref[:, lanes] = s_ref[c].astype(o_ref.dtype)

    return pl.pallas_call(
        body, name=f"from_residues_{dilation}", out_shape=jax.ShapeDtypeStruct((t, cols), y.dtype),
        grid=(t // tr, cols // tc),
        in_specs=[pl.BlockSpec((dilation, per, tc), lambda i, j: (0, i, j))],
        out_specs=pl.BlockSpec((tr, tc), lambda i, j: (i, j)),
        scratch_shapes=[pltpu.VMEM((tc // LANES, tr, LANES), F32)],
        compiler_params=pltpu.CompilerParams(dimension_semantics=("parallel", "parallel")),
    )(y)


def _att_specs(base, hd, nb, pp):
    hpn = hd // LANES
    assert base % (pp * LANES) == 0 and hpn % pp == 0

    def spec(which, shift):
        def imap(r, hp, n):
            row = jnp.minimum(n, nb - 1) if shift == 0 else jnp.maximum(n - 1, 0)
            return (r, row, (base // LANES + which * hpn) // pp + hp)
        return pl.BlockSpec((None, ATT_BLK, pp * LANES), imap)

    return [spec(0, 0), spec(1, 1), spec(1, 0), spec(2, 1), spec(2, 0)]


DELTA_LANE = 64


def _att_group_fwd(qkv_r, base, hd, gq2, gk2, slopes, g):
    dilation, lu, _ = qkv_r.shape
    nb = lu // ATT_BLK
    assert nb * ATT_BLK == lu and hd % LANES == 0
    hpn = hd // LANES
    pp = math.gcd(ATT_PAIRS, hpn)
    scale = 1.0 / math.sqrt(ATT_HEAD_DIM)

    def body(q_ref, kp_ref, kc_ref, vp_ref, vc_ref, gq_ref, gk_ref, sl_ref, o_ref, l_ref):
        n = pl.program_id(2)
        lane = lax.broadcasted_iota(jnp.int32, (1, LANES), 1)
        first = (lane // ATT_HEAD_DIM) == 0
        valid, dist = _att_mask_bias(n, dilation)
        stats = jnp.zeros((ATT_BLK, LANES), F32)
        for pair in range(pp):
            cols = slice(pair * LANES, (pair + 1) * LANES)
            qn, _, _ = _qk_norm(q_ref[:, cols].astype(F32), gq_ref[0:1, :])
            kn, _, _ = _qk_norm(jnp.concatenate([kp_ref[:, cols], kc_ref[:, cols]], axis=0).astype(F32),
                                gk_ref[0:1, :])
            kn16 = kn.astype(BF16)
            v16 = jnp.concatenate([vp_ref[:, cols], vc_ref[:, cols]], axis=0)
            outs, lses = [], []
            for hh in range(2):
                hmask = (lane // ATT_HEAD_DIM) == hh
                qh = jnp.where(hmask, qn, 0.0).astype(BF16)
                s = lax.dot_general(qh, kn16, (((1,), (1,)), ((), ())), preferred_element_type=F32) * scale
                slope = _head_col(sl_ref[pair, 0:1, :], hmask)
                logits = jnp.where(valid, s - slope * dist, NEG)
                mx = jnp.max(logits, axis=-1, keepdims=True)
                pexp = jnp.exp(logits - mx)
                den = jnp.sum(pexp, axis=-1, keepdims=True)
                outs.append(jnp.dot(pexp.astype(BF16), v16, preferred_element_type=F32) / den)
                stats = jnp.where(lane == 2 * pair + hh, mx + jnp.log(den), stats)
            o_ref[:, cols] = jnp.where(first, outs[0], outs[1]).astype(BF16)
        l_ref[...] = stats

    out_spec = pl.BlockSpec((None, ATT_BLK, pp * LANES), lambda r, hp, n: (r, n, hp))
    stat_spec = pl.BlockSpec((None, ATT_BLK, LANES), lambda r, hp, n: (r, n, hp))
    vec_spec = pl.BlockSpec((SUBLANES, LANES), lambda r, hp, n: (0, 0))
    o, lse = pl.pallas_call(
        body, name=f"att_fwd_g{g}",
        out_shape=(jax.ShapeDtypeStruct((dilation, lu, hd), BF16),
                   jax.ShapeDtypeStruct((dilation, lu, hpn // pp * LANES), F32)),
        grid=(dilation, hpn // pp, nb),
        in_specs=_att_specs(base, hd, nb, pp) + [
            vec_spec, vec_spec, pl.BlockSpec((pp, SUBLANES, LANES), lambda r, hp, n: (hp, 0, 0))],
        out_specs=(out_spec, stat_spec),
        compiler_params=pltpu.CompilerParams(dimension_semantics=("parallel", "parallel", "arbitrary")),
    )(*([qkv_r] * 5), gq2, gk2, slopes)
    return _from_residues(o), _from_residues(lse)


def _att_merge(outs, lses):
    t, hd = outs[0].shape
    sw = lses[0].shape[1]
    pp = hd // sw
    tr = _pick(t, (256, 128))
    ng = len(outs)

    def body(*refs):
        o_refs, l_refs, o16_ref, lt_ref = refs[:ng], refs[ng:2 * ng], refs[2 * ng], refs[2 * ng + 1]
        lane = lax.broadcasted_iota(jnp.int32, (1, LANES), 1)
        first = (lane // ATT_HEAD_DIM) == 0
        for blk in range(sw // LANES):
            scols = slice(blk * LANES, (blk + 1) * LANES)
            stats = jnp.zeros((tr, LANES), F32)
            for pair in range(pp):
                cols = slice((blk * pp + pair) * LANES, (blk * pp + pair + 1) * LANES)
                weights = []
                for hh in range(2):
                    pick = lane == 2 * pair + hh
                    ls = [_head_col(r[:, scols], pick) for r in l_refs]
                    mx = functools.reduce(jnp.maximum, ls)
                    es = [jnp.exp(l - mx) for l in ls]
                    den = functools.reduce(jnp.add, es)
                    weights.append([e / den for e in es])
                    stats = jnp.where(pick, mx + jnp.log(den), stats)
                acc = jnp.zeros((tr, LANES), F32)
                for gi in range(ng):
                    acc = acc + jnp.where(first, weights[0][gi], weights[1][gi]) * o_refs[gi][:, cols].astype(F32)
                o16_ref[:, cols] = acc.astype(BF16)
            lt_ref[:, scols] = stats

    spec = pl.BlockSpec((tr, hd), lambda i: (i, 0))
    sspec = pl.BlockSpec((tr, sw), lambda i: (i, 0))
    return pl.pallas_call(
        body, name="att_merge",
        out_shape=(jax.ShapeDtypeStruct((t, hd), BF16), jax.ShapeDtypeStruct((t, sw), F32)), grid=(t // tr,),
        in_specs=[spec] * ng + [sspec] * ng, out_specs=(spec, sspec),
        compiler_params=pltpu.CompilerParams(dimension_semantics=("parallel",)),
    )(*outs, *lses)


def _att_bwd_prep(do, o16, lse_tot):
    t, hd = do.shape
    sw = lse_tot.shape[1]
    pp = hd // sw
    tr = _pick(t, (256, 128))

    def body(do_ref, o_ref, l_ref, d16_ref, st_ref):
        lane = lax.broadcasted_iota(jnp.int32, (1, LANES), 1)
        d16_ref[...] = do_ref[...].astype(BF16)
        for blk in range(sw // LANES):
            scols = slice(blk * LANES, (blk + 1) * LANES)
            stats = l_ref[:, scols]
            for pair in range(pp):
                cols = slice((blk * pp + pair) * LANES, (blk * pp + pair + 1) * LANES)
                prod = do_ref[:, cols] * o_ref[:, cols].astype(F32)
                for hh in range(2):
                    hmask = (lane // ATT_HEAD_DIM) == hh
                    delta = jnp.sum(jnp.where(hmask, prod, 0.0), axis=-1, keepdims=True)
                    stats = jnp.where(lane == DELTA_LANE + 2 * pair + hh, delta, stats)
            st_ref[:, scols] = stats

    spec = pl.BlockSpec((tr, hd), lambda i: (i, 0))
    sspec = pl.BlockSpec((tr, sw), lambda i: (i, 0))
    return pl.pallas_call(
        body, name="att_bwd_prep",
        out_shape=(jax.ShapeDtypeStruct((t, hd), BF16), jax.ShapeDtypeStruct((t, sw), F32)), grid=(t // tr,),
        in_specs=[spec, spec, sspec], out_specs=(spec, sspec),
        compiler_params=pltpu.CompilerParams(dimension_semantics=("parallel",)),
    )(do, o16, lse_tot)


def _gather_dqkv(buf, parts, g, n_groups):
    dilation, lu, hd = parts[0].shape
    t = dilation * lu
    tr = _pick(t, (RELAYOUT_ROWS,))
    per = tr // dilation

    def body(*refs):
        o_ref, s_ref = refs[-2:]
        for sec, y_ref in enumerate(refs[:3]):
            for c in range(hd // LANES):
                lanes = slice(c * LANES, (c + 1) * LANES)
                for r in range(dilation):
                    s_ref[pl.ds(r, per, stride=dilation), :] = y_ref[r, :, lanes].astype(F32)
                o_ref[:, sec * hd + c * LANES:sec * hd + (c + 1) * LANES] = s_ref[...].astype(o_ref.dtype)

    part_spec = pl.BlockSpec((dilation, per, hd), lambda i: (0, i, 0))
    return pl.pallas_call(
        body, name=f"gather_dqkv_g{g}", out_shape=jax.ShapeDtypeStruct((t, 3 * hd * n_groups), parts[0].dtype),
        grid=(t // tr,),
        in_specs=[part_spec] * 3 + ([] if buf is None else [pl.BlockSpec(memory_space=pl.ANY)]),
        out_specs=pl.BlockSpec((tr, 3 * hd), lambda i: (i, g)),
        scratch_shapes=[pltpu.VMEM((tr, LANES), F32)],
        input_output_aliases={} if buf is None else {3: 0},
        compiler_params=pltpu.CompilerParams(dimension_semantics=("parallel",)),
    )(*parts, *(() if buf is None else (buf,)))


def _att_group_bwd(qkv_r, base, hd, gq2, gk2, slopes, stats, do16, g):
    dilation, lu, _ = qkv_r.shape
    nb = lu // ATT_BLK
    hpn = hd // LANES
    pp = math.gcd(ATT_PAIRS, hpn)
    hbn = hpn // pp
    scale = 1.0 / math.sqrt(ATT_HEAD_DIM)

    def body(q_ref, kp_ref, kc_ref, vp_ref, vc_ref, gq_ref, gk_ref, sl_ref, st_ref, do_ref,
             dq_ref, dk_ref, dv_ref, dgq_ref, dgk_ref, ck_ref, cv_ref):
        n = pl.program_id(2)
        lane = lax.broadcasted_iota(jnp.int32, (1, LANES), 1)
        gq, gk = gq_ref[0:1, :], gk_ref[0:1, :]

        @pl.when(n == 0)
        def _():
            ck_ref[...] = jnp.zeros_like(ck_ref)
            cv_ref[...] = jnp.zeros_like(cv_ref)
            dgq_ref[...] = jnp.zeros_like(dgq_ref)
            dgk_ref[...] = jnp.zeros_like(dgk_ref)

        @pl.when(n < nb)
        def _():
            valid, dist = _att_mask_bias(n, dilation)
            stats = st_ref[...]
            gq_sum = jnp.zeros((1, LANES), F32)
            gk_sum = jnp.zeros((1, LANES), F32)
            for pair in range(pp):
                cols = slice(pair * LANES, (pair + 1) * LANES)
                qn, qhat, qr = _qk_norm(q_ref[:, cols].astype(F32), gq)
                kn, khat, kr = _qk_norm(jnp.concatenate([kp_ref[:, cols], kc_ref[:, cols]], axis=0).astype(F32), gk)
                kn16 = kn.astype(BF16)
                v16 = jnp.concatenate([vp_ref[:, cols], vc_ref[:, cols]], axis=0)
                dov = do_ref[:, cols]
                dq_acc = jnp.zeros((ATT_BLK, LANES), F32)
                dk_acc = jnp.zeros((2 * ATT_BLK, LANES), F32)
                dv_acc = jnp.zeros((2 * ATT_BLK, LANES), F32)
                for hh in range(2):
                    hmask = (lane // ATT_HEAD_DIM) == hh
                    qh = jnp.where(hmask, qn, 0.0).astype(BF16)
                    doh = jnp.where(hmask, dov, jnp.zeros_like(dov))
                    s = lax.dot_general(qh, kn16, (((1,), (1,)), ((), ())), preferred_element_type=F32) * scale
                    slope = _head_col(sl_ref[pair, 0:1, :], hmask)
                    lse = _head_col(stats, lane == 2 * pair + hh)
                    delta = _head_col(stats, lane == DELTA_LANE + 2 * pair + hh)
                    pr = jnp.exp(jnp.where(valid, s - slope * dist - lse, NEG))
                    dp = lax.dot_general(doh, v16, (((1,), (1,)), ((), ())), preferred_element_type=F32)
                    ds = (pr * (dp - delta) * scale).astype(BF16)
                    dq_acc = dq_acc + jnp.where(hmask, jnp.dot(ds, kn16, preferred_element_type=F32), 0.0)
                    dk_acc = dk_acc + lax.dot_general(ds, qh, (((0,), (0,)), ((), ())), preferred_element_type=F32)
                    dv_acc = dv_acc + lax.dot_general(pr.astype(BF16), doh, (((0,), (0,)), ((), ())),
                                                      preferred_element_type=F32)
                dq_raw, gq_part = _qk_norm_bwd(dq_acc, qhat, qr, gq)
                gq_sum = gq_sum + jnp.sum(gq_part, axis=0, keepdims=True)
                dq_ref[:, cols] = dq_raw.astype(dq_ref.dtype)
                dk_raw, gk_part = _qk_norm_bwd(ck_ref[:, cols] + dk_acc[:ATT_BLK], khat[:ATT_BLK], kr[:ATT_BLK], gk)
                gk_sum = gk_sum + jnp.sum(gk_part, axis=0, keepdims=True)
                dk_ref[:, cols] = dk_raw.astype(dk_ref.dtype)
                dv_ref[:, cols] = (cv_ref[:, cols] + dv_acc[:ATT_BLK]).astype(dv_ref.dtype)
                ck_ref[:, cols] = dk_acc[ATT_BLK:]
                cv_ref[:, cols] = dv_acc[ATT_BLK:]
            dgq_ref[...] += jnp.broadcast_to(gq_sum, dgq_ref.shape)
            dgk_ref[...] += jnp.broadcast_to(gk_sum, dgk_ref.shape)

        @pl.when(n == nb)
        def _():
            gk_sum = jnp.zeros((1, LANES), F32)
            for pair in range(pp):
                cols = slice(pair * LANES, (pair + 1) * LANES)
                _, khat, kr = _qk_norm(kp_ref[:, cols].astype(F32), gk)
                dk_raw, gk_part = _qk_norm_bwd(ck_ref[:, cols], khat, kr, gk)
                gk_sum = gk_sum + jnp.sum(gk_part, axis=0, keepdims=True)
                dk_ref[:, cols] = dk_raw.astype(dk_ref.dtype)
                dv_ref[:, cols] = cv_ref[:, cols].astype(dv_ref.dtype)
            dgk_ref[...] += jnp.broadcast_to(gk_sum, dgk_ref.shape)

    width = pp * LANES
    q_out = pl.BlockSpec((None, ATT_BLK, width), lambda r, hp, n: (r, jnp.minimum(n, nb - 1), hp))
    kv_out = pl.BlockSpec((None, ATT_BLK, width), lambda r, hp, n: (r, jnp.maximum(n - 1, 0), hp))
    st_spec = pl.BlockSpec((None, ATT_BLK, LANES), lambda r, hp, n: (r, jnp.minimum(n, nb - 1), hp))
    vec_spec = pl.BlockSpec((SUBLANES, LANES), lambda r, hp, n: (0, 0))
    g_out = pl.BlockSpec((None, SUBLANES, LANES), lambda r, hp, n: (r * hbn + hp, 0, 0))
    shp = jax.ShapeDtypeStruct((dilation, lu, hd), BF16)
    gshp = jax.ShapeDtypeStruct((dilation * hbn, SUBLANES, LANES), F32)
    return pl.pallas_call(
        body, name=f"att_bwd_g{g}", out_shape=(shp, shp, shp, gshp, gshp), grid=(dilation, hbn, nb + 1),
        in_specs=_att_specs(base, hd, nb, pp) + [
            vec_spec, vec_spec, pl.BlockSpec((pp, SUBLANES, LANES), lambda r, hp, n: (hp, 0, 0)), st_spec, q_out],
        out_specs=(q_out, kv_out, kv_out, g_out, g_out),
        scratch_shapes=[pltpu.VMEM((ATT_BLK, width), F32), pltpu.VMEM((ATT_BLK, width), F32)],
        compiler_params=pltpu.CompilerParams(dimension_semantics=("parallel", "parallel", "arbitrary")),
    )(*([qkv_r] * 5), gq2, gk2, slopes, _to_residues(stats, dilation), _to_residues(do16, dilation))


def _att_consts(q_gain, k_gain, hd):
    heads = hd // ATT_HEAD_DIM
    gq2 = jnp.broadcast_to(jnp.tile(q_gain, 2)[None], (SUBLANES, LANES))
    gk2 = jnp.broadcast_to(jnp.tile(k_gain, 2)[None], (SUBLANES, LANES))
    sl = 2.0 ** (-8.0 * jnp.arange(1, heads + 1, dtype=F32) / heads)
    slopes = jnp.broadcast_to(jnp.repeat(sl, ATT_HEAD_DIM).reshape(hd // LANES, 1, LANES), (hd // LANES, SUBLANES, LANES))
    return gq2, gk2, slopes


def _attention_core_fwd(qkv, q_gain, k_gain):
    hd = qkv.shape[1] // (3 * len(DIL_PATTERNS))
    gq2, gk2, slopes = _att_consts(q_gain, k_gain, hd)
    outs, lses, views = [], [], []
    for g, (_, dilation) in enumerate(DIL_PATTERNS):
        base = g * 3 * hd if dilation == 1 else 0
        qkv_r = _to_residues(qkv, dilation) if dilation == 1 else _to_residues(qkv, dilation, g * 3 * hd, 3 * hd)
        o_g, l_g = _att_group_fwd(qkv_r, base, hd, gq2, gk2, slopes, g)
        outs.append(o_g)
        lses.append(l_g)
        views.append((qkv_r, base))
    o16, lse_tot = _att_merge(outs, lses)
    return o16, (views, q_gain, k_gain, o16, lse_tot)


def _attention_core_bwd(res, do):
    views, q_gain, k_gain, o16, lse_tot = res
    hd = o16.shape[1]
    gq2, gk2, slopes = _att_consts(q_gain, k_gain, hd)
    do16, stats = _att_bwd_prep(do, o16, lse_tot)
    dqkv, dgq, dgk = None, 0.0, 0.0
    for g, (qkv_r, base) in enumerate(views):
        dq, dk, dv, a, b = _att_group_bwd(qkv_r, base, hd, gq2, gk2, slopes, stats, do16, g)
        dqkv = _gather_dqkv(dqkv, (dq, dk, dv), g, len(views))
        dgq = dgq + a[:, 0, :].reshape(-1, ATT_HEAD_DIM).sum(0)
        dgk = dgk + b[:, 0, :].reshape(-1, ATT_HEAD_DIM).sum(0)
    return dqkv, dgq, dgk


HALO = 8


def _silu(v):
    return v * jax.nn.sigmoid(v)


def _silu_grad(v):
    s = jax.nn.sigmoid(v)
    return s * (1.0 + v * (1.0 - s))


def _conv_fwd(zx, conv_w, conv_b, d_inner):
    t = zx.shape[0]
    conv_dim = conv_w.shape[1]
    cb = _pick(d_inner, (1024, 512, 256, 128))
    assert conv_dim % cb == 0
    tr = _pick(t, (256, 128))
    off = d_inner // cb

    def body(x_ref, h_ref, w_ref, b_ref, o_ref):
        i = pl.program_id(1)
        halo = jnp.where(i > 0, h_ref[...], 0.0)
        ext = jnp.concatenate([halo, x_ref[...]], axis=0)
        acc = jnp.broadcast_to(b_ref[...], (tr, cb))
        for k in range(CONV_WIDTH):
            s = CONV_WIDTH - 1 - k
            sh = ext if s == 0 else pltpu.roll(ext, shift=s, axis=0)
            acc = acc + w_ref[k:k + 1, :] * sh[HALO:HALO + tr]
        o_ref[...] = acc

    return pl.pallas_call(
        body, name="ssm_conv_fwd", out_shape=jax.ShapeDtypeStruct((t, conv_dim), F32),
        grid=(conv_dim // cb, t // tr),
        in_specs=[pl.BlockSpec((tr, cb), lambda j, i: (i, off + j)),
                  pl.BlockSpec((HALO, cb), lambda j, i: (jnp.maximum(i * (tr // HALO) - 1, 0), off + j)),
                  pl.BlockSpec((CONV_WIDTH, cb), lambda j, i: (0, j)),
                  pl.BlockSpec((1, cb), lambda j, i: (0, j))],
        out_specs=pl.BlockSpec((tr, cb), lambda j, i: (i, j)),
        compiler_params=pltpu.CompilerParams(dimension_semantics=("parallel", "parallel")),
    )(zx, zx, conv_w, conv_b.reshape(1, -1))


def _conv_bwd(zx, conv_w, dpre, dzx, d_inner):
    t, width = zx.shape
    conv_dim = conv_w.shape[1]
    cb = _pick(d_inner, (1024, 512, 256, 128))
    tr = _pick(t, (256, 128))
    off = d_inner // cb
    nr = t // tr

    def body(x_ref, h_ref, w_ref, d_ref, dn_ref, dzx_in, dx_ref, dw_ref, db_ref):
        i = pl.program_id(1)

        @pl.when(i == 0)
        def _():
            dw_ref[...] = jnp.zeros_like(dw_ref)
            db_ref[...] = jnp.zeros_like(db_ref)

        halo = jnp.where(i > 0, h_ref[...], 0.0)
        ext = jnp.concatenate([halo, x_ref[...]], axis=0)
        d = d_ref[...]
        dext = jnp.concatenate([d, jnp.where(i < nr - 1, dn_ref[...], 0.0)], axis=0)
        dx = jnp.zeros((tr, cb), F32)
        for k in range(CONV_WIDTH):
            s = CONV_WIDTH - 1 - k
            fut = dext if s == 0 else pltpu.roll(dext, shift=tr + HALO - s, axis=0)
            dx = dx + w_ref[k:k + 1, :] * fut[:tr]
            past = ext if s == 0 else pltpu.roll(ext, shift=s, axis=0)
            dw_ref[k:k + 1, :] += jnp.sum(d * past[HALO:HALO + tr], axis=0, keepdims=True)
        dx_ref[...] = dx.astype(dx_ref.dtype)
        db_ref[...] += jnp.sum(d, axis=0, keepdims=True)

    last_halo = t // HALO - 1
    return pl.pallas_call(
        body, name="ssm_conv_bwd",
        out_shape=(jax.ShapeDtypeStruct(dzx.shape, dzx.dtype), jax.ShapeDtypeStruct((CONV_WIDTH, conv_dim), F32),
                   jax.ShapeDtypeStruct((1, conv_dim), F32)),
        grid=(conv_dim // cb, nr),
        in_specs=[pl.BlockSpec((tr, cb), lambda j, i: (i, off + j)),
                  pl.BlockSpec((HALO, cb), lambda j, i: (jnp.maximum(i * (tr // HALO) - 1, 0), off + j)),
                  pl.BlockSpec((CONV_WIDTH, cb), lambda j, i: (0, j)),
                  pl.BlockSpec((tr, cb), lambda j, i: (i, j)),
                  pl.BlockSpec((HALO, cb), lambda j, i: (jnp.minimum((i + 1) * (tr // HALO), last_halo), j)),
                  pl.BlockSpec(memory_space=pl.ANY)],
        out_specs=(pl.BlockSpec((tr, cb), lambda j, i: (i, off + j)),
                   pl.BlockSpec((CONV_WIDTH, cb), lambda j, i: (0, j)),
                   pl.BlockSpec((1, cb), lambda j, i: (0, j))),
        input_output_aliases={5: 0},
        compiler_params=pltpu.CompilerParams(dimension_semantics=("parallel", "arbitrary")),
    )(zx, zx, conv_w, dpre, dpre, dzx)


def _eye(n):
    return lax.broadcasted_iota(jnp.int32, (n, n), 0) == lax.broadcasted_iota(jnp.int32, (n, n), 1)


def _row_to_col(row):
    n = row.shape[1]
    return jnp.sum(jnp.where(_eye(n), row, 0.0), axis=1, keepdims=True)


def _col_to_row(col):
    n = col.shape[0]
    return jnp.sum(jnp.where(_eye(n), col, 0.0), axis=0, keepdims=True)


def _pair_lanes(c0, c1):
    lane = lax.broadcasted_iota(jnp.int32, (1, LANES), 1)
    return jnp.where(lane < SSM_HEAD_DIM, c0, c1)


def _ssd_chunk_common(pre_x_ref, pre_b_ref, pre_c_ref, dtr_ref, bias_ref, alog_ref, cs_ref):
    cl = SSD_CHUNK
    hpg = dtr_ref.shape[0]
    x = _silu(pre_x_ref[...])
    b16 = _silu(pre_b_ref[...]).astype(BF16)
    c16 = _silu(pre_c_ref[...]).astype(BF16)
    dt = jax.nn.softplus(dtr_ref[...] + bias_ref[...])
    a = -jnp.exp(alog_ref[...])
    li = lax.broadcasted_iota(jnp.int32, (cl, cl), 0)
    si = lax.broadcasted_iota(jnp.int32, (cl, cl), 1)
    upper = (li <= si).astype(F32)
    cs_ref[0:hpg, :] = jnp.dot(dt * a, upper, precision=lax.Precision.HIGHEST, preferred_element_type=F32)
    cs_ref[hpg:2 * hpg, :] = dt
    g = lax.dot_general(c16, b16, (((1,), (1,)), ((), ())), preferred_element_type=F32)
    return x, b16, c16, dt, a, g, li >= si


def _ssd_fwd(pre, dtT, bias, alog, dskip_lanes, d_inner):
    t = pre.shape[0]
    cl = SSD_CHUNK
    nc = t // cl
    ng = SSM_GROUPS
    hpg = dtT.shape[1]
    gw = hpg * SSM_HEAD_DIM
    assert d_inner == ng * gw and hpg % 2 == 0
    bo = d_inner // SSM_STATE

    def body(px_ref, pb_ref, pc_ref, dtr_ref, bias_ref, alog_ref, dsk_ref, y_ref, st_ref, s_ref, cs_ref):
        c = pl.program_id(1)

        @pl.when(c == 0)
        def _():
            s_ref[...] = jnp.zeros_like(s_ref)

        x, b16, c16, dt, a, g, causal = _ssd_chunk_common(px_ref, pb_ref, pc_ref, dtr_ref, bias_ref, alog_ref, cs_ref)
        st_ref[...] = s_ref[...]
        yoff = lax.dot_general(c16, s_ref[...].astype(BF16), (((1,), (1,)), ((), ())), preferred_element_type=F32)
        xde_parts = []
        for j in range(hpg // 2):
            cols = slice(j * LANES, (j + 1) * LANES)
            xp = x[:, cols]
            dcol, ecol, ocol, ms = [], [], [], []
            for hh in range(2):
                h = 2 * j + hh
                cs_row = cs_ref[h:h + 1, :]
                cs_col = _row_to_col(cs_row)
                dcol.append(_row_to_col(cs_ref[hpg + h:hpg + h + 1, :]))
                ecol.append(jnp.exp(cs_ref[h:h + 1, cl - 1:cl] - cs_col))
                ocol.append(jnp.exp(cs_col))
                lm = jnp.where(causal, jnp.exp(jnp.minimum(cs_col - cs_row, 0.0)), 0.0)
                ms.append((g * lm).astype(BF16))
            xd = xp * _pair_lanes(dcol[0], dcol[1])
            xd16 = xd.astype(BF16)
            yd = _pair_lanes(1.0, 0.0) * jnp.dot(ms[0], xd16, preferred_element_type=F32) \
                + _pair_lanes(0.0, 1.0) * jnp.dot(ms[1], xd16, preferred_element_type=F32)
            y_ref[:, cols] = yd + yoff[:, cols] * _pair_lanes(ocol[0], ocol[1]) + xp * dsk_ref[0:1, cols]
            xde_parts.append((xd * _pair_lanes(ecol[0], ecol[1])).astype(BF16))
        new = lax.dot_general(jnp.concatenate(xde_parts, axis=1), b16, (((0,), (0,)), ((), ())),
                              preferred_element_type=F32)
        for h in range(hpg):
            rows = slice(h * SSM_HEAD_DIM, (h + 1) * SSM_HEAD_DIM)
            s_ref[rows, :] = s_ref[rows, :] * jnp.exp(cs_ref[h:h + 1, cl - 1:cl]) + new[rows, :]

    vec = lambda n: pl.BlockSpec((None, hpg, n), lambda gi, c: (gi, 0, 0))
    return pl.pallas_call(
        body, name="ssd_fwd",
        out_shape=(jax.ShapeDtypeStruct((t, d_inner), F32), jax.ShapeDtypeStruct((ng, nc, gw, SSM_STATE), F32)),
        grid=(ng, nc),
        in_specs=[pl.BlockSpec((cl, gw), lambda gi, c: (c, gi)),
                  pl.BlockSpec((cl, SSM_STATE), lambda gi, c: (c, bo + gi)),
                  pl.BlockSpec((cl, SSM_STATE), lambda gi, c: (c, bo + ng + gi)),
                  pl.BlockSpec((None, hpg, cl), lambda gi, c: (gi, 0, c)),
                  vec(1), vec(1),
                  pl.BlockSpec((1, gw), lambda gi, c: (0, gi))],
        out_specs=(pl.BlockSpec((cl, gw), lambda gi, c: (c, gi)),
                   pl.BlockSpec((None, None, gw, SSM_STATE), lambda gi, c: (gi, c, 0, 0))),
        scratch_shapes=[pltpu.VMEM((gw, SSM_STATE), F32), pltpu.VMEM((2 * hpg, cl), F32)],
        compiler_params=pltpu.CompilerParams(dimension_semantics=("parallel", "arbitrary")),
    )(pre, pre, pre, dtT, bias, alog, dskip_lanes)


def _ssd_bwd(pre, dtT, bias, alog, dskip_lanes, states, dy, d_inner):
    t, conv_dim = pre.shape
    cl = SSD_CHUNK
    nc = t // cl
    ng = SSM_GROUPS
    hpg = dtT.shape[1]
    gw = hpg * SSM_HEAD_DIM
    bo = d_inner // SSM_STATE

    def body(px_ref, pb_ref, pc_ref, dtr_ref, bias_ref, alog_ref, dsk_ref, st_ref, dy_ref,
             dx_ref, db_ref, dc_ref, ddt_ref, acc_ref, dsk_out, ds_ref, cs_ref, dcs_ref):
        c = pl.program_id(1)

        @pl.when(c == 0)
        def _():
            ds_ref[...] = jnp.zeros_like(ds_ref)
            acc_ref[...] = jnp.zeros_like(acc_ref)
            dsk_out[...] = jnp.zeros_like(dsk_out)

        x, b16, c16, dt, a, g, causal = _ssd_chunk_common(px_ref, pb_ref, pc_ref, dtr_ref, bias_ref, alog_ref, cs_ref)
        s_prev = st_ref[...]
        s16 = s_prev.astype(BF16)
        ds = ds_ref[...]
        ds16 = ds.astype(BF16)
        dyv = dy_ref[...]
        yoff = lax.dot_general(c16, s16, (((1,), (1,)), ((), ())), preferred_element_type=F32)
        bds = lax.dot_general(b16, ds16, (((1,), (1,)), ((), ())), preferred_element_type=F32)
        dg = jnp.zeros((cl, cl), F32)
        xde_parts, dye_parts = [], []
        lane = lax.broadcasted_iota(jnp.int32, (1, LANES), 1)
        for j in range(hpg // 2):
            cols = slice(j * LANES, (j + 1) * LANES)
            xp, dyp = x[:, cols], dyv[:, cols]
            dcol, ecol, ocol, lms = [], [], [], []
            for hh in range(2):
                h = 2 * j + hh
                cs_row = cs_ref[h:h + 1, :]
                cs_col = _row_to_col(cs_row)
                dcol.append(_row_to_col(cs_ref[hpg + h:hpg + h + 1, :]))
                ecol.append(jnp.exp(cs_ref[h:h + 1, cl - 1:cl] - cs_col))
                ocol.append(jnp.exp(cs_col))
                lms.append(jnp.where(causal, jnp.exp(jnp.minimum(cs_col - cs_row, 0.0)), 0.0))
            dlanes, elanes, olanes = _pair_lanes(*dcol), _pair_lanes(*ecol), _pair_lanes(*ocol)
            xd = xp * dlanes
            xd16 = xd.astype(BF16)
            xde = xd * elanes
            yoffp = yoff[:, cols] * olanes
            bdsp = bds[:, cols]
            dxd = bdsp * elanes
            for hh in range(2):
                h = 2 * j + hh
                hmask = (lane // SSM_HEAD_DIM) == hh
                dyh16 = jnp.where(hmask, dyp, 0.0).astype(BF16)
                m = g * lms[hh]
                dm = lax.dot_general(dyh16, xd16, (((1,), (1,)), ((), ())), preferred_element_type=F32)
                w = dm * m
                dg = dg + dm * lms[hh]
                dxd = dxd + lax.dot_general(m.astype(BF16), dyh16, (((0,), (0,)), ((), ())),
                                            preferred_element_type=F32)
                term = jnp.sum(jnp.where(hmask, xde * bdsp, 0.0), axis=1, keepdims=True)
                dcs_col = (jnp.sum(w, axis=1, keepdims=True)
                           + jnp.sum(jnp.where(hmask, dyp * yoffp, 0.0), axis=1, keepdims=True) - term)
                rows = slice(h * SSM_HEAD_DIM, (h + 1) * SSM_HEAD_DIM)
                dec = jnp.exp(cs_ref[h:h + 1, cl - 1:cl])
                tail = jnp.sum(term, axis=0, keepdims=True) + dec * jnp.sum(
                    jnp.sum(s_prev[rows, :] * ds[rows, :], axis=1, keepdims=True), axis=0, keepdims=True)
                last = lax.broadcasted_iota(jnp.int32, (1, cl), 1) == cl - 1
                dcs_ref[h:h + 1, :] = _col_to_row(dcs_col) - jnp.sum(w, axis=0, keepdims=True) + jnp.where(last, tail, 0.0)
                dcs_ref[hpg + h:hpg + h + 1, :] = _col_to_row(
                    jnp.sum(jnp.where(hmask, dxd * xp, 0.0), axis=1, keepdims=True))
            dx_act = dxd * dlanes + dyp * dsk_ref[0:1, cols]
            dx_ref[:, cols] = dx_act * _silu_grad(px_ref[:, cols])
            dsk_out[0:1, cols] += jnp.sum(dyp * xp, axis=0, keepdims=True)
            xde_parts.append(xde.astype(BF16))
            dye_parts.append((dyp * olanes).astype(BF16))
        xde16 = jnp.concatenate(xde_parts, axis=1)
        dye16 = jnp.concatenate(dye_parts, axis=1)
        dg16 = dg.astype(BF16)
        dc_act = jnp.dot(dg16, b16, preferred_element_type=F32) + jnp.dot(dye16, s16, preferred_element_type=F32)
        db_act = lax.dot_general(dg16, c16, (((0,), (0,)), ((), ())), preferred_element_type=F32) \
            + jnp.dot(xde16, ds16, preferred_element_type=F32)
        dc_ref[...] = dc_act * _silu_grad(pc_ref[...])
        db_ref[...] = db_act * _silu_grad(pb_ref[...])
        ds_new = lax.dot_general(dye16, c16, (((0,), (0,)), ((), ())), preferred_element_type=F32)
        for h in range(hpg):
            rows = slice(h * SSM_HEAD_DIM, (h + 1) * SSM_HEAD_DIM)
            ds_ref[rows, :] = ds[rows, :] * jnp.exp(cs_ref[h:h + 1, cl - 1:cl]) + ds_new[rows, :]
        li = lax.broadcasted_iota(jnp.int32, (cl, cl), 0)
        si = lax.broadcasted_iota(jnp.int32, (cl, cl), 1)
        d_adt = jnp.dot(dcs_ref[0:hpg, :], (li >= si).astype(F32), precision=lax.Precision.HIGHEST,
                        preferred_element_type=F32)
        ddt = d_adt * a + dcs_ref[hpg:2 * hpg, :]
        ddt_raw = ddt * jax.nn.sigmoid(dtr_ref[...] + bias_ref[...])
        ddt_ref[...] = ddt_raw
        acc_ref[0:hpg, :] += d_adt * dt
        acc_ref[hpg:2 * hpg, :] += ddt_raw

    rc = lambda c: nc - 1 - c
    vec = lambda n: pl.BlockSpec((None, hpg, n), lambda gi, c: (gi, 0, 0))
    x_spec = pl.BlockSpec((cl, gw), lambda gi, c: (rc(c), gi))
    b_spec = pl.BlockSpec((cl, SSM_STATE), lambda gi, c: (rc(c), bo + gi))
    c_spec = pl.BlockSpec((cl, SSM_STATE), lambda gi, c: (rc(c), bo + ng + gi))
    dt_spec = pl.BlockSpec((None, hpg, cl), lambda gi, c: (gi, 0, rc(c)))
    return pl.pallas_call(
        body, name="ssd_bwd",
        out_shape=(jax.ShapeDtypeStruct((t, d_inner), F32), jax.ShapeDtypeStruct((t, ng * SSM_STATE), F32),
                   jax.ShapeDtypeStruct((t, ng * SSM_STATE), F32), jax.ShapeDtypeStruct(dtT.shape, F32),
                   jax.ShapeDtypeStruct((ng, 2 * hpg, cl), F32), jax.ShapeDtypeStruct((1, d_inner), F32)),
        grid=(ng, nc),
        in_specs=[x_spec, b_spec, c_spec, dt_spec, vec(1), vec(1),
                  pl.BlockSpec((1, gw), lambda gi, c: (0, gi)),
                  pl.BlockSpec((None, None, gw, SSM_STATE), lambda gi, c: (gi, rc(c), 0, 0)),
                  x_spec],
        out_specs=(x_spec, pl.BlockSpec((cl, SSM_STATE), lambda gi, c: (rc(c), gi)),
                   pl.BlockSpec((cl, SSM_STATE), lambda gi, c: (rc(c), gi)), dt_spec,
                   pl.BlockSpec((None, 2 * hpg, cl), lambda gi, c: (gi, 0, 0)),
                   pl.BlockSpec((1, gw), lambda gi, c: (0, gi))),
        scratch_shapes=[pltpu.VMEM((gw, SSM_STATE), F32), pltpu.VMEM((2 * hpg, cl), F32),
                        pltpu.VMEM((2 * hpg, cl), F32)],
        compiler_params=pltpu.CompilerParams(dimension_semantics=("parallel", "arbitrary")),
    )(pre, pre, pre, dtT, bias, alog, dskip_lanes, states, dy)


def _gate_norm_fwd(y, zx, norm_w, d_inner):
    t = y.shape[0]
    tr = _pick(t, (256, 128))
    gs = d_inner // SSM_GROUPS

    def body(y_ref, z_ref, w_ref, o_ref):
        for gi in range(SSM_GROUPS):
            cols = slice(gi * gs, (gi + 1) * gs)
            v = y_ref[:, cols] * _silu(z_ref[:, cols])
            r = lax.rsqrt(jnp.mean(v * v, axis=-1, keepdims=True) + NORM_EPS)
            o_ref[:, cols] = (v * r * w_ref[0:1, cols]).astype(BF16)

    spec = pl.BlockSpec((tr, d_inner), lambda i: (i, 0))
    return pl.pallas_call(
        body, name="ssm_gate_norm_fwd", out_shape=jax.ShapeDtypeStruct((t, d_inner), BF16), grid=(t // tr,),
        in_specs=[spec, spec, pl.BlockSpec((1, d_inner), lambda i: (0, 0))], out_specs=spec,
        compiler_params=pltpu.CompilerParams(dimension_semantics=("parallel",)),
    )(y, zx, norm_w.reshape(1, -1))


def _gate_norm_bwd(y, zx, norm_w, dout, d_inner):
    t, width = zx.shape
    tr = _pick(t, (256, 128))
    gs = d_inner // SSM_GROUPS

    def body(y_ref, z_ref, w_ref, do_ref, dy_ref, dz_ref, dw_ref):
        @pl.when(pl.program_id(0) == 0)
        def _():
            dw_ref[...] = jnp.zeros_like(dw_ref)

        for gi in range(SSM_GROUPS):
            cols = slice(gi * gs, (gi + 1) * gs)
            yv, zv = y_ref[:, cols], z_ref[:, cols]
            sz = _silu(zv)
            v = yv * sz
            r = lax.rsqrt(jnp.mean(v * v, axis=-1, keepdims=True) + NORM_EPS)
            vhat = v * r
            dn = do_ref[:, cols].astype(F32)
            dw_ref[0:1, cols] += jnp.sum(dn * vhat, axis=0, keepdims=True)
            dvh = dn * w_ref[0:1, cols]
            dv = r * (dvh - vhat * jnp.mean(dvh * vhat, axis=-1, keepdims=True))
            dy_ref[:, cols] = dv * sz
            dz_ref[:, cols] = (dv * yv * _silu_grad(zv)).astype(dz_ref.dtype)

    spec = pl.BlockSpec((tr, d_inner), lambda i: (i, 0))
    wspec = pl.BlockSpec((1, d_inner), lambda i: (0, 0))
    return pl.pallas_call(
        body, name="ssm_gate_norm_bwd",
        out_shape=(jax.ShapeDtypeStruct((t, d_inner), F32), jax.ShapeDtypeStruct((t, width), zx.dtype),
                   jax.ShapeDtypeStruct((1, d_inner), F32)),
        grid=(t // tr,),
        in_specs=[spec, spec, wspec, spec], out_specs=(spec, spec, wspec),
        compiler_params=pltpu.CompilerParams(dimension_semantics=("arbitrary",)),
    )(y, zx, norm_w.reshape(1, -1), dout)


def _ssm_small(dt_raw, dt_bias, a_log, d_skip):
    heads = dt_raw.shape[1]
    hpg = heads // SSM_GROUPS
    dtT = dt_raw.T.reshape(SSM_GROUPS, hpg, -1)
    return (dtT, dt_bias.reshape(SSM_GROUPS, hpg, 1), a_log.reshape(SSM_GROUPS, hpg, 1),
            jnp.repeat(d_skip, SSM_HEAD_DIM).reshape(1, -1))


def _ssm_core_fwd(zx, dt_raw, conv_w, conv_b, dt_bias, a_log, d_skip, norm_w):
    d_inner = norm_w.shape[0]
    pre = _conv_fwd(zx, conv_w, conv_b, d_inner)
    dtT, bias, alog, dsk = _ssm_small(dt_raw, dt_bias, a_log, d_skip)
    y, states = _ssd_fwd(pre, dtT, bias, alog, dsk, d_inner)
    out = _gate_norm_fwd(y, zx, norm_w, d_inner)
    return out, (zx, dt_raw, conv_w, dt_bias, a_log, d_skip, norm_w, pre, y, states)


def _ssm_core_bwd(res, dout):
    zx, dt_raw, conv_w, dt_bias, a_log, d_skip, norm_w, pre, y, states = res
    d_inner = norm_w.shape[0]
    heads = dt_raw.shape[1]
    dy, dzx, dnorm = _gate_norm_bwd(y, zx, norm_w, dout, d_inner)
    dtT, bias, alog, dsk = _ssm_small(dt_raw, dt_bias, a_log, d_skip)
    dx, db, dc, ddtT, acc, dsk_l = _ssd_bwd(pre, dtT, bias, alog, dsk, states, dy, d_inner)
    dpre = jnp.concatenate([dx, db, dc], axis=1)
    dzx, dconv_w, dconv_b = _conv_bwd(zx, conv_w, dpre, dzx, d_inner)
    d_dt_raw = ddtT.reshape(heads, -1).T
    hpg = heads // SSM_GROUPS
    da = acc[:, :hpg].sum(-1).reshape(heads)
    d_bias = acc[:, hpg:].sum(-1).reshape(heads)
    d_alog = da * (-jnp.exp(a_log))
    d_dskip = dsk_l.reshape(heads, SSM_HEAD_DIM).sum(-1)
    return dzx, d_dt_raw, dconv_w, dconv_b.reshape(-1), d_bias, d_alog, d_dskip, dnorm.reshape(-1)


def _rows_call(body, name, ins, outs, acc_outs=(), rows=256):
    t = max(a.shape[0] for a in ins)
    tr = _pick(t, (rows, 128, 64, 32, 16, 8))

    def spec(a):
        if a.shape[0] == t:
            return pl.BlockSpec((tr, a.shape[1]), lambda i: (i, 0))
        return pl.BlockSpec(a.shape, lambda i: (0, 0))

    return pl.pallas_call(
        body, name=name, out_shape=tuple(outs) + tuple(acc_outs), grid=(t // tr,),
        in_specs=[spec(a) for a in ins],
        out_specs=tuple(spec(a) for a in outs) + tuple(pl.BlockSpec(a.shape, lambda i: (0, 0)) for a in acc_outs),
        compiler_params=pltpu.CompilerParams(dimension_semantics=("arbitrary" if acc_outs else "parallel",)),
    )(*ins)


def _rms_fwd(x, gain, after=None):
    def body(x_ref, g_ref, *rest):
        v = x_ref[...]
        rest[-1][...] = (v * lax.rsqrt(jnp.mean(v * v, axis=-1, keepdims=True) + NORM_EPS) * g_ref[...]).astype(BF16)

    ins = [x, gain.reshape(1, -1)] + ([] if after is None else [after])
    (h,) = _rows_call(body, "rms_fwd", ins, [jax.ShapeDtypeStruct(x.shape, BF16)])
    return h


def _rms_bwd(x, gain, dh, dres, after):
    def body(x_ref, g_ref, dh_ref, dr_ref, *rest):
        dx_ref, dg_ref = rest[-2:]

        @pl.when(pl.program_id(0) == 0)
        def _():
            dg_ref[...] = jnp.zeros_like(dg_ref)

        v = x_ref[...]
        r = lax.rsqrt(jnp.mean(v * v, axis=-1, keepdims=True) + NORM_EPS)
        vhat = v * r
        d = dh_ref[...].astype(F32)
        dg_ref[...] += jnp.sum(d * vhat, axis=0, keepdims=True)
        dvh = d * g_ref[...]
        dx_ref[...] = dr_ref[...] + r * (dvh - vhat * jnp.mean(dvh * vhat, axis=-1, keepdims=True))

    ins = [x, gain.reshape(1, -1), dh, dres] + ([] if after is None else [after])
    dx, dg = _rows_call(body, "rms_bwd", ins, [jax.ShapeDtypeStruct(x.shape, F32)],
                        [jax.ShapeDtypeStruct((1, x.shape[1]), F32)])
    return dx, dg.reshape(gain.shape)


def _swiglu_fwd(gu):
    t, f2 = gu.shape
    f = f2 // 2

    def body(gu_ref, o_ref):
        o_ref[...] = (_silu(gu_ref[:, :f].astype(F32)) * gu_ref[:, f:].astype(F32)).astype(BF16)

    (act,) = _rows_call(body, "swiglu_fwd", [gu], [jax.ShapeDtypeStruct((t, f), BF16)])
    return act


def _swiglu_bwd(gu, dact):
    t, f2 = gu.shape
    f = f2 // 2

    def body(gu_ref, d_ref, o_ref):
        g, u, d = gu_ref[:, :f].astype(F32), gu_ref[:, f:].astype(F32), d_ref[...].astype(F32)
        o_ref[:, :f] = (d * u * _silu_grad(g)).astype(BF16)
        o_ref[:, f:] = (d * _silu(g)).astype(BF16)

    (dgu,) = _rows_call(body, "swiglu_bwd", [gu, dact], [jax.ShapeDtypeStruct((t, f2), BF16)])
    return dgu


def _ple_fwd(x, gl, ple):
    def body(x_ref, g_ref, p_ref, o_ref):
        o_ref[...] = x_ref[...] + jax.nn.sigmoid(g_ref[...]) * p_ref[...]

    (out,) = _rows_call(body, "ple_fwd", [x, gl, ple], [jax.ShapeDtypeStruct(x.shape, F32)])
    return out


def _ple_bwd(gl, ple, dout):
    def body(g_ref, p_ref, d_ref, dg_ref, dp_ref):
        s, d = jax.nn.sigmoid(g_ref[...]), d_ref[...]
        dg_ref[...] = (d * p_ref[...] * s * (1.0 - s)).astype(BF16)
        dp_ref[...] = (d * s).astype(BF16)

    shp = jax.ShapeDtypeStruct(gl.shape, BF16)
    return _rows_call(body, "ple_bwd", [gl, ple, dout], [shp, shp])


def _loss_fwd(y, target):
    inv = 1.0 / y.shape[1]

    def body(y_ref, t_ref, d_ref, l_ref):
        @pl.when(pl.program_id(0) == 0)
        def _():
            l_ref[...] = jnp.zeros_like(l_ref)

        e = y_ref[...] - t_ref[...]
        d_ref[...] = e * inv
        part = jnp.sum(jnp.sum(e * e, axis=1, keepdims=True), axis=0, keepdims=True) * (0.5 * inv)
        l_ref[...] += jnp.broadcast_to(part, l_ref.shape)

    dy, acc = _rows_call(body, "loss_fwd", [y, target], [jax.ShapeDtypeStruct(y.shape, F32)],
                         [jax.ShapeDtypeStruct((SUBLANES, LANES), F32)])
    return acc[0, 0], dy


def rmsnorm(x, gain):
    y = x * lax.rsqrt(jnp.mean(x * x, axis=-1, keepdims=True) + NORM_EPS)
    return y * gain


def causal_depthwise_conv(u, w, bias):
    k_width, chans = w.shape
    out = lax.conv_general_dilated(u, w[:, None, :], window_strides=(1,), padding=[(k_width - 1, 0)],
                                   dimension_numbers=("NWC", "WIO", "NWC"), feature_group_count=chans)
    return out + bias


def ssd_chunked(x, dt, a, bm, cm):
    b, t, heads, _ = x.shape
    nc, cl = t // SSD_CHUNK, SSD_CHUNK
    g, hg = SSM_GROUPS, heads // SSM_GROUPS
    xs = (x * dt[..., None]).reshape(b, nc, cl, g, hg, SSM_HEAD_DIM)
    a_dt = (dt * a).reshape(b, nc, cl, g, hg).transpose(0, 1, 3, 4, 2)
    a_cs = jnp.cumsum(a_dt, axis=-1)
    bc = bm.reshape(b, nc, cl, g, SSM_STATE)
    cc = cm.reshape(b, nc, cl, g, SSM_STATE)
    causal = jnp.tril(jnp.ones((cl, cl), dtype=bool))
    seg = a_cs[..., :, None] - a_cs[..., None, :]
    lmat = jnp.exp(jnp.where(causal, seg, -jnp.inf))
    cb = jnp.einsum("bclgn,bcsgn->bcgls", cc, bc)
    y_diag = jnp.einsum("bcgls,bcghls,bcsghp->bclghp", cb, lmat, xs)
    decay = jnp.exp(a_cs[..., -1:] - a_cs)
    states = jnp.einsum("bclgn,bcghl,bclghp->bcghpn", bc, decay, xs)
    chunk_decay = jnp.exp(a_cs[..., -1])

    def step(carry, inp):
        st, dec = inp
        return carry * dec[..., None, None] + st, carry

    init = jnp.zeros((b, g, hg, SSM_HEAD_DIM, SSM_STATE), F32)
    _, prev = lax.scan(step, init, (jnp.moveaxis(states, 1, 0), jnp.moveaxis(chunk_decay, 1, 0)))
    prev = jnp.moveaxis(prev, 0, 1)
    y_off = jnp.einsum("bclgn,bcghpn,bcghl->bclghp", cc, prev, jnp.exp(a_cs))
    return (y_diag + y_off).reshape(b, t, heads, SSM_HEAD_DIM)


def mamba2_mixer(h, wt_in, conv_w, conv_b, dt_bias, a_log, d_skip, norm_w, w_out):
    t, d_model = h.shape
    d_inner = 2 * d_model
    heads = d_inner // SSM_HEAD_DIM
    gn = SSM_GROUPS * SSM_STATE
    conv_dim = d_inner + 2 * gn
    zx = lin_t(h, wt_in[:d_inner + conv_dim])
    dt_raw = lin_t(h, wt_in[d_inner + conv_dim:])
    return lin(ssm_core(zx, dt_raw, conv_w, conv_b, dt_bias, a_log, d_skip, norm_w), w_out)


def ssm_core_jnp(zx, dt_raw, conv_w, conv_b, dt_bias, a_log, d_skip, norm_w):
    t = zx.shape[0]
    d_inner = norm_w.shape[0]
    heads = d_inner // SSM_HEAD_DIM
    gn = SSM_GROUPS * SSM_STATE
    z = zx[:, :d_inner]
    xbc = zx[:, d_inner:]
    xbc = jax.nn.silu(causal_depthwise_conv(xbc[None], conv_w, conv_b))[0]
    xs = xbc[:, :d_inner]
    bm = xbc[:, d_inner:d_inner + gn].reshape(1, t, SSM_GROUPS, SSM_STATE)
    cm = xbc[:, d_inner + gn:].reshape(1, t, SSM_GROUPS, SSM_STATE)
    dt = jax.nn.softplus(dt_raw + dt_bias)[None]
    a = -jnp.exp(a_log)
    xh = xs.reshape(1, t, heads, SSM_HEAD_DIM)
    y = ssd_chunked(xh, dt, a, bm, cm)
    y = y + xh * d_skip[:, None]
    y = y.reshape(t, d_inner) * jax.nn.silu(z)
    return rmsnorm(y.reshape(t, SSM_GROUPS, -1), norm_w.reshape(SSM_GROUPS, -1)).reshape(t, d_inner)


def alibi_slopes(n_heads):
    return 2.0 ** (-8.0 * jnp.arange(1, n_heads + 1, dtype=F32) / n_heads)


def dilated_group_attention(q, k, v, window, dilation, slopes):
    b, t, nh, e = q.shape
    span = window // dilation
    blk = span
    lu = t // dilation
    nb = -(-lu // blk)
    lp = nb * blk

    def to_blocks(arr):
        arr = arr.reshape(b, lu, dilation, nh, e)
        arr = jnp.pad(arr, ((0, 0), (0, lp - lu), (0, 0), (0, 0), (0, 0)))
        return arr.reshape(b, nb, blk, dilation, nh, e)

    qb, kb, vb = to_blocks(q), to_blocks(k), to_blocks(v)
    pad_prev = ((0, 0), (1, 0), (0, 0), (0, 0), (0, 0), (0, 0))
    kcat = jnp.concatenate([jnp.pad(kb, pad_prev)[:, :nb], kb], axis=2)
    vcat = jnp.concatenate([jnp.pad(vb, pad_prev)[:, :nb], vb], axis=2)
    scores = jnp.einsum("bnqrhe,bnkrhe->bnrhqk", qb, kcat) * (1.0 / math.sqrt(e))
    qi = jnp.arange(blk)[:, None]
    ki = jnp.arange(2 * blk)[None, :]
    dist = qi + blk - ki
    in_band = (dist >= 0) & (dist <= span)
    key_u = jnp.arange(nb)[:, None] * blk - blk + jnp.arange(2 * blk)[None, :]
    valid = in_band[None] & (key_u >= 0)[:, None, :]
    bias = -slopes[:, None, None] * (dilation * dist).astype(F32)[None]
    logits = jnp.where(valid[None, :, None, None], scores + bias[None, None, None], -jnp.inf)
    lse = jax.nn.logsumexp(logits, axis=-1)
    probs = jnp.exp(logits - lse[..., None])
    out = jnp.einsum("bnrhqk,bnkrhe->bnqrhe", probs, vcat)
    out = out.reshape(b, lp, dilation, nh, e)[:, :lu].reshape(b, t, nh, e)
    lse = lse.transpose(0, 1, 4, 2, 3).reshape(b, lp, dilation, nh)[:, :lu].reshape(b, t, nh)
    return out, lse


def dilated_attention_mixer(h, wt_qkv, q_gain, k_gain, w_o):
    t, d_model = h.shape
    heads = d_model // ATT_HEAD_DIM
    ng = len(DIL_PATTERNS)
    return lin(attention_core(lin_t(h, wt_qkv, BF16), q_gain, k_gain), w_o)


def attention_core_jnp(qkv, q_gain, k_gain):
    t = qkv.shape[0]
    ng = len(DIL_PATTERNS)
    heads = qkv.shape[1] // (3 * ng * ATT_HEAD_DIM)
    qkv = qkv.astype(F32).reshape(1, t, ng, 3, heads, ATT_HEAD_DIM)
    q = rmsnorm(qkv[:, :, :, 0], q_gain)
    k = rmsnorm(qkv[:, :, :, 1], k_gain)
    v = qkv[:, :, :, 2]
    slopes = alibi_slopes(heads)
    outs, lses = [], []
    for g, (window, dilation) in enumerate(DIL_PATTERNS):
        o_g, l_g = dilated_group_attention(q[:, :, g], k[:, :, g], v[:, :, g], window, dilation, slopes)
        outs.append(o_g)
        lses.append(l_g)
    alpha = jax.nn.softmax(jnp.stack(lses), axis=0)
    o = jnp.einsum("gbth,gbthe->bthe", alpha, jnp.stack(outs))
    return o.reshape(t, heads * ATT_HEAD_DIM)


def local_step(small, fetch, emit, x, p, target):
    depth = small['norm_mix'].shape[0]
    ssm_small = ('ssm_conv_w', 'ssm_conv_b', 'ssm_dt_bias', 'ssm_a_log', 'ssm_d_skip', 'ssm_norm_w')
    saved = []
    for i in range(depth):
        j = i // 2
        s = {'x': x}
        wm, token = fetch(2 * i, x)
        s['wm'] = wm
        h = s['h'] = _rms_fwd(x, small['norm_mix'][i], token)
        if i % 2 == 0:
            n_main = wm['ssm_w_in'].shape[0] - small['ssm_dt_bias'].shape[1]
            zx = _mm(h, wm['ssm_w_in'][:n_main], tb=True, name="ssm_in_fwd")
            dt_raw = _mm(h, wm['ssm_w_in'][n_main:], tb=True, name="ssm_dt_fwd")
            y, s['mix'] = _ssm_core_fwd(zx, dt_raw, wm['ssm_conv_w'], *[small[n][j] for n in ssm_small[1:]])
            x = _mm(y, wm['ssm_w_out'], add=x, name="ssm_out_fwd")
        else:
            qkv = _mm(h, wm['att_w_qkv'], tb=True, out_dtype=BF16, name="att_qkv_fwd")
            y, s['mix'] = _attention_core_fwd(qkv, small['att_q_norm'][j], small['att_k_norm'][j])
            x = _mm(y, wm['att_w_o'], add=x, name="att_o_fwd")
        s['y'], s['x1'] = y, x
        wf, token = fetch(2 * i + 1, x)
        s['wf'] = wf
        h2 = s['h2'] = _rms_fwd(x, small['norm_ffn'][i], token)
        gu = s['gu'] = _mm(h2, wf['ffn_w_gu'], tb=True, out_dtype=BF16, name="ffn_gu_fwd")
        act = s['act'] = _swiglu_fwd(gu)
        x = s['x2'] = _mm(act, wf['ffn_w_down'], add=x, name="ffn_down_fwd")
        gl = s['gl'] = _mm(x, wf['ple_w_gate'], name="ple_gate_fwd")
        ple = s['ple'] = _mm(p[i], wf['ple_w_proj'], tb=True, name="ple_proj_fwd")
        x = _ple_fwd(x, gl, ple)
        saved.append(s)
    loss, dx = _loss_fwd(x, target)

    g = {n: [None] * small[n].shape[0] for n in small}
    for i in reversed(range(depth)):
        j = i // 2
        s = saved[i]
        wm, wf = s['wm'], s['wf']
        gf = {}
        dgl, dple = _ple_bwd(s['gl'], s['ple'], dx)
        gf['ple_w_proj'] = _mm(dple, p[i], ta=True, out_dtype=BF16, name="ple_proj_dw")
        gf['ple_w_gate'] = _mm(s['x2'], dgl, ta=True, out_dtype=BF16, name="ple_gate_dw")
        dx = _mm(dgl, wf['ple_w_gate'], tb=True, add=dx, name="ple_gate_da")
        dact = _mm(dx, wf['ffn_w_down'], tb=True, out_dtype=BF16, name="ffn_down_da")
        gf['ffn_w_down'] = _mm(s['act'], dx, ta=True, out_dtype=BF16, name="ffn_down_dw")
        dgu = _swiglu_bwd(s['gu'], dact)
        dh2 = _mm(dgu, wf['ffn_w_gu'], out_dtype=BF16, name="ffn_gu_da")
        gf['ffn_w_gu'] = _mm(dgu, s['h2'], ta=True, out_dtype=BF16, name="ffn_gu_dw")
        dx, g['norm_ffn'][i] = _rms_bwd(s['x1'], small['norm_ffn'][i], dh2, dx, emit(2 * i + 1, gf))
        gm = {}
        if i % 2 == 0:
            n_main = wm['ssm_w_in'].shape[0] - small['ssm_dt_bias'].shape[1]
            dyn = _mm(dx, wm['ssm_w_out'], tb=True, out_dtype=BF16, name="ssm_out_da")
            gm['ssm_w_out'] = _mm(s['y'], dx, ta=True, out_dtype=BF16, name="ssm_out_dw")
            dzx, d_dt, *sg = _ssm_core_bwd(s['mix'], dyn)
            for n, v in zip(ssm_small, sg):
                g[n][j] = v
            dh = _mm(d_dt, wm['ssm_w_in'][n_main:], name="ssm_dt_da")
            dh = _mm(dzx, wm['ssm_w_in'][:n_main], add=dh, out_dtype=BF16, name="ssm_in_da")
            gm['ssm_w_in'] = jnp.concatenate([_mm(dzx, s['h'], ta=True, out_dtype=BF16, name="ssm_in_dw"),
                                              _mm(d_dt, s['h'], ta=True, out_dtype=BF16, name="ssm_dt_dw")], axis=0)
        else:
            do = _mm(dx, wm['att_w_o'], tb=True, name="att_o_da")
            gm['att_w_o'] = _mm(s['y'], dx, ta=True, out_dtype=BF16, name="att_o_dw")
            dqkv, g['att_q_norm'][j], g['att_k_norm'][j] = _attention_core_bwd(s['mix'], do)
            dh = _mm(dqkv, wm['att_w_qkv'], out_dtype=BF16, name="att_qkv_da")
            gm['att_w_qkv'] = _mm(dqkv, s['h'], ta=True, out_dtype=BF16, name="att_qkv_dw")
        dx, g['norm_mix'][i] = _rms_bwd(s['x'], small['norm_mix'][i], dh, dx, emit(2 * i, gm))
    return loss, dx, {n: jnp.stack(v) for n, v in g.items()}


MIXER_WEIGHTS = (('ssm_w_in', 'ssm_w_out', 'ssm_conv_w'), ('att_w_qkv', 'att_w_o'))
CHANNEL_WEIGHTS = ('ffn_w_gate', 'ffn_w_up', 'ffn_w_down', 'ple_w_proj', 'ple_w_gate')


def _pack_plan(shapes, width, stage):
    layer = stage // 2
    if stage % 2:
        members = [(n, layer) for n in CHANNEL_WEIGHTS]
    else:
        members = [(n, layer // 2) for n in MIXER_WEIGHTS[layer % 2]]
    plan, off = [], 0
    for name, lyr in members:
        _, r, c = shapes[name]
        if name in COL_SHARDED:
            r, c = c, r
        if name == 'ssm_conv_w':
            pr = -(-2 * r * c // width)
        else:
            assert (r * c) % width == 0, (name, r, c)
            pr = r * c // width
        plan.append((name, lyr, r, c, pr, off))
        off += _round_up(pr, BF16_ROWS)
    return plan, off


def _small_plan(shapes):
    plan, off = [], 0
    for name in SMALL:
        n = math.prod(shapes[name])
        plan.append((name, n, off))
        off += n
    return plan, _round_up(off, SUBLANES * LANES)


def kernel(x, p, norm_mix, norm_ffn, ssm_w_in, ssm_conv_w, ssm_conv_b, ssm_dt_bias, ssm_a_log, ssm_d_skip, ssm_norm_w, ssm_w_out, att_w_qkv, att_q_norm, att_k_norm, att_w_o, ffn_w_gate, ffn_w_up, ffn_w_down, ple_w_proj, ple_w_gate, loss_target, m_norm_mix, m_norm_ffn, m_ssm_w_in, m_ssm_conv_w, m_ssm_conv_b, m_ssm_dt_bias, m_ssm_a_log, m_ssm_d_skip, m_ssm_norm_w, m_ssm_w_out, m_att_w_qkv, m_att_q_norm, m_att_k_norm, m_att_w_o, m_ffn_w_gate, m_ffn_w_up, m_ffn_w_down, m_ple_w_proj, m_ple_w_gate, v_norm_mix, v_norm_ffn, v_ssm_w_in, v_ssm_conv_w, v_ssm_conv_b, v_ssm_dt_bias, v_ssm_a_log, v_ssm_d_skip, v_ssm_norm_w, v_ssm_w_out, v_att_w_qkv, v_att_q_norm, v_att_k_norm, v_att_w_o, v_ffn_w_gate, v_ffn_w_up, v_ffn_w_down, v_ple_w_proj, v_ple_w_gate):
    given = dict(locals())
    w_in = {n: given[n] for n in WEIGHTS}
    m_in = {n: given["m_" + n] for n in WEIGHTS}
    v_in = {n: given["v_" + n] for n in WEIGHTS}
    width = x.shape[-1]
    depth = norm_mix.shape[0]
    n_stages = 2 * depth

    plans = [_pack_plan({n: w_in[n].shape for n in BIG + ('ssm_conv_w',)}, width, stage) for stage in range(n_stages)]

    def start_gather(stage, after):
        pieces = []
        for name, layer, r, c, pr, off in plans[stage][0]:
            blk = w_in[name][layer]
            if name == 'ssm_conv_w':
                blk = lax.bitcast_convert_type(blk.reshape(-1), BF16).reshape(-1)
                blk = jnp.pad(blk, (0, pr * width - blk.shape[0]))
            elif name in COL_SHARDED:
                blk = blk.T
            blk = blk.astype(BF16).reshape(pr, width)
            pieces.append(jnp.pad(blk, ((0, _round_up(pr, BF16_ROWS) - pr), (0, 0))))
        return exchange_start(jnp.concatenate(pieces, axis=0), True, f"gather_start_{stage}", after)

    pending = [start_gather(0, None)]

    def fetch(stage, after):
        handle, token = pending[stage]
        land = exchange_wait(handle, token if stage == 0 else after, True, f"gather_wait_{stage}")
        token = None
        if stage + 1 < n_stages:
            pending.append(start_gather(stage + 1, land))
            token = pending[-1][1]
        got = {}
        for name, layer, r, c, pr, off in plans[stage][0]:
            piece = land[:, off:off + pr]
            if name == 'ssm_conv_w':
                taps, chans = w_in[name].shape[1:]
                bits = piece.reshape(N_DEV, -1)[:, :2 * taps * chans].reshape(N_DEV, taps * chans, 2)
                piece = lax.bitcast_convert_type(bits, F32).reshape(N_DEV, taps, chans)
                got[name] = piece.transpose(1, 0, 2).reshape(taps, N_DEV * chans)
            else:
                got[name] = piece.reshape(N_DEV * r, c)
        if 'ffn_w_gate' in got:
            got['ffn_w_gu'] = jnp.concatenate([got.pop('ffn_w_gate'), got.pop('ffn_w_up')], axis=0)
        return got, token

    scatters = [None] * n_stages

    def emit(stage, grads):
        grads = dict(grads)
        if 'ffn_w_gu' in grads:
            hidden = grads['ffn_w_gu'].shape[0] // 2
            grads['ffn_w_gate'], grads['ffn_w_up'] = grads['ffn_w_gu'][:hidden], grads['ffn_w_gu'][hidden:]
        pieces = []
        for name, layer, r, c, pr, off in plans[stage][0]:
            if name == 'ssm_conv_w':
                continue
            g = grads[name].reshape(N_DEV, pr, width)
            pieces.append(jnp.pad(g, ((0, 0), (0, _round_up(pr, BF16_ROWS) - pr), (0, 0))))
        scatters[stage], token = exchange_start(jnp.concatenate(pieces, axis=1), False, f"scatter_start_{stage}")
        return token

    small = {n: w_in[n] for n in SMALL}
    cs = w_in['ssm_conv_w'].shape[2]
    loss_local, gx, gw = local_step(small, fetch, emit, x[0], p[:, 0], loss_target[0])
    loss = lax.psum(loss_local, ("x", "y", "c"))

    parts = {}
    for stage in reversed(range(n_stages)):
        received = exchange_wait(scatters[stage], gx, False, f"scatter_wait_{stage}")
        gsum = sum_slots(received, f"sum_grads_{stage}")
        for name, layer, r, c, pr, off in plans[stage][0]:
            if name == 'ssm_conv_w':
                continue
            g = gsum[off:off + pr].reshape(r, c)
            parts[name, layer] = g.T if name in COL_SHARDED else g
    grads = {n: jnp.stack([parts[n, layer] for layer in range(w_in[n].shape[0])]) for n in BIG}

    splan, stotal = _small_plan({n: gw[n].shape for n in SMALL})
    svec = jnp.concatenate([gw[n].reshape(-1) for n, _, _ in splan])
    svec = jnp.pad(svec, (0, stotal - svec.shape[0])).reshape(stotal // LANES, LANES)
    _, ssum = all_gather_sum_small(svec, "sum_small_grads")
    ssum = ssum.reshape(-1)
    for name, n, off in splan:
        grads[name] = ssum[off:off + n].reshape(gw[name].shape)
    me = _me()
    grads['ssm_conv_w'] = lax.dynamic_slice_in_dim(grads['ssm_conv_w'], me * cs, cs, axis=2)

    delta, new_m, new_v = {}, {}, {}
    for name in BIG:
        shp = w_in[name].shape
        flat = lambda a: a.reshape(-1, shp[-1])
        d, nm, nv = adamw(flat(w_in[name]), flat(grads[name]), flat(m_in[name]), flat(v_in[name]), "adamw_" + name)
        delta[name], new_m[name], new_v[name] = d.reshape(shp), nm.reshape(shp), nv.reshape(shp)
    splan2, stotal2 = _small_plan({n: w_in[n].shape for n in SMALL})

    def pack_small(src):
        vec = jnp.concatenate([src[n].reshape(-1) for n, _, _ in splan2])
        return jnp.pad(vec, (0, stotal2 - vec.shape[0]), constant_values=1.0).reshape(stotal2 // LANES, LANES)

    sd, snm, snv = adamw(pack_small(w_in), pack_small(grads), pack_small(m_in), pack_small(v_in), "adamw_small")
    for name, n, off in splan2:
        shp = w_in[name].shape
        delta[name] = sd.reshape(-1)[off:off + n].reshape(shp)
        new_m[name] = snm.reshape(-1)[off:off + n].reshape(shp)
        new_v[name] = snv.reshape(-1)[off:off + n].reshape(shp)

    return (loss, gx[None], *[grads[n] for n in WEIGHTS], *[delta[n] for n in WEIGHTS],
            *[new_m[n] for n in WEIGHTS], *[new_v[n] for n in WEIGHTS])
```

```python
import functools
import math

import jax
import jax.numpy as jnp
from jax import lax
from jax.experimental import pallas as pl
from jax.experimental.pallas import tpu as pltpu

F32 = jnp.float32
BF16 = jnp.bfloat16
N_DEV = 8
MESH = pl.DeviceIdType.MESH

SSM_HEAD_DIM = 64
SSM_GROUPS = 4
SSM_STATE = 128
CONV_WIDTH = 4
SSD_CHUNK = 128
ATT_HEAD_DIM = 64
DIL_PATTERNS = ((128, 1), (512, 4), (2048, 16))
NORM_EPS = 1e-6
ADAM_LR = 0.001
ADAM_B1 = 0.9
ADAM_B2 = 0.999
ADAM_EPS = 1e-08
ADAM_WD = 0.01
ADAM_STEP = 10

BF16_ROWS = 16
LANES = 128
SUBLANES = 8

WEIGHTS = ['norm_mix', 'norm_ffn', 'ssm_w_in', 'ssm_conv_w', 'ssm_conv_b', 'ssm_dt_bias', 'ssm_a_log', 'ssm_d_skip',
           'ssm_norm_w', 'ssm_w_out', 'att_w_qkv', 'att_q_norm', 'att_k_norm', 'att_w_o', 'ffn_w_gate', 'ffn_w_up',
           'ffn_w_down', 'ple_w_proj', 'ple_w_gate']
COL_SHARDED = ('ssm_w_in', 'att_w_qkv', 'ffn_w_gate', 'ffn_w_up', 'ple_w_proj')
ROW_SHARDED = ('ssm_w_out', 'att_w_o', 'ffn_w_down', 'ple_w_gate')
BIG = COL_SHARDED + ROW_SHARDED
SMALL = ('norm_mix', 'norm_ffn', 'ssm_conv_w', 'ssm_conv_b', 'ssm_dt_bias', 'ssm_a_log', 'ssm_d_skip', 'ssm_norm_w',
         'att_q_norm', 'att_k_norm')


def _pick(n, cands):
    for c in cands:
        if n % c == 0:
            return c
    return n


def _round_up(n, m):
    return -(-n // m) * m


MM_TILES = (1024, 1408, 512, 256, 128)
MM_VMEM_BYTES = 48 * 1024 * 1024


def _mm(a, b, *, ta=False, tb=False, out_dtype=F32, add=None, name):
    k_dim, m_dim = (a.shape if ta else a.shape[::-1])
    n_dim = b.shape[0] if tb else b.shape[1]
    assert (b.shape[1] if tb else b.shape[0]) == k_dim, (a.shape, b.shape, ta, tb)
    tm = _pick(m_dim, MM_TILES)
    tn = _pick(n_dim, MM_TILES)
    tk = _pick(k_dim, MM_TILES)
    nk = k_dim // tk
    a_spec = pl.BlockSpec((tk, tm), lambda i, j, k: (k, i)) if ta else pl.BlockSpec((tm, tk), lambda i, j, k: (i, k))
    b_spec = pl.BlockSpec((tn, tk), lambda i, j, k: (j, k)) if tb else pl.BlockSpec((tk, tn), lambda i, j, k: (k, j))
    o_spec = pl.BlockSpec((tm, tn), lambda i, j, k: (i, j))
    dims = (((0 if ta else 1,), (1 if tb else 0,)), ((), ()))
    has_add = add is not None

    def body(*refs):
        a_ref, b_ref = refs[:2]
        o_ref = refs[2 + has_add]

        def dot():
            return lax.dot_general(a_ref[...].astype(BF16), b_ref[...].astype(BF16), dims,
                                   preferred_element_type=F32)

        def finish(acc):
            if has_add:
                acc = acc + refs[2][...].astype(F32)
            o_ref[...] = acc.astype(o_ref.dtype)

        if nk == 1:
            finish(dot())
            return
        acc_ref = refs[3 + has_add]
        k = pl.program_id(2)

        @pl.when(k == 0)
        def _():
            acc_ref[...] = dot()

        @pl.when((k > 0) & (k < nk - 1))
        def _():
            acc_ref[...] += dot()

        @pl.when(k == nk - 1)
        def _():
            finish(acc_ref[...] + dot())

    return pl.pallas_call(
        body, name=f"{name}_{m_dim}x{n_dim}x{k_dim}",
        out_shape=jax.ShapeDtypeStruct((m_dim, n_dim), out_dtype),
        grid=(m_dim // tm, n_dim // tn, nk),
        in_specs=[a_spec, b_spec] + ([o_spec] if has_add else []),
        out_specs=o_spec,
        scratch_shapes=[] if nk == 1 else [pltpu.VMEM((tm, tn), F32)],
        compiler_params=pltpu.CompilerParams(dimension_semantics=("parallel", "parallel", "arbitrary"),
                                             vmem_limit_bytes=MM_VMEM_BYTES),
    )(*((a, b) + ((add,) if has_add else ())))


def _me():
    return 4 * lax.axis_index("x") + 2 * lax.axis_index("y") + lax.axis_index("c")


def _peer(j):
    x, y, c = lax.axis_index("x"), lax.axis_index("y"), lax.axis_index("c")
    px = 1 - x if j & 4 else x
    py = 1 - y if j & 2 else y
    pc = 1 - c if j & 1 else c
    return (px, py, pc), 4 * px + 2 * py + pc


def _exchange_body(src_of, dst_ref, send_sems, recv_sems, local_sem):
    me = _me()
    mine = pltpu.make_async_copy(src_of(me), dst_ref.at[me], local_sem)
    mine.start()
    sends = []
    for j in range(1, N_DEV):
        peer, pidx = _peer(j)
        cp = pltpu.make_async_remote_copy(src_ref=src_of(pidx), dst_ref=dst_ref.at[me], send_sem=send_sems.at[j - 1],
                                          recv_sem=recv_sems.at[j - 1], device_id=peer, device_id_type=MESH)
        cp.start()
        sends.append(cp)
    for j in range(1, N_DEV):
        peer, pidx = _peer(j)
        pltpu.make_async_remote_copy(src_ref=src_of(pidx), dst_ref=dst_ref.at[pidx], send_sem=send_sems.at[j - 1],
                                     recv_sem=recv_sems.at[j - 1], device_id=peer, device_id_type=MESH).wait_recv()
    for cp in sends:
        cp.wait_send()
    mine.wait()


_EXCHANGE_SCRATCH = [pltpu.SemaphoreType.DMA((N_DEV - 1,)), pltpu.SemaphoreType.DMA((N_DEV - 1,)),
                     pltpu.SemaphoreType.DMA]


def all_gather_hbm(shard, name):
    def body(x_ref, out_ref, send_sems, recv_sems, local_sem):
        _exchange_body(lambda k: x_ref, out_ref, send_sems, recv_sems, local_sem)

    return pl.pallas_call(
        body, name=name,
        out_shape=jax.ShapeDtypeStruct((N_DEV,) + shard.shape, shard.dtype),
        in_specs=[pl.BlockSpec(memory_space=pl.ANY)],
        out_specs=pl.BlockSpec(memory_space=pl.ANY),
        scratch_shapes=list(_EXCHANGE_SCRATCH),
    )(shard)


def all_to_all_hbm(slots, name):
    def body(x_ref, out_ref, send_sems, recv_sems, local_sem):
        _exchange_body(lambda k: x_ref.at[k], out_ref, send_sems, recv_sems, local_sem)

    return pl.pallas_call(
        body, name=name,
        out_shape=jax.ShapeDtypeStruct(slots.shape, slots.dtype),
        in_specs=[pl.BlockSpec(memory_space=pl.ANY)],
        out_specs=pl.BlockSpec(memory_space=pl.ANY),
        scratch_shapes=list(_EXCHANGE_SCRATCH),
    )(slots)


_HBM = pl.BlockSpec(memory_space=pltpu.HBM)
_SEM = pl.BlockSpec(memory_space=pltpu.SEMAPHORE)


def _split_copies(src_ref, gather, land_ref, send_sems, recv_sems):
    me = _me()
    pairs = []
    for j in range(1, N_DEV):
        peer, pidx = _peer(j)

        def make(slot, peer=peer, pidx=pidx, j=j):
            return pltpu.make_async_remote_copy(
                src_ref=src_ref if gather else src_ref.at[pidx], dst_ref=land_ref.at[slot],
                send_sem=send_sems.at[j - 1], recv_sem=recv_sems.at[j - 1], device_id=peer, device_id_type=MESH)

        pairs.append((make(me), make(pidx)))
    return pairs


def exchange_start(src, gather, name, after=None):
    land_shape = ((N_DEV,) + src.shape) if gather else src.shape
    has_after = after is not None

    def body(*refs):
        src_ref, land_ref = refs[:2]
        send_sems, recv_sems = refs[2 + has_after:4 + has_after]
        for send, _ in _split_copies(src_ref, gather, land_ref, send_sems, recv_sems):
            send.start()
        refs[-1][...] = jnp.zeros_like(refs[-1])

    sem = pltpu.SemaphoreType.DMA((N_DEV - 1,))
    send_sems, recv_sems, src_thru, land, token = pl.pallas_call(
        body, name=name,
        out_shape=(sem, sem, pltpu.HBM(src.shape, src.dtype), pltpu.HBM(land_shape, src.dtype),
                   jax.ShapeDtypeStruct((SUBLANES, LANES), F32)),
        in_specs=(_HBM, _HBM) + ((pl.BlockSpec(memory_space=pl.ANY),) if has_after else ()),
        out_specs=(_SEM, _SEM, _HBM, _HBM, pl.BlockSpec(memory_space=pltpu.VMEM)),
        input_output_aliases={0: 2, 1: 3},
        compiler_params=pltpu.CompilerParams(has_side_effects=pltpu.SideEffectType.DATAFLOW_SIDE_EFFECTING),
    )(pltpu.with_memory_space_constraint(src, pltpu.HBM),
      pltpu.with_memory_space_constraint(lax.empty(land_shape, src.dtype), pltpu.HBM),
      *((after,) if has_after else ()))
    return (send_sems, recv_sems, src_thru, land), token


def exchange_wait(handle, after, gather, name):
    send_sems, recv_sems, src_thru, land = handle
    after = tuple(after) if isinstance(after, (tuple, list)) else (after,)

    def body(src_ref, land_ref, send_sems, recv_sems, *rest):
        for _, arrival in _split_copies(src_ref, gather, land_ref, send_sems, recv_sems):
            arrival.wait_send()
            arrival.wait_recv()

    src_done, got = pl.pallas_call(
        body, name=name,
        out_shape=(pltpu.HBM(src_thru.shape, src_thru.dtype), pltpu.HBM(land.shape, land.dtype)),
        in_specs=(_HBM, _HBM, _SEM, _SEM) + (pl.BlockSpec(memory_space=pl.ANY),) * len(after), out_specs=(_HBM, _HBM),
        input_output_aliases={0: 0, 1: 1},
        compiler_params=pltpu.CompilerParams(has_side_effects=pltpu.SideEffectType.DATAFLOW_SIDE_EFFECTING),
    )(src_thru, land, send_sems, recv_sems, *after)
    mine = src_done if gather else lax.dynamic_index_in_dim(src_done, _me(), 0, keepdims=False)
    return lax.dynamic_update_index_in_dim(got, mine, _me(), 0)


def all_gather_sum_small(v, name):
    def body(x_ref, out_ref, sum_ref, send_sems, recv_sems, local_sem):
        _exchange_body(lambda k: x_ref, out_ref, send_sems, recv_sems, local_sem)
        acc = out_ref[0]
        for k in range(1, N_DEV):
            acc = acc + out_ref[k]
        sum_ref[...] = acc

    return pl.pallas_call(
        body, name=name,
        out_shape=(jax.ShapeDtypeStruct((N_DEV,) + v.shape, v.dtype), jax.ShapeDtypeStruct(v.shape, v.dtype)),
        in_specs=[pl.BlockSpec(memory_space=pltpu.VMEM)],
        out_specs=(pl.BlockSpec(memory_space=pltpu.VMEM), pl.BlockSpec(memory_space=pltpu.VMEM)),
        scratch_shapes=list(_EXCHANGE_SCRATCH),
    )(v)


def sum_slots(slots, name):
    _, p_dim, c_dim = slots.shape
    tp = _pick(p_dim, (256, 128, 64, 32, 16))

    def body(x_ref, o_ref):
        acc = x_ref[0].astype(F32)
        for k in range(1, N_DEV):
            acc = acc + x_ref[k].astype(F32)
        o_ref[...] = acc

    return pl.pallas_call(
        body, name=name,
        out_shape=jax.ShapeDtypeStruct((p_dim, c_dim), F32),
        grid=(p_dim // tp,),
        in_specs=[pl.BlockSpec((N_DEV, tp, c_dim), lambda i: (0, i, 0))],
        out_specs=pl.BlockSpec((tp, c_dim), lambda i: (i, 0)),
        compiler_params=pltpu.CompilerParams(dimension_semantics=("parallel",)),
    )(slots)


def adamw(w, g, m, v, name):
    rows, cols = w.shape
    tr = _pick(rows, (256, 128, 64, 32, 16, 8))

    def body(w_ref, g_ref, m_ref, v_ref, d_ref, nm_ref, nv_ref):
        gv = g_ref[...]
        nm = ADAM_B1 * m_ref[...] + (1.0 - ADAM_B1) * gv
        nv = ADAM_B2 * v_ref[...] + (1.0 - ADAM_B2) * (gv * gv)
        m_hat = nm / (1.0 - ADAM_B1 ** ADAM_STEP)
        v_hat = nv / (1.0 - ADAM_B2 ** ADAM_STEP)
        d_ref[...] = -ADAM_LR * (m_hat / (jnp.sqrt(v_hat) + ADAM_EPS) + ADAM_WD * w_ref[...])
        nm_ref[...] = nm
        nv_ref[...] = nv

    spec = pl.BlockSpec((tr, cols), lambda i: (i, 0))
    shp = jax.ShapeDtypeStruct((rows, cols), F32)
    return pl.pallas_call(
        body, name=name, out_shape=(shp, shp, shp), grid=(rows // tr,),
        in_specs=[spec] * 4, out_specs=(spec,) * 3,
        compiler_params=pltpu.CompilerParams(dimension_semantics=("parallel",)),
    )(w, g, m, v)


ATT_BLK = 128
NEG = -1e30


def _head_sums(v):
    first = lax.broadcasted_iota(jnp.int32, (1, LANES), 1) < ATT_HEAD_DIM
    s0 = jnp.sum(jnp.where(first, v, 0.0), axis=-1, keepdims=True)
    s1 = jnp.sum(jnp.where(first, 0.0, v), axis=-1, keepdims=True)
    return jnp.where(first, s0, s1)


def _head_col(v, hmask):
    return jnp.max(jnp.where(hmask, v, -jnp.inf), axis=-1, keepdims=True)


def _qk_norm(raw, gain2):
    rstd = lax.rsqrt(_head_sums(raw * raw) * (1.0 / ATT_HEAD_DIM) + NORM_EPS)
    xhat = raw * rstd
    return xhat * gain2, xhat, rstd


def _qk_norm_bwd(dn, xhat, rstd, gain2):
    dxh = dn * gain2
    return rstd * (dxh - xhat * (_head_sums(dxh * xhat) * (1.0 / ATT_HEAD_DIM))), dn * xhat


def _att_mask_bias(n, dilation):
    qi = lax.broadcasted_iota(jnp.int32, (ATT_BLK, 2 * ATT_BLK), 0)
    ki = lax.broadcasted_iota(jnp.int32, (ATT_BLK, 2 * ATT_BLK), 1)
    dist = qi + ATT_BLK - ki
    valid = (dist >= 0) & (dist <= ATT_BLK) & ((n > 0) | (ki >= ATT_BLK))
    return valid, (dilation * dist).astype(F32)


ATT_PAIRS = 4
RELAYOUT_ROWS = 512
RELAYOUT_COLS = 512


def _to_residues(x, dilation, col0=0, cols=None):
    t = x.shape[0]
    cols = x.shape[1] if cols is None else cols
    if dilation == 1 and col0 == 0 and cols == x.shape[1]:
        return x.reshape(1, t, cols)
    tr = _pick(t, (RELAYOUT_ROWS,))
    tc = _pick(cols, (RELAYOUT_COLS, 256, 128))
    per = tr // dilation
    assert tr % dilation == 0 and col0 % tc == 0

    def body(x_ref, o_ref, s_ref):
        for c in range(tc // LANES):
            lanes = slice(c * LANES, (c + 1) * LANES)
            s_ref[c] = x_ref[:, lanes].astype(F32)
            for r in range(dilation):
                o_ref[r, :, lanes] = s_ref[c, pl.ds(r, per, stride=dilation), :].astype(o_ref.dtype)

    return pl.pallas_call(
        body, name=f"to_residues_{dilation}", out_shape=jax.ShapeDtypeStruct((dilation, t // dilation, cols), x.dtype),
        grid=(t // tr, cols // tc),
        in_specs=[pl.BlockSpec((tr, tc), lambda i, j: (i, col0 // tc + j))],
        out_specs=pl.BlockSpec((dilation, per, tc), lambda i, j: (0, i, j)),
        scratch_shapes=[pltpu.VMEM((tc // LANES, tr, LANES), F32)],
        compiler_params=pltpu.CompilerParams(dimension_semantics=("parallel", "parallel")),
    )(x)


def _from_residues(y):
    dilation, lu, cols = y.shape
    t = dilation * lu
    if dilation == 1:
        return y.reshape(t, cols)
    tr = _pick(t, (RELAYOUT_ROWS,))
    tc = _pick(cols, (RELAYOUT_COLS, 256, 128))
    per = tr // dilation

    def body(y_ref, o_ref, s_ref):
        for c in range(tc // LANES):
            lanes = slice(c * LANES, (c + 1) * LANES)
            for r in range(dilation):
                s_ref[c, pl.ds(r, per, stride=dilation), :] = y_ref[r, :, lanes].astype(F32)
            o_ref[:, lanes] = s_ref[c].astype(o_ref.dtype)

    return pl.pallas_call(
        body, name=f"from_residues_{dilation}", out_shape=jax.ShapeDtypeStruct((t, cols), y.dtype),
        grid=(t // tr, cols // tc),
        in_specs=[pl.BlockSpec((dilation, per, tc), lambda i, j: (0, i, j))],
        out_specs=pl.BlockSpec((tr, tc), lambda i, j: (i, j)),
        scratch_shapes=[pltpu.VMEM((tc // LANES, tr, LANES), F32)],
        compiler_params=pltpu.CompilerParams(dimension_semantics=("parallel", "parallel")),
    )(y)


def _att_specs(base, hd, nb, pp):
    hpn = hd // LANES
    assert base % (pp * LANES) == 0 and hpn % pp == 0

    def spec(which, shift):
        def imap(r, hp, n):
            row = jnp.minimum(n, nb - 1) if shift == 0 else jnp.maximum(n - 1, 0)
            return (r, row, (base // LANES + which * hpn) // pp + hp)
        return pl.BlockSpec((None, ATT_BLK, pp * LANES), imap)

    return [spec(0, 0), spec(1, 1), spec(1, 0), spec(2, 1), spec(2, 0)]


DELTA_LANE = 64


def _att_pre(qkv, g, dilation, gq2, gk2):
    t, width = qkv.shape
    hd = width // (3 * len(DIL_PATTERNS))
    tr = _pick(t, (RELAYOUT_ROWS,))
    tc = _pick(hd, (RELAYOUT_COLS, 256, 128))
    per = tr // dilation
    assert tr % dilation == 0 and (g * 3 * hd) % tc == 0

    def body(x_ref, gq_ref, gk_ref, o_ref, s_ref):
        section = pl.program_id(1) * tc // hd
        gain = jnp.where(section == 0, gq_ref[0:1, :], gk_ref[0:1, :])
        for c in range(tc // LANES):
            lanes = slice(c * LANES, (c + 1) * LANES)
            raw = x_ref[:, lanes].astype(F32)
            s_ref[c] = jnp.where(section == 2, raw, _qk_norm(raw, gain)[0])
            for r in range(dilation):
                o_ref[r, :, lanes] = s_ref[c, pl.ds(r, per, stride=dilation), :].astype(o_ref.dtype)

    vec_spec = pl.BlockSpec((SUBLANES, LANES), lambda i, j: (0, 0))
    return pl.pallas_call(
        body, name=f"att_pre_g{g}", out_shape=jax.ShapeDtypeStruct((dilation, t // dilation, 3 * hd), qkv.dtype),
        grid=(t // tr, 3 * hd // tc),
        in_specs=[pl.BlockSpec((tr, tc), lambda i, j: (i, g * 3 * hd // tc + j)), vec_spec, vec_spec],
        out_specs=pl.BlockSpec((dilation, per, tc), lambda i, j: (0, i, j)),
        scratch_shapes=[pltpu.VMEM((tc // LANES, tr, LANES), F32)],
        compiler_params=pltpu.CompilerParams(dimension_semantics=("parallel", "parallel")),
    )(qkv, gq2, gk2)


def _att_group_fwd(qkv_r, hd, slopes, g):
    dilation, lu, _ = qkv_r.shape
    nb = lu // ATT_BLK
    assert nb * ATT_BLK == lu and hd % LANES == 0
    hpn = hd // LANES
    pp = math.gcd(ATT_PAIRS, hpn)
    scale = 1.0 / math.sqrt(ATT_HEAD_DIM)

    def body(q_ref, kp_ref, kc_ref, vp_ref, vc_ref, sl_ref, o_ref, l_ref):
        n = pl.program_id(2)
        lane = lax.broadcasted_iota(jnp.int32, (1, LANES), 1)
        first = (lane // ATT_HEAD_DIM) == 0
        valid, dist = _att_mask_bias(n, dilation)
        stats = jnp.zeros((ATT_BLK, LANES), F32)
        for pair in range(pp):
            cols = slice(pair * LANES, (pair + 1) * LANES)
            qn = q_ref[:, cols]
            kn16 = jnp.concatenate([kp_ref[:, cols], kc_ref[:, cols]], axis=0)
            v16 = jnp.concatenate([vp_ref[:, cols], vc_ref[:, cols]], axis=0)
            outs = []
            for hh in range(2):
                hmask = (lane // ATT_HEAD_DIM) == hh
                qh = jnp.where(hmask, qn, jnp.zeros_like(qn))
                s = lax.dot_general(qh, kn16, (((1,), (1,)), ((), ())), preferred_element_type=F32) * scale
                slope = _head_col(sl_ref[pair, 0:1, :], hmask)
                logits = jnp.where(valid, s - slope * dist, NEG)
                mx = jnp.max(logits, axis=-1, keepdims=True)
                pexp = jnp.exp(logits - mx)
                den = jnp.sum(pexp, axis=-1, keepdims=True)
                outs.append(jnp.dot(pexp.astype(BF16), v16, preferred_element_type=F32) / den)
                stats = jnp.where(lane == 2 * pair + hh, mx + jnp.log(den), stats)
            o_ref[:, cols] = jnp.where(first, outs[0], outs[1]).astype(BF16)
        l_ref[...] = stats

    out_spec = pl.BlockSpec((None, ATT_BLK, pp * LANES), lambda r, hp, n: (r, n, hp))
    stat_spec = pl.BlockSpec((None, ATT_BLK, LANES), lambda r, hp, n: (r, n, hp))
    o, lse = pl.pallas_call(
        body, name=f"att_fwd_g{g}",
        out_shape=(jax.ShapeDtypeStruct((dilation, lu, hd), BF16),
                   jax.ShapeDtypeStruct((dilation, lu, hpn // pp * LANES), F32)),
        grid=(dilation, hpn // pp, nb),
        in_specs=_att_specs(0, hd, nb, pp) + [pl.BlockSpec((pp, SUBLANES, LANES), lambda r, hp, n: (hp, 0, 0))],
        out_specs=(out_spec, stat_spec),
        compiler_params=pltpu.CompilerParams(dimension_semantics=("parallel", "parallel", "arbitrary")),
    )(*([qkv_r] * 5), slopes)
    return _from_residues(o), _from_residues(lse)


def _att_merge(outs, lses):
    t, hd = outs[0].shape
    sw = lses[0].shape[1]
    pp = hd // sw
    tr = _pick(t, (256, 128))
    ng = len(outs)

    def body(*refs):
        o_refs, l_refs, o16_ref, lt_ref = refs[:ng], refs[ng:2 * ng], refs[2 * ng], refs[2 * ng + 1]
        lane = lax.broadcasted_iota(jnp.int32, (1, LANES), 1)
        first = (lane // ATT_HEAD_DIM) == 0
        for blk in range(sw // LANES):
            scols = slice(blk * LANES, (blk + 1) * LANES)
            stats = jnp.zeros((tr, LANES), F32)
            for pair in range(pp):
                cols = slice((blk * pp + pair) * LANES, (blk * pp + pair + 1) * LANES)
                weights = []
                for hh in range(2):
                    pick = lane == 2 * pair + hh
                    ls = [_head_col(r[:, scols], pick) for r in l_refs]
                    mx = functools.reduce(jnp.maximum, ls)
                    es = [jnp.exp(l - mx) for l in ls]
                    den = functools.reduce(jnp.add, es)
                    weights.append([e / den for e in es])
                    stats = jnp.where(pick, mx + jnp.log(den), stats)
                acc = jnp.zeros((tr, LANES), F32)
                for gi in range(ng):
                    acc = acc + jnp.where(first, weights[0][gi], weights[1][gi]) * o_refs[gi][:, cols].astype(F32)
                o16_ref[:, cols] = acc.astype(BF16)
            lt_ref[:, scols] = stats

    spec = pl.BlockSpec((tr, hd), lambda i: (i, 0))
    sspec = pl.BlockSpec((tr, sw), lambda i: (i, 0))
    return pl.pallas_call(
        body, name="att_merge",
        out_shape=(jax.ShapeDtypeStruct((t, hd), BF16), jax.ShapeDtypeStruct((t, sw), F32)), grid=(t // tr,),
        in_specs=[spec] * ng + [sspec] * ng, out_specs=(spec, sspec),
        compiler_params=pltpu.CompilerParams(dimension_semantics=("parallel",)),
    )(*outs, *lses)


def _att_bwd_prep(do, o16, lse_tot):
    t, hd = do.shape
    sw = lse_tot.shape[1]
    pp = hd // sw
    tr = _pick(t, (256, 128))

    def body(do_ref, o_ref, l_ref, d16_ref, st_ref):
        lane = lax.broadcasted_iota(jnp.int32, (1, LANES), 1)
        d16_ref[...] = do_ref[...].astype(BF16)
        for blk in range(sw // LANES):
            scols = slice(blk * LANES, (blk + 1) * LANES)
            stats = l_ref[:, scols]
            for pair in range(pp):
                cols = slice((blk * pp + pair) * LANES, (blk * pp + pair + 1) * LANES)
                prod = do_ref[:, cols] * o_ref[:, cols].astype(F32)
                for hh in range(2):
                    hmask = (lane // ATT_HEAD_DIM) == hh
                    delta = jnp.sum(jnp.where(hmask, prod, 0.0), axis=-1, keepdims=True)
                    stats = jnp.where(lane == DELTA_LANE + 2 * pair + hh, delta, stats)
            st_ref[:, scols] = stats

    spec = pl.BlockSpec((tr, hd), lambda i: (i, 0))
    sspec = pl.BlockSpec((tr, sw), lambda i: (i, 0))
    return pl.pallas_call(
        body, name="att_bwd_prep",
        out_shape=(jax.ShapeDtypeStruct((t, hd), BF16), jax.ShapeDtypeStruct((t, sw), F32)), grid=(t // tr,),
        in_specs=[spec, spec, sspec], out_specs=(spec, sspec),
        compiler_params=pltpu.CompilerParams(dimension_semantics=("parallel",)),
    )(do, o16, lse_tot)


def _att_post(buf, parts, qkv, gq2, gk2, g):
    dilation, lu, hd = parts[0].shape
    t = dilation * lu
    tr = _pick(t, (RELAYOUT_ROWS,))
    per = tr // dilation

    def body(*refs):
        raw_ref, gq_ref, gk_ref = refs[3:6]
        o_ref, dgq_ref, dgk_ref, s_ref = refs[-4:]

        @pl.when(pl.program_id(0) == 0)
        def _():
            dgq_ref[...] = jnp.zeros_like(dgq_ref)
            dgk_ref[...] = jnp.zeros_like(dgk_ref)

        for sec, y_ref in enumerate(refs[:3]):
            gsum = jnp.zeros((1, LANES), F32)
            for c in range(hd // LANES):
                lanes = slice(c * LANES, (c + 1) * LANES)
                out_lanes = slice(sec * hd + c * LANES, sec * hd + (c + 1) * LANES)
                for r in range(dilation):
                    s_ref[pl.ds(r, per, stride=dilation), :] = y_ref[r, :, lanes].astype(F32)
                d = s_ref[...]
                if sec < 2:
                    gain = (gq_ref if sec == 0 else gk_ref)[0:1, :]
                    _, xhat, rstd = _qk_norm(raw_ref[:, out_lanes].astype(F32), gain)
                    d, part = _qk_norm_bwd(d, xhat, rstd, gain)
                    gsum = gsum + jnp.sum(part, axis=0, keepdims=True)
                o_ref[:, out_lanes] = d.astype(o_ref.dtype)
            if sec < 2:
                acc = dgq_ref if sec == 0 else dgk_ref
                acc[...] += jnp.broadcast_to(gsum, acc.shape)

    part_spec = pl.BlockSpec((dilation, per, hd), lambda i: (0, i, 0))
    slab_spec = pl.BlockSpec((tr, 3 * hd), lambda i: (i, g))
    vec_spec = pl.BlockSpec((SUBLANES, LANES), lambda i: (0, 0))
    vec_shape = jax.ShapeDtypeStruct((SUBLANES, LANES), F32)
    return pl.pallas_call(
        body, name=f"att_post_g{g}", out_shape=(jax.ShapeDtypeStruct(qkv.shape, qkv.dtype), vec_shape, vec_shape),
        grid=(t // tr,),
        in_specs=[part_spec] * 3 + [slab_spec, vec_spec, vec_spec] + (
            [] if buf is None else [pl.BlockSpec(memory_space=pl.ANY)]),
        out_specs=(slab_spec, vec_spec, vec_spec),
        scratch_shapes=[pltpu.VMEM((tr, LANES), F32)],
        input_output_aliases={} if buf is None else {6: 0},
        compiler_params=pltpu.CompilerParams(dimension_semantics=("arbitrary",)),
    )(*parts, qkv, gq2, gk2, *(() if buf is None else (buf,)))


def _att_group_bwd(qkv_r, hd, slopes, stats, do16, g):
    dilation, lu, _ = qkv_r.shape
    nb = lu // ATT_BLK
    hpn = hd // LANES
    pp = math.gcd(ATT_PAIRS, hpn)
    hbn = hpn // pp
    scale = 1.0 / math.sqrt(ATT_HEAD_DIM)

    def body(q_ref, kp_ref, kc_ref, vp_ref, vc_ref, sl_ref, st_ref, do_ref, dq_ref, dk_ref, dv_ref, ck_ref, cv_ref):
        n = pl.program_id(2)
        lane = lax.broadcasted_iota(jnp.int32, (1, LANES), 1)

        @pl.when(n == 0)
        def _():
            ck_ref[...] = jnp.zeros_like(ck_ref)
            cv_ref[...] = jnp.zeros_like(cv_ref)

        @pl.when(n < nb)
        def _():
            valid, dist = _att_mask_bias(n, dilation)
            stats = st_ref[...]
            for pair in range(pp):
                cols = slice(pair * LANES, (pair + 1) * LANES)
                qn = q_ref[:, cols]
                kn16 = jnp.concatenate([kp_ref[:, cols], kc_ref[:, cols]], axis=0)
                v16 = jnp.concatenate([vp_ref[:, cols], vc_ref[:, cols]], axis=0)
                dov = do_ref[:, cols]
                dq_acc = jnp.zeros((ATT_BLK, LANES), F32)
                dk_acc = jnp.zeros((2 * ATT_BLK, LANES), F32)
                dv_acc = jnp.zeros((2 * ATT_BLK, LANES), F32)
                for hh in range(2):
                    hmask = (lane // ATT_HEAD_DIM) == hh
                    qh = jnp.where(hmask, qn, jnp.zeros_like(qn))
                    doh = jnp.where(hmask, dov, jnp.zeros_like(dov))
                    s = lax.dot_general(qh, kn16, (((1,), (1,)), ((), ())), preferred_element_type=F32) * scale
                    slope = _head_col(sl_ref[pair, 0:1, :], hmask)
                    lse = _head_col(stats, lane == 2 * pair + hh)
                    delta = _head_col(stats, lane == DELTA_LANE + 2 * pair + hh)
                    pr = jnp.exp(jnp.where(valid, s - slope * dist - lse, NEG))
                    dp = lax.dot_general(doh, v16, (((1,), (1,)), ((), ())), preferred_element_type=F32)
                    ds = (pr * (dp - delta) * scale).astype(BF16)
                    dq_acc = dq_acc + jnp.where(hmask, jnp.dot(ds, kn16, preferred_element_type=F32), 0.0)
                    dk_acc = dk_acc + lax.dot_general(ds, qh, (((0,), (0,)), ((), ())), preferred_element_type=F32)
                    dv_acc = dv_acc + lax.dot_general(pr.astype(BF16), doh, (((0,), (0,)), ((), ())),
                                                      preferred_element_type=F32)
                dq_ref[:, cols] = dq_acc.astype(dq_ref.dtype)
                dk_ref[:, cols] = (ck_ref[:, cols] + dk_acc[:ATT_BLK]).astype(dk_ref.dtype)
                dv_ref[:, cols] = (cv_ref[:, cols] + dv_acc[:ATT_BLK]).astype(dv_ref.dtype)
                ck_ref[:, cols] = dk_acc[ATT_BLK:]
                cv_ref[:, cols] = dv_acc[ATT_BLK:]

        @pl.when(n == nb)
        def _():
            dk_ref[...] = ck_ref[...].astype(dk_ref.dtype)
            dv_ref[...] = cv_ref[...].astype(dv_ref.dtype)

    width = pp * LANES
    q_out = pl.BlockSpec((None, ATT_BLK, width), lambda r, hp, n: (r, jnp.minimum(n, nb - 1), hp))
    kv_out = pl.BlockSpec((None, ATT_BLK, width), lambda r, hp, n: (r, jnp.maximum(n - 1, 0), hp))
    st_spec = pl.BlockSpec((None, ATT_BLK, LANES), lambda r, hp, n: (r, jnp.minimum(n, nb - 1), hp))
    shp = jax.ShapeDtypeStruct((dilation, lu, hd), BF16)
    return pl.pallas_call(
        body, name=f"att_bwd_g{g}", out_shape=(shp, shp, shp), grid=(dilation, hbn, nb + 1),
        in_specs=_att_specs(0, hd, nb, pp) + [
            pl.BlockSpec((pp, SUBLANES, LANES), lambda r, hp, n: (hp, 0, 0)), st_spec, q_out],
        out_specs=(q_out, kv_out, kv_out),
        scratch_shapes=[pltpu.VMEM((ATT_BLK, width), F32), pltpu.VMEM((ATT_BLK, width), F32)],
        compiler_params=pltpu.CompilerParams(dimension_semantics=("parallel", "parallel", "arbitrary")),
    )(*([qkv_r] * 5), slopes, _to_residues(stats, dilation), _to_residues(do16, dilation))


def _att_consts(q_gain, k_gain, hd):
    heads = hd // ATT_HEAD_DIM
    gq2 = jnp.broadcast_to(jnp.tile(q_gain, 2)[None], (SUBLANES, LANES))
    gk2 = jnp.broadcast_to(jnp.tile(k_gain, 2)[None], (SUBLANES, LANES))
    sl = 2.0 ** (-8.0 * jnp.arange(1, heads + 1, dtype=F32) / heads)
    slopes = jnp.broadcast_to(jnp.repeat(sl, ATT_HEAD_DIM).reshape(hd // LANES, 1, LANES), (hd // LANES, SUBLANES, LANES))
    return gq2, gk2, slopes


def _attention_core_fwd(qkv, q_gain, k_gain):
    hd = qkv.shape[1] // (3 * len(DIL_PATTERNS))
    gq2, gk2, slopes = _att_consts(q_gain, k_gain, hd)
    outs, lses, views = [], [], []
    for g, (_, dilation) in enumerate(DIL_PATTERNS):
        qkv_r = _att_pre(qkv, g, dilation, gq2, gk2)
        o_g, l_g = _att_group_fwd(qkv_r, hd, slopes, g)
        outs.append(o_g)
        lses.append(l_g)
        views.append(qkv_r)
    o16, lse_tot = _att_merge(outs, lses)
    return o16, (qkv, views, q_gain, k_gain, o16, lse_tot)


def _attention_core_bwd(res, do):
    qkv, views, q_gain, k_gain, o16, lse_tot = res
    hd = o16.shape[1]
    gq2, gk2, slopes = _att_consts(q_gain, k_gain, hd)
    do16, stats = _att_bwd_prep(do, o16, lse_tot)
    dqkv, dgq, dgk = None, 0.0, 0.0
    for g, qkv_r in enumerate(views):
        parts = _att_group_bwd(qkv_r, hd, slopes, stats, do16, g)
        dqkv, a, b = _att_post(dqkv, parts, qkv, gq2, gk2, g)
        dgq = dgq + a[0].reshape(-1, ATT_HEAD_DIM).sum(0)
        dgk = dgk + b[0].reshape(-1, ATT_HEAD_DIM).sum(0)
    return dqkv, dgq, dgk


HALO = 8


def _silu(v):
    return v * jax.nn.sigmoid(v)


def _silu_grad(v):
    s = jax.nn.sigmoid(v)
    return s * (1.0 + v * (1.0 - s))


def _conv_fwd(zx, conv_w, conv_b, d_inner):
    t = zx.shape[0]
    conv_dim = conv_w.shape[1]
    cb = _pick(d_inner, (1024, 512, 256, 128))
    assert conv_dim % cb == 0
    tr = _pick(t, (256, 128))
    off = d_inner // cb

    def body(x_ref, h_ref, w_ref, b_ref, o_ref):
        i = pl.program_id(1)
        halo = jnp.where(i > 0, h_ref[...], 0.0)
        ext = jnp.concatenate([halo, x_ref[...]], axis=0)
        acc = jnp.broadcast_to(b_ref[...], (tr, cb))
        for k in range(CONV_WIDTH):
            s = CONV_WIDTH - 1 - k
            sh = ext if s == 0 else pltpu.roll(ext, shift=s, axis=0)
            acc = acc + w_ref[k:k + 1, :] * sh[HALO:HALO + tr]
        o_ref[...] = acc

    return pl.pallas_call(
        body, name="ssm_conv_fwd", out_shape=jax.ShapeDtypeStruct((t, conv_dim), F32),
        grid=(conv_dim // cb, t // tr),
        in_specs=[pl.BlockSpec((tr, cb), lambda j, i: (i, off + j)),
                  pl.BlockSpec((HALO, cb), lambda j, i: (jnp.maximum(i * (tr // HALO) - 1, 0), off + j)),
                  pl.BlockSpec((CONV_WIDTH, cb), lambda j, i: (0, j)),
                  pl.BlockSpec((1, cb), lambda j, i: (0, j))],
        out_specs=pl.BlockSpec((tr, cb), lambda j, i: (i, j)),
        compiler_params=pltpu.CompilerParams(dimension_semantics=("parallel", "parallel")),
    )(zx, zx, conv_w, conv_b.reshape(1, -1))


def _conv_bwd(zx, conv_w, dpre, dzx, d_inner):
    t, width = zx.shape
    conv_dim = conv_w.shape[1]
    cb = _pick(d_inner, (1024, 512, 256, 128))
    tr = _pick(t, (256, 128))
    off = d_inner // cb
    nr = t // tr

    def body(x_ref, h_ref, w_ref, d_ref, dn_ref, dzx_in, dx_ref, dw_ref, db_ref):
        i = pl.program_id(1)

        @pl.when(i == 0)
        def _():
            dw_ref[...] = jnp.zeros_like(dw_ref)
            db_ref[...] = jnp.zeros_like(db_ref)

        halo = jnp.where(i > 0, h_ref[...], 0.0)
        ext = jnp.concatenate([halo, x_ref[...]], axis=0)
        d = d_ref[...]
        dext = jnp.concatenate([d, jnp.where(i < nr - 1, dn_ref[...], 0.0)], axis=0)
        dx = jnp.zeros((tr, cb), F32)
        for k in range(CONV_WIDTH):
            s = CONV_WIDTH - 1 - k
            fut = dext if s == 0 else pltpu.roll(dext, shift=tr + HALO - s, axis=0)
            dx = dx + w_ref[k:k + 1, :] * fut[:tr]
            past = ext if s == 0 else pltpu.roll(ext, shift=s, axis=0)
            dw_ref[k:k + 1, :] += jnp.sum(d * past[HALO:HALO + tr], axis=0, keepdims=True)
        dx_ref[...] = dx.astype(dx_ref.dtype)
        db_ref[...] += jnp.sum(d, axis=0, keepdims=True)

    last_halo = t // HALO - 1
    return pl.pallas_call(
        body, name="ssm_conv_bwd",
        out_shape=(jax.ShapeDtypeStruct(dzx.shape, dzx.dtype), jax.ShapeDtypeStruct((CONV_WIDTH, conv_dim), F32),
                   jax.ShapeDtypeStruct((1, conv_dim), F32)),
        grid=(conv_dim // cb, nr),
        in_specs=[pl.BlockSpec((tr, cb), lambda j, i: (i, off + j)),
                  pl.BlockSpec((HALO, cb), lambda j, i: (jnp.maximum(i * (tr // HALO) - 1, 0), off + j)),
                  pl.BlockSpec((CONV_WIDTH, cb), lambda j, i: (0, j)),
                  pl.BlockSpec((tr, cb), lambda j, i: (i, j)),
                  pl.BlockSpec((HALO, cb), lambda j, i: (jnp.minimum((i + 1) * (tr // HALO), last_halo), j)),
                  pl.BlockSpec(memory_space=pl.ANY)],
        out_specs=(pl.BlockSpec((tr, cb), lambda j, i: (i, off + j)),
                   pl.BlockSpec((CONV_WIDTH, cb), lambda j, i: (0, j)),
                   pl.BlockSpec((1, cb), lambda j, i: (0, j))),
        input_output_aliases={5: 0},
        compiler_params=pltpu.CompilerParams(dimension_semantics=("parallel", "arbitrary")),
    )(zx, zx, conv_w, dpre, dpre, dzx)


def _eye(n):
    return lax.broadcasted_iota(jnp.int32, (n, n), 0) == lax.broadcasted_iota(jnp.int32, (n, n), 1)


def _row_to_col(row):
    n = row.shape[1]
    return jnp.sum(jnp.where(_eye(n), row, 0.0), axis=1, keepdims=True)


def _col_to_row(col):
    n = col.shape[0]
    return jnp.sum(jnp.where(_eye(n), col, 0.0), axis=0, keepdims=True)


def _pair_lanes(c0, c1):
    lane = lax.broadcasted_iota(jnp.int32, (1, LANES), 1)
    return jnp.where(lane < SSM_HEAD_DIM, c0, c1)


def _ssd_chunk_common(pre_x_ref, pre_b_ref, pre_c_ref, dtr_ref, bias_ref, alog_ref, cs_ref):
    cl = SSD_CHUNK
    hpg = dtr_ref.shape[0]
    x = _silu(pre_x_ref[...])
    b16 = _silu(pre_b_ref[...]).astype(BF16)
    c16 = _silu(pre_c_ref[...]).astype(BF16)
    dt = jax.nn.softplus(dtr_ref[...] + bias_ref[...])
    a = -jnp.exp(alog_ref[...])
    li = lax.broadcasted_iota(jnp.int32, (cl, cl), 0)
    si = lax.broadcasted_iota(jnp.int32, (cl, cl), 1)
    upper = (li <= si).astype(F32)
    cs_ref[0:hpg, :] = jnp.dot(dt * a, upper, precision=lax.Precision.HIGHEST, preferred_element_type=F32)
    cs_ref[hpg:2 * hpg, :] = dt
    g = lax.dot_general(c16, b16, (((1,), (1,)), ((), ())), preferred_element_type=F32)
    return x, b16, c16, dt, a, g, li >= si


def _ssd_fwd(pre, dtT, bias, alog, dskip_lanes, d_inner):
    t = pre.shape[0]
    cl = SSD_CHUNK
    nc = t // cl
    ng = SSM_GROUPS
    hpg = dtT.shape[1]
    gw = hpg * SSM_HEAD_DIM
    assert d_inner == ng * gw and hpg % 2 == 0
    bo = d_inner // SSM_STATE

    def body(px_ref, pb_ref, pc_ref, dtr_ref, bias_ref, alog_ref, dsk_ref, y_ref, st_ref, s_ref, cs_ref):
        c = pl.program_id(1)

        @pl.when(c == 0)
        def _():
            s_ref[...] = jnp.zeros_like(s_ref)

        x, b16, c16, dt, a, g, causal = _ssd_chunk_common(px_ref, pb_ref, pc_ref, dtr_ref, bias_ref, alog_ref, cs_ref)
        st_ref[...] = s_ref[...]
        yoff = lax.dot_general(c16, s_ref[...].astype(BF16), (((1,), (1,)), ((), ())), preferred_element_type=F32)
        xde_parts = []
        for j in range(hpg // 2):
            cols = slice(j * LANES, (j + 1) * LANES)
            xp = x[:, cols]
            dcol, ecol, ocol, ms = [], [], [], []
            for hh in range(2):
                h = 2 * j + hh
                cs_row = cs_ref[h:h + 1, :]
                cs_col = _row_to_col(cs_row)
                dcol.append(_row_to_col(cs_ref[hpg + h:hpg + h + 1, :]))
                ecol.append(jnp.exp(cs_ref[h:h + 1, cl - 1:cl] - cs_col))
                ocol.append(jnp.exp(cs_col))
                lm = jnp.where(causal, jnp.exp(jnp.minimum(cs_col - cs_row, 0.0)), 0.0)
                ms.append((g * lm).astype(BF16))
            xd = xp * _pair_lanes(dcol[0], dcol[1])
            xd16 = xd.astype(BF16)
            yd = _pair_lanes(1.0, 0.0) * jnp.dot(ms[0], xd16, preferred_element_type=F32) \
                + _pair_lanes(0.0, 1.0) * jnp.dot(ms[1], xd16, preferred_element_type=F32)
            y_ref[:, cols] = yd + yoff[:, cols] * _pair_lanes(ocol[0], ocol[1]) + xp * dsk_ref[0:1, cols]
            xde_parts.append((xd * _pair_lanes(ecol[0], ecol[1])).astype(BF16))
        new = lax.dot_general(jnp.concatenate(xde_parts, axis=1), b16, (((0,), (0,)), ((), ())),
                              preferred_element_type=F32)
        for h in range(hpg):
            rows = slice(h * SSM_HEAD_DIM, (h + 1) * SSM_HEAD_DIM)
            s_ref[rows, :] = s_ref[rows, :] * jnp.exp(cs_ref[h:h + 1, cl - 1:cl]) + new[rows, :]

    vec = lambda n: pl.BlockSpec((None, hpg, n), lambda gi, c: (gi, 0, 0))
    return pl.pallas_call(
        body, name="ssd_fwd",
        out_shape=(jax.ShapeDtypeStruct((t, d_inner), F32), jax.ShapeDtypeStruct((ng, nc, gw, SSM_STATE), F32)),
        grid=(ng, nc),
        in_specs=[pl.BlockSpec((cl, gw), lambda gi, c: (c, gi)),
                  pl.BlockSpec((cl, SSM_STATE), lambda gi, c: (c, bo + gi)),
                  pl.BlockSpec((cl, SSM_STATE), lambda gi, c: (c, bo + ng + gi)),
                  pl.BlockSpec((None, hpg, cl), lambda gi, c: (gi, 0, c)),
                  vec(1), vec(1),
                  pl.BlockSpec((1, gw), lambda gi, c: (0, gi))],
        out_specs=(pl.BlockSpec((cl, gw), lambda gi, c: (c, gi)),
                   pl.BlockSpec((None, None, gw, SSM_STATE), lambda gi, c: (gi, c, 0, 0))),
        scratch_shapes=[pltpu.VMEM((gw, SSM_STATE), F32), pltpu.VMEM((2 * hpg, cl), F32)],
        compiler_params=pltpu.CompilerParams(dimension_semantics=("parallel", "arbitrary")),
    )(pre, pre, pre, dtT, bias, alog, dskip_lanes)


def _ssd_bwd(pre, dtT, bias, alog, dskip_lanes, states, dy, d_inner):
    t, conv_dim = pre.shape
    cl = SSD_CHUNK
    nc = t // cl
    ng = SSM_GROUPS
    hpg = dtT.shape[1]
    gw = hpg * SSM_HEAD_DIM
    bo = d_inner // SSM_STATE

    def body(px_ref, pb_ref, pc_ref, dtr_ref, bias_ref, alog_ref, dsk_ref, st_ref, dy_ref,
             dx_ref, db_ref, dc_ref, ddt_ref, acc_ref, dsk_out, ds_ref, cs_ref, dcs_ref):
        c = pl.program_id(1)

        @pl.when(c == 0)
        def _():
            ds_ref[...] = jnp.zeros_like(ds_ref)
            acc_ref[...] = jnp.zeros_like(acc_ref)
            dsk_out[...] = jnp.zeros_like(dsk_out)

        x, b16, c16, dt, a, g, causal = _ssd_chunk_common(px_ref, pb_ref, pc_ref, dtr_ref, bias_ref, alog_ref, cs_ref)
        s_prev = st_ref[...]
        s16 = s_prev.astype(BF16)
        ds = ds_ref[...]
        ds16 = ds.astype(BF16)
        dyv = dy_ref[...]
        yoff = lax.dot_general(c16, s16, (((1,), (1,)), ((), ())), preferred_element_type=F32)
        bds = lax.dot_general(b16, ds16, (((1,), (1,)), ((), ())), preferred_element_type=F32)
        dg = jnp.zeros((cl, cl), F32)
        xde_parts, dye_parts = [], []
        lane = lax.broadcasted_iota(jnp.int32, (1, LANES), 1)
        for j in range(hpg // 2):
            cols = slice(j * LANES, (j + 1) * LANES)
            xp, dyp = x[:, cols], dyv[:, cols]
            dcol, ecol, ocol, lms = [], [], [], []
            for hh in range(2):
                h = 2 * j + hh
                cs_row = cs_ref[h:h + 1, :]
                cs_col = _row_to_col(cs_row)
                dcol.append(_row_to_col(cs_ref[hpg + h:hpg + h + 1, :]))
                ecol.append(jnp.exp(cs_ref[h:h + 1, cl - 1:cl] - cs_col))
                ocol.append(jnp.exp(cs_col))
                lms.append(jnp.where(causal, jnp.exp(jnp.minimum(cs_col - cs_row, 0.0)), 0.0))
            dlanes, elanes, olanes = _pair_lanes(*dcol), _pair_lanes(*ecol), _pair_lanes(*ocol)
            xd = xp * dlanes
            xd16 = xd.astype(BF16)
            xde = xd * elanes
            yoffp = yoff[:, cols] * olanes
            bdsp = bds[:, cols]
            dxd = bdsp * elanes
            for hh in range(2):
                h = 2 * j + hh
                hmask = (lane // SSM_HEAD_DIM) == hh
                dyh16 = jnp.where(hmask, dyp, 0.0).astype(BF16)
                m = g * lms[hh]
                dm = lax.dot_general(dyh16, xd16, (((1,), (1,)), ((), ())), preferred_element_type=F32)
                w = dm * m
                dg = dg + dm * lms[hh]
                dxd = dxd + lax.dot_general(m.astype(BF16), dyh16, (((0,), (0,)), ((), ())),
                                            preferred_element_type=F32)
                term = jnp.sum(jnp.where(hmask, xde * bdsp, 0.0), axis=1, keepdims=True)
                dcs_col = (jnp.sum(w, axis=1, keepdims=True)
                           + jnp.sum(jnp.where(hmask, dyp * yoffp, 0.0), axis=1, keepdims=True) - term)
                rows = slice(h * SSM_HEAD_DIM, (h + 1) * SSM_HEAD_DIM)
                dec = jnp.exp(cs_ref[h:h + 1, cl - 1:cl])
                tail = jnp.sum(term, axis=0, keepdims=True) + dec * jnp.sum(
                    jnp.sum(s_prev[rows, :] * ds[rows, :], axis=1, keepdims=True), axis=0, keepdims=True)
                last = lax.broadcasted_iota(jnp.int32, (1, cl), 1) == cl - 1
                dcs_ref[h:h + 1, :] = _col_to_row(dcs_col) - jnp.sum(w, axis=0, keepdims=True) + jnp.where(last, tail, 0.0)
                dcs_ref[hpg + h:hpg + h + 1, :] = _col_to_row(
                    jnp.sum(jnp.where(hmask, dxd * xp, 0.0), axis=1, keepdims=True))
            dx_act = dxd * dlanes + dyp * dsk_ref[0:1, cols]
            dx_ref[:, cols] = dx_act * _silu_grad(px_ref[:, cols])
            dsk_out[0:1, cols] += jnp.sum(dyp * xp, axis=0, keepdims=True)
            xde_parts.append(xde.astype(BF16))
            dye_parts.append((dyp * olanes).astype(BF16))
        xde16 = jnp.concatenate(xde_parts, axis=1)
        dye16 = jnp.concatenate(dye_parts, axis=1)
        dg16 = dg.astype(BF16)
        dc_act = jnp.dot(dg16, b16, preferred_element_type=F32) + jnp.dot(dye16, s16, preferred_element_type=F32)
        db_act = lax.dot_general(dg16, c16, (((0,), (0,)), ((), ())), preferred_element_type=F32) \
            + jnp.dot(xde16, ds16, preferred_element_type=F32)
        dc_ref[...] = dc_act * _silu_grad(pc_ref[...])
        db_ref[...] = db_act * _silu_grad(pb_ref[...])
        ds_new = lax.dot_general(dye16, c16, (((0,), (0,)), ((), ())), preferred_element_type=F32)
        for h in range(hpg):
            rows = slice(h * SSM_HEAD_DIM, (h + 1) * SSM_HEAD_DIM)
            ds_ref[rows, :] = ds[rows, :] * jnp.exp(cs_ref[h:h + 1, cl - 1:cl]) + ds_new[rows, :]
        li = lax.broadcasted_iota(jnp.int32, (cl, cl), 0)
        si = lax.broadcasted_iota(jnp.int32, (cl, cl), 1)
        d_adt = jnp.dot(dcs_ref[0:hpg, :], (li >= si).astype(F32), precision=lax.Precision.HIGHEST,
                        preferred_element_type=F32)
        ddt = d_adt * a + dcs_ref[hpg:2 * hpg, :]
        ddt_raw = ddt * jax.nn.sigmoid(dtr_ref[...] + bias_ref[...])
        ddt_ref[...] = ddt_raw
        acc_ref[0:hpg, :] += d_adt * dt
        acc_ref[hpg:2 * hpg, :] += ddt_raw

    rc = lambda c: nc - 1 - c
    vec = lambda n: pl.BlockSpec((None, hpg, n), lambda gi, c: (gi, 0, 0))
    x_spec = pl.BlockSpec((cl, gw), lambda gi, c: (rc(c), gi))
    b_spec = pl.BlockSpec((cl, SSM_STATE), lambda gi, c: (rc(c), bo + gi))
    c_spec = pl.BlockSpec((cl, SSM_STATE), lambda gi, c: (rc(c), bo + ng + gi))
    dt_spec = pl.BlockSpec((None, hpg, cl), lambda gi, c: (gi, 0, rc(c)))
    return pl.pallas_call(
        body, name="ssd_bwd",
        out_shape=(jax.ShapeDtypeStruct((t, d_inner), F32), jax.ShapeDtypeStruct((t, ng * SSM_STATE), F32),
                   jax.ShapeDtypeStruct((t, ng * SSM_STATE), F32), jax.ShapeDtypeStruct(dtT.shape, F32),
                   jax.ShapeDtypeStruct((ng, 2 * hpg, cl), F32), jax.ShapeDtypeStruct((1, d_inner), F32)),
        grid=(ng, nc),
        in_specs=[x_spec, b_spec, c_spec, dt_spec, vec(1), vec(1),
                  pl.BlockSpec((1, gw), lambda gi, c: (0, gi)),
                  pl.BlockSpec((None, None, gw, SSM_STATE), lambda gi, c: (gi, rc(c), 0, 0)),
                  x_spec],
        out_specs=(x_spec, pl.BlockSpec((cl, SSM_STATE), lambda gi, c: (rc(c), gi)),
                   pl.BlockSpec((cl, SSM_STATE), lambda gi, c: (rc(c), gi)), dt_spec,
                   pl.BlockSpec((None, 2 * hpg, cl), lambda gi, c: (gi, 0, 0)),
                   pl.BlockSpec((1, gw), lambda gi, c: (0, gi))),
        scratch_shapes=[pltpu.VMEM((gw, SSM_STATE), F32), pltpu.VMEM((2 * hpg, cl), F32),
                        pltpu.VMEM((2 * hpg, cl), F32)],
        compiler_params=pltpu.CompilerParams(dimension_semantics=("parallel", "arbitrary")),
    )(pre, pre, pre, dtT, bias, alog, dskip_lanes, states, dy)


def _gate_norm_fwd(y, zx, norm_w, d_inner):
    t = y.shape[0]
    tr = _pick(t, (256, 128))
    gs = d_inner // SSM_GROUPS

    def body(y_ref, z_ref, w_ref, o_ref):
        for gi in range(SSM_GROUPS):
            cols = slice(gi * gs, (gi + 1) * gs)
            v = y_ref[:, cols] * _silu(z_ref[:, cols])
            r = lax.rsqrt(jnp.mean(v * v, axis=-1, keepdims=True) + NORM_EPS)
            o_ref[:, cols] = (v * r * w_ref[0:1, cols]).astype(BF16)

    spec = pl.BlockSpec((tr, d_inner), lambda i: (i, 0))
    return pl.pallas_call(
        body, name="ssm_gate_norm_fwd", out_shape=jax.ShapeDtypeStruct((t, d_inner), BF16), grid=(t // tr,),
        in_specs=[spec, spec, pl.BlockSpec((1, d_inner), lambda i: (0, 0))], out_specs=spec,
        compiler_params=pltpu.CompilerParams(dimension_semantics=("parallel",)),
    )(y, zx, norm_w.reshape(1, -1))


def _gate_norm_bwd(y, zx, norm_w, dout, d_inner):
    t, width = zx.shape
    tr = _pick(t, (256, 128))
    gs = d_inner // SSM_GROUPS

    def body(y_ref, z_ref, w_ref, do_ref, dy_ref, dz_ref, dw_ref):
        @pl.when(pl.program_id(0) == 0)
        def _():
            dw_ref[...] = jnp.zeros_like(dw_ref)

        for gi in range(SSM_GROUPS):
            cols = slice(gi * gs, (gi + 1) * gs)
            yv, zv = y_ref[:, cols], z_ref[:, cols]
            sz = _silu(zv)
            v = yv * sz
            r = lax.rsqrt(jnp.mean(v * v, axis=-1, keepdims=True) + NORM_EPS)
            vhat = v * r
            dn = do_ref[:, cols].astype(F32)
            dw_ref[0:1, cols] += jnp.sum(dn * vhat, axis=0, keepdims=True)
            dvh = dn * w_ref[0:1, cols]
            dv = r * (dvh - vhat * jnp.mean(dvh * vhat, axis=-1, keepdims=True))
            dy_ref[:, cols] = dv * sz
            dz_ref[:, cols] = (dv * yv * _silu_grad(zv)).astype(dz_ref.dtype)

    spec = pl.BlockSpec((tr, d_inner), lambda i: (i, 0))
    wspec = pl.BlockSpec((1, d_inner), lambda i: (0, 0))
    return pl.pallas_call(
        body, name="ssm_gate_norm_bwd",
        out_shape=(jax.ShapeDtypeStruct((t, d_inner), F32), jax.ShapeDtypeStruct((t, width), zx.dtype),
                   jax.ShapeDtypeStruct((1, d_inner), F32)),
        grid=(t // tr,),
        in_specs=[spec, spec, wspec, spec], out_specs=(spec, spec, wspec),
        compiler_params=pltpu.CompilerParams(dimension_semantics=("arbitrary",)),
    )(y, zx, norm_w.reshape(1, -1), dout)


def _ssm_small(dt_raw, dt_bias, a_log, d_skip):
    heads = dt_raw.shape[1]
    hpg = heads // SSM_GROUPS
    dtT = dt_raw.T.reshape(SSM_GROUPS, hpg, -1)
    return (dtT, dt_bias.reshape(SSM_GROUPS, hpg, 1), a_log.reshape(SSM_GROUPS, hpg, 1),
            jnp.repeat(d_skip, SSM_HEAD_DIM).reshape(1, -1))


def _ssm_core_fwd(zx, dt_raw, conv_w, conv_b, dt_bias, a_log, d_skip, norm_w):
    d_inner = norm_w.shape[0]
    pre = _conv_fwd(zx, conv_w, conv_b, d_inner)
    dtT, bias, alog, dsk = _ssm_small(dt_raw, dt_bias, a_log, d_skip)
    y, states = _ssd_fwd(pre, dtT, bias, alog, dsk, d_inner)
    out = _gate_norm_fwd(y, zx, norm_w, d_inner)
    return out, (zx, dt_raw, conv_w, dt_bias, a_log, d_skip, norm_w, pre, y, states)


def _ssm_core_bwd(res, dout):
    zx, dt_raw, conv_w, dt_bias, a_log, d_skip, norm_w, pre, y, states = res
    d_inner = norm_w.shape[0]
    heads = dt_raw.shape[1]
    dy, dzx, dnorm = _gate_norm_bwd(y, zx, norm_w, dout, d_inner)
    dtT, bias, alog, dsk = _ssm_small(dt_raw, dt_bias, a_log, d_skip)
    dx, db, dc, ddtT, acc, dsk_l = _ssd_bwd(pre, dtT, bias, alog, dsk, states, dy, d_inner)
    dpre = jnp.concatenate([dx, db, dc], axis=1)
    dzx, dconv_w, dconv_b = _conv_bwd(zx, conv_w, dpre, dzx, d_inner)
    d_dt_raw = ddtT.reshape(heads, -1).T
    hpg = heads // SSM_GROUPS
    da = acc[:, :hpg].sum(-1).reshape(heads)
    d_bias = acc[:, hpg:].sum(-1).reshape(heads)
    d_alog = da * (-jnp.exp(a_log))
    d_dskip = dsk_l.reshape(heads, SSM_HEAD_DIM).sum(-1)
    return dzx, d_dt_raw, dconv_w, dconv_b.reshape(-1), d_bias, d_alog, d_dskip, dnorm.reshape(-1)


def _rows_call(body, name, ins, outs, acc_outs=(), rows=256):
    t = max(a.shape[0] for a in ins)
    tr = _pick(t, (rows, 128, 64, 32, 16, 8))

    def spec(a):
        if a.shape[0] == t:
            return pl.BlockSpec((tr, a.shape[1]), lambda i: (i, 0))
        return pl.BlockSpec(a.shape, lambda i: (0, 0))

    return pl.pallas_call(
        body, name=name, out_shape=tuple(outs) + tuple(acc_outs), grid=(t // tr,),
        in_specs=[spec(a) for a in ins],
        out_specs=tuple(spec(a) for a in outs) + tuple(pl.BlockSpec(a.shape, lambda i: (0, 0)) for a in acc_outs),
        compiler_params=pltpu.CompilerParams(dimension_semantics=("arbitrary" if acc_outs else "parallel",)),
    )(*ins)


def _rms_fwd(x, gain, after=None):
    def body(x_ref, g_ref, *rest):
        v = x_ref[...]
        rest[-1][...] = (v * lax.rsqrt(jnp.mean(v * v, axis=-1, keepdims=True) + NORM_EPS) * g_ref[...]).astype(BF16)

    ins = [x, gain.reshape(1, -1)] + ([] if after is None else [after])
    (h,) = _rows_call(body, "rms_fwd", ins, [jax.ShapeDtypeStruct(x.shape, BF16)])
    return h


def _rms_bwd(x, gain, dh, dres, after):
    def body(x_ref, g_ref, dh_ref, dr_ref, *rest):
        dx_ref, dg_ref = rest[-2:]

        @pl.when(pl.program_id(0) == 0)
        def _():
            dg_ref[...] = jnp.zeros_like(dg_ref)

        v = x_ref[...]
        r = lax.rsqrt(jnp.mean(v * v, axis=-1, keepdims=True) + NORM_EPS)
        vhat = v * r
        d = dh_ref[...].astype(F32)
        dg_ref[...] += jnp.sum(d * vhat, axis=0, keepdims=True)
        dvh = d * g_ref[...]
        dx_ref[...] = dr_ref[...] + r * (dvh - vhat * jnp.mean(dvh * vhat, axis=-1, keepdims=True))

    ins = [x, gain.reshape(1, -1), dh, dres] + ([] if after is None else [after])
    dx, dg = _rows_call(body, "rms_bwd", ins, [jax.ShapeDtypeStruct(x.shape, F32)],
                        [jax.ShapeDtypeStruct((1, x.shape[1]), F32)])
    return dx, dg.reshape(gain.shape)


def _swiglu_fwd(gu):
    t, f2 = gu.shape
    f = f2 // 2

    def body(gu_ref, o_ref):
        o_ref[...] = (_silu(gu_ref[:, :f].astype(F32)) * gu_ref[:, f:].astype(F32)).astype(BF16)

    (act,) = _rows_call(body, "swiglu_fwd", [gu], [jax.ShapeDtypeStruct((t, f), BF16)])
    return act


def _swiglu_bwd(gu, dact):
    t, f2 = gu.shape
    f = f2 // 2

    def body(gu_ref, d_ref, o_ref):
        g, u, d = gu_ref[:, :f].astype(F32), gu_ref[:, f:].astype(F32), d_ref[...].astype(F32)
        o_ref[:, :f] = (d * u * _silu_grad(g)).astype(BF16)
        o_ref[:, f:] = (d * _silu(g)).astype(BF16)

    (dgu,) = _rows_call(body, "swiglu_bwd", [gu, dact], [jax.ShapeDtypeStruct((t, f2), BF16)])
    return dgu


def _ple_fwd(x, gl, ple):
    def body(x_ref, g_ref, p_ref, o_ref):
        o_ref[...] = x_ref[...] + jax.nn.sigmoid(g_ref[...]) * p_ref[...]

    (out,) = _rows_call(body, "ple_fwd", [x, gl, ple], [jax.ShapeDtypeStruct(x.shape, F32)])
    return out


def _ple_bwd(gl, ple, dout):
    def body(g_ref, p_ref, d_ref, dg_ref, dp_ref):
        s, d = jax.nn.sigmoid(g_ref[...]), d_ref[...]
        dg_ref[...] = (d * p_ref[...] * s * (1.0 - s)).astype(BF16)
        dp_ref[...] = (d * s).astype(BF16)

    shp = jax.ShapeDtypeStruct(gl.shape, BF16)
    return _rows_call(body, "ple_bwd", [gl, ple, dout], [shp, shp])


def _loss_fwd(y, target):
    inv = 1.0 / y.shape[1]

    def body(y_ref, t_ref, d_ref, l_ref):
        @pl.when(pl.program_id(0) == 0)
        def _():
            l_ref[...] = jnp.zeros_like(l_ref)

        e = y_ref[...] - t_ref[...]
        d_ref[...] = e * inv
        part = jnp.sum(jnp.sum(e * e, axis=1, keepdims=True), axis=0, keepdims=True) * (0.5 * inv)
        l_ref[...] += jnp.broadcast_to(part, l_ref.shape)

    dy, acc = _rows_call(body, "loss_fwd", [y, target], [jax.ShapeDtypeStruct(y.shape, F32)],
                         [jax.ShapeDtypeStruct((SUBLANES, LANES), F32)])
    return acc[0, 0], dy


def rmsnorm(x, gain):
    y = x * lax.rsqrt(jnp.mean(x * x, axis=-1, keepdims=True) + NORM_EPS)
    return y * gain


def causal_depthwise_conv(u, w, bias):
    k_width, chans = w.shape
    out = lax.conv_general_dilated(u, w[:, None, :], window_strides=(1,), padding=[(k_width - 1, 0)],
                                   dimension_numbers=("NWC", "WIO", "NWC"), feature_group_count=chans)
    return out + bias


def ssd_chunked(x, dt, a, bm, cm):
    b, t, heads, _ = x.shape
    nc, cl = t // SSD_CHUNK, SSD_CHUNK
    g, hg = SSM_GROUPS, heads // SSM_GROUPS
    xs = (x * dt[..., None]).reshape(b, nc, cl, g, hg, SSM_HEAD_DIM)
    a_dt = (dt * a).reshape(b, nc, cl, g, hg).transpose(0, 1, 3, 4, 2)
    a_cs = jnp.cumsum(a_dt, axis=-1)
    bc = bm.reshape(b, nc, cl, g, SSM_STATE)
    cc = cm.reshape(b, nc, cl, g, SSM_STATE)
    causal = jnp.tril(jnp.ones((cl, cl), dtype=bool))
    seg = a_cs[..., :, None] - a_cs[..., None, :]
    lmat = jnp.exp(jnp.where(causal, seg, -jnp.inf))
    cb = jnp.einsum("bclgn,bcsgn->bcgls", cc, bc)
    y_diag = jnp.einsum("bcgls,bcghls,bcsghp->bclghp", cb, lmat, xs)
    decay = jnp.exp(a_cs[..., -1:] - a_cs)
    states = jnp.einsum("bclgn,bcghl,bclghp->bcghpn", bc, decay, xs)
    chunk_decay = jnp.exp(a_cs[..., -1])

    def step(carry, inp):
        st, dec = inp
        return carry * dec[..., None, None] + st, carry

    init = jnp.zeros((b, g, hg, SSM_HEAD_DIM, SSM_STATE), F32)
    _, prev = lax.scan(step, init, (jnp.moveaxis(states, 1, 0), jnp.moveaxis(chunk_decay, 1, 0)))
    prev = jnp.moveaxis(prev, 0, 1)
    y_off = jnp.einsum("bclgn,bcghpn,bcghl->bclghp", cc, prev, jnp.exp(a_cs))
    return (y_diag + y_off).reshape(b, t, heads, SSM_HEAD_DIM)


def mamba2_mixer(h, wt_in, conv_w, conv_b, dt_bias, a_log, d_skip, norm_w, w_out):
    t, d_model = h.shape
    d_inner = 2 * d_model
    heads = d_inner // SSM_HEAD_DIM
    gn = SSM_GROUPS * SSM_STATE
    conv_dim = d_inner + 2 * gn
    zx = lin_t(h, wt_in[:d_inner + conv_dim])
    dt_raw = lin_t(h, wt_in[d_inner + conv_dim:])
    return lin(ssm_core(zx, dt_raw, conv_w, conv_b, dt_bias, a_log, d_skip, norm_w), w_out)


def ssm_core_jnp(zx, dt_raw, conv_w, conv_b, dt_bias, a_log, d_skip, norm_w):
    t = zx.shape[0]
    d_inner = norm_w.shape[0]
    heads = d_inner // SSM_HEAD_DIM
    gn = SSM_GROUPS * SSM_STATE
    z = zx[:, :d_inner]
    xbc = zx[:, d_inner:]
    xbc = jax.nn.silu(causal_depthwise_conv(xbc[None], conv_w, conv_b))[0]
    xs = xbc[:, :d_inner]
    bm = xbc[:, d_inner:d_inner + gn].reshape(1, t, SSM_GROUPS, SSM_STATE)
    cm = xbc[:, d_inner + gn:].reshape(1, t, SSM_GROUPS, SSM_STATE)
    dt = jax.nn.softplus(dt_raw + dt_bias)[None]
    a = -jnp.exp(a_log)
    xh = xs.reshape(1, t, heads, SSM_HEAD_DIM)
    y = ssd_chunked(xh, dt, a, bm, cm)
    y = y + xh * d_skip[:, None]
    y = y.reshape(t, d_inner) * jax.nn.silu(z)
    return rmsnorm(y.reshape(t, SSM_GROUPS, -1), norm_w.reshape(SSM_GROUPS, -1)).reshape(t, d_inner)


def alibi_slopes(n_heads):
    return 2.0 ** (-8.0 * jnp.arange(1, n_heads + 1, dtype=F32) / n_heads)


def dilated_group_attention(q, k, v, window, dilation, slopes):
    b, t, nh, e = q.shape
    span = window // dilation
    blk = span
    lu = t // dilation
    nb = -(-lu // blk)
    lp = nb * blk

    def to_blocks(arr):
        arr = arr.reshape(b, lu, dilation, nh, e)
        arr = jnp.pad(arr, ((0, 0), (0, lp - lu), (0, 0), (0, 0), (0, 0)))
        return arr.reshape(b, nb, blk, dilation, nh, e)

    qb, kb, vb = to_blocks(q), to_blocks(k), to_blocks(v)
    pad_prev = ((0, 0), (1, 0), (0, 0), (0, 0), (0, 0), (0, 0))
    kcat = jnp.concatenate([jnp.pad(kb, pad_prev)[:, :nb], kb], axis=2)
    vcat = jnp.concatenate([jnp.pad(vb, pad_prev)[:, :nb], vb], axis=2)
    scores = jnp.einsum("bnqrhe,bnkrhe->bnrhqk", qb, kcat) * (1.0 / math.sqrt(e))
    qi = jnp.arange(blk)[:, None]
    ki = jnp.arange(2 * blk)[None, :]
    dist = qi + blk - ki
    in_band = (dist >= 0) & (dist <= span)
    key_u = jnp.arange(nb)[:, None] * blk - blk + jnp.arange(2 * blk)[None, :]
    valid = in_band[None] & (key_u >= 0)[:, None, :]
    bias = -slopes[:, None, None] * (dilation * dist).astype(F32)[None]
    logits = jnp.where(valid[None, :, None, None], scores + bias[None, None, None], -jnp.inf)
    lse = jax.nn.logsumexp(logits, axis=-1)
    probs = jnp.exp(logits - lse[..., None])
    out = jnp.einsum("bnrhqk,bnkrhe->bnqrhe", probs, vcat)
    out = out.reshape(b, lp, dilation, nh, e)[:, :lu].reshape(b, t, nh, e)
    lse = lse.transpose(0, 1, 4, 2, 3).reshape(b, lp, dilation, nh)[:, :lu].reshape(b, t, nh)
    return out, lse


def dilated_attention_mixer(h, wt_qkv, q_gain, k_gain, w_o):
    t, d_model = h.shape
    heads = d_model // ATT_HEAD_DIM
    ng = len(DIL_PATTERNS)
    return lin(attention_core(lin_t(h, wt_qkv, BF16), q_gain, k_gain), w_o)


def attention_core_jnp(qkv, q_gain, k_gain):
    t = qkv.shape[0]
    ng = len(DIL_PATTERNS)
    heads = qkv.shape[1] // (3 * ng * ATT_HEAD_DIM)
    qkv = qkv.astype(F32).reshape(1, t, ng, 3, heads, ATT_HEAD_DIM)
    q = rmsnorm(qkv[:, :, :, 0], q_gain)
    k = rmsnorm(qkv[:, :, :, 1], k_gain)
    v = qkv[:, :, :, 2]
    slopes = alibi_slopes(heads)
    outs, lses = [], []
    for g, (window, dilation) in enumerate(DIL_PATTERNS):
        o_g, l_g = dilated_group_attention(q[:, :, g], k[:, :, g], v[:, :, g], window, dilation, slopes)
        outs.append(o_g)
        lses.append(l_g)
    alpha = jax.nn.softmax(jnp.stack(lses), axis=0)
    o = jnp.einsum("gbth,gbthe->bthe", alpha, jnp.stack(outs))
    return o.reshape(t, heads * ATT_HEAD_DIM)


def local_step(small, fetch, emit, x, p, target):
    depth = small['norm_mix'].shape[0]
    ssm_small = ('ssm_conv_w', 'ssm_conv_b', 'ssm_dt_bias', 'ssm_a_log', 'ssm_d_skip', 'ssm_norm_w')
    saved = []
    for i in range(depth):
        j = i // 2
        s = {'x': x}
        wm, token = fetch(2 * i, x)
        s['wm'] = wm
        h = s['h'] = _rms_fwd(x, small['norm_mix'][i], token)
        if i % 2 == 0:
            n_main = wm['ssm_w_in'].shape[0] - small['ssm_dt_bias'].shape[1]
            zx = _mm(h, wm['ssm_w_in'][:n_main], tb=True, name="ssm_in_fwd")
            dt_raw = _mm(h, wm['ssm_w_in'][n_main:], tb=True, name="ssm_dt_fwd")
            y, s['mix'] = _ssm_core_fwd(zx, dt_raw, wm['ssm_conv_w'], *[small[n][j] for n in ssm_small[1:]])
            x = _mm(y, wm['ssm_w_out'], add=x, name="ssm_out_fwd")
        else:
            qkv = _mm(h, wm['att_w_qkv'], tb=True, out_dtype=BF16, name="att_qkv_fwd")
            y, s['mix'] = _attention_core_fwd(qkv, small['att_q_norm'][j], small['att_k_norm'][j])
            x = _mm(y, wm['att_w_o'], add=x, name="att_o_fwd")
        s['y'], s['x1'] = y, x
        wf, token = fetch(2 * i + 1, x)
        s['wf'] = wf
        h2 = s['h2'] = _rms_fwd(x, small['norm_ffn'][i], token)
        gu = s['gu'] = _mm(h2, wf['ffn_w_gu'], tb=True, out_dtype=BF16, name="ffn_gu_fwd")
        act = s['act'] = _swiglu_fwd(gu)
        x = s['x2'] = _mm(act, wf['ffn_w_down'], add=x, name="ffn_down_fwd")
        gl = s['gl'] = _mm(x, wf['ple_w_gate'], name="ple_gate_fwd")
        ple = s['ple'] = _mm(p[i], wf['ple_w_proj'], tb=True, name="ple_proj_fwd")
        x = _ple_fwd(x, gl, ple)
        saved.append(s)
    loss, dx = _loss_fwd(x, target)

    g = {n: [None] * small[n].shape[0] for n in small}
    for i in reversed(range(depth)):
        j = i // 2
        s = saved[i]
        wm, wf = s['wm'], s['wf']
        gf = {}
        dgl, dple = _ple_bwd(s['gl'], s['ple'], dx)
        gf['ple_w_proj'] = _mm(dple, p[i], ta=True, out_dtype=BF16, name="ple_proj_dw")
        gf['ple_w_gate'] = _mm(s['x2'], dgl, ta=True, out_dtype=BF16, name="ple_gate_dw")
        dx = _mm(dgl, wf['ple_w_gate'], tb=True, add=dx, name="ple_gate_da")
        dact = _mm(dx, wf['ffn_w_down'], tb=True, out_dtype=BF16, name="ffn_down_da")
        gf['ffn_w_down'] = _mm(s['act'], dx, ta=True, out_dtype=BF16, name="ffn_down_dw")
        dgu = _swiglu_bwd(s['gu'], dact)
        dh2 = _mm(dgu, wf['ffn_w_gu'], out_dtype=BF16, name="ffn_gu_da")
        gf['ffn_w_gu'] = _mm(dgu, s['h2'], ta=True, out_dtype=BF16, name="ffn_gu_dw")
        dx, g['norm_ffn'][i] = _rms_bwd(s['x1'], small['norm_ffn'][i], dh2, dx, emit(2 * i + 1, gf))
        gm = {}
        if i % 2 == 0:
            n_main = wm['ssm_w_in'].shape[0] - small['ssm_dt_bias'].shape[1]
            dyn = _mm(dx, wm['ssm_w_out'], tb=True, out_dtype=BF16, name="ssm_out_da")
            gm['ssm_w_out'] = _mm(s['y'], dx, ta=True, out_dtype=BF16, name="ssm_out_dw")
            dzx, d_dt, *sg = _ssm_core_bwd(s['mix'], dyn)
            for n, v in zip(ssm_small, sg):
                g[n][j] = v
            dh = _mm(d_dt, wm['ssm_w_in'][n_main:], name="ssm_dt_da")
            dh = _mm(dzx, wm['ssm_w_in'][:n_main], add=dh, out_dtype=BF16, name="ssm_in_da")
            gm['ssm_w_in'] = jnp.concatenate([_mm(dzx, s['h'], ta=True, out_dtype=BF16, name="ssm_in_dw"),
                                              _mm(d_dt, s['h'], ta=True, out_dtype=BF16, name="ssm_dt_dw")], axis=0)
        else:
            do = _mm(dx, wm['att_w_o'], tb=True, name="att_o_da")
            gm['att_w_o'] = _mm(s['y'], dx, ta=True, out_dtype=BF16, name="att_o_dw")
            dqkv, g['att_q_norm'][j], g['att_k_norm'][j] = _attention_core_bwd(s['mix'], do)
            dh = _mm(dqkv, wm['att_w_qkv'], out_dtype=BF16, name="att_qkv_da")
            gm['att_w_qkv'] = _mm(dqkv, s['h'], ta=True, out_dtype=BF16, name="att_qkv_dw")
        dx, g['norm_mix'][i] = _rms_bwd(s['x'], small['norm_mix'][i], dh, dx, emit(2 * i, gm))
    return loss, dx, {n: jnp.stack(v) for n, v in g.items()}


MIXER_WEIGHTS = (('ssm_w_in', 'ssm_w_out', 'ssm_conv_w'), ('att_w_qkv', 'att_w_o'))
CHANNEL_WEIGHTS = ('ffn_w_gate', 'ffn_w_up', 'ffn_w_down', 'ple_w_proj', 'ple_w_gate')


def _pack_plan(shapes, width, stage):
    layer = stage // 2
    if stage % 2:
        members = [(n, layer) for n in CHANNEL_WEIGHTS]
    else:
        members = [(n, layer // 2) for n in MIXER_WEIGHTS[layer % 2]]
    plan, off = [], 0
    for name, lyr in members:
        _, r, c = shapes[name]
        if name in COL_SHARDED:
            r, c = c, r
        if name == 'ssm_conv_w':
            pr = -(-2 * r * c // width)
        else:
            assert (r * c) % width == 0, (name, r, c)
            pr = r * c // width
        plan.append((name, lyr, r, c, pr, off))
        off += _round_up(pr, BF16_ROWS)
    return plan, off


def _small_plan(shapes):
    plan, off = [], 0
    for name in SMALL:
        n = math.prod(shapes[name])
        plan.append((name, n, off))
        off += n
    return plan, _round_up(off, SUBLANES * LANES)


def kernel(x, p, norm_mix, norm_ffn, ssm_w_in, ssm_conv_w, ssm_conv_b, ssm_dt_bias, ssm_a_log, ssm_d_skip, ssm_norm_w, ssm_w_out, att_w_qkv, att_q_norm, att_k_norm, att_w_o, ffn_w_gate, ffn_w_up, ffn_w_down, ple_w_proj, ple_w_gate, loss_target, m_norm_mix, m_norm_ffn, m_ssm_w_in, m_ssm_conv_w, m_ssm_conv_b, m_ssm_dt_bias, m_ssm_a_log, m_ssm_d_skip, m_ssm_norm_w, m_ssm_w_out, m_att_w_qkv, m_att_q_norm, m_att_k_norm, m_att_w_o, m_ffn_w_gate, m_ffn_w_up, m_ffn_w_down, m_ple_w_proj, m_ple_w_gate, v_norm_mix, v_norm_ffn, v_ssm_w_in, v_ssm_conv_w, v_ssm_conv_b, v_ssm_dt_bias, v_ssm_a_log, v_ssm_d_skip, v_ssm_norm_w, v_ssm_w_out, v_att_w_qkv, v_att_q_norm, v_att_k_norm, v_att_w_o, v_ffn_w_gate, v_ffn_w_up, v_ffn_w_down, v_ple_w_proj, v_ple_w_gate):
    given = dict(locals())
    w_in = {n: given[n] for n in WEIGHTS}
    m_in = {n: given["m_" + n] for n in WEIGHTS}
    v_in = {n: given["v_" + n] for n in WEIGHTS}
    width = x.shape[-1]
    depth = norm_mix.shape[0]
    n_stages = 2 * depth

    plans = [_pack_plan({n: w_in[n].shape for n in BIG + ('ssm_conv_w',)}, width, stage) for stage in range(n_stages)]

    def pack_weights(stage):
        pieces = []
        for name, layer, r, c, pr, off in plans[stage][0]:
            blk = w_in[name][layer]
            if name == 'ssm_conv_w':
                blk = lax.bitcast_convert_type(blk.reshape(-1), BF16).reshape(-1)
                blk = jnp.pad(blk, (0, pr * width - blk.shape[0]))
            elif name in COL_SHARDED:
                blk = blk.T
            blk = blk.astype(BF16).reshape(pr, width)
            pieces.append(jnp.pad(blk, ((0, _round_up(pr, BF16_ROWS) - pr), (0, 0))))
        return jnp.concatenate(pieces, axis=0)

    packed = [pack_weights(0)]
    pending = [exchange_start(packed[0], True, "gather_start_0")]
    packed += [pack_weights(stage) for stage in range(1, n_stages)]

    def fetch(stage, after):
        handle, token = pending[stage]
        land = exchange_wait(handle, [token] + packed[1:] if stage == 0 else after, True, f"gather_wait_{stage}")
        token = None
        if stage + 1 < n_stages:
            pending.append(exchange_start(packed[stage + 1], True, f"gather_start_{stage + 1}", land))
            token = pending[-1][1]
        got = {}
        for name, layer, r, c, pr, off in plans[stage][0]:
            piece = land[:, off:off + pr]
            if name == 'ssm_conv_w':
                taps, chans = w_in[name].shape[1:]
                bits = piece.reshape(N_DEV, -1)[:, :2 * taps * chans].reshape(N_DEV, taps * chans, 2)
                piece = lax.bitcast_convert_type(bits, F32).reshape(N_DEV, taps, chans)
                got[name] = piece.transpose(1, 0, 2).reshape(taps, N_DEV * chans)
            else:
                got[name] = piece.reshape(N_DEV * r, c)
        if 'ffn_w_gate' in got:
            got['ffn_w_gu'] = jnp.concatenate([got.pop('ffn_w_gate'), got.pop('ffn_w_up')], axis=0)
        return got, token

    scatters = [None] * n_stages

    def emit(stage, grads):
        grads = dict(grads)
        if 'ffn_w_gu' in grads:
            hidden = grads['ffn_w_gu'].shape[0] // 2
            grads['ffn_w_gate'], grads['ffn_w_up'] = grads['ffn_w_gu'][:hidden], grads['ffn_w_gu'][hidden:]
        pieces = []
        for name, layer, r, c, pr, off in plans[stage][0]:
            if name == 'ssm_conv_w':
                continue
            g = grads[name].reshape(N_DEV, pr, width)
            pieces.append(jnp.pad(g, ((0, 0), (0, _round_up(pr, BF16_ROWS) - pr), (0, 0))))
        scatters[stage], token = exchange_start(jnp.concatenate(pieces, axis=1), False, f"scatter_start_{stage}")
        return token

    small = {n: w_in[n] for n in SMALL}
    cs = w_in['ssm_conv_w'].shape[2]
    loss_local, gx, gw = local_step(small, fetch, emit, x[0], p[:, 0], loss_target[0])
    loss = lax.psum(loss_local, ("x", "y", "c"))

    parts = {}
    for stage in reversed(range(n_stages)):
        received = exchange_wait(scatters[stage], gx, False, f"scatter_wait_{stage}")
        gsum = sum_slots(received, f"sum_grads_{stage}")
        for name, layer, r, c, pr, off in plans[stage][0]:
            if name == 'ssm_conv_w':
                continue
            g = gsum[off:off + pr].reshape(r, c)
            parts[name, layer] = g.T if name in COL_SHARDED else g
    grads = {n: jnp.stack([parts[n, layer] for layer in range(w_in[n].shape[0])]) for n in BIG}

    splan, stotal = _small_plan({n: gw[n].shape for n in SMALL})
    svec = jnp.concatenate([gw[n].reshape(-1) for n, _, _ in splan])
    svec = jnp.pad(svec, (0, stotal - svec.shape[0])).reshape(stotal // LANES, LANES)
    _, ssum = all_gather_sum_small(svec, "sum_small_grads")
    ssum = ssum.reshape(-1)
    for name, n, off in splan:
        grads[name] = ssum[off:off + n].reshape(gw[name].shape)
    me = _me()
    grads['ssm_conv_w'] = lax.dynamic_slice_in_dim(grads['ssm_conv_w'], me * cs, cs, axis=2)

    delta, new_m, new_v = {}, {}, {}
    for name in BIG:
        shp = w_in[name].shape
        flat = lambda a: a.reshape(-1, shp[-1])
        d, nm, nv = adamw(flat(w_in[name]), flat(grads[name]), flat(m_in[name]), flat(v_in[name]), "adamw_" + name)
        delta[name], new_m[name], new_v[name] = d.reshape(shp), nm.reshape(shp), nv.reshape(shp)
    splan2, stotal2 = _small_plan({n: w_in[n].shape for n in SMALL})

    def pack_small(src):
        vec = jnp.concatenate([src[n].reshape(-1) for n, _, _ in splan2])
        return jnp.pad(vec, (0, stotal2 - vec.shape[0]), constant_values=1.0).reshape(stotal2 // LANES, LANES)

    sd, snm, snv = adamw(pack_small(w_in), pack_small(grads), pack_small(m_in), pack_small(v_in), "adamw_small")
    for name, n, off in splan2:
        shp = w_in[name].shape
        delta[name] = sd.reshape(-1)[off:off + n].reshape(shp)
        new_m[name] = snm.reshape(-1)[off:off + n].reshape(shp)
        new_v[name] = snv.reshape(-1)[off:off + n].reshape(shp)

    return (loss, gx[None], *[grads[n] for n in WEIGHTS], *[delta[n] for n in WEIGHTS],
            *[new_m[n] for n in WEIGHTS], *[new_v[n] for n in WEIGHTS])
```

```python
import functools
import math

import jax
import jax.numpy as jnp
from jax import lax
from jax.experimental import pallas as pl
from jax.experimental.pallas import tpu as pltpu

F32 = jnp.float32
BF16 = jnp.bfloat16
N_DEV = 8
MESH = pl.DeviceIdType.MESH

SSM_HEAD_DIM = 64
SSM_GROUPS = 4
SSM_STATE = 128
CONV_WIDTH = 4
SSD_CHUNK = 128
ATT_HEAD_DIM = 64
DIL_PATTERNS = ((128, 1), (512, 4), (2048, 16))
NORM_EPS = 1e-6
ADAM_LR = 0.001
ADAM_B1 = 0.9
ADAM_B2 = 0.999
ADAM_EPS = 1e-08
ADAM_WD = 0.01
ADAM_STEP = 10

BF16_ROWS = 16
LANES = 128
SUBLANES = 8

WEIGHTS = ['norm_mix', 'norm_ffn', 'ssm_w_in', 'ssm_conv_w', 'ssm_conv_b', 'ssm_dt_bias', 'ssm_a_log', 'ssm_d_skip',
           'ssm_norm_w', 'ssm_w_out', 'att_w_qkv', 'att_q_norm', 'att_k_norm', 'att_w_o', 'ffn_w_gate', 'ffn_w_up',
           'ffn_w_down', 'ple_w_proj', 'ple_w_gate']
COL_SHARDED = ('ssm_w_in', 'att_w_qkv', 'ffn_w_gate', 'ffn_w_up', 'ple_w_proj')
ROW_SHARDED = ('ssm_w_out', 'att_w_o', 'ffn_w_down', 'ple_w_gate')
BIG = COL_SHARDED + ROW_SHARDED
SMALL = ('norm_mix', 'norm_ffn', 'ssm_conv_w', 'ssm_conv_b', 'ssm_dt_bias', 'ssm_a_log', 'ssm_d_skip', 'ssm_norm_w',
         'att_q_norm', 'att_k_norm')


def _pick(n, cands):
    for c in cands:
        if n % c == 0:
            return c
    return n


def _round_up(n, m):
    return -(-n // m) * m


MM_TILES = (1024, 1408, 512, 256, 128)
MM_VMEM_BYTES = 48 * 1024 * 1024


def _mm(a, b, *, ta=False, tb=False, out_dtype=F32, add=None, name):
    k_dim, m_dim = (a.shape if ta else a.shape[::-1])
    n_dim = b.shape[0] if tb else b.shape[1]
    assert (b.shape[1] if tb else b.shape[0]) == k_dim, (a.shape, b.shape, ta, tb)
    tm = _pick(m_dim, MM_TILES)
    tn = _pick(n_dim, MM_TILES)
    tk = _pick(k_dim, MM_TILES)
    nk = k_dim // tk
    a_spec = pl.BlockSpec((tk, tm), lambda i, j, k: (k, i)) if ta else pl.BlockSpec((tm, tk), lambda i, j, k: (i, k))
    b_spec = pl.BlockSpec((tn, tk), lambda i, j, k: (j, k)) if tb else pl.BlockSpec((tk, tn), lambda i, j, k: (k, j))
    o_spec = pl.BlockSpec((tm, tn), lambda i, j, k: (i, j))
    dims = (((0 if ta else 1,), (1 if tb else 0,)), ((), ()))
    has_add = add is not None

    def body(*refs):
        a_ref, b_ref = refs[:2]
        o_ref = refs[2 + has_add]

        def dot():
            return lax.dot_general(a_ref[...].astype(BF16), b_ref[...].astype(BF16), dims,
                                   preferred_element_type=F32)

        def finish(acc):
            if has_add:
                acc = acc + refs[2][...].astype(F32)
            o_ref[...] = acc.astype(o_ref.dtype)

        if nk == 1:
            finish(dot())
            return
        acc_ref = refs[3 + has_add]
        k = pl.program_id(2)

        @pl.when(k == 0)
        def _():
            acc_ref[...] = dot()

        @pl.when((k > 0) & (k < nk - 1))
        def _():
            acc_ref[...] += dot()

        @pl.when(k == nk - 1)
        def _():
            finish(acc_ref[...] + dot())

    return pl.pallas_call(
        body, name=f"{name}_{m_dim}x{n_dim}x{k_dim}",
        out_shape=jax.ShapeDtypeStruct((m_dim, n_dim), out_dtype),
        grid=(m_dim // tm, n_dim // tn, nk),
        in_specs=[a_spec, b_spec] + ([o_spec] if has_add else []),
        out_specs=o_spec,
        scratch_shapes=[] if nk == 1 else [pltpu.VMEM((tm, tn), F32)],
        compiler_params=pltpu.CompilerParams(dimension_semantics=("parallel", "parallel", "arbitrary"),
                                             vmem_limit_bytes=MM_VMEM_BYTES),
    )(*((a, b) + ((add,) if has_add else ())))


def _me():
    return 4 * lax.axis_index("x") + 2 * lax.axis_index("y") + lax.axis_index("c")


def _peer(j):
    x, y, c = lax.axis_index("x"), lax.axis_index("y"), lax.axis_index("c")
    px = 1 - x if j & 4 else x
    py = 1 - y if j & 2 else y
    pc = 1 - c if j & 1 else c
    return (px, py, pc), 4 * px + 2 * py + pc


def _exchange_body(src_of, dst_ref, send_sems, recv_sems, local_sem):
    me = _me()
    mine = pltpu.make_async_copy(src_of(me), dst_ref.at[me], local_sem)
    mine.start()
    sends = []
    for j in range(1, N_DEV):
        peer, pidx = _peer(j)
        cp = pltpu.make_async_remote_copy(src_ref=src_of(pidx), dst_ref=dst_ref.at[me], send_sem=send_sems.at[j - 1],
                                          recv_sem=recv_sems.at[j - 1], device_id=peer, device_id_type=MESH)
        cp.start()
        sends.append(cp)
    for j in range(1, N_DEV):
        peer, pidx = _peer(j)
        pltpu.make_async_remote_copy(src_ref=src_of(pidx), dst_ref=dst_ref.at[pidx], send_sem=send_sems.at[j - 1],
                                     recv_sem=recv_sems.at[j - 1], device_id=peer, device_id_type=MESH).wait_recv()
    for cp in sends:
        cp.wait_send()
    mine.wait()


_EXCHANGE_SCRATCH = [pltpu.SemaphoreType.DMA((N_DEV - 1,)), pltpu.SemaphoreType.DMA((N_DEV - 1,)),
                     pltpu.SemaphoreType.DMA]


def all_gather_hbm(shard, name):
    def body(x_ref, out_ref, send_sems, recv_sems, local_sem):
        _exchange_body(lambda k: x_ref, out_ref, send_sems, recv_sems, local_sem)

    return pl.pallas_call(
        body, name=name,
        out_shape=jax.ShapeDtypeStruct((N_DEV,) + shard.shape, shard.dtype),
        in_specs=[pl.BlockSpec(memory_space=pl.ANY)],
        out_specs=pl.BlockSpec(memory_space=pl.ANY),
        scratch_shapes=list(_EXCHANGE_SCRATCH),
    )(shard)


def all_to_all_hbm(slots, name):
    def body(x_ref, out_ref, send_sems, recv_sems, local_sem):
        _exchange_body(lambda k: x_ref.at[k], out_ref, send_sems, recv_sems, local_sem)

    return pl.pallas_call(
        body, name=name,
        out_shape=jax.ShapeDtypeStruct(slots.shape, slots.dtype),
        in_specs=[pl.BlockSpec(memory_space=pl.ANY)],
        out_specs=pl.BlockSpec(memory_space=pl.ANY),
        scratch_shapes=list(_EXCHANGE_SCRATCH),
    )(slots)


_HBM = pl.BlockSpec(memory_space=pltpu.HBM)
_SEM = pl.BlockSpec(memory_space=pltpu.SEMAPHORE)


def _split_copies(src_ref, gather, land_ref, send_sems, recv_sems):
    me = _me()
    pairs = []
    for j in range(1, N_DEV):
        peer, pidx = _peer(j)

        def make(slot, peer=peer, pidx=pidx, j=j):
            return pltpu.make_async_remote_copy(
                src_ref=src_ref if gather else src_ref.at[pidx], dst_ref=land_ref.at[slot],
                send_sem=send_sems.at[j - 1], recv_sem=recv_sems.at[j - 1], device_id=peer, device_id_type=MESH)

        pairs.append((make(me), make(pidx)))
    return pairs


def exchange_start(src, gather, name, after=None):
    land_shape = ((N_DEV,) + src.shape) if gather else src.shape
    has_after = after is not None

    def body(*refs):
        src_ref, land_ref = refs[:2]
        send_sems, recv_sems = refs[2 + has_after:4 + has_after]
        for send, _ in _split_copies(src_ref, gather, land_ref, send_sems, recv_sems):
            send.start()
        refs[-1][...] = jnp.zeros_like(refs[-1])

    sem = pltpu.SemaphoreType.DMA((N_DEV - 1,))
    send_sems, recv_sems, src_thru, land, token = pl.pallas_call(
        body, name=name,
        out_shape=(sem, sem, pltpu.HBM(src.shape, src.dtype), pltpu.HBM(land_shape, src.dtype),
                   jax.ShapeDtypeStruct((SUBLANES, LANES), F32)),
        in_specs=(_HBM, _HBM) + ((pl.BlockSpec(memory_space=pl.ANY),) if has_after else ()),
        out_specs=(_SEM, _SEM, _HBM, _HBM, pl.BlockSpec(memory_space=pltpu.VMEM)),
        input_output_aliases={0: 2, 1: 3},
        compiler_params=pltpu.CompilerParams(has_side_effects=pltpu.SideEffectType.DATAFLOW_SIDE_EFFECTING),
    )(pltpu.with_memory_space_constraint(src, pltpu.HBM),
      pltpu.with_memory_space_constraint(lax.empty(land_shape, src.dtype), pltpu.HBM),
      *((after,) if has_after else ()))
    return (send_sems, recv_sems, src_thru, land), token


def exchange_wait(handle, after, gather, name):
    send_sems, recv_sems, src_thru, land = handle
    after = tuple(after) if isinstance(after, (tuple, list)) else (after,)

    def body(src_ref, land_ref, send_sems, recv_sems, *rest):
        for _, arrival in _split_copies(src_ref, gather, land_ref, send_sems, recv_sems):
            arrival.wait_send()
            arrival.wait_recv()

    src_done, got = pl.pallas_call(
        body, name=name,
        out_shape=(pltpu.HBM(src_thru.shape, src_thru.dtype), pltpu.HBM(land.shape, land.dtype)),
        in_specs=(_HBM, _HBM, _SEM, _SEM) + (pl.BlockSpec(memory_space=pl.ANY),) * len(after), out_specs=(_HBM, _HBM),
        input_output_aliases={0: 0, 1: 1},
        compiler_params=pltpu.CompilerParams(has_side_effects=pltpu.SideEffectType.DATAFLOW_SIDE_EFFECTING),
    )(src_thru, land, send_sems, recv_sems, *after)
    mine = src_done if gather else lax.dynamic_index_in_dim(src_done, _me(), 0, keepdims=False)
    return lax.dynamic_update_index_in_dim(got, mine, _me(), 0)


def all_gather_sum_small(v, name):
    def body(x_ref, out_ref, sum_ref, send_sems, recv_sems, local_sem):
        _exchange_body(lambda k: x_ref, out_ref, send_sems, recv_sems, local_sem)
        acc = out_ref[0]
        for k in range(1, N_DEV):
            acc = acc + out_ref[k]
        sum_ref[...] = acc

    return pl.pallas_call(
        body, name=name,
        out_shape=(jax.ShapeDtypeStruct((N_DEV,) + v.shape, v.dtype), jax.ShapeDtypeStruct(v.shape, v.dtype)),
        in_specs=[pl.BlockSpec(memory_space=pltpu.VMEM)],
        out_specs=(pl.BlockSpec(memory_space=pltpu.VMEM), pl.BlockSpec(memory_space=pltpu.VMEM)),
        scratch_shapes=list(_EXCHANGE_SCRATCH),
    )(v)


def sum_slots(slots, name):
    _, p_dim, c_dim = slots.shape
    tp = next(tp for tp in range(512, 0, -BF16_ROWS) if p_dim % tp == 0)

    def body(x_ref, o_ref):
        acc = x_ref[0].astype(F32)
        for k in range(1, N_DEV):
            acc = acc + x_ref[k].astype(F32)
        o_ref[...] = acc

    return pl.pallas_call(
        body, name=name,
        out_shape=jax.ShapeDtypeStruct((p_dim, c_dim), F32),
        grid=(p_dim // tp,),
        in_specs=[pl.BlockSpec((N_DEV, tp, c_dim), lambda i: (0, i, 0))],
        out_specs=pl.BlockSpec((tp, c_dim), lambda i: (i, 0)),
        compiler_params=pltpu.CompilerParams(dimension_semantics=("parallel",)),
    )(slots)


def adamw(w, g, m, v, name):
    rows, cols = w.shape
    tr = _pick(rows, (256, 128, 64, 32, 16, 8))

    def body(w_ref, g_ref, m_ref, v_ref, d_ref, nm_ref, nv_ref):
        gv = g_ref[...]
        nm = ADAM_B1 * m_ref[...] + (1.0 - ADAM_B1) * gv
        nv = ADAM_B2 * v_ref[...] + (1.0 - ADAM_B2) * (gv * gv)
        m_hat = nm / (1.0 - ADAM_B1 ** ADAM_STEP)
        v_hat = nv / (1.0 - ADAM_B2 ** ADAM_STEP)
        d_ref[...] = -ADAM_LR * (m_hat / (jnp.sqrt(v_hat) + ADAM_EPS) + ADAM_WD * w_ref[...])
        nm_ref[...] = nm
        nv_ref[...] = nv

    spec = pl.BlockSpec((tr, cols), lambda i: (i, 0))
    shp = jax.ShapeDtypeStruct((rows, cols), F32)
    return pl.pallas_call(
        body, name=name, out_shape=(shp, shp, shp), grid=(rows // tr,),
        in_specs=[spec] * 4, out_specs=(spec,) * 3,
        compiler_params=pltpu.CompilerParams(dimension_semantics=("parallel",)),
    )(w, g, m, v)


ATT_BLK = 128
NEG = -1e30


def _head_sums(v):
    li = lax.broadcasted_iota(jnp.int32, (LANES, LANES), 0) // ATT_HEAD_DIM
    lj = lax.broadcasted_iota(jnp.int32, (LANES, LANES), 1) // ATT_HEAD_DIM
    ones = (li == lj).astype(BF16)
    hi = v.astype(BF16)
    lo = (v - hi.astype(F32)).astype(BF16)
    return jnp.dot(hi, ones, preferred_element_type=F32) + jnp.dot(lo, ones, preferred_element_type=F32)


def _head_col(v, hmask):
    return jnp.max(jnp.where(hmask, v, -jnp.inf), axis=-1, keepdims=True)


def _qk_norm(raw, gain2):
    rstd = lax.rsqrt(_head_sums(raw * raw) * (1.0 / ATT_HEAD_DIM) + NORM_EPS)
    xhat = raw * rstd
    return xhat * gain2, xhat, rstd


def _qk_norm_bwd(dn, xhat, rstd, gain2):
    dxh = dn * gain2
    return rstd * (dxh - xhat * (_head_sums(dxh * xhat) * (1.0 / ATT_HEAD_DIM))), dn * xhat


def _att_mask_bias(n, dilation):
    qi = lax.broadcasted_iota(jnp.int32, (ATT_BLK, 2 * ATT_BLK), 0)
    ki = lax.broadcasted_iota(jnp.int32, (ATT_BLK, 2 * ATT_BLK), 1)
    dist = qi + ATT_BLK - ki
    valid = (dist >= 0) & (dist <= ATT_BLK) & ((n > 0) | (ki >= ATT_BLK))
    return valid, (dilation * dist).astype(F32)


ATT_PAIRS = 4
RELAYOUT_ROWS = 512
RELAYOUT_COLS = 512


def _to_residues(x, dilation, col0=0, cols=None):
    t = x.shape[0]
    cols = x.shape[1] if cols is None else cols
    if dilation == 1 and col0 == 0 and cols == x.shape[1]:
        return x.reshape(1, t, cols)
    tr = _pick(t, (RELAYOUT_ROWS,))
    tc = _pick(cols, (RELAYOUT_COLS, 256, 128))
    per = tr // dilation
    assert tr % dilation == 0 and col0 % tc == 0

    def body(x_ref, o_ref, s_ref):
        for c in range(tc // LANES):
            lanes = slice(c * LANES, (c + 1) * LANES)
            s_ref[c] = x_ref[:, lanes].astype(F32)
            for r in range(dilation):
                o_ref[r, :, lanes] = s_ref[c, pl.ds(r, per, stride=dilation), :].astype(o_ref.dtype)

    return pl.pallas_call(
        body, name=f"to_residues_{dilation}", out_shape=jax.ShapeDtypeStruct((dilation, t // dilation, cols), x.dtype),
        grid=(t // tr, cols // tc),
        in_specs=[pl.BlockSpec((tr, tc), lambda i, j: (i, col0 // tc + j))],
        out_specs=pl.BlockSpec((dilation, per, tc), lambda i, j: (0, i, j)),
        scratch_shapes=[pltpu.VMEM((tc // LANES, tr, LANES), F32)],
        compiler_params=pltpu.CompilerParams(dimension_semantics=("parallel", "parallel")),
    )(x)


def _from_residues(y):
    dilation, lu, cols = y.shape
    t = dilation * lu
    if dilation == 1:
        return y.reshape(t, cols)
    tr = _pick(t, (RELAYOUT_ROWS,))
    tc = _pick(cols, (RELAYOUT_COLS, 256, 128))
    per = tr // dilation

    def body(y_ref, o_ref, s_ref):
        for c in range(tc // LANES):
            lanes = slice(c * LANES, (c + 1) * LANES)
            for r in range(dilation):
                s_ref[c, pl.ds(r, per, stride=dilation), :] = y_ref[r, :, lanes].astype(F32)
            o_ref[:, lanes] = s_ref[c].astype(o_ref.dtype)

    return pl.pallas_call(
        body, name=f"from_residues_{dilation}", out_shape=jax.ShapeDtypeStruct((t, cols), y.dtype),
        grid=(t // tr, cols // tc),
        in_specs=[pl.BlockSpec((dilation, per, tc), lambda i, j: (0, i, j))],
        out_specs=pl.BlockSpec((tr, tc), lambda i, j: (i, j)),
        scratch_shapes=[pltpu.VMEM((tc // LANES, tr, LANES), F32)],
        compiler_params=pltpu.CompilerParams(dimension_semantics=("parallel", "parallel")),
    )(y)


def _att_specs(base, hd, nb, pp):
    hpn = hd // LANES
    assert base % (pp * LANES) == 0 and hpn % pp == 0

    def spec(which, shift):
        def imap(r, hp, n):
            row = jnp.minimum(n, nb - 1) if shift == 0 else jnp.maximum(n - 1, 0)
            return (r, row, (base // LANES + which * hpn) // pp + hp)
        return pl.BlockSpec((None, ATT_BLK, pp * LANES), imap)

    return [spec(0, 0), spec(1, 1), spec(1, 0), spec(2, 1), spec(2, 0)]


DELTA_LANE = 64


def _att_pre(qkv, g, dilation, gq2, gk2):
    t, width = qkv.shape
    hd = width // (3 * len(DIL_PATTERNS))
    tr = _pick(t, (RELAYOUT_ROWS,))
    tc = _pick(hd, (RELAYOUT_COLS, 256, 128))
    per = tr // dilation
    assert tr % dilation == 0 and (g * 3 * hd) % tc == 0

    def body(x_ref, gq_ref, gk_ref, o_ref, s_ref):
        section = pl.program_id(1) * tc // hd
        gain = jnp.where(section == 0, gq_ref[0:1, :], gk_ref[0:1, :])
        for c in range(tc // LANES):
            lanes = slice(c * LANES, (c + 1) * LANES)
            raw = x_ref[:, lanes].astype(F32)
            s_ref[c] = jnp.where(section == 2, raw, _qk_norm(raw, gain)[0])
            for r in range(dilation):
                o_ref[r, :, lanes] = s_ref[c, pl.ds(r, per, stride=dilation), :].astype(o_ref.dtype)

    vec_spec = pl.BlockSpec((SUBLANES, LANES), lambda i, j: (0, 0))
    return pl.pallas_call(
        body, name=f"att_pre_g{g}", out_shape=jax.ShapeDtypeStruct((dilation, t // dilation, 3 * hd), qkv.dtype),
        grid=(t // tr, 3 * hd // tc),
        in_specs=[pl.BlockSpec((tr, tc), lambda i, j: (i, g * 3 * hd // tc + j)), vec_spec, vec_spec],
        out_specs=pl.BlockSpec((dilation, per, tc), lambda i, j: (0, i, j)),
        scratch_shapes=[pltpu.VMEM((tc // LANES, tr, LANES), F32)],
        compiler_params=pltpu.CompilerParams(dimension_semantics=("parallel", "parallel")),
    )(qkv, gq2, gk2)


def _att_group_fwd(qkv_r, hd, slopes, g):
    dilation, lu, _ = qkv_r.shape
    nb = lu // ATT_BLK
    assert nb * ATT_BLK == lu and hd % LANES == 0
    hpn = hd // LANES
    pp = math.gcd(ATT_PAIRS, hpn)
    scale = 1.0 / math.sqrt(ATT_HEAD_DIM)

    def body(q_ref, kp_ref, kc_ref, vp_ref, vc_ref, sl_ref, o_ref, l_ref):
        n = pl.program_id(2)
        lane = lax.broadcasted_iota(jnp.int32, (1, LANES), 1)
        first = (lane // ATT_HEAD_DIM) == 0
        valid, dist = _att_mask_bias(n, dilation)
        stats = jnp.zeros((ATT_BLK, LANES), F32)
        for pair in range(pp):
            cols = slice(pair * LANES, (pair + 1) * LANES)
            qn = q_ref[:, cols]
            kn16 = jnp.concatenate([kp_ref[:, cols], kc_ref[:, cols]], axis=0)
            v16 = jnp.concatenate([vp_ref[:, cols], vc_ref[:, cols]], axis=0)
            outs = []
            for hh in range(2):
                hmask = (lane // ATT_HEAD_DIM) == hh
                qh = jnp.where(hmask, qn, jnp.zeros_like(qn))
                s = lax.dot_general(qh, kn16, (((1,), (1,)), ((), ())), preferred_element_type=F32) * scale
                slope = _head_col(sl_ref[pair, 0:1, :], hmask)
                logits = jnp.where(valid, s - slope * dist, NEG)
                mx = jnp.max(logits, axis=-1, keepdims=True)
                pexp = jnp.exp(logits - mx)
                den = jnp.sum(pexp, axis=-1, keepdims=True)
                outs.append(jnp.dot(pexp.astype(BF16), v16, preferred_element_type=F32) / den)
                stats = jnp.where(lane == 2 * pair + hh, mx + jnp.log(den), stats)
            o_ref[:, cols] = jnp.where(first, outs[0], outs[1]).astype(BF16)
        l_ref[...] = stats

    out_spec = pl.BlockSpec((None, ATT_BLK, pp * LANES), lambda r, hp, n: (r, n, hp))
    stat_spec = pl.BlockSpec((None, ATT_BLK, LANES), lambda r, hp, n: (r, n, hp))
    o, lse = pl.pallas_call(
        body, name=f"att_fwd_g{g}",
        out_shape=(jax.ShapeDtypeStruct((dilation, lu, hd), BF16),
                   jax.ShapeDtypeStruct((dilation, lu, hpn // pp * LANES), F32)),
        grid=(dilation, hpn // pp, nb),
        in_specs=_att_specs(0, hd, nb, pp) + [pl.BlockSpec((pp, SUBLANES, LANES), lambda r, hp, n: (hp, 0, 0))],
        out_specs=(out_spec, stat_spec),
        compiler_params=pltpu.CompilerParams(dimension_semantics=("parallel", "parallel", "arbitrary")),
    )(*([qkv_r] * 5), slopes)
    return _from_residues(o), _from_residues(lse)


def _att_merge(outs, lses):
    t, hd = outs[0].shape
    sw = lses[0].shape[1]
    pp = hd // sw
    tr = _pick(t, (256, 128))
    ng = len(outs)

    def body(*refs):
        o_refs, l_refs, o16_ref, lt_ref = refs[:ng], refs[ng:2 * ng], refs[2 * ng], refs[2 * ng + 1]
        lane = lax.broadcasted_iota(jnp.int32, (1, LANES), 1)
        first = (lane // ATT_HEAD_DIM) == 0
        for blk in range(sw // LANES):
            scols = slice(blk * LANES, (blk + 1) * LANES)
            stats = jnp.zeros((tr, LANES), F32)
            for pair in range(pp):
                cols = slice((blk * pp + pair) * LANES, (blk * pp + pair + 1) * LANES)
                weights = []
                for hh in range(2):
                    pick = lane == 2 * pair + hh
                    ls = [_head_col(r[:, scols], pick) for r in l_refs]
                    mx = functools.reduce(jnp.maximum, ls)
                    es = [jnp.exp(l - mx) for l in ls]
                    den = functools.reduce(jnp.add, es)
                    weights.append([e / den for e in es])
                    stats = jnp.where(pick, mx + jnp.log(den), stats)
                acc = jnp.zeros((tr, LANES), F32)
                for gi in range(ng):
                    acc = acc + jnp.where(first, weights[0][gi], weights[1][gi]) * o_refs[gi][:, cols].astype(F32)
                o16_ref[:, cols] = acc.astype(BF16)
            lt_ref[:, scols] = stats

    spec = pl.BlockSpec((tr, hd), lambda i: (i, 0))
    sspec = pl.BlockSpec((tr, sw), lambda i: (i, 0))
    return pl.pallas_call(
        body, name="att_merge",
        out_shape=(jax.ShapeDtypeStruct((t, hd), BF16), jax.ShapeDtypeStruct((t, sw), F32)), grid=(t // tr,),
        in_specs=[spec] * ng + [sspec] * ng, out_specs=(spec, sspec),
        compiler_params=pltpu.CompilerParams(dimension_semantics=("parallel",)),
    )(*outs, *lses)


def _att_bwd_prep(do, o16, lse_tot):
    t, hd = do.shape
    sw = lse_tot.shape[1]
    pp = hd // sw
    tr = _pick(t, (256, 128))

    def body(do_ref, o_ref, l_ref, d16_ref, st_ref):
        lane = lax.broadcasted_iota(jnp.int32, (1, LANES), 1)
        d16_ref[...] = do_ref[...].astype(BF16)
        for blk in range(sw // LANES):
            scols = slice(blk * LANES, (blk + 1) * LANES)
            stats = l_ref[:, scols]
            for pair in range(pp):
                cols = slice((blk * pp + pair) * LANES, (blk * pp + pair + 1) * LANES)
                prod = do_ref[:, cols] * o_ref[:, cols].astype(F32)
                for hh in range(2):
                    hmask = (lane // ATT_HEAD_DIM) == hh
                    delta = jnp.sum(jnp.where(hmask, prod, 0.0), axis=-1, keepdims=True)
                    stats = jnp.where(lane == DELTA_LANE + 2 * pair + hh, delta, stats)
            st_ref[:, scols] = stats

    spec = pl.BlockSpec((tr, hd), lambda i: (i, 0))
    sspec = pl.BlockSpec((tr, sw), lambda i: (i, 0))
    return pl.pallas_call(
        body, name="att_bwd_prep",
        out_shape=(jax.ShapeDtypeStruct((t, hd), BF16), jax.ShapeDtypeStruct((t, sw), F32)), grid=(t // tr,),
        in_specs=[spec, spec, sspec], out_specs=(spec, sspec),
        compiler_params=pltpu.CompilerParams(dimension_semantics=("parallel",)),
    )(do, o16, lse_tot)


def _att_post(buf, parts, qkv, gq2, gk2, g):
    dilation, lu, hd = parts[0].shape
    t = dilation * lu
    tr = _pick(t, (RELAYOUT_ROWS,))
    per = tr // dilation

    def body(*refs):
        raw_ref, gq_ref, gk_ref = refs[3:6]
        o_ref, dgq_ref, dgk_ref, s_ref = refs[-4:]

        @pl.when(pl.program_id(0) == 0)
        def _():
            dgq_ref[...] = jnp.zeros_like(dgq_ref)
            dgk_ref[...] = jnp.zeros_like(dgk_ref)

        for sec, y_ref in enumerate(refs[:3]):
            gsum = jnp.zeros((1, LANES), F32)
            for c in range(hd // LANES):
                lanes = slice(c * LANES, (c + 1) * LANES)
                out_lanes = slice(sec * hd + c * LANES, sec * hd + (c + 1) * LANES)
                for r in range(dilation):
                    s_ref[pl.ds(r, per, stride=dilation), :] = y_ref[r, :, lanes].astype(F32)
                d = s_ref[...]
                if sec < 2:
                    gain = (gq_ref if sec == 0 else gk_ref)[0:1, :]
                    _, xhat, rstd = _qk_norm(raw_ref[:, out_lanes].astype(F32), gain)
                    d, part = _qk_norm_bwd(d, xhat, rstd, gain)
                    gsum = gsum + jnp.sum(part, axis=0, keepdims=True)
                o_ref[:, out_lanes] = d.astype(o_ref.dtype)
            if sec < 2:
                acc = dgq_ref if sec == 0 else dgk_ref
                acc[...] += jnp.broadcast_to(gsum, acc.shape)

    part_spec = pl.BlockSpec((dilation, per, hd), lambda i: (0, i, 0))
    slab_spec = pl.BlockSpec((tr, 3 * hd), lambda i: (i, g))
    vec_spec = pl.BlockSpec((SUBLANES, LANES), lambda i: (0, 0))
    vec_shape = jax.ShapeDtypeStruct((SUBLANES, LANES), F32)
    return pl.pallas_call(
        body, name=f"att_post_g{g}", out_shape=(jax.ShapeDtypeStruct(qkv.shape, qkv.dtype), vec_shape, vec_shape),
        grid=(t // tr,),
        in_specs=[part_spec] * 3 + [slab_spec, vec_spec, vec_spec] + (
            [] if buf is None else [pl.BlockSpec(memory_space=pl.ANY)]),
        out_specs=(slab_spec, vec_spec, vec_spec),
        scratch_shapes=[pltpu.VMEM((tr, LANES), F32)],
        input_output_aliases={} if buf is None else {6: 0},
        compiler_params=pltpu.CompilerParams(dimension_semantics=("arbitrary",)),
    )(*parts, qkv, gq2, gk2, *(() if buf is None else (buf,)))


def _att_group_bwd(qkv_r, hd, slopes, stats, do16, g):
    dilation, lu, _ = qkv_r.shape
    nb = lu // ATT_BLK
    hpn = hd // LANES
    pp = math.gcd(ATT_PAIRS, hpn)
    hbn = hpn // pp
    scale = 1.0 / math.sqrt(ATT_HEAD_DIM)

    def body(q_ref, kp_ref, kc_ref, vp_ref, vc_ref, sl_ref, st_ref, do_ref, dq_ref, dk_ref, dv_ref, ck_ref, cv_ref):
        n = pl.program_id(2)
        lane = lax.broadcasted_iota(jnp.int32, (1, LANES), 1)

        @pl.when(n == 0)
        def _():
            ck_ref[...] = jnp.zeros_like(ck_ref)
            cv_ref[...] = jnp.zeros_like(cv_ref)

        @pl.when(n < nb)
        def _():
            valid, dist = _att_mask_bias(n, dilation)
            stats = st_ref[...]
            for pair in range(pp):
                cols = slice(pair * LANES, (pair + 1) * LANES)
                qn = q_ref[:, cols]
                kn16 = jnp.concatenate([kp_ref[:, cols], kc_ref[:, cols]], axis=0)
                v16 = jnp.concatenate([vp_ref[:, cols], vc_ref[:, cols]], axis=0)
                dov = do_ref[:, cols]
                dq_acc = jnp.zeros((ATT_BLK, LANES), F32)
                dk_acc = jnp.zeros((2 * ATT_BLK, LANES), F32)
                dv_acc = jnp.zeros((2 * ATT_BLK, LANES), F32)
                for hh in range(2):
                    hmask = (lane // ATT_HEAD_DIM) == hh
                    qh = jnp.where(hmask, qn, jnp.zeros_like(qn))
                    doh = jnp.where(hmask, dov, jnp.zeros_like(dov))
                    s = lax.dot_general(qh, kn16, (((1,), (1,)), ((), ())), preferred_element_type=F32) * scale
                    slope = _head_col(sl_ref[pair, 0:1, :], hmask)
                    lse = _head_col(stats, lane == 2 * pair + hh)
                    delta = _head_col(stats, lane == DELTA_LANE + 2 * pair + hh)
                    pr = jnp.exp(jnp.where(valid, s - slope * dist - lse, NEG))
                    dp = lax.dot_general(doh, v16, (((1,), (1,)), ((), ())), preferred_element_type=F32)
                    ds = (pr * (dp - delta) * scale).astype(BF16)
                    dq_acc = dq_acc + jnp.where(hmask, jnp.dot(ds, kn16, preferred_element_type=F32), 0.0)
                    dk_acc = dk_acc + lax.dot_general(ds, qh, (((0,), (0,)), ((), ())), preferred_element_type=F32)
                    dv_acc = dv_acc + lax.dot_general(pr.astype(BF16), doh, (((0,), (0,)), ((), ())),
                                                      preferred_element_type=F32)
                dq_ref[:, cols] = dq_acc.astype(dq_ref.dtype)
                dk_ref[:, cols] = (ck_ref[:, cols] + dk_acc[:ATT_BLK]).astype(dk_ref.dtype)
                dv_ref[:, cols] = (cv_ref[:, cols] + dv_acc[:ATT_BLK]).astype(dv_ref.dtype)
                ck_ref[:, cols] = dk_acc[ATT_BLK:]
                cv_ref[:, cols] = dv_acc[ATT_BLK:]

        @pl.when(n == nb)
        def _():
            dk_ref[...] = ck_ref[...].astype(dk_ref.dtype)
            dv_ref[...] = cv_ref[...].astype(dv_ref.dtype)

    width = pp * LANES
    q_out = pl.BlockSpec((None, ATT_BLK, width), lambda r, hp, n: (r, jnp.minimum(n, nb - 1), hp))
    kv_out = pl.BlockSpec((None, ATT_BLK, width), lambda r, hp, n: (r, jnp.maximum(n - 1, 0), hp))
    st_spec = pl.BlockSpec((None, ATT_BLK, LANES), lambda r, hp, n: (r, jnp.minimum(n, nb - 1), hp))
    shp = jax.ShapeDtypeStruct((dilation, lu, hd), BF16)
    return pl.pallas_call(
        body, name=f"att_bwd_g{g}", out_shape=(shp, shp, shp), grid=(dilation, hbn, nb + 1),
        in_specs=_att_specs(0, hd, nb, pp) + [
            pl.BlockSpec((pp, SUBLANES, LANES), lambda r, hp, n: (hp, 0, 0)), st_spec, q_out],
        out_specs=(q_out, kv_out, kv_out),
        scratch_shapes=[pltpu.VMEM((ATT_BLK, width), F32), pltpu.VMEM((ATT_BLK, width), F32)],
        compiler_params=pltpu.CompilerParams(dimension_semantics=("parallel", "parallel", "arbitrary")),
    )(*([qkv_r] * 5), slopes, _to_residues(stats, dilation), _to_residues(do16, dilation))


def _att_consts(q_gain, k_gain, hd):
    heads = hd // ATT_HEAD_DIM
    gq2 = jnp.broadcast_to(jnp.tile(q_gain, 2)[None], (SUBLANES, LANES))
    gk2 = jnp.broadcast_to(jnp.tile(k_gain, 2)[None], (SUBLANES, LANES))
    sl = 2.0 ** (-8.0 * jnp.arange(1, heads + 1, dtype=F32) / heads)
    slopes = jnp.broadcast_to(jnp.repeat(sl, ATT_HEAD_DIM).reshape(hd // LANES, 1, LANES), (hd // LANES, SUBLANES, LANES))
    return gq2, gk2, slopes


def _attention_core_fwd(qkv, q_gain, k_gain):
    hd = qkv.shape[1] // (3 * len(DIL_PATTERNS))
    gq2, gk2, slopes = _att_consts(q_gain, k_gain, hd)
    outs, lses, views = [], [], []
    for g, (_, dilation) in enumerate(DIL_PATTERNS):
        qkv_r = _att_pre(qkv, g, dilation, gq2, gk2)
        o_g, l_g = _att_group_fwd(qkv_r, hd, slopes, g)
        outs.append(o_g)
        lses.append(l_g)
        views.append(qkv_r)
    o16, lse_tot = _att_merge(outs, lses)
    return o16, (qkv, views, q_gain, k_gain, o16, lse_tot)


def _attention_core_bwd(res, do):
    qkv, views, q_gain, k_gain, o16, lse_tot = res
    hd = o16.shape[1]
    gq2, gk2, slopes = _att_consts(q_gain, k_gain, hd)
    do16, stats = _att_bwd_prep(do, o16, lse_tot)
    dqkv, dgq, dgk = None, 0.0, 0.0
    for g, qkv_r in enumerate(views):
        parts = _att_group_bwd(qkv_r, hd, slopes, stats, do16, g)
        dqkv, a, b = _att_post(dqkv, parts, qkv, gq2, gk2, g)
        dgq = dgq + a[0].reshape(-1, ATT_HEAD_DIM).sum(0)
        dgk = dgk + b[0].reshape(-1, ATT_HEAD_DIM).sum(0)
    return dqkv, dgq, dgk


HALO = 8


def _silu(v):
    return v * jax.nn.sigmoid(v)


def _silu_grad(v):
    s = jax.nn.sigmoid(v)
    return s * (1.0 + v * (1.0 - s))


def _conv_fwd(zx, conv_w, conv_b, d_inner):
    t = zx.shape[0]
    conv_dim = conv_w.shape[1]
    cb = _pick(d_inner, (1024, 512, 256, 128))
    assert conv_dim % cb == 0
    tr = _pick(t, (256, 128))
    off = d_inner // cb

    def body(x_ref, h_ref, w_ref, b_ref, o_ref):
        i = pl.program_id(1)
        halo = jnp.where(i > 0, h_ref[...], 0.0)
        ext = jnp.concatenate([halo, x_ref[...]], axis=0)
        acc = jnp.broadcast_to(b_ref[...], (tr, cb))
        for k in range(CONV_WIDTH):
            s = CONV_WIDTH - 1 - k
            sh = ext if s == 0 else pltpu.roll(ext, shift=s, axis=0)
            acc = acc + w_ref[k:k + 1, :] * sh[HALO:HALO + tr]
        o_ref[...] = acc

    return pl.pallas_call(
        body, name="ssm_conv_fwd", out_shape=jax.ShapeDtypeStruct((t, conv_dim), F32),
        grid=(conv_dim // cb, t // tr),
        in_specs=[pl.BlockSpec((tr, cb), lambda j, i: (i, off + j)),
                  pl.BlockSpec((HALO, cb), lambda j, i: (jnp.maximum(i * (tr // HALO) - 1, 0), off + j)),
                  pl.BlockSpec((CONV_WIDTH, cb), lambda j, i: (0, j)),
                  pl.BlockSpec((1, cb), lambda j, i: (0, j))],
        out_specs=pl.BlockSpec((tr, cb), lambda j, i: (i, j)),
        compiler_params=pltpu.CompilerParams(dimension_semantics=("parallel", "parallel")),
    )(zx, zx, conv_w, conv_b.reshape(1, -1))


def _conv_bwd(zx, conv_w, dpre, dzx, d_inner, col0):
    t, width = zx.shape
    conv_dim = dpre.shape[1]
    cb = _pick(conv_dim, (1024, 512, 256, 128))
    assert (d_inner + col0) % cb == 0
    tr = _pick(t, (256, 128))
    off = (d_inner + col0) // cb
    woff = col0 // cb
    nr = t // tr

    def body(x_ref, h_ref, w_ref, d_ref, dn_ref, dzx_in, dx_ref, dw_ref, db_ref):
        i = pl.program_id(1)

        @pl.when(i == 0)
        def _():
            dw_ref[...] = jnp.zeros_like(dw_ref)
            db_ref[...] = jnp.zeros_like(db_ref)

        halo = jnp.where(i > 0, h_ref[...], 0.0)
        ext = jnp.concatenate([halo, x_ref[...]], axis=0)
        d = d_ref[...]
        dext = jnp.concatenate([d, jnp.where(i < nr - 1, dn_ref[...], 0.0)], axis=0)
        dx = jnp.zeros((tr, cb), F32)
        for k in range(CONV_WIDTH):
            s = CONV_WIDTH - 1 - k
            fut = dext if s == 0 else pltpu.roll(dext, shift=tr + HALO - s, axis=0)
            dx = dx + w_ref[k:k + 1, :] * fut[:tr]
            past = ext if s == 0 else pltpu.roll(ext, shift=s, axis=0)
            dw_ref[k:k + 1, :] += jnp.sum(d * past[HALO:HALO + tr], axis=0, keepdims=True)
        dx_ref[...] = dx.astype(dx_ref.dtype)
        db_ref[...] += jnp.sum(d, axis=0, keepdims=True)

    last_halo = t // HALO - 1
    return pl.pallas_call(
        body, name=f"ssm_conv_bwd_{col0}",
        out_shape=(jax.ShapeDtypeStruct(dzx.shape, dzx.dtype), jax.ShapeDtypeStruct((CONV_WIDTH, conv_dim), F32),
                   jax.ShapeDtypeStruct((1, conv_dim), F32)),
        grid=(conv_dim // cb, nr),
        in_specs=[pl.BlockSpec((tr, cb), lambda j, i: (i, off + j)),
                  pl.BlockSpec((HALO, cb), lambda j, i: (jnp.maximum(i * (tr // HALO) - 1, 0), off + j)),
                  pl.BlockSpec((CONV_WIDTH, cb), lambda j, i: (0, woff + j)),
                  pl.BlockSpec((tr, cb), lambda j, i: (i, j)),
                  pl.BlockSpec((HALO, cb), lambda j, i: (jnp.minimum((i + 1) * (tr // HALO), last_halo), j)),
                  pl.BlockSpec(memory_space=pl.ANY)],
        out_specs=(pl.BlockSpec((tr, cb), lambda j, i: (i, off + j)),
                   pl.BlockSpec((CONV_WIDTH, cb), lambda j, i: (0, j)),
                   pl.BlockSpec((1, cb), lambda j, i: (0, j))),
        input_output_aliases={5: 0},
        compiler_params=pltpu.CompilerParams(dimension_semantics=("parallel", "arbitrary")),
    )(zx, zx, conv_w, dpre, dpre, dzx)


def _eye(n):
    return lax.broadcasted_iota(jnp.int32, (n, n), 0) == lax.broadcasted_iota(jnp.int32, (n, n), 1)


def _row_to_col(row):
    n = row.shape[1]
    return jnp.sum(jnp.where(_eye(n), row, 0.0), axis=1, keepdims=True)


def _col_to_row(col):
    n = col.shape[0]
    return jnp.sum(jnp.where(_eye(n), col, 0.0), axis=0, keepdims=True)


def _pair_lanes(c0, c1):
    lane = lax.broadcasted_iota(jnp.int32, (1, LANES), 1)
    return jnp.where(lane < SSM_HEAD_DIM, c0, c1)


def _ssd_chunk_common(pre_x_ref, pre_b_ref, pre_c_ref, dtr_ref, bias_ref, alog_ref, cs_ref):
    cl = SSD_CHUNK
    hpg = dtr_ref.shape[0]
    x = _silu(pre_x_ref[...])
    b16 = _silu(pre_b_ref[...]).astype(BF16)
    c16 = _silu(pre_c_ref[...]).astype(BF16)
    dt = jax.nn.softplus(dtr_ref[...] + bias_ref[...])
    a = -jnp.exp(alog_ref[...])
    li = lax.broadcasted_iota(jnp.int32, (cl, cl), 0)
    si = lax.broadcasted_iota(jnp.int32, (cl, cl), 1)
    upper = (li <= si).astype(F32)
    cs_ref[0:hpg, :] = jnp.dot(dt * a, upper, precision=lax.Precision.HIGHEST, preferred_element_type=F32)
    cs_ref[hpg:2 * hpg, :] = dt
    g = lax.dot_general(c16, b16, (((1,), (1,)), ((), ())), preferred_element_type=F32)
    return x, b16, c16, dt, a, g, li >= si


def _ssd_fwd(pre, dtT, bias, alog, dskip_lanes, d_inner):
    t = pre.shape[0]
    cl = SSD_CHUNK
    nc = t // cl
    ng = SSM_GROUPS
    hpg = dtT.shape[1]
    gw = hpg * SSM_HEAD_DIM
    assert d_inner == ng * gw and hpg % 2 == 0
    bo = d_inner // SSM_STATE

    def body(px_ref, pb_ref, pc_ref, dtr_ref, bias_ref, alog_ref, dsk_ref, y_ref, st_ref, s_ref, cs_ref):
        c = pl.program_id(1)

        @pl.when(c == 0)
        def _():
            s_ref[...] = jnp.zeros_like(s_ref)

        x, b16, c16, dt, a, g, causal = _ssd_chunk_common(px_ref, pb_ref, pc_ref, dtr_ref, bias_ref, alog_ref, cs_ref)
        st_ref[...] = s_ref[...]
        yoff = lax.dot_general(c16, s_ref[...].astype(BF16), (((1,), (1,)), ((), ())), preferred_element_type=F32)
        xde_parts = []
        for j in range(hpg // 2):
            cols = slice(j * LANES, (j + 1) * LANES)
            xp = x[:, cols]
            dcol, ecol, ocol, ms = [], [], [], []
            for hh in range(2):
                h = 2 * j + hh
                cs_row = cs_ref[h:h + 1, :]
                cs_col = _row_to_col(cs_row)
                dcol.append(_row_to_col(cs_ref[hpg + h:hpg + h + 1, :]))
                ecol.append(jnp.exp(cs_ref[h:h + 1, cl - 1:cl] - cs_col))
                ocol.append(jnp.exp(cs_col))
                lm = jnp.where(causal, jnp.exp(jnp.minimum(cs_col - cs_row, 0.0)), 0.0)
                ms.append((g * lm).astype(BF16))
            xd = xp * _pair_lanes(dcol[0], dcol[1])
            xd16 = xd.astype(BF16)
            yd = _pair_lanes(1.0, 0.0) * jnp.dot(ms[0], xd16, preferred_element_type=F32) \
                + _pair_lanes(0.0, 1.0) * jnp.dot(ms[1], xd16, preferred_element_type=F32)
            y_ref[:, cols] = yd + yoff[:, cols] * _pair_lanes(ocol[0], ocol[1]) + xp * dsk_ref[0:1, cols]
            xde_parts.append((xd * _pair_lanes(ecol[0], ecol[1])).astype(BF16))
        new = lax.dot_general(jnp.concatenate(xde_parts, axis=1), b16, (((0,), (0,)), ((), ())),
                              preferred_element_type=F32)
        for h in range(hpg):
            rows = slice(h * SSM_HEAD_DIM, (h + 1) * SSM_HEAD_DIM)
            s_ref[rows, :] = s_ref[rows, :] * jnp.exp(cs_ref[h:h + 1, cl - 1:cl]) + new[rows, :]

    vec = lambda n: pl.BlockSpec((None, hpg, n), lambda gi, c: (gi, 0, 0))
    return pl.pallas_call(
        body, name="ssd_fwd",
        out_shape=(jax.ShapeDtypeStruct((t, d_inner), F32), jax.ShapeDtypeStruct((ng, nc, gw, SSM_STATE), F32)),
        grid=(ng, nc),
        in_specs=[pl.BlockSpec((cl, gw), lambda gi, c: (c, gi)),
                  pl.BlockSpec((cl, SSM_STATE), lambda gi, c: (c, bo + gi)),
                  pl.BlockSpec((cl, SSM_STATE), lambda gi, c: (c, bo + ng + gi)),
                  pl.BlockSpec((None, hpg, cl), lambda gi, c: (gi, 0, c)),
                  vec(1), vec(1),
                  pl.BlockSpec((1, gw), lambda gi, c: (0, gi))],
        out_specs=(pl.BlockSpec((cl, gw), lambda gi, c: (c, gi)),
                   pl.BlockSpec((None, None, gw, SSM_STATE), lambda gi, c: (gi, c, 0, 0))),
        scratch_shapes=[pltpu.VMEM((gw, SSM_STATE), F32), pltpu.VMEM((2 * hpg, cl), F32)],
        compiler_params=pltpu.CompilerParams(dimension_semantics=("parallel", "arbitrary")),
    )(pre, pre, pre, dtT, bias, alog, dskip_lanes)


def _ssd_bwd(pre, dtT, bias, alog, dskip_lanes, states, dy, d_inner):
    t, conv_dim = pre.shape
    cl = SSD_CHUNK
    nc = t // cl
    ng = SSM_GROUPS
    hpg = dtT.shape[1]
    gw = hpg * SSM_HEAD_DIM
    bo = d_inner // SSM_STATE

    def body(px_ref, pb_ref, pc_ref, dtr_ref, bias_ref, alog_ref, dsk_ref, st_ref, dy_ref,
             dx_ref, db_ref, dc_ref, ddt_ref, acc_ref, dsk_out, ds_ref, cs_ref, dcs_ref):
        c = pl.program_id(1)

        @pl.when(c == 0)
        def _():
            ds_ref[...] = jnp.zeros_like(ds_ref)
            acc_ref[...] = jnp.zeros_like(acc_ref)
            dsk_out[...] = jnp.zeros_like(dsk_out)

        x, b16, c16, dt, a, g, causal = _ssd_chunk_common(px_ref, pb_ref, pc_ref, dtr_ref, bias_ref, alog_ref, cs_ref)
        s_prev = st_ref[...]
        s16 = s_prev.astype(BF16)
        ds = ds_ref[...]
        ds16 = ds.astype(BF16)
        dyv = dy_ref[...]
        yoff = lax.dot_general(c16, s16, (((1,), (1,)), ((), ())), preferred_element_type=F32)
        bds = lax.dot_general(b16, ds16, (((1,), (1,)), ((), ())), preferred_element_type=F32)
        dg = jnp.zeros((cl, cl), F32)
        xde_parts, dye_parts = [], []
        lane = lax.broadcasted_iota(jnp.int32, (1, LANES), 1)
        for j in range(hpg // 2):
            cols = slice(j * LANES, (j + 1) * LANES)
            xp, dyp = x[:, cols], dyv[:, cols]
            dcol, ecol, ocol, lms = [], [], [], []
            for hh in range(2):
                h = 2 * j + hh
                cs_row = cs_ref[h:h + 1, :]
                cs_col = _row_to_col(cs_row)
                dcol.append(_row_to_col(cs_ref[hpg + h:hpg + h + 1, :]))
                ecol.append(jnp.exp(cs_ref[h:h + 1, cl - 1:cl] - cs_col))
                ocol.append(jnp.exp(cs_col))
                lms.append(jnp.where(causal, jnp.exp(jnp.minimum(cs_col - cs_row, 0.0)), 0.0))
            dlanes, elanes, olanes = _pair_lanes(*dcol), _pair_lanes(*ecol), _pair_lanes(*ocol)
            xd = xp * dlanes
            xd16 = xd.astype(BF16)
            xde = xd * elanes
            yoffp = yoff[:, cols] * olanes
            bdsp = bds[:, cols]
            dxd = bdsp * elanes
            for hh in range(2):
                h = 2 * j + hh
                hmask = (lane // SSM_HEAD_DIM) == hh
                dyh16 = jnp.where(hmask, dyp, 0.0).astype(BF16)
                m = g * lms[hh]
                dm = lax.dot_general(dyh16, xd16, (((1,), (1,)), ((), ())), preferred_element_type=F32)
                w = dm * m
                dg = dg + dm * lms[hh]
                dxd = dxd + lax.dot_general(m.astype(BF16), dyh16, (((0,), (0,)), ((), ())),
                                            preferred_element_type=F32)
                term = jnp.sum(jnp.where(hmask, xde * bdsp, 0.0), axis=1, keepdims=True)
                dcs_col = (jnp.sum(w, axis=1, keepdims=True)
                           + jnp.sum(jnp.where(hmask, dyp * yoffp, 0.0), axis=1, keepdims=True) - term)
                rows = slice(h * SSM_HEAD_DIM, (h + 1) * SSM_HEAD_DIM)
                dec = jnp.exp(cs_ref[h:h + 1, cl - 1:cl])
                tail = jnp.sum(term, axis=0, keepdims=True) + dec * jnp.sum(
                    jnp.sum(s_prev[rows, :] * ds[rows, :], axis=1, keepdims=True), axis=0, keepdims=True)
                last = lax.broadcasted_iota(jnp.int32, (1, cl), 1) == cl - 1
                dcs_ref[h:h + 1, :] = _col_to_row(dcs_col) - jnp.sum(w, axis=0, keepdims=True) + jnp.where(last, tail, 0.0)
                dcs_ref[hpg + h:hpg + h + 1, :] = _col_to_row(
                    jnp.sum(jnp.where(hmask, dxd * xp, 0.0), axis=1, keepdims=True))
            dx_act = dxd * dlanes + dyp * dsk_ref[0:1, cols]
            dx_ref[:, cols] = dx_act * _silu_grad(px_ref[:, cols])
            dsk_out[0:1, cols] += jnp.sum(dyp * xp, axis=0, keepdims=True)
            xde_parts.append(xde.astype(BF16))
            dye_parts.append((dyp * olanes).astype(BF16))
        xde16 = jnp.concatenate(xde_parts, axis=1)
        dye16 = jnp.concatenate(dye_parts, axis=1)
        dg16 = dg.astype(BF16)
        dc_act = jnp.dot(dg16, b16, preferred_element_type=F32) + jnp.dot(dye16, s16, preferred_element_type=F32)
        db_act = lax.dot_general(dg16, c16, (((0,), (0,)), ((), ())), preferred_element_type=F32) \
            + jnp.dot(xde16, ds16, preferred_element_type=F32)
        dc_ref[...] = dc_act * _silu_grad(pc_ref[...])
        db_ref[...] = db_act * _silu_grad(pb_ref[...])
        ds_new = lax.dot_general(dye16, c16, (((0,), (0,)), ((), ())), preferred_element_type=F32)
        for h in range(hpg):
            rows = slice(h * SSM_HEAD_DIM, (h + 1) * SSM_HEAD_DIM)
            ds_ref[rows, :] = ds[rows, :] * jnp.exp(cs_ref[h:h + 1, cl - 1:cl]) + ds_new[rows, :]
        li = lax.broadcasted_iota(jnp.int32, (cl, cl), 0)
        si = lax.broadcasted_iota(jnp.int32, (cl, cl), 1)
        d_adt = jnp.dot(dcs_ref[0:hpg, :], (li >= si).astype(F32), precision=lax.Precision.HIGHEST,
                        preferred_element_type=F32)
        ddt = d_adt * a + dcs_ref[hpg:2 * hpg, :]
        ddt_raw = ddt * jax.nn.sigmoid(dtr_ref[...] + bias_ref[...])
        ddt_ref[...] = ddt_raw
        acc_ref[0:hpg, :] += d_adt * dt
        acc_ref[hpg:2 * hpg, :] += ddt_raw

    rc = lambda c: nc - 1 - c
    vec = lambda n: pl.BlockSpec((None, hpg, n), lambda gi, c: (gi, 0, 0))
    x_spec = pl.BlockSpec((cl, gw), lambda gi, c: (rc(c), gi))
    b_spec = pl.BlockSpec((cl, SSM_STATE), lambda gi, c: (rc(c), bo + gi))
    c_spec = pl.BlockSpec((cl, SSM_STATE), lambda gi, c: (rc(c), bo + ng + gi))
    dt_spec = pl.BlockSpec((None, hpg, cl), lambda gi, c: (gi, 0, rc(c)))
    return pl.pallas_call(
        body, name="ssd_bwd",
        out_shape=(jax.ShapeDtypeStruct((t, d_inner), F32), jax.ShapeDtypeStruct((t, ng * SSM_STATE), F32),
                   jax.ShapeDtypeStruct((t, ng * SSM_STATE), F32), jax.ShapeDtypeStruct(dtT.shape, F32),
                   jax.ShapeDtypeStruct((ng, 2 * hpg, cl), F32), jax.ShapeDtypeStruct((1, d_inner), F32)),
        grid=(ng, nc),
        in_specs=[x_spec, b_spec, c_spec, dt_spec, vec(1), vec(1),
                  pl.BlockSpec((1, gw), lambda gi, c: (0, gi)),
                  pl.BlockSpec((None, None, gw, SSM_STATE), lambda gi, c: (gi, rc(c), 0, 0)),
                  x_spec],
        out_specs=(x_spec, pl.BlockSpec((cl, SSM_STATE), lambda gi, c: (rc(c), gi)),
                   pl.BlockSpec((cl, SSM_STATE), lambda gi, c: (rc(c), gi)), dt_spec,
                   pl.BlockSpec((None, 2 * hpg, cl), lambda gi, c: (gi, 0, 0)),
                   pl.BlockSpec((1, gw), lambda gi, c: (0, gi))),
        scratch_shapes=[pltpu.VMEM((gw, SSM_STATE), F32), pltpu.VMEM((2 * hpg, cl), F32),
                        pltpu.VMEM((2 * hpg, cl), F32)],
        compiler_params=pltpu.CompilerParams(dimension_semantics=("parallel", "arbitrary")),
    )(pre, pre, pre, dtT, bias, alog, dskip_lanes, states, dy)


def _gate_norm_fwd(y, zx, norm_w, d_inner):
    t = y.shape[0]
    tr = _pick(t, (256, 128))
    gs = d_inner // SSM_GROUPS

    def body(y_ref, z_ref, w_ref, o_ref):
        for gi in range(SSM_GROUPS):
            cols = slice(gi * gs, (gi + 1) * gs)
            v = y_ref[:, cols] * _silu(z_ref[:, cols])
            r = lax.rsqrt(jnp.mean(v * v, axis=-1, keepdims=True) + NORM_EPS)
            o_ref[:, cols] = (v * r * w_ref[0:1, cols]).astype(BF16)

    spec = pl.BlockSpec((tr, d_inner), lambda i: (i, 0))
    return pl.pallas_call(
        body, name="ssm_gate_norm_fwd", out_shape=jax.ShapeDtypeStruct((t, d_inner), BF16), grid=(t // tr,),
        in_specs=[spec, spec, pl.BlockSpec((1, d_inner), lambda i: (0, 0))], out_specs=spec,
        compiler_params=pltpu.CompilerParams(dimension_semantics=("parallel",)),
    )(y, zx, norm_w.reshape(1, -1))


def _gate_norm_bwd(y, zx, norm_w, dout, d_inner):
    t, width = zx.shape
    tr = _pick(t, (256, 128))
    gs = d_inner // SSM_GROUPS

    def body(y_ref, z_ref, w_ref, do_ref, dy_ref, dz_ref, dw_ref):
        @pl.when(pl.program_id(0) == 0)
        def _():
            dw_ref[...] = jnp.zeros_like(dw_ref)

        for gi in range(SSM_GROUPS):
            cols = slice(gi * gs, (gi + 1) * gs)
            yv, zv = y_ref[:, cols], z_ref[:, cols]
            sz = _silu(zv)
            v = yv * sz
            r = lax.rsqrt(jnp.mean(v * v, axis=-1, keepdims=True) + NORM_EPS)
            vhat = v * r
            dn = do_ref[:, cols].astype(F32)
            dw_ref[0:1, cols] += jnp.sum(dn * vhat, axis=0, keepdims=True)
            dvh = dn * w_ref[0:1, cols]
            dv = r * (dvh - vhat * jnp.mean(dvh * vhat, axis=-1, keepdims=True))
            dy_ref[:, cols] = dv * sz
            dz_ref[:, cols] = (dv * yv * _silu_grad(zv)).astype(dz_ref.dtype)

    spec = pl.BlockSpec((tr, d_inner), lambda i: (i, 0))
    wspec = pl.BlockSpec((1, d_inner), lambda i: (0, 0))
    return pl.pallas_call(
        body, name="ssm_gate_norm_bwd",
        out_shape=(jax.ShapeDtypeStruct((t, d_inner), F32), jax.ShapeDtypeStruct((t, width), zx.dtype),
                   jax.ShapeDtypeStruct((1, d_inner), F32)),
        grid=(t // tr,),
        in_specs=[spec, spec, wspec, spec], out_specs=(spec, spec, wspec),
        compiler_params=pltpu.CompilerParams(dimension_semantics=("arbitrary",)),
    )(y, zx, norm_w.reshape(1, -1), dout)


def _ssm_small(dt_raw, dt_bias, a_log, d_skip):
    heads = dt_raw.shape[1]
    hpg = heads // SSM_GROUPS
    dtT = dt_raw.T.reshape(SSM_GROUPS, hpg, -1)
    return (dtT, dt_bias.reshape(SSM_GROUPS, hpg, 1), a_log.reshape(SSM_GROUPS, hpg, 1),
            jnp.repeat(d_skip, SSM_HEAD_DIM).reshape(1, -1))


def _ssm_core_fwd(zx, dt_raw, conv_w, conv_b, dt_bias, a_log, d_skip, norm_w):
    d_inner = norm_w.shape[0]
    pre = _conv_fwd(zx, conv_w, conv_b, d_inner)
    dtT, bias, alog, dsk = _ssm_small(dt_raw, dt_bias, a_log, d_skip)
    y, states = _ssd_fwd(pre, dtT, bias, alog, dsk, d_inner)
    out = _gate_norm_fwd(y, zx, norm_w, d_inner)
    return out, (zx, dt_raw, conv_w, dt_bias, a_log, d_skip, norm_w, pre, y, states)


def _ssm_core_bwd(res, dout):
    zx, dt_raw, conv_w, dt_bias, a_log, d_skip, norm_w, pre, y, states = res
    d_inner = norm_w.shape[0]
    heads = dt_raw.shape[1]
    dy, dzx, dnorm = _gate_norm_bwd(y, zx, norm_w, dout, d_inner)
    dtT, bias, alog, dsk = _ssm_small(dt_raw, dt_bias, a_log, d_skip)
    dx, db, dc, ddtT, acc, dsk_l = _ssd_bwd(pre, dtT, bias, alog, dsk, states, dy, d_inner)
    dws, dbs, col0 = [], [], 0
    for part in (dx, db, dc):
        dzx, dw_part, db_part = _conv_bwd(zx, conv_w, part, dzx, d_inner, col0)
        dws.append(dw_part)
        dbs.append(db_part)
        col0 += part.shape[1]
    dconv_w, dconv_b = jnp.concatenate(dws, axis=1), jnp.concatenate(dbs, axis=1)
    d_dt_raw = ddtT.reshape(heads, -1).T
    hpg = heads // SSM_GROUPS
    da = acc[:, :hpg].sum(-1).reshape(heads)
    d_bias = acc[:, hpg:].sum(-1).reshape(heads)
    d_alog = da * (-jnp.exp(a_log))
    d_dskip = dsk_l.reshape(heads, SSM_HEAD_DIM).sum(-1)
    return dzx, d_dt_raw, dconv_w, dconv_b.reshape(-1), d_bias, d_alog, d_dskip, dnorm.reshape(-1)


def _rows_call(body, name, ins, outs, acc_outs=(), rows=256):
    t = max(a.shape[0] for a in ins)
    tr = _pick(t, (rows, 128, 64, 32, 16, 8))

    def spec(a):
        if a.shape[0] == t:
            return pl.BlockSpec((tr, a.shape[1]), lambda i: (i, 0))
        return pl.BlockSpec(a.shape, lambda i: (0, 0))

    return pl.pallas_call(
        body, name=name, out_shape=tuple(outs) + tuple(acc_outs), grid=(t // tr,),
        in_specs=[spec(a) for a in ins],
        out_specs=tuple(spec(a) for a in outs) + tuple(pl.BlockSpec(a.shape, lambda i: (0, 0)) for a in acc_outs),
        compiler_params=pltpu.CompilerParams(dimension_semantics=("arbitrary" if acc_outs else "parallel",)),
    )(*ins)


def _rms_fwd(x, gain, after=None):
    def body(x_ref, g_ref, *rest):
        v = x_ref[...]
        rest[-1][...] = (v * lax.rsqrt(jnp.mean(v * v, axis=-1, keepdims=True) + NORM_EPS) * g_ref[...]).astype(BF16)

    ins = [x, gain.reshape(1, -1)] + ([] if after is None else [after])
    (h,) = _rows_call(body, "rms_fwd", ins, [jax.ShapeDtypeStruct(x.shape, BF16)])
    return h


def _rms_bwd(x, gain, dh, dres, after):
    def body(x_ref, g_ref, dh_ref, dr_ref, *rest):
        dx_ref, dg_ref = rest[-2:]

        @pl.when(pl.program_id(0) == 0)
        def _():
            dg_ref[...] = jnp.zeros_like(dg_ref)

        v = x_ref[...]
        r = lax.rsqrt(jnp.mean(v * v, axis=-1, keepdims=True) + NORM_EPS)
        vhat = v * r
        d = dh_ref[...].astype(F32)
        dg_ref[...] += jnp.sum(d * vhat, axis=0, keepdims=True)
        dvh = d * g_ref[...]
        dx_ref[...] = dr_ref[...] + r * (dvh - vhat * jnp.mean(dvh * vhat, axis=-1, keepdims=True))

    ins = [x, gain.reshape(1, -1), dh, dres] + ([] if after is None else [after])
    dx, dg = _rows_call(body, "rms_bwd", ins, [jax.ShapeDtypeStruct(x.shape, F32)],
                        [jax.ShapeDtypeStruct((1, x.shape[1]), F32)])
    return dx, dg.reshape(gain.shape)


def _swiglu_fwd(gu):
    t, f2 = gu.shape
    f = f2 // 2

    def body(gu_ref, o_ref):
        o_ref[...] = (_silu(gu_ref[:, :f].astype(F32)) * gu_ref[:, f:].astype(F32)).astype(BF16)

    (act,) = _rows_call(body, "swiglu_fwd", [gu], [jax.ShapeDtypeStruct((t, f), BF16)])
    return act


def _swiglu_bwd(gu, dact):
    t, f2 = gu.shape
    f = f2 // 2

    def body(gu_ref, d_ref, o_ref):
        g, u, d = gu_ref[:, :f].astype(F32), gu_ref[:, f:].astype(F32), d_ref[...].astype(F32)
        o_ref[:, :f] = (d * u * _silu_grad(g)).astype(BF16)
        o_ref[:, f:] = (d * _silu(g)).astype(BF16)

    (dgu,) = _rows_call(body, "swiglu_bwd", [gu, dact], [jax.ShapeDtypeStruct((t, f2), BF16)])
    return dgu


def _ple_fwd(x, gl, ple):
    def body(x_ref, g_ref, p_ref, o_ref):
        o_ref[...] = x_ref[...] + jax.nn.sigmoid(g_ref[...]) * p_ref[...]

    (out,) = _rows_call(body, "ple_fwd", [x, gl, ple], [jax.ShapeDtypeStruct(x.shape, F32)])
    return out


def _ple_bwd(gl, ple, dout):
    def body(g_ref, p_ref, d_ref, dg_ref, dp_ref):
        s, d = jax.nn.sigmoid(g_ref[...]), d_ref[...]
        dg_ref[...] = (d * p_ref[...] * s * (1.0 - s)).astype(BF16)
        dp_ref[...] = (d * s).astype(BF16)

    shp = jax.ShapeDtypeStruct(gl.shape, BF16)
    return _rows_call(body, "ple_bwd", [gl, ple, dout], [shp, shp])


def _loss_fwd(y, target):
    inv = 1.0 / y.shape[1]

    def body(y_ref, t_ref, d_ref, l_ref):
        @pl.when(pl.program_id(0) == 0)
        def _():
            l_ref[...] = jnp.zeros_like(l_ref)

        e = y_ref[...] - t_ref[...]
        d_ref[...] = e * inv
        part = jnp.sum(jnp.sum(e * e, axis=1, keepdims=True), axis=0, keepdims=True) * (0.5 * inv)
        l_ref[...] += jnp.broadcast_to(part, l_ref.shape)

    dy, acc = _rows_call(body, "loss_fwd", [y, target], [jax.ShapeDtypeStruct(y.shape, F32)],
                         [jax.ShapeDtypeStruct((SUBLANES, LANES), F32)])
    return acc[0, 0], dy


def rmsnorm(x, gain):
    y = x * lax.rsqrt(jnp.mean(x * x, axis=-1, keepdims=True) + NORM_EPS)
    return y * gain


def causal_depthwise_conv(u, w, bias):
    k_width, chans = w.shape
    out = lax.conv_general_dilated(u, w[:, None, :], window_strides=(1,), padding=[(k_width - 1, 0)],
                                   dimension_numbers=("NWC", "WIO", "NWC"), feature_group_count=chans)
    return out + bias


def ssd_chunked(x, dt, a, bm, cm):
    b, t, heads, _ = x.shape
    nc, cl = t // SSD_CHUNK, SSD_CHUNK
    g, hg = SSM_GROUPS, heads // SSM_GROUPS
    xs = (x * dt[..., None]).reshape(b, nc, cl, g, hg, SSM_HEAD_DIM)
    a_dt = (dt * a).reshape(b, nc, cl, g, hg).transpose(0, 1, 3, 4, 2)
    a_cs = jnp.cumsum(a_dt, axis=-1)
    bc = bm.reshape(b, nc, cl, g, SSM_STATE)
    cc = cm.reshape(b, nc, cl, g, SSM_STATE)
    causal = jnp.tril(jnp.ones((cl, cl), dtype=bool))
    seg = a_cs[..., :, None] - a_cs[..., None, :]
    lmat = jnp.exp(jnp.where(causal, seg, -jnp.inf))
    cb = jnp.einsum("bclgn,bcsgn->bcgls", cc, bc)
    y_diag = jnp.einsum("bcgls,bcghls,bcsghp->bclghp", cb, lmat, xs)
    decay = jnp.exp(a_cs[..., -1:] - a_cs)
    states = jnp.einsum("bclgn,bcghl,bclghp->bcghpn", bc, decay, xs)
    chunk_decay = jnp.exp(a_cs[..., -1])

    def step(carry, inp):
        st, dec = inp
        return carry * dec[..., None, None] + st, carry

    init = jnp.zeros((b, g, hg, SSM_HEAD_DIM, SSM_STATE), F32)
    _, prev = lax.scan(step, init, (jnp.moveaxis(states, 1, 0), jnp.moveaxis(chunk_decay, 1, 0)))
    prev = jnp.moveaxis(prev, 0, 1)
    y_off = jnp.einsum("bclgn,bcghpn,bcghl->bclghp", cc, prev, jnp.exp(a_cs))
    return (y_diag + y_off).reshape(b, t, heads, SSM_HEAD_DIM)


def mamba2_mixer(h, wt_in, conv_w, conv_b, dt_bias, a_log, d_skip, norm_w, w_out):
    t, d_model = h.shape
    d_inner = 2 * d_model
    heads = d_inner // SSM_HEAD_DIM
    gn = SSM_GROUPS * SSM_STATE
    conv_dim = d_inner + 2 * gn
    zx = lin_t(h, wt_in[:d_inner + conv_dim])
    dt_raw = lin_t(h, wt_in[d_inner + conv_dim:])
    return lin(ssm_core(zx, dt_raw, conv_w, conv_b, dt_bias, a_log, d_skip, norm_w), w_out)


def ssm_core_jnp(zx, dt_raw, conv_w, conv_b, dt_bias, a_log, d_skip, norm_w):
    t = zx.shape[0]
    d_inner = norm_w.shape[0]
    heads = d_inner // SSM_HEAD_DIM
    gn = SSM_GROUPS * SSM_STATE
    z = zx[:, :d_inner]
    xbc = zx[:, d_inner:]
    xbc = jax.nn.silu(causal_depthwise_conv(xbc[None], conv_w, conv_b))[0]
    xs = xbc[:, :d_inner]
    bm = xbc[:, d_inner:d_inner + gn].reshape(1, t, SSM_GROUPS, SSM_STATE)
    cm = xbc[:, d_inner + gn:].reshape(1, t, SSM_GROUPS, SSM_STATE)
    dt = jax.nn.softplus(dt_raw + dt_bias)[None]
    a = -jnp.exp(a_log)
    xh = xs.reshape(1, t, heads, SSM_HEAD_DIM)
    y = ssd_chunked(xh, dt, a, bm, cm)
    y = y + xh * d_skip[:, None]
    y = y.reshape(t, d_inner) * jax.nn.silu(z)
    return rmsnorm(y.reshape(t, SSM_GROUPS, -1), norm_w.reshape(SSM_GROUPS, -1)).reshape(t, d_inner)


def alibi_slopes(n_heads):
    return 2.0 ** (-8.0 * jnp.arange(1, n_heads + 1, dtype=F32) / n_heads)


def dilated_group_attention(q, k, v, window, dilation, slopes):
    b, t, nh, e = q.shape
    span = window // dilation
    blk = span
    lu = t // dilation
    nb = -(-lu // blk)
    lp = nb * blk

    def to_blocks(arr):
        arr = arr.reshape(b, lu, dilation, nh, e)
        arr = jnp.pad(arr, ((0, 0), (0, lp - lu), (0, 0), (0, 0), (0, 0)))
        return arr.reshape(b, nb, blk, dilation, nh, e)

    qb, kb, vb = to_blocks(q), to_blocks(k), to_blocks(v)
    pad_prev = ((0, 0), (1, 0), (0, 0), (0, 0), (0, 0), (0, 0))
    kcat = jnp.concatenate([jnp.pad(kb, pad_prev)[:, :nb], kb], axis=2)
    vcat = jnp.concatenate([jnp.pad(vb, pad_prev)[:, :nb], vb], axis=2)
    scores = jnp.einsum("bnqrhe,bnkrhe->bnrhqk", qb, kcat) * (1.0 / math.sqrt(e))
    qi = jnp.arange(blk)[:, None]
    ki = jnp.arange(2 * blk)[None, :]
    dist = qi + blk - ki
    in_band = (dist >= 0) & (dist <= span)
    key_u = jnp.arange(nb)[:, None] * blk - blk + jnp.arange(2 * blk)[None, :]
    valid = in_band[None] & (key_u >= 0)[:, None, :]
    bias = -slopes[:, None, None] * (dilation * dist).astype(F32)[None]
    logits = jnp.where(valid[None, :, None, None], scores + bias[None, None, None], -jnp.inf)
    lse = jax.nn.logsumexp(logits, axis=-1)
    probs = jnp.exp(logits - lse[..., None])
    out = jnp.einsum("bnrhqk,bnkrhe->bnqrhe", probs, vcat)
    out = out.reshape(b, lp, dilation, nh, e)[:, :lu].reshape(b, t, nh, e)
    lse = lse.transpose(0, 1, 4, 2, 3).reshape(b, lp, dilation, nh)[:, :lu].reshape(b, t, nh)
    return out, lse


def dilated_attention_mixer(h, wt_qkv, q_gain, k_gain, w_o):
    t, d_model = h.shape
    heads = d_model // ATT_HEAD_DIM
    ng = len(DIL_PATTERNS)
    return lin(attention_core(lin_t(h, wt_qkv, BF16), q_gain, k_gain), w_o)


def attention_core_jnp(qkv, q_gain, k_gain):
    t = qkv.shape[0]
    ng = len(DIL_PATTERNS)
    heads = qkv.shape[1] // (3 * ng * ATT_HEAD_DIM)
    qkv = qkv.astype(F32).reshape(1, t, ng, 3, heads, ATT_HEAD_DIM)
    q = rmsnorm(qkv[:, :, :, 0], q_gain)
    k = rmsnorm(qkv[:, :, :, 1], k_gain)
    v = qkv[:, :, :, 2]
    slopes = alibi_slopes(heads)
    outs, lses = [], []
    for g, (window, dilation) in enumerate(DIL_PATTERNS):
        o_g, l_g = dilated_group_attention(q[:, :, g], k[:, :, g], v[:, :, g], window, dilation, slopes)
        outs.append(o_g)
        lses.append(l_g)
    alpha = jax.nn.softmax(jnp.stack(lses), axis=0)
    o = jnp.einsum("gbth,gbthe->bthe", alpha, jnp.stack(outs))
    return o.reshape(t, heads * ATT_HEAD_DIM)


def local_step(small, fetch, emit, x, p, target):
    depth = small['norm_mix'].shape[0]
    ssm_small = ('ssm_conv_w', 'ssm_conv_b', 'ssm_dt_bias', 'ssm_a_log', 'ssm_d_skip', 'ssm_norm_w')
    saved = []
    for i in range(depth):
        j = i // 2
        s = {'x': x}
        wm, token = fetch(2 * i, x)
        s['wm'] = wm
        h = s['h'] = _rms_fwd(x, small['norm_mix'][i], token)
        if i % 2 == 0:
            n_main = wm['ssm_w_in'].shape[0] - small['ssm_dt_bias'].shape[1]
            zx = _mm(h, wm['ssm_w_in'][:n_main], tb=True, name="ssm_in_fwd")
            dt_raw = _mm(h, wm['ssm_w_in'][n_main:], tb=True, name="ssm_dt_fwd")
            y, s['mix'] = _ssm_core_fwd(zx, dt_raw, wm['ssm_conv_w'], *[small[n][j] for n in ssm_small[1:]])
            x = _mm(y, wm['ssm_w_out'], add=x, name="ssm_out_fwd")
        else:
            qkv = _mm(h, wm['att_w_qkv'], tb=True, out_dtype=BF16, name="att_qkv_fwd")
            y, s['mix'] = _attention_core_fwd(qkv, small['att_q_norm'][j], small['att_k_norm'][j])
            x = _mm(y, wm['att_w_o'], add=x, name="att_o_fwd")
        s['y'], s['x1'] = y, x
        wf, token = fetch(2 * i + 1, x)
        s['wf'] = wf
        h2 = s['h2'] = _rms_fwd(x, small['norm_ffn'][i], token)
        gu = s['gu'] = _mm(h2, wf['ffn_w_gu'], tb=True, out_dtype=BF16, name="ffn_gu_fwd")
        act = s['act'] = _swiglu_fwd(gu)
        x = s['x2'] = _mm(act, wf['ffn_w_down'], add=x, name="ffn_down_fwd")
        gl = s['gl'] = _mm(x, wf['ple_w_gate'], name="ple_gate_fwd")
        ple = s['ple'] = _mm(p[i], wf['ple_w_proj'], tb=True, name="ple_proj_fwd")
        x = _ple_fwd(x, gl, ple)
        saved.append(s)
    loss, dx = _loss_fwd(x, target)

    g = {n: [None] * small[n].shape[0] for n in small}
    for i in reversed(range(depth)):
        j = i // 2
        s = saved[i]
        wm, wf = s['wm'], s['wf']
        gf = {}
        dgl, dple = _ple_bwd(s['gl'], s['ple'], dx)
        gf['ple_w_proj'] = _mm(dple, p[i], ta=True, out_dtype=BF16, name="ple_proj_dw")
        gf['ple_w_gate'] = _mm(s['x2'], dgl, ta=True, out_dtype=BF16, name="ple_gate_dw")
        dx = _mm(dgl, wf['ple_w_gate'], tb=True, add=dx, name="ple_gate_da")
        dact = _mm(dx, wf['ffn_w_down'], tb=True, out_dtype=BF16, name="ffn_down_da")
        gf['ffn_w_down'] = _mm(s['act'], dx, ta=True, out_dtype=BF16, name="ffn_down_dw")
        dgu = _swiglu_bwd(s['gu'], dact)
        dh2 = _mm(dgu, wf['ffn_w_gu'], out_dtype=BF16, name="ffn_gu_da")
        gf['ffn_w_gu'] = _mm(dgu, s['h2'], ta=True, out_dtype=BF16, name="ffn_gu_dw")
        dx, g['norm_ffn'][i] = _rms_bwd(s['x1'], small['norm_ffn'][i], dh2, dx, emit(2 * i + 1, gf))
        gm = {}
        if i % 2 == 0:
            n_main = wm['ssm_w_in'].shape[0] - small['ssm_dt_bias'].shape[1]
            dyn = _mm(dx, wm['ssm_w_out'], tb=True, out_dtype=BF16, name="ssm_out_da")
            gm['ssm_w_out'] = _mm(s['y'], dx, ta=True, out_dtype=BF16, name="ssm_out_dw")
            dzx, d_dt, *sg = _ssm_core_bwd(s['mix'], dyn)
            for n, v in zip(ssm_small, sg):
                g[n][j] = v
            dh = _mm(d_dt, wm['ssm_w_in'][n_main:], name="ssm_dt_da")
            dh = _mm(dzx, wm['ssm_w_in'][:n_main], add=dh, out_dtype=BF16, name="ssm_in_da")
            gm['ssm_w_in'] = jnp.concatenate([_mm(dzx, s['h'], ta=True, out_dtype=BF16, name="ssm_in_dw"),
                                              _mm(d_dt, s['h'], ta=True, out_dtype=BF16, name="ssm_dt_dw")], axis=0)
        else:
            do = _mm(dx, wm['att_w_o'], tb=True, name="att_o_da")
            gm['att_w_o'] = _mm(s['y'], dx, ta=True, out_dtype=BF16, name="att_o_dw")
            dqkv, g['att_q_norm'][j], g['att_k_norm'][j] = _attention_core_bwd(s['mix'], do)
            dh = _mm(dqkv, wm['att_w_qkv'], out_dtype=BF16, name="att_qkv_da")
            gm['att_w_qkv'] = _mm(dqkv, s['h'], ta=True, out_dtype=BF16, name="att_qkv_dw")
        dx, g['norm_mix'][i] = _rms_bwd(s['x'], small['norm_mix'][i], dh, dx, emit(2 * i, gm))
    return loss, dx, {n: jnp.stack(v) for n, v in g.items()}


MIXER_WEIGHTS = (('ssm_w_in', 'ssm_w_out', 'ssm_conv_w'), ('att_w_qkv', 'att_w_o'))
CHANNEL_WEIGHTS = ('ffn_w_gate', 'ffn_w_up', 'ffn_w_down', 'ple_w_proj', 'ple_w_gate')


def _pack_plan(shapes, width, stage):
    layer = stage // 2
    if stage % 2:
        members = [(n, layer) for n in CHANNEL_WEIGHTS]
    else:
        members = [(n, layer // 2) for n in MIXER_WEIGHTS[layer % 2]]
    plan, off = [], 0
    for name, lyr in members:
        _, r, c = shapes[name]
        if name in COL_SHARDED:
            r, c = c, r
        if name == 'ssm_conv_w':
            pr = -(-2 * r * c // width)
        else:
            assert (r * c) % width == 0, (name, r, c)
            pr = r * c // width
        plan.append((name, lyr, r, c, pr, off))
        off += _round_up(pr, BF16_ROWS)
    return plan, off


def _small_plan(shapes):
    plan, off = [], 0
    for name in SMALL:
        n = math.prod(shapes[name])
        plan.append((name, n, off))
        off += n
    return plan, _round_up(off, SUBLANES * LANES)


def kernel(x, p, norm_mix, norm_ffn, ssm_w_in, ssm_conv_w, ssm_conv_b, ssm_dt_bias, ssm_a_log, ssm_d_skip, ssm_norm_w, ssm_w_out, att_w_qkv, att_q_norm, att_k_norm, att_w_o, ffn_w_gate, ffn_w_up, ffn_w_down, ple_w_proj, ple_w_gate, loss_target, m_norm_mix, m_norm_ffn, m_ssm_w_in, m_ssm_conv_w, m_ssm_conv_b, m_ssm_dt_bias, m_ssm_a_log, m_ssm_d_skip, m_ssm_norm_w, m_ssm_w_out, m_att_w_qkv, m_att_q_norm, m_att_k_norm, m_att_w_o, m_ffn_w_gate, m_ffn_w_up, m_ffn_w_down, m_ple_w_proj, m_ple_w_gate, v_norm_mix, v_norm_ffn, v_ssm_w_in, v_ssm_conv_w, v_ssm_conv_b, v_ssm_dt_bias, v_ssm_a_log, v_ssm_d_skip, v_ssm_norm_w, v_ssm_w_out, v_att_w_qkv, v_att_q_norm, v_att_k_norm, v_att_w_o, v_ffn_w_gate, v_ffn_w_up, v_ffn_w_down, v_ple_w_proj, v_ple_w_gate):
    given = dict(locals())
    w_in = {n: given[n] for n in WEIGHTS}
    m_in = {n: given["m_" + n] for n in WEIGHTS}
    v_in = {n: given["v_" + n] for n in WEIGHTS}
    width = x.shape[-1]
    depth = norm_mix.shape[0]
    n_stages = 2 * depth

    plans = [_pack_plan({n: w_in[n].shape for n in BIG + ('ssm_conv_w',)}, width, stage) for stage in range(n_stages)]

    def pack_weights(stage):
        pieces = []
        for name, layer, r, c, pr, off in plans[stage][0]:
            blk = w_in[name][layer]
            if name == 'ssm_conv_w':
                blk = lax.bitcast_convert_type(blk.reshape(-1), BF16).reshape(-1)
                blk = jnp.pad(blk, (0, pr * width - blk.shape[0]))
            elif name in COL_SHARDED:
                blk = blk.T
            blk = blk.astype(BF16).reshape(pr, width)
            pieces.append(jnp.pad(blk, ((0, _round_up(pr, BF16_ROWS) - pr), (0, 0))))
        return jnp.concatenate(pieces, axis=0)

    packed = [pack_weights(0)]
    pending = [exchange_start(packed[0], True, "gather_start_0")]
    packed += [pack_weights(stage) for stage in range(1, n_stages)]

    def fetch(stage, after):
        handle, token = pending[stage]
        land = exchange_wait(handle, [token] + packed[1:] if stage == 0 else after, True, f"gather_wait_{stage}")
        token = None
        if stage + 1 < n_stages:
            pending.append(exchange_start(packed[stage + 1], True, f"gather_start_{stage + 1}", land))
            token = pending[-1][1]
        got = {}
        for name, layer, r, c, pr, off in plans[stage][0]:
            piece = land[:, off:off + pr]
            if name == 'ssm_conv_w':
                taps, chans = w_in[name].shape[1:]
                bits = piece.reshape(N_DEV, -1)[:, :2 * taps * chans].reshape(N_DEV, taps * chans, 2)
                piece = lax.bitcast_convert_type(bits, F32).reshape(N_DEV, taps, chans)
                got[name] = piece.transpose(1, 0, 2).reshape(taps, N_DEV * chans)
            else:
                got[name] = piece.reshape(N_DEV * r, c)
        if 'ffn_w_gate' in got:
            got['ffn_w_gu'] = jnp.concatenate([got.pop('ffn_w_gate'), got.pop('ffn_w_up')], axis=0)
        return got, token

    scatters = [None] * n_stages

    def emit(stage, grads):
        grads = dict(grads)
        if 'ffn_w_gu' in grads:
            hidden = grads['ffn_w_gu'].shape[0] // 2
            grads['ffn_w_gate'], grads['ffn_w_up'] = grads['ffn_w_gu'][:hidden], grads['ffn_w_gu'][hidden:]
        pieces = []
        for name, layer, r, c, pr, off in plans[stage][0]:
            if name == 'ssm_conv_w':
                continue
            g = grads[name].reshape(N_DEV, pr, width)
            pieces.append(jnp.pad(g, ((0, 0), (0, _round_up(pr, BF16_ROWS) - pr), (0, 0))))
        scatters[stage], token = exchange_start(jnp.concatenate(pieces, axis=1), False, f"scatter_start_{stage}")
        return token

    small = {n: w_in[n] for n in SMALL}
    cs = w_in['ssm_conv_w'].shape[2]
    loss_local, gx, gw = local_step(small, fetch, emit, x[0], p[:, 0], loss_target[0])
    loss = lax.psum(loss_local, ("x", "y", "c"))

    parts = {}
    for stage in reversed(range(n_stages)):
        received = exchange_wait(scatters[stage], gx, False, f"scatter_wait_{stage}")
        gsum = sum_slots(received, f"sum_grads_{stage}")
        for name, layer, r, c, pr, off in plans[stage][0]:
            if name == 'ssm_conv_w':
                continue
            g = gsum[off:off + pr].reshape(r, c)
            parts[name, layer] = g.T if name in COL_SHARDED else g
    grads = {n: jnp.stack([parts[n, layer] for layer in range(w_in[n].shape[0])]) for n in BIG}

    splan, stotal = _small_plan({n: gw[n].shape for n in SMALL})
    svec = jnp.concatenate([gw[n].reshape(-1) for n, _, _ in splan])
    svec = jnp.pad(svec, (0, stotal - svec.shape[0])).reshape(stotal // LANES, LANES)
    _, ssum = all_gather_sum_small(svec, "sum_small_grads")
    ssum = ssum.reshape(-1)
    for name, n, off in splan:
        grads[name] = ssum[off:off + n].reshape(gw[name].shape)
    me = _me()
    grads['ssm_conv_w'] = lax.dynamic_slice_in_dim(grads['ssm_conv_w'], me * cs, cs, axis=2)

    delta, new_m, new_v = {}, {}, {}
    for name in BIG:
        shp = w_in[name].shape
        flat = lambda a: a.reshape(-1, shp[-1])
        d, nm, nv = adamw(flat(w_in[name]), flat(grads[name]), flat(m_in[name]), flat(v_in[name]), "adamw_" + name)
        delta[name], new_m[name], new_v[name] = d.reshape(shp), nm.reshape(shp), nv.reshape(shp)
    splan2, stotal2 = _small_plan({n: w_in[n].shape for n in SMALL})

    def pack_small(src):
        vec = jnp.concatenate([src[n].reshape(-1) for n, _, _ in splan2])
        return jnp.pad(vec, (0, stotal2 - vec.shape[0]), constant_values=1.0).reshape(stotal2 // LANES, LANES)

    sd, snm, snv = adamw(pack_small(w_in), pack_small(grads), pack_small(m_in), pack_small(v_in), "adamw_small")
    for name, n, off in splan2:
        shp = w_in[name].shape
        delta[name] = sd.reshape(-1)[off:off + n].reshape(shp)
        new_m[name] = snm.reshape(-1)[off:off + n].reshape(shp)
        new_v[name] = snv.reshape(-1)[off:off + n].reshape(shp)

    return (loss, gx[None], *[grads[n] for n in WEIGHTS], *[delta[n] for n in WEIGHTS],
            *[new_m[n] for n in WEIGHTS], *[new_v[n] for n in WEIGHTS])
```

```python
import functools
import math

import jax
import jax.numpy as jnp
from jax import lax
from jax.experimental import pallas as pl
from jax.experimental.pallas import tpu as pltpu

F32 = jnp.float32
BF16 = jnp.bfloat16
N_DEV = 8
MESH = pl.DeviceIdType.MESH

SSM_HEAD_DIM = 64
SSM_GROUPS = 4
SSM_STATE = 128
CONV_WIDTH = 4
SSD_CHUNK = 128
ATT_HEAD_DIM = 64
DIL_PATTERNS = ((128, 1), (512, 4), (2048, 16))
NORM_EPS = 1e-6
ADAM_LR = 0.001
ADAM_B1 = 0.9
ADAM_B2 = 0.999
ADAM_EPS = 1e-08
ADAM_WD = 0.01
ADAM_STEP = 10

BF16_ROWS = 16
LANES = 128
SUBLANES = 8

WEIGHTS = ['norm_mix', 'norm_ffn', 'ssm_w_in', 'ssm_conv_w', 'ssm_conv_b', 'ssm_dt_bias', 'ssm_a_log', 'ssm_d_skip',
           'ssm_norm_w', 'ssm_w_out', 'att_w_qkv', 'att_q_norm', 'att_k_norm', 'att_w_o', 'ffn_w_gate', 'ffn_w_up',
           'ffn_w_down', 'ple_w_proj', 'ple_w_gate']
COL_SHARDED = ('ssm_w_in', 'att_w_qkv', 'ffn_w_gate', 'ffn_w_up', 'ple_w_proj')
ROW_SHARDED = ('ssm_w_out', 'att_w_o', 'ffn_w_down', 'ple_w_gate')
BIG = COL_SHARDED + ROW_SHARDED
SMALL = ('norm_mix', 'norm_ffn', 'ssm_conv_w', 'ssm_conv_b', 'ssm_dt_bias', 'ssm_a_log', 'ssm_d_skip', 'ssm_norm_w',
         'att_q_norm', 'att_k_norm')


def _pick(n, cands):
    for c in cands:
        if n % c == 0:
            return c
    return n


def _round_up(n, m):
    return -(-n // m) * m


MM_TILES = (1024, 1408, 512, 256, 128)
MM_VMEM_BYTES = 48 * 1024 * 1024


def _mm(a, b, *, ta=False, tb=False, out_dtype=F32, add=None, after=None, name):
    k_dim, m_dim = (a.shape if ta else a.shape[::-1])
    n_dim = b.shape[0] if tb else b.shape[1]
    assert (b.shape[1] if tb else b.shape[0]) == k_dim, (a.shape, b.shape, ta, tb)
    tm = _pick(m_dim, MM_TILES)
    tn = _pick(n_dim, MM_TILES)
    tk = _pick(k_dim, MM_TILES)
    nk = k_dim // tk
    a_spec = pl.BlockSpec((tk, tm), lambda i, j, k: (k, i)) if ta else pl.BlockSpec((tm, tk), lambda i, j, k: (i, k))
    b_spec = pl.BlockSpec((tn, tk), lambda i, j, k: (j, k)) if tb else pl.BlockSpec((tk, tn), lambda i, j, k: (k, j))
    o_spec = pl.BlockSpec((tm, tn), lambda i, j, k: (i, j))
    dims = (((0 if ta else 1,), (1 if tb else 0,)), ((), ()))
    has_add = add is not None
    n_in = 2 + has_add + (after is not None)

    def body(*refs):
        a_ref, b_ref = refs[:2]
        o_ref = refs[n_in]

        def dot():
            return lax.dot_general(a_ref[...].astype(BF16), b_ref[...].astype(BF16), dims,
                                   preferred_element_type=F32)

        def finish(acc):
            if has_add:
                acc = acc + refs[2][...].astype(F32)
            o_ref[...] = acc.astype(o_ref.dtype)

        if nk == 1:
            finish(dot())
            return
        acc_ref = refs[n_in + 1]
        k = pl.program_id(2)

        @pl.when(k == 0)
        def _():
            acc_ref[...] = dot()

        @pl.when((k > 0) & (k < nk - 1))
        def _():
            acc_ref[...] += dot()

        @pl.when(k == nk - 1)
        def _():
            finish(acc_ref[...] + dot())

    return pl.pallas_call(
        body, name=f"{name}_{m_dim}x{n_dim}x{k_dim}",
        out_shape=jax.ShapeDtypeStruct((m_dim, n_dim), out_dtype),
        grid=(m_dim // tm, n_dim // tn, nk),
        in_specs=[a_spec, b_spec] + ([o_spec] if has_add else []) + (
            [] if after is None else [pl.BlockSpec(memory_space=pl.ANY)]),
        out_specs=o_spec,
        scratch_shapes=[] if nk == 1 else [pltpu.VMEM((tm, tn), F32)],
        compiler_params=pltpu.CompilerParams(dimension_semantics=("parallel", "parallel", "arbitrary"),
                                             vmem_limit_bytes=MM_VMEM_BYTES),
    )(*((a, b) + ((add,) if has_add else ()) + (() if after is None else (after,))))


def _me():
    return 4 * lax.axis_index("x") + 2 * lax.axis_index("y") + lax.axis_index("c")


def _peer(j):
    x, y, c = lax.axis_index("x"), lax.axis_index("y"), lax.axis_index("c")
    px = 1 - x if j & 4 else x
    py = 1 - y if j & 2 else y
    pc = 1 - c if j & 1 else c
    return (px, py, pc), 4 * px + 2 * py + pc


def _exchange_body(src_of, dst_ref, send_sems, recv_sems, local_sem):
    me = _me()
    mine = pltpu.make_async_copy(src_of(me), dst_ref.at[me], local_sem)
    mine.start()
    sends = []
    for j in range(1, N_DEV):
        peer, pidx = _peer(j)
        cp = pltpu.make_async_remote_copy(src_ref=src_of(pidx), dst_ref=dst_ref.at[me], send_sem=send_sems.at[j - 1],
                                          recv_sem=recv_sems.at[j - 1], device_id=peer, device_id_type=MESH)
        cp.start()
        sends.append(cp)
    for j in range(1, N_DEV):
        peer, pidx = _peer(j)
        pltpu.make_async_remote_copy(src_ref=src_of(pidx), dst_ref=dst_ref.at[pidx], send_sem=send_sems.at[j - 1],
                                     recv_sem=recv_sems.at[j - 1], device_id=peer, device_id_type=MESH).wait_recv()
    for cp in sends:
        cp.wait_send()
    mine.wait()


_EXCHANGE_SCRATCH = [pltpu.SemaphoreType.DMA((N_DEV - 1,)), pltpu.SemaphoreType.DMA((N_DEV - 1,)),
                     pltpu.SemaphoreType.DMA]


def all_gather_hbm(shard, name):
    def body(x_ref, out_ref, send_sems, recv_sems, local_sem):
        _exchange_body(lambda k: x_ref, out_ref, send_sems, recv_sems, local_sem)

    return pl.pallas_call(
        body, name=name,
        out_shape=jax.ShapeDtypeStruct((N_DEV,) + shard.shape, shard.dtype),
        in_specs=[pl.BlockSpec(memory_space=pl.ANY)],
        out_specs=pl.BlockSpec(memory_space=pl.ANY),
        scratch_shapes=list(_EXCHANGE_SCRATCH),
    )(shard)


def all_to_all_hbm(slots, name):
    def body(x_ref, out_ref, send_sems, recv_sems, local_sem):
        _exchange_body(lambda k: x_ref.at[k], out_ref, send_sems, recv_sems, local_sem)

    return pl.pallas_call(
        body, name=name,
        out_shape=jax.ShapeDtypeStruct(slots.shape, slots.dtype),
        in_specs=[pl.BlockSpec(memory_space=pl.ANY)],
        out_specs=pl.BlockSpec(memory_space=pl.ANY),
        scratch_shapes=list(_EXCHANGE_SCRATCH),
    )(slots)


_HBM = pl.BlockSpec(memory_space=pltpu.HBM)
_SEM = pl.BlockSpec(memory_space=pltpu.SEMAPHORE)


def _split_copies(src_ref, gather, land_ref, send_sems, recv_sems):
    me = _me()
    pairs = []
    for j in range(1, N_DEV):
        peer, pidx = _peer(j)

        def make(slot, peer=peer, pidx=pidx, j=j):
            return pltpu.make_async_remote_copy(
                src_ref=src_ref if gather else src_ref.at[pidx], dst_ref=land_ref.at[slot],
                send_sem=send_sems.at[j - 1], recv_sem=recv_sems.at[j - 1], device_id=peer, device_id_type=MESH)

        pairs.append((make(me), make(pidx)))
    return pairs


def exchange_start(src, gather, name, after=None):
    land_shape = ((N_DEV,) + src.shape) if gather else src.shape
    has_after = after is not None

    def body(*refs):
        src_ref, land_ref = refs[:2]
        send_sems, recv_sems = refs[2 + has_after:4 + has_after]
        for send, _ in _split_copies(src_ref, gather, land_ref, send_sems, recv_sems):
            send.start()
        refs[-1][...] = jnp.zeros_like(refs[-1])

    sem = pltpu.SemaphoreType.DMA((N_DEV - 1,))
    send_sems, recv_sems, src_thru, land, token = pl.pallas_call(
        body, name=name,
        out_shape=(sem, sem, pltpu.HBM(src.shape, src.dtype), pltpu.HBM(land_shape, src.dtype),
                   jax.ShapeDtypeStruct((SUBLANES, LANES), F32)),
        in_specs=(_HBM, _HBM) + ((pl.BlockSpec(memory_space=pl.ANY),) if has_after else ()),
        out_specs=(_SEM, _SEM, _HBM, _HBM, pl.BlockSpec(memory_space=pltpu.VMEM)),
        input_output_aliases={0: 2, 1: 3},
        compiler_params=pltpu.CompilerParams(has_side_effects=pltpu.SideEffectType.DATAFLOW_SIDE_EFFECTING),
    )(pltpu.with_memory_space_constraint(src, pltpu.HBM),
      pltpu.with_memory_space_constraint(lax.empty(land_shape, src.dtype), pltpu.HBM),
      *((after,) if has_after else ()))
    return (send_sems, recv_sems, src_thru, land), token


def exchange_wait(handle, after, gather, name):
    send_sems, recv_sems, src_thru, land = handle
    after = tuple(after) if isinstance(after, (tuple, list)) else (after,)

    def body(src_ref, land_ref, send_sems, recv_sems, *rest):
        for _, arrival in _split_copies(src_ref, gather, land_ref, send_sems, recv_sems):
            arrival.wait_send()
            arrival.wait_recv()

    src_done, got = pl.pallas_call(
        body, name=name,
        out_shape=(pltpu.HBM(src_thru.shape, src_thru.dtype), pltpu.HBM(land.shape, land.dtype)),
        in_specs=(_HBM, _HBM, _SEM, _SEM) + (pl.BlockSpec(memory_space=pl.ANY),) * len(after), out_specs=(_HBM, _HBM),
        input_output_aliases={0: 0, 1: 1},
        compiler_params=pltpu.CompilerParams(has_side_effects=pltpu.SideEffectType.DATAFLOW_SIDE_EFFECTING),
    )(src_thru, land, send_sems, recv_sems, *after)
    mine = src_done if gather else lax.dynamic_index_in_dim(src_done, _me(), 0, keepdims=False)
    return lax.dynamic_update_index_in_dim(got, mine, _me(), 0)


def all_gather_sum_small(v, name):
    def body(x_ref, out_ref, sum_ref, send_sems, recv_sems, local_sem):
        _exchange_body(lambda k: x_ref, out_ref, send_sems, recv_sems, local_sem)
        acc = out_ref[0]
        for k in range(1, N_DEV):
            acc = acc + out_ref[k]
        sum_ref[...] = acc

    return pl.pallas_call(
        body, name=name,
        out_shape=(jax.ShapeDtypeStruct((N_DEV,) + v.shape, v.dtype), jax.ShapeDtypeStruct(v.shape, v.dtype)),
        in_specs=[pl.BlockSpec(memory_space=pltpu.VMEM)],
        out_specs=(pl.BlockSpec(memory_space=pltpu.VMEM), pl.BlockSpec(memory_space=pltpu.VMEM)),
        scratch_shapes=list(_EXCHANGE_SCRATCH),
    )(v)


def sum_slots(slots, name):
    _, p_dim, c_dim = slots.shape
    tp = next(tp for tp in range(512, 0, -BF16_ROWS) if p_dim % tp == 0)

    def body(x_ref, o_ref):
        acc = x_ref[0].astype(F32)
        for k in range(1, N_DEV):
            acc = acc + x_ref[k].astype(F32)
        o_ref[...] = acc

    return pl.pallas_call(
        body, name=name,
        out_shape=jax.ShapeDtypeStruct((p_dim, c_dim), F32),
        grid=(p_dim // tp,),
        in_specs=[pl.BlockSpec((N_DEV, tp, c_dim), lambda i: (0, i, 0))],
        out_specs=pl.BlockSpec((tp, c_dim), lambda i: (i, 0)),
        compiler_params=pltpu.CompilerParams(dimension_semantics=("parallel",)),
    )(slots)


def adamw(w, g, m, v, name):
    rows, cols = w.shape
    tr = _pick(rows, (256, 128, 64, 32, 16, 8))

    def body(w_ref, g_ref, m_ref, v_ref, d_ref, nm_ref, nv_ref):
        gv = g_ref[...]
        nm = ADAM_B1 * m_ref[...] + (1.0 - ADAM_B1) * gv
        nv = ADAM_B2 * v_ref[...] + (1.0 - ADAM_B2) * (gv * gv)
        m_hat = nm / (1.0 - ADAM_B1 ** ADAM_STEP)
        v_hat = nv / (1.0 - ADAM_B2 ** ADAM_STEP)
        d_ref[...] = -ADAM_LR * (m_hat / (jnp.sqrt(v_hat) + ADAM_EPS) + ADAM_WD * w_ref[...])
        nm_ref[...] = nm
        nv_ref[...] = nv

    spec = pl.BlockSpec((tr, cols), lambda i: (i, 0))
    shp = jax.ShapeDtypeStruct((rows, cols), F32)
    return pl.pallas_call(
        body, name=name, out_shape=(shp, shp, shp), grid=(rows // tr,),
        in_specs=[spec] * 4, out_specs=(spec,) * 3,
        compiler_params=pltpu.CompilerParams(dimension_semantics=("parallel",)),
    )(w, g, m, v)


ATT_BLK = 128
NEG = -1e30


def _head_sums(v):
    li = lax.broadcasted_iota(jnp.int32, (LANES, LANES), 0) // ATT_HEAD_DIM
    lj = lax.broadcasted_iota(jnp.int32, (LANES, LANES), 1) // ATT_HEAD_DIM
    ones = (li == lj).astype(BF16)
    hi = v.astype(BF16)
    lo = (v - hi.astype(F32)).astype(BF16)
    return jnp.dot(hi, ones, preferred_element_type=F32) + jnp.dot(lo, ones, preferred_element_type=F32)


def _head_col(v, hmask):
    return jnp.max(jnp.where(hmask, v, -jnp.inf), axis=-1, keepdims=True)


def _qk_norm(raw, gain2):
    rstd = lax.rsqrt(_head_sums(raw * raw) * (1.0 / ATT_HEAD_DIM) + NORM_EPS)
    xhat = raw * rstd
    return xhat * gain2, xhat, rstd


def _qk_norm_bwd(dn, xhat, rstd, gain2):
    dxh = dn * gain2
    return rstd * (dxh - xhat * (_head_sums(dxh * xhat) * (1.0 / ATT_HEAD_DIM))), dn * xhat


def _att_mask_bias(n, dilation):
    qi = lax.broadcasted_iota(jnp.int32, (ATT_BLK, 2 * ATT_BLK), 0)
    ki = lax.broadcasted_iota(jnp.int32, (ATT_BLK, 2 * ATT_BLK), 1)
    dist = qi + ATT_BLK - ki
    valid = (dist >= 0) & (dist <= ATT_BLK) & ((n > 0) | (ki >= ATT_BLK))
    return valid, (dilation * dist).astype(F32)


ATT_PAIRS = 4
RELAYOUT_ROWS = 512
RELAYOUT_COLS = 512


def _to_residues(x, dilation, col0=0, cols=None):
    t = x.shape[0]
    cols = x.shape[1] if cols is None else cols
    if dilation == 1 and col0 == 0 and cols == x.shape[1]:
        return x.reshape(1, t, cols)
    tr = _pick(t, (RELAYOUT_ROWS,))
    tc = _pick(cols, (RELAYOUT_COLS, 256, 128))
    per = tr // dilation
    assert tr % dilation == 0 and col0 % tc == 0

    def body(x_ref, o_ref, s_ref):
        for c in range(tc // LANES):
            lanes = slice(c * LANES, (c + 1) * LANES)
            s_ref[c] = x_ref[:, lanes].astype(F32)
            for r in range(dilation):
                o_ref[r, :, lanes] = s_ref[c, pl.ds(r, per, stride=dilation), :].astype(o_ref.dtype)

    return pl.pallas_call(
        body, name=f"to_residues_{dilation}", out_shape=jax.ShapeDtypeStruct((dilation, t // dilation, cols), x.dtype),
        grid=(t // tr, cols // tc),
        in_specs=[pl.BlockSpec((tr, tc), lambda i, j: (i, col0 // tc + j))],
        out_specs=pl.BlockSpec((dilation, per, tc), lambda i, j: (0, i, j)),
        scratch_shapes=[pltpu.VMEM((tc // LANES, tr, LANES), F32)],
        compiler_params=pltpu.CompilerParams(dimension_semantics=("parallel", "parallel")),
    )(x)


def _from_residues(y):
    dilation, lu, cols = y.shape
    t = dilation * lu
    if dilation == 1:
        return y.reshape(t, cols)
    tr = _pick(t, (RELAYOUT_ROWS,))
    tc = _pick(cols, (RELAYOUT_COLS, 256, 128))
    per = tr // dilation

    def body(y_ref, o_ref, s_ref):
        for c in range(tc // LANES):
            lanes = slice(c * LANES, (c + 1) * LANES)
            for r in range(dilation):
                s_ref[c, pl.ds(r, per, stride=dilation), :] = y_ref[r, :, lanes].astype(F32)
            o_ref[:, lanes] = s_ref[c].astype(o_ref.dtype)

    return pl.pallas_call(
        body, name=f"from_residues_{dilation}", out_shape=jax.ShapeDtypeStruct((t, cols), y.dtype),
        grid=(t // tr, cols // tc),
        in_specs=[pl.BlockSpec((dilation, per, tc), lambda i, j: (0, i, j))],
        out_specs=pl.BlockSpec((tr, tc), lambda i, j: (i, j)),
        scratch_shapes=[pltpu.VMEM((tc // LANES, tr, LANES), F32)],
        compiler_params=pltpu.CompilerParams(dimension_semantics=("parallel", "parallel")),
    )(y)


def _att_specs(base, hd, nb, pp):
    hpn = hd // LANES
    assert base % (pp * LANES) == 0 and hpn % pp == 0

    def spec(which, shift):
        def imap(r, hp, n):
            row = jnp.minimum(n, nb - 1) if shift == 0 else jnp.maximum(n - 1, 0)
            return (r, row, (base // LANES + which * hpn) // pp + hp)
        return pl.BlockSpec((None, ATT_BLK, pp * LANES), imap)

    return [spec(0, 0), spec(1, 1), spec(1, 0), spec(2, 1), spec(2, 0)]


DELTA_LANE = 64


def _att_pre(qkv, g, dilation, gq2, gk2):
    t, width = qkv.shape
    hd = width // (3 * len(DIL_PATTERNS))
    tr = _pick(t, (RELAYOUT_ROWS,))
    tc = _pick(hd, (RELAYOUT_COLS, 256, 128))
    per = tr // dilation
    assert tr % dilation == 0 and (g * 3 * hd) % tc == 0

    def body(x_ref, gq_ref, gk_ref, o_ref, s_ref):
        section = pl.program_id(1) * tc // hd
        gain = jnp.where(section == 0, gq_ref[0:1, :], gk_ref[0:1, :])
        for c in range(tc // LANES):
            lanes = slice(c * LANES, (c + 1) * LANES)
            raw = x_ref[:, lanes].astype(F32)
            s_ref[c] = jnp.where(section == 2, raw, _qk_norm(raw, gain)[0])
            for r in range(dilation):
                o_ref[r, :, lanes] = s_ref[c, pl.ds(r, per, stride=dilation), :].astype(o_ref.dtype)

    vec_spec = pl.BlockSpec((SUBLANES, LANES), lambda i, j: (0, 0))
    return pl.pallas_call(
        body, name=f"att_pre_g{g}", out_shape=jax.ShapeDtypeStruct((dilation, t // dilation, 3 * hd), qkv.dtype),
        grid=(t // tr, 3 * hd // tc),
        in_specs=[pl.BlockSpec((tr, tc), lambda i, j: (i, g * 3 * hd // tc + j)), vec_spec, vec_spec],
        out_specs=pl.BlockSpec((dilation, per, tc), lambda i, j: (0, i, j)),
        scratch_shapes=[pltpu.VMEM((tc // LANES, tr, LANES), F32)],
        compiler_params=pltpu.CompilerParams(dimension_semantics=("parallel", "parallel")),
    )(qkv, gq2, gk2)


def _att_group_fwd(qkv_r, hd, slopes, g):
    dilation, lu, _ = qkv_r.shape
    nb = lu // ATT_BLK
    assert nb * ATT_BLK == lu and hd % LANES == 0
    hpn = hd // LANES
    pp = math.gcd(ATT_PAIRS, hpn)
    scale = 1.0 / math.sqrt(ATT_HEAD_DIM)

    def body(q_ref, kp_ref, kc_ref, vp_ref, vc_ref, sl_ref, o_ref, l_ref):
        n = pl.program_id(2)
        lane = lax.broadcasted_iota(jnp.int32, (1, LANES), 1)
        first = (lane // ATT_HEAD_DIM) == 0
        valid, dist = _att_mask_bias(n, dilation)
        stats = jnp.zeros((ATT_BLK, LANES), F32)
        for pair in range(pp):
            cols = slice(pair * LANES, (pair + 1) * LANES)
            qn = q_ref[:, cols]
            kn16 = jnp.concatenate([kp_ref[:, cols], kc_ref[:, cols]], axis=0)
            v16 = jnp.concatenate([vp_ref[:, cols], vc_ref[:, cols]], axis=0)
            outs = []
            for hh in range(2):
                hmask = (lane // ATT_HEAD_DIM) == hh
                qh = jnp.where(hmask, qn, jnp.zeros_like(qn))
                s = lax.dot_general(qh, kn16, (((1,), (1,)), ((), ())), preferred_element_type=F32) * scale
                slope = _head_col(sl_ref[pair, 0:1, :], hmask)
                logits = jnp.where(valid, s - slope * dist, NEG)
                mx = jnp.max(logits, axis=-1, keepdims=True)
                pexp = jnp.exp(logits - mx)
                den = jnp.sum(pexp, axis=-1, keepdims=True)
                outs.append(jnp.dot(pexp.astype(BF16), v16, preferred_element_type=F32) / den)
                stats = jnp.where(lane == 2 * pair + hh, mx + jnp.log(den), stats)
            o_ref[:, cols] = jnp.where(first, outs[0], outs[1]).astype(BF16)
        l_ref[...] = stats

    out_spec = pl.BlockSpec((None, ATT_BLK, pp * LANES), lambda r, hp, n: (r, n, hp))
    stat_spec = pl.BlockSpec((None, ATT_BLK, LANES), lambda r, hp, n: (r, n, hp))
    o, lse = pl.pallas_call(
        body, name=f"att_fwd_g{g}",
        out_shape=(jax.ShapeDtypeStruct((dilation, lu, hd), BF16),
                   jax.ShapeDtypeStruct((dilation, lu, hpn // pp * LANES), F32)),
        grid=(dilation, hpn // pp, nb),
        in_specs=_att_specs(0, hd, nb, pp) + [pl.BlockSpec((pp, SUBLANES, LANES), lambda r, hp, n: (hp, 0, 0))],
        out_specs=(out_spec, stat_spec),
        compiler_params=pltpu.CompilerParams(dimension_semantics=("parallel", "parallel", "arbitrary")),
    )(*([qkv_r] * 5), slopes)
    return _from_residues(o), _from_residues(lse)


def _att_merge(outs, lses):
    t, hd = outs[0].shape
    sw = lses[0].shape[1]
    pp = hd // sw
    tr = _pick(t, (256, 128))
    ng = len(outs)

    def body(*refs):
        o_refs, l_refs, o16_ref, lt_ref = refs[:ng], refs[ng:2 * ng], refs[2 * ng], refs[2 * ng + 1]
        lane = lax.broadcasted_iota(jnp.int32, (1, LANES), 1)
        first = (lane // ATT_HEAD_DIM) == 0
        for blk in range(sw // LANES):
            scols = slice(blk * LANES, (blk + 1) * LANES)
            stats = jnp.zeros((tr, LANES), F32)
            for pair in range(pp):
                cols = slice((blk * pp + pair) * LANES, (blk * pp + pair + 1) * LANES)
                weights = []
                for hh in range(2):
                    pick = lane == 2 * pair + hh
                    ls = [_head_col(r[:, scols], pick) for r in l_refs]
                    mx = functools.reduce(jnp.maximum, ls)
                    es = [jnp.exp(l - mx) for l in ls]
                    den = functools.reduce(jnp.add, es)
                    weights.append([e / den for e in es])
                    stats = jnp.where(pick, mx + jnp.log(den), stats)
                acc = jnp.zeros((tr, LANES), F32)
                for gi in range(ng):
                    acc = acc + jnp.where(first, weights[0][gi], weights[1][gi]) * o_refs[gi][:, cols].astype(F32)
                o16_ref[:, cols] = acc.astype(BF16)
            lt_ref[:, scols] = stats

    spec = pl.BlockSpec((tr, hd), lambda i: (i, 0))
    sspec = pl.BlockSpec((tr, sw), lambda i: (i, 0))
    return pl.pallas_call(
        body, name="att_merge",
        out_shape=(jax.ShapeDtypeStruct((t, hd), BF16), jax.ShapeDtypeStruct((t, sw), F32)), grid=(t // tr,),
        in_specs=[spec] * ng + [sspec] * ng, out_specs=(spec, sspec),
        compiler_params=pltpu.CompilerParams(dimension_semantics=("parallel",)),
    )(*outs, *lses)


def _att_bwd_prep(do, o16, lse_tot):
    t, hd = do.shape
    sw = lse_tot.shape[1]
    pp = hd // sw
    tr = _pick(t, (256, 128))

    def body(do_ref, o_ref, l_ref, d16_ref, st_ref):
        lane = lax.broadcasted_iota(jnp.int32, (1, LANES), 1)
        d16_ref[...] = do_ref[...].astype(BF16)
        for blk in range(sw // LANES):
            scols = slice(blk * LANES, (blk + 1) * LANES)
            stats = l_ref[:, scols]
            for pair in range(pp):
                cols = slice((blk * pp + pair) * LANES, (blk * pp + pair + 1) * LANES)
                prod = do_ref[:, cols] * o_ref[:, cols].astype(F32)
                for hh in range(2):
                    hmask = (lane // ATT_HEAD_DIM) == hh
                    delta = jnp.sum(jnp.where(hmask, prod, 0.0), axis=-1, keepdims=True)
                    stats = jnp.where(lane == DELTA_LANE + 2 * pair + hh, delta, stats)
            st_ref[:, scols] = stats

    spec = pl.BlockSpec((tr, hd), lambda i: (i, 0))
    sspec = pl.BlockSpec((tr, sw), lambda i: (i, 0))
    return pl.pallas_call(
        body, name="att_bwd_prep",
        out_shape=(jax.ShapeDtypeStruct((t, hd), BF16), jax.ShapeDtypeStruct((t, sw), F32)), grid=(t // tr,),
        in_specs=[spec, spec, sspec], out_specs=(spec, sspec),
        compiler_params=pltpu.CompilerParams(dimension_semantics=("parallel",)),
    )(do, o16, lse_tot)


def _att_post(buf, parts, qkv, gq2, gk2, g):
    dilation, lu, hd = parts[0].shape
    t = dilation * lu
    tr = _pick(t, (RELAYOUT_ROWS,))
    per = tr // dilation

    def body(*refs):
        raw_ref, gq_ref, gk_ref = refs[3:6]
        o_ref, dgq_ref, dgk_ref, s_ref = refs[-4:]

        @pl.when(pl.program_id(0) == 0)
        def _():
            dgq_ref[...] = jnp.zeros_like(dgq_ref)
            dgk_ref[...] = jnp.zeros_like(dgk_ref)

        for sec, y_ref in enumerate(refs[:3]):
            gsum = jnp.zeros((1, LANES), F32)
            for c in range(hd // LANES):
                lanes = slice(c * LANES, (c + 1) * LANES)
                out_lanes = slice(sec * hd + c * LANES, sec * hd + (c + 1) * LANES)
                for r in range(dilation):
                    s_ref[pl.ds(r, per, stride=dilation), :] = y_ref[r, :, lanes].astype(F32)
                d = s_ref[...]
                if sec < 2:
                    gain = (gq_ref if sec == 0 else gk_ref)[0:1, :]
                    _, xhat, rstd = _qk_norm(raw_ref[:, out_lanes].astype(F32), gain)
                    d, part = _qk_norm_bwd(d, xhat, rstd, gain)
                    gsum = gsum + jnp.sum(part, axis=0, keepdims=True)
                o_ref[:, out_lanes] = d.astype(o_ref.dtype)
            if sec < 2:
                acc = dgq_ref if sec == 0 else dgk_ref
                acc[...] += jnp.broadcast_to(gsum, acc.shape)

    part_spec = pl.BlockSpec((dilation, per, hd), lambda i: (0, i, 0))
    slab_spec = pl.BlockSpec((tr, 3 * hd), lambda i: (i, g))
    vec_spec = pl.BlockSpec((SUBLANES, LANES), lambda i: (0, 0))
    vec_shape = jax.ShapeDtypeStruct((SUBLANES, LANES), F32)
    return pl.pallas_call(
        body, name=f"att_post_g{g}", out_shape=(jax.ShapeDtypeStruct(qkv.shape, qkv.dtype), vec_shape, vec_shape),
        grid=(t // tr,),
        in_specs=[part_spec] * 3 + [slab_spec, vec_spec, vec_spec] + (
            [] if buf is None else [pl.BlockSpec(memory_space=pl.ANY)]),
        out_specs=(slab_spec, vec_spec, vec_spec),
        scratch_shapes=[pltpu.VMEM((tr, LANES), F32)],
        input_output_aliases={} if buf is None else {6: 0},
        compiler_params=pltpu.CompilerParams(dimension_semantics=("arbitrary",)),
    )(*parts, qkv, gq2, gk2, *(() if buf is None else (buf,)))


def _att_group_bwd(qkv_r, hd, slopes, stats, do16, g):
    dilation, lu, _ = qkv_r.shape
    nb = lu // ATT_BLK
    hpn = hd // LANES
    pp = math.gcd(ATT_PAIRS, hpn)
    hbn = hpn // pp
    scale = 1.0 / math.sqrt(ATT_HEAD_DIM)

    def body(q_ref, kp_ref, kc_ref, vp_ref, vc_ref, sl_ref, st_ref, do_ref, dq_ref, dk_ref, dv_ref, ck_ref, cv_ref):
        n = pl.program_id(2)
        lane = lax.broadcasted_iota(jnp.int32, (1, LANES), 1)

        @pl.when(n == 0)
        def _():
            ck_ref[...] = jnp.zeros_like(ck_ref)
            cv_ref[...] = jnp.zeros_like(cv_ref)

        @pl.when(n < nb)
        def _():
            valid, dist = _att_mask_bias(n, dilation)
            stats = st_ref[...]
            for pair in range(pp):
                cols = slice(pair * LANES, (pair + 1) * LANES)
                qn = q_ref[:, cols]
                kn16 = jnp.concatenate([kp_ref[:, cols], kc_ref[:, cols]], axis=0)
                v16 = jnp.concatenate([vp_ref[:, cols], vc_ref[:, cols]], axis=0)
                dov = do_ref[:, cols]
                dq_acc = jnp.zeros((ATT_BLK, LANES), F32)
                dk_acc = jnp.zeros((2 * ATT_BLK, LANES), F32)
                dv_acc = jnp.zeros((2 * ATT_BLK, LANES), F32)
                for hh in range(2):
                    hmask = (lane // ATT_HEAD_DIM) == hh
                    qh = jnp.where(hmask, qn, jnp.zeros_like(qn))
                    doh = jnp.where(hmask, dov, jnp.zeros_like(dov))
                    s = lax.dot_general(qh, kn16, (((1,), (1,)), ((), ())), preferred_element_type=F32) * scale
                    slope = _head_col(sl_ref[pair, 0:1, :], hmask)
                    lse = _head_col(stats, lane == 2 * pair + hh)
                    delta = _head_col(stats, lane == DELTA_LANE + 2 * pair + hh)
                    pr = jnp.exp(jnp.where(valid, s - slope * dist - lse, NEG))
                    dp = lax.dot_general(doh, v16, (((1,), (1,)), ((), ())), preferred_element_type=F32)
                    ds = (pr * (dp - delta) * scale).astype(BF16)
                    dq_acc = dq_acc + jnp.where(hmask, jnp.dot(ds, kn16, preferred_element_type=F32), 0.0)
                    dk_acc = dk_acc + lax.dot_general(ds, qh, (((0,), (0,)), ((), ())), preferred_element_type=F32)
                    dv_acc = dv_acc + lax.dot_general(pr.astype(BF16), doh, (((0,), (0,)), ((), ())),
                                                      preferred_element_type=F32)
                dq_ref[:, cols] = dq_acc.astype(dq_ref.dtype)
                dk_ref[:, cols] = (ck_ref[:, cols] + dk_acc[:ATT_BLK]).astype(dk_ref.dtype)
                dv_ref[:, cols] = (cv_ref[:, cols] + dv_acc[:ATT_BLK]).astype(dv_ref.dtype)
                ck_ref[:, cols] = dk_acc[ATT_BLK:]
                cv_ref[:, cols] = dv_acc[ATT_BLK:]

        @pl.when(n == nb)
        def _():
            dk_ref[...] = ck_ref[...].astype(dk_ref.dtype)
            dv_ref[...] = cv_ref[...].astype(dv_ref.dtype)

    width = pp * LANES
    q_out = pl.BlockSpec((None, ATT_BLK, width), lambda r, hp, n: (r, jnp.minimum(n, nb - 1), hp))
    kv_out = pl.BlockSpec((None, ATT_BLK, width), lambda r, hp, n: (r, jnp.maximum(n - 1, 0), hp))
    st_spec = pl.BlockSpec((None, ATT_BLK, LANES), lambda r, hp, n: (r, jnp.minimum(n, nb - 1), hp))
    shp = jax.ShapeDtypeStruct((dilation, lu, hd), BF16)
    return pl.pallas_call(
        body, name=f"att_bwd_g{g}", out_shape=(shp, shp, shp), grid=(dilation, hbn, nb + 1),
        in_specs=_att_specs(0, hd, nb, pp) + [
            pl.BlockSpec((pp, SUBLANES, LANES), lambda r, hp, n: (hp, 0, 0)), st_spec, q_out],
        out_specs=(q_out, kv_out, kv_out),
        scratch_shapes=[pltpu.VMEM((ATT_BLK, width), F32), pltpu.VMEM((ATT_BLK, width), F32)],
        compiler_params=pltpu.CompilerParams(dimension_semantics=("parallel", "parallel", "arbitrary")),
    )(*([qkv_r] * 5), slopes, _to_residues(stats, dilation), _to_residues(do16, dilation))


def _att_consts(q_gain, k_gain, hd):
    heads = hd // ATT_HEAD_DIM
    gq2 = jnp.broadcast_to(jnp.tile(q_gain, 2)[None], (SUBLANES, LANES))
    gk2 = jnp.broadcast_to(jnp.tile(k_gain, 2)[None], (SUBLANES, LANES))
    sl = 2.0 ** (-8.0 * jnp.arange(1, heads + 1, dtype=F32) / heads)
    slopes = jnp.broadcast_to(jnp.repeat(sl, ATT_HEAD_DIM).reshape(hd // LANES, 1, LANES), (hd // LANES, SUBLANES, LANES))
    return gq2, gk2, slopes


def _attention_core_fwd(qkv, q_gain, k_gain):
    hd = qkv.shape[1] // (3 * len(DIL_PATTERNS))
    gq2, gk2, slopes = _att_consts(q_gain, k_gain, hd)
    outs, lses, views = [], [], []
    for g, (_, dilation) in enumerate(DIL_PATTERNS):
        qkv_r = _att_pre(qkv, g, dilation, gq2, gk2)
        o_g, l_g = _att_group_fwd(qkv_r, hd, slopes, g)
        outs.append(o_g)
        lses.append(l_g)
        views.append(qkv_r)
    o16, lse_tot = _att_merge(outs, lses)
    return o16, (qkv, views, q_gain, k_gain, o16, lse_tot)


def _attention_core_bwd(res, do):
    qkv, views, q_gain, k_gain, o16, lse_tot = res
    hd = o16.shape[1]
    gq2, gk2, slopes = _att_consts(q_gain, k_gain, hd)
    do16, stats = _att_bwd_prep(do, o16, lse_tot)
    dqkv, dgq, dgk = None, 0.0, 0.0
    for g, qkv_r in enumerate(views):
        parts = _att_group_bwd(qkv_r, hd, slopes, stats, do16, g)
        dqkv, a, b = _att_post(dqkv, parts, qkv, gq2, gk2, g)
        dgq = dgq + a[0].reshape(-1, ATT_HEAD_DIM).sum(0)
        dgk = dgk + b[0].reshape(-1, ATT_HEAD_DIM).sum(0)
    return dqkv, dgq, dgk


HALO = 8


def _silu(v):
    return v * jax.nn.sigmoid(v)


def _silu_grad(v):
    s = jax.nn.sigmoid(v)
    return s * (1.0 + v * (1.0 - s))


def _halo_rows(dtype):
    return BF16_ROWS if dtype == BF16 else HALO


def _conv_fwd(zx, conv_w, conv_b, d_inner):
    t = zx.shape[0]
    conv_dim = conv_w.shape[1]
    cb = _pick(d_inner, (1024, 512, 256, 128))
    assert conv_dim % cb == 0
    tr = _pick(t, (256, 128))
    off = d_inner // cb
    hx = _halo_rows(zx.dtype)

    def body(x_ref, h_ref, w_ref, b_ref, o_ref):
        i = pl.program_id(1)
        halo = jnp.where(i > 0, h_ref[...].astype(F32), 0.0)
        ext = jnp.concatenate([halo, x_ref[...].astype(F32)], axis=0)
        acc = jnp.broadcast_to(b_ref[...], (tr, cb))
        for k in range(CONV_WIDTH):
            s = CONV_WIDTH - 1 - k
            sh = ext if s == 0 else pltpu.roll(ext, shift=s, axis=0)
            acc = acc + w_ref[k:k + 1, :] * sh[hx:hx + tr]
        o_ref[...] = acc

    return pl.pallas_call(
        body, name="ssm_conv_fwd", out_shape=jax.ShapeDtypeStruct((t, conv_dim), F32),
        grid=(conv_dim // cb, t // tr),
        in_specs=[pl.BlockSpec((tr, cb), lambda j, i: (i, off + j)),
                  pl.BlockSpec((hx, cb), lambda j, i: (jnp.maximum(i * (tr // hx) - 1, 0), off + j)),
                  pl.BlockSpec((CONV_WIDTH, cb), lambda j, i: (0, j)),
                  pl.BlockSpec((1, cb), lambda j, i: (0, j))],
        out_specs=pl.BlockSpec((tr, cb), lambda j, i: (i, j)),
        compiler_params=pltpu.CompilerParams(dimension_semantics=("parallel", "parallel")),
    )(zx, zx, conv_w, conv_b.reshape(1, -1))


def _conv_bwd(zx, conv_w, dpre, dzx, d_inner, col0):
    t, width = zx.shape
    conv_dim = dpre.shape[1]
    cb = _pick(conv_dim, (1024, 512, 256, 128))
    assert (d_inner + col0) % cb == 0
    tr = _pick(t, (256, 128))
    off = (d_inner + col0) // cb
    woff = col0 // cb
    nr = t // tr
    hx = _halo_rows(zx.dtype)

    def body(x_ref, h_ref, w_ref, d_ref, dn_ref, dzx_in, dx_ref, dw_ref, db_ref):
        i = pl.program_id(1)

        @pl.when(i == 0)
        def _():
            dw_ref[...] = jnp.zeros_like(dw_ref)
            db_ref[...] = jnp.zeros_like(db_ref)

        halo = jnp.where(i > 0, h_ref[...].astype(F32), 0.0)
        ext = jnp.concatenate([halo, x_ref[...].astype(F32)], axis=0)
        d = d_ref[...]
        dext = jnp.concatenate([d, jnp.where(i < nr - 1, dn_ref[...], 0.0)], axis=0)
        dx = jnp.zeros((tr, cb), F32)
        for k in range(CONV_WIDTH):
            s = CONV_WIDTH - 1 - k
            fut = dext if s == 0 else pltpu.roll(dext, shift=tr + HALO - s, axis=0)
            dx = dx + w_ref[k:k + 1, :] * fut[:tr]
            past = ext if s == 0 else pltpu.roll(ext, shift=s, axis=0)
            dw_ref[k:k + 1, :] += jnp.sum(d * past[hx:hx + tr], axis=0, keepdims=True)
        dx_ref[...] = dx.astype(dx_ref.dtype)
        db_ref[...] += jnp.sum(d, axis=0, keepdims=True)

    last_halo = t // HALO - 1
    return pl.pallas_call(
        body, name=f"ssm_conv_bwd_{col0}",
        out_shape=(jax.ShapeDtypeStruct(dzx.shape, dzx.dtype), jax.ShapeDtypeStruct((CONV_WIDTH, conv_dim), F32),
                   jax.ShapeDtypeStruct((1, conv_dim), F32)),
        grid=(conv_dim // cb, nr),
        in_specs=[pl.BlockSpec((tr, cb), lambda j, i: (i, off + j)),
                  pl.BlockSpec((hx, cb), lambda j, i: (jnp.maximum(i * (tr // hx) - 1, 0), off + j)),
                  pl.BlockSpec((CONV_WIDTH, cb), lambda j, i: (0, woff + j)),
                  pl.BlockSpec((tr, cb), lambda j, i: (i, j)),
                  pl.BlockSpec((HALO, cb), lambda j, i: (jnp.minimum((i + 1) * (tr // HALO), last_halo), j)),
                  pl.BlockSpec(memory_space=pl.ANY)],
        out_specs=(pl.BlockSpec((tr, cb), lambda j, i: (i, off + j)),
                   pl.BlockSpec((CONV_WIDTH, cb), lambda j, i: (0, j)),
                   pl.BlockSpec((1, cb), lambda j, i: (0, j))),
        input_output_aliases={5: 0},
        compiler_params=pltpu.CompilerParams(dimension_semantics=("parallel", "arbitrary")),
    )(zx, zx, conv_w, dpre, dpre, dzx)


def _eye(n):
    return lax.broadcasted_iota(jnp.int32, (n, n), 0) == lax.broadcasted_iota(jnp.int32, (n, n), 1)


def _row_to_col(row):
    n = row.shape[1]
    return jnp.sum(jnp.where(_eye(n), row, 0.0), axis=1, keepdims=True)


def _col_to_row(col):
    n = col.shape[0]
    return jnp.sum(jnp.where(_eye(n), col, 0.0), axis=0, keepdims=True)


def _pair_lanes(c0, c1):
    lane = lax.broadcasted_iota(jnp.int32, (1, LANES), 1)
    return jnp.where(lane < SSM_HEAD_DIM, c0, c1)


def _ssd_chunk_common(pre_x_ref, pre_b_ref, pre_c_ref, dtr_ref, bias_ref, alog_ref, cs_ref):
    cl = SSD_CHUNK
    hpg = dtr_ref.shape[0]
    x = _silu(pre_x_ref[...])
    b16 = _silu(pre_b_ref[...]).astype(BF16)
    c16 = _silu(pre_c_ref[...]).astype(BF16)
    dt = jax.nn.softplus(dtr_ref[...] + bias_ref[...])
    a = -jnp.exp(alog_ref[...])
    li = lax.broadcasted_iota(jnp.int32, (cl, cl), 0)
    si = lax.broadcasted_iota(jnp.int32, (cl, cl), 1)
    upper = (li <= si).astype(F32)
    cs_ref[0:hpg, :] = jnp.dot(dt * a, upper, precision=lax.Precision.HIGHEST, preferred_element_type=F32)
    cs_ref[hpg:2 * hpg, :] = dt
    g = lax.dot_general(c16, b16, (((1,), (1,)), ((), ())), preferred_element_type=F32)
    return x, b16, c16, dt, a, g, li >= si


def _ssd_fwd(pre, dtT, bias, alog, dskip_lanes, d_inner):
    t = pre.shape[0]
    cl = SSD_CHUNK
    nc = t // cl
    ng = SSM_GROUPS
    hpg = dtT.shape[1]
    gw = hpg * SSM_HEAD_DIM
    assert d_inner == ng * gw and hpg % 2 == 0
    bo = d_inner // SSM_STATE

    def body(px_ref, pb_ref, pc_ref, dtr_ref, bias_ref, alog_ref, dsk_ref, y_ref, st_ref, s_ref, cs_ref):
        c = pl.program_id(1)

        @pl.when(c == 0)
        def _():
            s_ref[...] = jnp.zeros_like(s_ref)

        x, b16, c16, dt, a, g, causal = _ssd_chunk_common(px_ref, pb_ref, pc_ref, dtr_ref, bias_ref, alog_ref, cs_ref)
        st_ref[...] = s_ref[...]
        yoff = lax.dot_general(c16, s_ref[...].astype(BF16), (((1,), (1,)), ((), ())), preferred_element_type=F32)
        xde_parts = []
        for j in range(hpg // 2):
            cols = slice(j * LANES, (j + 1) * LANES)
            xp = x[:, cols]
            dcol, ecol, ocol, ms = [], [], [], []
            for hh in range(2):
                h = 2 * j + hh
                cs_row = cs_ref[h:h + 1, :]
                cs_col = _row_to_col(cs_row)
                dcol.append(_row_to_col(cs_ref[hpg + h:hpg + h + 1, :]))
                ecol.append(jnp.exp(cs_ref[h:h + 1, cl - 1:cl] - cs_col))
                ocol.append(jnp.exp(cs_col))
                lm = jnp.where(causal, jnp.exp(jnp.minimum(cs_col - cs_row, 0.0)), 0.0)
                ms.append((g * lm).astype(BF16))
            xd = xp * _pair_lanes(dcol[0], dcol[1])
            xd16 = xd.astype(BF16)
            yd = _pair_lanes(1.0, 0.0) * jnp.dot(ms[0], xd16, preferred_element_type=F32) \
                + _pair_lanes(0.0, 1.0) * jnp.dot(ms[1], xd16, preferred_element_type=F32)
            y_ref[:, cols] = yd + yoff[:, cols] * _pair_lanes(ocol[0], ocol[1]) + xp * dsk_ref[0:1, cols]
            xde_parts.append((xd * _pair_lanes(ecol[0], ecol[1])).astype(BF16))
        new = lax.dot_general(jnp.concatenate(xde_parts, axis=1), b16, (((0,), (0,)), ((), ())),
                              preferred_element_type=F32)
        for h in range(hpg):
            rows = slice(h * SSM_HEAD_DIM, (h + 1) * SSM_HEAD_DIM)
            s_ref[rows, :] = s_ref[rows, :] * jnp.exp(cs_ref[h:h + 1, cl - 1:cl]) + new[rows, :]

    vec = lambda n: pl.BlockSpec((None, hpg, n), lambda gi, c: (gi, 0, 0))
    return pl.pallas_call(
        body, name="ssd_fwd",
        out_shape=(jax.ShapeDtypeStruct((t, d_inner), F32), jax.ShapeDtypeStruct((ng, nc, gw, SSM_STATE), F32)),
        grid=(ng, nc),
        in_specs=[pl.BlockSpec((cl, gw), lambda gi, c: (c, gi)),
                  pl.BlockSpec((cl, SSM_STATE), lambda gi, c: (c, bo + gi)),
                  pl.BlockSpec((cl, SSM_STATE), lambda gi, c: (c, bo + ng + gi)),
                  pl.BlockSpec((None, hpg, cl), lambda gi, c: (gi, 0, c)),
                  vec(1), vec(1),
                  pl.BlockSpec((1, gw), lambda gi, c: (0, gi))],
        out_specs=(pl.BlockSpec((cl, gw), lambda gi, c: (c, gi)),
                   pl.BlockSpec((None, None, gw, SSM_STATE), lambda gi, c: (gi, c, 0, 0))),
        scratch_shapes=[pltpu.VMEM((gw, SSM_STATE), F32), pltpu.VMEM((2 * hpg, cl), F32)],
        compiler_params=pltpu.CompilerParams(dimension_semantics=("parallel", "arbitrary")),
    )(pre, pre, pre, dtT, bias, alog, dskip_lanes)


def _ssd_bwd(pre, dtT, bias, alog, dskip_lanes, states, dy, d_inner):
    t, conv_dim = pre.shape
    cl = SSD_CHUNK
    nc = t // cl
    ng = SSM_GROUPS
    hpg = dtT.shape[1]
    gw = hpg * SSM_HEAD_DIM
    bo = d_inner // SSM_STATE

    def body(px_ref, pb_ref, pc_ref, dtr_ref, bias_ref, alog_ref, dsk_ref, st_ref, dy_ref,
             dx_ref, db_ref, dc_ref, ddt_ref, acc_ref, dsk_out, ds_ref, cs_ref, dcs_ref):
        c = pl.program_id(1)

        @pl.when(c == 0)
        def _():
            ds_ref[...] = jnp.zeros_like(ds_ref)
            acc_ref[...] = jnp.zeros_like(acc_ref)
            dsk_out[...] = jnp.zeros_like(dsk_out)

        x, b16, c16, dt, a, g, causal = _ssd_chunk_common(px_ref, pb_ref, pc_ref, dtr_ref, bias_ref, alog_ref, cs_ref)
        s_prev = st_ref[...]
        s16 = s_prev.astype(BF16)
        ds = ds_ref[...]
        ds16 = ds.astype(BF16)
        dyv = dy_ref[...]
        yoff = lax.dot_general(c16, s16, (((1,), (1,)), ((), ())), preferred_element_type=F32)
        bds = lax.dot_general(b16, ds16, (((1,), (1,)), ((), ())), preferred_element_type=F32)
        dg = jnp.zeros((cl, cl), F32)
        xde_parts, dye_parts = [], []
        lane = lax.broadcasted_iota(jnp.int32, (1, LANES), 1)
        for j in range(hpg // 2):
            cols = slice(j * LANES, (j + 1) * LANES)
            xp, dyp = x[:, cols], dyv[:, cols]
            dcol, ecol, ocol, lms = [], [], [], []
            for hh in range(2):
                h = 2 * j + hh
                cs_row = cs_ref[h:h + 1, :]
                cs_col = _row_to_col(cs_row)
                dcol.append(_row_to_col(cs_ref[hpg + h:hpg + h + 1, :]))
                ecol.append(jnp.exp(cs_ref[h:h + 1, cl - 1:cl] - cs_col))
                ocol.append(jnp.exp(cs_col))
                lms.append(jnp.where(causal, jnp.exp(jnp.minimum(cs_col - cs_row, 0.0)), 0.0))
            dlanes, elanes, olanes = _pair_lanes(*dcol), _pair_lanes(*ecol), _pair_lanes(*ocol)
            xd = xp * dlanes
            xd16 = xd.astype(BF16)
            xde = xd * elanes
            yoffp = yoff[:, cols] * olanes
            bdsp = bds[:, cols]
            dxd = bdsp * elanes
            for hh in range(2):
                h = 2 * j + hh
                hmask = (lane // SSM_HEAD_DIM) == hh
                dyh16 = jnp.where(hmask, dyp, 0.0).astype(BF16)
                m = g * lms[hh]
                dm = lax.dot_general(dyh16, xd16, (((1,), (1,)), ((), ())), preferred_element_type=F32)
                w = dm * m
                dg = dg + dm * lms[hh]
                dxd = dxd + lax.dot_general(m.astype(BF16), dyh16, (((0,), (0,)), ((), ())),
                                            preferred_element_type=F32)
                term = jnp.sum(jnp.where(hmask, xde * bdsp, 0.0), axis=1, keepdims=True)
                dcs_col = (jnp.sum(w, axis=1, keepdims=True)
                           + jnp.sum(jnp.where(hmask, dyp * yoffp, 0.0), axis=1, keepdims=True) - term)
                rows = slice(h * SSM_HEAD_DIM, (h + 1) * SSM_HEAD_DIM)
                dec = jnp.exp(cs_ref[h:h + 1, cl - 1:cl])
                tail = jnp.sum(term, axis=0, keepdims=True) + dec * jnp.sum(
                    jnp.sum(s_prev[rows, :] * ds[rows, :], axis=1, keepdims=True), axis=0, keepdims=True)
                last = lax.broadcasted_iota(jnp.int32, (1, cl), 1) == cl - 1
                dcs_ref[h:h + 1, :] = _col_to_row(dcs_col) - jnp.sum(w, axis=0, keepdims=True) + jnp.where(last, tail, 0.0)
                dcs_ref[hpg + h:hpg + h + 1, :] = _col_to_row(
                    jnp.sum(jnp.where(hmask, dxd * xp, 0.0), axis=1, keepdims=True))
            dx_act = dxd * dlanes + dyp * dsk_ref[0:1, cols]
            dx_ref[:, cols] = dx_act * _silu_grad(px_ref[:, cols])
            dsk_out[0:1, cols] += jnp.sum(dyp * xp, axis=0, keepdims=True)
            xde_parts.append(xde.astype(BF16))
            dye_parts.append((dyp * olanes).astype(BF16))
        xde16 = jnp.concatenate(xde_parts, axis=1)
        dye16 = jnp.concatenate(dye_parts, axis=1)
        dg16 = dg.astype(BF16)
        dc_act = jnp.dot(dg16, b16, preferred_element_type=F32) + jnp.dot(dye16, s16, preferred_element_type=F32)
        db_act = lax.dot_general(dg16, c16, (((0,), (0,)), ((), ())), preferred_element_type=F32) \
            + jnp.dot(xde16, ds16, preferred_element_type=F32)
        dc_ref[...] = dc_act * _silu_grad(pc_ref[...])
        db_ref[...] = db_act * _silu_grad(pb_ref[...])
        ds_new = lax.dot_general(dye16, c16, (((0,), (0,)), ((), ())), preferred_element_type=F32)
        for h in range(hpg):
            rows = slice(h * SSM_HEAD_DIM, (h + 1) * SSM_HEAD_DIM)
            ds_ref[rows, :] = ds[rows, :] * jnp.exp(cs_ref[h:h + 1, cl - 1:cl]) + ds_new[rows, :]
        li = lax.broadcasted_iota(jnp.int32, (cl, cl), 0)
        si = lax.broadcasted_iota(jnp.int32, (cl, cl), 1)
        d_adt = jnp.dot(dcs_ref[0:hpg, :], (li >= si).astype(F32), precision=lax.Precision.HIGHEST,
                        preferred_element_type=F32)
        ddt = d_adt * a + dcs_ref[hpg:2 * hpg, :]
        ddt_raw = ddt * jax.nn.sigmoid(dtr_ref[...] + bias_ref[...])
        ddt_ref[...] = ddt_raw
        acc_ref[0:hpg, :] += d_adt * dt
        acc_ref[hpg:2 * hpg, :] += ddt_raw

    rc = lambda c: nc - 1 - c
    vec = lambda n: pl.BlockSpec((None, hpg, n), lambda gi, c: (gi, 0, 0))
    x_spec = pl.BlockSpec((cl, gw), lambda gi, c: (rc(c), gi))
    b_spec = pl.BlockSpec((cl, SSM_STATE), lambda gi, c: (rc(c), bo + gi))
    c_spec = pl.BlockSpec((cl, SSM_STATE), lambda gi, c: (rc(c), bo + ng + gi))
    dt_spec = pl.BlockSpec((None, hpg, cl), lambda gi, c: (gi, 0, rc(c)))
    return pl.pallas_call(
        body, name="ssd_bwd",
        out_shape=(jax.ShapeDtypeStruct((t, d_inner), F32), jax.ShapeDtypeStruct((t, ng * SSM_STATE), F32),
                   jax.ShapeDtypeStruct((t, ng * SSM_STATE), F32), jax.ShapeDtypeStruct(dtT.shape, F32),
                   jax.ShapeDtypeStruct((ng, 2 * hpg, cl), F32), jax.ShapeDtypeStruct((1, d_inner), F32)),
        grid=(ng, nc),
        in_specs=[x_spec, b_spec, c_spec, dt_spec, vec(1), vec(1),
                  pl.BlockSpec((1, gw), lambda gi, c: (0, gi)),
                  pl.BlockSpec((None, None, gw, SSM_STATE), lambda gi, c: (gi, rc(c), 0, 0)),
                  x_spec],
        out_specs=(x_spec, pl.BlockSpec((cl, SSM_STATE), lambda gi, c: (rc(c), gi)),
                   pl.BlockSpec((cl, SSM_STATE), lambda gi, c: (rc(c), gi)), dt_spec,
                   pl.BlockSpec((None, 2 * hpg, cl), lambda gi, c: (gi, 0, 0)),
                   pl.BlockSpec((1, gw), lambda gi, c: (0, gi))),
        scratch_shapes=[pltpu.VMEM((gw, SSM_STATE), F32), pltpu.VMEM((2 * hpg, cl), F32),
                        pltpu.VMEM((2 * hpg, cl), F32)],
        compiler_params=pltpu.CompilerParams(dimension_semantics=("parallel", "arbitrary")),
    )(pre, pre, pre, dtT, bias, alog, dskip_lanes, states, dy)


def _gate_norm_fwd(y, zx, norm_w, d_inner):
    t = y.shape[0]
    tr = _pick(t, (256, 128))
    gs = d_inner // SSM_GROUPS

    def body(y_ref, z_ref, w_ref, o_ref):
        for gi in range(SSM_GROUPS):
            cols = slice(gi * gs, (gi + 1) * gs)
            v = y_ref[:, cols] * _silu(z_ref[:, cols].astype(F32))
            r = lax.rsqrt(jnp.mean(v * v, axis=-1, keepdims=True) + NORM_EPS)
            o_ref[:, cols] = (v * r * w_ref[0:1, cols]).astype(BF16)

    spec = pl.BlockSpec((tr, d_inner), lambda i: (i, 0))
    return pl.pallas_call(
        body, name="ssm_gate_norm_fwd", out_shape=jax.ShapeDtypeStruct((t, d_inner), BF16), grid=(t // tr,),
        in_specs=[spec, spec, pl.BlockSpec((1, d_inner), lambda i: (0, 0))], out_specs=spec,
        compiler_params=pltpu.CompilerParams(dimension_semantics=("parallel",)),
    )(y, zx, norm_w.reshape(1, -1))


def _gate_norm_bwd(y, zx, norm_w, dout, d_inner):
    t, width = zx.shape
    tr = _pick(t, (256, 128))
    gs = d_inner // SSM_GROUPS

    def body(y_ref, z_ref, w_ref, do_ref, dy_ref, dz_ref, dw_ref):
        @pl.when(pl.program_id(0) == 0)
        def _():
            dw_ref[...] = jnp.zeros_like(dw_ref)

        for gi in range(SSM_GROUPS):
            cols = slice(gi * gs, (gi + 1) * gs)
            yv, zv = y_ref[:, cols], z_ref[:, cols].astype(F32)
            sz = _silu(zv)
            v = yv * sz
            r = lax.rsqrt(jnp.mean(v * v, axis=-1, keepdims=True) + NORM_EPS)
            vhat = v * r
            dn = do_ref[:, cols].astype(F32)
            dw_ref[0:1, cols] += jnp.sum(dn * vhat, axis=0, keepdims=True)
            dvh = dn * w_ref[0:1, cols]
            dv = r * (dvh - vhat * jnp.mean(dvh * vhat, axis=-1, keepdims=True))
            dy_ref[:, cols] = dv * sz
            dz_ref[:, cols] = (dv * yv * _silu_grad(zv)).astype(dz_ref.dtype)

    spec = pl.BlockSpec((tr, d_inner), lambda i: (i, 0))
    wspec = pl.BlockSpec((1, d_inner), lambda i: (0, 0))
    return pl.pallas_call(
        body, name="ssm_gate_norm_bwd",
        out_shape=(jax.ShapeDtypeStruct((t, d_inner), F32), jax.ShapeDtypeStruct((t, width), zx.dtype),
                   jax.ShapeDtypeStruct((1, d_inner), F32)),
        grid=(t // tr,),
        in_specs=[spec, spec, wspec, spec], out_specs=(spec, spec, wspec),
        compiler_params=pltpu.CompilerParams(dimension_semantics=("arbitrary",)),
    )(y, zx, norm_w.reshape(1, -1), dout)


def _ssm_small(dt_raw, dt_bias, a_log, d_skip):
    heads = dt_raw.shape[1]
    hpg = heads // SSM_GROUPS
    dtT = dt_raw.T.reshape(SSM_GROUPS, hpg, -1)
    return (dtT, dt_bias.reshape(SSM_GROUPS, hpg, 1), a_log.reshape(SSM_GROUPS, hpg, 1),
            jnp.repeat(d_skip, SSM_HEAD_DIM).reshape(1, -1))


def _ssm_core_fwd(zx, dt_raw, conv_w, conv_b, dt_bias, a_log, d_skip, norm_w):
    d_inner = norm_w.shape[0]
    pre = _conv_fwd(zx, conv_w, conv_b, d_inner)
    dtT, bias, alog, dsk = _ssm_small(dt_raw, dt_bias, a_log, d_skip)
    y, states = _ssd_fwd(pre, dtT, bias, alog, dsk, d_inner)
    out = _gate_norm_fwd(y, zx, norm_w, d_inner)
    return out, (zx, dt_raw, conv_w, dt_bias, a_log, d_skip, norm_w, pre, y, states)


def _ssm_core_bwd(res, dout):
    zx, dt_raw, conv_w, dt_bias, a_log, d_skip, norm_w, pre, y, states = res
    d_inner = norm_w.shape[0]
    heads = dt_raw.shape[1]
    dy, dzx, dnorm = _gate_norm_bwd(y, zx, norm_w, dout, d_inner)
    dtT, bias, alog, dsk = _ssm_small(dt_raw, dt_bias, a_log, d_skip)
    dx, db, dc, ddtT, acc, dsk_l = _ssd_bwd(pre, dtT, bias, alog, dsk, states, dy, d_inner)
    dws, dbs, col0 = [], [], 0
    for part in (dx, db, dc):
        dzx, dw_part, db_part = _conv_bwd(zx, conv_w, part, dzx, d_inner, col0)
        dws.append(dw_part)
        dbs.append(db_part)
        col0 += part.shape[1]
    dconv_w, dconv_b = jnp.concatenate(dws, axis=1), jnp.concatenate(dbs, axis=1)
    d_dt_raw = ddtT.reshape(heads, -1).T
    hpg = heads // SSM_GROUPS
    da = acc[:, :hpg].sum(-1).reshape(heads)
    d_bias = acc[:, hpg:].sum(-1).reshape(heads)
    d_alog = da * (-jnp.exp(a_log))
    d_dskip = dsk_l.reshape(heads, SSM_HEAD_DIM).sum(-1)
    return dzx, d_dt_raw, dconv_w, dconv_b.reshape(-1), d_bias, d_alog, d_dskip, dnorm.reshape(-1)


def _rows_call(body, name, ins, outs, acc_outs=(), rows=256):
    t = max(a.shape[0] for a in ins)
    tr = _pick(t, (rows, 128, 64, 32, 16, 8))

    def spec(a):
        if a.shape[0] == t:
            return pl.BlockSpec((tr, a.shape[1]), lambda i: (i, 0))
        return pl.BlockSpec(a.shape, lambda i: (0, 0))

    return pl.pallas_call(
        body, name=name, out_shape=tuple(outs) + tuple(acc_outs), grid=(t // tr,),
        in_specs=[spec(a) for a in ins],
        out_specs=tuple(spec(a) for a in outs) + tuple(pl.BlockSpec(a.shape, lambda i: (0, 0)) for a in acc_outs),
        compiler_params=pltpu.CompilerParams(dimension_semantics=("arbitrary" if acc_outs else "parallel",)),
    )(*ins)


def _rms_fwd(x, gain, after=None):
    def body(x_ref, g_ref, *rest):
        v = x_ref[...]
        rest[-1][...] = (v * lax.rsqrt(jnp.mean(v * v, axis=-1, keepdims=True) + NORM_EPS) * g_ref[...]).astype(BF16)

    ins = [x, gain.reshape(1, -1)] + ([] if after is None else [after])
    (h,) = _rows_call(body, "rms_fwd", ins, [jax.ShapeDtypeStruct(x.shape, BF16)])
    return h


def _rms_bwd(x, gain, dh, dres, after):
    def body(x_ref, g_ref, dh_ref, dr_ref, *rest):
        dx_ref, dg_ref = rest[-2:]

        @pl.when(pl.program_id(0) == 0)
        def _():
            dg_ref[...] = jnp.zeros_like(dg_ref)

        v = x_ref[...]
        r = lax.rsqrt(jnp.mean(v * v, axis=-1, keepdims=True) + NORM_EPS)
        vhat = v * r
        d = dh_ref[...].astype(F32)
        dg_ref[...] += jnp.sum(d * vhat, axis=0, keepdims=True)
        dvh = d * g_ref[...]
        dx_ref[...] = dr_ref[...] + r * (dvh - vhat * jnp.mean(dvh * vhat, axis=-1, keepdims=True))

    ins = [x, gain.reshape(1, -1), dh, dres] + ([] if after is None else [after])
    dx, dg = _rows_call(body, "rms_bwd", ins, [jax.ShapeDtypeStruct(x.shape, F32)],
                        [jax.ShapeDtypeStruct((1, x.shape[1]), F32)])
    return dx, dg.reshape(gain.shape)


def _swiglu_fwd(gu):
    t, f2 = gu.shape
    f = f2 // 2

    def body(gu_ref, o_ref):
        o_ref[...] = (_silu(gu_ref[:, :f].astype(F32)) * gu_ref[:, f:].astype(F32)).astype(BF16)

    (act,) = _rows_call(body, "swiglu_fwd", [gu], [jax.ShapeDtypeStruct((t, f), BF16)])
    return act


def _swiglu_bwd(gu, dact):
    t, f2 = gu.shape
    f = f2 // 2

    def body(gu_ref, d_ref, o_ref):
        g, u, d = gu_ref[:, :f].astype(F32), gu_ref[:, f:].astype(F32), d_ref[...].astype(F32)
        o_ref[:, :f] = (d * u * _silu_grad(g)).astype(BF16)
        o_ref[:, f:] = (d * _silu(g)).astype(BF16)

    (dgu,) = _rows_call(body, "swiglu_bwd", [gu, dact], [jax.ShapeDtypeStruct((t, f2), BF16)])
    return dgu


def _ple_fwd(x, gl, ple):
    def body(x_ref, g_ref, p_ref, o_ref):
        o_ref[...] = x_ref[...] + jax.nn.sigmoid(g_ref[...]) * p_ref[...]

    (out,) = _rows_call(body, "ple_fwd", [x, gl, ple], [jax.ShapeDtypeStruct(x.shape, F32)])
    return out


def _ple_bwd(gl, ple, dout):
    def body(g_ref, p_ref, d_ref, dg_ref, dp_ref):
        s, d = jax.nn.sigmoid(g_ref[...]), d_ref[...]
        dg_ref[...] = (d * p_ref[...] * s * (1.0 - s)).astype(BF16)
        dp_ref[...] = (d * s).astype(BF16)

    shp = jax.ShapeDtypeStruct(gl.shape, BF16)
    return _rows_call(body, "ple_bwd", [gl, ple, dout], [shp, shp])


def _loss_fwd(y, target):
    inv = 1.0 / y.shape[1]

    def body(y_ref, t_ref, d_ref, l_ref):
        @pl.when(pl.program_id(0) == 0)
        def _():
            l_ref[...] = jnp.zeros_like(l_ref)

        e = y_ref[...] - t_ref[...]
        d_ref[...] = e * inv
        part = jnp.sum(jnp.sum(e * e, axis=1, keepdims=True), axis=0, keepdims=True) * (0.5 * inv)
        l_ref[...] += jnp.broadcast_to(part, l_ref.shape)

    dy, acc = _rows_call(body, "loss_fwd", [y, target], [jax.ShapeDtypeStruct(y.shape, F32)],
                         [jax.ShapeDtypeStruct((SUBLANES, LANES), F32)])
    return acc[0, 0], dy


def rmsnorm(x, gain):
    y = x * lax.rsqrt(jnp.mean(x * x, axis=-1, keepdims=True) + NORM_EPS)
    return y * gain


def causal_depthwise_conv(u, w, bias):
    k_width, chans = w.shape
    out = lax.conv_general_dilated(u, w[:, None, :], window_strides=(1,), padding=[(k_width - 1, 0)],
                                   dimension_numbers=("NWC", "WIO", "NWC"), feature_group_count=chans)
    return out + bias


def ssd_chunked(x, dt, a, bm, cm):
    b, t, heads, _ = x.shape
    nc, cl = t // SSD_CHUNK, SSD_CHUNK
    g, hg = SSM_GROUPS, heads // SSM_GROUPS
    xs = (x * dt[..., None]).reshape(b, nc, cl, g, hg, SSM_HEAD_DIM)
    a_dt = (dt * a).reshape(b, nc, cl, g, hg).transpose(0, 1, 3, 4, 2)
    a_cs = jnp.cumsum(a_dt, axis=-1)
    bc = bm.reshape(b, nc, cl, g, SSM_STATE)
    cc = cm.reshape(b, nc, cl, g, SSM_STATE)
    causal = jnp.tril(jnp.ones((cl, cl), dtype=bool))
    seg = a_cs[..., :, None] - a_cs[..., None, :]
    lmat = jnp.exp(jnp.where(causal, seg, -jnp.inf))
    cb = jnp.einsum("bclgn,bcsgn->bcgls", cc, bc)
    y_diag = jnp.einsum("bcgls,bcghls,bcsghp->bclghp", cb, lmat, xs)
    decay = jnp.exp(a_cs[..., -1:] - a_cs)
    states = jnp.einsum("bclgn,bcghl,bclghp->bcghpn", bc, decay, xs)
    chunk_decay = jnp.exp(a_cs[..., -1])

    def step(carry, inp):
        st, dec = inp
        return carry * dec[..., None, None] + st, carry

    init = jnp.zeros((b, g, hg, SSM_HEAD_DIM, SSM_STATE), F32)
    _, prev = lax.scan(step, init, (jnp.moveaxis(states, 1, 0), jnp.moveaxis(chunk_decay, 1, 0)))
    prev = jnp.moveaxis(prev, 0, 1)
    y_off = jnp.einsum("bclgn,bcghpn,bcghl->bclghp", cc, prev, jnp.exp(a_cs))
    return (y_diag + y_off).reshape(b, t, heads, SSM_HEAD_DIM)


def mamba2_mixer(h, wt_in, conv_w, conv_b, dt_bias, a_log, d_skip, norm_w, w_out):
    t, d_model = h.shape
    d_inner = 2 * d_model
    heads = d_inner // SSM_HEAD_DIM
    gn = SSM_GROUPS * SSM_STATE
    conv_dim = d_inner + 2 * gn
    zx = lin_t(h, wt_in[:d_inner + conv_dim])
    dt_raw = lin_t(h, wt_in[d_inner + conv_dim:])
    return lin(ssm_core(zx, dt_raw, conv_w, conv_b, dt_bias, a_log, d_skip, norm_w), w_out)


def ssm_core_jnp(zx, dt_raw, conv_w, conv_b, dt_bias, a_log, d_skip, norm_w):
    t = zx.shape[0]
    d_inner = norm_w.shape[0]
    heads = d_inner // SSM_HEAD_DIM
    gn = SSM_GROUPS * SSM_STATE
    z = zx[:, :d_inner]
    xbc = zx[:, d_inner:]
    xbc = jax.nn.silu(causal_depthwise_conv(xbc[None], conv_w, conv_b))[0]
    xs = xbc[:, :d_inner]
    bm = xbc[:, d_inner:d_inner + gn].reshape(1, t, SSM_GROUPS, SSM_STATE)
    cm = xbc[:, d_inner + gn:].reshape(1, t, SSM_GROUPS, SSM_STATE)
    dt = jax.nn.softplus(dt_raw + dt_bias)[None]
    a = -jnp.exp(a_log)
    xh = xs.reshape(1, t, heads, SSM_HEAD_DIM)
    y = ssd_chunked(xh, dt, a, bm, cm)
    y = y + xh * d_skip[:, None]
    y = y.reshape(t, d_inner) * jax.nn.silu(z)
    return rmsnorm(y.reshape(t, SSM_GROUPS, -1), norm_w.reshape(SSM_GROUPS, -1)).reshape(t, d_inner)


def alibi_slopes(n_heads):
    return 2.0 ** (-8.0 * jnp.arange(1, n_heads + 1, dtype=F32) / n_heads)


def dilated_group_attention(q, k, v, window, dilation, slopes):
    b, t, nh, e = q.shape
    span = window // dilation
    blk = span
    lu = t // dilation
    nb = -(-lu // blk)
    lp = nb * blk

    def to_blocks(arr):
        arr = arr.reshape(b, lu, dilation, nh, e)
        arr = jnp.pad(arr, ((0, 0), (0, lp - lu), (0, 0), (0, 0), (0, 0)))
        return arr.reshape(b, nb, blk, dilation, nh, e)

    qb, kb, vb = to_blocks(q), to_blocks(k), to_blocks(v)
    pad_prev = ((0, 0), (1, 0), (0, 0), (0, 0), (0, 0), (0, 0))
    kcat = jnp.concatenate([jnp.pad(kb, pad_prev)[:, :nb], kb], axis=2)
    vcat = jnp.concatenate([jnp.pad(vb, pad_prev)[:, :nb], vb], axis=2)
    scores = jnp.einsum("bnqrhe,bnkrhe->bnrhqk", qb, kcat) * (1.0 / math.sqrt(e))
    qi = jnp.arange(blk)[:, None]
    ki = jnp.arange(2 * blk)[None, :]
    dist = qi + blk - ki
    in_band = (dist >= 0) & (dist <= span)
    key_u = jnp.arange(nb)[:, None] * blk - blk + jnp.arange(2 * blk)[None, :]
    valid = in_band[None] & (key_u >= 0)[:, None, :]
    bias = -slopes[:, None, None] * (dilation * dist).astype(F32)[None]
    logits = jnp.where(valid[None, :, None, None], scores + bias[None, None, None], -jnp.inf)
    lse = jax.nn.logsumexp(logits, axis=-1)
    probs = jnp.exp(logits - lse[..., None])
    out = jnp.einsum("bnrhqk,bnkrhe->bnqrhe", probs, vcat)
    out = out.reshape(b, lp, dilation, nh, e)[:, :lu].reshape(b, t, nh, e)
    lse = lse.transpose(0, 1, 4, 2, 3).reshape(b, lp, dilation, nh)[:, :lu].reshape(b, t, nh)
    return out, lse


def dilated_attention_mixer(h, wt_qkv, q_gain, k_gain, w_o):
    t, d_model = h.shape
    heads = d_model // ATT_HEAD_DIM
    ng = len(DIL_PATTERNS)
    return lin(attention_core(lin_t(h, wt_qkv, BF16), q_gain, k_gain), w_o)


def attention_core_jnp(qkv, q_gain, k_gain):
    t = qkv.shape[0]
    ng = len(DIL_PATTERNS)
    heads = qkv.shape[1] // (3 * ng * ATT_HEAD_DIM)
    qkv = qkv.astype(F32).reshape(1, t, ng, 3, heads, ATT_HEAD_DIM)
    q = rmsnorm(qkv[:, :, :, 0], q_gain)
    k = rmsnorm(qkv[:, :, :, 1], k_gain)
    v = qkv[:, :, :, 2]
    slopes = alibi_slopes(heads)
    outs, lses = [], []
    for g, (window, dilation) in enumerate(DIL_PATTERNS):
        o_g, l_g = dilated_group_attention(q[:, :, g], k[:, :, g], v[:, :, g], window, dilation, slopes)
        outs.append(o_g)
        lses.append(l_g)
    alpha = jax.nn.softmax(jnp.stack(lses), axis=0)
    o = jnp.einsum("gbth,gbthe->bthe", alpha, jnp.stack(outs))
    return o.reshape(t, heads * ATT_HEAD_DIM)


def local_step(small, fetch, emit, x, p, target):
    depth = small['norm_mix'].shape[0]
    ssm_small = ('ssm_conv_w', 'ssm_conv_b', 'ssm_dt_bias', 'ssm_a_log', 'ssm_d_skip', 'ssm_norm_w')
    saved = []
    for i in range(depth):
        j = i // 2
        s = {'x': x}
        wm, token = fetch(('mix', i), x)
        h = s['h'] = _rms_fwd(x, small['norm_mix'][i], token)
        if i % 2 == 0:
            n_main = wm['ssm_w_in'].shape[0] - small['ssm_dt_bias'].shape[1]
            zx = _mm(h, wm['ssm_w_in'][:n_main], tb=True, out_dtype=BF16, name="ssm_in_fwd")
            more, token = fetch(('out', i), zx)
            wm = {**wm, **more}
            dt_raw = _mm(h, wm['ssm_w_in'][n_main:], tb=True, after=token, name="ssm_dt_fwd")
            y, s['mix'] = _ssm_core_fwd(zx, dt_raw, wm['ssm_conv_w'], *[small[n][j] for n in ssm_small[1:]])
            x = _mm(y, wm['ssm_w_out'], add=x, name="ssm_out_fwd")
        else:
            qkv = _mm(h, wm['att_w_qkv'], tb=True, out_dtype=BF16, name="att_qkv_fwd")
            y, s['mix'] = _attention_core_fwd(qkv, small['att_q_norm'][j], small['att_k_norm'][j])
            x = _mm(y, wm['att_w_o'], add=x, name="att_o_fwd")
        s['wm'], s['y'], s['x1'] = wm, y, x
        wf, token = fetch(('ffn', i), x)
        s['wf'] = wf
        h2 = s['h2'] = _rms_fwd(x, small['norm_ffn'][i], token)
        gu = s['gu'] = _mm(h2, wf['ffn_w_gu'], tb=True, out_dtype=BF16, name="ffn_gu_fwd")
        act = s['act'] = _swiglu_fwd(gu)
        x = s['x2'] = _mm(act, wf['ffn_w_down'], add=x, name="ffn_down_fwd")
        gl = s['gl'] = _mm(x, wf['ple_w_gate'], name="ple_gate_fwd")
        ple = s['ple'] = _mm(p[i], wf['ple_w_proj'], tb=True, name="ple_proj_fwd")
        x = _ple_fwd(x, gl, ple)
        saved.append(s)
    loss, dx = _loss_fwd(x, target)

    g = {n: [None] * small[n].shape[0] for n in small}
    for i in reversed(range(depth)):
        j = i // 2
        s = saved[i]
        wm, wf = s['wm'], s['wf']
        gf = {}
        dgl, dple = _ple_bwd(s['gl'], s['ple'], dx)
        gf['ple_w_proj'] = _mm(dple, p[i], ta=True, out_dtype=BF16, name="ple_proj_dw")
        gf['ple_w_gate'] = _mm(s['x2'], dgl, ta=True, out_dtype=BF16, name="ple_gate_dw")
        dx = _mm(dgl, wf['ple_w_gate'], tb=True, add=dx, name="ple_gate_da")
        dact = _mm(dx, wf['ffn_w_down'], tb=True, out_dtype=BF16, name="ffn_down_da")
        gf['ffn_w_down'] = _mm(s['act'], dx, ta=True, out_dtype=BF16, name="ffn_down_dw")
        dgu = _swiglu_bwd(s['gu'], dact)
        dh2 = _mm(dgu, wf['ffn_w_gu'], out_dtype=BF16, name="ffn_gu_da")
        gf['ffn_w_gu'] = _mm(dgu, s['h2'], ta=True, out_dtype=BF16, name="ffn_gu_dw")
        dx, g['norm_ffn'][i] = _rms_bwd(s['x1'], small['norm_ffn'][i], dh2, dx, emit(('ffn', i), gf))
        gm = {}
        if i % 2 == 0:
            n_main = wm['ssm_w_in'].shape[0] - small['ssm_dt_bias'].shape[1]
            dyn = _mm(dx, wm['ssm_w_out'], tb=True, out_dtype=BF16, name="ssm_out_da")
            gm['ssm_w_out'] = _mm(s['y'], dx, ta=True, out_dtype=BF16, name="ssm_out_dw")
            dzx, d_dt, *sg = _ssm_core_bwd(s['mix'], dyn)
            for n, v in zip(ssm_small, sg):
                g[n][j] = v
            dh = _mm(d_dt, wm['ssm_w_in'][n_main:], name="ssm_dt_da")
            dh = _mm(dzx, wm['ssm_w_in'][:n_main], add=dh, out_dtype=BF16, name="ssm_in_da")
            gm['ssm_w_in'] = jnp.concatenate([_mm(dzx, s['h'], ta=True, out_dtype=BF16, name="ssm_in_dw"),
                                              _mm(d_dt, s['h'], ta=True, out_dtype=BF16, name="ssm_dt_dw")], axis=0)
        else:
            do = _mm(dx, wm['att_w_o'], tb=True, name="att_o_da")
            gm['att_w_o'] = _mm(s['y'], dx, ta=True, out_dtype=BF16, name="att_o_dw")
            dqkv, g['att_q_norm'][j], g['att_k_norm'][j] = _attention_core_bwd(s['mix'], do)
            dh = _mm(dqkv, wm['att_w_qkv'], out_dtype=BF16, name="att_qkv_da")
            gm['att_w_qkv'] = _mm(dqkv, s['h'], ta=True, out_dtype=BF16, name="att_qkv_dw")
        dx, g['norm_mix'][i] = _rms_bwd(s['x'], small['norm_mix'][i], dh, dx, emit(('mix', i), gm))
    return loss, dx, {n: jnp.stack(v) for n, v in g.items()}


CHANNEL_WEIGHTS = ('ffn_w_gate', 'ffn_w_up', 'ffn_w_down', 'ple_w_proj', 'ple_w_gate')


def _stages(depth):
    gather, scatter = {}, {}
    for i in range(depth):
        j = i // 2
        if i % 2 == 0:
            gather['mix', i] = [('ssm_w_in', j), ('ssm_conv_w', j)]
            gather['out', i] = [('ssm_w_out', j)]
            scatter['mix', i] = [('ssm_w_in', j), ('ssm_w_out', j)]
        else:
            gather['mix', i] = scatter['mix', i] = [('att_w_qkv', j), ('att_w_o', j)]
        gather['ffn', i] = scatter['ffn', i] = [(n, i) for n in CHANNEL_WEIGHTS]
    return gather, scatter


def _pack_plan(shapes, width, members):
    plan, off = [], 0
    for name, lyr in members:
        _, r, c = shapes[name]
        if name in COL_SHARDED:
            r, c = c, r
        if name == 'ssm_conv_w':
            pr = -(-2 * r * c // width)
        else:
            assert (r * c) % width == 0, (name, r, c)
            pr = r * c // width
        plan.append((name, lyr, r, c, pr, off))
        off += _round_up(pr, BF16_ROWS)
    return plan, off


def _small_plan(shapes):
    plan, off = [], 0
    for name in SMALL:
        n = math.prod(shapes[name])
        plan.append((name, n, off))
        off += n
    return plan, _round_up(off, SUBLANES * LANES)


def kernel(x, p, norm_mix, norm_ffn, ssm_w_in, ssm_conv_w, ssm_conv_b, ssm_dt_bias, ssm_a_log, ssm_d_skip, ssm_norm_w, ssm_w_out, att_w_qkv, att_q_norm, att_k_norm, att_w_o, ffn_w_gate, ffn_w_up, ffn_w_down, ple_w_proj, ple_w_gate, loss_target, m_norm_mix, m_norm_ffn, m_ssm_w_in, m_ssm_conv_w, m_ssm_conv_b, m_ssm_dt_bias, m_ssm_a_log, m_ssm_d_skip, m_ssm_norm_w, m_ssm_w_out, m_att_w_qkv, m_att_q_norm, m_att_k_norm, m_att_w_o, m_ffn_w_gate, m_ffn_w_up, m_ffn_w_down, m_ple_w_proj, m_ple_w_gate, v_norm_mix, v_norm_ffn, v_ssm_w_in, v_ssm_conv_w, v_ssm_conv_b, v_ssm_dt_bias, v_ssm_a_log, v_ssm_d_skip, v_ssm_norm_w, v_ssm_w_out, v_att_w_qkv, v_att_q_norm, v_att_k_norm, v_att_w_o, v_ffn_w_gate, v_ffn_w_up, v_ffn_w_down, v_ple_w_proj, v_ple_w_gate):
    given = dict(locals())
    w_in = {n: given[n] for n in WEIGHTS}
    m_in = {n: given["m_" + n] for n in WEIGHTS}
    v_in = {n: given["v_" + n] for n in WEIGHTS}
    width = x.shape[-1]
    depth = norm_mix.shape[0]

    gather_members, scatter_members = _stages(depth)
    shapes = {n: w_in[n].shape for n in BIG + ('ssm_conv_w',)}
    plans = {key: _pack_plan(shapes, width, members)[0] for key, members in gather_members.items()}
    splans = {key: _pack_plan(shapes, width, members)[0] for key, members in scatter_members.items()}
    order = list(gather_members)

    def pack_weights(stage):
        pieces = []
        for name, layer, r, c, pr, off in plans[stage]:
            blk = w_in[name][layer]
            if name == 'ssm_conv_w':
                blk = lax.bitcast_convert_type(blk.reshape(-1), BF16).reshape(-1)
                blk = jnp.pad(blk, (0, pr * width - blk.shape[0]))
            elif name in COL_SHARDED:
                blk = blk.T
            blk = blk.astype(BF16).reshape(pr, width)
            pieces.append(jnp.pad(blk, ((0, _round_up(pr, BF16_ROWS) - pr), (0, 0))))
        return jnp.concatenate(pieces, axis=0)

    packed = [pack_weights(order[0])]
    pending = [exchange_start(packed[0], True, "gather_start_0")]
    packed += [pack_weights(stage) for stage in order[1:]]

    def fetch(stage, after):
        k = order.index(stage)
        handle, token = pending[k]
        land = exchange_wait(handle, [token] + packed[1:] if k == 0 else after, True, f"gather_wait_{k}")
        token = None
        if k + 1 < len(order):
            pending.append(exchange_start(packed[k + 1], True, f"gather_start_{k + 1}", land))
            token = pending[-1][1]
        got = {}
        for name, layer, r, c, pr, off in plans[stage]:
            piece = land[:, off:off + pr]
            if name == 'ssm_conv_w':
                taps, chans = w_in[name].shape[1:]
                bits = piece.reshape(N_DEV, -1)[:, :2 * taps * chans].reshape(N_DEV, taps * chans, 2)
                piece = lax.bitcast_convert_type(bits, F32).reshape(N_DEV, taps, chans)
                got[name] = piece.transpose(1, 0, 2).reshape(taps, N_DEV * chans)
            else:
                got[name] = piece.reshape(N_DEV * r, c)
        if 'ffn_w_gate' in got:
            got['ffn_w_gu'] = jnp.concatenate([got.pop('ffn_w_gate'), got.pop('ffn_w_up')], axis=0)
        return got, token

    scatters = {}

    def emit(stage, grads):
        grads = dict(grads)
        if 'ffn_w_gu' in grads:
            hidden = grads['ffn_w_gu'].shape[0] // 2
            grads['ffn_w_gate'], grads['ffn_w_up'] = grads['ffn_w_gu'][:hidden], grads['ffn_w_gu'][hidden:]
        pieces = []
        for name, layer, r, c, pr, off in splans[stage]:
            g = grads[name].reshape(N_DEV, pr, width)
            pieces.append(jnp.pad(g, ((0, 0), (0, _round_up(pr, BF16_ROWS) - pr), (0, 0))))
        scatters[stage], token = exchange_start(jnp.concatenate(pieces, axis=1), False,
                                                f"scatter_start_{len(scatters)}")
        return token

    small = {n: w_in[n] for n in SMALL}
    cs = w_in['ssm_conv_w'].shape[2]
    loss_local, gx, gw = local_step(small, fetch, emit, x[0], p[:, 0], loss_target[0])
    loss = lax.psum(loss_local, ("x", "y", "c"))

    parts = {}
    for k, (stage, handle) in enumerate(scatters.items()):
        received = exchange_wait(handle, gx, False, f"scatter_wait_{k}")
        gsum = sum_slots(received, f"sum_grads_{k}")
        for name, layer, r, c, pr, off in splans[stage]:
            g = gsum[off:off + pr].reshape(r, c)
            parts[name, layer] = g.T if name in COL_SHARDED else g
    grads = {n: jnp.stack([parts[n, layer] for layer in range(w_in[n].shape[0])]) for n in BIG}

    splan, stotal = _small_plan({n: gw[n].shape for n in SMALL})
    svec = jnp.concatenate([gw[n].reshape(-1) for n, _, _ in splan])
    svec = jnp.pad(svec, (0, stotal - svec.shape[0])).reshape(stotal // LANES, LANES)
    _, ssum = all_gather_sum_small(svec, "sum_small_grads")
    ssum = ssum.reshape(-1)
    for name, n, off in splan:
        grads[name] = ssum[off:off + n].reshape(gw[name].shape)
    me = _me()
    grads['ssm_conv_w'] = lax.dynamic_slice_in_dim(grads['ssm_conv_w'], me * cs, cs, axis=2)

    delta, new_m, new_v = {}, {}, {}
    for name in BIG:
        shp = w_in[name].shape
        flat = lambda a: a.reshape(-1, shp[-1])
        d, nm, nv = adamw(flat(w_in[name]), flat(grads[name]), flat(m_in[name]), flat(v_in[name]), "adamw_" + name)
        delta[name], new_m[name], new_v[name] = d.reshape(shp), nm.reshape(shp), nv.reshape(shp)
    splan2, stotal2 = _small_plan({n: w_in[n].shape for n in SMALL})

    def pack_small(src):
        vec = jnp.concatenate([src[n].reshape(-1) for n, _, _ in splan2])
        return jnp.pad(vec, (0, stotal2 - vec.shape[0]), constant_values=1.0).reshape(stotal2 // LANES, LANES)

    sd, snm, snv = adamw(pack_small(w_in), pack_small(grads), pack_small(m_in), pack_small(v_in), "adamw_small")
    for name, n, off in splan2:
        shp = w_in[name].shape
        delta[name] = sd.reshape(-1)[off:off + n].reshape(shp)
        new_m[name] = snm.reshape(-1)[off:off + n].reshape(shp)
        new_v[name] = snv.reshape(-1)[off:off + n].reshape(shp)

    return (loss, gx[None], *[grads[n] for n in WEIGHTS], *[delta[n] for n in WEIGHTS],
            *[new_m[n] for n in WEIGHTS], *[new_v[n] for n in WEIGHTS])
```

```python
import functools
import math

import jax
import jax.numpy as jnp
from jax import lax
from jax.experimental import pallas as pl
from jax.experimental.pallas import tpu as pltpu

F32 = jnp.float32
BF16 = jnp.bfloat16
N_DEV = 8
MESH = pl.DeviceIdType.MESH

SSM_HEAD_DIM = 64
SSM_GROUPS = 4
SSM_STATE = 128
CONV_WIDTH = 4
SSD_CHUNK = 128
ATT_HEAD_DIM = 64
DIL_PATTERNS = ((128, 1), (512, 4), (2048, 16))
NORM_EPS = 1e-6
ADAM_LR = 0.001
ADAM_B1 = 0.9
ADAM_B2 = 0.999
ADAM_EPS = 1e-08
ADAM_WD = 0.01
ADAM_STEP = 10

BF16_ROWS = 16
LANES = 128
SUBLANES = 8

WEIGHTS = ['norm_mix', 'norm_ffn', 'ssm_w_in', 'ssm_conv_w', 'ssm_conv_b', 'ssm_dt_bias', 'ssm_a_log', 'ssm_d_skip',
           'ssm_norm_w', 'ssm_w_out', 'att_w_qkv', 'att_q_norm', 'att_k_norm', 'att_w_o', 'ffn_w_gate', 'ffn_w_up',
           'ffn_w_down', 'ple_w_proj', 'ple_w_gate']
COL_SHARDED = ('ssm_w_in', 'att_w_qkv', 'ffn_w_gate', 'ffn_w_up', 'ple_w_proj')
ROW_SHARDED = ('ssm_w_out', 'att_w_o', 'ffn_w_down', 'ple_w_gate')
BIG = COL_SHARDED + ROW_SHARDED
SMALL = ('norm_mix', 'norm_ffn', 'ssm_conv_w', 'ssm_conv_b', 'ssm_dt_bias', 'ssm_a_log', 'ssm_d_skip', 'ssm_norm_w',
         'att_q_norm', 'att_k_norm')


def _pick(n, cands):
    for c in cands:
        if n % c == 0:
            return c
    return n


def _round_up(n, m):
    return -(-n // m) * m


MM_TILES = (1024, 1408, 512, 256, 128)
MM_VMEM_BYTES = 48 * 1024 * 1024


def _mm(a, b, *, ta=False, tb=False, out_dtype=F32, add=None, after=None, name):
    k_dim, m_dim = (a.shape if ta else a.shape[::-1])
    n_dim = b.shape[0] if tb else b.shape[1]
    assert (b.shape[1] if tb else b.shape[0]) == k_dim, (a.shape, b.shape, ta, tb)
    tm = _pick(m_dim, MM_TILES)
    tn = _pick(n_dim, MM_TILES)
    tk = _pick(k_dim, MM_TILES)
    nk = k_dim // tk
    a_spec = pl.BlockSpec((tk, tm), lambda i, j, k: (k, i)) if ta else pl.BlockSpec((tm, tk), lambda i, j, k: (i, k))
    b_spec = pl.BlockSpec((tn, tk), lambda i, j, k: (j, k)) if tb else pl.BlockSpec((tk, tn), lambda i, j, k: (k, j))
    o_spec = pl.BlockSpec((tm, tn), lambda i, j, k: (i, j))
    dims = (((0 if ta else 1,), (1 if tb else 0,)), ((), ()))
    has_add = add is not None
    n_in = 2 + has_add + (after is not None)

    def body(*refs):
        a_ref, b_ref = refs[:2]
        o_ref = refs[n_in]

        def dot():
            return lax.dot_general(a_ref[...].astype(BF16), b_ref[...].astype(BF16), dims,
                                   preferred_element_type=F32)

        def finish(acc):
            if has_add:
                acc = acc + refs[2][...].astype(F32)
            o_ref[...] = acc.astype(o_ref.dtype)

        if nk == 1:
            finish(dot())
            return
        acc_ref = refs[n_in + 1]
        k = pl.program_id(2)

        @pl.when(k == 0)
        def _():
            acc_ref[...] = dot()

        @pl.when((k > 0) & (k < nk - 1))
        def _():
            acc_ref[...] += dot()

        @pl.when(k == nk - 1)
        def _():
            finish(acc_ref[...] + dot())

    return pl.pallas_call(
        body, name=f"{name}_{m_dim}x{n_dim}x{k_dim}",
        out_shape=jax.ShapeDtypeStruct((m_dim, n_dim), out_dtype),
        grid=(m_dim // tm, n_dim // tn, nk),
        in_specs=[a_spec, b_spec] + ([o_spec] if has_add else []) + (
            [] if after is None else [pl.BlockSpec(memory_space=pl.ANY)]),
        out_specs=o_spec,
        scratch_shapes=[] if nk == 1 else [pltpu.VMEM((tm, tn), F32)],
        compiler_params=pltpu.CompilerParams(dimension_semantics=("parallel", "parallel", "arbitrary"),
                                             vmem_limit_bytes=MM_VMEM_BYTES),
    )(*((a, b) + ((add,) if has_add else ()) + (() if after is None else (after,))))


def _me():
    return 4 * lax.axis_index("x") + 2 * lax.axis_index("y") + lax.axis_index("c")


def _peer(j):
    x, y, c = lax.axis_index("x"), lax.axis_index("y"), lax.axis_index("c")
    px = 1 - x if j & 4 else x
    py = 1 - y if j & 2 else y
    pc = 1 - c if j & 1 else c
    return (px, py, pc), 4 * px + 2 * py + pc


def _exchange_body(src_of, dst_ref, send_sems, recv_sems, local_sem):
    me = _me()
    mine = pltpu.make_async_copy(src_of(me), dst_ref.at[me], local_sem)
    mine.start()
    sends = []
    for j in range(1, N_DEV):
        peer, pidx = _peer(j)
        cp = pltpu.make_async_remote_copy(src_ref=src_of(pidx), dst_ref=dst_ref.at[me], send_sem=send_sems.at[j - 1],
                                          recv_sem=recv_sems.at[j - 1], device_id=peer, device_id_type=MESH)
        cp.start()
        sends.append(cp)
    for j in range(1, N_DEV):
        peer, pidx = _peer(j)
        pltpu.make_async_remote_copy(src_ref=src_of(pidx), dst_ref=dst_ref.at[pidx], send_sem=send_sems.at[j - 1],
                                     recv_sem=recv_sems.at[j - 1], device_id=peer, device_id_type=MESH).wait_recv()
    for cp in sends:
        cp.wait_send()
    mine.wait()


_EXCHANGE_SCRATCH = [pltpu.SemaphoreType.DMA((N_DEV - 1,)), pltpu.SemaphoreType.DMA((N_DEV - 1,)),
                     pltpu.SemaphoreType.DMA]


def all_gather_hbm(shard, name):
    def body(x_ref, out_ref, send_sems, recv_sems, local_sem):
        _exchange_body(lambda k: x_ref, out_ref, send_sems, recv_sems, local_sem)

    return pl.pallas_call(
        body, name=name,
        out_shape=jax.ShapeDtypeStruct((N_DEV,) + shard.shape, shard.dtype),
        in_specs=[pl.BlockSpec(memory_space=pl.ANY)],
        out_specs=pl.BlockSpec(memory_space=pl.ANY),
        scratch_shapes=list(_EXCHANGE_SCRATCH),
    )(shard)


def all_to_all_hbm(slots, name):
    def body(x_ref, out_ref, send_sems, recv_sems, local_sem):
        _exchange_body(lambda k: x_ref.at[k], out_ref, send_sems, recv_sems, local_sem)

    return pl.pallas_call(
        body, name=name,
        out_shape=jax.ShapeDtypeStruct(slots.shape, slots.dtype),
        in_specs=[pl.BlockSpec(memory_space=pl.ANY)],
        out_specs=pl.BlockSpec(memory_space=pl.ANY),
        scratch_shapes=list(_EXCHANGE_SCRATCH),
    )(slots)


_HBM = pl.BlockSpec(memory_space=pltpu.HBM)
_SEM = pl.BlockSpec(memory_space=pltpu.SEMAPHORE)


def _split_copies(src_ref, gather, land_ref, send_sems, recv_sems):
    me = _me()
    pairs = []
    for j in range(1, N_DEV):
        peer, pidx = _peer(j)

        def make(slot, peer=peer, pidx=pidx, j=j):
            return pltpu.make_async_remote_copy(
                src_ref=src_ref if gather else src_ref.at[pidx], dst_ref=land_ref.at[slot],
                send_sem=send_sems.at[j - 1], recv_sem=recv_sems.at[j - 1], device_id=peer, device_id_type=MESH)

        pairs.append((make(me), make(pidx)))
    return pairs


def exchange_start(src, gather, name, after=None):
    land_shape = ((N_DEV,) + src.shape) if gather else src.shape
    has_after = after is not None

    def body(*refs):
        src_ref, land_ref = refs[:2]
        send_sems, recv_sems = refs[2 + has_after:4 + has_after]
        for send, _ in _split_copies(src_ref, gather, land_ref, send_sems, recv_sems):
            send.start()
        refs[-1][...] = jnp.zeros_like(refs[-1])

    sem = pltpu.SemaphoreType.DMA((N_DEV - 1,))
    send_sems, recv_sems, src_thru, land, token = pl.pallas_call(
        body, name=name,
        out_shape=(sem, sem, pltpu.HBM(src.shape, src.dtype), pltpu.HBM(land_shape, src.dtype),
                   jax.ShapeDtypeStruct((SUBLANES, LANES), F32)),
        in_specs=(_HBM, _HBM) + ((pl.BlockSpec(memory_space=pl.ANY),) if has_after else ()),
        out_specs=(_SEM, _SEM, _HBM, _HBM, pl.BlockSpec(memory_space=pltpu.VMEM)),
        input_output_aliases={0: 2, 1: 3},
        compiler_params=pltpu.CompilerParams(has_side_effects=pltpu.SideEffectType.DATAFLOW_SIDE_EFFECTING),
    )(pltpu.with_memory_space_constraint(src, pltpu.HBM),
      pltpu.with_memory_space_constraint(lax.empty(land_shape, src.dtype), pltpu.HBM),
      *((after,) if has_after else ()))
    return (send_sems, recv_sems, src_thru, land), token


def exchange_wait(handle, after, gather, name):
    send_sems, recv_sems, src_thru, land = handle
    after = tuple(after) if isinstance(after, (tuple, list)) else (after,)

    def body(src_ref, land_ref, send_sems, recv_sems, *rest):
        for _, arrival in _split_copies(src_ref, gather, land_ref, send_sems, recv_sems):
            arrival.wait_send()
            arrival.wait_recv()

    src_done, got = pl.pallas_call(
        body, name=name,
        out_shape=(pltpu.HBM(src_thru.shape, src_thru.dtype), pltpu.HBM(land.shape, land.dtype)),
        in_specs=(_HBM, _HBM, _SEM, _SEM) + (pl.BlockSpec(memory_space=pl.ANY),) * len(after), out_specs=(_HBM, _HBM),
        input_output_aliases={0: 0, 1: 1},
        compiler_params=pltpu.CompilerParams(has_side_effects=pltpu.SideEffectType.DATAFLOW_SIDE_EFFECTING),
    )(src_thru, land, send_sems, recv_sems, *after)
    mine = src_done if gather else lax.dynamic_index_in_dim(src_done, _me(), 0, keepdims=False)
    return lax.dynamic_update_index_in_dim(got, mine, _me(), 0)


def all_gather_sum_small(v, name):
    def body(x_ref, out_ref, sum_ref, send_sems, recv_sems, local_sem):
        _exchange_body(lambda k: x_ref, out_ref, send_sems, recv_sems, local_sem)
        acc = out_ref[0]
        for k in range(1, N_DEV):
            acc = acc + out_ref[k]
        sum_ref[...] = acc

    return pl.pallas_call(
        body, name=name,
        out_shape=(jax.ShapeDtypeStruct((N_DEV,) + v.shape, v.dtype), jax.ShapeDtypeStruct(v.shape, v.dtype)),
        in_specs=[pl.BlockSpec(memory_space=pltpu.VMEM)],
        out_specs=(pl.BlockSpec(memory_space=pltpu.VMEM), pl.BlockSpec(memory_space=pltpu.VMEM)),
        scratch_shapes=list(_EXCHANGE_SCRATCH),
    )(v)


def sum_slots(slots, name):
    _, p_dim, c_dim = slots.shape
    tp = next(tp for tp in range(512, 0, -BF16_ROWS) if p_dim % tp == 0)

    def body(x_ref, o_ref):
        acc = x_ref[0].astype(F32)
        for k in range(1, N_DEV):
            acc = acc + x_ref[k].astype(F32)
        o_ref[...] = acc

    return pl.pallas_call(
        body, name=name,
        out_shape=jax.ShapeDtypeStruct((p_dim, c_dim), F32),
        grid=(p_dim // tp,),
        in_specs=[pl.BlockSpec((N_DEV, tp, c_dim), lambda i: (0, i, 0))],
        out_specs=pl.BlockSpec((tp, c_dim), lambda i: (i, 0)),
        compiler_params=pltpu.CompilerParams(dimension_semantics=("parallel",)),
    )(slots)


def adamw(w, g, m, v, name):
    rows, cols = w.shape
    tr = _pick(rows, (256, 128, 64, 32, 16, 8))

    def body(w_ref, g_ref, m_ref, v_ref, d_ref, nm_ref, nv_ref):
        gv = g_ref[...]
        nm = ADAM_B1 * m_ref[...] + (1.0 - ADAM_B1) * gv
        nv = ADAM_B2 * v_ref[...] + (1.0 - ADAM_B2) * (gv * gv)
        m_hat = nm / (1.0 - ADAM_B1 ** ADAM_STEP)
        v_hat = nv / (1.0 - ADAM_B2 ** ADAM_STEP)
        d_ref[...] = -ADAM_LR * (m_hat / (jnp.sqrt(v_hat) + ADAM_EPS) + ADAM_WD * w_ref[...])
        nm_ref[...] = nm
        nv_ref[...] = nv

    spec = pl.BlockSpec((tr, cols), lambda i: (i, 0))
    shp = jax.ShapeDtypeStruct((rows, cols), F32)
    return pl.pallas_call(
        body, name=name, out_shape=(shp, shp, shp), grid=(rows // tr,),
        in_specs=[spec] * 4, out_specs=(spec,) * 3,
        compiler_params=pltpu.CompilerParams(dimension_semantics=("parallel",)),
    )(w, g, m, v)


ATT_BLK = 128
NEG = -1e30


def _head_sums(v):
    li = lax.broadcasted_iota(jnp.int32, (LANES, LANES), 0) // ATT_HEAD_DIM
    lj = lax.broadcasted_iota(jnp.int32, (LANES, LANES), 1) // ATT_HEAD_DIM
    ones = (li == lj).astype(BF16)
    hi = v.astype(BF16)
    lo = (v - hi.astype(F32)).astype(BF16)
    return jnp.dot(hi, ones, preferred_element_type=F32) + jnp.dot(lo, ones, preferred_element_type=F32)


def _head_col(v, hmask):
    return jnp.max(jnp.where(hmask, v, -jnp.inf), axis=-1, keepdims=True)


def _qk_norm(raw, gain2):
    rstd = lax.rsqrt(_head_sums(raw * raw) * (1.0 / ATT_HEAD_DIM) + NORM_EPS)
    xhat = raw * rstd
    return xhat * gain2, xhat, rstd


def _qk_norm_bwd(dn, xhat, rstd, gain2):
    dxh = dn * gain2
    return rstd * (dxh - xhat * (_head_sums(dxh * xhat) * (1.0 / ATT_HEAD_DIM))), dn * xhat


def _att_mask_bias(n, dilation):
    qi = lax.broadcasted_iota(jnp.int32, (ATT_BLK, 2 * ATT_BLK), 0)
    ki = lax.broadcasted_iota(jnp.int32, (ATT_BLK, 2 * ATT_BLK), 1)
    dist = qi + ATT_BLK - ki
    valid = (dist >= 0) & (dist <= ATT_BLK) & ((n > 0) | (ki >= ATT_BLK))
    return valid, (dilation * dist).astype(F32)


ATT_PAIRS = 4
RELAYOUT_ROWS = 512
RELAYOUT_COLS = 512


def _to_residues(x, dilation, col0=0, cols=None):
    t = x.shape[0]
    cols = x.shape[1] if cols is None else cols
    if dilation == 1 and col0 == 0 and cols == x.shape[1]:
        return x.reshape(1, t, cols)
    tr = _pick(t, (RELAYOUT_ROWS,))
    tc = _pick(cols, (RELAYOUT_COLS, 256, 128))
    per = tr // dilation
    assert tr % dilation == 0 and col0 % tc == 0

    def body(x_ref, o_ref, s_ref):
        for c in range(tc // LANES):
            lanes = slice(c * LANES, (c + 1) * LANES)
            s_ref[c] = x_ref[:, lanes].astype(F32)
            for r in range(dilation):
                o_ref[r, :, lanes] = s_ref[c, pl.ds(r, per, stride=dilation), :].astype(o_ref.dtype)

    return pl.pallas_call(
        body, name=f"to_residues_{dilation}", out_shape=jax.ShapeDtypeStruct((dilation, t // dilation, cols), x.dtype),
        grid=(t // tr, cols // tc),
        in_specs=[pl.BlockSpec((tr, tc), lambda i, j: (i, col0 // tc + j))],
        out_specs=pl.BlockSpec((dilation, per, tc), lambda i, j: (0, i, j)),
        scratch_shapes=[pltpu.VMEM((tc // LANES, tr, LANES), F32)],
        compiler_params=pltpu.CompilerParams(dimension_semantics=("parallel", "parallel")),
    )(x)


def _from_residues(y):
    dilation, lu, cols = y.shape
    t = dilation * lu
    if dilation == 1:
        return y.reshape(t, cols)
    tr = _pick(t, (RELAYOUT_ROWS,))
    tc = _pick(cols, (RELAYOUT_COLS, 256, 128))
    per = tr // dilation

    def body(y_ref, o_ref, s_ref):
        for c in range(tc // LANES):
            lanes = slice(c * LANES, (c + 1) * LANES)
            for r in range(dilation):
                s_ref[c, pl.ds(r, per, stride=dilation), :] = y_ref[r, :, lanes].astype(F32)
            o_ref[:, lanes] = s_ref[c].astype(o_ref.dtype)

    return pl.pallas_call(
        body, name=f"from_residues_{dilation}", out_shape=jax.ShapeDtypeStruct((t, cols), y.dtype),
        grid=(t // tr, cols // tc),
        in_specs=[pl.BlockSpec((dilation, per, tc), lambda i, j: (0, i, j))],
        out_specs=pl.BlockSpec((tr, tc), lambda i, j: (i, j)),
        scratch_shapes=[pltpu.VMEM((tc // LANES, tr, LANES), F32)],
        compiler_params=pltpu.CompilerParams(dimension_semantics=("parallel", "parallel")),
    )(y)


def _att_specs(base, hd, nb, pp):
    hpn = hd // LANES
    assert base % (pp * LANES) == 0 and hpn % pp == 0

    def spec(which, shift):
        def imap(r, hp, n):
            row = jnp.minimum(n, nb - 1) if shift == 0 else jnp.maximum(n - 1, 0)
            return (r, row, (base // LANES + which * hpn) // pp + hp)
        return pl.BlockSpec((None, ATT_BLK, pp * LANES), imap)

    return [spec(0, 0), spec(1, 1), spec(1, 0), spec(2, 1), spec(2, 0)]


DELTA_LANE = 64


def _att_pre(qkv, g, dilation, gq2, gk2):
    t, width = qkv.shape
    hd = width // (3 * len(DIL_PATTERNS))
    tr = _pick(t, (RELAYOUT_ROWS,))
    tc = _pick(hd, (RELAYOUT_COLS, 256, 128))
    per = tr // dilation
    assert tr % dilation == 0 and (g * 3 * hd) % tc == 0

    def body(x_ref, gq_ref, gk_ref, o_ref, s_ref):
        section = pl.program_id(1) * tc // hd
        gain = jnp.where(section == 0, gq_ref[0:1, :], gk_ref[0:1, :])
        for c in range(tc // LANES):
            lanes = slice(c * LANES, (c + 1) * LANES)
            raw = x_ref[:, lanes].astype(F32)

            @pl.when(section == 2)
            def _():
                s_ref[c] = raw

            @pl.when(section != 2)
            def _():
                s_ref[c] = _qk_norm(raw, gain)[0]

            for r in range(dilation):
                o_ref[r, :, lanes] = s_ref[c, pl.ds(r, per, stride=dilation), :].astype(o_ref.dtype)

    vec_spec = pl.BlockSpec((SUBLANES, LANES), lambda i, j: (0, 0))
    return pl.pallas_call(
        body, name=f"att_pre_g{g}", out_shape=jax.ShapeDtypeStruct((dilation, t // dilation, 3 * hd), qkv.dtype),
        grid=(t // tr, 3 * hd // tc),
        in_specs=[pl.BlockSpec((tr, tc), lambda i, j: (i, g * 3 * hd // tc + j)), vec_spec, vec_spec],
        out_specs=pl.BlockSpec((dilation, per, tc), lambda i, j: (0, i, j)),
        scratch_shapes=[pltpu.VMEM((tc // LANES, tr, LANES), F32)],
        compiler_params=pltpu.CompilerParams(dimension_semantics=("parallel", "parallel")),
    )(qkv, gq2, gk2)


def _att_group_fwd(qkv_r, hd, slopes, g):
    dilation, lu, _ = qkv_r.shape
    nb = lu // ATT_BLK
    assert nb * ATT_BLK == lu and hd % LANES == 0
    hpn = hd // LANES
    pp = math.gcd(ATT_PAIRS, hpn)
    scale = 1.0 / math.sqrt(ATT_HEAD_DIM)

    def body(q_ref, kp_ref, kc_ref, vp_ref, vc_ref, sl_ref, o_ref, l_ref):
        n = pl.program_id(2)
        lane = lax.broadcasted_iota(jnp.int32, (1, LANES), 1)
        first = (lane // ATT_HEAD_DIM) == 0
        valid, dist = _att_mask_bias(n, dilation)
        stats = jnp.zeros((ATT_BLK, LANES), F32)
        for pair in range(pp):
            cols = slice(pair * LANES, (pair + 1) * LANES)
            qn = q_ref[:, cols]
            kn16 = jnp.concatenate([kp_ref[:, cols], kc_ref[:, cols]], axis=0)
            v16 = jnp.concatenate([vp_ref[:, cols], vc_ref[:, cols]], axis=0)
            outs = []
            for hh in range(2):
                hmask = (lane // ATT_HEAD_DIM) == hh
                qh = jnp.where(hmask, qn, jnp.zeros_like(qn))
                s = lax.dot_general(qh, kn16, (((1,), (1,)), ((), ())), preferred_element_type=F32) * scale
                slope = _head_col(sl_ref[pair, 0:1, :], hmask)
                logits = jnp.where(valid, s - slope * dist, NEG)
                mx = jnp.max(logits, axis=-1, keepdims=True)
                pexp = jnp.exp(logits - mx)
                den = jnp.sum(pexp, axis=-1, keepdims=True)
                outs.append(jnp.dot(pexp.astype(BF16), v16, preferred_element_type=F32) / den)
                stats = jnp.where(lane == 2 * pair + hh, mx + jnp.log(den), stats)
            o_ref[:, cols] = jnp.where(first, outs[0], outs[1]).astype(BF16)
        l_ref[...] = stats

    out_spec = pl.BlockSpec((None, ATT_BLK, pp * LANES), lambda r, hp, n: (r, n, hp))
    stat_spec = pl.BlockSpec((None, ATT_BLK, LANES), lambda r, hp, n: (r, n, hp))
    o, lse = pl.pallas_call(
        body, name=f"att_fwd_g{g}",
        out_shape=(jax.ShapeDtypeStruct((dilation, lu, hd), BF16),
                   jax.ShapeDtypeStruct((dilation, lu, hpn // pp * LANES), F32)),
        grid=(dilation, hpn // pp, nb),
        in_specs=_att_specs(0, hd, nb, pp) + [pl.BlockSpec((pp, SUBLANES, LANES), lambda r, hp, n: (hp, 0, 0))],
        out_specs=(out_spec, stat_spec),
        compiler_params=pltpu.CompilerParams(dimension_semantics=("parallel", "parallel", "arbitrary")),
    )(*([qkv_r] * 5), slopes)
    return _from_residues(o), _from_residues(lse)


def _att_merge(outs, lses):
    t, hd = outs[0].shape
    sw = lses[0].shape[1]
    pp = hd // sw
    tr = _pick(t, (256, 128))
    ng = len(outs)

    def body(*refs):
        o_refs, l_refs, o16_ref, lt_ref = refs[:ng], refs[ng:2 * ng], refs[2 * ng], refs[2 * ng + 1]
        lane = lax.broadcasted_iota(jnp.int32, (1, LANES), 1)
        first = (lane // ATT_HEAD_DIM) == 0
        for blk in range(sw // LANES):
            scols = slice(blk * LANES, (blk + 1) * LANES)
            stats = jnp.zeros((tr, LANES), F32)
            for pair in range(pp):
                cols = slice((blk * pp + pair) * LANES, (blk * pp + pair + 1) * LANES)
                weights = []
                for hh in range(2):
                    pick = lane == 2 * pair + hh
                    ls = [_head_col(r[:, scols], pick) for r in l_refs]
                    mx = functools.reduce(jnp.maximum, ls)
                    es = [jnp.exp(l - mx) for l in ls]
                    den = functools.reduce(jnp.add, es)
                    weights.append([e / den for e in es])
                    stats = jnp.where(pick, mx + jnp.log(den), stats)
                acc = jnp.zeros((tr, LANES), F32)
                for gi in range(ng):
                    acc = acc + jnp.where(first, weights[0][gi], weights[1][gi]) * o_refs[gi][:, cols].astype(F32)
                o16_ref[:, cols] = acc.astype(BF16)
            lt_ref[:, scols] = stats

    spec = pl.BlockSpec((tr, hd), lambda i: (i, 0))
    sspec = pl.BlockSpec((tr, sw), lambda i: (i, 0))
    return pl.pallas_call(
        body, name="att_merge",
        out_shape=(jax.ShapeDtypeStruct((t, hd), BF16), jax.ShapeDtypeStruct((t, sw), F32)), grid=(t // tr,),
        in_specs=[spec] * ng + [sspec] * ng, out_specs=(spec, sspec),
        compiler_params=pltpu.CompilerParams(dimension_semantics=("parallel",)),
    )(*outs, *lses)


def _att_bwd_prep(do, o16, lse_tot):
    t, hd = do.shape
    sw = lse_tot.shape[1]
    pp = hd // sw
    tr = _pick(t, (256, 128))

    def body(do_ref, o_ref, l_ref, d16_ref, st_ref):
        lane = lax.broadcasted_iota(jnp.int32, (1, LANES), 1)
        d16_ref[...] = do_ref[...].astype(BF16)
        for blk in range(sw // LANES):
            scols = slice(blk * LANES, (blk + 1) * LANES)
            stats = l_ref[:, scols]
            for pair in range(pp):
                cols = slice((blk * pp + pair) * LANES, (blk * pp + pair + 1) * LANES)
                prod = do_ref[:, cols] * o_ref[:, cols].astype(F32)
                for hh in range(2):
                    hmask = (lane // ATT_HEAD_DIM) == hh
                    delta = jnp.sum(jnp.where(hmask, prod, 0.0), axis=-1, keepdims=True)
                    stats = jnp.where(lane == DELTA_LANE + 2 * pair + hh, delta, stats)
            st_ref[:, scols] = stats

    spec = pl.BlockSpec((tr, hd), lambda i: (i, 0))
    sspec = pl.BlockSpec((tr, sw), lambda i: (i, 0))
    return pl.pallas_call(
        body, name="att_bwd_prep",
        out_shape=(jax.ShapeDtypeStruct((t, hd), BF16), jax.ShapeDtypeStruct((t, sw), F32)), grid=(t // tr,),
        in_specs=[spec, spec, sspec], out_specs=(spec, sspec),
        compiler_params=pltpu.CompilerParams(dimension_semantics=("parallel",)),
    )(do, o16, lse_tot)


def _att_post(buf, parts, qkv, gq2, gk2, g):
    dilation, lu, hd = parts[0].shape
    t = dilation * lu
    tr = _pick(t, (RELAYOUT_ROWS,))
    per = tr // dilation

    def body(*refs):
        raw_ref, gq_ref, gk_ref = refs[3:6]
        o_ref, dgq_ref, dgk_ref, s_ref = refs[-4:]

        @pl.when(pl.program_id(0) == 0)
        def _():
            dgq_ref[...] = jnp.zeros_like(dgq_ref)
            dgk_ref[...] = jnp.zeros_like(dgk_ref)

        for sec, y_ref in enumerate(refs[:3]):
            gsum = jnp.zeros((1, LANES), F32)
            for c in range(hd // LANES):
                lanes = slice(c * LANES, (c + 1) * LANES)
                out_lanes = slice(sec * hd + c * LANES, sec * hd + (c + 1) * LANES)
                for r in range(dilation):
                    s_ref[pl.ds(r, per, stride=dilation), :] = y_ref[r, :, lanes].astype(F32)
                d = s_ref[...]
                if sec < 2:
                    gain = (gq_ref if sec == 0 else gk_ref)[0:1, :]
                    _, xhat, rstd = _qk_norm(raw_ref[:, out_lanes].astype(F32), gain)
                    d, part = _qk_norm_bwd(d, xhat, rstd, gain)
                    gsum = gsum + jnp.sum(part, axis=0, keepdims=True)
                o_ref[:, out_lanes] = d.astype(o_ref.dtype)
            if sec < 2:
                acc = dgq_ref if sec == 0 else dgk_ref
                acc[...] += jnp.broadcast_to(gsum, acc.shape)

    part_spec = pl.BlockSpec((dilation, per, hd), lambda i: (0, i, 0))
    slab_spec = pl.BlockSpec((tr, 3 * hd), lambda i: (i, g))
    vec_spec = pl.BlockSpec((SUBLANES, LANES), lambda i: (0, 0))
    vec_shape = jax.ShapeDtypeStruct((SUBLANES, LANES), F32)
    return pl.pallas_call(
        body, name=f"att_post_g{g}", out_shape=(jax.ShapeDtypeStruct(qkv.shape, qkv.dtype), vec_shape, vec_shape),
        grid=(t // tr,),
        in_specs=[part_spec] * 3 + [slab_spec, vec_spec, vec_spec] + (
            [] if buf is None else [pl.BlockSpec(memory_space=pl.ANY)]),
        out_specs=(slab_spec, vec_spec, vec_spec),
        scratch_shapes=[pltpu.VMEM((tr, LANES), F32)],
        input_output_aliases={} if buf is None else {6: 0},
        compiler_params=pltpu.CompilerParams(dimension_semantics=("arbitrary",)),
    )(*parts, qkv, gq2, gk2, *(() if buf is None else (buf,)))


def _att_group_bwd(qkv_r, hd, slopes, stats, do16, g):
    dilation, lu, _ = qkv_r.shape
    nb = lu // ATT_BLK
    hpn = hd // LANES
    pp = math.gcd(ATT_PAIRS, hpn)
    hbn = hpn // pp
    scale = 1.0 / math.sqrt(ATT_HEAD_DIM)

    def body(q_ref, kp_ref, kc_ref, vp_ref, vc_ref, sl_ref, st_ref, do_ref, dq_ref, dk_ref, dv_ref, ck_ref, cv_ref):
        n = pl.program_id(2)
        lane = lax.broadcasted_iota(jnp.int32, (1, LANES), 1)

        @pl.when(n == 0)
        def _():
            ck_ref[...] = jnp.zeros_like(ck_ref)
            cv_ref[...] = jnp.zeros_like(cv_ref)

        @pl.when(n < nb)
        def _():
            valid, dist = _att_mask_bias(n, dilation)
            stats = st_ref[...]
            for pair in range(pp):
                cols = slice(pair * LANES, (pair + 1) * LANES)
                qn = q_ref[:, cols]
                kn16 = jnp.concatenate([kp_ref[:, cols], kc_ref[:, cols]], axis=0)
                v16 = jnp.concatenate([vp_ref[:, cols], vc_ref[:, cols]], axis=0)
                dov = do_ref[:, cols]
                dq_acc = jnp.zeros((ATT_BLK, LANES), F32)
                dk_acc = jnp.zeros((2 * ATT_BLK, LANES), F32)
                dv_acc = jnp.zeros((2 * ATT_BLK, LANES), F32)
                for hh in range(2):
                    hmask = (lane // ATT_HEAD_DIM) == hh
                    qh = jnp.where(hmask, qn, jnp.zeros_like(qn))
                    doh = jnp.where(hmask, dov, jnp.zeros_like(dov))
                    s = lax.dot_general(qh, kn16, (((1,), (1,)), ((), ())), preferred_element_type=F32) * scale
                    slope = _head_col(sl_ref[pair, 0:1, :], hmask)
                    lse = _head_col(stats, lane == 2 * pair + hh)
                    delta = _head_col(stats, lane == DELTA_LANE + 2 * pair + hh)
                    pr = jnp.exp(jnp.where(valid, s - slope * dist - lse, NEG))
                    dp = lax.dot_general(doh, v16, (((1,), (1,)), ((), ())), preferred_element_type=F32)
                    ds = (pr * (dp - delta) * scale).astype(BF16)
                    dq_acc = dq_acc + jnp.where(hmask, jnp.dot(ds, kn16, preferred_element_type=F32), 0.0)
                    dk_acc = dk_acc + lax.dot_general(ds, qh, (((0,), (0,)), ((), ())), preferred_element_type=F32)
                    dv_acc = dv_acc + lax.dot_general(pr.astype(BF16), doh, (((0,), (0,)), ((), ())),
                                                      preferred_element_type=F32)
                dq_ref[:, cols] = dq_acc.astype(dq_ref.dtype)
                dk_ref[:, cols] = (ck_ref[:, cols] + dk_acc[:ATT_BLK]).astype(dk_ref.dtype)
                dv_ref[:, cols] = (cv_ref[:, cols] + dv_acc[:ATT_BLK]).astype(dv_ref.dtype)
                ck_ref[:, cols] = dk_acc[ATT_BLK:]
                cv_ref[:, cols] = dv_acc[ATT_BLK:]

        @pl.when(n == nb)
        def _():
            dk_ref[...] = ck_ref[...].astype(dk_ref.dtype)
            dv_ref[...] = cv_ref[...].astype(dv_ref.dtype)

    width = pp * LANES
    q_out = pl.BlockSpec((None, ATT_BLK, width), lambda r, hp, n: (r, jnp.minimum(n, nb - 1), hp))
    kv_out = pl.BlockSpec((None, ATT_BLK, width), lambda r, hp, n: (r, jnp.maximum(n - 1, 0), hp))
    st_spec = pl.BlockSpec((None, ATT_BLK, LANES), lambda r, hp, n: (r, jnp.minimum(n, nb - 1), hp))
    shp = jax.ShapeDtypeStruct((dilation, lu, hd), BF16)
    return pl.pallas_call(
        body, name=f"att_bwd_g{g}", out_shape=(shp, shp, shp), grid=(dilation, hbn, nb + 1),
        in_specs=_att_specs(0, hd, nb, pp) + [
            pl.BlockSpec((pp, SUBLANES, LANES), lambda r, hp, n: (hp, 0, 0)), st_spec, q_out],
        out_specs=(q_out, kv_out, kv_out),
        scratch_shapes=[pltpu.VMEM((ATT_BLK, width), F32), pltpu.VMEM((ATT_BLK, width), F32)],
        compiler_params=pltpu.CompilerParams(dimension_semantics=("parallel", "parallel", "arbitrary")),
    )(*([qkv_r] * 5), slopes, _to_residues(stats, dilation), _to_residues(do16, dilation))


def _att_consts(q_gain, k_gain, hd):
    heads = hd // ATT_HEAD_DIM
    gq2 = jnp.broadcast_to(jnp.tile(q_gain, 2)[None], (SUBLANES, LANES))
    gk2 = jnp.broadcast_to(jnp.tile(k_gain, 2)[None], (SUBLANES, LANES))
    sl = 2.0 ** (-8.0 * jnp.arange(1, heads + 1, dtype=F32) / heads)
    slopes = jnp.broadcast_to(jnp.repeat(sl, ATT_HEAD_DIM).reshape(hd // LANES, 1, LANES), (hd // LANES, SUBLANES, LANES))
    return gq2, gk2, slopes


def _attention_core_fwd(qkv, q_gain, k_gain):
    hd = qkv.shape[1] // (3 * len(DIL_PATTERNS))
    gq2, gk2, slopes = _att_consts(q_gain, k_gain, hd)
    outs, lses, views = [], [], []
    for g, (_, dilation) in enumerate(DIL_PATTERNS):
        qkv_r = _att_pre(qkv, g, dilation, gq2, gk2)
        o_g, l_g = _att_group_fwd(qkv_r, hd, slopes, g)
        outs.append(o_g)
        lses.append(l_g)
        views.append(qkv_r)
    o16, lse_tot = _att_merge(outs, lses)
    return o16, (qkv, views, q_gain, k_gain, o16, lse_tot)


def _attention_core_bwd(res, do):
    qkv, views, q_gain, k_gain, o16, lse_tot = res
    hd = o16.shape[1]
    gq2, gk2, slopes = _att_consts(q_gain, k_gain, hd)
    do16, stats = _att_bwd_prep(do, o16, lse_tot)
    dqkv, dgq, dgk = None, 0.0, 0.0
    for g, qkv_r in enumerate(views):
        parts = _att_group_bwd(qkv_r, hd, slopes, stats, do16, g)
        dqkv, a, b = _att_post(dqkv, parts, qkv, gq2, gk2, g)
        dgq = dgq + a[0].reshape(-1, ATT_HEAD_DIM).sum(0)
        dgk = dgk + b[0].reshape(-1, ATT_HEAD_DIM).sum(0)
    return dqkv, dgq, dgk


HALO = 8


def _silu(v):
    return v * jax.nn.sigmoid(v)


def _silu_grad(v):
    s = jax.nn.sigmoid(v)
    return s * (1.0 + v * (1.0 - s))


def _halo_rows(dtype):
    return BF16_ROWS if dtype == BF16 else HALO


def _conv_fwd(zx, conv_w, conv_b, d_inner):
    t = zx.shape[0]
    conv_dim = conv_w.shape[1]
    cb = _pick(d_inner, (1024, 512, 256, 128))
    assert conv_dim % cb == 0
    tr = _pick(t, (256, 128))
    off = d_inner // cb
    hx = _halo_rows(zx.dtype)

    def body(x_ref, h_ref, w_ref, b_ref, o_ref):
        i = pl.program_id(1)
        halo = jnp.where(i > 0, h_ref[...].astype(F32), 0.0)
        ext = jnp.concatenate([halo, x_ref[...].astype(F32)], axis=0)
        acc = jnp.broadcast_to(b_ref[...], (tr, cb))
        for k in range(CONV_WIDTH):
            s = CONV_WIDTH - 1 - k
            sh = ext if s == 0 else pltpu.roll(ext, shift=s, axis=0)
            acc = acc + w_ref[k:k + 1, :] * sh[hx:hx + tr]
        o_ref[...] = acc.astype(o_ref.dtype)

    return pl.pallas_call(
        body, name="ssm_conv_fwd", out_shape=jax.ShapeDtypeStruct((t, conv_dim), BF16),
        grid=(conv_dim // cb, t // tr),
        in_specs=[pl.BlockSpec((tr, cb), lambda j, i: (i, off + j)),
                  pl.BlockSpec((hx, cb), lambda j, i: (jnp.maximum(i * (tr // hx) - 1, 0), off + j)),
                  pl.BlockSpec((CONV_WIDTH, cb), lambda j, i: (0, j)),
                  pl.BlockSpec((1, cb), lambda j, i: (0, j))],
        out_specs=pl.BlockSpec((tr, cb), lambda j, i: (i, j)),
        compiler_params=pltpu.CompilerParams(dimension_semantics=("parallel", "parallel")),
    )(zx, zx, conv_w, conv_b.reshape(1, -1))


def _conv_bwd(zx, conv_w, dpre, dzx, d_inner, col0):
    t, width = zx.shape
    conv_dim = dpre.shape[1]
    cb = _pick(conv_dim, (1024, 512, 256, 128))
    assert (d_inner + col0) % cb == 0
    tr = _pick(t, (256, 128))
    off = (d_inner + col0) // cb
    woff = col0 // cb
    nr = t // tr
    hx = _halo_rows(zx.dtype)
    hd = _halo_rows(dpre.dtype)

    def body(x_ref, h_ref, w_ref, d_ref, dn_ref, dzx_in, dx_ref, dw_ref, db_ref):
        i = pl.program_id(1)

        @pl.when(i == 0)
        def _():
            dw_ref[...] = jnp.zeros_like(dw_ref)
            db_ref[...] = jnp.zeros_like(db_ref)

        halo = jnp.where(i > 0, h_ref[...].astype(F32), 0.0)
        ext = jnp.concatenate([halo, x_ref[...].astype(F32)], axis=0)
        d = d_ref[...].astype(F32)
        dext = jnp.concatenate([d, jnp.where(i < nr - 1, dn_ref[...].astype(F32), 0.0)], axis=0)
        dx = jnp.zeros((tr, cb), F32)
        for k in range(CONV_WIDTH):
            s = CONV_WIDTH - 1 - k
            fut = dext if s == 0 else pltpu.roll(dext, shift=tr + hd - s, axis=0)
            dx = dx + w_ref[k:k + 1, :] * fut[:tr]
            past = ext if s == 0 else pltpu.roll(ext, shift=s, axis=0)
            dw_ref[k:k + 1, :] += jnp.sum(d * past[hx:hx + tr], axis=0, keepdims=True)
        dx_ref[...] = dx.astype(dx_ref.dtype)
        db_ref[...] += jnp.sum(d, axis=0, keepdims=True)

    last_halo = t // hd - 1
    return pl.pallas_call(
        body, name=f"ssm_conv_bwd_{col0}",
        out_shape=(jax.ShapeDtypeStruct(dzx.shape, dzx.dtype), jax.ShapeDtypeStruct((CONV_WIDTH, conv_dim), F32),
                   jax.ShapeDtypeStruct((1, conv_dim), F32)),
        grid=(conv_dim // cb, nr),
        in_specs=[pl.BlockSpec((tr, cb), lambda j, i: (i, off + j)),
                  pl.BlockSpec((hx, cb), lambda j, i: (jnp.maximum(i * (tr // hx) - 1, 0), off + j)),
                  pl.BlockSpec((CONV_WIDTH, cb), lambda j, i: (0, woff + j)),
                  pl.BlockSpec((tr, cb), lambda j, i: (i, j)),
                  pl.BlockSpec((hd, cb), lambda j, i: (jnp.minimum((i + 1) * (tr // hd), last_halo), j)),
                  pl.BlockSpec(memory_space=pl.ANY)],
        out_specs=(pl.BlockSpec((tr, cb), lambda j, i: (i, off + j)),
                   pl.BlockSpec((CONV_WIDTH, cb), lambda j, i: (0, j)),
                   pl.BlockSpec((1, cb), lambda j, i: (0, j))),
        input_output_aliases={5: 0},
        compiler_params=pltpu.CompilerParams(dimension_semantics=("parallel", "arbitrary")),
    )(zx, zx, conv_w, dpre, dpre, dzx)


def _eye(n):
    return lax.broadcasted_iota(jnp.int32, (n, n), 0) == lax.broadcasted_iota(jnp.int32, (n, n), 1)


def _row_to_col(row):
    n = row.shape[1]
    return jnp.sum(jnp.where(_eye(n), row, 0.0), axis=1, keepdims=True)


def _col_to_row(col):
    n = col.shape[0]
    return jnp.sum(jnp.where(_eye(n), col, 0.0), axis=0, keepdims=True)


def _pair_lanes(c0, c1):
    lane = lax.broadcasted_iota(jnp.int32, (1, LANES), 1)
    return jnp.where(lane < SSM_HEAD_DIM, c0, c1)


def _ssd_chunk_common(pre_x_ref, pre_b_ref, pre_c_ref, dtr_ref, bias_ref, alog_ref, cs_ref):
    cl = SSD_CHUNK
    hpg = dtr_ref.shape[0]
    x = _silu(pre_x_ref[...].astype(F32))
    b16 = _silu(pre_b_ref[...].astype(F32)).astype(BF16)
    c16 = _silu(pre_c_ref[...].astype(F32)).astype(BF16)
    dt = jax.nn.softplus(dtr_ref[...] + bias_ref[...])
    a = -jnp.exp(alog_ref[...])
    li = lax.broadcasted_iota(jnp.int32, (cl, cl), 0)
    si = lax.broadcasted_iota(jnp.int32, (cl, cl), 1)
    upper = (li <= si).astype(F32)
    cs_ref[0:hpg, :] = jnp.dot(dt * a, upper, precision=lax.Precision.HIGHEST, preferred_element_type=F32)
    cs_ref[hpg:2 * hpg, :] = dt
    g = lax.dot_general(c16, b16, (((1,), (1,)), ((), ())), preferred_element_type=F32)
    return x, b16, c16, dt, a, g, li >= si


def _ssd_fwd(pre, dtT, bias, alog, dskip_lanes, d_inner):
    t = pre.shape[0]
    cl = SSD_CHUNK
    nc = t // cl
    ng = SSM_GROUPS
    hpg = dtT.shape[1]
    gw = hpg * SSM_HEAD_DIM
    assert d_inner == ng * gw and hpg % 2 == 0
    bo = d_inner // SSM_STATE

    def body(px_ref, pb_ref, pc_ref, dtr_ref, bias_ref, alog_ref, dsk_ref, y_ref, st_ref, s_ref, cs_ref):
        c = pl.program_id(1)

        @pl.when(c == 0)
        def _():
            s_ref[...] = jnp.zeros_like(s_ref)

        x, b16, c16, dt, a, g, causal = _ssd_chunk_common(px_ref, pb_ref, pc_ref, dtr_ref, bias_ref, alog_ref, cs_ref)
        st_ref[...] = s_ref[...]
        yoff = lax.dot_general(c16, s_ref[...].astype(BF16), (((1,), (1,)), ((), ())), preferred_element_type=F32)
        xde_parts = []
        for j in range(hpg // 2):
            cols = slice(j * LANES, (j + 1) * LANES)
            xp = x[:, cols]
            dcol, ecol, ocol, ms = [], [], [], []
            for hh in range(2):
                h = 2 * j + hh
                cs_row = cs_ref[h:h + 1, :]
                cs_col = _row_to_col(cs_row)
                dcol.append(_row_to_col(cs_ref[hpg + h:hpg + h + 1, :]))
                ecol.append(jnp.exp(cs_ref[h:h + 1, cl - 1:cl] - cs_col))
                ocol.append(jnp.exp(cs_col))
                lm = jnp.where(causal, jnp.exp(jnp.minimum(cs_col - cs_row, 0.0)), 0.0)
                ms.append((g * lm).astype(BF16))
            xd = xp * _pair_lanes(dcol[0], dcol[1])
            xd16 = xd.astype(BF16)
            yd = _pair_lanes(1.0, 0.0) * jnp.dot(ms[0], xd16, preferred_element_type=F32) \
                + _pair_lanes(0.0, 1.0) * jnp.dot(ms[1], xd16, preferred_element_type=F32)
            y_ref[:, cols] = yd + yoff[:, cols] * _pair_lanes(ocol[0], ocol[1]) + xp * dsk_ref[0:1, cols]
            xde_parts.append((xd * _pair_lanes(ecol[0], ecol[1])).astype(BF16))
        new = lax.dot_general(jnp.concatenate(xde_parts, axis=1), b16, (((0,), (0,)), ((), ())),
                              preferred_element_type=F32)
        for h in range(hpg):
            rows = slice(h * SSM_HEAD_DIM, (h + 1) * SSM_HEAD_DIM)
            s_ref[rows, :] = s_ref[rows, :] * jnp.exp(cs_ref[h:h + 1, cl - 1:cl]) + new[rows, :]

    vec = lambda n: pl.BlockSpec((None, hpg, n), lambda gi, c: (gi, 0, 0))
    return pl.pallas_call(
        body, name="ssd_fwd",
        out_shape=(jax.ShapeDtypeStruct((t, d_inner), F32), jax.ShapeDtypeStruct((ng, nc, gw, SSM_STATE), F32)),
        grid=(ng, nc),
        in_specs=[pl.BlockSpec((cl, gw), lambda gi, c: (c, gi)),
                  pl.BlockSpec((cl, SSM_STATE), lambda gi, c: (c, bo + gi)),
                  pl.BlockSpec((cl, SSM_STATE), lambda gi, c: (c, bo + ng + gi)),
                  pl.BlockSpec((None, hpg, cl), lambda gi, c: (gi, 0, c)),
                  vec(1), vec(1),
                  pl.BlockSpec((1, gw), lambda gi, c: (0, gi))],
        out_specs=(pl.BlockSpec((cl, gw), lambda gi, c: (c, gi)),
                   pl.BlockSpec((None, None, gw, SSM_STATE), lambda gi, c: (gi, c, 0, 0))),
        scratch_shapes=[pltpu.VMEM((gw, SSM_STATE), F32), pltpu.VMEM((2 * hpg, cl), F32)],
        compiler_params=pltpu.CompilerParams(dimension_semantics=("parallel", "arbitrary")),
    )(pre, pre, pre, dtT, bias, alog, dskip_lanes)


def _ssd_bwd(pre, dtT, bias, alog, dskip_lanes, states, dy, d_inner):
    t, conv_dim = pre.shape
    cl = SSD_CHUNK
    nc = t // cl
    ng = SSM_GROUPS
    hpg = dtT.shape[1]
    gw = hpg * SSM_HEAD_DIM
    bo = d_inner // SSM_STATE

    def body(px_ref, pb_ref, pc_ref, dtr_ref, bias_ref, alog_ref, dsk_ref, st_ref, dy_ref,
             dx_ref, db_ref, dc_ref, ddt_ref, acc_ref, dsk_out, ds_ref, cs_ref, dcs_ref):
        c = pl.program_id(1)

        @pl.when(c == 0)
        def _():
            ds_ref[...] = jnp.zeros_like(ds_ref)
            acc_ref[...] = jnp.zeros_like(acc_ref)
            dsk_out[...] = jnp.zeros_like(dsk_out)

        x, b16, c16, dt, a, g, causal = _ssd_chunk_common(px_ref, pb_ref, pc_ref, dtr_ref, bias_ref, alog_ref, cs_ref)
        s_prev = st_ref[...]
        s16 = s_prev.astype(BF16)
        ds = ds_ref[...]
        ds16 = ds.astype(BF16)
        dyv = dy_ref[...]
        yoff = lax.dot_general(c16, s16, (((1,), (1,)), ((), ())), preferred_element_type=F32)
        bds = lax.dot_general(b16, ds16, (((1,), (1,)), ((), ())), preferred_element_type=F32)
        dg = jnp.zeros((cl, cl), F32)
        xde_parts, dye_parts = [], []
        lane = lax.broadcasted_iota(jnp.int32, (1, LANES), 1)
        for j in range(hpg // 2):
            cols = slice(j * LANES, (j + 1) * LANES)
            xp, dyp = x[:, cols], dyv[:, cols]
            dcol, ecol, ocol, lms = [], [], [], []
            for hh in range(2):
                h = 2 * j + hh
                cs_row = cs_ref[h:h + 1, :]
                cs_col = _row_to_col(cs_row)
                dcol.append(_row_to_col(cs_ref[hpg + h:hpg + h + 1, :]))
                ecol.append(jnp.exp(cs_ref[h:h + 1, cl - 1:cl] - cs_col))
                ocol.append(jnp.exp(cs_col))
                lms.append(jnp.where(causal, jnp.exp(jnp.minimum(cs_col - cs_row, 0.0)), 0.0))
            dlanes, elanes, olanes = _pair_lanes(*dcol), _pair_lanes(*ecol), _pair_lanes(*ocol)
            xd = xp * dlanes
            xd16 = xd.astype(BF16)
            xde = xd * elanes
            yoffp = yoff[:, cols] * olanes
            bdsp = bds[:, cols]
            dxd = bdsp * elanes
            for hh in range(2):
                h = 2 * j + hh
                hmask = (lane // SSM_HEAD_DIM) == hh
                dyh16 = jnp.where(hmask, dyp, 0.0).astype(BF16)
                m = g * lms[hh]
                dm = lax.dot_general(dyh16, xd16, (((1,), (1,)), ((), ())), preferred_element_type=F32)
                w = dm * m
                dg = dg + dm * lms[hh]
                dxd = dxd + lax.dot_general(m.astype(BF16), dyh16, (((0,), (0,)), ((), ())),
                                            preferred_element_type=F32)
                term = jnp.sum(jnp.where(hmask, xde * bdsp, 0.0), axis=1, keepdims=True)
                dcs_col = (jnp.sum(w, axis=1, keepdims=True)
                           + jnp.sum(jnp.where(hmask, dyp * yoffp, 0.0), axis=1, keepdims=True) - term)
                rows = slice(h * SSM_HEAD_DIM, (h + 1) * SSM_HEAD_DIM)
                dec = jnp.exp(cs_ref[h:h + 1, cl - 1:cl])
                tail = jnp.sum(term, axis=0, keepdims=True) + dec * jnp.sum(
                    jnp.sum(s_prev[rows, :] * ds[rows, :], axis=1, keepdims=True), axis=0, keepdims=True)
                last = lax.broadcasted_iota(jnp.int32, (1, cl), 1) == cl - 1
                dcs_ref[h:h + 1, :] = _col_to_row(dcs_col) - jnp.sum(w, axis=0, keepdims=True) + jnp.where(last, tail, 0.0)
                dcs_ref[hpg + h:hpg + h + 1, :] = _col_to_row(
                    jnp.sum(jnp.where(hmask, dxd * xp, 0.0), axis=1, keepdims=True))
            dx_act = dxd * dlanes + dyp * dsk_ref[0:1, cols]
            dx_ref[:, cols] = (dx_act * _silu_grad(px_ref[:, cols].astype(F32))).astype(dx_ref.dtype)
            dsk_out[0:1, cols] += jnp.sum(dyp * xp, axis=0, keepdims=True)
            xde_parts.append(xde.astype(BF16))
            dye_parts.append((dyp * olanes).astype(BF16))
        xde16 = jnp.concatenate(xde_parts, axis=1)
        dye16 = jnp.concatenate(dye_parts, axis=1)
        dg16 = dg.astype(BF16)
        dc_act = jnp.dot(dg16, b16, preferred_element_type=F32) + jnp.dot(dye16, s16, preferred_element_type=F32)
        db_act = lax.dot_general(dg16, c16, (((0,), (0,)), ((), ())), preferred_element_type=F32) \
            + jnp.dot(xde16, ds16, preferred_element_type=F32)
        dc_ref[...] = (dc_act * _silu_grad(pc_ref[...].astype(F32))).astype(dc_ref.dtype)
        db_ref[...] = (db_act * _silu_grad(pb_ref[...].astype(F32))).astype(db_ref.dtype)
        ds_new = lax.dot_general(dye16, c16, (((0,), (0,)), ((), ())), preferred_element_type=F32)
        for h in range(hpg):
            rows = slice(h * SSM_HEAD_DIM, (h + 1) * SSM_HEAD_DIM)
            ds_ref[rows, :] = ds[rows, :] * jnp.exp(cs_ref[h:h + 1, cl - 1:cl]) + ds_new[rows, :]
        li = lax.broadcasted_iota(jnp.int32, (cl, cl), 0)
        si = lax.broadcasted_iota(jnp.int32, (cl, cl), 1)
        d_adt = jnp.dot(dcs_ref[0:hpg, :], (li >= si).astype(F32), precision=lax.Precision.HIGHEST,
                        preferred_element_type=F32)
        ddt = d_adt * a + dcs_ref[hpg:2 * hpg, :]
        ddt_raw = ddt * jax.nn.sigmoid(dtr_ref[...] + bias_ref[...])
        ddt_ref[...] = ddt_raw
        acc_ref[0:hpg, :] += d_adt * dt
        acc_ref[hpg:2 * hpg, :] += ddt_raw

    rc = lambda c: nc - 1 - c
    vec = lambda n: pl.BlockSpec((None, hpg, n), lambda gi, c: (gi, 0, 0))
    x_spec = pl.BlockSpec((cl, gw), lambda gi, c: (rc(c), gi))
    b_spec = pl.BlockSpec((cl, SSM_STATE), lambda gi, c: (rc(c), bo + gi))
    c_spec = pl.BlockSpec((cl, SSM_STATE), lambda gi, c: (rc(c), bo + ng + gi))
    dt_spec = pl.BlockSpec((None, hpg, cl), lambda gi, c: (gi, 0, rc(c)))
    return pl.pallas_call(
        body, name="ssd_bwd",
        out_shape=(jax.ShapeDtypeStruct((t, d_inner), BF16), jax.ShapeDtypeStruct((t, ng * SSM_STATE), BF16),
                   jax.ShapeDtypeStruct((t, ng * SSM_STATE), BF16), jax.ShapeDtypeStruct(dtT.shape, F32),
                   jax.ShapeDtypeStruct((ng, 2 * hpg, cl), F32), jax.ShapeDtypeStruct((1, d_inner), F32)),
        grid=(ng, nc),
        in_specs=[x_spec, b_spec, c_spec, dt_spec, vec(1), vec(1),
                  pl.BlockSpec((1, gw), lambda gi, c: (0, gi)),
                  pl.BlockSpec((None, None, gw, SSM_STATE), lambda gi, c: (gi, rc(c), 0, 0)),
                  x_spec],
        out_specs=(x_spec, pl.BlockSpec((cl, SSM_STATE), lambda gi, c: (rc(c), gi)),
                   pl.BlockSpec((cl, SSM_STATE), lambda gi, c: (rc(c), gi)), dt_spec,
                   pl.BlockSpec((None, 2 * hpg, cl), lambda gi, c: (gi, 0, 0)),
                   pl.BlockSpec((1, gw), lambda gi, c: (0, gi))),
        scratch_shapes=[pltpu.VMEM((gw, SSM_STATE), F32), pltpu.VMEM((2 * hpg, cl), F32),
                        pltpu.VMEM((2 * hpg, cl), F32)],
        compiler_params=pltpu.CompilerParams(dimension_semantics=("parallel", "arbitrary")),
    )(pre, pre, pre, dtT, bias, alog, dskip_lanes, states, dy)


def _gate_norm_fwd(y, zx, norm_w, d_inner):
    t = y.shape[0]
    tr = _pick(t, (256, 128))
    gs = d_inner // SSM_GROUPS

    def body(y_ref, z_ref, w_ref, o_ref):
        for gi in range(SSM_GROUPS):
            cols = slice(gi * gs, (gi + 1) * gs)
            v = y_ref[:, cols] * _silu(z_ref[:, cols].astype(F32))
            r = lax.rsqrt(jnp.mean(v * v, axis=-1, keepdims=True) + NORM_EPS)
            o_ref[:, cols] = (v * r * w_ref[0:1, cols]).astype(BF16)

    spec = pl.BlockSpec((tr, d_inner), lambda i: (i, 0))
    return pl.pallas_call(
        body, name="ssm_gate_norm_fwd", out_shape=jax.ShapeDtypeStruct((t, d_inner), BF16), grid=(t // tr,),
        in_specs=[spec, spec, pl.BlockSpec((1, d_inner), lambda i: (0, 0))], out_specs=spec,
        compiler_params=pltpu.CompilerParams(dimension_semantics=("parallel",)),
    )(y, zx, norm_w.reshape(1, -1))


def _gate_norm_bwd(y, zx, norm_w, dout, d_inner):
    t, width = zx.shape
    tr = _pick(t, (256, 128))
    gs = d_inner // SSM_GROUPS

    def body(y_ref, z_ref, w_ref, do_ref, dy_ref, dz_ref, dw_ref):
        @pl.when(pl.program_id(0) == 0)
        def _():
            dw_ref[...] = jnp.zeros_like(dw_ref)

        for gi in range(SSM_GROUPS):
            cols = slice(gi * gs, (gi + 1) * gs)
            yv, zv = y_ref[:, cols], z_ref[:, cols].astype(F32)
            sz = _silu(zv)
            v = yv * sz
            r = lax.rsqrt(jnp.mean(v * v, axis=-1, keepdims=True) + NORM_EPS)
            vhat = v * r
            dn = do_ref[:, cols].astype(F32)
            dw_ref[0:1, cols] += jnp.sum(dn * vhat, axis=0, keepdims=True)
            dvh = dn * w_ref[0:1, cols]
            dv = r * (dvh - vhat * jnp.mean(dvh * vhat, axis=-1, keepdims=True))
            dy_ref[:, cols] = dv * sz
            dz_ref[:, cols] = (dv * yv * _silu_grad(zv)).astype(dz_ref.dtype)

    spec = pl.BlockSpec((tr, d_inner), lambda i: (i, 0))
    wspec = pl.BlockSpec((1, d_inner), lambda i: (0, 0))
    return pl.pallas_call(
        body, name="ssm_gate_norm_bwd",
        out_shape=(jax.ShapeDtypeStruct((t, d_inner), F32), jax.ShapeDtypeStruct((t, width), zx.dtype),
                   jax.ShapeDtypeStruct((1, d_inner), F32)),
        grid=(t // tr,),
        in_specs=[spec, spec, wspec, spec], out_specs=(spec, spec, wspec),
        compiler_params=pltpu.CompilerParams(dimension_semantics=("arbitrary",)),
    )(y, zx, norm_w.reshape(1, -1), dout)


def _ssm_small(dt_raw, dt_bias, a_log, d_skip):
    heads = dt_raw.shape[1]
    hpg = heads // SSM_GROUPS
    dtT = dt_raw.T.reshape(SSM_GROUPS, hpg, -1)
    return (dtT, dt_bias.reshape(SSM_GROUPS, hpg, 1), a_log.reshape(SSM_GROUPS, hpg, 1),
            jnp.repeat(d_skip, SSM_HEAD_DIM).reshape(1, -1))


def _ssm_core_fwd(zx, dt_raw, conv_w, conv_b, dt_bias, a_log, d_skip, norm_w):
    d_inner = norm_w.shape[0]
    pre = _conv_fwd(zx, conv_w, conv_b, d_inner)
    dtT, bias, alog, dsk = _ssm_small(dt_raw, dt_bias, a_log, d_skip)
    y, states = _ssd_fwd(pre, dtT, bias, alog, dsk, d_inner)
    out = _gate_norm_fwd(y, zx, norm_w, d_inner)
    return out, (zx, dt_raw, conv_w, dt_bias, a_log, d_skip, norm_w, pre, y, states)


def _ssm_core_bwd(res, dout):
    zx, dt_raw, conv_w, dt_bias, a_log, d_skip, norm_w, pre, y, states = res
    d_inner = norm_w.shape[0]
    heads = dt_raw.shape[1]
    dy, dzx, dnorm = _gate_norm_bwd(y, zx, norm_w, dout, d_inner)
    dtT, bias, alog, dsk = _ssm_small(dt_raw, dt_bias, a_log, d_skip)
    dx, db, dc, ddtT, acc, dsk_l = _ssd_bwd(pre, dtT, bias, alog, dsk, states, dy, d_inner)
    dws, dbs, col0 = [], [], 0
    for part in (dx, db, dc):
        dzx, dw_part, db_part = _conv_bwd(zx, conv_w, part, dzx, d_inner, col0)
        dws.append(dw_part)
        dbs.append(db_part)
        col0 += part.shape[1]
    dconv_w, dconv_b = jnp.concatenate(dws, axis=1), jnp.concatenate(dbs, axis=1)
    d_dt_raw = ddtT.reshape(heads, -1).T
    hpg = heads // SSM_GROUPS
    da = acc[:, :hpg].sum(-1).reshape(heads)
    d_bias = acc[:, hpg:].sum(-1).reshape(heads)
    d_alog = da * (-jnp.exp(a_log))
    d_dskip = dsk_l.reshape(heads, SSM_HEAD_DIM).sum(-1)
    return dzx, d_dt_raw, dconv_w, dconv_b.reshape(-1), d_bias, d_alog, d_dskip, dnorm.reshape(-1)


def _rows_call(body, name, ins, outs, acc_outs=(), rows=256):
    t = max(a.shape[0] for a in ins)
    tr = _pick(t, (rows, 128, 64, 32, 16, 8))

    def spec(a):
        if a.shape[0] == t:
            return pl.BlockSpec((tr, a.shape[1]), lambda i: (i, 0))
        return pl.BlockSpec(a.shape, lambda i: (0, 0))

    return pl.pallas_call(
        body, name=name, out_shape=tuple(outs) + tuple(acc_outs), grid=(t // tr,),
        in_specs=[spec(a) for a in ins],
        out_specs=tuple(spec(a) for a in outs) + tuple(pl.BlockSpec(a.shape, lambda i: (0, 0)) for a in acc_outs),
        compiler_params=pltpu.CompilerParams(dimension_semantics=("arbitrary" if acc_outs else "parallel",)),
    )(*ins)


def _rms_fwd(x, gain, after=None):
    def body(x_ref, g_ref, *rest):
        v = x_ref[...]
        rest[-1][...] = (v * lax.rsqrt(jnp.mean(v * v, axis=-1, keepdims=True) + NORM_EPS) * g_ref[...]).astype(BF16)

    ins = [x, gain.reshape(1, -1)] + ([] if after is None else [after])
    (h,) = _rows_call(body, "rms_fwd", ins, [jax.ShapeDtypeStruct(x.shape, BF16)])
    return h


def _rms_bwd(x, gain, dh, dres, after):
    def body(x_ref, g_ref, dh_ref, dr_ref, *rest):
        dx_ref, dg_ref = rest[-2:]

        @pl.when(pl.program_id(0) == 0)
        def _():
            dg_ref[...] = jnp.zeros_like(dg_ref)

        v = x_ref[...]
        r = lax.rsqrt(jnp.mean(v * v, axis=-1, keepdims=True) + NORM_EPS)
        vhat = v * r
        d = dh_ref[...].astype(F32)
        dg_ref[...] += jnp.sum(d * vhat, axis=0, keepdims=True)
        dvh = d * g_ref[...]
        dx_ref[...] = dr_ref[...] + r * (dvh - vhat * jnp.mean(dvh * vhat, axis=-1, keepdims=True))

    ins = [x, gain.reshape(1, -1), dh, dres] + ([] if after is None else [after])
    dx, dg = _rows_call(body, "rms_bwd", ins, [jax.ShapeDtypeStruct(x.shape, F32)],
                        [jax.ShapeDtypeStruct((1, x.shape[1]), F32)])
    return dx, dg.reshape(gain.shape)


def _swiglu_fwd(gu):
    t, f2 = gu.shape
    f = f2 // 2

    def body(gu_ref, o_ref):
        o_ref[...] = (_silu(gu_ref[:, :f].astype(F32)) * gu_ref[:, f:].astype(F32)).astype(BF16)

    (act,) = _rows_call(body, "swiglu_fwd", [gu], [jax.ShapeDtypeStruct((t, f), BF16)])
    return act


def _swiglu_bwd(gu, dact):
    t, f2 = gu.shape
    f = f2 // 2

    def body(gu_ref, d_ref, o_ref):
        g, u, d = gu_ref[:, :f].astype(F32), gu_ref[:, f:].astype(F32), d_ref[...].astype(F32)
        o_ref[:, :f] = (d * u * _silu_grad(g)).astype(BF16)
        o_ref[:, f:] = (d * _silu(g)).astype(BF16)

    (dgu,) = _rows_call(body, "swiglu_bwd", [gu, dact], [jax.ShapeDtypeStruct((t, f2), BF16)])
    return dgu


def _ple_fwd(x, gl, ple):
    def body(x_ref, g_ref, p_ref, o_ref):
        o_ref[...] = x_ref[...] + jax.nn.sigmoid(g_ref[...].astype(F32)) * p_ref[...].astype(F32)

    (out,) = _rows_call(body, "ple_fwd", [x, gl, ple], [jax.ShapeDtypeStruct(x.shape, F32)])
    return out


def _ple_bwd(gl, ple, dout):
    def body(g_ref, p_ref, d_ref, dg_ref, dp_ref):
        s, d = jax.nn.sigmoid(g_ref[...].astype(F32)), d_ref[...]
        dg_ref[...] = (d * p_ref[...].astype(F32) * s * (1.0 - s)).astype(BF16)
        dp_ref[...] = (d * s).astype(BF16)

    shp = jax.ShapeDtypeStruct(gl.shape, BF16)
    return _rows_call(body, "ple_bwd", [gl, ple, dout], [shp, shp])


def _loss_fwd(y, target):
    inv = 1.0 / y.shape[1]

    def body(y_ref, t_ref, d_ref, l_ref):
        @pl.when(pl.program_id(0) == 0)
        def _():
            l_ref[...] = jnp.zeros_like(l_ref)

        e = y_ref[...] - t_ref[...]
        d_ref[...] = e * inv
        part = jnp.sum(jnp.sum(e * e, axis=1, keepdims=True), axis=0, keepdims=True) * (0.5 * inv)
        l_ref[...] += jnp.broadcast_to(part, l_ref.shape)

    dy, acc = _rows_call(body, "loss_fwd", [y, target], [jax.ShapeDtypeStruct(y.shape, F32)],
                         [jax.ShapeDtypeStruct((SUBLANES, LANES), F32)])
    return acc[0, 0], dy


def rmsnorm(x, gain):
    y = x * lax.rsqrt(jnp.mean(x * x, axis=-1, keepdims=True) + NORM_EPS)
    return y * gain


def causal_depthwise_conv(u, w, bias):
    k_width, chans = w.shape
    out = lax.conv_general_dilated(u, w[:, None, :], window_strides=(1,), padding=[(k_width - 1, 0)],
                                   dimension_numbers=("NWC", "WIO", "NWC"), feature_group_count=chans)
    return out + bias


def ssd_chunked(x, dt, a, bm, cm):
    b, t, heads, _ = x.shape
    nc, cl = t // SSD_CHUNK, SSD_CHUNK
    g, hg = SSM_GROUPS, heads // SSM_GROUPS
    xs = (x * dt[..., None]).reshape(b, nc, cl, g, hg, SSM_HEAD_DIM)
    a_dt = (dt * a).reshape(b, nc, cl, g, hg).transpose(0, 1, 3, 4, 2)
    a_cs = jnp.cumsum(a_dt, axis=-1)
    bc = bm.reshape(b, nc, cl, g, SSM_STATE)
    cc = cm.reshape(b, nc, cl, g, SSM_STATE)
    causal = jnp.tril(jnp.ones((cl, cl), dtype=bool))
    seg = a_cs[..., :, None] - a_cs[..., None, :]
    lmat = jnp.exp(jnp.where(causal, seg, -jnp.inf))
    cb = jnp.einsum("bclgn,bcsgn->bcgls", cc, bc)
    y_diag = jnp.einsum("bcgls,bcghls,bcsghp->bclghp", cb, lmat, xs)
    decay = jnp.exp(a_cs[..., -1:] - a_cs)
    states = jnp.einsum("bclgn,bcghl,bclghp->bcghpn", bc, decay, xs)
    chunk_decay = jnp.exp(a_cs[..., -1])

    def step(carry, inp):
        st, dec = inp
        return carry * dec[..., None, None] + st, carry

    init = jnp.zeros((b, g, hg, SSM_HEAD_DIM, SSM_STATE), F32)
    _, prev = lax.scan(step, init, (jnp.moveaxis(states, 1, 0), jnp.moveaxis(chunk_decay, 1, 0)))
    prev = jnp.moveaxis(prev, 0, 1)
    y_off = jnp.einsum("bclgn,bcghpn,bcghl->bclghp", cc, prev, jnp.exp(a_cs))
    return (y_diag + y_off).reshape(b, t, heads, SSM_HEAD_DIM)


def mamba2_mixer(h, wt_in, conv_w, conv_b, dt_bias, a_log, d_skip, norm_w, w_out):
    t, d_model = h.shape
    d_inner = 2 * d_model
    heads = d_inner // SSM_HEAD_DIM
    gn = SSM_GROUPS * SSM_STATE
    conv_dim = d_inner + 2 * gn
    zx = lin_t(h, wt_in[:d_inner + conv_dim])
    dt_raw = lin_t(h, wt_in[d_inner + conv_dim:])
    return lin(ssm_core(zx, dt_raw, conv_w, conv_b, dt_bias, a_log, d_skip, norm_w), w_out)


def ssm_core_jnp(zx, dt_raw, conv_w, conv_b, dt_bias, a_log, d_skip, norm_w):
    t = zx.shape[0]
    d_inner = norm_w.shape[0]
    heads = d_inner // SSM_HEAD_DIM
    gn = SSM_GROUPS * SSM_STATE
    z = zx[:, :d_inner]
    xbc = zx[:, d_inner:]
    xbc = jax.nn.silu(causal_depthwise_conv(xbc[None], conv_w, conv_b))[0]
    xs = xbc[:, :d_inner]
    bm = xbc[:, d_inner:d_inner + gn].reshape(1, t, SSM_GROUPS, SSM_STATE)
    cm = xbc[:, d_inner + gn:].reshape(1, t, SSM_GROUPS, SSM_STATE)
    dt = jax.nn.softplus(dt_raw + dt_bias)[None]
    a = -jnp.exp(a_log)
    xh = xs.reshape(1, t, heads, SSM_HEAD_DIM)
    y = ssd_chunked(xh, dt, a, bm, cm)
    y = y + xh * d_skip[:, None]
    y = y.reshape(t, d_inner) * jax.nn.silu(z)
    return rmsnorm(y.reshape(t, SSM_GROUPS, -1), norm_w.reshape(SSM_GROUPS, -1)).reshape(t, d_inner)


def alibi_slopes(n_heads):
    return 2.0 ** (-8.0 * jnp.arange(1, n_heads + 1, dtype=F32) / n_heads)


def dilated_group_attention(q, k, v, window, dilation, slopes):
    b, t, nh, e = q.shape
    span = window // dilation
    blk = span
    lu = t // dilation
    nb = -(-lu // blk)
    lp = nb * blk

    def to_blocks(arr):
        arr = arr.reshape(b, lu, dilation, nh, e)
        arr = jnp.pad(arr, ((0, 0), (0, lp - lu), (0, 0), (0, 0), (0, 0)))
        return arr.reshape(b, nb, blk, dilation, nh, e)

    qb, kb, vb = to_blocks(q), to_blocks(k), to_blocks(v)
    pad_prev = ((0, 0), (1, 0), (0, 0), (0, 0), (0, 0), (0, 0))
    kcat = jnp.concatenate([jnp.pad(kb, pad_prev)[:, :nb], kb], axis=2)
    vcat = jnp.concatenate([jnp.pad(vb, pad_prev)[:, :nb], vb], axis=2)
    scores = jnp.einsum("bnqrhe,bnkrhe->bnrhqk", qb, kcat) * (1.0 / math.sqrt(e))
    qi = jnp.arange(blk)[:, None]
    ki = jnp.arange(2 * blk)[None, :]
    dist = qi + blk - ki
    in_band = (dist >= 0) & (dist <= span)
    key_u = jnp.arange(nb)[:, None] * blk - blk + jnp.arange(2 * blk)[None, :]
    valid = in_band[None] & (key_u >= 0)[:, None, :]
    bias = -slopes[:, None, None] * (dilation * dist).astype(F32)[None]
    logits = jnp.where(valid[None, :, None, None], scores + bias[None, None, None], -jnp.inf)
    lse = jax.nn.logsumexp(logits, axis=-1)
    probs = jnp.exp(logits - lse[..., None])
    out = jnp.einsum("bnrhqk,bnkrhe->bnqrhe", probs, vcat)
    out = out.reshape(b, lp, dilation, nh, e)[:, :lu].reshape(b, t, nh, e)
    lse = lse.transpose(0, 1, 4, 2, 3).reshape(b, lp, dilation, nh)[:, :lu].reshape(b, t, nh)
    return out, lse


def dilated_attention_mixer(h, wt_qkv, q_gain, k_gain, w_o):
    t, d_model = h.shape
    heads = d_model // ATT_HEAD_DIM
    ng = len(DIL_PATTERNS)
    return lin(attention_core(lin_t(h, wt_qkv, BF16), q_gain, k_gain), w_o)


def attention_core_jnp(qkv, q_gain, k_gain):
    t = qkv.shape[0]
    ng = len(DIL_PATTERNS)
    heads = qkv.shape[1] // (3 * ng * ATT_HEAD_DIM)
    qkv = qkv.astype(F32).reshape(1, t, ng, 3, heads, ATT_HEAD_DIM)
    q = rmsnorm(qkv[:, :, :, 0], q_gain)
    k = rmsnorm(qkv[:, :, :, 1], k_gain)
    v = qkv[:, :, :, 2]
    slopes = alibi_slopes(heads)
    outs, lses = [], []
    for g, (window, dilation) in enumerate(DIL_PATTERNS):
        o_g, l_g = dilated_group_attention(q[:, :, g], k[:, :, g], v[:, :, g], window, dilation, slopes)
        outs.append(o_g)
        lses.append(l_g)
    alpha = jax.nn.softmax(jnp.stack(lses), axis=0)
    o = jnp.einsum("gbth,gbthe->bthe", alpha, jnp.stack(outs))
    return o.reshape(t, heads * ATT_HEAD_DIM)


def local_step(small, fetch, emit, x, p, target):
    depth = small['norm_mix'].shape[0]
    ssm_small = ('ssm_conv_w', 'ssm_conv_b', 'ssm_dt_bias', 'ssm_a_log', 'ssm_d_skip', 'ssm_norm_w')
    saved = []
    for i in range(depth):
        j = i // 2
        s = {'x': x}
        wm, token = fetch(('mix', i), x)
        h = s['h'] = _rms_fwd(x, small['norm_mix'][i], token)
        if i % 2 == 0:
            n_main = wm['ssm_w_in'].shape[0] - small['ssm_dt_bias'].shape[1]
            zx = _mm(h, wm['ssm_w_in'][:n_main], tb=True, out_dtype=BF16, name="ssm_in_fwd")
            more, token = fetch(('out', i), zx)
            wm = {**wm, **more}
            dt_raw = _mm(h, wm['ssm_w_in'][n_main:], tb=True, after=token, name="ssm_dt_fwd")
            y, s['mix'] = _ssm_core_fwd(zx, dt_raw, wm['ssm_conv_w'], *[small[n][j] for n in ssm_small[1:]])
            x = _mm(y, wm['ssm_w_out'], add=x, name="ssm_out_fwd")
        else:
            qkv = _mm(h, wm['att_w_qkv'], tb=True, out_dtype=BF16, name="att_qkv_fwd")
            y, s['mix'] = _attention_core_fwd(qkv, small['att_q_norm'][j], small['att_k_norm'][j])
            x = _mm(y, wm['att_w_o'], add=x, name="att_o_fwd")
        s['wm'], s['y'], s['x1'] = wm, y, x
        wf, token = fetch(('ffn', i), x)
        s['wf'] = wf
        h2 = s['h2'] = _rms_fwd(x, small['norm_ffn'][i], token)
        gu = s['gu'] = _mm(h2, wf['ffn_w_gu'], tb=True, out_dtype=BF16, name="ffn_gu_fwd")
        act = s['act'] = _swiglu_fwd(gu)
        x = s['x2'] = _mm(act, wf['ffn_w_down'], add=x, name="ffn_down_fwd")
        gl = s['gl'] = _mm(x, wf['ple_w_gate'], out_dtype=BF16, name="ple_gate_fwd")
        ple = s['ple'] = _mm(p[i], wf['ple_w_proj'], tb=True, out_dtype=BF16, name="ple_proj_fwd")
        x = _ple_fwd(x, gl, ple)
        saved.append(s)
    loss, dx = _loss_fwd(x, target)

    g = {n: [None] * small[n].shape[0] for n in small}
    for i in reversed(range(depth)):
        j = i // 2
        s = saved[i]
        wm, wf = s['wm'], s['wf']
        gf = {}
        dgl, dple = _ple_bwd(s['gl'], s['ple'], dx)
        gf['ple_w_proj'] = _mm(dple, p[i], ta=True, out_dtype=BF16, name="ple_proj_dw")
        gf['ple_w_gate'] = _mm(s['x2'], dgl, ta=True, out_dtype=BF16, name="ple_gate_dw")
        dx = _mm(dgl, wf['ple_w_gate'], tb=True, add=dx, name="ple_gate_da")
        dact = _mm(dx, wf['ffn_w_down'], tb=True, out_dtype=BF16, name="ffn_down_da")
        gf['ffn_w_down'] = _mm(s['act'], dx, ta=True, out_dtype=BF16, name="ffn_down_dw")
        dgu = _swiglu_bwd(s['gu'], dact)
        dh2 = _mm(dgu, wf['ffn_w_gu'], out_dtype=BF16, name="ffn_gu_da")
        gf['ffn_w_gu'] = _mm(dgu, s['h2'], ta=True, out_dtype=BF16, name="ffn_gu_dw")
        dx, g['norm_ffn'][i] = _rms_bwd(s['x1'], small['norm_ffn'][i], dh2, dx, emit(('ffn', i), gf))
        gm = {}
        if i % 2 == 0:
            n_main = wm['ssm_w_in'].shape[0] - small['ssm_dt_bias'].shape[1]
            dyn = _mm(dx, wm['ssm_w_out'], tb=True, out_dtype=BF16, name="ssm_out_da")
            gm['ssm_w_out'] = _mm(s['y'], dx, ta=True, out_dtype=BF16, name="ssm_out_dw")
            dzx, d_dt, *sg = _ssm_core_bwd(s['mix'], dyn)
            for n, v in zip(ssm_small, sg):
                g[n][j] = v
            dh = _mm(d_dt, wm['ssm_w_in'][n_main:], name="ssm_dt_da")
            dh = _mm(dzx, wm['ssm_w_in'][:n_main], add=dh, out_dtype=BF16, name="ssm_in_da")
            gm['ssm_w_in'] = jnp.concatenate([_mm(dzx, s['h'], ta=True, out_dtype=BF16, name="ssm_in_dw"),
                                              _mm(d_dt, s['h'], ta=True, out_dtype=BF16, name="ssm_dt_dw")], axis=0)
        else:
            do = _mm(dx, wm['att_w_o'], tb=True, name="att_o_da")
            gm['att_w_o'] = _mm(s['y'], dx, ta=True, out_dtype=BF16, name="att_o_dw")
            dqkv, g['att_q_norm'][j], g['att_k_norm'][j] = _attention_core_bwd(s['mix'], do)
            dh = _mm(dqkv, wm['att_w_qkv'], out_dtype=BF16, name="att_qkv_da")
            gm['att_w_qkv'] = _mm(dqkv, s['h'], ta=True, out_dtype=BF16, name="att_qkv_dw")
        dx, g['norm_mix'][i] = _rms_bwd(s['x'], small['norm_mix'][i], dh, dx, emit(('mix', i), gm))
    return loss, dx, {n: jnp.stack(v) for n, v in g.items()}


CHANNEL_WEIGHTS = ('ffn_w_gate', 'ffn_w_up', 'ffn_w_down', 'ple_w_proj', 'ple_w_gate')


def _stages(depth):
    gather, scatter = {}, {}
    for i in range(depth):
        j = i // 2
        if i % 2 == 0:
            gather['mix', i] = [('ssm_w_in', j), ('ssm_conv_w', j)]
            gather['out', i] = [('ssm_w_out', j)]
            scatter['mix', i] = [('ssm_w_in', j), ('ssm_w_out', j)]
        else:
            gather['mix', i] = scatter['mix', i] = [('att_w_qkv', j), ('att_w_o', j)]
        gather['ffn', i] = scatter['ffn', i] = [(n, i) for n in CHANNEL_WEIGHTS]
    return gather, scatter


def _pack_plan(shapes, width, members):
    plan, off = [], 0
    for name, lyr in members:
        _, r, c = shapes[name]
        if name in COL_SHARDED:
            r, c = c, r
        if name == 'ssm_conv_w':
            pr = -(-2 * r * c // width)
        else:
            assert (r * c) % width == 0, (name, r, c)
            pr = r * c // width
        plan.append((name, lyr, r, c, pr, off))
        off += _round_up(pr, BF16_ROWS)
    return plan, off


def _small_plan(shapes):
    plan, off = [], 0
    for name in SMALL:
        n = math.prod(shapes[name])
        plan.append((name, n, off))
        off += n
    return plan, _round_up(off, SUBLANES * LANES)


def kernel(x, p, norm_mix, norm_ffn, ssm_w_in, ssm_conv_w, ssm_conv_b, ssm_dt_bias, ssm_a_log, ssm_d_skip, ssm_norm_w, ssm_w_out, att_w_qkv, att_q_norm, att_k_norm, att_w_o, ffn_w_gate, ffn_w_up, ffn_w_down, ple_w_proj, ple_w_gate, loss_target, m_norm_mix, m_norm_ffn, m_ssm_w_in, m_ssm_conv_w, m_ssm_conv_b, m_ssm_dt_bias, m_ssm_a_log, m_ssm_d_skip, m_ssm_norm_w, m_ssm_w_out, m_att_w_qkv, m_att_q_norm, m_att_k_norm, m_att_w_o, m_ffn_w_gate, m_ffn_w_up, m_ffn_w_down, m_ple_w_proj, m_ple_w_gate, v_norm_mix, v_norm_ffn, v_ssm_w_in, v_ssm_conv_w, v_ssm_conv_b, v_ssm_dt_bias, v_ssm_a_log, v_ssm_d_skip, v_ssm_norm_w, v_ssm_w_out, v_att_w_qkv, v_att_q_norm, v_att_k_norm, v_att_w_o, v_ffn_w_gate, v_ffn_w_up, v_ffn_w_down, v_ple_w_proj, v_ple_w_gate):
    given = dict(locals())
    w_in = {n: given[n] for n in WEIGHTS}
    m_in = {n: given["m_" + n] for n in WEIGHTS}
    v_in = {n: given["v_" + n] for n in WEIGHTS}
    width = x.shape[-1]
    depth = norm_mix.shape[0]

    gather_members, scatter_members = _stages(depth)
    shapes = {n: w_in[n].shape for n in BIG + ('ssm_conv_w',)}
    plans = {key: _pack_plan(shapes, width, members)[0] for key, members in gather_members.items()}
    splans = {key: _pack_plan(shapes, width, members)[0] for key, members in scatter_members.items()}
    order = list(gather_members)

    def pack_weights(stage):
        pieces = []
        for name, layer, r, c, pr, off in plans[stage]:
            blk = w_in[name][layer]
            if name == 'ssm_conv_w':
                blk = lax.bitcast_convert_type(blk.reshape(-1), BF16).reshape(-1)
                blk = jnp.pad(blk, (0, pr * width - blk.shape[0]))
            elif name in COL_SHARDED:
                blk = blk.T
            blk = blk.astype(BF16).reshape(pr, width)
            pieces.append(jnp.pad(blk, ((0, _round_up(pr, BF16_ROWS) - pr), (0, 0))))
        return jnp.concatenate(pieces, axis=0)

    packed = [pack_weights(order[0])]
    pending = [exchange_start(packed[0], True, "gather_start_0")]
    packed += [pack_weights(stage) for stage in order[1:]]

    def fetch(stage, after):
        k = order.index(stage)
        handle, token = pending[k]
        land = exchange_wait(handle, [token] + packed[1:] if k == 0 else after, True, f"gather_wait_{k}")
        token = None
        if k + 1 < len(order):
            pending.append(exchange_start(packed[k + 1], True, f"gather_start_{k + 1}", land))
            token = pending[-1][1]
        got = {}
        for name, layer, r, c, pr, off in plans[stage]:
            piece = land[:, off:off + pr]
            if name == 'ssm_conv_w':
                taps, chans = w_in[name].shape[1:]
                bits = piece.reshape(N_DEV, -1)[:, :2 * taps * chans].reshape(N_DEV, taps * chans, 2)
                piece = lax.bitcast_convert_type(bits, F32).reshape(N_DEV, taps, chans)
                got[name] = piece.transpose(1, 0, 2).reshape(taps, N_DEV * chans)
            else:
                got[name] = piece.reshape(N_DEV * r, c)
        if 'ffn_w_gate' in got:
            got['ffn_w_gu'] = jnp.concatenate([got.pop('ffn_w_gate'), got.pop('ffn_w_up')], axis=0)
        return got, token

    scatters = {}

    def emit(stage, grads):
        grads = dict(grads)
        if 'ffn_w_gu' in grads:
            hidden = grads['ffn_w_gu'].shape[0] // 2
            grads['ffn_w_gate'], grads['ffn_w_up'] = grads['ffn_w_gu'][:hidden], grads['ffn_w_gu'][hidden:]
        pieces = []
        for name, layer, r, c, pr, off in splans[stage]:
            g = grads[name].reshape(N_DEV, pr, width)
            pieces.append(jnp.pad(g, ((0, 0), (0, _round_up(pr, BF16_ROWS) - pr), (0, 0))))
        scatters[stage], token = exchange_start(jnp.concatenate(pieces, axis=1), False,
                                                f"scatter_start_{len(scatters)}")
        return token

    small = {n: w_in[n] for n in SMALL}
    cs = w_in['ssm_conv_w'].shape[2]
    loss_local, gx, gw = local_step(small, fetch, emit, x[0], p[:, 0], loss_target[0])
    loss = lax.psum(loss_local, ("x", "y", "c"))

    parts = {}
    for k, (stage, handle) in enumerate(scatters.items()):
        received = exchange_wait(handle, gx, False, f"scatter_wait_{k}")
        gsum = sum_slots(received, f"sum_grads_{k}")
        for name, layer, r, c, pr, off in splans[stage]:
            g = gsum[off:off + pr].reshape(r, c)
            parts[name, layer] = g.T if name in COL_SHARDED else g
    grads = {n: jnp.stack([parts[n, layer] for layer in range(w_in[n].shape[0])]) for n in BIG}

    splan, stotal = _small_plan({n: gw[n].shape for n in SMALL})
    svec = jnp.concatenate([gw[n].reshape(-1) for n, _, _ in splan])
    svec = jnp.pad(svec, (0, stotal - svec.shape[0])).reshape(stotal // LANES, LANES)
    _, ssum = all_gather_sum_small(svec, "sum_small_grads")
    ssum = ssum.reshape(-1)
    for name, n, off in splan:
        grads[name] = ssum[off:off + n].reshape(gw[name].shape)
    me = _me()
    grads['ssm_conv_w'] = lax.dynamic_slice_in_dim(grads['ssm_conv_w'], me * cs, cs, axis=2)

    delta, new_m, new_v = {}, {}, {}
    for name in BIG:
        shp = w_in[name].shape
        flat = lambda a: a.reshape(-1, shp[-1])
        d, nm, nv = adamw(flat(w_in[name]), flat(grads[name]), flat(m_in[name]), flat(v_in[name]), "adamw_" + name)
        delta[name], new_m[name], new_v[name] = d.reshape(shp), nm.reshape(shp), nv.reshape(shp)
    splan2, stotal2 = _small_plan({n: w_in[n].shape for n in SMALL})

    def pack_small(src):
        vec = jnp.concatenate([src[n].reshape(-1) for n, _, _ in splan2])
        return jnp.pad(vec, (0, stotal2 - vec.shape[0]), constant_values=1.0).reshape(stotal2 // LANES, LANES)

    sd, snm, snv = adamw(pack_small(w_in), pack_small(grads), pack_small(m_in), pack_small(v_in), "adamw_small")
    for name, n, off in splan2:
        shp = w_in[name].shape
        delta[name] = sd.reshape(-1)[off:off + n].reshape(shp)
        new_m[name] = snm.reshape(-1)[off:off + n].reshape(shp)
        new_v[name] = snv.reshape(-1)[off:off + n].reshape(shp)

    return (loss, gx[None], *[grads[n] for n in WEIGHTS], *[delta[n] for n in WEIGHTS],
            *[new_m[n] for n in WEIGHTS], *[new_v[n] for n in WEIGHTS])
```

```python
import functools
import math

import jax
import jax.numpy as jnp
from jax import lax
from jax.experimental import pallas as pl
from jax.experimental.pallas import tpu as pltpu

F32 = jnp.float32
BF16 = jnp.bfloat16
N_DEV = 8
MESH = pl.DeviceIdType.MESH

SSM_HEAD_DIM = 64
SSM_GROUPS = 4
SSM_STATE = 128
CONV_WIDTH = 4
SSD_CHUNK = 128
ATT_HEAD_DIM = 64
DIL_PATTERNS = ((128, 1), (512, 4), (2048, 16))
NORM_EPS = 1e-6
ADAM_LR = 0.001
ADAM_B1 = 0.9
ADAM_B2 = 0.999
ADAM_EPS = 1e-08
ADAM_WD = 0.01
ADAM_STEP = 10

BF16_ROWS = 16
LANES = 128
SUBLANES = 8

WEIGHTS = ['norm_mix', 'norm_ffn', 'ssm_w_in', 'ssm_conv_w', 'ssm_conv_b', 'ssm_dt_bias', 'ssm_a_log', 'ssm_d_skip',
           'ssm_norm_w', 'ssm_w_out', 'att_w_qkv', 'att_q_norm', 'att_k_norm', 'att_w_o', 'ffn_w_gate', 'ffn_w_up',
           'ffn_w_down', 'ple_w_proj', 'ple_w_gate']
COL_SHARDED = ('ssm_w_in', 'att_w_qkv', 'ffn_w_gate', 'ffn_w_up', 'ple_w_proj')
ROW_SHARDED = ('ssm_w_out', 'att_w_o', 'ffn_w_down', 'ple_w_gate')
BIG = COL_SHARDED + ROW_SHARDED
SMALL = ('norm_mix', 'norm_ffn', 'ssm_conv_w', 'ssm_conv_b', 'ssm_dt_bias', 'ssm_a_log', 'ssm_d_skip', 'ssm_norm_w',
         'att_q_norm', 'att_k_norm')


def _pick(n, cands):
    for c in cands:
        if n % c == 0:
            return c
    return n


def _round_up(n, m):
    return -(-n // m) * m


MM_TILES = (1024, 1408, 512, 256, 128)
MM_VMEM_BYTES = 48 * 1024 * 1024


def _mm(a, b, *, ta=False, tb=False, out_dtype=F32, add=None, after=None, name):
    k_dim, m_dim = (a.shape if ta else a.shape[::-1])
    n_dim = b.shape[0] if tb else b.shape[1]
    assert (b.shape[1] if tb else b.shape[0]) == k_dim, (a.shape, b.shape, ta, tb)
    tm = _pick(m_dim, MM_TILES)
    tn = _pick(n_dim, MM_TILES)
    tk = _pick(k_dim, MM_TILES)
    nk = k_dim // tk
    a_spec = pl.BlockSpec((tk, tm), lambda i, j, k: (k, i)) if ta else pl.BlockSpec((tm, tk), lambda i, j, k: (i, k))
    b_spec = pl.BlockSpec((tn, tk), lambda i, j, k: (j, k)) if tb else pl.BlockSpec((tk, tn), lambda i, j, k: (k, j))
    o_spec = pl.BlockSpec((tm, tn), lambda i, j, k: (i, j))
    dims = (((0 if ta else 1,), (1 if tb else 0,)), ((), ()))
    has_add = add is not None
    n_in = 2 + has_add + (after is not None)

    def body(*refs):
        a_ref, b_ref = refs[:2]
        o_ref = refs[n_in]

        def dot():
            return lax.dot_general(a_ref[...].astype(BF16), b_ref[...].astype(BF16), dims,
                                   preferred_element_type=F32)

        def finish(acc):
            if has_add:
                acc = acc + refs[2][...].astype(F32)
            o_ref[...] = acc.astype(o_ref.dtype)

        if nk == 1:
            finish(dot())
            return
        acc_ref = refs[n_in + 1]
        k = pl.program_id(2)

        @pl.when(k == 0)
        def _():
            acc_ref[...] = dot()

        @pl.when((k > 0) & (k < nk - 1))
        def _():
            acc_ref[...] += dot()

        @pl.when(k == nk - 1)
        def _():
            finish(acc_ref[...] + dot())

    return pl.pallas_call(
        body, name=f"{name}_{m_dim}x{n_dim}x{k_dim}",
        out_shape=jax.ShapeDtypeStruct((m_dim, n_dim), out_dtype),
        grid=(m_dim // tm, n_dim // tn, nk),
        in_specs=[a_spec, b_spec] + ([o_spec] if has_add else []) + (
            [] if after is None else [pl.BlockSpec(memory_space=pl.ANY)]),
        out_specs=o_spec,
        scratch_shapes=[] if nk == 1 else [pltpu.VMEM((tm, tn), F32)],
        compiler_params=pltpu.CompilerParams(dimension_semantics=("parallel", "parallel", "arbitrary"),
                                             vmem_limit_bytes=MM_VMEM_BYTES),
    )(*((a, b) + ((add,) if has_add else ()) + (() if after is None else (after,))))


def _me():
    return 4 * lax.axis_index("x") + 2 * lax.axis_index("y") + lax.axis_index("c")


def _peer(j):
    x, y, c = lax.axis_index("x"), lax.axis_index("y"), lax.axis_index("c")
    px = 1 - x if j & 4 else x
    py = 1 - y if j & 2 else y
    pc = 1 - c if j & 1 else c
    return (px, py, pc), 4 * px + 2 * py + pc


def _exchange_body(src_of, dst_ref, send_sems, recv_sems, local_sem):
    me = _me()
    mine = pltpu.make_async_copy(src_of(me), dst_ref.at[me], local_sem)
    mine.start()
    sends = []
    for j in range(1, N_DEV):
        peer, pidx = _peer(j)
        cp = pltpu.make_async_remote_copy(src_ref=src_of(pidx), dst_ref=dst_ref.at[me], send_sem=send_sems.at[j - 1],
                                          recv_sem=recv_sems.at[j - 1], device_id=peer, device_id_type=MESH)
        cp.start()
        sends.append(cp)
    for j in range(1, N_DEV):
        peer, pidx = _peer(j)
        pltpu.make_async_remote_copy(src_ref=src_of(pidx), dst_ref=dst_ref.at[pidx], send_sem=send_sems.at[j - 1],
                                     recv_sem=recv_sems.at[j - 1], device_id=peer, device_id_type=MESH).wait_recv()
    for cp in sends:
        cp.wait_send()
    mine.wait()


_EXCHANGE_SCRATCH = [pltpu.SemaphoreType.DMA((N_DEV - 1,)), pltpu.SemaphoreType.DMA((N_DEV - 1,)),
                     pltpu.SemaphoreType.DMA]


_HBM = pl.BlockSpec(memory_space=pltpu.HBM)
_SEM = pl.BlockSpec(memory_space=pltpu.SEMAPHORE)


def _split_copies(src_ref, gather, land_ref, send_sems, recv_sems):
    me = _me()
    pairs = []
    for j in range(1, N_DEV):
        peer, pidx = _peer(j)

        def make(slot, peer=peer, pidx=pidx, j=j):
            return pltpu.make_async_remote_copy(
                src_ref=src_ref if gather else src_ref.at[pidx], dst_ref=land_ref.at[slot],
                send_sem=send_sems.at[j - 1], recv_sem=recv_sems.at[j - 1], device_id=peer, device_id_type=MESH)

        pairs.append((make(me), make(pidx)))
    return pairs


def exchange_start(src, gather, name, after=None):
    land_shape = ((N_DEV,) + src.shape) if gather else src.shape
    has_after = after is not None

    def body(*refs):
        src_ref, land_ref = refs[:2]
        send_sems, recv_sems = refs[2 + has_after:4 + has_after]
        for send, _ in _split_copies(src_ref, gather, land_ref, send_sems, recv_sems):
            send.start()
        refs[-1][...] = jnp.zeros_like(refs[-1])

    sem = pltpu.SemaphoreType.DMA((N_DEV - 1,))
    send_sems, recv_sems, src_thru, land, token = pl.pallas_call(
        body, name=name,
        out_shape=(sem, sem, pltpu.HBM(src.shape, src.dtype), pltpu.HBM(land_shape, src.dtype),
                   jax.ShapeDtypeStruct((SUBLANES, LANES), F32)),
        in_specs=(_HBM, _HBM) + ((pl.BlockSpec(memory_space=pl.ANY),) if has_after else ()),
        out_specs=(_SEM, _SEM, _HBM, _HBM, pl.BlockSpec(memory_space=pltpu.VMEM)),
        input_output_aliases={0: 2, 1: 3},
        compiler_params=pltpu.CompilerParams(has_side_effects=pltpu.SideEffectType.DATAFLOW_SIDE_EFFECTING),
    )(pltpu.with_memory_space_constraint(src, pltpu.HBM),
      pltpu.with_memory_space_constraint(lax.empty(land_shape, src.dtype), pltpu.HBM),
      *((after,) if has_after else ()))
    return (send_sems, recv_sems, src_thru, land), token


def exchange_wait(handle, after, gather, name):
    send_sems, recv_sems, src_thru, land = handle
    after = tuple(after) if isinstance(after, (tuple, list)) else (after,)

    def body(src_ref, land_ref, send_sems, recv_sems, *rest):
        for _, arrival in _split_copies(src_ref, gather, land_ref, send_sems, recv_sems):
            arrival.wait_send()
            arrival.wait_recv()

    src_done, got = pl.pallas_call(
        body, name=name,
        out_shape=(pltpu.HBM(src_thru.shape, src_thru.dtype), pltpu.HBM(land.shape, land.dtype)),
        in_specs=(_HBM, _HBM, _SEM, _SEM) + (pl.BlockSpec(memory_space=pl.ANY),) * len(after), out_specs=(_HBM, _HBM),
        input_output_aliases={0: 0, 1: 1},
        compiler_params=pltpu.CompilerParams(has_side_effects=pltpu.SideEffectType.DATAFLOW_SIDE_EFFECTING),
    )(src_thru, land, send_sems, recv_sems, *after)
    mine = src_done if gather else lax.dynamic_index_in_dim(src_done, _me(), 0, keepdims=False)
    return lax.dynamic_update_index_in_dim(got, mine, _me(), 0)


def all_gather_sum_small(v, name):
    def body(x_ref, out_ref, sum_ref, send_sems, recv_sems, local_sem):
        _exchange_body(lambda k: x_ref, out_ref, send_sems, recv_sems, local_sem)
        acc = out_ref[0]
        for k in range(1, N_DEV):
            acc = acc + out_ref[k]
        sum_ref[...] = acc

    return pl.pallas_call(
        body, name=name,
        out_shape=(jax.ShapeDtypeStruct((N_DEV,) + v.shape, v.dtype), jax.ShapeDtypeStruct(v.shape, v.dtype)),
        in_specs=[pl.BlockSpec(memory_space=pltpu.VMEM)],
        out_specs=(pl.BlockSpec(memory_space=pltpu.VMEM), pl.BlockSpec(memory_space=pltpu.VMEM)),
        scratch_shapes=list(_EXCHANGE_SCRATCH),
    )(v)


def sum_slots(slots, name):
    _, p_dim, c_dim = slots.shape
    tp = next(tp for tp in range(512, 0, -BF16_ROWS) if p_dim % tp == 0)

    def body(x_ref, o_ref):
        acc = x_ref[0].astype(F32)
        for k in range(1, N_DEV):
            acc = acc + x_ref[k].astype(F32)
        o_ref[...] = acc

    return pl.pallas_call(
        body, name=name,
        out_shape=jax.ShapeDtypeStruct((p_dim, c_dim), F32),
        grid=(p_dim // tp,),
        in_specs=[pl.BlockSpec((N_DEV, tp, c_dim), lambda i: (0, i, 0))],
        out_specs=pl.BlockSpec((tp, c_dim), lambda i: (i, 0)),
        compiler_params=pltpu.CompilerParams(dimension_semantics=("parallel",)),
    )(slots)


def adamw(w, g, m, v, name):
    rows, cols = w.shape
    tr = _pick(rows, (256, 128, 64, 32, 16, 8))

    def body(w_ref, g_ref, m_ref, v_ref, d_ref, nm_ref, nv_ref):
        gv = g_ref[...]
        nm = ADAM_B1 * m_ref[...] + (1.0 - ADAM_B1) * gv
        nv = ADAM_B2 * v_ref[...] + (1.0 - ADAM_B2) * (gv * gv)
        m_hat = nm / (1.0 - ADAM_B1 ** ADAM_STEP)
        v_hat = nv / (1.0 - ADAM_B2 ** ADAM_STEP)
        d_ref[...] = -ADAM_LR * (m_hat / (jnp.sqrt(v_hat) + ADAM_EPS) + ADAM_WD * w_ref[...])
        nm_ref[...] = nm
        nv_ref[...] = nv

    spec = pl.BlockSpec((tr, cols), lambda i: (i, 0))
    shp = jax.ShapeDtypeStruct((rows, cols), F32)
    return pl.pallas_call(
        body, name=name, out_shape=(shp, shp, shp), grid=(rows // tr,),
        in_specs=[spec] * 4, out_specs=(spec,) * 3,
        compiler_params=pltpu.CompilerParams(dimension_semantics=("parallel",)),
    )(w, g, m, v)


ATT_BLK = 128
NEG = -1e30


def _head_sums(v):
    li = lax.broadcasted_iota(jnp.int32, (LANES, LANES), 0) // ATT_HEAD_DIM
    lj = lax.broadcasted_iota(jnp.int32, (LANES, LANES), 1) // ATT_HEAD_DIM
    ones = (li == lj).astype(BF16)
    hi = v.astype(BF16)
    lo = (v - hi.astype(F32)).astype(BF16)
    return jnp.dot(hi, ones, preferred_element_type=F32) + jnp.dot(lo, ones, preferred_element_type=F32)


def _head_col(v, hmask):
    return jnp.max(jnp.where(hmask, v, -jnp.inf), axis=-1, keepdims=True)


def _qk_norm(raw, gain2):
    rstd = lax.rsqrt(_head_sums(raw * raw) * (1.0 / ATT_HEAD_DIM) + NORM_EPS)
    xhat = raw * rstd
    return xhat * gain2, xhat, rstd


def _qk_norm_bwd(dn, xhat, rstd, gain2):
    dxh = dn * gain2
    return rstd * (dxh - xhat * (_head_sums(dxh * xhat) * (1.0 / ATT_HEAD_DIM))), dn * xhat


def _att_mask_bias(n, dilation):
    qi = lax.broadcasted_iota(jnp.int32, (ATT_BLK, 2 * ATT_BLK), 0)
    ki = lax.broadcasted_iota(jnp.int32, (ATT_BLK, 2 * ATT_BLK), 1)
    dist = qi + ATT_BLK - ki
    valid = (dist >= 0) & (dist <= ATT_BLK) & ((n > 0) | (ki >= ATT_BLK))
    return valid, (dilation * dist).astype(F32)


ATT_PAIRS = 4
RELAYOUT_ROWS = 512
RELAYOUT_COLS = 512


def _to_residues(x, dilation, col0=0, cols=None):
    t = x.shape[0]
    cols = x.shape[1] if cols is None else cols
    if dilation == 1 and col0 == 0 and cols == x.shape[1]:
        return x.reshape(1, t, cols)
    tr = _pick(t, (RELAYOUT_ROWS,))
    tc = _pick(cols, (RELAYOUT_COLS, 256, 128))
    per = tr // dilation
    assert tr % dilation == 0 and col0 % tc == 0

    def body(x_ref, o_ref, s_ref):
        for c in range(tc // LANES):
            lanes = slice(c * LANES, (c + 1) * LANES)
            s_ref[c] = x_ref[:, lanes].astype(F32)
            for r in range(dilation):
                o_ref[r, :, lanes] = s_ref[c, pl.ds(r, per, stride=dilation), :].astype(o_ref.dtype)

    return pl.pallas_call(
        body, name=f"to_residues_{dilation}", out_shape=jax.ShapeDtypeStruct((dilation, t // dilation, cols), x.dtype),
        grid=(t // tr, cols // tc),
        in_specs=[pl.BlockSpec((tr, tc), lambda i, j: (i, col0 // tc + j))],
        out_specs=pl.BlockSpec((dilation, per, tc), lambda i, j: (0, i, j)),
        scratch_shapes=[pltpu.VMEM((tc // LANES, tr, LANES), F32)],
        compiler_params=pltpu.CompilerParams(dimension_semantics=("parallel", "parallel")),
    )(x)


def _from_residues(y):
    dilation, lu, cols = y.shape
    t = dilation * lu
    if dilation == 1:
        return y.reshape(t, cols)
    tr = _pick(t, (RELAYOUT_ROWS,))
    tc = _pick(cols, (RELAYOUT_COLS, 256, 128))
    per = tr // dilation

    def body(y_ref, o_ref, s_ref):
        for c in range(tc // LANES):
            lanes = slice(c * LANES, (c + 1) * LANES)
            for r in range(dilation):
                s_ref[c, pl.ds(r, per, stride=dilation), :] = y_ref[r, :, lanes].astype(F32)
            o_ref[:, lanes] = s_ref[c].astype(o_ref.dtype)

    return pl.pallas_call(
        body, name=f"from_residues_{dilation}", out_shape=jax.ShapeDtypeStruct((t, cols), y.dtype),
        grid=(t // tr, cols // tc),
        in_specs=[pl.BlockSpec((dilation, per, tc), lambda i, j: (0, i, j))],
        out_specs=pl.BlockSpec((tr, tc), lambda i, j: (i, j)),
        scratch_shapes=[pltpu.VMEM((tc // LANES, tr, LANES), F32)],
        compiler_params=pltpu.CompilerParams(dimension_semantics=("parallel", "parallel")),
    )(y)


def _att_specs(hd, v_base, nb, pp):
    width = pp * LANES
    assert hd % width == 0 and v_base % width == 0

    def spec(base, shift):
        def imap(r, hp, n):
            row = jnp.minimum(n, nb - 1) if shift == 0 else jnp.maximum(n - 1, 0)
            return (r, row, base // width + hp)
        return pl.BlockSpec((None, ATT_BLK, width), imap)

    return [spec(0, 0), spec(hd, 1), spec(hd, 0), spec(v_base, 1), spec(v_base, 0)]


DELTA_LANE = 64


def _att_pre(qkv, g, dilation, gq2, gk2):
    t, width = qkv.shape
    hd = width // (3 * len(DIL_PATTERNS))
    tr = _pick(t, (RELAYOUT_ROWS,))
    tc = _pick(hd, (RELAYOUT_COLS, 256, 128))
    per = tr // dilation
    assert tr % dilation == 0 and (g * 3 * hd) % tc == 0

    def body(x_ref, gq_ref, gk_ref, o_ref, s_ref):
        gain = jnp.where(pl.program_id(1) * tc < hd, gq_ref[0:1, :], gk_ref[0:1, :])
        for c in range(tc // LANES):
            lanes = slice(c * LANES, (c + 1) * LANES)
            s_ref[c] = _qk_norm(x_ref[:, lanes].astype(F32), gain)[0]
            for r in range(dilation):
                o_ref[r, :, lanes] = s_ref[c, pl.ds(r, per, stride=dilation), :].astype(o_ref.dtype)

    vec_spec = pl.BlockSpec((SUBLANES, LANES), lambda i, j: (0, 0))
    return pl.pallas_call(
        body, name=f"att_pre_g{g}", out_shape=jax.ShapeDtypeStruct((dilation, t // dilation, 2 * hd), qkv.dtype),
        grid=(t // tr, 2 * hd // tc),
        in_specs=[pl.BlockSpec((tr, tc), lambda i, j: (i, g * 3 * hd // tc + j)), vec_spec, vec_spec],
        out_specs=pl.BlockSpec((dilation, per, tc), lambda i, j: (0, i, j)),
        scratch_shapes=[pltpu.VMEM((tc // LANES, tr, LANES), F32)],
        compiler_params=pltpu.CompilerParams(dimension_semantics=("parallel", "parallel")),
    )(qkv, gq2, gk2)


def _att_group_fwd(view, hd, slopes, g):
    qk_r, v_r, v_base = view
    dilation, lu, _ = qk_r.shape
    nb = lu // ATT_BLK
    assert nb * ATT_BLK == lu and hd % LANES == 0
    hpn = hd // LANES
    pp = math.gcd(ATT_PAIRS, hpn)
    scale = 1.0 / math.sqrt(ATT_HEAD_DIM)

    def body(q_ref, kp_ref, kc_ref, vp_ref, vc_ref, sl_ref, o_ref, l_ref):
        n = pl.program_id(2)
        lane = lax.broadcasted_iota(jnp.int32, (1, LANES), 1)
        first = (lane // ATT_HEAD_DIM) == 0
        valid, dist = _att_mask_bias(n, dilation)
        stats = jnp.zeros((ATT_BLK, LANES), F32)
        for pair in range(pp):
            cols = slice(pair * LANES, (pair + 1) * LANES)
            qn = q_ref[:, cols]
            kn16 = jnp.concatenate([kp_ref[:, cols], kc_ref[:, cols]], axis=0)
            v16 = jnp.concatenate([vp_ref[:, cols], vc_ref[:, cols]], axis=0)
            outs = []
            for hh in range(2):
                hmask = (lane // ATT_HEAD_DIM) == hh
                qh = jnp.where(hmask, qn, jnp.zeros_like(qn))
                s = lax.dot_general(qh, kn16, (((1,), (1,)), ((), ())), preferred_element_type=F32) * scale
                slope = _head_col(sl_ref[pair, 0:1, :], hmask)
                logits = jnp.where(valid, s - slope * dist, NEG)
                mx = jnp.max(logits, axis=-1, keepdims=True)
                pexp = jnp.exp(logits - mx)
                den = jnp.sum(pexp, axis=-1, keepdims=True)
                outs.append(jnp.dot(pexp.astype(BF16), v16, preferred_element_type=F32) / den)
                stats = jnp.where(lane == 2 * pair + hh, mx + jnp.log(den), stats)
            o_ref[:, cols] = jnp.where(first, outs[0], outs[1]).astype(BF16)
        l_ref[...] = stats

    out_spec = pl.BlockSpec((None, ATT_BLK, pp * LANES), lambda r, hp, n: (r, n, hp))
    stat_spec = pl.BlockSpec((None, ATT_BLK, LANES), lambda r, hp, n: (r, n, hp))
    o, lse = pl.pallas_call(
        body, name=f"att_fwd_g{g}",
        out_shape=(jax.ShapeDtypeStruct((dilation, lu, hd), BF16),
                   jax.ShapeDtypeStruct((dilation, lu, hpn // pp * LANES), F32)),
        grid=(dilation, hpn // pp, nb),
        in_specs=_att_specs(hd, v_base, nb, pp) + [pl.BlockSpec((pp, SUBLANES, LANES), lambda r, hp, n: (hp, 0, 0))],
        out_specs=(out_spec, stat_spec),
        compiler_params=pltpu.CompilerParams(dimension_semantics=("parallel", "parallel", "arbitrary")),
    )(qk_r, qk_r, qk_r, v_r, v_r, slopes)
    return _from_residues(o), _from_residues(lse)


def _att_merge(outs, lses):
    t, hd = outs[0].shape
    sw = lses[0].shape[1]
    pp = hd // sw
    tr = _pick(t, (256, 128))
    ng = len(outs)

    def body(*refs):
        o_refs, l_refs, o16_ref, lt_ref = refs[:ng], refs[ng:2 * ng], refs[2 * ng], refs[2 * ng + 1]
        lane = lax.broadcasted_iota(jnp.int32, (1, LANES), 1)
        first = (lane // ATT_HEAD_DIM) == 0
        for blk in range(sw // LANES):
            scols = slice(blk * LANES, (blk + 1) * LANES)
            stats = jnp.zeros((tr, LANES), F32)
            for pair in range(pp):
                cols = slice((blk * pp + pair) * LANES, (blk * pp + pair + 1) * LANES)
                weights = []
                for hh in range(2):
                    pick = lane == 2 * pair + hh
                    ls = [_head_col(r[:, scols], pick) for r in l_refs]
                    mx = functools.reduce(jnp.maximum, ls)
                    es = [jnp.exp(l - mx) for l in ls]
                    den = functools.reduce(jnp.add, es)
                    weights.append([e / den for e in es])
                    stats = jnp.where(pick, mx + jnp.log(den), stats)
                acc = jnp.zeros((tr, LANES), F32)
                for gi in range(ng):
                    acc = acc + jnp.where(first, weights[0][gi], weights[1][gi]) * o_refs[gi][:, cols].astype(F32)
                o16_ref[:, cols] = acc.astype(BF16)
            lt_ref[:, scols] = stats

    spec = pl.BlockSpec((tr, hd), lambda i: (i, 0))
    sspec = pl.BlockSpec((tr, sw), lambda i: (i, 0))
    return pl.pallas_call(
        body, name="att_merge",
        out_shape=(jax.ShapeDtypeStruct((t, hd), BF16), jax.ShapeDtypeStruct((t, sw), F32)), grid=(t // tr,),
        in_specs=[spec] * ng + [sspec] * ng, out_specs=(spec, sspec),
        compiler_params=pltpu.CompilerParams(dimension_semantics=("parallel",)),
    )(*outs, *lses)


def _att_bwd_prep(do, o16, lse_tot):
    t, hd = do.shape
    sw = lse_tot.shape[1]
    pp = hd // sw
    tr = _pick(t, (256, 128))

    def body(do_ref, o_ref, l_ref, d16_ref, st_ref):
        lane = lax.broadcasted_iota(jnp.int32, (1, LANES), 1)
        d16_ref[...] = do_ref[...].astype(BF16)
        for blk in range(sw // LANES):
            scols = slice(blk * LANES, (blk + 1) * LANES)
            stats = l_ref[:, scols]
            for pair in range(pp):
                cols = slice((blk * pp + pair) * LANES, (blk * pp + pair + 1) * LANES)
                prod = do_ref[:, cols] * o_ref[:, cols].astype(F32)
                for hh in range(2):
                    hmask = (lane // ATT_HEAD_DIM) == hh
                    delta = jnp.sum(jnp.where(hmask, prod, 0.0), axis=-1, keepdims=True)
                    stats = jnp.where(lane == DELTA_LANE + 2 * pair + hh, delta, stats)
            st_ref[:, scols] = stats

    spec = pl.BlockSpec((tr, hd), lambda i: (i, 0))
    sspec = pl.BlockSpec((tr, sw), lambda i: (i, 0))
    return pl.pallas_call(
        body, name="att_bwd_prep",
        out_shape=(jax.ShapeDtypeStruct((t, hd), BF16), jax.ShapeDtypeStruct((t, sw), F32)), grid=(t // tr,),
        in_specs=[spec, spec, sspec], out_specs=(spec, sspec),
        compiler_params=pltpu.CompilerParams(dimension_semantics=("parallel",)),
    )(do, o16, lse_tot)


def _att_post(buf, parts, qkv, gq2, gk2, g):
    dilation, lu, hd = parts[0].shape
    t = dilation * lu
    tr = _pick(t, (RELAYOUT_ROWS,))
    per = tr // dilation

    def body(*refs):
        raw_ref, gq_ref, gk_ref = refs[3:6]
        o_ref, dgq_ref, dgk_ref, s_ref = refs[-4:]

        @pl.when(pl.program_id(0) == 0)
        def _():
            dgq_ref[...] = jnp.zeros_like(dgq_ref)
            dgk_ref[...] = jnp.zeros_like(dgk_ref)

        for sec, y_ref in enumerate(refs[:3]):
            gsum = jnp.zeros((1, LANES), F32)
            for c in range(hd // LANES):
                lanes = slice(c * LANES, (c + 1) * LANES)
                out_lanes = slice(sec * hd + c * LANES, sec * hd + (c + 1) * LANES)
                for r in range(dilation):
                    s_ref[pl.ds(r, per, stride=dilation), :] = y_ref[r, :, lanes].astype(F32)
                d = s_ref[...]
                if sec < 2:
                    gain = (gq_ref if sec == 0 else gk_ref)[0:1, :]
                    _, xhat, rstd = _qk_norm(raw_ref[:, out_lanes].astype(F32), gain)
                    d, part = _qk_norm_bwd(d, xhat, rstd, gain)
                    gsum = gsum + jnp.sum(part, axis=0, keepdims=True)
                o_ref[:, out_lanes] = d.astype(o_ref.dtype)
            if sec < 2:
                acc = dgq_ref if sec == 0 else dgk_ref
                acc[...] += jnp.broadcast_to(gsum, acc.shape)

    part_spec = pl.BlockSpec((dilation, per, hd), lambda i: (0, i, 0))
    slab_spec = pl.BlockSpec((tr, 3 * hd), lambda i: (i, g))
    vec_spec = pl.BlockSpec((SUBLANES, LANES), lambda i: (0, 0))
    vec_shape = jax.ShapeDtypeStruct((SUBLANES, LANES), F32)
    return pl.pallas_call(
        body, name=f"att_post_g{g}", out_shape=(jax.ShapeDtypeStruct(qkv.shape, qkv.dtype), vec_shape, vec_shape),
        grid=(t // tr,),
        in_specs=[part_spec] * 3 + [slab_spec, vec_spec, vec_spec] + (
            [] if buf is None else [pl.BlockSpec(memory_space=pl.ANY)]),
        out_specs=(slab_spec, vec_spec, vec_spec),
        scratch_shapes=[pltpu.VMEM((tr, LANES), F32)],
        input_output_aliases={} if buf is None else {6: 0},
        compiler_params=pltpu.CompilerParams(dimension_semantics=("arbitrary",)),
    )(*parts, qkv, gq2, gk2, *(() if buf is None else (buf,)))


def _att_group_bwd(view, hd, slopes, stats, do16, g):
    qk_r, v_r, v_base = view
    dilation, lu, _ = qk_r.shape
    nb = lu // ATT_BLK
    hpn = hd // LANES
    pp = math.gcd(ATT_PAIRS, hpn)
    hbn = hpn // pp
    scale = 1.0 / math.sqrt(ATT_HEAD_DIM)

    def body(q_ref, kp_ref, kc_ref, vp_ref, vc_ref, sl_ref, st_ref, do_ref, dq_ref, dk_ref, dv_ref, ck_ref, cv_ref):
        n = pl.program_id(2)
        lane = lax.broadcasted_iota(jnp.int32, (1, LANES), 1)

        @pl.when(n == 0)
        def _():
            ck_ref[...] = jnp.zeros_like(ck_ref)
            cv_ref[...] = jnp.zeros_like(cv_ref)

        @pl.when(n < nb)
        def _():
            valid, dist = _att_mask_bias(n, dilation)
            stats = st_ref[...]
            for pair in range(pp):
                cols = slice(pair * LANES, (pair + 1) * LANES)
                qn = q_ref[:, cols]
                kn16 = jnp.concatenate([kp_ref[:, cols], kc_ref[:, cols]], axis=0)
                v16 = jnp.concatenate([vp_ref[:, cols], vc_ref[:, cols]], axis=0)
                dov = do_ref[:, cols]
                dq_acc = jnp.zeros((ATT_BLK, LANES), F32)
                dk_acc = jnp.zeros((2 * ATT_BLK, LANES), F32)
                dv_acc = jnp.zeros((2 * ATT_BLK, LANES), F32)
                for hh in range(2):
                    hmask = (lane // ATT_HEAD_DIM) == hh
                    qh = jnp.where(hmask, qn, jnp.zeros_like(qn))
                    doh = jnp.where(hmask, dov, jnp.zeros_like(dov))
                    s = lax.dot_general(qh, kn16, (((1,), (1,)), ((), ())), preferred_element_type=F32) * scale
                    slope = _head_col(sl_ref[pair, 0:1, :], hmask)
                    lse = _head_col(stats, lane == 2 * pair + hh)
                    delta = _head_col(stats, lane == DELTA_LANE + 2 * pair + hh)
                    pr = jnp.exp(jnp.where(valid, s - slope * dist - lse, NEG))
                    dp = lax.dot_general(doh, v16, (((1,), (1,)), ((), ())), preferred_element_type=F32)
                    ds = (pr * (dp - delta) * scale).astype(BF16)
                    dq_acc = dq_acc + jnp.where(hmask, jnp.dot(ds, kn16, preferred_element_type=F32), 0.0)
                    dk_acc = dk_acc + lax.dot_general(ds, qh, (((0,), (0,)), ((), ())), preferred_element_type=F32)
                    dv_acc = dv_acc + lax.dot_general(pr.astype(BF16), doh, (((0,), (0,)), ((), ())),
                                                      preferred_element_type=F32)
                dq_ref[:, cols] = dq_acc.astype(dq_ref.dtype)
                dk_ref[:, cols] = (ck_ref[:, cols] + dk_acc[:ATT_BLK]).astype(dk_ref.dtype)
                dv_ref[:, cols] = (cv_ref[:, cols] + dv_acc[:ATT_BLK]).astype(dv_ref.dtype)
                ck_ref[:, cols] = dk_acc[ATT_BLK:]
                cv_ref[:, cols] = dv_acc[ATT_BLK:]

        @pl.when(n == nb)
        def _():
            dk_ref[...] = ck_ref[...].astype(dk_ref.dtype)
            dv_ref[...] = cv_ref[...].astype(dv_ref.dtype)

    width = pp * LANES
    q_out = pl.BlockSpec((None, ATT_BLK, width), lambda r, hp, n: (r, jnp.minimum(n, nb - 1), hp))
    kv_out = pl.BlockSpec((None, ATT_BLK, width), lambda r, hp, n: (r, jnp.maximum(n - 1, 0), hp))
    st_spec = pl.BlockSpec((None, ATT_BLK, LANES), lambda r, hp, n: (r, jnp.minimum(n, nb - 1), hp))
    shp = jax.ShapeDtypeStruct((dilation, lu, hd), BF16)
    return pl.pallas_call(
        body, name=f"att_bwd_g{g}", out_shape=(shp, shp, shp), grid=(dilation, hbn, nb + 1),
        in_specs=_att_specs(hd, v_base, nb, pp) + [
            pl.BlockSpec((pp, SUBLANES, LANES), lambda r, hp, n: (hp, 0, 0)), st_spec, q_out],
        out_specs=(q_out, kv_out, kv_out),
        scratch_shapes=[pltpu.VMEM((ATT_BLK, width), F32), pltpu.VMEM((ATT_BLK, width), F32)],
        compiler_params=pltpu.CompilerParams(dimension_semantics=("parallel", "parallel", "arbitrary")),
    )(qk_r, qk_r, qk_r, v_r, v_r, slopes, _to_residues(stats, dilation), _to_residues(do16, dilation))


def _att_consts(q_gain, k_gain, hd):
    heads = hd // ATT_HEAD_DIM
    gq2 = jnp.broadcast_to(jnp.tile(q_gain, 2)[None], (SUBLANES, LANES))
    gk2 = jnp.broadcast_to(jnp.tile(k_gain, 2)[None], (SUBLANES, LANES))
    sl = 2.0 ** (-8.0 * jnp.arange(1, heads + 1, dtype=F32) / heads)
    slopes = jnp.broadcast_to(jnp.repeat(sl, ATT_HEAD_DIM).reshape(hd // LANES, 1, LANES), (hd // LANES, SUBLANES, LANES))
    return gq2, gk2, slopes


def _attention_core_fwd(qkv, q_gain, k_gain):
    hd = qkv.shape[1] // (3 * len(DIL_PATTERNS))
    gq2, gk2, slopes = _att_consts(q_gain, k_gain, hd)
    outs, lses, views = [], [], []
    for g, (_, dilation) in enumerate(DIL_PATTERNS):
        v_col = (3 * g + 2) * hd
        view = (_att_pre(qkv, g, dilation, gq2, gk2),) + (
            (_to_residues(qkv, 1), v_col) if dilation == 1 else (_to_residues(qkv, dilation, v_col, hd), 0))
        o_g, l_g = _att_group_fwd(view, hd, slopes, g)
        outs.append(o_g)
        lses.append(l_g)
        views.append(view)
    o16, lse_tot = _att_merge(outs, lses)
    return o16, (qkv, views, q_gain, k_gain, o16, lse_tot)


def _attention_core_bwd(res, do):
    qkv, views, q_gain, k_gain, o16, lse_tot = res
    hd = o16.shape[1]
    gq2, gk2, slopes = _att_consts(q_gain, k_gain, hd)
    do16, stats = _att_bwd_prep(do, o16, lse_tot)
    dqkv, dgq, dgk = None, 0.0, 0.0
    for g, view in enumerate(views):
        parts = _att_group_bwd(view, hd, slopes, stats, do16, g)
        dqkv, a, b = _att_post(dqkv, parts, qkv, gq2, gk2, g)
        dgq = dgq + a[0].reshape(-1, ATT_HEAD_DIM).sum(0)
        dgk = dgk + b[0].reshape(-1, ATT_HEAD_DIM).sum(0)
    return dqkv, dgq, dgk


HALO = 8


def _silu(v):
    return v * jax.nn.sigmoid(v)


def _silu_grad(v):
    s = jax.nn.sigmoid(v)
    return s * (1.0 + v * (1.0 - s))


def _halo_rows(dtype):
    return BF16_ROWS if dtype == BF16 else HALO


def _conv_fwd(zx, conv_w, conv_b, d_inner):
    t = zx.shape[0]
    conv_dim = conv_w.shape[1]
    cb = _pick(d_inner, (1024, 512, 256, 128))
    assert conv_dim % cb == 0
    tr = _pick(t, (256, 128))
    off = d_inner // cb
    hx = _halo_rows(zx.dtype)

    def body(x_ref, h_ref, w_ref, b_ref, o_ref):
        i = pl.program_id(1)
        halo = jnp.where(i > 0, h_ref[...].astype(F32), 0.0)
        ext = jnp.concatenate([halo, x_ref[...].astype(F32)], axis=0)
        acc = jnp.broadcast_to(b_ref[...], (tr, cb))
        for k in range(CONV_WIDTH):
            s = CONV_WIDTH - 1 - k
            sh = ext if s == 0 else pltpu.roll(ext, shift=s, axis=0)
            acc = acc + w_ref[k:k + 1, :] * sh[hx:hx + tr]
        o_ref[...] = acc.astype(o_ref.dtype)

    return pl.pallas_call(
        body, name="ssm_conv_fwd", out_shape=jax.ShapeDtypeStruct((t, conv_dim), BF16),
        grid=(conv_dim // cb, t // tr),
        in_specs=[pl.BlockSpec((tr, cb), lambda j, i: (i, off + j)),
                  pl.BlockSpec((hx, cb), lambda j, i: (jnp.maximum(i * (tr // hx) - 1, 0), off + j)),
                  pl.BlockSpec((CONV_WIDTH, cb), lambda j, i: (0, j)),
                  pl.BlockSpec((1, cb), lambda j, i: (0, j))],
        out_specs=pl.BlockSpec((tr, cb), lambda j, i: (i, j)),
        compiler_params=pltpu.CompilerParams(dimension_semantics=("parallel", "parallel")),
    )(zx, zx, conv_w, conv_b.reshape(1, -1))


def _conv_bwd(zx, conv_w, dpre, dzx, d_inner, col0):
    t, width = zx.shape
    conv_dim = dpre.shape[1]
    cb = _pick(conv_dim, (1024, 512, 256, 128))
    assert (d_inner + col0) % cb == 0
    tr = _pick(t, (256, 128))
    off = (d_inner + col0) // cb
    woff = col0 // cb
    nr = t // tr
    hx = _halo_rows(zx.dtype)
    hd = _halo_rows(dpre.dtype)

    def body(x_ref, h_ref, w_ref, d_ref, dn_ref, dzx_in, dx_ref, dw_ref, db_ref):
        i = pl.program_id(1)

        @pl.when(i == 0)
        def _():
            dw_ref[...] = jnp.zeros_like(dw_ref)
            db_ref[...] = jnp.zeros_like(db_ref)

        halo = jnp.where(i > 0, h_ref[...].astype(F32), 0.0)
        ext = jnp.concatenate([halo, x_ref[...].astype(F32)], axis=0)
        d = d_ref[...].astype(F32)
        dext = jnp.concatenate([d, jnp.where(i < nr - 1, dn_ref[...].astype(F32), 0.0)], axis=0)
        dx = jnp.zeros((tr, cb), F32)
        for k in range(CONV_WIDTH):
            s = CONV_WIDTH - 1 - k
            fut = dext if s == 0 else pltpu.roll(dext, shift=tr + hd - s, axis=0)
            dx = dx + w_ref[k:k + 1, :] * fut[:tr]
            past = ext if s == 0 else pltpu.roll(ext, shift=s, axis=0)
            dw_ref[k:k + 1, :] += jnp.sum(d * past[hx:hx + tr], axis=0, keepdims=True)
        dx_ref[...] = dx.astype(dx_ref.dtype)
        db_ref[...] += jnp.sum(d, axis=0, keepdims=True)

    last_halo = t // hd - 1
    return pl.pallas_call(
        body, name=f"ssm_conv_bwd_{col0}",
        out_shape=(jax.ShapeDtypeStruct(dzx.shape, dzx.dtype), jax.ShapeDtypeStruct((CONV_WIDTH, conv_dim), F32),
                   jax.ShapeDtypeStruct((1, conv_dim), F32)),
        grid=(conv_dim // cb, nr),
        in_specs=[pl.BlockSpec((tr, cb), lambda j, i: (i, off + j)),
                  pl.BlockSpec((hx, cb), lambda j, i: (jnp.maximum(i * (tr // hx) - 1, 0), off + j)),
                  pl.BlockSpec((CONV_WIDTH, cb), lambda j, i: (0, woff + j)),
                  pl.BlockSpec((tr, cb), lambda j, i: (i, j)),
                  pl.BlockSpec((hd, cb), lambda j, i: (jnp.minimum((i + 1) * (tr // hd), last_halo), j)),
                  pl.BlockSpec(memory_space=pl.ANY)],
        out_specs=(pl.BlockSpec((tr, cb), lambda j, i: (i, off + j)),
                   pl.BlockSpec((CONV_WIDTH, cb), lambda j, i: (0, j)),
                   pl.BlockSpec((1, cb), lambda j, i: (0, j))),
        input_output_aliases={5: 0},
        compiler_params=pltpu.CompilerParams(dimension_semantics=("parallel", "arbitrary")),
    )(zx, zx, conv_w, dpre, dpre, dzx)


def _eye(n):
    return lax.broadcasted_iota(jnp.int32, (n, n), 0) == lax.broadcasted_iota(jnp.int32, (n, n), 1)


def _row_to_col(row):
    n = row.shape[1]
    return jnp.sum(jnp.where(_eye(n), row, 0.0), axis=1, keepdims=True)


def _col_to_row(col):
    n = col.shape[0]
    return jnp.sum(jnp.where(_eye(n), col, 0.0), axis=0, keepdims=True)


def _pair_lanes(c0, c1):
    lane = lax.broadcasted_iota(jnp.int32, (1, LANES), 1)
    return jnp.where(lane < SSM_HEAD_DIM, c0, c1)


def _ssd_chunk_common(pre_x_ref, pre_b_ref, pre_c_ref, dtr_ref, bias_ref, alog_ref, cs_ref):
    cl = SSD_CHUNK
    hpg = dtr_ref.shape[0]
    x = _silu(pre_x_ref[...].astype(F32))
    b16 = _silu(pre_b_ref[...].astype(F32)).astype(BF16)
    c16 = _silu(pre_c_ref[...].astype(F32)).astype(BF16)
    dt = jax.nn.softplus(dtr_ref[...] + bias_ref[...])
    a = -jnp.exp(alog_ref[...])
    li = lax.broadcasted_iota(jnp.int32, (cl, cl), 0)
    si = lax.broadcasted_iota(jnp.int32, (cl, cl), 1)
    upper = (li <= si).astype(F32)
    cs_ref[0:hpg, :] = jnp.dot(dt * a, upper, precision=lax.Precision.HIGHEST, preferred_element_type=F32)
    cs_ref[hpg:2 * hpg, :] = dt
    g = lax.dot_general(c16, b16, (((1,), (1,)), ((), ())), preferred_element_type=F32)
    return x, b16, c16, dt, a, g, li >= si


def _ssd_fwd(pre, dtT, bias, alog, dskip_lanes, d_inner):
    t = pre.shape[0]
    cl = SSD_CHUNK
    nc = t // cl
    ng = SSM_GROUPS
    hpg = dtT.shape[1]
    gw = hpg * SSM_HEAD_DIM
    assert d_inner == ng * gw and hpg % 2 == 0
    bo = d_inner // SSM_STATE

    def body(px_ref, pb_ref, pc_ref, dtr_ref, bias_ref, alog_ref, dsk_ref, y_ref, st_ref, s_ref, cs_ref):
        c = pl.program_id(1)

        @pl.when(c == 0)
        def _():
            s_ref[...] = jnp.zeros_like(s_ref)

        x, b16, c16, dt, a, g, causal = _ssd_chunk_common(px_ref, pb_ref, pc_ref, dtr_ref, bias_ref, alog_ref, cs_ref)
        st_ref[...] = s_ref[...]
        yoff = lax.dot_general(c16, s_ref[...].astype(BF16), (((1,), (1,)), ((), ())), preferred_element_type=F32)
        xde_parts = []
        for j in range(hpg // 2):
            cols = slice(j * LANES, (j + 1) * LANES)
            xp = x[:, cols]
            dcol, ecol, ocol, ms = [], [], [], []
            for hh in range(2):
                h = 2 * j + hh
                cs_row = cs_ref[h:h + 1, :]
                cs_col = _row_to_col(cs_row)
                dcol.append(_row_to_col(cs_ref[hpg + h:hpg + h + 1, :]))
                ecol.append(jnp.exp(cs_ref[h:h + 1, cl - 1:cl] - cs_col))
                ocol.append(jnp.exp(cs_col))
                lm = jnp.where(causal, jnp.exp(jnp.minimum(cs_col - cs_row, 0.0)), 0.0)
                ms.append((g * lm).astype(BF16))
            xd = xp * _pair_lanes(dcol[0], dcol[1])
            xd16 = xd.astype(BF16)
            yd = _pair_lanes(1.0, 0.0) * jnp.dot(ms[0], xd16, preferred_element_type=F32) \
                + _pair_lanes(0.0, 1.0) * jnp.dot(ms[1], xd16, preferred_element_type=F32)
            y_ref[:, cols] = yd + yoff[:, cols] * _pair_lanes(ocol[0], ocol[1]) + xp * dsk_ref[0:1, cols]
            xde_parts.append((xd * _pair_lanes(ecol[0], ecol[1])).astype(BF16))
        new = lax.dot_general(jnp.concatenate(xde_parts, axis=1), b16, (((0,), (0,)), ((), ())),
                              preferred_element_type=F32)
        for h in range(hpg):
            rows = slice(h * SSM_HEAD_DIM, (h + 1) * SSM_HEAD_DIM)
            s_ref[rows, :] = s_ref[rows, :] * jnp.exp(cs_ref[h:h + 1, cl - 1:cl]) + new[rows, :]

    vec = lambda n: pl.BlockSpec((None, hpg, n), lambda gi, c: (gi, 0, 0))
    return pl.pallas_call(
        body, name="ssd_fwd",
        out_shape=(jax.ShapeDtypeStruct((t, d_inner), F32), jax.ShapeDtypeStruct((ng, nc, gw, SSM_STATE), F32)),
        grid=(ng, nc),
        in_specs=[pl.BlockSpec((cl, gw), lambda gi, c: (c, gi)),
                  pl.BlockSpec((cl, SSM_STATE), lambda gi, c: (c, bo + gi)),
                  pl.BlockSpec((cl, SSM_STATE), lambda gi, c: (c, bo + ng + gi)),
                  pl.BlockSpec((None, hpg, cl), lambda gi, c: (gi, 0, c)),
                  vec(1), vec(1),
                  pl.BlockSpec((1, gw), lambda gi, c: (0, gi))],
        out_specs=(pl.BlockSpec((cl, gw), lambda gi, c: (c, gi)),
                   pl.BlockSpec((None, None, gw, SSM_STATE), lambda gi, c: (gi, c, 0, 0))),
        scratch_shapes=[pltpu.VMEM((gw, SSM_STATE), F32), pltpu.VMEM((2 * hpg, cl), F32)],
        compiler_params=pltpu.CompilerParams(dimension_semantics=("parallel", "arbitrary")),
    )(pre, pre, pre, dtT, bias, alog, dskip_lanes)


def _ssd_bwd(pre, dtT, bias, alog, dskip_lanes, states, dy, d_inner):
    t, conv_dim = pre.shape
    cl = SSD_CHUNK
    nc = t // cl
    ng = SSM_GROUPS
    hpg = dtT.shape[1]
    gw = hpg * SSM_HEAD_DIM
    bo = d_inner // SSM_STATE

    def body(px_ref, pb_ref, pc_ref, dtr_ref, bias_ref, alog_ref, dsk_ref, st_ref, dy_ref,
             dx_ref, db_ref, dc_ref, ddt_ref, acc_ref, dsk_out, ds_ref, cs_ref, dcs_ref):
        c = pl.program_id(1)

        @pl.when(c == 0)
        def _():
            ds_ref[...] = jnp.zeros_like(ds_ref)
            acc_ref[...] = jnp.zeros_like(acc_ref)
            dsk_out[...] = jnp.zeros_like(dsk_out)

        x, b16, c16, dt, a, g, causal = _ssd_chunk_common(px_ref, pb_ref, pc_ref, dtr_ref, bias_ref, alog_ref, cs_ref)
        s_prev = st_ref[...]
        s16 = s_prev.astype(BF16)
        ds = ds_ref[...]
        ds16 = ds.astype(BF16)
        dyv = dy_ref[...]
        yoff = lax.dot_general(c16, s16, (((1,), (1,)), ((), ())), preferred_element_type=F32)
        bds = lax.dot_general(b16, ds16, (((1,), (1,)), ((), ())), preferred_element_type=F32)
        dg = jnp.zeros((cl, cl), F32)
        xde_parts, dye_parts = [], []
        lane = lax.broadcasted_iota(jnp.int32, (1, LANES), 1)
        for j in range(hpg // 2):
            cols = slice(j * LANES, (j + 1) * LANES)
            xp, dyp = x[:, cols], dyv[:, cols]
            dcol, ecol, ocol, lms = [], [], [], []
            for hh in range(2):
                h = 2 * j + hh
                cs_row = cs_ref[h:h + 1, :]
                cs_col = _row_to_col(cs_row)
                dcol.append(_row_to_col(cs_ref[hpg + h:hpg + h + 1, :]))
                ecol.append(jnp.exp(cs_ref[h:h + 1, cl - 1:cl] - cs_col))
                ocol.append(jnp.exp(cs_col))
                lms.append(jnp.where(causal, jnp.exp(jnp.minimum(cs_col - cs_row, 0.0)), 0.0))
            dlanes, elanes, olanes = _pair_lanes(*dcol), _pair_lanes(*ecol), _pair_lanes(*ocol)
            xd = xp * dlanes
            xd16 = xd.astype(BF16)
            xde = xd * elanes
            yoffp = yoff[:, cols] * olanes
            bdsp = bds[:, cols]
            dxd = bdsp * elanes
            for hh in range(2):
                h = 2 * j + hh
                hmask = (lane // SSM_HEAD_DIM) == hh
                dyh16 = jnp.where(hmask, dyp, 0.0).astype(BF16)
                m = g * lms[hh]
                dm = lax.dot_general(dyh16, xd16, (((1,), (1,)), ((), ())), preferred_element_type=F32)
                w = dm * m
                dg = dg + dm * lms[hh]
                dxd = dxd + lax.dot_general(m.astype(BF16), dyh16, (((0,), (0,)), ((), ())),
                                            preferred_element_type=F32)
                term = jnp.sum(jnp.where(hmask, xde * bdsp, 0.0), axis=1, keepdims=True)
                dcs_col = (jnp.sum(w, axis=1, keepdims=True)
                           + jnp.sum(jnp.where(hmask, dyp * yoffp, 0.0), axis=1, keepdims=True) - term)
                rows = slice(h * SSM_HEAD_DIM, (h + 1) * SSM_HEAD_DIM)
                dec = jnp.exp(cs_ref[h:h + 1, cl - 1:cl])
                tail = jnp.sum(term, axis=0, keepdims=True) + dec * jnp.sum(
                    jnp.sum(s_prev[rows, :] * ds[rows, :], axis=1, keepdims=True), axis=0, keepdims=True)
                last = lax.broadcasted_iota(jnp.int32, (1, cl), 1) == cl - 1
                dcs_ref[h:h + 1, :] = _col_to_row(dcs_col) - jnp.sum(w, axis=0, keepdims=True) + jnp.where(last, tail, 0.0)
                dcs_ref[hpg + h:hpg + h + 1, :] = _col_to_row(
                    jnp.sum(jnp.where(hmask, dxd * xp, 0.0), axis=1, keepdims=True))
            dx_act = dxd * dlanes + dyp * dsk_ref[0:1, cols]
            dx_ref[:, cols] = (dx_act * _silu_grad(px_ref[:, cols].astype(F32))).astype(dx_ref.dtype)
            dsk_out[0:1, cols] += jnp.sum(dyp * xp, axis=0, keepdims=True)
            xde_parts.append(xde.astype(BF16))
            dye_parts.append((dyp * olanes).astype(BF16))
        xde16 = jnp.concatenate(xde_parts, axis=1)
        dye16 = jnp.concatenate(dye_parts, axis=1)
        dg16 = dg.astype(BF16)
        dc_act = jnp.dot(dg16, b16, preferred_element_type=F32) + jnp.dot(dye16, s16, preferred_element_type=F32)
        db_act = lax.dot_general(dg16, c16, (((0,), (0,)), ((), ())), preferred_element_type=F32) \
            + jnp.dot(xde16, ds16, preferred_element_type=F32)
        dc_ref[...] = (dc_act * _silu_grad(pc_ref[...].astype(F32))).astype(dc_ref.dtype)
        db_ref[...] = (db_act * _silu_grad(pb_ref[...].astype(F32))).astype(db_ref.dtype)
        ds_new = lax.dot_general(dye16, c16, (((0,), (0,)), ((), ())), preferred_element_type=F32)
        for h in range(hpg):
            rows = slice(h * SSM_HEAD_DIM, (h + 1) * SSM_HEAD_DIM)
            ds_ref[rows, :] = ds[rows, :] * jnp.exp(cs_ref[h:h + 1, cl - 1:cl]) + ds_new[rows, :]
        li = lax.broadcasted_iota(jnp.int32, (cl, cl), 0)
        si = lax.broadcasted_iota(jnp.int32, (cl, cl), 1)
        d_adt = jnp.dot(dcs_ref[0:hpg, :], (li >= si).astype(F32), precision=lax.Precision.HIGHEST,
                        preferred_element_type=F32)
        ddt = d_adt * a + dcs_ref[hpg:2 * hpg, :]
        ddt_raw = ddt * jax.nn.sigmoid(dtr_ref[...] + bias_ref[...])
        ddt_ref[...] = ddt_raw
        acc_ref[0:hpg, :] += d_adt * dt
        acc_ref[hpg:2 * hpg, :] += ddt_raw

    rc = lambda c: nc - 1 - c
    vec = lambda n: pl.BlockSpec((None, hpg, n), lambda gi, c: (gi, 0, 0))
    x_spec = pl.BlockSpec((cl, gw), lambda gi, c: (rc(c), gi))
    b_spec = pl.BlockSpec((cl, SSM_STATE), lambda gi, c: (rc(c), bo + gi))
    c_spec = pl.BlockSpec((cl, SSM_STATE), lambda gi, c: (rc(c), bo + ng + gi))
    dt_spec = pl.BlockSpec((None, hpg, cl), lambda gi, c: (gi, 0, rc(c)))
    return pl.pallas_call(
        body, name="ssd_bwd",
        out_shape=(jax.ShapeDtypeStruct((t, d_inner), BF16), jax.ShapeDtypeStruct((t, ng * SSM_STATE), BF16),
                   jax.ShapeDtypeStruct((t, ng * SSM_STATE), BF16), jax.ShapeDtypeStruct(dtT.shape, F32),
                   jax.ShapeDtypeStruct((ng, 2 * hpg, cl), F32), jax.ShapeDtypeStruct((1, d_inner), F32)),
        grid=(ng, nc),
        in_specs=[x_spec, b_spec, c_spec, dt_spec, vec(1), vec(1),
                  pl.BlockSpec((1, gw), lambda gi, c: (0, gi)),
                  pl.BlockSpec((None, None, gw, SSM_STATE), lambda gi, c: (gi, rc(c), 0, 0)),
                  x_spec],
        out_specs=(x_spec, pl.BlockSpec((cl, SSM_STATE), lambda gi, c: (rc(c), gi)),
                   pl.BlockSpec((cl, SSM_STATE), lambda gi, c: (rc(c), gi)), dt_spec,
                   pl.BlockSpec((None, 2 * hpg, cl), lambda gi, c: (gi, 0, 0)),
                   pl.BlockSpec((1, gw), lambda gi, c: (0, gi))),
        scratch_shapes=[pltpu.VMEM((gw, SSM_STATE), F32), pltpu.VMEM((2 * hpg, cl), F32),
                        pltpu.VMEM((2 * hpg, cl), F32)],
        compiler_params=pltpu.CompilerParams(dimension_semantics=("parallel", "arbitrary")),
    )(pre, pre, pre, dtT, bias, alog, dskip_lanes, states, dy)


def _gate_norm_fwd(y, zx, norm_w, d_inner):
    t = y.shape[0]
    tr = _pick(t, (256, 128))
    gs = d_inner // SSM_GROUPS

    def body(y_ref, z_ref, w_ref, o_ref):
        for gi in range(SSM_GROUPS):
            cols = slice(gi * gs, (gi + 1) * gs)
            v = y_ref[:, cols] * _silu(z_ref[:, cols].astype(F32))
            r = lax.rsqrt(jnp.mean(v * v, axis=-1, keepdims=True) + NORM_EPS)
            o_ref[:, cols] = (v * r * w_ref[0:1, cols]).astype(BF16)

    spec = pl.BlockSpec((tr, d_inner), lambda i: (i, 0))
    return pl.pallas_call(
        body, name="ssm_gate_norm_fwd", out_shape=jax.ShapeDtypeStruct((t, d_inner), BF16), grid=(t // tr,),
        in_specs=[spec, spec, pl.BlockSpec((1, d_inner), lambda i: (0, 0))], out_specs=spec,
        compiler_params=pltpu.CompilerParams(dimension_semantics=("parallel",)),
    )(y, zx, norm_w.reshape(1, -1))


def _gate_norm_bwd(y, zx, norm_w, dout, d_inner):
    t, width = zx.shape
    tr = _pick(t, (256, 128))
    gs = d_inner // SSM_GROUPS

    def body(y_ref, z_ref, w_ref, do_ref, dy_ref, dz_ref, dw_ref):
        @pl.when(pl.program_id(0) == 0)
        def _():
            dw_ref[...] = jnp.zeros_like(dw_ref)

        for gi in range(SSM_GROUPS):
            cols = slice(gi * gs, (gi + 1) * gs)
            yv, zv = y_ref[:, cols], z_ref[:, cols].astype(F32)
            sz = _silu(zv)
            v = yv * sz
            r = lax.rsqrt(jnp.mean(v * v, axis=-1, keepdims=True) + NORM_EPS)
            vhat = v * r
            dn = do_ref[:, cols].astype(F32)
            dw_ref[0:1, cols] += jnp.sum(dn * vhat, axis=0, keepdims=True)
            dvh = dn * w_ref[0:1, cols]
            dv = r * (dvh - vhat * jnp.mean(dvh * vhat, axis=-1, keepdims=True))
            dy_ref[:, cols] = dv * sz
            dz_ref[:, cols] = (dv * yv * _silu_grad(zv)).astype(dz_ref.dtype)

    spec = pl.BlockSpec((tr, d_inner), lambda i: (i, 0))
    wspec = pl.BlockSpec((1, d_inner), lambda i: (0, 0))
    return pl.pallas_call(
        body, name="ssm_gate_norm_bwd",
        out_shape=(jax.ShapeDtypeStruct((t, d_inner), F32), jax.ShapeDtypeStruct((t, width), zx.dtype),
                   jax.ShapeDtypeStruct((1, d_inner), F32)),
        grid=(t // tr,),
        in_specs=[spec, spec, wspec, spec], out_specs=(spec, spec, wspec),
        compiler_params=pltpu.CompilerParams(dimension_semantics=("arbitrary",)),
    )(y, zx, norm_w.reshape(1, -1), dout)


def _ssm_small(dt_raw, dt_bias, a_log, d_skip):
    heads = dt_raw.shape[1]
    hpg = heads // SSM_GROUPS
    dtT = dt_raw.T.reshape(SSM_GROUPS, hpg, -1)
    return (dtT, dt_bias.reshape(SSM_GROUPS, hpg, 1), a_log.reshape(SSM_GROUPS, hpg, 1),
            jnp.repeat(d_skip, SSM_HEAD_DIM).reshape(1, -1))


def _ssm_core_fwd(zx, dt_raw, conv_w, conv_b, dt_bias, a_log, d_skip, norm_w):
    d_inner = norm_w.shape[0]
    pre = _conv_fwd(zx, conv_w, conv_b, d_inner)
    dtT, bias, alog, dsk = _ssm_small(dt_raw, dt_bias, a_log, d_skip)
    y, states = _ssd_fwd(pre, dtT, bias, alog, dsk, d_inner)
    out = _gate_norm_fwd(y, zx, norm_w, d_inner)
    return out, (zx, dt_raw, conv_w, dt_bias, a_log, d_skip, norm_w, pre, y, states)


def _ssm_core_bwd(res, dout):
    zx, dt_raw, conv_w, dt_bias, a_log, d_skip, norm_w, pre, y, states = res
    d_inner = norm_w.shape[0]
    heads = dt_raw.shape[1]
    dy, dzx, dnorm = _gate_norm_bwd(y, zx, norm_w, dout, d_inner)
    dtT, bias, alog, dsk = _ssm_small(dt_raw, dt_bias, a_log, d_skip)
    dx, db, dc, ddtT, acc, dsk_l = _ssd_bwd(pre, dtT, bias, alog, dsk, states, dy, d_inner)
    dws, dbs, col0 = [], [], 0
    for part in (dx, db, dc):
        dzx, dw_part, db_part = _conv_bwd(zx, conv_w, part, dzx, d_inner, col0)
        dws.append(dw_part)
        dbs.append(db_part)
        col0 += part.shape[1]
    dconv_w, dconv_b = jnp.concatenate(dws, axis=1), jnp.concatenate(dbs, axis=1)
    d_dt_raw = ddtT.reshape(heads, -1).T
    hpg = heads // SSM_GROUPS
    da = acc[:, :hpg].sum(-1).reshape(heads)
    d_bias = acc[:, hpg:].sum(-1).reshape(heads)
    d_alog = da * (-jnp.exp(a_log))
    d_dskip = dsk_l.reshape(heads, SSM_HEAD_DIM).sum(-1)
    return dzx, d_dt_raw, dconv_w, dconv_b.reshape(-1), d_bias, d_alog, d_dskip, dnorm.reshape(-1)


def _rows_call(body, name, ins, outs, acc_outs=(), rows=256):
    t = max(a.shape[0] for a in ins)
    tr = _pick(t, (rows, 128, 64, 32, 16, 8))

    def spec(a):
        if a.shape[0] == t:
            return pl.BlockSpec((tr, a.shape[1]), lambda i: (i, 0))
        return pl.BlockSpec(a.shape, lambda i: (0, 0))

    return pl.pallas_call(
        body, name=name, out_shape=tuple(outs) + tuple(acc_outs), grid=(t // tr,),
        in_specs=[spec(a) for a in ins],
        out_specs=tuple(spec(a) for a in outs) + tuple(pl.BlockSpec(a.shape, lambda i: (0, 0)) for a in acc_outs),
        compiler_params=pltpu.CompilerParams(dimension_semantics=("arbitrary" if acc_outs else "parallel",)),
    )(*ins)


def _rms_fwd(x, gain, after=None):
    def body(x_ref, g_ref, *rest):
        v = x_ref[...]
        rest[-1][...] = (v * lax.rsqrt(jnp.mean(v * v, axis=-1, keepdims=True) + NORM_EPS) * g_ref[...]).astype(BF16)

    ins = [x, gain.reshape(1, -1)] + ([] if after is None else [after])
    (h,) = _rows_call(body, "rms_fwd", ins, [jax.ShapeDtypeStruct(x.shape, BF16)])
    return h


def _rms_bwd(x, gain, dh, dres, after):
    def body(x_ref, g_ref, dh_ref, dr_ref, *rest):
        dx_ref, dg_ref = rest[-2:]

        @pl.when(pl.program_id(0) == 0)
        def _():
            dg_ref[...] = jnp.zeros_like(dg_ref)

        v = x_ref[...]
        r = lax.rsqrt(jnp.mean(v * v, axis=-1, keepdims=True) + NORM_EPS)
        vhat = v * r
        d = dh_ref[...].astype(F32)
        dg_ref[...] += jnp.sum(d * vhat, axis=0, keepdims=True)
        dvh = d * g_ref[...]
        dx_ref[...] = dr_ref[...] + r * (dvh - vhat * jnp.mean(dvh * vhat, axis=-1, keepdims=True))

    ins = [x, gain.reshape(1, -1), dh, dres] + ([] if after is None else [after])
    dx, dg = _rows_call(body, "rms_bwd", ins, [jax.ShapeDtypeStruct(x.shape, F32)],
                        [jax.ShapeDtypeStruct((1, x.shape[1]), F32)])
    return dx, dg.reshape(gain.shape)


def _swiglu_fwd(gu):
    t, f2 = gu.shape
    f = f2 // 2

    def body(gu_ref, o_ref):
        o_ref[...] = (_silu(gu_ref[:, :f].astype(F32)) * gu_ref[:, f:].astype(F32)).astype(BF16)

    (act,) = _rows_call(body, "swiglu_fwd", [gu], [jax.ShapeDtypeStruct((t, f), BF16)])
    return act


def _swiglu_bwd(gu, dact):
    t, f2 = gu.shape
    f = f2 // 2

    def body(gu_ref, d_ref, o_ref):
        g, u, d = gu_ref[:, :f].astype(F32), gu_ref[:, f:].astype(F32), d_ref[...].astype(F32)
        o_ref[:, :f] = (d * u * _silu_grad(g)).astype(BF16)
        o_ref[:, f:] = (d * _silu(g)).astype(BF16)

    (dgu,) = _rows_call(body, "swiglu_bwd", [gu, dact], [jax.ShapeDtypeStruct((t, f2), BF16)])
    return dgu


def _ple_fwd(x, gl, ple):
    def body(x_ref, g_ref, p_ref, o_ref):
        o_ref[...] = x_ref[...] + jax.nn.sigmoid(g_ref[...].astype(F32)) * p_ref[...].astype(F32)

    (out,) = _rows_call(body, "ple_fwd", [x, gl, ple], [jax.ShapeDtypeStruct(x.shape, F32)])
    return out


def _ple_bwd(gl, ple, dout):
    def body(g_ref, p_ref, d_ref, dg_ref, dp_ref):
        s, d = jax.nn.sigmoid(g_ref[...].astype(F32)), d_ref[...]
        dg_ref[...] = (d * p_ref[...].astype(F32) * s * (1.0 - s)).astype(BF16)
        dp_ref[...] = (d * s).astype(BF16)

    shp = jax.ShapeDtypeStruct(gl.shape, BF16)
    return _rows_call(body, "ple_bwd", [gl, ple, dout], [shp, shp])


def _loss_fwd(y, target):
    inv = 1.0 / y.shape[1]

    def body(y_ref, t_ref, d_ref, l_ref):
        @pl.when(pl.program_id(0) == 0)
        def _():
            l_ref[...] = jnp.zeros_like(l_ref)

        e = y_ref[...] - t_ref[...]
        d_ref[...] = e * inv
        part = jnp.sum(jnp.sum(e * e, axis=1, keepdims=True), axis=0, keepdims=True) * (0.5 * inv)
        l_ref[...] += jnp.broadcast_to(part, l_ref.shape)

    dy, acc = _rows_call(body, "loss_fwd", [y, target], [jax.ShapeDtypeStruct(y.shape, F32)],
                         [jax.ShapeDtypeStruct((SUBLANES, LANES), F32)])
    return acc[0, 0], dy


def local_step(small, fetch, emit, x, p, target):
    depth = small['norm_mix'].shape[0]
    ssm_small = ('ssm_conv_w', 'ssm_conv_b', 'ssm_dt_bias', 'ssm_a_log', 'ssm_d_skip', 'ssm_norm_w')
    saved = []
    for i in range(depth):
        j = i // 2
        s = {'x': x}
        wm, token = fetch(('mix', i), x)
        h = s['h'] = _rms_fwd(x, small['norm_mix'][i], token)
        if i % 2 == 0:
            n_main = wm['ssm_w_in'].shape[0] - small['ssm_dt_bias'].shape[1]
            zx = _mm(h, wm['ssm_w_in'][:n_main], tb=True, out_dtype=BF16, name="ssm_in_fwd")
            more, token = fetch(('out', i), zx)
            wm = {**wm, **more}
            dt_raw = _mm(h, wm['ssm_w_in'][n_main:], tb=True, after=token, name="ssm_dt_fwd")
            y, s['mix'] = _ssm_core_fwd(zx, dt_raw, wm['ssm_conv_w'], *[small[n][j] for n in ssm_small[1:]])
            x = _mm(y, wm['ssm_w_out'], add=x, name="ssm_out_fwd")
        else:
            qkv = _mm(h, wm['att_w_qkv'], tb=True, out_dtype=BF16, name="att_qkv_fwd")
            y, s['mix'] = _attention_core_fwd(qkv, small['att_q_norm'][j], small['att_k_norm'][j])
            x = _mm(y, wm['att_w_o'], add=x, name="att_o_fwd")
        s['wm'], s['y'], s['x1'] = wm, y, x
        wf, token = fetch(('ffn', i), x)
        s['wf'] = wf
        h2 = s['h2'] = _rms_fwd(x, small['norm_ffn'][i], token)
        gu = s['gu'] = _mm(h2, wf['ffn_w_gu'], tb=True, out_dtype=BF16, name="ffn_gu_fwd")
        act = s['act'] = _swiglu_fwd(gu)
        x = s['x2'] = _mm(act, wf['ffn_w_down'], add=x, name="ffn_down_fwd")
        gl = s['gl'] = _mm(x, wf['ple_w_gate'], out_dtype=BF16, name="ple_gate_fwd")
        ple = s['ple'] = _mm(p[i], wf['ple_w_proj'], tb=True, out_dtype=BF16, name="ple_proj_fwd")
        x = _ple_fwd(x, gl, ple)
        saved.append(s)
    loss, dx = _loss_fwd(x, target)

    g = {n: [None] * small[n].shape[0] for n in small}
    for i in reversed(range(depth)):
        j = i // 2
        s = saved[i]
        wm, wf = s['wm'], s['wf']
        gf = {}
        dgl, dple = _ple_bwd(s['gl'], s['ple'], dx)
        gf['ple_w_proj'] = _mm(dple, p[i], ta=True, out_dtype=BF16, name="ple_proj_dw")
        gf['ple_w_gate'] = _mm(s['x2'], dgl, ta=True, out_dtype=BF16, name="ple_gate_dw")
        dx = _mm(dgl, wf['ple_w_gate'], tb=True, add=dx, name="ple_gate_da")
        dact = _mm(dx, wf['ffn_w_down'], tb=True, out_dtype=BF16, name="ffn_down_da")
        gf['ffn_w_down'] = _mm(s['act'], dx, ta=True, out_dtype=BF16, name="ffn_down_dw")
        dgu = _swiglu_bwd(s['gu'], dact)
        dh2 = _mm(dgu, wf['ffn_w_gu'], out_dtype=BF16, name="ffn_gu_da")
        gf['ffn_w_gu'] = _mm(dgu, s['h2'], ta=True, out_dtype=BF16, name="ffn_gu_dw")
        dx, g['norm_ffn'][i] = _rms_bwd(s['x1'], small['norm_ffn'][i], dh2, dx, emit(('ffn', i), gf))
        gm = {}
        if i % 2 == 0:
            n_main = wm['ssm_w_in'].shape[0] - small['ssm_dt_bias'].shape[1]
            dyn = _mm(dx, wm['ssm_w_out'], tb=True, out_dtype=BF16, name="ssm_out_da")
            gm['ssm_w_out'] = _mm(s['y'], dx, ta=True, out_dtype=BF16, name="ssm_out_dw")
            dzx, d_dt, *sg = _ssm_core_bwd(s['mix'], dyn)
            for n, v in zip(ssm_small, sg):
                g[n][j] = v
            dh = _mm(d_dt, wm['ssm_w_in'][n_main:], name="ssm_dt_da")
            dh = _mm(dzx, wm['ssm_w_in'][:n_main], add=dh, out_dtype=BF16, name="ssm_in_da")
            gm['ssm_w_in'] = jnp.concatenate([_mm(dzx, s['h'], ta=True, out_dtype=BF16, name="ssm_in_dw"),
                                              _mm(d_dt, s['h'], ta=True, out_dtype=BF16, name="ssm_dt_dw")], axis=0)
        else:
            do = _mm(dx, wm['att_w_o'], tb=True, name="att_o_da")
            gm['att_w_o'] = _mm(s['y'], dx, ta=True, out_dtype=BF16, name="att_o_dw")
            dqkv, g['att_q_norm'][j], g['att_k_norm'][j] = _attention_core_bwd(s['mix'], do)
            dh = _mm(dqkv, wm['att_w_qkv'], out_dtype=BF16, name="att_qkv_da")
            gm['att_w_qkv'] = _mm(dqkv, s['h'], ta=True, out_dtype=BF16, name="att_qkv_dw")
        dx, g['norm_mix'][i] = _rms_bwd(s['x'], small['norm_mix'][i], dh, dx, emit(('mix', i), gm))
    return loss, dx, {n: jnp.stack(v) for n, v in g.items()}


CHANNEL_WEIGHTS = ('ffn_w_gate', 'ffn_w_up', 'ffn_w_down', 'ple_w_proj', 'ple_w_gate')


def _stages(depth):
    gather, scatter = {}, {}
    for i in range(depth):
        j = i // 2
        if i % 2 == 0:
            gather['mix', i] = [('ssm_w_in', j), ('ssm_conv_w', j)]
            gather['out', i] = [('ssm_w_out', j)]
            scatter['mix', i] = [('ssm_w_in', j), ('ssm_w_out', j)]
        else:
            gather['mix', i] = scatter['mix', i] = [('att_w_qkv', j), ('att_w_o', j)]
        gather['ffn', i] = scatter['ffn', i] = [(n, i) for n in CHANNEL_WEIGHTS]
    return gather, scatter


def _pack_plan(shapes, width, members):
    plan, off = [], 0
    for name, lyr in members:
        _, r, c = shapes[name]
        if name in COL_SHARDED:
            r, c = c, r
        if name == 'ssm_conv_w':
            pr = -(-2 * r * c // width)
        else:
            assert (r * c) % width == 0, (name, r, c)
            pr = r * c // width
        plan.append((name, lyr, r, c, pr, off))
        off += _round_up(pr, BF16_ROWS)
    return plan, off


def _small_plan(shapes):
    plan, off = [], 0
    for name in SMALL:
        n = math.prod(shapes[name])
        plan.append((name, n, off))
        off += n
    return plan, _round_up(off, SUBLANES * LANES)


def kernel(x, p, norm_mix, norm_ffn, ssm_w_in, ssm_conv_w, ssm_conv_b, ssm_dt_bias, ssm_a_log, ssm_d_skip, ssm_norm_w, ssm_w_out, att_w_qkv, att_q_norm, att_k_norm, att_w_o, ffn_w_gate, ffn_w_up, ffn_w_down, ple_w_proj, ple_w_gate, loss_target, m_norm_mix, m_norm_ffn, m_ssm_w_in, m_ssm_conv_w, m_ssm_conv_b, m_ssm_dt_bias, m_ssm_a_log, m_ssm_d_skip, m_ssm_norm_w, m_ssm_w_out, m_att_w_qkv, m_att_q_norm, m_att_k_norm, m_att_w_o, m_ffn_w_gate, m_ffn_w_up, m_ffn_w_down, m_ple_w_proj, m_ple_w_gate, v_norm_mix, v_norm_ffn, v_ssm_w_in, v_ssm_conv_w, v_ssm_conv_b, v_ssm_dt_bias, v_ssm_a_log, v_ssm_d_skip, v_ssm_norm_w, v_ssm_w_out, v_att_w_qkv, v_att_q_norm, v_att_k_norm, v_att_w_o, v_ffn_w_gate, v_ffn_w_up, v_ffn_w_down, v_ple_w_proj, v_ple_w_gate):
    given = dict(locals())
    w_in = {n: given[n] for n in WEIGHTS}
    m_in = {n: given["m_" + n] for n in WEIGHTS}
    v_in = {n: given["v_" + n] for n in WEIGHTS}
    width = x.shape[-1]
    depth = norm_mix.shape[0]

    gather_members, scatter_members = _stages(depth)
    shapes = {n: w_in[n].shape for n in BIG + ('ssm_conv_w',)}
    plans = {key: _pack_plan(shapes, width, members)[0] for key, members in gather_members.items()}
    splans = {key: _pack_plan(shapes, width, members)[0] for key, members in scatter_members.items()}
    order = list(gather_members)

    def pack_weights(stage):
        pieces = []
        for name, layer, r, c, pr, off in plans[stage]:
            blk = w_in[name][layer]
            if name == 'ssm_conv_w':
                blk = lax.bitcast_convert_type(blk.reshape(-1), BF16).reshape(-1)
                blk = jnp.pad(blk, (0, pr * width - blk.shape[0]))
            elif name in COL_SHARDED:
                blk = blk.T
            blk = blk.astype(BF16).reshape(pr, width)
            pieces.append(jnp.pad(blk, ((0, _round_up(pr, BF16_ROWS) - pr), (0, 0))))
        return jnp.concatenate(pieces, axis=0)

    packed = [pack_weights(order[0])]
    pending = [exchange_start(packed[0], True, "gather_start_0")]
    packed += [pack_weights(stage) for stage in order[1:]]

    def fetch(stage, after):
        k = order.index(stage)
        handle, token = pending[k]
        land = exchange_wait(handle, [token] + packed[1:] if k == 0 else after, True, f"gather_wait_{k}")
        token = None
        if k + 1 < len(order):
            pending.append(exchange_start(packed[k + 1], True, f"gather_start_{k + 1}", land))
            token = pending[-1][1]
        got = {}
        for name, layer, r, c, pr, off in plans[stage]:
            piece = land[:, off:off + pr]
            if name == 'ssm_conv_w':
                taps, chans = w_in[name].shape[1:]
                bits = piece.reshape(N_DEV, -1)[:, :2 * taps * chans].reshape(N_DEV, taps * chans, 2)
                piece = lax.bitcast_convert_type(bits, F32).reshape(N_DEV, taps, chans)
                got[name] = piece.transpose(1, 0, 2).reshape(taps, N_DEV * chans)
            else:
                got[name] = piece.reshape(N_DEV * r, c)
        if 'ffn_w_gate' in got:
            got['ffn_w_gu'] = jnp.concatenate([got.pop('ffn_w_gate'), got.pop('ffn_w_up')], axis=0)
        return got, token

    scatters = {}

    def emit(stage, grads):
        grads = dict(grads)
        if 'ffn_w_gu' in grads:
            hidden = grads['ffn_w_gu'].shape[0] // 2
            grads['ffn_w_gate'], grads['ffn_w_up'] = grads['ffn_w_gu'][:hidden], grads['ffn_w_gu'][hidden:]
        pieces = []
        for name, layer, r, c, pr, off in splans[stage]:
            g = grads[name].reshape(N_DEV, pr, width)
            pieces.append(jnp.pad(g, ((0, 0), (0, _round_up(pr, BF16_ROWS) - pr), (0, 0))))
        scatters[stage], token = exchange_start(jnp.concatenate(pieces, axis=1), False,
                                                f"scatter_start_{len(scatters)}")
        return token

    small = {n: w_in[n] for n in SMALL}
    cs = w_in['ssm_conv_w'].shape[2]
    loss_local, gx, gw = local_step(small, fetch, emit, x[0], p[:, 0], loss_target[0])
    loss = lax.psum(loss_local, ("x", "y", "c"))

    parts = {}
    for k, (stage, handle) in enumerate(scatters.items()):
        received = exchange_wait(handle, gx, False, f"scatter_wait_{k}")
        gsum = sum_slots(received, f"sum_grads_{k}")
        for name, layer, r, c, pr, off in splans[stage]:
            g = gsum[off:off + pr].reshape(r, c)
            parts[name, layer] = g.T if name in COL_SHARDED else g
    grads = {n: jnp.stack([parts[n, layer] for layer in range(w_in[n].shape[0])]) for n in BIG}

    splan, stotal = _small_plan({n: gw[n].shape for n in SMALL})
    svec = jnp.concatenate([gw[n].reshape(-1) for n, _, _ in splan])
    svec = jnp.pad(svec, (0, stotal - svec.shape[0])).reshape(stotal // LANES, LANES)
    _, ssum = all_gather_sum_small(svec, "sum_small_grads")
    ssum = ssum.reshape(-1)
    for name, n, off in splan:
        grads[name] = ssum[off:off + n].reshape(gw[name].shape)
    me = _me()
    grads['ssm_conv_w'] = lax.dynamic_slice_in_dim(grads['ssm_conv_w'], me * cs, cs, axis=2)

    delta, new_m, new_v = {}, {}, {}
    for name in BIG:
        shp = w_in[name].shape
        flat = lambda a: a.reshape(-1, shp[-1])
        d, nm, nv = adamw(flat(w_in[name]), flat(grads[name]), flat(m_in[name]), flat(v_in[name]), "adamw_" + name)
        delta[name], new_m[name], new_v[name] = d.reshape(shp), nm.reshape(shp), nv.reshape(shp)
    splan2, stotal2 = _small_plan({n: w_in[n].shape for n in SMALL})

    def pack_small(src):
        vec = jnp.concatenate([src[n].reshape(-1) for n, _, _ in splan2])
        return jnp.pad(vec, (0, stotal2 - vec.shape[0]), constant_values=1.0).reshape(stotal2 // LANES, LANES)

    sd, snm, snv = adamw(pack_small(w_in), pack_small(grads), pack_small(m_in), pack_small(v_in), "adamw_small")
    for name, n, off in splan2:
        shp = w_in[name].shape
        delta[name] = sd.reshape(-1)[off:off + n].reshape(shp)
        new_m[name] = snm.reshape(-1)[off:off + n].reshape(shp)
        new_v[name] = snv.reshape(-1)[off:off + n].reshape(shp)

    return (loss, gx[None], *[grads[n] for n in WEIGHTS], *[delta[n] for n in WEIGHTS],
            *[new_m[n] for n in WEIGHTS], *[new_v[n] for n in WEIGHTS])
```

```python
import functools
import math

import jax
import jax.numpy as jnp
from jax import lax
from jax.experimental import pallas as pl
from jax.experimental.pallas import tpu as pltpu

F32 = jnp.float32
BF16 = jnp.bfloat16
N_DEV = 8
MESH = pl.DeviceIdType.MESH

SSM_HEAD_DIM = 64
SSM_GROUPS = 4
SSM_STATE = 128
CONV_WIDTH = 4
SSD_CHUNK = 128
ATT_HEAD_DIM = 64
DIL_PATTERNS = ((128, 1), (512, 4), (2048, 16))
NORM_EPS = 1e-6
ADAM_LR = 0.001
ADAM_B1 = 0.9
ADAM_B2 = 0.999
ADAM_EPS = 1e-08
ADAM_WD = 0.01
ADAM_STEP = 10

BF16_ROWS = 16
LANES = 128
SUBLANES = 8

WEIGHTS = ['norm_mix', 'norm_ffn', 'ssm_w_in', 'ssm_conv_w', 'ssm_conv_b', 'ssm_dt_bias', 'ssm_a_log', 'ssm_d_skip',
           'ssm_norm_w', 'ssm_w_out', 'att_w_qkv', 'att_q_norm', 'att_k_norm', 'att_w_o', 'ffn_w_gate', 'ffn_w_up',
           'ffn_w_down', 'ple_w_proj', 'ple_w_gate']
COL_SHARDED = ('ssm_w_in', 'att_w_qkv', 'ffn_w_gate', 'ffn_w_up', 'ple_w_proj')
ROW_SHARDED = ('ssm_w_out', 'att_w_o', 'ffn_w_down', 'ple_w_gate')
BIG = COL_SHARDED + ROW_SHARDED
SMALL = ('norm_mix', 'norm_ffn', 'ssm_conv_w', 'ssm_conv_b', 'ssm_dt_bias', 'ssm_a_log', 'ssm_d_skip', 'ssm_norm_w',
         'att_q_norm', 'att_k_norm')


def _pick(n, cands):
    for c in cands:
        if n % c == 0:
            return c
    return n


def _round_up(n, m):
    return -(-n // m) * m


MM_TILES = (1024, 1408, 512, 256, 128)
MM_VMEM_BYTES = 48 * 1024 * 1024


def _mm(a, b, *, ta=False, tb=False, out_dtype=F32, add=None, after=None, name):
    k_dim, m_dim = (a.shape if ta else a.shape[::-1])
    n_dim = b.shape[0] if tb else b.shape[1]
    assert (b.shape[1] if tb else b.shape[0]) == k_dim, (a.shape, b.shape, ta, tb)
    tm = _pick(m_dim, MM_TILES)
    tn = _pick(n_dim, MM_TILES)
    tk = _pick(k_dim, MM_TILES)
    nk = k_dim // tk
    a_spec = pl.BlockSpec((tk, tm), lambda i, j, k: (k, i)) if ta else pl.BlockSpec((tm, tk), lambda i, j, k: (i, k))
    b_spec = pl.BlockSpec((tn, tk), lambda i, j, k: (j, k)) if tb else pl.BlockSpec((tk, tn), lambda i, j, k: (k, j))
    o_spec = pl.BlockSpec((tm, tn), lambda i, j, k: (i, j))
    dims = (((0 if ta else 1,), (1 if tb else 0,)), ((), ()))
    has_add = add is not None
    n_in = 2 + has_add + (after is not None)

    def body(*refs):
        a_ref, b_ref = refs[:2]
        o_ref = refs[n_in]

        def dot():
            return lax.dot_general(a_ref[...].astype(BF16), b_ref[...].astype(BF16), dims,
                                   preferred_element_type=F32)

        def finish(acc):
            if has_add:
                acc = acc + refs[2][...].astype(F32)
            o_ref[...] = acc.astype(o_ref.dtype)

        if nk == 1:
            finish(dot())
            return
        acc_ref = refs[n_in + 1]
        k = pl.program_id(2)

        @pl.when(k == 0)
        def _():
            acc_ref[...] = dot()

        @pl.when((k > 0) & (k < nk - 1))
        def _():
            acc_ref[...] += dot()

        @pl.when(k == nk - 1)
        def _():
            finish(acc_ref[...] + dot())

    return pl.pallas_call(
        body, name=f"{name}_{m_dim}x{n_dim}x{k_dim}",
        out_shape=jax.ShapeDtypeStruct((m_dim, n_dim), out_dtype),
        grid=(m_dim // tm, n_dim // tn, nk),
        in_specs=[a_spec, b_spec] + ([o_spec] if has_add else []) + (
            [] if after is None else [pl.BlockSpec(memory_space=pl.ANY)]),
        out_specs=o_spec,
        scratch_shapes=[] if nk == 1 else [pltpu.VMEM((tm, tn), F32)],
        compiler_params=pltpu.CompilerParams(dimension_semantics=("parallel", "parallel", "arbitrary"),
                                             vmem_limit_bytes=MM_VMEM_BYTES),
    )(*((a, b) + ((add,) if has_add else ()) + (() if after is None else (after,))))


def _me():
    return 4 * lax.axis_index("x") + 2 * lax.axis_index("y") + lax.axis_index("c")


def _peer(j):
    x, y, c = lax.axis_index("x"), lax.axis_index("y"), lax.axis_index("c")
    px = 1 - x if j & 4 else x
    py = 1 - y if j & 2 else y
    pc = 1 - c if j & 1 else c
    return (px, py, pc), 4 * px + 2 * py + pc


def _exchange_body(src_of, dst_ref, send_sems, recv_sems, local_sem):
    me = _me()
    mine = pltpu.make_async_copy(src_of(me), dst_ref.at[me], local_sem)
    mine.start()
    sends = []
    for j in range(1, N_DEV):
        peer, pidx = _peer(j)
        cp = pltpu.make_async_remote_copy(src_ref=src_of(pidx), dst_ref=dst_ref.at[me], send_sem=send_sems.at[j - 1],
                                          recv_sem=recv_sems.at[j - 1], device_id=peer, device_id_type=MESH)
        cp.start()
        sends.append(cp)
    for j in range(1, N_DEV):
        peer, pidx = _peer(j)
        pltpu.make_async_remote_copy(src_ref=src_of(pidx), dst_ref=dst_ref.at[pidx], send_sem=send_sems.at[j - 1],
                                     recv_sem=recv_sems.at[j - 1], device_id=peer, device_id_type=MESH).wait_recv()
    for cp in sends:
        cp.wait_send()
    mine.wait()


_EXCHANGE_SCRATCH = [pltpu.SemaphoreType.DMA((N_DEV - 1,)), pltpu.SemaphoreType.DMA((N_DEV - 1,)),
                     pltpu.SemaphoreType.DMA]


_HBM = pl.BlockSpec(memory_space=pltpu.HBM)
_SEM = pl.BlockSpec(memory_space=pltpu.SEMAPHORE)


def _split_copies(src_ref, gather, land_ref, send_sems, recv_sems):
    me = _me()
    pairs = []
    for j in range(1, N_DEV):
        peer, pidx = _peer(j)

        def make(slot, peer=peer, pidx=pidx, j=j):
            return pltpu.make_async_remote_copy(
                src_ref=src_ref if gather else src_ref.at[pidx], dst_ref=land_ref.at[slot],
                send_sem=send_sems.at[j - 1], recv_sem=recv_sems.at[j - 1], device_id=peer, device_id_type=MESH)

        pairs.append((make(me), make(pidx)))
    return pairs


def exchange_start(src, gather, name, after=None):
    land_shape = ((N_DEV,) + src.shape) if gather else src.shape
    has_after = after is not None

    def body(*refs):
        src_ref, land_ref = refs[:2]
        send_sems, recv_sems = refs[2 + has_after:4 + has_after]
        for send, _ in _split_copies(src_ref, gather, land_ref, send_sems, recv_sems):
            send.start()
        refs[-1][...] = jnp.zeros_like(refs[-1])

    sem = pltpu.SemaphoreType.DMA((N_DEV - 1,))
    send_sems, recv_sems, src_thru, land, token = pl.pallas_call(
        body, name=name,
        out_shape=(sem, sem, pltpu.HBM(src.shape, src.dtype), pltpu.HBM(land_shape, src.dtype),
                   jax.ShapeDtypeStruct((SUBLANES, LANES), F32)),
        in_specs=(_HBM, _HBM) + ((pl.BlockSpec(memory_space=pl.ANY),) if has_after else ()),
        out_specs=(_SEM, _SEM, _HBM, _HBM, pl.BlockSpec(memory_space=pltpu.VMEM)),
        input_output_aliases={0: 2, 1: 3},
        compiler_params=pltpu.CompilerParams(has_side_effects=pltpu.SideEffectType.DATAFLOW_SIDE_EFFECTING),
    )(pltpu.with_memory_space_constraint(src, pltpu.HBM),
      pltpu.with_memory_space_constraint(lax.empty(land_shape, src.dtype), pltpu.HBM),
      *((after,) if has_after else ()))
    return (send_sems, recv_sems, src_thru, land), token


def exchange_wait(handle, after, gather, name):
    send_sems, recv_sems, src_thru, land = handle
    after = tuple(after) if isinstance(after, (tuple, list)) else (after,)

    def body(src_ref, land_ref, send_sems, recv_sems, *rest):
        for _, arrival in _split_copies(src_ref, gather, land_ref, send_sems, recv_sems):
            arrival.wait_send()
            arrival.wait_recv()

    src_done, got = pl.pallas_call(
        body, name=name,
        out_shape=(pltpu.HBM(src_thru.shape, src_thru.dtype), pltpu.HBM(land.shape, land.dtype)),
        in_specs=(_HBM, _HBM, _SEM, _SEM) + (pl.BlockSpec(memory_space=pl.ANY),) * len(after), out_specs=(_HBM, _HBM),
        input_output_aliases={0: 0, 1: 1},
        compiler_params=pltpu.CompilerParams(has_side_effects=pltpu.SideEffectType.DATAFLOW_SIDE_EFFECTING),
    )(src_thru, land, send_sems, recv_sems, *after)
    mine = src_done if gather else lax.dynamic_index_in_dim(src_done, _me(), 0, keepdims=False)
    return lax.dynamic_update_index_in_dim(got, mine, _me(), 0)


def all_gather_sum_small(v, name):
    def body(x_ref, out_ref, sum_ref, send_sems, recv_sems, local_sem):
        _exchange_body(lambda k: x_ref, out_ref, send_sems, recv_sems, local_sem)
        acc = out_ref[0]
        for k in range(1, N_DEV):
            acc = acc + out_ref[k]
        sum_ref[...] = acc

    return pl.pallas_call(
        body, name=name,
        out_shape=(jax.ShapeDtypeStruct((N_DEV,) + v.shape, v.dtype), jax.ShapeDtypeStruct(v.shape, v.dtype)),
        in_specs=[pl.BlockSpec(memory_space=pltpu.VMEM)],
        out_specs=(pl.BlockSpec(memory_space=pltpu.VMEM), pl.BlockSpec(memory_space=pltpu.VMEM)),
        scratch_shapes=list(_EXCHANGE_SCRATCH),
    )(v)


def sum_slots(slots, name):
    _, p_dim, c_dim = slots.shape
    tp = next(tp for tp in range(512, 0, -BF16_ROWS) if p_dim % tp == 0)

    def body(x_ref, o_ref):
        acc = x_ref[0].astype(F32)
        for k in range(1, N_DEV):
            acc = acc + x_ref[k].astype(F32)
        o_ref[...] = acc

    return pl.pallas_call(
        body, name=name,
        out_shape=jax.ShapeDtypeStruct((p_dim, c_dim), F32),
        grid=(p_dim // tp,),
        in_specs=[pl.BlockSpec((N_DEV, tp, c_dim), lambda i: (0, i, 0))],
        out_specs=pl.BlockSpec((tp, c_dim), lambda i: (i, 0)),
        compiler_params=pltpu.CompilerParams(dimension_semantics=("parallel",)),
    )(slots)


def adamw(w, g, m, v, name):
    rows, cols = w.shape
    tr = _pick(rows, (256, 128, 64, 32, 16, 8))

    def body(w_ref, g_ref, m_ref, v_ref, d_ref, nm_ref, nv_ref):
        gv = g_ref[...]
        nm = ADAM_B1 * m_ref[...] + (1.0 - ADAM_B1) * gv
        nv = ADAM_B2 * v_ref[...] + (1.0 - ADAM_B2) * (gv * gv)
        m_hat = nm / (1.0 - ADAM_B1 ** ADAM_STEP)
        v_hat = nv / (1.0 - ADAM_B2 ** ADAM_STEP)
        d_ref[...] = -ADAM_LR * (m_hat / (jnp.sqrt(v_hat) + ADAM_EPS) + ADAM_WD * w_ref[...])
        nm_ref[...] = nm
        nv_ref[...] = nv

    spec = pl.BlockSpec((tr, cols), lambda i: (i, 0))
    shp = jax.ShapeDtypeStruct((rows, cols), F32)
    return pl.pallas_call(
        body, name=name, out_shape=(shp, shp, shp), grid=(rows // tr,),
        in_specs=[spec] * 4, out_specs=(spec,) * 3,
        compiler_params=pltpu.CompilerParams(dimension_semantics=("parallel",)),
    )(w, g, m, v)


ATT_BLK = 128
NEG = -1e30


def _head_sums(v):
    li = lax.broadcasted_iota(jnp.int32, (LANES, LANES), 0) // ATT_HEAD_DIM
    lj = lax.broadcasted_iota(jnp.int32, (LANES, LANES), 1) // ATT_HEAD_DIM
    ones = (li == lj).astype(BF16)
    hi = v.astype(BF16)
    lo = (v - hi.astype(F32)).astype(BF16)
    return jnp.dot(hi, ones, preferred_element_type=F32) + jnp.dot(lo, ones, preferred_element_type=F32)


def _head_col(v, hmask):
    return jnp.max(jnp.where(hmask, v, -jnp.inf), axis=-1, keepdims=True)


def _qk_norm(raw, gain2):
    rstd = lax.rsqrt(_head_sums(raw * raw) * (1.0 / ATT_HEAD_DIM) + NORM_EPS)
    xhat = raw * rstd
    return xhat * gain2, xhat, rstd


def _qk_norm_bwd(dn, xhat, rstd, gain2):
    dxh = dn * gain2
    return rstd * (dxh - xhat * (_head_sums(dxh * xhat) * (1.0 / ATT_HEAD_DIM))), dn * xhat


def _att_mask_bias(n, dilation):
    qi = lax.broadcasted_iota(jnp.int32, (ATT_BLK, 2 * ATT_BLK), 0)
    ki = lax.broadcasted_iota(jnp.int32, (ATT_BLK, 2 * ATT_BLK), 1)
    dist = qi + ATT_BLK - ki
    valid = (dist >= 0) & (dist <= ATT_BLK) & ((n > 0) | (ki >= ATT_BLK))
    return valid, (dilation * dist).astype(F32)


ATT_PAIRS = 4
RELAYOUT_ROWS = 512
RELAYOUT_COLS = 512


def _to_residues(x, dilation, col0=0, cols=None):
    t = x.shape[0]
    cols = x.shape[1] if cols is None else cols
    if dilation == 1 and col0 == 0 and cols == x.shape[1]:
        return x.reshape(1, t, cols)
    tr = _pick(t, (RELAYOUT_ROWS,))
    tc = _pick(cols, (RELAYOUT_COLS, 256, 128))
    per = tr // dilation
    assert tr % dilation == 0 and col0 % tc == 0

    def body(x_ref, o_ref, s_ref):
        for c in range(tc // LANES):
            lanes = slice(c * LANES, (c + 1) * LANES)
            s_ref[c] = x_ref[:, lanes].astype(F32)
            for r in range(dilation):
                o_ref[r, :, lanes] = s_ref[c, pl.ds(r, per, stride=dilation), :].astype(o_ref.dtype)

    return pl.pallas_call(
        body, name=f"to_residues_{dilation}", out_shape=jax.ShapeDtypeStruct((dilation, t // dilation, cols), x.dtype),
        grid=(t // tr, cols // tc),
        in_specs=[pl.BlockSpec((tr, tc), lambda i, j: (i, col0 // tc + j))],
        out_specs=pl.BlockSpec((dilation, per, tc), lambda i, j: (0, i, j)),
        scratch_shapes=[pltpu.VMEM((tc // LANES, tr, LANES), F32)],
        compiler_params=pltpu.CompilerParams(dimension_semantics=("parallel", "parallel")),
    )(x)


def _from_residues(y):
    dilation, lu, cols = y.shape
    t = dilation * lu
    if dilation == 1:
        return y.reshape(t, cols)
    tr = _pick(t, (RELAYOUT_ROWS,))
    tc = _pick(cols, (RELAYOUT_COLS, 256, 128))
    per = tr // dilation

    def body(y_ref, o_ref, s_ref):
        for c in range(tc // LANES):
            lanes = slice(c * LANES, (c + 1) * LANES)
            for r in range(dilation):
                s_ref[c, pl.ds(r, per, stride=dilation), :] = y_ref[r, :, lanes].astype(F32)
            o_ref[:, lanes] = s_ref[c].astype(o_ref.dtype)

    return pl.pallas_call(
        body, name=f"from_residues_{dilation}", out_shape=jax.ShapeDtypeStruct((t, cols), y.dtype),
        grid=(t // tr, cols // tc),
        in_specs=[pl.BlockSpec((dilation, per, tc), lambda i, j: (0, i, j))],
        out_specs=pl.BlockSpec((tr, tc), lambda i, j: (i, j)),
        scratch_shapes=[pltpu.VMEM((tc // LANES, tr, LANES), F32)],
        compiler_params=pltpu.CompilerParams(dimension_semantics=("parallel", "parallel")),
    )(y)


def _att_specs(hd, v_base, nb, pp):
    width = pp * LANES
    assert hd % width == 0 and v_base % width == 0

    def spec(base, shift):
        def imap(r, hp, n):
            row = jnp.minimum(n, nb - 1) if shift == 0 else jnp.maximum(n - 1, 0)
            return (r, row, base // width + hp)
        return pl.BlockSpec((None, ATT_BLK, width), imap)

    return [spec(0, 0), spec(hd, 1), spec(hd, 0), spec(v_base, 1), spec(v_base, 0)]


DELTA_LANE = 64


def _att_pre(qkv, g, dilation, gq2, gk2):
    t, width = qkv.shape
    hd = width // (3 * len(DIL_PATTERNS))
    tr = _pick(t, (RELAYOUT_ROWS,))
    tc = _pick(hd, (RELAYOUT_COLS, 256, 128))
    per = tr // dilation
    assert tr % dilation == 0 and (g * 3 * hd) % tc == 0

    def body(x_ref, gq_ref, gk_ref, o_ref, s_ref):
        gain = jnp.where(pl.program_id(1) * tc < hd, gq_ref[0:1, :], gk_ref[0:1, :])
        for c in range(tc // LANES):
            lanes = slice(c * LANES, (c + 1) * LANES)
            s_ref[c] = _qk_norm(x_ref[:, lanes].astype(F32), gain)[0]
            for r in range(dilation):
                o_ref[r, :, lanes] = s_ref[c, pl.ds(r, per, stride=dilation), :].astype(o_ref.dtype)

    vec_spec = pl.BlockSpec((SUBLANES, LANES), lambda i, j: (0, 0))
    return pl.pallas_call(
        body, name=f"att_pre_g{g}", out_shape=jax.ShapeDtypeStruct((dilation, t // dilation, 2 * hd), qkv.dtype),
        grid=(t // tr, 2 * hd // tc),
        in_specs=[pl.BlockSpec((tr, tc), lambda i, j: (i, g * 3 * hd // tc + j)), vec_spec, vec_spec],
        out_specs=pl.BlockSpec((dilation, per, tc), lambda i, j: (0, i, j)),
        scratch_shapes=[pltpu.VMEM((tc // LANES, tr, LANES), F32)],
        compiler_params=pltpu.CompilerParams(dimension_semantics=("parallel", "parallel")),
    )(qkv, gq2, gk2)


def _att_group_fwd(view, hd, slopes, g):
    qk_r, v_r, v_base = view
    dilation, lu, _ = qk_r.shape
    nb = lu // ATT_BLK
    assert nb * ATT_BLK == lu and hd % LANES == 0
    hpn = hd // LANES
    pp = math.gcd(ATT_PAIRS, hpn)
    scale = 1.0 / math.sqrt(ATT_HEAD_DIM)

    def body(q_ref, kp_ref, kc_ref, vp_ref, vc_ref, sl_ref, o_ref, l_ref):
        n = pl.program_id(2)
        lane = lax.broadcasted_iota(jnp.int32, (1, LANES), 1)
        first = (lane // ATT_HEAD_DIM) == 0
        valid, dist = _att_mask_bias(n, dilation)
        stats = jnp.zeros((ATT_BLK, LANES), F32)
        for pair in range(pp):
            cols = slice(pair * LANES, (pair + 1) * LANES)
            qn = q_ref[:, cols]
            kn16 = jnp.concatenate([kp_ref[:, cols], kc_ref[:, cols]], axis=0)
            v16 = jnp.concatenate([vp_ref[:, cols], vc_ref[:, cols]], axis=0)
            outs = []
            for hh in range(2):
                hmask = (lane // ATT_HEAD_DIM) == hh
                qh = jnp.where(hmask, qn, jnp.zeros_like(qn))
                s = lax.dot_general(qh, kn16, (((1,), (1,)), ((), ())), preferred_element_type=F32) * scale
                slope = _head_col(sl_ref[pair, 0:1, :], hmask)
                logits = jnp.where(valid, s - slope * dist, NEG)
                mx = jnp.max(logits, axis=-1, keepdims=True)
                pexp = jnp.exp(logits - mx)
                den = jnp.sum(pexp, axis=-1, keepdims=True)
                outs.append(jnp.dot(pexp.astype(BF16), v16, preferred_element_type=F32) / den)
                stats = jnp.where(lane == 2 * pair + hh, mx + jnp.log(den), stats)
            o_ref[:, cols] = jnp.where(first, outs[0], outs[1]).astype(BF16)
        l_ref[...] = stats

    out_spec = pl.BlockSpec((None, ATT_BLK, pp * LANES), lambda r, hp, n: (r, n, hp))
    stat_spec = pl.BlockSpec((None, ATT_BLK, LANES), lambda r, hp, n: (r, n, hp))
    o, lse = pl.pallas_call(
        body, name=f"att_fwd_g{g}",
        out_shape=(jax.ShapeDtypeStruct((dilation, lu, hd), BF16),
                   jax.ShapeDtypeStruct((dilation, lu, hpn // pp * LANES), F32)),
        grid=(dilation, hpn // pp, nb),
        in_specs=_att_specs(hd, v_base, nb, pp) + [pl.BlockSpec((pp, SUBLANES, LANES), lambda r, hp, n: (hp, 0, 0))],
        out_specs=(out_spec, stat_spec),
        compiler_params=pltpu.CompilerParams(dimension_semantics=("parallel", "parallel", "arbitrary")),
    )(qk_r, qk_r, qk_r, v_r, v_r, slopes)
    return _from_residues(o), _from_residues(lse)


def _att_merge(outs, lses):
    t, hd = outs[0].shape
    sw = lses[0].shape[1]
    pp = hd // sw
    tr = _pick(t, (256, 128))
    ng = len(outs)

    def body(*refs):
        o_refs, l_refs, o16_ref, lt_ref = refs[:ng], refs[ng:2 * ng], refs[2 * ng], refs[2 * ng + 1]
        lane = lax.broadcasted_iota(jnp.int32, (1, LANES), 1)
        first = (lane // ATT_HEAD_DIM) == 0
        for blk in range(sw // LANES):
            scols = slice(blk * LANES, (blk + 1) * LANES)
            stats = jnp.zeros((tr, LANES), F32)
            for pair in range(pp):
                cols = slice((blk * pp + pair) * LANES, (blk * pp + pair + 1) * LANES)
                weights = []
                for hh in range(2):
                    pick = lane == 2 * pair + hh
                    ls = [_head_col(r[:, scols], pick) for r in l_refs]
                    mx = functools.reduce(jnp.maximum, ls)
                    es = [jnp.exp(l - mx) for l in ls]
                    den = functools.reduce(jnp.add, es)
                    weights.append([e / den for e in es])
                    stats = jnp.where(pick, mx + jnp.log(den), stats)
                acc = jnp.zeros((tr, LANES), F32)
                for gi in range(ng):
                    acc = acc + jnp.where(first, weights[0][gi], weights[1][gi]) * o_refs[gi][:, cols].astype(F32)
                o16_ref[:, cols] = acc.astype(BF16)
            lt_ref[:, scols] = stats

    spec = pl.BlockSpec((tr, hd), lambda i: (i, 0))
    sspec = pl.BlockSpec((tr, sw), lambda i: (i, 0))
    return pl.pallas_call(
        body, name="att_merge",
        out_shape=(jax.ShapeDtypeStruct((t, hd), BF16), jax.ShapeDtypeStruct((t, sw), F32)), grid=(t // tr,),
        in_specs=[spec] * ng + [sspec] * ng, out_specs=(spec, sspec),
        compiler_params=pltpu.CompilerParams(dimension_semantics=("parallel",)),
    )(*outs, *lses)


def _att_bwd_prep(do, o16, lse_tot):
    t, hd = do.shape
    sw = lse_tot.shape[1]
    pp = hd // sw
    tr = _pick(t, (256, 128))

    def body(do_ref, o_ref, l_ref, d16_ref, st_ref):
        lane = lax.broadcasted_iota(jnp.int32, (1, LANES), 1)
        d16_ref[...] = do_ref[...].astype(BF16)
        for blk in range(sw // LANES):
            scols = slice(blk * LANES, (blk + 1) * LANES)
            stats = l_ref[:, scols]
            for pair in range(pp):
                cols = slice((blk * pp + pair) * LANES, (blk * pp + pair + 1) * LANES)
                prod = do_ref[:, cols] * o_ref[:, cols].astype(F32)
                for hh in range(2):
                    hmask = (lane // ATT_HEAD_DIM) == hh
                    delta = jnp.sum(jnp.where(hmask, prod, 0.0), axis=-1, keepdims=True)
                    stats = jnp.where(lane == DELTA_LANE + 2 * pair + hh, delta, stats)
            st_ref[:, scols] = stats

    spec = pl.BlockSpec((tr, hd), lambda i: (i, 0))
    sspec = pl.BlockSpec((tr, sw), lambda i: (i, 0))
    return pl.pallas_call(
        body, name="att_bwd_prep",
        out_shape=(jax.ShapeDtypeStruct((t, hd), BF16), jax.ShapeDtypeStruct((t, sw), F32)), grid=(t // tr,),
        in_specs=[spec, spec, sspec], out_specs=(spec, sspec),
        compiler_params=pltpu.CompilerParams(dimension_semantics=("parallel",)),
    )(do, o16, lse_tot)


def _att_post(buf, parts, qkv, gq2, gk2, g):
    dilation, lu, hd = parts[0].shape
    t = dilation * lu
    tr = _pick(t, (RELAYOUT_ROWS,))
    per = tr // dilation

    def body(*refs):
        raw_ref, gq_ref, gk_ref = refs[3:6]
        o_ref, dgq_ref, dgk_ref, s_ref = refs[-4:]

        @pl.when(pl.program_id(0) == 0)
        def _():
            dgq_ref[...] = jnp.zeros_like(dgq_ref)
            dgk_ref[...] = jnp.zeros_like(dgk_ref)

        for sec, y_ref in enumerate(refs[:3]):
            gsum = jnp.zeros((1, LANES), F32)
            for c in range(hd // LANES):
                lanes = slice(c * LANES, (c + 1) * LANES)
                out_lanes = slice(sec * hd + c * LANES, sec * hd + (c + 1) * LANES)
                for r in range(dilation):
                    s_ref[pl.ds(r, per, stride=dilation), :] = y_ref[r, :, lanes].astype(F32)
                d = s_ref[...]
                if sec < 2:
                    gain = (gq_ref if sec == 0 else gk_ref)[0:1, :]
                    _, xhat, rstd = _qk_norm(raw_ref[:, out_lanes].astype(F32), gain)
                    d, part = _qk_norm_bwd(d, xhat, rstd, gain)
                    gsum = gsum + jnp.sum(part, axis=0, keepdims=True)
                o_ref[:, out_lanes] = d.astype(o_ref.dtype)
            if sec < 2:
                acc = dgq_ref if sec == 0 else dgk_ref
                acc[...] += jnp.broadcast_to(gsum, acc.shape)

    part_spec = pl.BlockSpec((dilation, per, hd), lambda i: (0, i, 0))
    slab_spec = pl.BlockSpec((tr, 3 * hd), lambda i: (i, g))
    vec_spec = pl.BlockSpec((SUBLANES, LANES), lambda i: (0, 0))
    vec_shape = jax.ShapeDtypeStruct((SUBLANES, LANES), F32)
    return pl.pallas_call(
        body, name=f"att_post_g{g}", out_shape=(jax.ShapeDtypeStruct(qkv.shape, qkv.dtype), vec_shape, vec_shape),
        grid=(t // tr,),
        in_specs=[part_spec] * 3 + [slab_spec, vec_spec, vec_spec] + (
            [] if buf is None else [pl.BlockSpec(memory_space=pl.ANY)]),
        out_specs=(slab_spec, vec_spec, vec_spec),
        scratch_shapes=[pltpu.VMEM((tr, LANES), F32)],
        input_output_aliases={} if buf is None else {6: 0},
        compiler_params=pltpu.CompilerParams(dimension_semantics=("arbitrary",)),
    )(*parts, qkv, gq2, gk2, *(() if buf is None else (buf,)))


def _att_group_bwd(view, hd, slopes, stats, do16, g):
    qk_r, v_r, v_base = view
    dilation, lu, _ = qk_r.shape
    nb = lu // ATT_BLK
    hpn = hd // LANES
    pp = math.gcd(ATT_PAIRS, hpn)
    hbn = hpn // pp
    scale = 1.0 / math.sqrt(ATT_HEAD_DIM)

    def body(q_ref, kp_ref, kc_ref, vp_ref, vc_ref, sl_ref, st_ref, do_ref, dq_ref, dk_ref, dv_ref, ck_ref, cv_ref):
        n = pl.program_id(2)
        lane = lax.broadcasted_iota(jnp.int32, (1, LANES), 1)

        @pl.when(n == 0)
        def _():
            ck_ref[...] = jnp.zeros_like(ck_ref)
            cv_ref[...] = jnp.zeros_like(cv_ref)

        @pl.when(n < nb)
        def _():
            valid, dist = _att_mask_bias(n, dilation)
            stats = st_ref[...]
            for pair in range(pp):
                cols = slice(pair * LANES, (pair + 1) * LANES)
                qn = q_ref[:, cols]
                kn16 = jnp.concatenate([kp_ref[:, cols], kc_ref[:, cols]], axis=0)
                v16 = jnp.concatenate([vp_ref[:, cols], vc_ref[:, cols]], axis=0)
                dov = do_ref[:, cols]
                dq_acc = jnp.zeros((ATT_BLK, LANES), F32)
                dk_acc = jnp.zeros((2 * ATT_BLK, LANES), F32)
                dv_acc = jnp.zeros((2 * ATT_BLK, LANES), F32)
                for hh in range(2):
                    hmask = (lane // ATT_HEAD_DIM) == hh
                    qh = jnp.where(hmask, qn, jnp.zeros_like(qn))
                    doh = jnp.where(hmask, dov, jnp.zeros_like(dov))
                    s = lax.dot_general(qh, kn16, (((1,), (1,)), ((), ())), preferred_element_type=F32) * scale
                    slope = _head_col(sl_ref[pair, 0:1, :], hmask)
                    lse = _head_col(stats, lane == 2 * pair + hh)
                    delta = _head_col(stats, lane == DELTA_LANE + 2 * pair + hh)
                    pr = jnp.exp(jnp.where(valid, s - slope * dist - lse, NEG))
                    dp = lax.dot_general(doh, v16, (((1,), (1,)), ((), ())), preferred_element_type=F32)
                    ds = (pr * (dp - delta) * scale).astype(BF16)
                    dq_acc = dq_acc + jnp.where(hmask, jnp.dot(ds, kn16, preferred_element_type=F32), 0.0)
                    dk_acc = dk_acc + lax.dot_general(ds, qh, (((0,), (0,)), ((), ())), preferred_element_type=F32)
                    dv_acc = dv_acc + lax.dot_general(pr.astype(BF16), doh, (((0,), (0,)), ((), ())),
                                                      preferred_element_type=F32)
                dq_ref[:, cols] = dq_acc.astype(dq_ref.dtype)
                dk_ref[:, cols] = (ck_ref[:, cols] + dk_acc[:ATT_BLK]).astype(dk_ref.dtype)
                dv_ref[:, cols] = (cv_ref[:, cols] + dv_acc[:ATT_BLK]).astype(dv_ref.dtype)
                ck_ref[:, cols] = dk_acc[ATT_BLK:]
                cv_ref[:, cols] = dv_acc[ATT_BLK:]

        @pl.when(n == nb)
        def _():
            dk_ref[...] = ck_ref[...].astype(dk_ref.dtype)
            dv_ref[...] = cv_ref[...].astype(dv_ref.dtype)

    width = pp * LANES
    q_out = pl.BlockSpec((None, ATT_BLK, width), lambda r, hp, n: (r, jnp.minimum(n, nb - 1), hp))
    kv_out = pl.BlockSpec((None, ATT_BLK, width), lambda r, hp, n: (r, jnp.maximum(n - 1, 0), hp))
    st_spec = pl.BlockSpec((None, ATT_BLK, LANES), lambda r, hp, n: (r, jnp.minimum(n, nb - 1), hp))
    shp = jax.ShapeDtypeStruct((dilation, lu, hd), BF16)
    return pl.pallas_call(
        body, name=f"att_bwd_g{g}", out_shape=(shp, shp, shp), grid=(dilation, hbn, nb + 1),
        in_specs=_att_specs(hd, v_base, nb, pp) + [
            pl.BlockSpec((pp, SUBLANES, LANES), lambda r, hp, n: (hp, 0, 0)), st_spec, q_out],
        out_specs=(q_out, kv_out, kv_out),
        scratch_shapes=[pltpu.VMEM((ATT_BLK, width), F32), pltpu.VMEM((ATT_BLK, width), F32)],
        compiler_params=pltpu.CompilerParams(dimension_semantics=("parallel", "parallel", "arbitrary")),
    )(qk_r, qk_r, qk_r, v_r, v_r, slopes, _to_residues(stats, dilation), _to_residues(do16, dilation))


def _att_consts(q_gain, k_gain, hd):
    heads = hd // ATT_HEAD_DIM
    gq2 = jnp.broadcast_to(jnp.tile(q_gain, 2)[None], (SUBLANES, LANES))
    gk2 = jnp.broadcast_to(jnp.tile(k_gain, 2)[None], (SUBLANES, LANES))
    sl = 2.0 ** (-8.0 * jnp.arange(1, heads + 1, dtype=F32) / heads)
    slopes = jnp.broadcast_to(jnp.repeat(sl, ATT_HEAD_DIM).reshape(hd // LANES, 1, LANES), (hd // LANES, SUBLANES, LANES))
    return gq2, gk2, slopes


def _attention_core_fwd(qkv, q_gain, k_gain):
    hd = qkv.shape[1] // (3 * len(DIL_PATTERNS))
    gq2, gk2, slopes = _att_consts(q_gain, k_gain, hd)
    outs, lses, views = [], [], []
    for g, (_, dilation) in enumerate(DIL_PATTERNS):
        v_col = (3 * g + 2) * hd
        view = (_att_pre(qkv, g, dilation, gq2, gk2),) + (
            (_to_residues(qkv, 1), v_col) if dilation == 1 else (_to_residues(qkv, dilation, v_col, hd), 0))
        o_g, l_g = _att_group_fwd(view, hd, slopes, g)
        outs.append(o_g)
        lses.append(l_g)
        views.append(view)
    o16, lse_tot = _att_merge(outs, lses)
    return o16, (qkv, views, q_gain, k_gain, o16, lse_tot)


def _attention_core_bwd(res, do):
    qkv, views, q_gain, k_gain, o16, lse_tot = res
    hd = o16.shape[1]
    gq2, gk2, slopes = _att_consts(q_gain, k_gain, hd)
    do16, stats = _att_bwd_prep(do, o16, lse_tot)
    dqkv, dgq, dgk = None, 0.0, 0.0
    for g, view in enumerate(views):
        parts = _att_group_bwd(view, hd, slopes, stats, do16, g)
        dqkv, a, b = _att_post(dqkv, parts, qkv, gq2, gk2, g)
        dgq = dgq + a[0].reshape(-1, ATT_HEAD_DIM).sum(0)
        dgk = dgk + b[0].reshape(-1, ATT_HEAD_DIM).sum(0)
    return dqkv, dgq, dgk


HALO = 8


def _silu(v):
    return v * jax.nn.sigmoid(v)


def _silu_grad(v):
    s = jax.nn.sigmoid(v)
    return s * (1.0 + v * (1.0 - s))


def _halo_rows(dtype):
    return BF16_ROWS if dtype == BF16 else HALO


def _conv_fwd(zx, conv_w, conv_b, d_inner):
    t = zx.shape[0]
    conv_dim = conv_w.shape[1]
    cb = _pick(d_inner, (1024, 512, 256, 128))
    assert conv_dim % cb == 0
    tr = _pick(t, (256, 128))
    off = d_inner // cb
    hx = _halo_rows(zx.dtype)

    def body(x_ref, h_ref, w_ref, b_ref, o_ref):
        i = pl.program_id(1)
        halo = jnp.where(i > 0, h_ref[...].astype(F32), 0.0)
        ext = jnp.concatenate([halo, x_ref[...].astype(F32)], axis=0)
        acc = jnp.broadcast_to(b_ref[...], (tr, cb))
        for k in range(CONV_WIDTH):
            s = CONV_WIDTH - 1 - k
            sh = ext if s == 0 else pltpu.roll(ext, shift=s, axis=0)
            acc = acc + w_ref[k:k + 1, :] * sh[hx:hx + tr]
        o_ref[...] = acc.astype(o_ref.dtype)

    return pl.pallas_call(
        body, name="ssm_conv_fwd", out_shape=jax.ShapeDtypeStruct((t, conv_dim), BF16),
        grid=(conv_dim // cb, t // tr),
        in_specs=[pl.BlockSpec((tr, cb), lambda j, i: (i, off + j)),
                  pl.BlockSpec((hx, cb), lambda j, i: (jnp.maximum(i * (tr // hx) - 1, 0), off + j)),
                  pl.BlockSpec((CONV_WIDTH, cb), lambda j, i: (0, j)),
                  pl.BlockSpec((1, cb), lambda j, i: (0, j))],
        out_specs=pl.BlockSpec((tr, cb), lambda j, i: (i, j)),
        compiler_params=pltpu.CompilerParams(dimension_semantics=("parallel", "parallel")),
    )(zx, zx, conv_w, conv_b.reshape(1, -1))


def _conv_bwd(zx, conv_w, dpre, dzx, d_inner, col0):
    t, width = zx.shape
    conv_dim = dpre.shape[1]
    cb = _pick(conv_dim, (1024, 512, 256, 128))
    assert (d_inner + col0) % cb == 0
    tr = _pick(t, (256, 128))
    off = (d_inner + col0) // cb
    woff = col0 // cb
    nr = t // tr
    hx = _halo_rows(zx.dtype)
    hd = _halo_rows(dpre.dtype)

    def body(x_ref, h_ref, w_ref, d_ref, dn_ref, dzx_in, dx_ref, dw_ref, db_ref):
        i = pl.program_id(1)

        @pl.when(i == 0)
        def _():
            dw_ref[...] = jnp.zeros_like(dw_ref)
            db_ref[...] = jnp.zeros_like(db_ref)

        halo = jnp.where(i > 0, h_ref[...].astype(F32), 0.0)
        ext = jnp.concatenate([halo, x_ref[...].astype(F32)], axis=0)
        d = d_ref[...].astype(F32)
        dext = jnp.concatenate([d, jnp.where(i < nr - 1, dn_ref[...].astype(F32), 0.0)], axis=0)
        dx = jnp.zeros((tr, cb), F32)
        for k in range(CONV_WIDTH):
            s = CONV_WIDTH - 1 - k
            fut = dext if s == 0 else pltpu.roll(dext, shift=tr + hd - s, axis=0)
            dx = dx + w_ref[k:k + 1, :] * fut[:tr]
            past = ext if s == 0 else pltpu.roll(ext, shift=s, axis=0)
            dw_ref[k:k + 1, :] += jnp.sum(d * past[hx:hx + tr], axis=0, keepdims=True)
        dx_ref[...] = dx.astype(dx_ref.dtype)
        db_ref[...] += jnp.sum(d, axis=0, keepdims=True)

    last_halo = t // hd - 1
    return pl.pallas_call(
        body, name=f"ssm_conv_bwd_{col0}",
        out_shape=(jax.ShapeDtypeStruct(dzx.shape, dzx.dtype), jax.ShapeDtypeStruct((CONV_WIDTH, conv_dim), F32),
                   jax.ShapeDtypeStruct((1, conv_dim), F32)),
        grid=(conv_dim // cb, nr),
        in_specs=[pl.BlockSpec((tr, cb), lambda j, i: (i, off + j)),
                  pl.BlockSpec((hx, cb), lambda j, i: (jnp.maximum(i * (tr // hx) - 1, 0), off + j)),
                  pl.BlockSpec((CONV_WIDTH, cb), lambda j, i: (0, woff + j)),
                  pl.BlockSpec((tr, cb), lambda j, i: (i, j)),
                  pl.BlockSpec((hd, cb), lambda j, i: (jnp.minimum((i + 1) * (tr // hd), last_halo), j)),
                  pl.BlockSpec(memory_space=pl.ANY)],
        out_specs=(pl.BlockSpec((tr, cb), lambda j, i: (i, off + j)),
                   pl.BlockSpec((CONV_WIDTH, cb), lambda j, i: (0, j)),
                   pl.BlockSpec((1, cb), lambda j, i: (0, j))),
        input_output_aliases={5: 0},
        compiler_params=pltpu.CompilerParams(dimension_semantics=("parallel", "arbitrary")),
    )(zx, zx, conv_w, dpre, dpre, dzx)


def _pair_lanes(c0, c1):
    lane = lax.broadcasted_iota(jnp.int32, (1, LANES), 1)
    return jnp.where(lane < SSM_HEAD_DIM, c0, c1)


def _ssd_chunk_common(pre_x_ref, pre_b_ref, pre_c_ref, dtr_ref, bias_ref, alog_ref, cs_ref, col_ref):
    cl = SSD_CHUNK
    hpg = dtr_ref.shape[0]
    x = _silu(pre_x_ref[...].astype(F32))
    b16 = _silu(pre_b_ref[...].astype(F32)).astype(BF16)
    c16 = _silu(pre_c_ref[...].astype(F32)).astype(BF16)
    dt = jax.nn.softplus(dtr_ref[...] + bias_ref[...])
    a = -jnp.exp(alog_ref[...])
    li = lax.broadcasted_iota(jnp.int32, (cl, cl), 0)
    si = lax.broadcasted_iota(jnp.int32, (cl, cl), 1)
    upper = (li <= si).astype(F32)
    cs_ref[0:hpg, :] = jnp.dot(dt * a, upper, precision=lax.Precision.HIGHEST, preferred_element_type=F32)
    cs_ref[hpg:2 * hpg, :] = dt
    col_ref[...] = jnp.concatenate([cs_ref[...], jnp.zeros((cl - 2 * hpg, cl), F32)], axis=0).T
    g = lax.dot_general(c16, b16, (((1,), (1,)), ((), ())), preferred_element_type=F32)
    return x, b16, c16, dt, a, g, li >= si


def _ssd_fwd(pre, dtT, bias, alog, dskip_lanes, d_inner):
    t = pre.shape[0]
    cl = SSD_CHUNK
    nc = t // cl
    ng = SSM_GROUPS
    hpg = dtT.shape[1]
    gw = hpg * SSM_HEAD_DIM
    assert d_inner == ng * gw and hpg % 2 == 0
    bo = d_inner // SSM_STATE

    def body(px_ref, pb_ref, pc_ref, dtr_ref, bias_ref, alog_ref, dsk_ref, y_ref, st_ref, s_ref, cs_ref, col_ref):
        c = pl.program_id(1)

        @pl.when(c == 0)
        def _():
            s_ref[...] = jnp.zeros_like(s_ref)

        x, b16, c16, dt, a, g, causal = _ssd_chunk_common(px_ref, pb_ref, pc_ref, dtr_ref, bias_ref, alog_ref, cs_ref,
                                                          col_ref)
        st_ref[...] = s_ref[...]
        yoff = lax.dot_general(c16, s_ref[...].astype(BF16), (((1,), (1,)), ((), ())), preferred_element_type=F32)
        xde_parts = []
        for j in range(hpg // 2):
            cols = slice(j * LANES, (j + 1) * LANES)
            xp = x[:, cols]
            dcol, ecol, ocol, ms = [], [], [], []
            for hh in range(2):
                h = 2 * j + hh
                cs_row = cs_ref[h:h + 1, :]
                cs_col = col_ref[:, h:h + 1]
                dcol.append(col_ref[:, hpg + h:hpg + h + 1])
                ecol.append(jnp.exp(cs_ref[h:h + 1, cl - 1:cl] - cs_col))
                ocol.append(jnp.exp(cs_col))
                lm = jnp.where(causal, jnp.exp(jnp.minimum(cs_col - cs_row, 0.0)), 0.0)
                ms.append((g * lm).astype(BF16))
            xd = xp * _pair_lanes(dcol[0], dcol[1])
            xd16 = xd.astype(BF16)
            yd = _pair_lanes(1.0, 0.0) * jnp.dot(ms[0], xd16, preferred_element_type=F32) \
                + _pair_lanes(0.0, 1.0) * jnp.dot(ms[1], xd16, preferred_element_type=F32)
            y_ref[:, cols] = yd + yoff[:, cols] * _pair_lanes(ocol[0], ocol[1]) + xp * dsk_ref[0:1, cols]
            xde_parts.append((xd * _pair_lanes(ecol[0], ecol[1])).astype(BF16))
        new = lax.dot_general(jnp.concatenate(xde_parts, axis=1), b16, (((0,), (0,)), ((), ())),
                              preferred_element_type=F32)
        for h in range(hpg):
            rows = slice(h * SSM_HEAD_DIM, (h + 1) * SSM_HEAD_DIM)
            s_ref[rows, :] = s_ref[rows, :] * jnp.exp(cs_ref[h:h + 1, cl - 1:cl]) + new[rows, :]

    vec = lambda n: pl.BlockSpec((None, hpg, n), lambda gi, c: (gi, 0, 0))
    return pl.pallas_call(
        body, name="ssd_fwd",
        out_shape=(jax.ShapeDtypeStruct((t, d_inner), F32), jax.ShapeDtypeStruct((ng, nc, gw, SSM_STATE), F32)),
        grid=(ng, nc),
        in_specs=[pl.BlockSpec((cl, gw), lambda gi, c: (c, gi)),
                  pl.BlockSpec((cl, SSM_STATE), lambda gi, c: (c, bo + gi)),
                  pl.BlockSpec((cl, SSM_STATE), lambda gi, c: (c, bo + ng + gi)),
                  pl.BlockSpec((None, hpg, cl), lambda gi, c: (gi, 0, c)),
                  vec(1), vec(1),
                  pl.BlockSpec((1, gw), lambda gi, c: (0, gi))],
        out_specs=(pl.BlockSpec((cl, gw), lambda gi, c: (c, gi)),
                   pl.BlockSpec((None, None, gw, SSM_STATE), lambda gi, c: (gi, c, 0, 0))),
        scratch_shapes=[pltpu.VMEM((gw, SSM_STATE), F32), pltpu.VMEM((2 * hpg, cl), F32), pltpu.VMEM((cl, cl), F32)],
        compiler_params=pltpu.CompilerParams(dimension_semantics=("parallel", "arbitrary")),
    )(pre, pre, pre, dtT, bias, alog, dskip_lanes)


def _ssd_bwd(pre, dtT, bias, alog, dskip_lanes, states, dy, d_inner):
    t, conv_dim = pre.shape
    cl = SSD_CHUNK
    nc = t // cl
    ng = SSM_GROUPS
    hpg = dtT.shape[1]
    gw = hpg * SSM_HEAD_DIM
    bo = d_inner // SSM_STATE

    def body(px_ref, pb_ref, pc_ref, dtr_ref, bias_ref, alog_ref, dsk_ref, st_ref, dy_ref,
             dx_ref, db_ref, dc_ref, ddt_ref, acc_ref, dsk_out, ds_ref, cs_ref, dcs_ref, col_ref, dcol_ref):
        c = pl.program_id(1)

        @pl.when(c == 0)
        def _():
            ds_ref[...] = jnp.zeros_like(ds_ref)
            acc_ref[...] = jnp.zeros_like(acc_ref)
            dsk_out[...] = jnp.zeros_like(dsk_out)
            dcol_ref[...] = jnp.zeros_like(dcol_ref)

        x, b16, c16, dt, a, g, causal = _ssd_chunk_common(px_ref, pb_ref, pc_ref, dtr_ref, bias_ref, alog_ref, cs_ref,
                                                          col_ref)
        s_prev = st_ref[...]
        s16 = s_prev.astype(BF16)
        ds = ds_ref[...]
        ds16 = ds.astype(BF16)
        dyv = dy_ref[...]
        yoff = lax.dot_general(c16, s16, (((1,), (1,)), ((), ())), preferred_element_type=F32)
        bds = lax.dot_general(b16, ds16, (((1,), (1,)), ((), ())), preferred_element_type=F32)
        dg = jnp.zeros((cl, cl), F32)
        xde_parts, dye_parts = [], []
        lane = lax.broadcasted_iota(jnp.int32, (1, LANES), 1)
        for j in range(hpg // 2):
            cols = slice(j * LANES, (j + 1) * LANES)
            xp, dyp = x[:, cols], dyv[:, cols]
            dcol, ecol, ocol, lms = [], [], [], []
            for hh in range(2):
                h = 2 * j + hh
                cs_row = cs_ref[h:h + 1, :]
                cs_col = col_ref[:, h:h + 1]
                dcol.append(col_ref[:, hpg + h:hpg + h + 1])
                ecol.append(jnp.exp(cs_ref[h:h + 1, cl - 1:cl] - cs_col))
                ocol.append(jnp.exp(cs_col))
                lms.append(jnp.where(causal, jnp.exp(jnp.minimum(cs_col - cs_row, 0.0)), 0.0))
            dlanes, elanes, olanes = _pair_lanes(*dcol), _pair_lanes(*ecol), _pair_lanes(*ocol)
            xd = xp * dlanes
            xd16 = xd.astype(BF16)
            xde = xd * elanes
            yoffp = yoff[:, cols] * olanes
            bdsp = bds[:, cols]
            dxd = bdsp * elanes
            for hh in range(2):
                h = 2 * j + hh
                hmask = (lane // SSM_HEAD_DIM) == hh
                dyh16 = jnp.where(hmask, dyp, 0.0).astype(BF16)
                m = g * lms[hh]
                dm = lax.dot_general(dyh16, xd16, (((1,), (1,)), ((), ())), preferred_element_type=F32)
                w = dm * m
                dg = dg + dm * lms[hh]
                dxd = dxd + lax.dot_general(m.astype(BF16), dyh16, (((0,), (0,)), ((), ())),
                                            preferred_element_type=F32)
                term = jnp.sum(jnp.where(hmask, xde * bdsp, 0.0), axis=1, keepdims=True)
                dcs_col = (jnp.sum(w, axis=1, keepdims=True)
                           + jnp.sum(jnp.where(hmask, dyp * yoffp, 0.0), axis=1, keepdims=True) - term)
                rows = slice(h * SSM_HEAD_DIM, (h + 1) * SSM_HEAD_DIM)
                dec = jnp.exp(cs_ref[h:h + 1, cl - 1:cl])
                tail = jnp.sum(term, axis=0, keepdims=True) + dec * jnp.sum(
                    jnp.sum(s_prev[rows, :] * ds[rows, :], axis=1, keepdims=True), axis=0, keepdims=True)
                last = lax.broadcasted_iota(jnp.int32, (1, cl), 1) == cl - 1
                dcs_ref[h:h + 1, :] = jnp.where(last, tail, 0.0) - jnp.sum(w, axis=0, keepdims=True)
                dcol_ref[:, h:h + 1] = dcs_col
                dcol_ref[:, hpg + h:hpg + h + 1] = jnp.sum(jnp.where(hmask, dxd * xp, 0.0), axis=1, keepdims=True)
            dx_act = dxd * dlanes + dyp * dsk_ref[0:1, cols]
            dx_ref[:, cols] = (dx_act * _silu_grad(px_ref[:, cols].astype(F32))).astype(dx_ref.dtype)
            dsk_out[0:1, cols] += jnp.sum(dyp * xp, axis=0, keepdims=True)
            xde_parts.append(xde.astype(BF16))
            dye_parts.append((dyp * olanes).astype(BF16))
        xde16 = jnp.concatenate(xde_parts, axis=1)
        dye16 = jnp.concatenate(dye_parts, axis=1)
        dg16 = dg.astype(BF16)
        dc_act = jnp.dot(dg16, b16, preferred_element_type=F32) + jnp.dot(dye16, s16, preferred_element_type=F32)
        db_act = lax.dot_general(dg16, c16, (((0,), (0,)), ((), ())), preferred_element_type=F32) \
            + jnp.dot(xde16, ds16, preferred_element_type=F32)
        dc_ref[...] = (dc_act * _silu_grad(pc_ref[...].astype(F32))).astype(dc_ref.dtype)
        db_ref[...] = (db_act * _silu_grad(pb_ref[...].astype(F32))).astype(db_ref.dtype)
        ds_new = lax.dot_general(dye16, c16, (((0,), (0,)), ((), ())), preferred_element_type=F32)
        for h in range(hpg):
            rows = slice(h * SSM_HEAD_DIM, (h + 1) * SSM_HEAD_DIM)
            ds_ref[rows, :] = ds[rows, :] * jnp.exp(cs_ref[h:h + 1, cl - 1:cl]) + ds_new[rows, :]
        li = lax.broadcasted_iota(jnp.int32, (cl, cl), 0)
        si = lax.broadcasted_iota(jnp.int32, (cl, cl), 1)
        as_rows = dcol_ref[...].T
        d_adt = jnp.dot(dcs_ref[0:hpg, :] + as_rows[0:hpg], (li >= si).astype(F32), precision=lax.Precision.HIGHEST,
                        preferred_element_type=F32)
        ddt = d_adt * a + as_rows[hpg:2 * hpg]
        ddt_raw = ddt * jax.nn.sigmoid(dtr_ref[...] + bias_ref[...])
        ddt_ref[...] = ddt_raw
        acc_ref[0:hpg, :] += d_adt * dt
        acc_ref[hpg:2 * hpg, :] += ddt_raw

    rc = lambda c: nc - 1 - c
    vec = lambda n: pl.BlockSpec((None, hpg, n), lambda gi, c: (gi, 0, 0))
    x_spec = pl.BlockSpec((cl, gw), lambda gi, c: (rc(c), gi))
    b_spec = pl.BlockSpec((cl, SSM_STATE), lambda gi, c: (rc(c), bo + gi))
    c_spec = pl.BlockSpec((cl, SSM_STATE), lambda gi, c: (rc(c), bo + ng + gi))
    dt_spec = pl.BlockSpec((None, hpg, cl), lambda gi, c: (gi, 0, rc(c)))
    return pl.pallas_call(
        body, name="ssd_bwd",
        out_shape=(jax.ShapeDtypeStruct((t, d_inner), BF16), jax.ShapeDtypeStruct((t, ng * SSM_STATE), BF16),
                   jax.ShapeDtypeStruct((t, ng * SSM_STATE), BF16), jax.ShapeDtypeStruct(dtT.shape, F32),
                   jax.ShapeDtypeStruct((ng, 2 * hpg, cl), F32), jax.ShapeDtypeStruct((1, d_inner), F32)),
        grid=(ng, nc),
        in_specs=[x_spec, b_spec, c_spec, dt_spec, vec(1), vec(1),
                  pl.BlockSpec((1, gw), lambda gi, c: (0, gi)),
                  pl.BlockSpec((None, None, gw, SSM_STATE), lambda gi, c: (gi, rc(c), 0, 0)),
                  x_spec],
        out_specs=(x_spec, pl.BlockSpec((cl, SSM_STATE), lambda gi, c: (rc(c), gi)),
                   pl.BlockSpec((cl, SSM_STATE), lambda gi, c: (rc(c), gi)), dt_spec,
                   pl.BlockSpec((None, 2 * hpg, cl), lambda gi, c: (gi, 0, 0)),
                   pl.BlockSpec((1, gw), lambda gi, c: (0, gi))),
        scratch_shapes=[pltpu.VMEM((gw, SSM_STATE), F32), pltpu.VMEM((2 * hpg, cl), F32),
                        pltpu.VMEM((2 * hpg, cl), F32), pltpu.VMEM((cl, cl), F32), pltpu.VMEM((cl, cl), F32)],
        compiler_params=pltpu.CompilerParams(dimension_semantics=("parallel", "arbitrary")),
    )(pre, pre, pre, dtT, bias, alog, dskip_lanes, states, dy)


def _gate_norm_fwd(y, zx, norm_w, d_inner):
    t = y.shape[0]
    tr = _pick(t, (256, 128))
    gs = d_inner // SSM_GROUPS

    def body(y_ref, z_ref, w_ref, o_ref):
        for gi in range(SSM_GROUPS):
            cols = slice(gi * gs, (gi + 1) * gs)
            v = y_ref[:, cols] * _silu(z_ref[:, cols].astype(F32))
            r = lax.rsqrt(jnp.mean(v * v, axis=-1, keepdims=True) + NORM_EPS)
            o_ref[:, cols] = (v * r * w_ref[0:1, cols]).astype(BF16)

    spec = pl.BlockSpec((tr, d_inner), lambda i: (i, 0))
    return pl.pallas_call(
        body, name="ssm_gate_norm_fwd", out_shape=jax.ShapeDtypeStruct((t, d_inner), BF16), grid=(t // tr,),
        in_specs=[spec, spec, pl.BlockSpec((1, d_inner), lambda i: (0, 0))], out_specs=spec,
        compiler_params=pltpu.CompilerParams(dimension_semantics=("parallel",)),
    )(y, zx, norm_w.reshape(1, -1))


def _gate_norm_bwd(y, zx, norm_w, dout, d_inner):
    t, width = zx.shape
    tr = _pick(t, (256, 128))
    gs = d_inner // SSM_GROUPS

    def body(y_ref, z_ref, w_ref, do_ref, dy_ref, dz_ref, dw_ref):
        @pl.when(pl.program_id(0) == 0)
        def _():
            dw_ref[...] = jnp.zeros_like(dw_ref)

        for gi in range(SSM_GROUPS):
            cols = slice(gi * gs, (gi + 1) * gs)
            yv, zv = y_ref[:, cols], z_ref[:, cols].astype(F32)
            sz = _silu(zv)
            v = yv * sz
            r = lax.rsqrt(jnp.mean(v * v, axis=-1, keepdims=True) + NORM_EPS)
            vhat = v * r
            dn = do_ref[:, cols].astype(F32)
            dw_ref[0:1, cols] += jnp.sum(dn * vhat, axis=0, keepdims=True)
            dvh = dn * w_ref[0:1, cols]
            dv = r * (dvh - vhat * jnp.mean(dvh * vhat, axis=-1, keepdims=True))
            dy_ref[:, cols] = dv * sz
            dz_ref[:, cols] = (dv * yv * _silu_grad(zv)).astype(dz_ref.dtype)

    spec = pl.BlockSpec((tr, d_inner), lambda i: (i, 0))
    wspec = pl.BlockSpec((1, d_inner), lambda i: (0, 0))
    return pl.pallas_call(
        body, name="ssm_gate_norm_bwd",
        out_shape=(jax.ShapeDtypeStruct((t, d_inner), F32), jax.ShapeDtypeStruct((t, width), zx.dtype),
                   jax.ShapeDtypeStruct((1, d_inner), F32)),
        grid=(t // tr,),
        in_specs=[spec, spec, wspec, spec], out_specs=(spec, spec, wspec),
        compiler_params=pltpu.CompilerParams(dimension_semantics=("arbitrary",)),
    )(y, zx, norm_w.reshape(1, -1), dout)


def _ssm_small(dt_raw, dt_bias, a_log, d_skip):
    heads = dt_raw.shape[1]
    hpg = heads // SSM_GROUPS
    dtT = dt_raw.T.reshape(SSM_GROUPS, hpg, -1)
    return (dtT, dt_bias.reshape(SSM_GROUPS, hpg, 1), a_log.reshape(SSM_GROUPS, hpg, 1),
            jnp.repeat(d_skip, SSM_HEAD_DIM).reshape(1, -1))


def _ssm_core_fwd(zx, dt_raw, conv_w, conv_b, dt_bias, a_log, d_skip, norm_w):
    d_inner = norm_w.shape[0]
    pre = _conv_fwd(zx, conv_w, conv_b, d_inner)
    dtT, bias, alog, dsk = _ssm_small(dt_raw, dt_bias, a_log, d_skip)
    y, states = _ssd_fwd(pre, dtT, bias, alog, dsk, d_inner)
    out = _gate_norm_fwd(y, zx, norm_w, d_inner)
    return out, (zx, dt_raw, conv_w, dt_bias, a_log, d_skip, norm_w, pre, y, states)


def _ssm_core_bwd(res, dout):
    zx, dt_raw, conv_w, dt_bias, a_log, d_skip, norm_w, pre, y, states = res
    d_inner = norm_w.shape[0]
    heads = dt_raw.shape[1]
    dy, dzx, dnorm = _gate_norm_bwd(y, zx, norm_w, dout, d_inner)
    dtT, bias, alog, dsk = _ssm_small(dt_raw, dt_bias, a_log, d_skip)
    dx, db, dc, ddtT, acc, dsk_l = _ssd_bwd(pre, dtT, bias, alog, dsk, states, dy, d_inner)
    dws, dbs, col0 = [], [], 0
    for part in (dx, db, dc):
        dzx, dw_part, db_part = _conv_bwd(zx, conv_w, part, dzx, d_inner, col0)
        dws.append(dw_part)
        dbs.append(db_part)
        col0 += part.shape[1]
    dconv_w, dconv_b = jnp.concatenate(dws, axis=1), jnp.concatenate(dbs, axis=1)
    d_dt_raw = ddtT.reshape(heads, -1).T
    hpg = heads // SSM_GROUPS
    da = acc[:, :hpg].sum(-1).reshape(heads)
    d_bias = acc[:, hpg:].sum(-1).reshape(heads)
    d_alog = da * (-jnp.exp(a_log))
    d_dskip = dsk_l.reshape(heads, SSM_HEAD_DIM).sum(-1)
    return dzx, d_dt_raw, dconv_w, dconv_b.reshape(-1), d_bias, d_alog, d_dskip, dnorm.reshape(-1)


def _rows_call(body, name, ins, outs, acc_outs=(), rows=256):
    t = max(a.shape[0] for a in ins)
    tr = _pick(t, (rows, 128, 64, 32, 16, 8))

    def spec(a):
        if a.shape[0] == t:
            return pl.BlockSpec((tr, a.shape[1]), lambda i: (i, 0))
        return pl.BlockSpec(a.shape, lambda i: (0, 0))

    return pl.pallas_call(
        body, name=name, out_shape=tuple(outs) + tuple(acc_outs), grid=(t // tr,),
        in_specs=[spec(a) for a in ins],
        out_specs=tuple(spec(a) for a in outs) + tuple(pl.BlockSpec(a.shape, lambda i: (0, 0)) for a in acc_outs),
        compiler_params=pltpu.CompilerParams(dimension_semantics=("arbitrary" if acc_outs else "parallel",)),
    )(*ins)


def _rms_fwd(x, gain, after=None):
    def body(x_ref, g_ref, *rest):
        v = x_ref[...]
        rest[-1][...] = (v * lax.rsqrt(jnp.mean(v * v, axis=-1, keepdims=True) + NORM_EPS) * g_ref[...]).astype(BF16)

    ins = [x, gain.reshape(1, -1)] + ([] if after is None else [after])
    (h,) = _rows_call(body, "rms_fwd", ins, [jax.ShapeDtypeStruct(x.shape, BF16)])
    return h


def _rms_bwd(x, gain, dh, dres, after):
    def body(x_ref, g_ref, dh_ref, dr_ref, *rest):
        dx_ref, dg_ref = rest[-2:]

        @pl.when(pl.program_id(0) == 0)
        def _():
            dg_ref[...] = jnp.zeros_like(dg_ref)

        v = x_ref[...]
        r = lax.rsqrt(jnp.mean(v * v, axis=-1, keepdims=True) + NORM_EPS)
        vhat = v * r
        d = dh_ref[...].astype(F32)
        dg_ref[...] += jnp.sum(d * vhat, axis=0, keepdims=True)
        dvh = d * g_ref[...]
        dx_ref[...] = dr_ref[...] + r * (dvh - vhat * jnp.mean(dvh * vhat, axis=-1, keepdims=True))

    ins = [x, gain.reshape(1, -1), dh, dres] + ([] if after is None else [after])
    dx, dg = _rows_call(body, "rms_bwd", ins, [jax.ShapeDtypeStruct(x.shape, F32)],
                        [jax.ShapeDtypeStruct((1, x.shape[1]), F32)])
    return dx, dg.reshape(gain.shape)


def _swiglu_fwd(gu):
    t, f2 = gu.shape
    f = f2 // 2

    def body(gu_ref, o_ref):
        o_ref[...] = (_silu(gu_ref[:, :f].astype(F32)) * gu_ref[:, f:].astype(F32)).astype(BF16)

    (act,) = _rows_call(body, "swiglu_fwd", [gu], [jax.ShapeDtypeStruct((t, f), BF16)])
    return act


def _swiglu_bwd(gu, dact):
    t, f2 = gu.shape
    f = f2 // 2

    def body(gu_ref, d_ref, o_ref):
        g, u, d = gu_ref[:, :f].astype(F32), gu_ref[:, f:].astype(F32), d_ref[...].astype(F32)
        o_ref[:, :f] = (d * u * _silu_grad(g)).astype(BF16)
        o_ref[:, f:] = (d * _silu(g)).astype(BF16)

    (dgu,) = _rows_call(body, "swiglu_bwd", [gu, dact], [jax.ShapeDtypeStruct((t, f2), BF16)])
    return dgu


def _ple_fwd(x, gl, ple):
    def body(x_ref, g_ref, p_ref, o_ref):
        o_ref[...] = x_ref[...] + jax.nn.sigmoid(g_ref[...].astype(F32)) * p_ref[...].astype(F32)

    (out,) = _rows_call(body, "ple_fwd", [x, gl, ple], [jax.ShapeDtypeStruct(x.shape, F32)])
    return out


def _ple_bwd(gl, ple, dout):
    def body(g_ref, p_ref, d_ref, dg_ref, dp_ref):
        s, d = jax.nn.sigmoid(g_ref[...].astype(F32)), d_ref[...]
        dg_ref[...] = (d * p_ref[...].astype(F32) * s * (1.0 - s)).astype(BF16)
        dp_ref[...] = (d * s).astype(BF16)

    shp = jax.ShapeDtypeStruct(gl.shape, BF16)
    return _rows_call(body, "ple_bwd", [gl, ple, dout], [shp, shp])


def _loss_fwd(y, target):
    inv = 1.0 / y.shape[1]

    def body(y_ref, t_ref, d_ref, l_ref):
        @pl.when(pl.program_id(0) == 0)
        def _():
            l_ref[...] = jnp.zeros_like(l_ref)

        e = y_ref[...] - t_ref[...]
        d_ref[...] = e * inv
        part = jnp.sum(jnp.sum(e * e, axis=1, keepdims=True), axis=0, keepdims=True) * (0.5 * inv)
        l_ref[...] += jnp.broadcast_to(part, l_ref.shape)

    dy, acc = _rows_call(body, "loss_fwd", [y, target], [jax.ShapeDtypeStruct(y.shape, F32)],
                         [jax.ShapeDtypeStruct((SUBLANES, LANES), F32)])
    return acc[0, 0], dy


def local_step(small, fetch, emit, x, p, target):
    depth = small['norm_mix'].shape[0]
    ssm_small = ('ssm_conv_w', 'ssm_conv_b', 'ssm_dt_bias', 'ssm_a_log', 'ssm_d_skip', 'ssm_norm_w')
    saved = []
    for i in range(depth):
        j = i // 2
        s = {'x': x}
        wm, token = fetch(('mix', i), x)
        h = s['h'] = _rms_fwd(x, small['norm_mix'][i], token)
        if i % 2 == 0:
            n_main = wm['ssm_w_in'].shape[0] - small['ssm_dt_bias'].shape[1]
            zx = _mm(h, wm['ssm_w_in'][:n_main], tb=True, out_dtype=BF16, name="ssm_in_fwd")
            more, token = fetch(('out', i), zx)
            wm = {**wm, **more}
            dt_raw = _mm(h, wm['ssm_w_in'][n_main:], tb=True, after=token, name="ssm_dt_fwd")
            y, s['mix'] = _ssm_core_fwd(zx, dt_raw, wm['ssm_conv_w'], *[small[n][j] for n in ssm_small[1:]])
            x = _mm(y, wm['ssm_w_out'], add=x, name="ssm_out_fwd")
        else:
            qkv = _mm(h, wm['att_w_qkv'], tb=True, out_dtype=BF16, name="att_qkv_fwd")
            y, s['mix'] = _attention_core_fwd(qkv, small['att_q_norm'][j], small['att_k_norm'][j])
            x = _mm(y, wm['att_w_o'], add=x, name="att_o_fwd")
        s['wm'], s['y'], s['x1'] = wm, y, x
        wf, token = fetch(('ffn', i), x)
        s['wf'] = wf
        h2 = s['h2'] = _rms_fwd(x, small['norm_ffn'][i], token)
        gu = s['gu'] = _mm(h2, wf['ffn_w_gu'], tb=True, out_dtype=BF16, name="ffn_gu_fwd")
        act = s['act'] = _swiglu_fwd(gu)
        x = s['x2'] = _mm(act, wf['ffn_w_down'], add=x, name="ffn_down_fwd")
        gl = s['gl'] = _mm(x, wf['ple_w_gate'], out_dtype=BF16, name="ple_gate_fwd")
        ple = s['ple'] = _mm(p[i], wf['ple_w_proj'], tb=True, out_dtype=BF16, name="ple_proj_fwd")
        x = _ple_fwd(x, gl, ple)
        saved.append(s)
    loss, dx = _loss_fwd(x, target)

    g = {n: [None] * small[n].shape[0] for n in small}
    for i in reversed(range(depth)):
        j = i // 2
        s = saved[i]
        wm, wf = s['wm'], s['wf']
        gf = {}
        dgl, dple = _ple_bwd(s['gl'], s['ple'], dx)
        gf['ple_w_proj'] = _mm(dple, p[i], ta=True, out_dtype=BF16, name="ple_proj_dw")
        gf['ple_w_gate'] = _mm(s['x2'], dgl, ta=True, out_dtype=BF16, name="ple_gate_dw")
        dx = _mm(dgl, wf['ple_w_gate'], tb=True, add=dx, name="ple_gate_da")
        dact = _mm(dx, wf['ffn_w_down'], tb=True, out_dtype=BF16, name="ffn_down_da")
        gf['ffn_w_down'] = _mm(s['act'], dx, ta=True, out_dtype=BF16, name="ffn_down_dw")
        dgu = _swiglu_bwd(s['gu'], dact)
        dh2 = _mm(dgu, wf['ffn_w_gu'], out_dtype=BF16, name="ffn_gu_da")
        gf['ffn_w_gu'] = _mm(dgu, s['h2'], ta=True, out_dtype=BF16, name="ffn_gu_dw")
        dx, g['norm_ffn'][i] = _rms_bwd(s['x1'], small['norm_ffn'][i], dh2, dx, emit(('ffn', i), gf))
        gm = {}
        if i % 2 == 0:
            n_main = wm['ssm_w_in'].shape[0] - small['ssm_dt_bias'].shape[1]
            dyn = _mm(dx, wm['ssm_w_out'], tb=True, out_dtype=BF16, name="ssm_out_da")
            gm['ssm_w_out'] = _mm(s['y'], dx, ta=True, out_dtype=BF16, name="ssm_out_dw")
            dzx, d_dt, *sg = _ssm_core_bwd(s['mix'], dyn)
            for n, v in zip(ssm_small, sg):
                g[n][j] = v
            dh = _mm(d_dt, wm['ssm_w_in'][n_main:], name="ssm_dt_da")
            dh = _mm(dzx, wm['ssm_w_in'][:n_main], add=dh, out_dtype=BF16, name="ssm_in_da")
            gm['ssm_w_in'] = jnp.concatenate([_mm(dzx, s['h'], ta=True, out_dtype=BF16, name="ssm_in_dw"),
                                              _mm(d_dt, s['h'], ta=True, out_dtype=BF16, name="ssm_dt_dw")], axis=0)
        else:
            do = _mm(dx, wm['att_w_o'], tb=True, name="att_o_da")
            gm['att_w_o'] = _mm(s['y'], dx, ta=True, out_dtype=BF16, name="att_o_dw")
            dqkv, g['att_q_norm'][j], g['att_k_norm'][j] = _attention_core_bwd(s['mix'], do)
            dh = _mm(dqkv, wm['att_w_qkv'], out_dtype=BF16, name="att_qkv_da")
            gm['att_w_qkv'] = _mm(dqkv, s['h'], ta=True, out_dtype=BF16, name="att_qkv_dw")
        dx, g['norm_mix'][i] = _rms_bwd(s['x'], small['norm_mix'][i], dh, dx, emit(('mix', i), gm))
    return loss, dx, {n: jnp.stack(v) for n, v in g.items()}


CHANNEL_WEIGHTS = ('ffn_w_gate', 'ffn_w_up', 'ffn_w_down', 'ple_w_proj', 'ple_w_gate')


def _stages(depth):
    gather, scatter = {}, {}
    for i in range(depth):
        j = i // 2
        if i % 2 == 0:
            gather['mix', i] = [('ssm_w_in', j), ('ssm_conv_w', j)]
            gather['out', i] = [('ssm_w_out', j)]
            scatter['mix', i] = [('ssm_w_in', j), ('ssm_w_out', j)]
        else:
            gather['mix', i] = scatter['mix', i] = [('att_w_qkv', j), ('att_w_o', j)]
        gather['ffn', i] = scatter['ffn', i] = [(n, i) for n in CHANNEL_WEIGHTS]
    return gather, scatter


def _pack_plan(shapes, width, members):
    plan, off = [], 0
    for name, lyr in members:
        _, r, c = shapes[name]
        if name in COL_SHARDED:
            r, c = c, r
        if name == 'ssm_conv_w':
            pr = -(-2 * r * c // width)
        else:
            assert (r * c) % width == 0, (name, r, c)
            pr = r * c // width
        plan.append((name, lyr, r, c, pr, off))
        off += _round_up(pr, BF16_ROWS)
    return plan, off


def _small_plan(shapes):
    plan, off = [], 0
    for name in SMALL:
        n = math.prod(shapes[name])
        plan.append((name, n, off))
        off += n
    return plan, _round_up(off, SUBLANES * LANES)


def kernel(x, p, norm_mix, norm_ffn, ssm_w_in, ssm_conv_w, ssm_conv_b, ssm_dt_bias, ssm_a_log, ssm_d_skip, ssm_norm_w, ssm_w_out, att_w_qkv, att_q_norm, att_k_norm, att_w_o, ffn_w_gate, ffn_w_up, ffn_w_down, ple_w_proj, ple_w_gate, loss_target, m_norm_mix, m_norm_ffn, m_ssm_w_in, m_ssm_conv_w, m_ssm_conv_b, m_ssm_dt_bias, m_ssm_a_log, m_ssm_d_skip, m_ssm_norm_w, m_ssm_w_out, m_att_w_qkv, m_att_q_norm, m_att_k_norm, m_att_w_o, m_ffn_w_gate, m_ffn_w_up, m_ffn_w_down, m_ple_w_proj, m_ple_w_gate, v_norm_mix, v_norm_ffn, v_ssm_w_in, v_ssm_conv_w, v_ssm_conv_b, v_ssm_dt_bias, v_ssm_a_log, v_ssm_d_skip, v_ssm_norm_w, v_ssm_w_out, v_att_w_qkv, v_att_q_norm, v_att_k_norm, v_att_w_o, v_ffn_w_gate, v_ffn_w_up, v_ffn_w_down, v_ple_w_proj, v_ple_w_gate):
    given = dict(locals())
    w_in = {n: given[n] for n in WEIGHTS}
    m_in = {n: given["m_" + n] for n in WEIGHTS}
    v_in = {n: given["v_" + n] for n in WEIGHTS}
    width = x.shape[-1]
    depth = norm_mix.shape[0]

    gather_members, scatter_members = _stages(depth)
    shapes = {n: w_in[n].shape for n in BIG + ('ssm_conv_w',)}
    plans = {key: _pack_plan(shapes, width, members)[0] for key, members in gather_members.items()}
    splans = {key: _pack_plan(shapes, width, members)[0] for key, members in scatter_members.items()}
    order = list(gather_members)

    def pack_weights(stage):
        pieces = []
        for name, layer, r, c, pr, off in plans[stage]:
            blk = w_in[name][layer]
            if name == 'ssm_conv_w':
                blk = lax.bitcast_convert_type(blk.reshape(-1), BF16).reshape(-1)
                blk = jnp.pad(blk, (0, pr * width - blk.shape[0]))
            elif name in COL_SHARDED:
                blk = blk.T
            blk = blk.astype(BF16).reshape(pr, width)
            pieces.append(jnp.pad(blk, ((0, _round_up(pr, BF16_ROWS) - pr), (0, 0))))
        return jnp.concatenate(pieces, axis=0)

    packed = [pack_weights(order[0])]
    pending = [exchange_start(packed[0], True, "gather_start_0")]
    packed += [pack_weights(stage) for stage in order[1:]]

    def fetch(stage, after):
        k = order.index(stage)
        handle, token = pending[k]
        land = exchange_wait(handle, [token] + packed[1:] if k == 0 else after, True, f"gather_wait_{k}")
        token = None
        if k + 1 < len(order):
            pending.append(exchange_start(packed[k + 1], True, f"gather_start_{k + 1}", land))
            token = pending[-1][1]
        got = {}
        for name, layer, r, c, pr, off in plans[stage]:
            piece = land[:, off:off + pr]
            if name == 'ssm_conv_w':
                taps, chans = w_in[name].shape[1:]
                bits = piece.reshape(N_DEV, -1)[:, :2 * taps * chans].reshape(N_DEV, taps * chans, 2)
                piece = lax.bitcast_convert_type(bits, F32).reshape(N_DEV, taps, chans)
                got[name] = piece.transpose(1, 0, 2).reshape(taps, N_DEV * chans)
            else:
                got[name] = piece.reshape(N_DEV * r, c)
        if 'ffn_w_gate' in got:
            got['ffn_w_gu'] = jnp.concatenate([got.pop('ffn_w_gate'), got.pop('ffn_w_up')], axis=0)
        return got, token

    scatters = {}

    def emit(stage, grads):
        grads = dict(grads)
        if 'ffn_w_gu' in grads:
            hidden = grads['ffn_w_gu'].shape[0] // 2
            grads['ffn_w_gate'], grads['ffn_w_up'] = grads['ffn_w_gu'][:hidden], grads['ffn_w_gu'][hidden:]
        pieces = []
        for name, layer, r, c, pr, off in splans[stage]:
            g = grads[name].reshape(N_DEV, pr, width)
            pieces.append(jnp.pad(g, ((0, 0), (0, _round_up(pr, BF16_ROWS) - pr), (0, 0))))
        scatters[stage], token = exchange_start(jnp.concatenate(pieces, axis=1), False,
                                                f"scatter_start_{len(scatters)}")
        return token

    small = {n: w_in[n] for n in SMALL}
    cs = w_in['ssm_conv_w'].shape[2]
    loss_local, gx, gw = local_step(small, fetch, emit, x[0], p[:, 0], loss_target[0])
    loss = lax.psum(loss_local, ("x", "y", "c"))

    parts = {}
    for k, (stage, handle) in enumerate(scatters.items()):
        received = exchange_wait(handle, gx, False, f"scatter_wait_{k}")
        gsum = sum_slots(received, f"sum_grads_{k}")
        for name, layer, r, c, pr, off in splans[stage]:
            g = gsum[off:off + pr].reshape(r, c)
            parts[name, layer] = g.T if name in COL_SHARDED else g
    grads = {n: jnp.stack([parts[n, layer] for layer in range(w_in[n].shape[0])]) for n in BIG}

    splan, stotal = _small_plan({n: gw[n].shape for n in SMALL})
    svec = jnp.concatenate([gw[n].reshape(-1) for n, _, _ in splan])
    svec = jnp.pad(svec, (0, stotal - svec.shape[0])).reshape(stotal // LANES, LANES)
    _, ssum = all_gather_sum_small(svec, "sum_small_grads")
    ssum = ssum.reshape(-1)
    for name, n, off in splan:
        grads[name] = ssum[off:off + n].reshape(gw[name].shape)
    me = _me()
    grads['ssm_conv_w'] = lax.dynamic_slice_in_dim(grads['ssm_conv_w'], me * cs, cs, axis=2)

    delta, new_m, new_v = {}, {}, {}
    for name in BIG:
        shp = w_in[name].shape
        flat = lambda a: a.reshape(-1, shp[-1])
        d, nm, nv = adamw(flat(w_in[name]), flat(grads[name]), flat(m_in[name]), flat(v_in[name]), "adamw_" + name)
        delta[name], new_m[name], new_v[name] = d.reshape(shp), nm.reshape(shp), nv.reshape(shp)
    splan2, stotal2 = _small_plan({n: w_in[n].shape for n in SMALL})

    def pack_small(src):
        vec = jnp.concatenate([src[n].reshape(-1) for n, _, _ in splan2])
        return jnp.pad(vec, (0, stotal2 - vec.shape[0]), constant_values=1.0).reshape(stotal2 // LANES, LANES)

    sd, snm, snv = adamw(pack_small(w_in), pack_small(grads), pack_small(m_in), pack_small(v_in), "adamw_small")
    for name, n, off in splan2:
        shp = w_in[name].shape
        delta[name] = sd.reshape(-1)[off:off + n].reshape(shp)
        new_m[name] = snm.reshape(-1)[off:off + n].reshape(shp)
        new_v[name] = snv.reshape(-1)[off:off + n].reshape(shp)

    return (loss, gx[None], *[grads[n] for n in WEIGHTS], *[delta[n] for n in WEIGHTS],
            *[new_m[n] for n in WEIGHTS], *[new_v[n] for n in WEIGHTS])
```

```python
import functools
import math

import jax
import jax.numpy as jnp
from jax import lax
from jax.experimental import pallas as pl
from jax.experimental.pallas import tpu as pltpu

F32 = jnp.float32
BF16 = jnp.bfloat16
N_DEV = 8
MESH = pl.DeviceIdType.MESH

SSM_HEAD_DIM = 64
SSM_GROUPS = 4
SSM_STATE = 128
CONV_WIDTH = 4
SSD_CHUNK = 128
ATT_HEAD_DIM = 64
DIL_PATTERNS = ((128, 1), (512, 4), (2048, 16))
NORM_EPS = 1e-6
ADAM_LR = 0.001
ADAM_B1 = 0.9
ADAM_B2 = 0.999
ADAM_EPS = 1e-08
ADAM_WD = 0.01
ADAM_STEP = 10

BF16_ROWS = 16
LANES = 128
SUBLANES = 8

WEIGHTS = ['norm_mix', 'norm_ffn', 'ssm_w_in', 'ssm_conv_w', 'ssm_conv_b', 'ssm_dt_bias', 'ssm_a_log', 'ssm_d_skip',
           'ssm_norm_w', 'ssm_w_out', 'att_w_qkv', 'att_q_norm', 'att_k_norm', 'att_w_o', 'ffn_w_gate', 'ffn_w_up',
           'ffn_w_down', 'ple_w_proj', 'ple_w_gate']
COL_SHARDED = ('ssm_w_in', 'att_w_qkv', 'ffn_w_gate', 'ffn_w_up', 'ple_w_proj')
ROW_SHARDED = ('ssm_w_out', 'att_w_o', 'ffn_w_down', 'ple_w_gate')
BIG = COL_SHARDED + ROW_SHARDED
SMALL = ('norm_mix', 'norm_ffn', 'ssm_conv_w', 'ssm_conv_b', 'ssm_dt_bias', 'ssm_a_log', 'ssm_d_skip', 'ssm_norm_w',
         'att_q_norm', 'att_k_norm')


def _pick(n, cands):
    for c in cands:
        if n % c == 0:
            return c
    return n


def _round_up(n, m):
    return -(-n // m) * m


MM_TILES = (1024, 1408, 512, 256, 128)
MM_VMEM_BYTES = 48 * 1024 * 1024


def _mm(a, b, *, ta=False, tb=False, out_dtype=F32, add=None, after=None, name):
    k_dim, m_dim = (a.shape if ta else a.shape[::-1])
    n_dim = b.shape[0] if tb else b.shape[1]
    assert (b.shape[1] if tb else b.shape[0]) == k_dim, (a.shape, b.shape, ta, tb)
    tm = _pick(m_dim, MM_TILES)
    tn = _pick(n_dim, MM_TILES)
    tk = _pick(k_dim, MM_TILES)
    nk = k_dim // tk
    a_spec = pl.BlockSpec((tk, tm), lambda i, j, k: (k, i)) if ta else pl.BlockSpec((tm, tk), lambda i, j, k: (i, k))
    b_spec = pl.BlockSpec((tn, tk), lambda i, j, k: (j, k)) if tb else pl.BlockSpec((tk, tn), lambda i, j, k: (k, j))
    o_spec = pl.BlockSpec((tm, tn), lambda i, j, k: (i, j))
    dims = (((0 if ta else 1,), (1 if tb else 0,)), ((), ()))
    has_add = add is not None
    n_in = 2 + has_add + (after is not None)

    def body(*refs):
        a_ref, b_ref = refs[:2]
        o_ref = refs[n_in]

        def dot():
            return lax.dot_general(a_ref[...].astype(BF16), b_ref[...].astype(BF16), dims,
                                   preferred_element_type=F32)

        def finish(acc):
            if has_add:
                acc = acc + refs[2][...].astype(F32)
            o_ref[...] = acc.astype(o_ref.dtype)

        if nk == 1:
            finish(dot())
            return
        acc_ref = refs[n_in + 1]
        k = pl.program_id(2)

        @pl.when(k == 0)
        def _():
            acc_ref[...] = dot()

        @pl.when((k > 0) & (k < nk - 1))
        def _():
            acc_ref[...] += dot()

        @pl.when(k == nk - 1)
        def _():
            finish(acc_ref[...] + dot())

    return pl.pallas_call(
        body, name=f"{name}_{m_dim}x{n_dim}x{k_dim}",
        out_shape=jax.ShapeDtypeStruct((m_dim, n_dim), out_dtype),
        grid=(m_dim // tm, n_dim // tn, nk),
        in_specs=[a_spec, b_spec] + ([o_spec] if has_add else []) + (
            [] if after is None else [pl.BlockSpec(memory_space=pl.ANY)]),
        out_specs=o_spec,
        scratch_shapes=[] if nk == 1 else [pltpu.VMEM((tm, tn), F32)],
        compiler_params=pltpu.CompilerParams(dimension_semantics=("parallel", "parallel", "arbitrary"),
                                             vmem_limit_bytes=MM_VMEM_BYTES),
    )(*((a, b) + ((add,) if has_add else ()) + (() if after is None else (after,))))


def _me():
    return 4 * lax.axis_index("x") + 2 * lax.axis_index("y") + lax.axis_index("c")


def _peer(j):
    x, y, c = lax.axis_index("x"), lax.axis_index("y"), lax.axis_index("c")
    px = 1 - x if j & 4 else x
    py = 1 - y if j & 2 else y
    pc = 1 - c if j & 1 else c
    return (px, py, pc), 4 * px + 2 * py + pc


def _exchange_body(src_of, dst_ref, send_sems, recv_sems, local_sem):
    me = _me()
    mine = pltpu.make_async_copy(src_of(me), dst_ref.at[me], local_sem)
    mine.start()
    sends = []
    for j in range(1, N_DEV):
        peer, pidx = _peer(j)
        cp = pltpu.make_async_remote_copy(src_ref=src_of(pidx), dst_ref=dst_ref.at[me], send_sem=send_sems.at[j - 1],
                                          recv_sem=recv_sems.at[j - 1], device_id=peer, device_id_type=MESH)
        cp.start()
        sends.append(cp)
    for j in range(1, N_DEV):
        peer, pidx = _peer(j)
        pltpu.make_async_remote_copy(src_ref=src_of(pidx), dst_ref=dst_ref.at[pidx], send_sem=send_sems.at[j - 1],
                                     recv_sem=recv_sems.at[j - 1], device_id=peer, device_id_type=MESH).wait_recv()
    for cp in sends:
        cp.wait_send()
    mine.wait()


_EXCHANGE_SCRATCH = [pltpu.SemaphoreType.DMA((N_DEV - 1,)), pltpu.SemaphoreType.DMA((N_DEV - 1,)),
                     pltpu.SemaphoreType.DMA]


_HBM = pl.BlockSpec(memory_space=pltpu.HBM)
_SEM = pl.BlockSpec(memory_space=pltpu.SEMAPHORE)


def _split_copies(src_ref, gather, land_ref, send_sems, recv_sems):
    me = _me()
    pairs = []
    for j in range(1, N_DEV):
        peer, pidx = _peer(j)

        def make(slot, peer=peer, pidx=pidx, j=j):
            return pltpu.make_async_remote_copy(
                src_ref=src_ref if gather else src_ref.at[pidx], dst_ref=land_ref.at[slot],
                send_sem=send_sems.at[j - 1], recv_sem=recv_sems.at[j - 1], device_id=peer, device_id_type=MESH)

        pairs.append((make(me), make(pidx)))
    return pairs


def exchange_start(src, gather, name, after=None):
    land_shape = ((N_DEV,) + src.shape) if gather else src.shape
    has_after = after is not None

    def body(*refs):
        src_ref, land_ref = refs[:2]
        send_sems, recv_sems = refs[2 + has_after:4 + has_after]
        for send, _ in _split_copies(src_ref, gather, land_ref, send_sems, recv_sems):
            send.start()
        refs[-1][...] = jnp.zeros_like(refs[-1])

    sem = pltpu.SemaphoreType.DMA((N_DEV - 1,))
    send_sems, recv_sems, src_thru, land, token = pl.pallas_call(
        body, name=name,
        out_shape=(sem, sem, pltpu.HBM(src.shape, src.dtype), pltpu.HBM(land_shape, src.dtype),
                   jax.ShapeDtypeStruct((SUBLANES, LANES), F32)),
        in_specs=(_HBM, _HBM) + ((pl.BlockSpec(memory_space=pl.ANY),) if has_after else ()),
        out_specs=(_SEM, _SEM, _HBM, _HBM, pl.BlockSpec(memory_space=pltpu.VMEM)),
        input_output_aliases={0: 2, 1: 3},
        compiler_params=pltpu.CompilerParams(has_side_effects=pltpu.SideEffectType.DATAFLOW_SIDE_EFFECTING),
    )(pltpu.with_memory_space_constraint(src, pltpu.HBM),
      pltpu.with_memory_space_constraint(lax.empty(land_shape, src.dtype), pltpu.HBM),
      *((after,) if has_after else ()))
    return (send_sems, recv_sems, src_thru, land), token


def exchange_wait(handle, after, gather, name):
    send_sems, recv_sems, src_thru, land = handle
    after = tuple(after) if isinstance(after, (tuple, list)) else (after,)

    def body(src_ref, land_ref, send_sems, recv_sems, *rest):
        for _, arrival in _split_copies(src_ref, gather, land_ref, send_sems, recv_sems):
            arrival.wait_send()
            arrival.wait_recv()

    src_done, got = pl.pallas_call(
        body, name=name,
        out_shape=(pltpu.HBM(src_thru.shape, src_thru.dtype), pltpu.HBM(land.shape, land.dtype)),
        in_specs=(_HBM, _HBM, _SEM, _SEM) + (pl.BlockSpec(memory_space=pl.ANY),) * len(after), out_specs=(_HBM, _HBM),
        input_output_aliases={0: 0, 1: 1},
        compiler_params=pltpu.CompilerParams(has_side_effects=pltpu.SideEffectType.DATAFLOW_SIDE_EFFECTING),
    )(src_thru, land, send_sems, recv_sems, *after)
    mine = src_done if gather else lax.dynamic_index_in_dim(src_done, _me(), 0, keepdims=False)
    return lax.dynamic_update_index_in_dim(got, mine, _me(), 0)


def all_gather_sum_small(v, name):
    def body(x_ref, out_ref, sum_ref, send_sems, recv_sems, local_sem):
        _exchange_body(lambda k: x_ref, out_ref, send_sems, recv_sems, local_sem)
        acc = out_ref[0]
        for k in range(1, N_DEV):
            acc = acc + out_ref[k]
        sum_ref[...] = acc

    return pl.pallas_call(
        body, name=name,
        out_shape=(jax.ShapeDtypeStruct((N_DEV,) + v.shape, v.dtype), jax.ShapeDtypeStruct(v.shape, v.dtype)),
        in_specs=[pl.BlockSpec(memory_space=pltpu.VMEM)],
        out_specs=(pl.BlockSpec(memory_space=pltpu.VMEM), pl.BlockSpec(memory_space=pltpu.VMEM)),
        scratch_shapes=list(_EXCHANGE_SCRATCH),
    )(v)


def sum_slots(slots, name):
    _, p_dim, c_dim = slots.shape
    tp = next(tp for tp in range(512, 0, -BF16_ROWS) if p_dim % tp == 0)

    def body(x_ref, o_ref):
        acc = x_ref[0].astype(F32)
        for k in range(1, N_DEV):
            acc = acc + x_ref[k].astype(F32)
        o_ref[...] = acc

    return pl.pallas_call(
        body, name=name,
        out_shape=jax.ShapeDtypeStruct((p_dim, c_dim), F32),
        grid=(p_dim // tp,),
        in_specs=[pl.BlockSpec((N_DEV, tp, c_dim), lambda i: (0, i, 0))],
        out_specs=pl.BlockSpec((tp, c_dim), lambda i: (i, 0)),
        compiler_params=pltpu.CompilerParams(dimension_semantics=("parallel",)),
    )(slots)


def adamw(w, g, m, v, name):
    rows, cols = w.shape
    tr = _pick(rows, (256, 128, 64, 32, 16, 8))

    def body(w_ref, g_ref, m_ref, v_ref, d_ref, nm_ref, nv_ref):
        gv = g_ref[...]
        nm = ADAM_B1 * m_ref[...] + (1.0 - ADAM_B1) * gv
        nv = ADAM_B2 * v_ref[...] + (1.0 - ADAM_B2) * (gv * gv)
        m_hat = nm / (1.0 - ADAM_B1 ** ADAM_STEP)
        v_hat = nv / (1.0 - ADAM_B2 ** ADAM_STEP)
        d_ref[...] = -ADAM_LR * (m_hat / (jnp.sqrt(v_hat) + ADAM_EPS) + ADAM_WD * w_ref[...])
        nm_ref[...] = nm
        nv_ref[...] = nv

    spec = pl.BlockSpec((tr, cols), lambda i: (i, 0))
    shp = jax.ShapeDtypeStruct((rows, cols), F32)
    return pl.pallas_call(
        body, name=name, out_shape=(shp, shp, shp), grid=(rows // tr,),
        in_specs=[spec] * 4, out_specs=(spec,) * 3,
        compiler_params=pltpu.CompilerParams(dimension_semantics=("parallel",)),
    )(w, g, m, v)


ATT_BLK = 128
NEG = -1e30


def _head_sums(v):
    li = lax.broadcasted_iota(jnp.int32, (LANES, LANES), 0) // ATT_HEAD_DIM
    lj = lax.broadcasted_iota(jnp.int32, (LANES, LANES), 1) // ATT_HEAD_DIM
    ones = (li == lj).astype(BF16)
    hi = v.astype(BF16)
    lo = (v - hi.astype(F32)).astype(BF16)
    return jnp.dot(hi, ones, preferred_element_type=F32) + jnp.dot(lo, ones, preferred_element_type=F32)


def _head_col(v, hmask):
    return jnp.max(jnp.where(hmask, v, -jnp.inf), axis=-1, keepdims=True)


def _qk_norm(raw, gain2):
    rstd = lax.rsqrt(_head_sums(raw * raw) * (1.0 / ATT_HEAD_DIM) + NORM_EPS)
    xhat = raw * rstd
    return xhat * gain2, xhat, rstd


def _qk_norm_bwd(dn, xhat, rstd, gain2):
    dxh = dn * gain2
    return rstd * (dxh - xhat * (_head_sums(dxh * xhat) * (1.0 / ATT_HEAD_DIM))), dn * xhat


def _att_mask_bias(n, dilation):
    qi = lax.broadcasted_iota(jnp.int32, (ATT_BLK, 2 * ATT_BLK), 0)
    ki = lax.broadcasted_iota(jnp.int32, (ATT_BLK, 2 * ATT_BLK), 1)
    dist = qi + ATT_BLK - ki
    valid = (dist >= 0) & (dist <= ATT_BLK) & ((n > 0) | (ki >= ATT_BLK))
    return valid, (dilation * dist).astype(F32)


ATT_PAIRS = 8
RELAYOUT_ROWS = 512
RELAYOUT_COLS = 512


def _to_residues(x, dilation, col0=0, cols=None):
    t = x.shape[0]
    cols = x.shape[1] if cols is None else cols
    if dilation == 1 and col0 == 0 and cols == x.shape[1]:
        return x.reshape(1, t, cols)
    tr = _pick(t, (RELAYOUT_ROWS,))
    tc = _pick(cols, (RELAYOUT_COLS, 256, 128))
    per = tr // dilation
    assert tr % dilation == 0 and col0 % tc == 0

    def body(x_ref, o_ref, s_ref):
        for c in range(tc // LANES):
            lanes = slice(c * LANES, (c + 1) * LANES)
            s_ref[c] = x_ref[:, lanes].astype(F32)
            for r in range(dilation):
                o_ref[r, :, lanes] = s_ref[c, pl.ds(r, per, stride=dilation), :].astype(o_ref.dtype)

    return pl.pallas_call(
        body, name=f"to_residues_{dilation}", out_shape=jax.ShapeDtypeStruct((dilation, t // dilation, cols), x.dtype),
        grid=(t // tr, cols // tc),
        in_specs=[pl.BlockSpec((tr, tc), lambda i, j: (i, col0 // tc + j))],
        out_specs=pl.BlockSpec((dilation, per, tc), lambda i, j: (0, i, j)),
        scratch_shapes=[pltpu.VMEM((tc // LANES, tr, LANES), F32)],
        compiler_params=pltpu.CompilerParams(dimension_semantics=("parallel", "parallel")),
    )(x)


def _from_residues(y):
    dilation, lu, cols = y.shape
    t = dilation * lu
    if dilation == 1:
        return y.reshape(t, cols)
    tr = _pick(t, (RELAYOUT_ROWS,))
    tc = _pick(cols, (RELAYOUT_COLS, 256, 128))
    per = tr // dilation

    def body(y_ref, o_ref, s_ref):
        for c in range(tc // LANES):
            lanes = slice(c * LANES, (c + 1) * LANES)
            for r in range(dilation):
                s_ref[c, pl.ds(r, per, stride=dilation), :] = y_ref[r, :, lanes].astype(F32)
            o_ref[:, lanes] = s_ref[c].astype(o_ref.dtype)

    return pl.pallas_call(
        body, name=f"from_residues_{dilation}", out_shape=jax.ShapeDtypeStruct((t, cols), y.dtype),
        grid=(t // tr, cols // tc),
        in_specs=[pl.BlockSpec((dilation, per, tc), lambda i, j: (0, i, j))],
        out_specs=pl.BlockSpec((tr, tc), lambda i, j: (i, j)),
        scratch_shapes=[pltpu.VMEM((tc // LANES, tr, LANES), F32)],
        compiler_params=pltpu.CompilerParams(dimension_semantics=("parallel", "parallel")),
    )(y)


def _att_specs(hd, v_base, nb, pp):
    width = pp * LANES
    assert hd % width == 0 and v_base % width == 0

    def spec(base, shift):
        def imap(r, hp, n):
            row = jnp.minimum(n, nb - 1) if shift == 0 else jnp.maximum(n - 1, 0)
            return (r, row, base // width + hp)
        return pl.BlockSpec((None, ATT_BLK, width), imap)

    return [spec(0, 0), spec(hd, 1), spec(hd, 0), spec(v_base, 1), spec(v_base, 0)]


DELTA_LANE = 64


def _att_pre(qkv, g, dilation, gq2, gk2):
    t, width = qkv.shape
    hd = width // (3 * len(DIL_PATTERNS))
    tr = _pick(t, (RELAYOUT_ROWS,))
    tc = _pick(hd, (RELAYOUT_COLS, 256, 128))
    per = tr // dilation
    assert tr % dilation == 0 and (g * 3 * hd) % tc == 0

    def body(x_ref, gq_ref, gk_ref, o_ref, s_ref):
        gain = jnp.where(pl.program_id(1) * tc < hd, gq_ref[0:1, :], gk_ref[0:1, :])
        for c in range(tc // LANES):
            lanes = slice(c * LANES, (c + 1) * LANES)
            s_ref[c] = _qk_norm(x_ref[:, lanes].astype(F32), gain)[0]
            for r in range(dilation):
                o_ref[r, :, lanes] = s_ref[c, pl.ds(r, per, stride=dilation), :].astype(o_ref.dtype)

    vec_spec = pl.BlockSpec((SUBLANES, LANES), lambda i, j: (0, 0))
    return pl.pallas_call(
        body, name=f"att_pre_g{g}", out_shape=jax.ShapeDtypeStruct((dilation, t // dilation, 2 * hd), qkv.dtype),
        grid=(t // tr, 2 * hd // tc),
        in_specs=[pl.BlockSpec((tr, tc), lambda i, j: (i, g * 3 * hd // tc + j)), vec_spec, vec_spec],
        out_specs=pl.BlockSpec((dilation, per, tc), lambda i, j: (0, i, j)),
        scratch_shapes=[pltpu.VMEM((tc // LANES, tr, LANES), F32)],
        compiler_params=pltpu.CompilerParams(dimension_semantics=("parallel", "parallel")),
    )(qkv, gq2, gk2)


def _att_group_fwd(view, hd, slopes, g):
    qk_r, v_r, v_base = view
    dilation, lu, _ = qk_r.shape
    nb = lu // ATT_BLK
    assert nb * ATT_BLK == lu and hd % LANES == 0
    hpn = hd // LANES
    pp = math.gcd(ATT_PAIRS, hpn)
    scale = 1.0 / math.sqrt(ATT_HEAD_DIM)

    def body(q_ref, kp_ref, kc_ref, vp_ref, vc_ref, sl_ref, o_ref, l_ref):
        n = pl.program_id(2)
        lane = lax.broadcasted_iota(jnp.int32, (1, LANES), 1)
        first = (lane // ATT_HEAD_DIM) == 0
        valid, dist = _att_mask_bias(n, dilation)
        stats = jnp.zeros((ATT_BLK, LANES), F32)
        for pair in range(pp):
            cols = slice(pair * LANES, (pair + 1) * LANES)
            qn = q_ref[:, cols]
            kn16 = jnp.concatenate([kp_ref[:, cols], kc_ref[:, cols]], axis=0)
            v16 = jnp.concatenate([vp_ref[:, cols], vc_ref[:, cols]], axis=0)
            outs = []
            for hh in range(2):
                hmask = (lane // ATT_HEAD_DIM) == hh
                qh = jnp.where(hmask, qn, jnp.zeros_like(qn))
                s = lax.dot_general(qh, kn16, (((1,), (1,)), ((), ())), preferred_element_type=F32) * scale
                slope = _head_col(sl_ref[pair, 0:1, :], hmask)
                logits = jnp.where(valid, s - slope * dist, NEG)
                mx = jnp.max(logits, axis=-1, keepdims=True)
                pexp = jnp.exp(logits - mx)
                den = jnp.sum(pexp, axis=-1, keepdims=True)
                outs.append(jnp.dot(pexp.astype(BF16), v16, preferred_element_type=F32) / den)
                stats = jnp.where(lane == 2 * pair + hh, mx + jnp.log(den), stats)
            o_ref[:, cols] = jnp.where(first, outs[0], outs[1]).astype(BF16)
        l_ref[...] = stats

    out_spec = pl.BlockSpec((None, ATT_BLK, pp * LANES), lambda r, hp, n: (r, n, hp))
    stat_spec = pl.BlockSpec((None, ATT_BLK, LANES), lambda r, hp, n: (r, n, hp))
    o, lse = pl.pallas_call(
        body, name=f"att_fwd_g{g}",
        out_shape=(jax.ShapeDtypeStruct((dilation, lu, hd), BF16),
                   jax.ShapeDtypeStruct((dilation, lu, hpn // pp * LANES), F32)),
        grid=(dilation, hpn // pp, nb),
        in_specs=_att_specs(hd, v_base, nb, pp) + [pl.BlockSpec((pp, SUBLANES, LANES), lambda r, hp, n: (hp, 0, 0))],
        out_specs=(out_spec, stat_spec),
        compiler_params=pltpu.CompilerParams(dimension_semantics=("parallel", "parallel", "arbitrary")),
    )(qk_r, qk_r, qk_r, v_r, v_r, slopes)
    return _from_residues(o), _from_residues(lse)


def _att_merge(outs, lses):
    t, hd = outs[0].shape
    sw = lses[0].shape[1]
    pp = hd // sw
    tr = _pick(t, (256, 128))
    ng = len(outs)

    def body(*refs):
        o_refs, l_refs, o16_ref, lt_ref = refs[:ng], refs[ng:2 * ng], refs[2 * ng], refs[2 * ng + 1]
        lane = lax.broadcasted_iota(jnp.int32, (1, LANES), 1)
        first = (lane // ATT_HEAD_DIM) == 0
        for blk in range(sw // LANES):
            scols = slice(blk * LANES, (blk + 1) * LANES)
            stats = jnp.zeros((tr, LANES), F32)
            for pair in range(pp):
                cols = slice((blk * pp + pair) * LANES, (blk * pp + pair + 1) * LANES)
                weights = []
                for hh in range(2):
                    pick = lane == 2 * pair + hh
                    ls = [_head_col(r[:, scols], pick) for r in l_refs]
                    mx = functools.reduce(jnp.maximum, ls)
                    es = [jnp.exp(l - mx) for l in ls]
                    den = functools.reduce(jnp.add, es)
                    weights.append([e / den for e in es])
                    stats = jnp.where(pick, mx + jnp.log(den), stats)
                acc = jnp.zeros((tr, LANES), F32)
                for gi in range(ng):
                    acc = acc + jnp.where(first, weights[0][gi], weights[1][gi]) * o_refs[gi][:, cols].astype(F32)
                o16_ref[:, cols] = acc.astype(BF16)
            lt_ref[:, scols] = stats

    spec = pl.BlockSpec((tr, hd), lambda i: (i, 0))
    sspec = pl.BlockSpec((tr, sw), lambda i: (i, 0))
    return pl.pallas_call(
        body, name="att_merge",
        out_shape=(jax.ShapeDtypeStruct((t, hd), BF16), jax.ShapeDtypeStruct((t, sw), F32)), grid=(t // tr,),
        in_specs=[spec] * ng + [sspec] * ng, out_specs=(spec, sspec),
        compiler_params=pltpu.CompilerParams(dimension_semantics=("parallel",)),
    )(*outs, *lses)


def _att_bwd_prep(do, o16, lse_tot):
    t, hd = do.shape
    sw = lse_tot.shape[1]
    pp = hd // sw
    tr = _pick(t, (256, 128))

    def body(do_ref, o_ref, l_ref, d16_ref, st_ref):
        lane = lax.broadcasted_iota(jnp.int32, (1, LANES), 1)
        d16_ref[...] = do_ref[...].astype(BF16)
        for blk in range(sw // LANES):
            scols = slice(blk * LANES, (blk + 1) * LANES)
            stats = l_ref[:, scols]
            for pair in range(pp):
                cols = slice((blk * pp + pair) * LANES, (blk * pp + pair + 1) * LANES)
                prod = do_ref[:, cols] * o_ref[:, cols].astype(F32)
                for hh in range(2):
                    hmask = (lane // ATT_HEAD_DIM) == hh
                    delta = jnp.sum(jnp.where(hmask, prod, 0.0), axis=-1, keepdims=True)
                    stats = jnp.where(lane == DELTA_LANE + 2 * pair + hh, delta, stats)
            st_ref[:, scols] = stats

    spec = pl.BlockSpec((tr, hd), lambda i: (i, 0))
    sspec = pl.BlockSpec((tr, sw), lambda i: (i, 0))
    return pl.pallas_call(
        body, name="att_bwd_prep",
        out_shape=(jax.ShapeDtypeStruct((t, hd), BF16), jax.ShapeDtypeStruct((t, sw), F32)), grid=(t // tr,),
        in_specs=[spec, spec, sspec], out_specs=(spec, sspec),
        compiler_params=pltpu.CompilerParams(dimension_semantics=("parallel",)),
    )(do, o16, lse_tot)


def _att_post(buf, parts, qkv, gq2, gk2, g):
    dilation, lu, hd = parts[0].shape
    t = dilation * lu
    tr = _pick(t, (RELAYOUT_ROWS,))
    per = tr // dilation

    def body(*refs):
        raw_ref, gq_ref, gk_ref = refs[3:6]
        o_ref, dgq_ref, dgk_ref, s_ref = refs[-4:]

        @pl.when(pl.program_id(0) == 0)
        def _():
            dgq_ref[...] = jnp.zeros_like(dgq_ref)
            dgk_ref[...] = jnp.zeros_like(dgk_ref)

        for sec, y_ref in enumerate(refs[:3]):
            gsum = jnp.zeros((1, LANES), F32)
            for c in range(hd // LANES):
                lanes = slice(c * LANES, (c + 1) * LANES)
                out_lanes = slice(sec * hd + c * LANES, sec * hd + (c + 1) * LANES)
                for r in range(dilation):
                    s_ref[pl.ds(r, per, stride=dilation), :] = y_ref[r, :, lanes].astype(F32)
                d = s_ref[...]
                if sec < 2:
                    gain = (gq_ref if sec == 0 else gk_ref)[0:1, :]
                    _, xhat, rstd = _qk_norm(raw_ref[:, out_lanes].astype(F32), gain)
                    d, part = _qk_norm_bwd(d, xhat, rstd, gain)
                    gsum = gsum + jnp.sum(part, axis=0, keepdims=True)
                o_ref[:, out_lanes] = d.astype(o_ref.dtype)
            if sec < 2:
                acc = dgq_ref if sec == 0 else dgk_ref
                acc[...] += jnp.broadcast_to(gsum, acc.shape)

    part_spec = pl.BlockSpec((dilation, per, hd), lambda i: (0, i, 0))
    slab_spec = pl.BlockSpec((tr, 3 * hd), lambda i: (i, g))
    vec_spec = pl.BlockSpec((SUBLANES, LANES), lambda i: (0, 0))
    vec_shape = jax.ShapeDtypeStruct((SUBLANES, LANES), F32)
    return pl.pallas_call(
        body, name=f"att_post_g{g}", out_shape=(jax.ShapeDtypeStruct(qkv.shape, qkv.dtype), vec_shape, vec_shape),
        grid=(t // tr,),
        in_specs=[part_spec] * 3 + [slab_spec, vec_spec, vec_spec] + (
            [] if buf is None else [pl.BlockSpec(memory_space=pl.ANY)]),
        out_specs=(slab_spec, vec_spec, vec_spec),
        scratch_shapes=[pltpu.VMEM((tr, LANES), F32)],
        input_output_aliases={} if buf is None else {6: 0},
        compiler_params=pltpu.CompilerParams(dimension_semantics=("arbitrary",)),
    )(*parts, qkv, gq2, gk2, *(() if buf is None else (buf,)))


def _att_group_bwd(view, hd, slopes, stats, do16, g):
    qk_r, v_r, v_base = view
    dilation, lu, _ = qk_r.shape
    nb = lu // ATT_BLK
    hpn = hd // LANES
    pp = math.gcd(ATT_PAIRS, hpn)
    hbn = hpn // pp
    scale = 1.0 / math.sqrt(ATT_HEAD_DIM)

    def body(q_ref, kp_ref, kc_ref, vp_ref, vc_ref, sl_ref, st_ref, do_ref, dq_ref, dk_ref, dv_ref, ck_ref, cv_ref):
        n = pl.program_id(2)
        lane = lax.broadcasted_iota(jnp.int32, (1, LANES), 1)

        @pl.when(n == 0)
        def _():
            ck_ref[...] = jnp.zeros_like(ck_ref)
            cv_ref[...] = jnp.zeros_like(cv_ref)

        @pl.when(n < nb)
        def _():
            valid, dist = _att_mask_bias(n, dilation)
            stats = st_ref[...]
            for pair in range(pp):
                cols = slice(pair * LANES, (pair + 1) * LANES)
                qn = q_ref[:, cols]
                kn16 = jnp.concatenate([kp_ref[:, cols], kc_ref[:, cols]], axis=0)
                v16 = jnp.concatenate([vp_ref[:, cols], vc_ref[:, cols]], axis=0)
                dov = do_ref[:, cols]
                dq_acc = jnp.zeros((ATT_BLK, LANES), F32)
                dk_acc = jnp.zeros((2 * ATT_BLK, LANES), F32)
                dv_acc = jnp.zeros((2 * ATT_BLK, LANES), F32)
                for hh in range(2):
                    hmask = (lane // ATT_HEAD_DIM) == hh
                    qh = jnp.where(hmask, qn, jnp.zeros_like(qn))
                    doh = jnp.where(hmask, dov, jnp.zeros_like(dov))
                    s = lax.dot_general(qh, kn16, (((1,), (1,)), ((), ())), preferred_element_type=F32) * scale
                    slope = _head_col(sl_ref[pair, 0:1, :], hmask)
                    lse = _head_col(stats, lane == 2 * pair + hh)
                    delta = _head_col(stats, lane == DELTA_LANE + 2 * pair + hh)
                    pr = jnp.exp(jnp.where(valid, s - slope * dist - lse, NEG))
                    dp = lax.dot_general(doh, v16, (((1,), (1,)), ((), ())), preferred_element_type=F32)
                    ds = (pr * (dp - delta) * scale).astype(BF16)
                    dq_acc = dq_acc + jnp.where(hmask, jnp.dot(ds, kn16, preferred_element_type=F32), 0.0)
                    dk_acc = dk_acc + lax.dot_general(ds, qh, (((0,), (0,)), ((), ())), preferred_element_type=F32)
                    dv_acc = dv_acc + lax.dot_general(pr.astype(BF16), doh, (((0,), (0,)), ((), ())),
                                                      preferred_element_type=F32)
                dq_ref[:, cols] = dq_acc.astype(dq_ref.dtype)
                dk_ref[:, cols] = (ck_ref[:, cols] + dk_acc[:ATT_BLK]).astype(dk_ref.dtype)
                dv_ref[:, cols] = (cv_ref[:, cols] + dv_acc[:ATT_BLK]).astype(dv_ref.dtype)
                ck_ref[:, cols] = dk_acc[ATT_BLK:]
                cv_ref[:, cols] = dv_acc[ATT_BLK:]

        @pl.when(n == nb)
        def _():
            dk_ref[...] = ck_ref[...].astype(dk_ref.dtype)
            dv_ref[...] = cv_ref[...].astype(dv_ref.dtype)

    width = pp * LANES
    q_out = pl.BlockSpec((None, ATT_BLK, width), lambda r, hp, n: (r, jnp.minimum(n, nb - 1), hp))
    kv_out = pl.BlockSpec((None, ATT_BLK, width), lambda r, hp, n: (r, jnp.maximum(n - 1, 0), hp))
    st_spec = pl.BlockSpec((None, ATT_BLK, LANES), lambda r, hp, n: (r, jnp.minimum(n, nb - 1), hp))
    shp = jax.ShapeDtypeStruct((dilation, lu, hd), BF16)
    return pl.pallas_call(
        body, name=f"att_bwd_g{g}", out_shape=(shp, shp, shp), grid=(dilation, hbn, nb + 1),
        in_specs=_att_specs(hd, v_base, nb, pp) + [
            pl.BlockSpec((pp, SUBLANES, LANES), lambda r, hp, n: (hp, 0, 0)), st_spec, q_out],
        out_specs=(q_out, kv_out, kv_out),
        scratch_shapes=[pltpu.VMEM((ATT_BLK, width), F32), pltpu.VMEM((ATT_BLK, width), F32)],
        compiler_params=pltpu.CompilerParams(dimension_semantics=("parallel", "parallel", "arbitrary")),
    )(qk_r, qk_r, qk_r, v_r, v_r, slopes, _to_residues(stats, dilation), _to_residues(do16, dilation))


def _att_consts(q_gain, k_gain, hd):
    heads = hd // ATT_HEAD_DIM
    gq2 = jnp.broadcast_to(jnp.tile(q_gain, 2)[None], (SUBLANES, LANES))
    gk2 = jnp.broadcast_to(jnp.tile(k_gain, 2)[None], (SUBLANES, LANES))
    sl = 2.0 ** (-8.0 * jnp.arange(1, heads + 1, dtype=F32) / heads)
    slopes = jnp.broadcast_to(jnp.repeat(sl, ATT_HEAD_DIM).reshape(hd // LANES, 1, LANES), (hd // LANES, SUBLANES, LANES))
    return gq2, gk2, slopes


def _attention_core_fwd(qkv, q_gain, k_gain):
    hd = qkv.shape[1] // (3 * len(DIL_PATTERNS))
    gq2, gk2, slopes = _att_consts(q_gain, k_gain, hd)
    outs, lses, views = [], [], []
    for g, (_, dilation) in enumerate(DIL_PATTERNS):
        v_col = (3 * g + 2) * hd
        view = (_att_pre(qkv, g, dilation, gq2, gk2),) + (
            (_to_residues(qkv, 1), v_col) if dilation == 1 else (_to_residues(qkv, dilation, v_col, hd), 0))
        o_g, l_g = _att_group_fwd(view, hd, slopes, g)
        outs.append(o_g)
        lses.append(l_g)
        views.append(view)
    o16, lse_tot = _att_merge(outs, lses)
    return o16, (qkv, views, q_gain, k_gain, o16, lse_tot)


def _attention_core_bwd(res, do):
    qkv, views, q_gain, k_gain, o16, lse_tot = res
    hd = o16.shape[1]
    gq2, gk2, slopes = _att_consts(q_gain, k_gain, hd)
    do16, stats = _att_bwd_prep(do, o16, lse_tot)
    dqkv, dgq, dgk = None, 0.0, 0.0
    for g, view in enumerate(views):
        parts = _att_group_bwd(view, hd, slopes, stats, do16, g)
        dqkv, a, b = _att_post(dqkv, parts, qkv, gq2, gk2, g)
        dgq = dgq + a[0].reshape(-1, ATT_HEAD_DIM).sum(0)
        dgk = dgk + b[0].reshape(-1, ATT_HEAD_DIM).sum(0)
    return dqkv, dgq, dgk


HALO = 8


def _silu(v):
    return v * jax.nn.sigmoid(v)


def _silu_grad(v):
    s = jax.nn.sigmoid(v)
    return s * (1.0 + v * (1.0 - s))


def _halo_rows(dtype):
    return BF16_ROWS if dtype == BF16 else HALO


def _conv_fwd(zx, conv_w, conv_b, d_inner):
    t = zx.shape[0]
    conv_dim = conv_w.shape[1]
    cb = _pick(d_inner, (1024, 512, 256, 128))
    assert conv_dim % cb == 0
    tr = _pick(t, (256, 128))
    off = d_inner // cb
    hx = _halo_rows(zx.dtype)

    def body(x_ref, h_ref, w_ref, b_ref, o_ref):
        i = pl.program_id(1)
        halo = jnp.where(i > 0, h_ref[...].astype(F32), 0.0)
        ext = jnp.concatenate([halo, x_ref[...].astype(F32)], axis=0)
        acc = jnp.broadcast_to(b_ref[...], (tr, cb))
        for k in range(CONV_WIDTH):
            s = CONV_WIDTH - 1 - k
            sh = ext if s == 0 else pltpu.roll(ext, shift=s, axis=0)
            acc = acc + w_ref[k:k + 1, :] * sh[hx:hx + tr]
        o_ref[...] = acc.astype(o_ref.dtype)

    return pl.pallas_call(
        body, name="ssm_conv_fwd", out_shape=jax.ShapeDtypeStruct((t, conv_dim), BF16),
        grid=(conv_dim // cb, t // tr),
        in_specs=[pl.BlockSpec((tr, cb), lambda j, i: (i, off + j)),
                  pl.BlockSpec((hx, cb), lambda j, i: (jnp.maximum(i * (tr // hx) - 1, 0), off + j)),
                  pl.BlockSpec((CONV_WIDTH, cb), lambda j, i: (0, j)),
                  pl.BlockSpec((1, cb), lambda j, i: (0, j))],
        out_specs=pl.BlockSpec((tr, cb), lambda j, i: (i, j)),
        compiler_params=pltpu.CompilerParams(dimension_semantics=("parallel", "parallel")),
    )(zx, zx, conv_w, conv_b.reshape(1, -1))


def _conv_bwd(zx, conv_w, dpre, dzx, d_inner, col0):
    t, width = zx.shape
    conv_dim = dpre.shape[1]
    cb = _pick(conv_dim, (1024, 512, 256, 128))
    assert (d_inner + col0) % cb == 0
    tr = _pick(t, (256, 128))
    off = (d_inner + col0) // cb
    woff = col0 // cb
    nr = t // tr
    hx = _halo_rows(zx.dtype)
    hd = _halo_rows(dpre.dtype)

    def body(x_ref, h_ref, w_ref, d_ref, dn_ref, dzx_in, dx_ref, dw_ref, db_ref):
        i = pl.program_id(1)

        @pl.when(i == 0)
        def _():
            dw_ref[...] = jnp.zeros_like(dw_ref)
            db_ref[...] = jnp.zeros_like(db_ref)

        halo = jnp.where(i > 0, h_ref[...].astype(F32), 0.0)
        ext = jnp.concatenate([halo, x_ref[...].astype(F32)], axis=0)
        d = d_ref[...].astype(F32)
        dext = jnp.concatenate([d, jnp.where(i < nr - 1, dn_ref[...].astype(F32), 0.0)], axis=0)
        dx = jnp.zeros((tr, cb), F32)
        for k in range(CONV_WIDTH):
            s = CONV_WIDTH - 1 - k
            fut = dext if s == 0 else pltpu.roll(dext, shift=tr + hd - s, axis=0)
            dx = dx + w_ref[k:k + 1, :] * fut[:tr]
            past = ext if s == 0 else pltpu.roll(ext, shift=s, axis=0)
            dw_ref[k:k + 1, :] += jnp.sum(d * past[hx:hx + tr], axis=0, keepdims=True)
        dx_ref[...] = dx.astype(dx_ref.dtype)
        db_ref[...] += jnp.sum(d, axis=0, keepdims=True)

    last_halo = t // hd - 1
    return pl.pallas_call(
        body, name=f"ssm_conv_bwd_{col0}",
        out_shape=(jax.ShapeDtypeStruct(dzx.shape, dzx.dtype), jax.ShapeDtypeStruct((CONV_WIDTH, conv_dim), F32),
                   jax.ShapeDtypeStruct((1, conv_dim), F32)),
        grid=(conv_dim // cb, nr),
        in_specs=[pl.BlockSpec((tr, cb), lambda j, i: (i, off + j)),
                  pl.BlockSpec((hx, cb), lambda j, i: (jnp.maximum(i * (tr // hx) - 1, 0), off + j)),
                  pl.BlockSpec((CONV_WIDTH, cb), lambda j, i: (0, woff + j)),
                  pl.BlockSpec((tr, cb), lambda j, i: (i, j)),
                  pl.BlockSpec((hd, cb), lambda j, i: (jnp.minimum((i + 1) * (tr // hd), last_halo), j)),
                  pl.BlockSpec(memory_space=pl.ANY)],
        out_specs=(pl.BlockSpec((tr, cb), lambda j, i: (i, off + j)),
                   pl.BlockSpec((CONV_WIDTH, cb), lambda j, i: (0, j)),
                   pl.BlockSpec((1, cb), lambda j, i: (0, j))),
        input_output_aliases={5: 0},
        compiler_params=pltpu.CompilerParams(dimension_semantics=("parallel", "arbitrary")),
    )(zx, zx, conv_w, dpre, dpre, dzx)


def _eye(n):
    return lax.broadcasted_iota(jnp.int32, (n, n), 0) == lax.broadcasted_iota(jnp.int32, (n, n), 1)


def _row_to_col(row):
    n = row.shape[1]
    return jnp.sum(jnp.where(_eye(n), row, 0.0), axis=1, keepdims=True)


def _col_to_row(col):
    n = col.shape[0]
    return jnp.sum(jnp.where(_eye(n), col, 0.0), axis=0, keepdims=True)


def _pair_lanes(c0, c1):
    lane = lax.broadcasted_iota(jnp.int32, (1, LANES), 1)
    return jnp.where(lane < SSM_HEAD_DIM, c0, c1)


def _ssd_chunk_common(pre_x_ref, pre_b_ref, pre_c_ref, dtr_ref, bias_ref, alog_ref, cs_ref):
    cl = SSD_CHUNK
    hpg = dtr_ref.shape[0]
    x = _silu(pre_x_ref[...].astype(F32))
    b16 = _silu(pre_b_ref[...].astype(F32)).astype(BF16)
    c16 = _silu(pre_c_ref[...].astype(F32)).astype(BF16)
    dt = jax.nn.softplus(dtr_ref[...] + bias_ref[...])
    a = -jnp.exp(alog_ref[...])
    li = lax.broadcasted_iota(jnp.int32, (cl, cl), 0)
    si = lax.broadcasted_iota(jnp.int32, (cl, cl), 1)
    upper = (li <= si).astype(F32)
    cs_ref[0:hpg, :] = jnp.dot(dt * a, upper, precision=lax.Precision.HIGHEST, preferred_element_type=F32)
    cs_ref[hpg:2 * hpg, :] = dt
    g = lax.dot_general(c16, b16, (((1,), (1,)), ((), ())), preferred_element_type=F32)
    return x, b16, c16, dt, a, g, li >= si


def _ssd_fwd(pre, dtT, bias, alog, dskip_lanes, d_inner):
    t = pre.shape[0]
    cl = SSD_CHUNK
    nc = t // cl
    ng = SSM_GROUPS
    hpg = dtT.shape[1]
    gw = hpg * SSM_HEAD_DIM
    assert d_inner == ng * gw and hpg % 2 == 0
    bo = d_inner // SSM_STATE

    def body(px_ref, pb_ref, pc_ref, dtr_ref, bias_ref, alog_ref, dsk_ref, y_ref, st_ref, s_ref, cs_ref):
        c = pl.program_id(1)

        @pl.when(c == 0)
        def _():
            s_ref[...] = jnp.zeros_like(s_ref)

        x, b16, c16, dt, a, g, causal = _ssd_chunk_common(px_ref, pb_ref, pc_ref, dtr_ref, bias_ref, alog_ref, cs_ref)
        st_ref[...] = s_ref[...]
        yoff = lax.dot_general(c16, s_ref[...].astype(BF16), (((1,), (1,)), ((), ())), preferred_element_type=F32)
        xde_parts = []
        for j in range(hpg // 2):
            cols = slice(j * LANES, (j + 1) * LANES)
            xp = x[:, cols]
            dcol, ecol, ocol, ms = [], [], [], []
            for hh in range(2):
                h = 2 * j + hh
                cs_row = cs_ref[h:h + 1, :]
                cs_col = _row_to_col(cs_row)
                dcol.append(_row_to_col(cs_ref[hpg + h:hpg + h + 1, :]))
                ecol.append(jnp.exp(cs_ref[h:h + 1, cl - 1:cl] - cs_col))
                ocol.append(jnp.exp(cs_col))
                lm = jnp.where(causal, jnp.exp(jnp.minimum(cs_col - cs_row, 0.0)), 0.0)
                ms.append((g * lm).astype(BF16))
            xd = xp * _pair_lanes(dcol[0], dcol[1])
            xd16 = xd.astype(BF16)
            yd = _pair_lanes(1.0, 0.0) * jnp.dot(ms[0], xd16, preferred_element_type=F32) \
                + _pair_lanes(0.0, 1.0) * jnp.dot(ms[1], xd16, preferred_element_type=F32)
            y_ref[:, cols] = yd + yoff[:, cols] * _pair_lanes(ocol[0], ocol[1]) + xp * dsk_ref[0:1, cols]
            xde_parts.append((xd * _pair_lanes(ecol[0], ecol[1])).astype(BF16))
        new = lax.dot_general(jnp.concatenate(xde_parts, axis=1), b16, (((0,), (0,)), ((), ())),
                              preferred_element_type=F32)
        for h in range(hpg):
            rows = slice(h * SSM_HEAD_DIM, (h + 1) * SSM_HEAD_DIM)
            s_ref[rows, :] = s_ref[rows, :] * jnp.exp(cs_ref[h:h + 1, cl - 1:cl]) + new[rows, :]

    vec = lambda n: pl.BlockSpec((None, hpg, n), lambda gi, c: (gi, 0, 0))
    return pl.pallas_call(
        body, name="ssd_fwd",
        out_shape=(jax.ShapeDtypeStruct((t, d_inner), F32), jax.ShapeDtypeStruct((ng, nc, gw, SSM_STATE), F32)),
        grid=(ng, nc),
        in_specs=[pl.BlockSpec((cl, gw), lambda gi, c: (c, gi)),
                  pl.BlockSpec((cl, SSM_STATE), lambda gi, c: (c, bo + gi)),
                  pl.BlockSpec((cl, SSM_STATE), lambda gi, c: (c, bo + ng + gi)),
                  pl.BlockSpec((None, hpg, cl), lambda gi, c: (gi, 0, c)),
                  vec(1), vec(1),
                  pl.BlockSpec((1, gw), lambda gi, c: (0, gi))],
        out_specs=(pl.BlockSpec((cl, gw), lambda gi, c: (c, gi)),
                   pl.BlockSpec((None, None, gw, SSM_STATE), lambda gi, c: (gi, c, 0, 0))),
        scratch_shapes=[pltpu.VMEM((gw, SSM_STATE), F32), pltpu.VMEM((2 * hpg, cl), F32)],
        compiler_params=pltpu.CompilerParams(dimension_semantics=("parallel", "arbitrary")),
    )(pre, pre, pre, dtT, bias, alog, dskip_lanes)


def _ssd_bwd(pre, dtT, bias, alog, dskip_lanes, states, dy, d_inner):
    t, conv_dim = pre.shape
    cl = SSD_CHUNK
    nc = t // cl
    ng = SSM_GROUPS
    hpg = dtT.shape[1]
    gw = hpg * SSM_HEAD_DIM
    bo = d_inner // SSM_STATE

    def body(px_ref, pb_ref, pc_ref, dtr_ref, bias_ref, alog_ref, dsk_ref, st_ref, dy_ref,
             dx_ref, db_ref, dc_ref, ddt_ref, acc_ref, dsk_out, ds_ref, cs_ref, dcs_ref):
        c = pl.program_id(1)

        @pl.when(c == 0)
        def _():
            ds_ref[...] = jnp.zeros_like(ds_ref)
            acc_ref[...] = jnp.zeros_like(acc_ref)
            dsk_out[...] = jnp.zeros_like(dsk_out)

        x, b16, c16, dt, a, g, causal = _ssd_chunk_common(px_ref, pb_ref, pc_ref, dtr_ref, bias_ref, alog_ref, cs_ref)
        s_prev = st_ref[...]
        s16 = s_prev.astype(BF16)
        ds = ds_ref[...]
        ds16 = ds.astype(BF16)
        dyv = dy_ref[...]
        yoff = lax.dot_general(c16, s16, (((1,), (1,)), ((), ())), preferred_element_type=F32)
        bds = lax.dot_general(b16, ds16, (((1,), (1,)), ((), ())), preferred_element_type=F32)
        dg = jnp.zeros((cl, cl), F32)
        xde_parts, dye_parts = [], []
        lane = lax.broadcasted_iota(jnp.int32, (1, LANES), 1)
        for j in range(hpg // 2):
            cols = slice(j * LANES, (j + 1) * LANES)
            xp, dyp = x[:, cols], dyv[:, cols]
            dcol, ecol, ocol, lms = [], [], [], []
            for hh in range(2):
                h = 2 * j + hh
                cs_row = cs_ref[h:h + 1, :]
                cs_col = _row_to_col(cs_row)
                dcol.append(_row_to_col(cs_ref[hpg + h:hpg + h + 1, :]))
                ecol.append(jnp.exp(cs_ref[h:h + 1, cl - 1:cl] - cs_col))
                ocol.append(jnp.exp(cs_col))
                lms.append(jnp.where(causal, jnp.exp(jnp.minimum(cs_col - cs_row, 0.0)), 0.0))
            dlanes, elanes, olanes = _pair_lanes(*dcol), _pair_lanes(*ecol), _pair_lanes(*ocol)
            xd = xp * dlanes
            xd16 = xd.astype(BF16)
            xde = xd * elanes
            yoffp = yoff[:, cols] * olanes
            bdsp = bds[:, cols]
            dxd = bdsp * elanes
            for hh in range(2):
                h = 2 * j + hh
                hmask = (lane // SSM_HEAD_DIM) == hh
                dyh16 = jnp.where(hmask, dyp, 0.0).astype(BF16)
                m = g * lms[hh]
                dm = lax.dot_general(dyh16, xd16, (((1,), (1,)), ((), ())), preferred_element_type=F32)
                w = dm * m
                dg = dg + dm * lms[hh]
                dxd = dxd + lax.dot_general(m.astype(BF16), dyh16, (((0,), (0,)), ((), ())),
                                            preferred_element_type=F32)
                term = jnp.sum(jnp.where(hmask, xde * bdsp, 0.0), axis=1, keepdims=True)
                dcs_col = (jnp.sum(w, axis=1, keepdims=True)
                           + jnp.sum(jnp.where(hmask, dyp * yoffp, 0.0), axis=1, keepdims=True) - term)
                rows = slice(h * SSM_HEAD_DIM, (h + 1) * SSM_HEAD_DIM)
                dec = jnp.exp(cs_ref[h:h + 1, cl - 1:cl])
                tail = jnp.sum(term, axis=0, keepdims=True) + dec * jnp.sum(
                    jnp.sum(s_prev[rows, :] * ds[rows, :], axis=1, keepdims=True), axis=0, keepdims=True)
                last = lax.broadcasted_iota(jnp.int32, (1, cl), 1) == cl - 1
                dcs_ref[h:h + 1, :] = _col_to_row(dcs_col) - jnp.sum(w, axis=0, keepdims=True) + jnp.where(last, tail, 0.0)
                dcs_ref[hpg + h:hpg + h + 1, :] = _col_to_row(
                    jnp.sum(jnp.where(hmask, dxd * xp, 0.0), axis=1, keepdims=True))
            dx_act = dxd * dlanes + dyp * dsk_ref[0:1, cols]
            dx_ref[:, cols] = (dx_act * _silu_grad(px_ref[:, cols].astype(F32))).astype(dx_ref.dtype)
            dsk_out[0:1, cols] += jnp.sum(dyp * xp, axis=0, keepdims=True)
            xde_parts.append(xde.astype(BF16))
            dye_parts.append((dyp * olanes).astype(BF16))
        xde16 = jnp.concatenate(xde_parts, axis=1)
        dye16 = jnp.concatenate(dye_parts, axis=1)
        dg16 = dg.astype(BF16)
        dc_act = jnp.dot(dg16, b16, preferred_element_type=F32) + jnp.dot(dye16, s16, preferred_element_type=F32)
        db_act = lax.dot_general(dg16, c16, (((0,), (0,)), ((), ())), preferred_element_type=F32) \
            + jnp.dot(xde16, ds16, preferred_element_type=F32)
        dc_ref[...] = (dc_act * _silu_grad(pc_ref[...].astype(F32))).astype(dc_ref.dtype)
        db_ref[...] = (db_act * _silu_grad(pb_ref[...].astype(F32))).astype(db_ref.dtype)
        ds_new = lax.dot_general(dye16, c16, (((0,), (0,)), ((), ())), preferred_element_type=F32)
        for h in range(hpg):
            rows = slice(h * SSM_HEAD_DIM, (h + 1) * SSM_HEAD_DIM)
            ds_ref[rows, :] = ds[rows, :] * jnp.exp(cs_ref[h:h + 1, cl - 1:cl]) + ds_new[rows, :]
        li = lax.broadcasted_iota(jnp.int32, (cl, cl), 0)
        si = lax.broadcasted_iota(jnp.int32, (cl, cl), 1)
        d_adt = jnp.dot(dcs_ref[0:hpg, :], (li >= si).astype(F32), precision=lax.Precision.HIGHEST,
                        preferred_element_type=F32)
        ddt = d_adt * a + dcs_ref[hpg:2 * hpg, :]
        ddt_raw = ddt * jax.nn.sigmoid(dtr_ref[...] + bias_ref[...])
        ddt_ref[...] = ddt_raw
        acc_ref[0:hpg, :] += d_adt * dt
        acc_ref[hpg:2 * hpg, :] += ddt_raw

    rc = lambda c: nc - 1 - c
    vec = lambda n: pl.BlockSpec((None, hpg, n), lambda gi, c: (gi, 0, 0))
    x_spec = pl.BlockSpec((cl, gw), lambda gi, c: (rc(c), gi))
    b_spec = pl.BlockSpec((cl, SSM_STATE), lambda gi, c: (rc(c), bo + gi))
    c_spec = pl.BlockSpec((cl, SSM_STATE), lambda gi, c: (rc(c), bo + ng + gi))
    dt_spec = pl.BlockSpec((None, hpg, cl), lambda gi, c: (gi, 0, rc(c)))
    return pl.pallas_call(
        body, name="ssd_bwd",
        out_shape=(jax.ShapeDtypeStruct((t, d_inner), BF16), jax.ShapeDtypeStruct((t, ng * SSM_STATE), BF16),
                   jax.ShapeDtypeStruct((t, ng * SSM_STATE), BF16), jax.ShapeDtypeStruct(dtT.shape, F32),
                   jax.ShapeDtypeStruct((ng, 2 * hpg, cl), F32), jax.ShapeDtypeStruct((1, d_inner), F32)),
        grid=(ng, nc),
        in_specs=[x_spec, b_spec, c_spec, dt_spec, vec(1), vec(1),
                  pl.BlockSpec((1, gw), lambda gi, c: (0, gi)),
                  pl.BlockSpec((None, None, gw, SSM_STATE), lambda gi, c: (gi, rc(c), 0, 0)),
                  x_spec],
        out_specs=(x_spec, pl.BlockSpec((cl, SSM_STATE), lambda gi, c: (rc(c), gi)),
                   pl.BlockSpec((cl, SSM_STATE), lambda gi, c: (rc(c), gi)), dt_spec,
                   pl.BlockSpec((None, 2 * hpg, cl), lambda gi, c: (gi, 0, 0)),
                   pl.BlockSpec((1, gw), lambda gi, c: (0, gi))),
        scratch_shapes=[pltpu.VMEM((gw, SSM_STATE), F32), pltpu.VMEM((2 * hpg, cl), F32),
                        pltpu.VMEM((2 * hpg, cl), F32)],
        compiler_params=pltpu.CompilerParams(dimension_semantics=("parallel", "arbitrary")),
    )(pre, pre, pre, dtT, bias, alog, dskip_lanes, states, dy)


def _gate_norm_fwd(y, zx, norm_w, d_inner):
    t = y.shape[0]
    tr = _pick(t, (256, 128))
    gs = d_inner // SSM_GROUPS

    def body(y_ref, z_ref, w_ref, o_ref):
        for gi in range(SSM_GROUPS):
            cols = slice(gi * gs, (gi + 1) * gs)
            v = y_ref[:, cols] * _silu(z_ref[:, cols].astype(F32))
            r = lax.rsqrt(jnp.mean(v * v, axis=-1, keepdims=True) + NORM_EPS)
            o_ref[:, cols] = (v * r * w_ref[0:1, cols]).astype(BF16)

    spec = pl.BlockSpec((tr, d_inner), lambda i: (i, 0))
    return pl.pallas_call(
        body, name="ssm_gate_norm_fwd", out_shape=jax.ShapeDtypeStruct((t, d_inner), BF16), grid=(t // tr,),
        in_specs=[spec, spec, pl.BlockSpec((1, d_inner), lambda i: (0, 0))], out_specs=spec,
        compiler_params=pltpu.CompilerParams(dimension_semantics=("parallel",)),
    )(y, zx, norm_w.reshape(1, -1))


def _gate_norm_bwd(y, zx, norm_w, dout, d_inner):
    t, width = zx.shape
    tr = _pick(t, (256, 128))
    gs = d_inner // SSM_GROUPS

    def body(y_ref, z_ref, w_ref, do_ref, dy_ref, dz_ref, dw_ref):
        @pl.when(pl.program_id(0) == 0)
        def _():
            dw_ref[...] = jnp.zeros_like(dw_ref)

        for gi in range(SSM_GROUPS):
            cols = slice(gi * gs, (gi + 1) * gs)
            yv, zv = y_ref[:, cols], z_ref[:, cols].astype(F32)
            sz = _silu(zv)
            v = yv * sz
            r = lax.rsqrt(jnp.mean(v * v, axis=-1, keepdims=True) + NORM_EPS)
            vhat = v * r
            dn = do_ref[:, cols].astype(F32)
            dw_ref[0:1, cols] += jnp.sum(dn * vhat, axis=0, keepdims=True)
            dvh = dn * w_ref[0:1, cols]
            dv = r * (dvh - vhat * jnp.mean(dvh * vhat, axis=-1, keepdims=True))
            dy_ref[:, cols] = dv * sz
            dz_ref[:, cols] = (dv * yv * _silu_grad(zv)).astype(dz_ref.dtype)

    spec = pl.BlockSpec((tr, d_inner), lambda i: (i, 0))
    wspec = pl.BlockSpec((1, d_inner), lambda i: (0, 0))
    return pl.pallas_call(
        body, name="ssm_gate_norm_bwd",
        out_shape=(jax.ShapeDtypeStruct((t, d_inner), F32), jax.ShapeDtypeStruct((t, width), zx.dtype),
                   jax.ShapeDtypeStruct((1, d_inner), F32)),
        grid=(t // tr,),
        in_specs=[spec, spec, wspec, spec], out_specs=(spec, spec, wspec),
        compiler_params=pltpu.CompilerParams(dimension_semantics=("arbitrary",)),
    )(y, zx, norm_w.reshape(1, -1), dout)


def _ssm_small(dt_raw, dt_bias, a_log, d_skip):
    heads = dt_raw.shape[1]
    hpg = heads // SSM_GROUPS
    dtT = dt_raw.T.reshape(SSM_GROUPS, hpg, -1)
    return (dtT, dt_bias.reshape(SSM_GROUPS, hpg, 1), a_log.reshape(SSM_GROUPS, hpg, 1),
            jnp.repeat(d_skip, SSM_HEAD_DIM).reshape(1, -1))


def _ssm_core_fwd(zx, dt_raw, conv_w, conv_b, dt_bias, a_log, d_skip, norm_w):
    d_inner = norm_w.shape[0]
    pre = _conv_fwd(zx, conv_w, conv_b, d_inner)
    dtT, bias, alog, dsk = _ssm_small(dt_raw, dt_bias, a_log, d_skip)
    y, states = _ssd_fwd(pre, dtT, bias, alog, dsk, d_inner)
    out = _gate_norm_fwd(y, zx, norm_w, d_inner)
    return out, (zx, dt_raw, conv_w, dt_bias, a_log, d_skip, norm_w, pre, y, states)


def _ssm_core_bwd(res, dout):
    zx, dt_raw, conv_w, dt_bias, a_log, d_skip, norm_w, pre, y, states = res
    d_inner = norm_w.shape[0]
    heads = dt_raw.shape[1]
    dy, dzx, dnorm = _gate_norm_bwd(y, zx, norm_w, dout, d_inner)
    dtT, bias, alog, dsk = _ssm_small(dt_raw, dt_bias, a_log, d_skip)
    dx, db, dc, ddtT, acc, dsk_l = _ssd_bwd(pre, dtT, bias, alog, dsk, states, dy, d_inner)
    dws, dbs, col0 = [], [], 0
    for part in (dx, db, dc):
        dzx, dw_part, db_part = _conv_bwd(zx, conv_w, part, dzx, d_inner, col0)
        dws.append(dw_part)
        dbs.append(db_part)
        col0 += part.shape[1]
    dconv_w, dconv_b = jnp.concatenate(dws, axis=1), jnp.concatenate(dbs, axis=1)
    d_dt_raw = ddtT.reshape(heads, -1).T
    hpg = heads // SSM_GROUPS
    da = acc[:, :hpg].sum(-1).reshape(heads)
    d_bias = acc[:, hpg:].sum(-1).reshape(heads)
    d_alog = da * (-jnp.exp(a_log))
    d_dskip = dsk_l.reshape(heads, SSM_HEAD_DIM).sum(-1)
    return dzx, d_dt_raw, dconv_w, dconv_b.reshape(-1), d_bias, d_alog, d_dskip, dnorm.reshape(-1)


def _rows_call(body, name, ins, outs, acc_outs=(), rows=256):
    t = max(a.shape[0] for a in ins)
    tr = _pick(t, (rows, 128, 64, 32, 16, 8))

    def spec(a):
        if a.shape[0] == t:
            return pl.BlockSpec((tr, a.shape[1]), lambda i: (i, 0))
        return pl.BlockSpec(a.shape, lambda i: (0, 0))

    return pl.pallas_call(
        body, name=name, out_shape=tuple(outs) + tuple(acc_outs), grid=(t // tr,),
        in_specs=[spec(a) for a in ins],
        out_specs=tuple(spec(a) for a in outs) + tuple(pl.BlockSpec(a.shape, lambda i: (0, 0)) for a in acc_outs),
        compiler_params=pltpu.CompilerParams(dimension_semantics=("arbitrary" if acc_outs else "parallel",)),
    )(*ins)


def _rms_fwd(x, gain, after=None):
    def body(x_ref, g_ref, *rest):
        v = x_ref[...]
        rest[-1][...] = (v * lax.rsqrt(jnp.mean(v * v, axis=-1, keepdims=True) + NORM_EPS) * g_ref[...]).astype(BF16)

    ins = [x, gain.reshape(1, -1)] + ([] if after is None else [after])
    (h,) = _rows_call(body, "rms_fwd", ins, [jax.ShapeDtypeStruct(x.shape, BF16)])
    return h


def _rms_bwd(x, gain, dh, dres, after):
    def body(x_ref, g_ref, dh_ref, dr_ref, *rest):
        dx_ref, dg_ref = rest[-2:]

        @pl.when(pl.program_id(0) == 0)
        def _():
            dg_ref[...] = jnp.zeros_like(dg_ref)

        v = x_ref[...]
        r = lax.rsqrt(jnp.mean(v * v, axis=-1, keepdims=True) + NORM_EPS)
        vhat = v * r
        d = dh_ref[...].astype(F32)
        dg_ref[...] += jnp.sum(d * vhat, axis=0, keepdims=True)
        dvh = d * g_ref[...]
        dx_ref[...] = dr_ref[...] + r * (dvh - vhat * jnp.mean(dvh * vhat, axis=-1, keepdims=True))

    ins = [x, gain.reshape(1, -1), dh, dres] + ([] if after is None else [after])
    dx, dg = _rows_call(body, "rms_bwd", ins, [jax.ShapeDtypeStruct(x.shape, F32)],
                        [jax.ShapeDtypeStruct((1, x.shape[1]), F32)])
    return dx, dg.reshape(gain.shape)


def _swiglu_fwd(gu):
    t, f2 = gu.shape
    f = f2 // 2

    def body(gu_ref, o_ref):
        o_ref[...] = (_silu(gu_ref[:, :f].astype(F32)) * gu_ref[:, f:].astype(F32)).astype(BF16)

    (act,) = _rows_call(body, "swiglu_fwd", [gu], [jax.ShapeDtypeStruct((t, f), BF16)])
    return act


def _swiglu_bwd(gu, dact):
    t, f2 = gu.shape
    f = f2 // 2

    def body(gu_ref, d_ref, o_ref):
        g, u, d = gu_ref[:, :f].astype(F32), gu_ref[:, f:].astype(F32), d_ref[...].astype(F32)
        o_ref[:, :f] = (d * u * _silu_grad(g)).astype(BF16)
        o_ref[:, f:] = (d * _silu(g)).astype(BF16)

    (dgu,) = _rows_call(body, "swiglu_bwd", [gu, dact], [jax.ShapeDtypeStruct((t, f2), BF16)])
    return dgu


def _ple_fwd(x, gl, ple):
    def body(x_ref, g_ref, p_ref, o_ref):
        o_ref[...] = x_ref[...] + jax.nn.sigmoid(g_ref[...].astype(F32)) * p_ref[...].astype(F32)

    (out,) = _rows_call(body, "ple_fwd", [x, gl, ple], [jax.ShapeDtypeStruct(x.shape, F32)])
    return out


def _ple_bwd(gl, ple, dout):
    def body(g_ref, p_ref, d_ref, dg_ref, dp_ref):
        s, d = jax.nn.sigmoid(g_ref[...].astype(F32)), d_ref[...]
        dg_ref[...] = (d * p_ref[...].astype(F32) * s * (1.0 - s)).astype(BF16)
        dp_ref[...] = (d * s).astype(BF16)

    shp = jax.ShapeDtypeStruct(gl.shape, BF16)
    return _rows_call(body, "ple_bwd", [gl, ple, dout], [shp, shp])


def _loss_fwd(y, target):
    inv = 1.0 / y.shape[1]

    def body(y_ref, t_ref, d_ref, l_ref):
        @pl.when(pl.program_id(0) == 0)
        def _():
            l_ref[...] = jnp.zeros_like(l_ref)

        e = y_ref[...] - t_ref[...]
        d_ref[...] = e * inv
        part = jnp.sum(jnp.sum(e * e, axis=1, keepdims=True), axis=0, keepdims=True) * (0.5 * inv)
        l_ref[...] += jnp.broadcast_to(part, l_ref.shape)

    dy, acc = _rows_call(body, "loss_fwd", [y, target], [jax.ShapeDtypeStruct(y.shape, F32)],
                         [jax.ShapeDtypeStruct((SUBLANES, LANES), F32)])
    return acc[0, 0], dy


def local_step(small, fetch, emit, x, p, target):
    depth = small['norm_mix'].shape[0]
    ssm_small = ('ssm_conv_w', 'ssm_conv_b', 'ssm_dt_bias', 'ssm_a_log', 'ssm_d_skip', 'ssm_norm_w')
    saved = []
    for i in range(depth):
        j = i // 2
        s = {'x': x}
        wm, token = fetch(('mix', i), x)
        h = s['h'] = _rms_fwd(x, small['norm_mix'][i], token)
        if i % 2 == 0:
            n_main = wm['ssm_w_in'].shape[0] - small['ssm_dt_bias'].shape[1]
            zx = _mm(h, wm['ssm_w_in'][:n_main], tb=True, out_dtype=BF16, name="ssm_in_fwd")
            more, token = fetch(('out', i), zx)
            wm = {**wm, **more}
            dt_raw = _mm(h, wm['ssm_w_in'][n_main:], tb=True, after=token, name="ssm_dt_fwd")
            y, s['mix'] = _ssm_core_fwd(zx, dt_raw, wm['ssm_conv_w'], *[small[n][j] for n in ssm_small[1:]])
            x = _mm(y, wm['ssm_w_out'], add=x, name="ssm_out_fwd")
        else:
            qkv = _mm(h, wm['att_w_qkv'], tb=True, out_dtype=BF16, name="att_qkv_fwd")
            y, s['mix'] = _attention_core_fwd(qkv, small['att_q_norm'][j], small['att_k_norm'][j])
            x = _mm(y, wm['att_w_o'], add=x, name="att_o_fwd")
        s['wm'], s['y'], s['x1'] = wm, y, x
        wf, token = fetch(('ffn', i), x)
        s['wf'] = wf
        h2 = s['h2'] = _rms_fwd(x, small['norm_ffn'][i], token)
        gu = s['gu'] = _mm(h2, wf['ffn_w_gu'], tb=True, out_dtype=BF16, name="ffn_gu_fwd")
        act = s['act'] = _swiglu_fwd(gu)
        x = s['x2'] = _mm(act, wf['ffn_w_down'], add=x, name="ffn_down_fwd")
        gl = s['gl'] = _mm(x, wf['ple_w_gate'], out_dtype=BF16, name="ple_gate_fwd")
        ple = s['ple'] = _mm(p[i], wf['ple_w_proj'], tb=True, out_dtype=BF16, name="ple_proj_fwd")
        x = _ple_fwd(x, gl, ple)
        saved.append(s)
    loss, dx = _loss_fwd(x, target)

    g = {n: [None] * small[n].shape[0] for n in small}
    for i in reversed(range(depth)):
        j = i // 2
        s = saved[i]
        wm, wf = s['wm'], s['wf']
        gf = {}
        dgl, dple = _ple_bwd(s['gl'], s['ple'], dx)
        gf['ple_w_proj'] = _mm(dple, p[i], ta=True, out_dtype=BF16, name="ple_proj_dw")
        gf['ple_w_gate'] = _mm(s['x2'], dgl, ta=True, out_dtype=BF16, name="ple_gate_dw")
        dx = _mm(dgl, wf['ple_w_gate'], tb=True, add=dx, name="ple_gate_da")
        dact = _mm(dx, wf['ffn_w_down'], tb=True, out_dtype=BF16, name="ffn_down_da")
        gf['ffn_w_down'] = _mm(s['act'], dx, ta=True, out_dtype=BF16, name="ffn_down_dw")
        dgu = _swiglu_bwd(s['gu'], dact)
        dh2 = _mm(dgu, wf['ffn_w_gu'], out_dtype=BF16, name="ffn_gu_da")
        gf['ffn_w_gu'] = _mm(dgu, s['h2'], ta=True, out_dtype=BF16, name="ffn_gu_dw")
        dx, g['norm_ffn'][i] = _rms_bwd(s['x1'], small['norm_ffn'][i], dh2, dx, emit(('ffn', i), gf))
        gm = {}
        if i % 2 == 0:
            n_main = wm['ssm_w_in'].shape[0] - small['ssm_dt_bias'].shape[1]
            dyn = _mm(dx, wm['ssm_w_out'], tb=True, out_dtype=BF16, name="ssm_out_da")
            gm['ssm_w_out'] = _mm(s['y'], dx, ta=True, out_dtype=BF16, name="ssm_out_dw")
            dzx, d_dt, *sg = _ssm_core_bwd(s['mix'], dyn)
            for n, v in zip(ssm_small, sg):
                g[n][j] = v
            dh = _mm(d_dt, wm['ssm_w_in'][n_main:], name="ssm_dt_da")
            dh = _mm(dzx, wm['ssm_w_in'][:n_main], add=dh, out_dtype=BF16, name="ssm_in_da")
            gm['ssm_w_in'] = jnp.concatenate([_mm(dzx, s['h'], ta=True, out_dtype=BF16, name="ssm_in_dw"),
                                              _mm(d_dt, s['h'], ta=True, out_dtype=BF16, name="ssm_dt_dw")], axis=0)
        else:
            do = _mm(dx, wm['att_w_o'], tb=True, name="att_o_da")
            gm['att_w_o'] = _mm(s['y'], dx, ta=True, out_dtype=BF16, name="att_o_dw")
            dqkv, g['att_q_norm'][j], g['att_k_norm'][j] = _attention_core_bwd(s['mix'], do)
            dh = _mm(dqkv, wm['att_w_qkv'], out_dtype=BF16, name="att_qkv_da")
            gm['att_w_qkv'] = _mm(dqkv, s['h'], ta=True, out_dtype=BF16, name="att_qkv_dw")
        dx, g['norm_mix'][i] = _rms_bwd(s['x'], small['norm_mix'][i], dh, dx, emit(('mix', i), gm))
    return loss, dx, {n: jnp.stack(v) for n, v in g.items()}


CHANNEL_WEIGHTS = ('ffn_w_gate', 'ffn_w_up', 'ffn_w_down', 'ple_w_proj', 'ple_w_gate')


def _stages(depth):
    gather, scatter = {}, {}
    for i in range(depth):
        j = i // 2
        if i % 2 == 0:
            gather['mix', i] = [('ssm_w_in', j), ('ssm_conv_w', j)]
            gather['out', i] = [('ssm_w_out', j)]
            scatter['mix', i] = [('ssm_w_in', j), ('ssm_w_out', j)]
        else:
            gather['mix', i] = scatter['mix', i] = [('att_w_qkv', j), ('att_w_o', j)]
        gather['ffn', i] = scatter['ffn', i] = [(n, i) for n in CHANNEL_WEIGHTS]
    return gather, scatter


def _pack_plan(shapes, width, members):
    plan, off = [], 0
    for name, lyr in members:
        _, r, c = shapes[name]
        if name in COL_SHARDED:
            r, c = c, r
        if name == 'ssm_conv_w':
            pr = -(-2 * r * c // width)
        else:
            assert (r * c) % width == 0, (name, r, c)
            pr = r * c // width
        plan.append((name, lyr, r, c, pr, off))
        off += _round_up(pr, BF16_ROWS)
    return plan, off


def _small_plan(shapes):
    plan, off = [], 0
    for name in SMALL:
        n = math.prod(shapes[name])
        plan.append((name, n, off))
        off += n
    return plan, _round_up(off, SUBLANES * LANES)


def kernel(x, p, norm_mix, norm_ffn, ssm_w_in, ssm_conv_w, ssm_conv_b, ssm_dt_bias, ssm_a_log, ssm_d_skip, ssm_norm_w, ssm_w_out, att_w_qkv, att_q_norm, att_k_norm, att_w_o, ffn_w_gate, ffn_w_up, ffn_w_down, ple_w_proj, ple_w_gate, loss_target, m_norm_mix, m_norm_ffn, m_ssm_w_in, m_ssm_conv_w, m_ssm_conv_b, m_ssm_dt_bias, m_ssm_a_log, m_ssm_d_skip, m_ssm_norm_w, m_ssm_w_out, m_att_w_qkv, m_att_q_norm, m_att_k_norm, m_att_w_o, m_ffn_w_gate, m_ffn_w_up, m_ffn_w_down, m_ple_w_proj, m_ple_w_gate, v_norm_mix, v_norm_ffn, v_ssm_w_in, v_ssm_conv_w, v_ssm_conv_b, v_ssm_dt_bias, v_ssm_a_log, v_ssm_d_skip, v_ssm_norm_w, v_ssm_w_out, v_att_w_qkv, v_att_q_norm, v_att_k_norm, v_att_w_o, v_ffn_w_gate, v_ffn_w_up, v_ffn_w_down, v_ple_w_proj, v_ple_w_gate):
    given = dict(locals())
    w_in = {n: given[n] for n in WEIGHTS}
    m_in = {n: given["m_" + n] for n in WEIGHTS}
    v_in = {n: given["v_" + n] for n in WEIGHTS}
    width = x.shape[-1]
    depth = norm_mix.shape[0]

    gather_members, scatter_members = _stages(depth)
    shapes = {n: w_in[n].shape for n in BIG + ('ssm_conv_w',)}
    plans = {key: _pack_plan(shapes, width, members)[0] for key, members in gather_members.items()}
    splans = {key: _pack_plan(shapes, width, members)[0] for key, members in scatter_members.items()}
    order = list(gather_members)

    def pack_weights(stage):
        pieces = []
        for name, layer, r, c, pr, off in plans[stage]:
            blk = w_in[name][layer]
            if name == 'ssm_conv_w':
                blk = lax.bitcast_convert_type(blk.reshape(-1), BF16).reshape(-1)
                blk = jnp.pad(blk, (0, pr * width - blk.shape[0]))
            elif name in COL_SHARDED:
                blk = blk.T
            blk = blk.astype(BF16).reshape(pr, width)
            pieces.append(jnp.pad(blk, ((0, _round_up(pr, BF16_ROWS) - pr), (0, 0))))
        return jnp.concatenate(pieces, axis=0)

    packed = [pack_weights(order[0])]
    pending = [exchange_start(packed[0], True, "gather_start_0")]
    packed += [pack_weights(stage) for stage in order[1:]]

    def fetch(stage, after):
        k = order.index(stage)
        handle, token = pending[k]
        land = exchange_wait(handle, [token] + packed[1:] if k == 0 else after, True, f"gather_wait_{k}")
        token = None
        if k + 1 < len(order):
            pending.append(exchange_start(packed[k + 1], True, f"gather_start_{k + 1}", land))
            token = pending[-1][1]
        got = {}
        for name, layer, r, c, pr, off in plans[stage]:
            piece = land[:, off:off + pr]
            if name == 'ssm_conv_w':
                taps, chans = w_in[name].shape[1:]
                bits = piece.reshape(N_DEV, -1)[:, :2 * taps * chans].reshape(N_DEV, taps * chans, 2)
                piece = lax.bitcast_convert_type(bits, F32).reshape(N_DEV, taps, chans)
                got[name] = piece.transpose(1, 0, 2).reshape(taps, N_DEV * chans)
            else:
                got[name] = piece.reshape(N_DEV * r, c)
        if 'ffn_w_gate' in got:
            got['ffn_w_gu'] = jnp.concatenate([got.pop('ffn_w_gate'), got.pop('ffn_w_up')], axis=0)
        return got, token

    scatters = {}

    def emit(stage, grads):
        grads = dict(grads)
        if 'ffn_w_gu' in grads:
            hidden = grads['ffn_w_gu'].shape[0] // 2
            grads['ffn_w_gate'], grads['ffn_w_up'] = grads['ffn_w_gu'][:hidden], grads['ffn_w_gu'][hidden:]
        pieces = []
        for name, layer, r, c, pr, off in splans[stage]:
            g = grads[name].reshape(N_DEV, pr, width)
            pieces.append(jnp.pad(g, ((0, 0), (0, _round_up(pr, BF16_ROWS) - pr), (0, 0))))
        scatters[stage], token = exchange_start(jnp.concatenate(pieces, axis=1), False,
                                                f"scatter_start_{len(scatters)}")
        return token

    small = {n: w_in[n] for n in SMALL}
    cs = w_in['ssm_conv_w'].shape[2]
    loss_local, gx, gw = local_step(small, fetch, emit, x[0], p[:, 0], loss_target[0])
    loss = lax.psum(loss_local, ("x", "y", "c"))

    parts = {}
    for k, (stage, handle) in enumerate(scatters.items()):
        received = exchange_wait(handle, gx, False, f"scatter_wait_{k}")
        gsum = sum_slots(received, f"sum_grads_{k}")
        for name, layer, r, c, pr, off in splans[stage]:
            g = gsum[off:off + pr].reshape(r, c)
            parts[name, layer] = g.T if name in COL_SHARDED else g
    grads = {n: jnp.stack([parts[n, layer] for layer in range(w_in[n].shape[0])]) for n in BIG}

    splan, stotal = _small_plan({n: gw[n].shape for n in SMALL})
    svec = jnp.concatenate([gw[n].reshape(-1) for n, _, _ in splan])
    svec = jnp.pad(svec, (0, stotal - svec.shape[0])).reshape(stotal // LANES, LANES)
    _, ssum = all_gather_sum_small(svec, "sum_small_grads")
    ssum = ssum.reshape(-1)
    for name, n, off in splan:
        grads[name] = ssum[off:off + n].reshape(gw[name].shape)
    me = _me()
    grads['ssm_conv_w'] = lax.dynamic_slice_in_dim(grads['ssm_conv_w'], me * cs, cs, axis=2)

    delta, new_m, new_v = {}, {}, {}
    for name in BIG:
        shp = w_in[name].shape
        flat = lambda a: a.reshape(-1, shp[-1])
        d, nm, nv = adamw(flat(w_in[name]), flat(grads[name]), flat(m_in[name]), flat(v_in[name]), "adamw_" + name)
        delta[name], new_m[name], new_v[name] = d.reshape(shp), nm.reshape(shp), nv.reshape(shp)
    splan2, stotal2 = _small_plan({n: w_in[n].shape for n in SMALL})

    def pack_small(src):
        vec = jnp.concatenate([src[n].reshape(-1) for n, _, _ in splan2])
        return jnp.pad(vec, (0, stotal2 - vec.shape[0]), constant_values=1.0).reshape(stotal2 // LANES, LANES)

    sd, snm, snv = adamw(pack_small(w_in), pack_small(grads), pack_small(m_in), pack_small(v_in), "adamw_small")
    for name, n, off in splan2:
        shp = w_in[name].shape
        delta[name] = sd.reshape(-1)[off:off + n].reshape(shp)
        new_m[name] = snm.reshape(-1)[off:off + n].reshape(shp)
        new_v[name] = snv.reshape(-1)[off:off + n].reshape(shp)

    return (loss, gx[None], *[grads[n] for n in WEIGHTS], *[delta[n] for n in WEIGHTS],
            *[new_m[n] for n in WEIGHTS], *[new_v[n] for n in WEIGHTS])
```

```python
import functools
import math

import jax
import jax.numpy as jnp
from jax import lax
from jax.experimental import pallas as pl
from jax.experimental.pallas import tpu as pltpu

F32 = jnp.float32
BF16 = jnp.bfloat16
N_DEV = 8
MESH = pl.DeviceIdType.MESH

SSM_HEAD_DIM = 64
SSM_GROUPS = 4
SSM_STATE = 128
CONV_WIDTH = 4
SSD_CHUNK = 128
ATT_HEAD_DIM = 64
DIL_PATTERNS = ((128, 1), (512, 4), (2048, 16))
NORM_EPS = 1e-6
ADAM_LR = 0.001
ADAM_B1 = 0.9
ADAM_B2 = 0.999
ADAM_EPS = 1e-08
ADAM_WD = 0.01
ADAM_STEP = 10

BF16_ROWS = 16
LANES = 128
SUBLANES = 8

WEIGHTS = ['norm_mix', 'norm_ffn', 'ssm_w_in', 'ssm_conv_w', 'ssm_conv_b', 'ssm_dt_bias', 'ssm_a_log', 'ssm_d_skip',
           'ssm_norm_w', 'ssm_w_out', 'att_w_qkv', 'att_q_norm', 'att_k_norm', 'att_w_o', 'ffn_w_gate', 'ffn_w_up',
           'ffn_w_down', 'ple_w_proj', 'ple_w_gate']
COL_SHARDED = ('ssm_w_in', 'att_w_qkv', 'ffn_w_gate', 'ffn_w_up', 'ple_w_proj')
ROW_SHARDED = ('ssm_w_out', 'att_w_o', 'ffn_w_down', 'ple_w_gate')
BIG = COL_SHARDED + ROW_SHARDED
SMALL = ('norm_mix', 'norm_ffn', 'ssm_conv_w', 'ssm_conv_b', 'ssm_dt_bias', 'ssm_a_log', 'ssm_d_skip', 'ssm_norm_w',
         'att_q_norm', 'att_k_norm')


def _pick(n, cands):
    for c in cands:
        if n % c == 0:
            return c
    return n


def _round_up(n, m):
    return -(-n // m) * m


MM_TILES = (1024, 1408, 512, 256, 128)
MM_VMEM_BYTES = 48 * 1024 * 1024


def _mm(a, b, *, ta=False, tb=False, out_dtype=F32, add=None, after=None, name):
    k_dim, m_dim = (a.shape if ta else a.shape[::-1])
    n_dim = b.shape[0] if tb else b.shape[1]
    assert (b.shape[1] if tb else b.shape[0]) == k_dim, (a.shape, b.shape, ta, tb)
    tm = _pick(m_dim, MM_TILES)
    tn = _pick(n_dim, MM_TILES)
    tk = _pick(k_dim, MM_TILES)
    nk = k_dim // tk
    a_spec = pl.BlockSpec((tk, tm), lambda i, j, k: (k, i)) if ta else pl.BlockSpec((tm, tk), lambda i, j, k: (i, k))
    b_spec = pl.BlockSpec((tn, tk), lambda i, j, k: (j, k)) if tb else pl.BlockSpec((tk, tn), lambda i, j, k: (k, j))
    o_spec = pl.BlockSpec((tm, tn), lambda i, j, k: (i, j))
    dims = (((0 if ta else 1,), (1 if tb else 0,)), ((), ()))
    has_add = add is not None
    n_in = 2 + has_add + (after is not None)

    def body(*refs):
        a_ref, b_ref = refs[:2]
        o_ref = refs[n_in]

        def dot():
            return lax.dot_general(a_ref[...].astype(BF16), b_ref[...].astype(BF16), dims,
                                   preferred_element_type=F32)

        def finish(acc):
            if has_add:
                acc = acc + refs[2][...].astype(F32)
            o_ref[...] = acc.astype(o_ref.dtype)

        if nk == 1:
            finish(dot())
            return
        acc_ref = refs[n_in + 1]
        k = pl.program_id(2)

        @pl.when(k == 0)
        def _():
            acc_ref[...] = dot()

        @pl.when((k > 0) & (k < nk - 1))
        def _():
            acc_ref[...] += dot()

        @pl.when(k == nk - 1)
        def _():
            finish(acc_ref[...] + dot())

    return pl.pallas_call(
        body, name=f"{name}_{m_dim}x{n_dim}x{k_dim}",
        out_shape=jax.ShapeDtypeStruct((m_dim, n_dim), out_dtype),
        grid=(m_dim // tm, n_dim // tn, nk),
        in_specs=[a_spec, b_spec] + ([o_spec] if has_add else []) + (
            [] if after is None else [pl.BlockSpec(memory_space=pl.ANY)]),
        out_specs=o_spec,
        scratch_shapes=[] if nk == 1 else [pltpu.VMEM((tm, tn), F32)],
        compiler_params=pltpu.CompilerParams(dimension_semantics=("parallel", "parallel", "arbitrary"),
                                             vmem_limit_bytes=MM_VMEM_BYTES),
    )(*((a, b) + ((add,) if has_add else ()) + (() if after is None else (after,))))


def _me():
    return 4 * lax.axis_index("x") + 2 * lax.axis_index("y") + lax.axis_index("c")


def _peer(j):
    x, y, c = lax.axis_index("x"), lax.axis_index("y"), lax.axis_index("c")
    px = 1 - x if j & 4 else x
    py = 1 - y if j & 2 else y
    pc = 1 - c if j & 1 else c
    return (px, py, pc), 4 * px + 2 * py + pc


def _exchange_body(src_of, dst_ref, send_sems, recv_sems, local_sem):
    me = _me()
    mine = pltpu.make_async_copy(src_of(me), dst_ref.at[me], local_sem)
    mine.start()
    sends = []
    for j in range(1, N_DEV):
        peer, pidx = _peer(j)
        cp = pltpu.make_async_remote_copy(src_ref=src_of(pidx), dst_ref=dst_ref.at[me], send_sem=send_sems.at[j - 1],
                                          recv_sem=recv_sems.at[j - 1], device_id=peer, device_id_type=MESH)
        cp.start()
        sends.append(cp)
    for j in range(1, N_DEV):
        peer, pidx = _peer(j)
        pltpu.make_async_remote_copy(src_ref=src_of(pidx), dst_ref=dst_ref.at[pidx], send_sem=send_sems.at[j - 1],
                                     recv_sem=recv_sems.at[j - 1], device_id=peer, device_id_type=MESH).wait_recv()
    for cp in sends:
        cp.wait_send()
    mine.wait()


_EXCHANGE_SCRATCH = [pltpu.SemaphoreType.DMA((N_DEV - 1,)), pltpu.SemaphoreType.DMA((N_DEV - 1,)),
                     pltpu.SemaphoreType.DMA]


_HBM = pl.BlockSpec(memory_space=pltpu.HBM)
_SEM = pl.BlockSpec(memory_space=pltpu.SEMAPHORE)


def _split_copies(src_ref, gather, land_ref, send_sems, recv_sems):
    me = _me()
    pairs = []
    for j in range(1, N_DEV):
        peer, pidx = _peer(j)

        def make(slot, peer=peer, pidx=pidx, j=j):
            return pltpu.make_async_remote_copy(
                src_ref=src_ref if gather else src_ref.at[pidx], dst_ref=land_ref.at[slot],
                send_sem=send_sems.at[j - 1], recv_sem=recv_sems.at[j - 1], device_id=peer, device_id_type=MESH)

        pairs.append((make(me), make(pidx)))
    return pairs


def exchange_start(src, gather, name, after=None):
    land_shape = ((N_DEV,) + src.shape) if gather else src.shape
    has_after = after is not None

    def body(*refs):
        src_ref, land_ref = refs[:2]
        send_sems, recv_sems = refs[2 + has_after:4 + has_after]
        for send, _ in _split_copies(src_ref, gather, land_ref, send_sems, recv_sems):
            send.start()
        refs[-1][...] = jnp.zeros_like(refs[-1])

    sem = pltpu.SemaphoreType.DMA((N_DEV - 1,))
    send_sems, recv_sems, src_thru, land, token = pl.pallas_call(
        body, name=name,
        out_shape=(sem, sem, pltpu.HBM(src.shape, src.dtype), pltpu.HBM(land_shape, src.dtype),
                   jax.ShapeDtypeStruct((SUBLANES, LANES), F32)),
        in_specs=(_HBM, _HBM) + ((pl.BlockSpec(memory_space=pl.ANY),) if has_after else ()),
        out_specs=(_SEM, _SEM, _HBM, _HBM, pl.BlockSpec(memory_space=pltpu.VMEM)),
        input_output_aliases={0: 2, 1: 3},
        compiler_params=pltpu.CompilerParams(has_side_effects=pltpu.SideEffectType.DATAFLOW_SIDE_EFFECTING),
    )(pltpu.with_memory_space_constraint(src, pltpu.HBM),
      pltpu.with_memory_space_constraint(lax.empty(land_shape, src.dtype), pltpu.HBM),
      *((after,) if has_after else ()))
    return (send_sems, recv_sems, src_thru, land), token


def exchange_wait(handle, after, gather, name):
    send_sems, recv_sems, src_thru, land = handle
    after = tuple(after) if isinstance(after, (tuple, list)) else (after,)

    def body(src_ref, land_ref, send_sems, recv_sems, *rest):
        for _, arrival in _split_copies(src_ref, gather, land_ref, send_sems, recv_sems):
            arrival.wait_send()
            arrival.wait_recv()

    src_done, got = pl.pallas_call(
        body, name=name,
        out_shape=(pltpu.HBM(src_thru.shape, src_thru.dtype), pltpu.HBM(land.shape, land.dtype)),
        in_specs=(_HBM, _HBM, _SEM, _SEM) + (pl.BlockSpec(memory_space=pl.ANY),) * len(after), out_specs=(_HBM, _HBM),
        input_output_aliases={0: 0, 1: 1},
        compiler_params=pltpu.CompilerParams(has_side_effects=pltpu.SideEffectType.DATAFLOW_SIDE_EFFECTING),
    )(src_thru, land, send_sems, recv_sems, *after)
    mine = src_done if gather else lax.dynamic_index_in_dim(src_done, _me(), 0, keepdims=False)
    return lax.dynamic_update_index_in_dim(got, mine, _me(), 0)


def all_gather_sum_small(v, name):
    def body(x_ref, out_ref, sum_ref, send_sems, recv_sems, local_sem):
        _exchange_body(lambda k: x_ref, out_ref, send_sems, recv_sems, local_sem)
        acc = out_ref[0]
        for k in range(1, N_DEV):
            acc = acc + out_ref[k]
        sum_ref[...] = acc

    return pl.pallas_call(
        body, name=name,
        out_shape=(jax.ShapeDtypeStruct((N_DEV,) + v.shape, v.dtype), jax.ShapeDtypeStruct(v.shape, v.dtype)),
        in_specs=[pl.BlockSpec(memory_space=pltpu.VMEM)],
        out_specs=(pl.BlockSpec(memory_space=pltpu.VMEM), pl.BlockSpec(memory_space=pltpu.VMEM)),
        scratch_shapes=list(_EXCHANGE_SCRATCH),
    )(v)


def sum_slots(slots, name):
    _, p_dim, c_dim = slots.shape
    tp = next(tp for tp in range(512, 0, -BF16_ROWS) if p_dim % tp == 0)

    def body(x_ref, o_ref):
        acc = x_ref[0].astype(F32)
        for k in range(1, N_DEV):
            acc = acc + x_ref[k].astype(F32)
        o_ref[...] = acc

    return pl.pallas_call(
        body, name=name,
        out_shape=jax.ShapeDtypeStruct((p_dim, c_dim), F32),
        grid=(p_dim // tp,),
        in_specs=[pl.BlockSpec((N_DEV, tp, c_dim), lambda i: (0, i, 0))],
        out_specs=pl.BlockSpec((tp, c_dim), lambda i: (i, 0)),
        compiler_params=pltpu.CompilerParams(dimension_semantics=("parallel",)),
    )(slots)


def adamw(w, g, m, v, name):
    rows, cols = w.shape
    tr = _pick(rows, (256, 128, 64, 32, 16, 8))

    def body(w_ref, g_ref, m_ref, v_ref, d_ref, nm_ref, nv_ref):
        gv = g_ref[...]
        nm = ADAM_B1 * m_ref[...] + (1.0 - ADAM_B1) * gv
        nv = ADAM_B2 * v_ref[...] + (1.0 - ADAM_B2) * (gv * gv)
        m_hat = nm / (1.0 - ADAM_B1 ** ADAM_STEP)
        v_hat = nv / (1.0 - ADAM_B2 ** ADAM_STEP)
        d_ref[...] = -ADAM_LR * (m_hat / (jnp.sqrt(v_hat) + ADAM_EPS) + ADAM_WD * w_ref[...])
        nm_ref[...] = nm
        nv_ref[...] = nv

    spec = pl.BlockSpec((tr, cols), lambda i: (i, 0))
    shp = jax.ShapeDtypeStruct((rows, cols), F32)
    return pl.pallas_call(
        body, name=name, out_shape=(shp, shp, shp), grid=(rows // tr,),
        in_specs=[spec] * 4, out_specs=(spec,) * 3,
        compiler_params=pltpu.CompilerParams(dimension_semantics=("parallel",)),
    )(w, g, m, v)


ATT_BLK = 128
NEG = -1e30


def _head_sums(v):
    li = lax.broadcasted_iota(jnp.int32, (LANES, LANES), 0) // ATT_HEAD_DIM
    lj = lax.broadcasted_iota(jnp.int32, (LANES, LANES), 1) // ATT_HEAD_DIM
    ones = (li == lj).astype(BF16)
    hi = v.astype(BF16)
    lo = (v - hi.astype(F32)).astype(BF16)
    return jnp.dot(hi, ones, preferred_element_type=F32) + jnp.dot(lo, ones, preferred_element_type=F32)


def _head_col(v, hmask):
    return jnp.max(jnp.where(hmask, v, -jnp.inf), axis=-1, keepdims=True)


def _qk_norm(raw, gain2):
    rstd = lax.rsqrt(_head_sums(raw * raw) * (1.0 / ATT_HEAD_DIM) + NORM_EPS)
    xhat = raw * rstd
    return xhat * gain2, xhat, rstd


def _qk_norm_bwd(dn, xhat, rstd, gain2):
    dxh = dn * gain2
    return rstd * (dxh - xhat * (_head_sums(dxh * xhat) * (1.0 / ATT_HEAD_DIM))), dn * xhat


def _att_mask_bias(n, dilation):
    qi = lax.broadcasted_iota(jnp.int32, (ATT_BLK, 2 * ATT_BLK), 0)
    ki = lax.broadcasted_iota(jnp.int32, (ATT_BLK, 2 * ATT_BLK), 1)
    dist = qi + ATT_BLK - ki
    valid = (dist >= 0) & (dist <= ATT_BLK) & ((n > 0) | (ki >= ATT_BLK))
    return valid, (dilation * dist).astype(F32)


ATT_PAIRS = 8
RELAYOUT_ROWS = 512
RELAYOUT_COLS = 512


def _to_residues(x, dilation, col0=0, cols=None):
    t = x.shape[0]
    cols = x.shape[1] if cols is None else cols
    if dilation == 1 and col0 == 0 and cols == x.shape[1]:
        return x.reshape(1, t, cols)
    tr = _pick(t, (RELAYOUT_ROWS,))
    tc = _pick(cols, (RELAYOUT_COLS, 256, 128))
    per = tr // dilation
    assert tr % dilation == 0 and col0 % tc == 0

    def body(x_ref, o_ref, s_ref):
        for c in range(tc // LANES):
            lanes = slice(c * LANES, (c + 1) * LANES)
            s_ref[c] = x_ref[:, lanes].astype(F32)
            for r in range(dilation):
                o_ref[r, :, lanes] = s_ref[c, pl.ds(r, per, stride=dilation), :].astype(o_ref.dtype)

    return pl.pallas_call(
        body, name=f"to_residues_{dilation}", out_shape=jax.ShapeDtypeStruct((dilation, t // dilation, cols), x.dtype),
        grid=(t // tr, cols // tc),
        in_specs=[pl.BlockSpec((tr, tc), lambda i, j: (i, col0 // tc + j))],
        out_specs=pl.BlockSpec((dilation, per, tc), lambda i, j: (0, i, j)),
        scratch_shapes=[pltpu.VMEM((tc // LANES, tr, LANES), F32)],
        compiler_params=pltpu.CompilerParams(dimension_semantics=("parallel", "parallel")),
    )(x)


def _from_residues(y):
    dilation, lu, cols = y.shape
    t = dilation * lu
    if dilation == 1:
        return y.reshape(t, cols)
    tr = _pick(t, (RELAYOUT_ROWS,))
    tc = _pick(cols, (RELAYOUT_COLS, 256, 128))
    per = tr // dilation

    def body(y_ref, o_ref, s_ref):
        for c in range(tc // LANES):
            lanes = slice(c * LANES, (c + 1) * LANES)
            for r in range(dilation):
                s_ref[c, pl.ds(r, per, stride=dilation), :] = y_ref[r, :, lanes].astype(F32)
            o_ref[:, lanes] = s_ref[c].astype(o_ref.dtype)

    return pl.pallas_call(
        body, name=f"from_residues_{dilation}", out_shape=jax.ShapeDtypeStruct((t, cols), y.dtype),
        grid=(t // tr, cols // tc),
        in_specs=[pl.BlockSpec((dilation, per, tc), lambda i, j: (0, i, j))],
        out_specs=pl.BlockSpec((tr, tc), lambda i, j: (i, j)),
        scratch_shapes=[pltpu.VMEM((tc // LANES, tr, LANES), F32)],
        compiler_params=pltpu.CompilerParams(dimension_semantics=("parallel", "parallel")),
    )(y)


def _att_specs(hd, v_base, nb, pp):
    width = pp * LANES
    assert hd % width == 0 and v_base % width == 0

    def spec(base, shift):
        def imap(r, hp, n):
            row = jnp.minimum(n, nb - 1) if shift == 0 else jnp.maximum(n - 1, 0)
            return (r, row, base // width + hp)
        return pl.BlockSpec((None, ATT_BLK, width), imap)

    return [spec(0, 0), spec(hd, 1), spec(hd, 0), spec(v_base, 1), spec(v_base, 0)]


DELTA_LANE = 64


def _att_pre(qkv, g, dilation, gq2, gk2):
    t, width = qkv.shape
    hd = width // (3 * len(DIL_PATTERNS))
    tr = _pick(t, (RELAYOUT_ROWS,))
    tc = _pick(hd, (RELAYOUT_COLS, 256, 128))
    per = tr // dilation
    assert tr % dilation == 0 and (g * 3 * hd) % tc == 0

    def body(x_ref, gq_ref, gk_ref, o_ref, s_ref):
        gain = jnp.where(pl.program_id(1) * tc < hd, gq_ref[0:1, :], gk_ref[0:1, :])
        for c in range(tc // LANES):
            lanes = slice(c * LANES, (c + 1) * LANES)
            s_ref[c] = _qk_norm(x_ref[:, lanes].astype(F32), gain)[0]
            for r in range(dilation):
                o_ref[r, :, lanes] = s_ref[c, pl.ds(r, per, stride=dilation), :].astype(o_ref.dtype)

    vec_spec = pl.BlockSpec((SUBLANES, LANES), lambda i, j: (0, 0))
    return pl.pallas_call(
        body, name=f"att_pre_g{g}", out_shape=jax.ShapeDtypeStruct((dilation, t // dilation, 2 * hd), qkv.dtype),
        grid=(t // tr, 2 * hd // tc),
        in_specs=[pl.BlockSpec((tr, tc), lambda i, j: (i, g * 3 * hd // tc + j)), vec_spec, vec_spec],
        out_specs=pl.BlockSpec((dilation, per, tc), lambda i, j: (0, i, j)),
        scratch_shapes=[pltpu.VMEM((tc // LANES, tr, LANES), F32)],
        compiler_params=pltpu.CompilerParams(dimension_semantics=("parallel", "parallel")),
    )(qkv, gq2, gk2)


def _att_group_fwd(view, hd, slopes, g):
    qk_r, v_r, v_base = view
    dilation, lu, _ = qk_r.shape
    nb = lu // ATT_BLK
    assert nb * ATT_BLK == lu and hd % LANES == 0
    hpn = hd // LANES
    pp = math.gcd(ATT_PAIRS, hpn)
    scale = 1.0 / math.sqrt(ATT_HEAD_DIM)

    def body(q_ref, kp_ref, kc_ref, vp_ref, vc_ref, sl_ref, o_ref, l_ref):
        n = pl.program_id(2)
        lane = lax.broadcasted_iota(jnp.int32, (1, LANES), 1)
        first = (lane // ATT_HEAD_DIM) == 0
        valid, dist = _att_mask_bias(n, dilation)
        stats = jnp.zeros((ATT_BLK, LANES), F32)
        for pair in range(pp):
            cols = slice(pair * LANES, (pair + 1) * LANES)
            qn = q_ref[:, cols]
            kn16 = jnp.concatenate([kp_ref[:, cols], kc_ref[:, cols]], axis=0)
            v16 = jnp.concatenate([vp_ref[:, cols], vc_ref[:, cols]], axis=0)
            outs = []
            for hh in range(2):
                hmask = (lane // ATT_HEAD_DIM) == hh
                qh = jnp.where(hmask, qn, jnp.zeros_like(qn))
                s = lax.dot_general(qh, kn16, (((1,), (1,)), ((), ())), preferred_element_type=F32) * scale
                slope = _head_col(sl_ref[pair, 0:1, :], hmask)
                logits = jnp.where(valid, s - slope * dist, NEG)
                mx = jnp.max(logits, axis=-1, keepdims=True)
                pexp = jnp.exp(logits - mx)
                den = jnp.sum(pexp, axis=-1, keepdims=True)
                outs.append(jnp.dot(pexp.astype(BF16), v16, preferred_element_type=F32) / den)
                stats = jnp.where(lane == 2 * pair + hh, mx + jnp.log(den), stats)
            o_ref[:, cols] = jnp.where(first, outs[0], outs[1]).astype(BF16)
        l_ref[...] = stats

    out_spec = pl.BlockSpec((None, ATT_BLK, pp * LANES), lambda r, hp, n: (r, n, hp))
    stat_spec = pl.BlockSpec((None, ATT_BLK, LANES), lambda r, hp, n: (r, n, hp))
    o, lse = pl.pallas_call(
        body, name=f"att_fwd_g{g}",
        out_shape=(jax.ShapeDtypeStruct((dilation, lu, hd), BF16),
                   jax.ShapeDtypeStruct((dilation, lu, hpn // pp * LANES), F32)),
        grid=(dilation, hpn // pp, nb),
        in_specs=_att_specs(hd, v_base, nb, pp) + [pl.BlockSpec((pp, SUBLANES, LANES), lambda r, hp, n: (hp, 0, 0))],
        out_specs=(out_spec, stat_spec),
        compiler_params=pltpu.CompilerParams(dimension_semantics=("parallel", "parallel", "arbitrary")),
    )(qk_r, qk_r, qk_r, v_r, v_r, slopes)
    return _from_residues(o), _from_residues(lse)


def _att_merge(outs, lses):
    t, hd = outs[0].shape
    sw = lses[0].shape[1]
    pp = hd // sw
    tr = _pick(t, (256, 128))
    ng = len(outs)

    def body(*refs):
        o_refs, l_refs, o16_ref, lt_ref = refs[:ng], refs[ng:2 * ng], refs[2 * ng], refs[2 * ng + 1]
        lane = lax.broadcasted_iota(jnp.int32, (1, LANES), 1)
        first = (lane // ATT_HEAD_DIM) == 0
        for blk in range(sw // LANES):
            scols = slice(blk * LANES, (blk + 1) * LANES)
            stats = jnp.zeros((tr, LANES), F32)
            for pair in range(pp):
                cols = slice((blk * pp + pair) * LANES, (blk * pp + pair + 1) * LANES)
                weights = []
                for hh in range(2):
                    pick = lane == 2 * pair + hh
                    ls = [_head_col(r[:, scols], pick) for r in l_refs]
                    mx = functools.reduce(jnp.maximum, ls)
                    es = [jnp.exp(l - mx) for l in ls]
                    den = functools.reduce(jnp.add, es)
                    weights.append([e / den for e in es])
                    stats = jnp.where(pick, mx + jnp.log(den), stats)
                acc = jnp.zeros((tr, LANES), F32)
                for gi in range(ng):
                    acc = acc + jnp.where(first, weights[0][gi], weights[1][gi]) * o_refs[gi][:, cols].astype(F32)
                o16_ref[:, cols] = acc.astype(BF16)
            lt_ref[:, scols] = stats

    spec = pl.BlockSpec((tr, hd), lambda i: (i, 0))
    sspec = pl.BlockSpec((tr, sw), lambda i: (i, 0))
    return pl.pallas_call(
        body, name="att_merge",
        out_shape=(jax.ShapeDtypeStruct((t, hd), BF16), jax.ShapeDtypeStruct((t, sw), F32)), grid=(t // tr,),
        in_specs=[spec] * ng + [sspec] * ng, out_specs=(spec, sspec),
        compiler_params=pltpu.CompilerParams(dimension_semantics=("parallel",)),
    )(*outs, *lses)


def _att_bwd_prep(do, o16, lse_tot):
    t, hd = do.shape
    sw = lse_tot.shape[1]
    pp = hd // sw
    tr = _pick(t, (256, 128))

    def body(do_ref, o_ref, l_ref, d16_ref, st_ref):
        lane = lax.broadcasted_iota(jnp.int32, (1, LANES), 1)
        d16_ref[...] = do_ref[...].astype(BF16)
        for blk in range(sw // LANES):
            scols = slice(blk * LANES, (blk + 1) * LANES)
            stats = l_ref[:, scols]
            for pair in range(pp):
                cols = slice((blk * pp + pair) * LANES, (blk * pp + pair + 1) * LANES)
                prod = do_ref[:, cols] * o_ref[:, cols].astype(F32)
                for hh in range(2):
                    hmask = (lane // ATT_HEAD_DIM) == hh
                    delta = jnp.sum(jnp.where(hmask, prod, 0.0), axis=-1, keepdims=True)
                    stats = jnp.where(lane == DELTA_LANE + 2 * pair + hh, delta, stats)
            st_ref[:, scols] = stats

    spec = pl.BlockSpec((tr, hd), lambda i: (i, 0))
    sspec = pl.BlockSpec((tr, sw), lambda i: (i, 0))
    return pl.pallas_call(
        body, name="att_bwd_prep",
        out_shape=(jax.ShapeDtypeStruct((t, hd), BF16), jax.ShapeDtypeStruct((t, sw), F32)), grid=(t // tr,),
        in_specs=[spec, spec, sspec], out_specs=(spec, sspec),
        compiler_params=pltpu.CompilerParams(dimension_semantics=("parallel",)),
    )(do, o16, lse_tot)


def _att_post(buf, parts, qkv, gq2, gk2, g):
    dilation, lu, hd = parts[0].shape
    t = dilation * lu
    tr = _pick(t, (RELAYOUT_ROWS,))
    per = tr // dilation

    def body(*refs):
        raw_ref, gq_ref, gk_ref = refs[3:6]
        o_ref, dgq_ref, dgk_ref, s_ref = refs[-4:]

        @pl.when(pl.program_id(0) == 0)
        def _():
            dgq_ref[...] = jnp.zeros_like(dgq_ref)
            dgk_ref[...] = jnp.zeros_like(dgk_ref)

        for sec, y_ref in enumerate(refs[:3]):
            gsum = jnp.zeros((1, LANES), F32)
            for c in range(hd // LANES):
                lanes = slice(c * LANES, (c + 1) * LANES)
                out_lanes = slice(sec * hd + c * LANES, sec * hd + (c + 1) * LANES)
                for r in range(dilation):
                    s_ref[pl.ds(r, per, stride=dilation), :] = y_ref[r, :, lanes].astype(F32)
                d = s_ref[...]
                if sec < 2:
                    gain = (gq_ref if sec == 0 else gk_ref)[0:1, :]
                    _, xhat, rstd = _qk_norm(raw_ref[:, out_lanes].astype(F32), gain)
                    d, part = _qk_norm_bwd(d, xhat, rstd, gain)
                    gsum = gsum + jnp.sum(part, axis=0, keepdims=True)
                o_ref[:, out_lanes] = d.astype(o_ref.dtype)
            if sec < 2:
                acc = dgq_ref if sec == 0 else dgk_ref
                acc[...] += jnp.broadcast_to(gsum, acc.shape)

    part_spec = pl.BlockSpec((dilation, per, hd), lambda i: (0, i, 0))
    slab_spec = pl.BlockSpec((tr, 3 * hd), lambda i: (i, g))
    vec_spec = pl.BlockSpec((SUBLANES, LANES), lambda i: (0, 0))
    vec_shape = jax.ShapeDtypeStruct((SUBLANES, LANES), F32)
    return pl.pallas_call(
        body, name=f"att_post_g{g}", out_shape=(jax.ShapeDtypeStruct(qkv.shape, qkv.dtype), vec_shape, vec_shape),
        grid=(t // tr,),
        in_specs=[part_spec] * 3 + [slab_spec, vec_spec, vec_spec] + (
            [] if buf is None else [pl.BlockSpec(memory_space=pl.ANY)]),
        out_specs=(slab_spec, vec_spec, vec_spec),
        scratch_shapes=[pltpu.VMEM((tr, LANES), F32)],
        input_output_aliases={} if buf is None else {6: 0},
        compiler_params=pltpu.CompilerParams(dimension_semantics=("arbitrary",)),
    )(*parts, qkv, gq2, gk2, *(() if buf is None else (buf,)))


def _att_group_bwd(view, hd, slopes, stats, do16, g):
    qk_r, v_r, v_base = view
    dilation, lu, _ = qk_r.shape
    nb = lu // ATT_BLK
    hpn = hd // LANES
    pp = math.gcd(ATT_PAIRS, hpn)
    hbn = hpn // pp
    scale = 1.0 / math.sqrt(ATT_HEAD_DIM)

    def body(q_ref, kp_ref, kc_ref, vp_ref, vc_ref, sl_ref, st_ref, do_ref, dq_ref, dk_ref, dv_ref, ck_ref, cv_ref):
        n = pl.program_id(2)
        lane = lax.broadcasted_iota(jnp.int32, (1, LANES), 1)

        @pl.when(n == 0)
        def _():
            ck_ref[...] = jnp.zeros_like(ck_ref)
            cv_ref[...] = jnp.zeros_like(cv_ref)

        @pl.when(n < nb)
        def _():
            valid, dist = _att_mask_bias(n, dilation)
            stats = st_ref[...]
            for pair in range(pp):
                cols = slice(pair * LANES, (pair + 1) * LANES)
                qn = q_ref[:, cols]
                kn16 = jnp.concatenate([kp_ref[:, cols], kc_ref[:, cols]], axis=0)
                v16 = jnp.concatenate([vp_ref[:, cols], vc_ref[:, cols]], axis=0)
                dov = do_ref[:, cols]
                dq_acc = jnp.zeros((ATT_BLK, LANES), F32)
                dk_acc = jnp.zeros((2 * ATT_BLK, LANES), F32)
                dv_acc = jnp.zeros((2 * ATT_BLK, LANES), F32)
                for hh in range(2):
                    hmask = (lane // ATT_HEAD_DIM) == hh
                    qh = jnp.where(hmask, qn, jnp.zeros_like(qn))
                    doh = jnp.where(hmask, dov, jnp.zeros_like(dov))
                    s = lax.dot_general(qh, kn16, (((1,), (1,)), ((), ())), preferred_element_type=F32) * scale
                    slope = _head_col(sl_ref[pair, 0:1, :], hmask)
                    lse = _head_col(stats, lane == 2 * pair + hh)
                    delta = _head_col(stats, lane == DELTA_LANE + 2 * pair + hh)
                    pr = jnp.exp(jnp.where(valid, s - slope * dist - lse, NEG))
                    dp = lax.dot_general(doh, v16, (((1,), (1,)), ((), ())), preferred_element_type=F32)
                    ds = (pr * (dp - delta) * scale).astype(BF16)
                    dq_acc = dq_acc + jnp.where(hmask, jnp.dot(ds, kn16, preferred_element_type=F32), 0.0)
                    dk_acc = dk_acc + lax.dot_general(ds, qh, (((0,), (0,)), ((), ())), preferred_element_type=F32)
                    dv_acc = dv_acc + lax.dot_general(pr.astype(BF16), doh, (((0,), (0,)), ((), ())),
                                                      preferred_element_type=F32)
                dq_ref[:, cols] = dq_acc.astype(dq_ref.dtype)
                dk_ref[:, cols] = (ck_ref[:, cols] + dk_acc[:ATT_BLK]).astype(dk_ref.dtype)
                dv_ref[:, cols] = (cv_ref[:, cols] + dv_acc[:ATT_BLK]).astype(dv_ref.dtype)
                ck_ref[:, cols] = dk_acc[ATT_BLK:]
                cv_ref[:, cols] = dv_acc[ATT_BLK:]

        @pl.when(n == nb)
        def _():
            dk_ref[...] = ck_ref[...].astype(dk_ref.dtype)
            dv_ref[...] = cv_ref[...].astype(dv_ref.dtype)

    width = pp * LANES
    q_out = pl.BlockSpec((None, ATT_BLK, width), lambda r, hp, n: (r, jnp.minimum(n, nb - 1), hp))
    kv_out = pl.BlockSpec((None, ATT_BLK, width), lambda r, hp, n: (r, jnp.maximum(n - 1, 0), hp))
    st_spec = pl.BlockSpec((None, ATT_BLK, LANES), lambda r, hp, n: (r, jnp.minimum(n, nb - 1), hp))
    shp = jax.ShapeDtypeStruct((dilation, lu, hd), BF16)
    return pl.pallas_call(
        body, name=f"att_bwd_g{g}", out_shape=(shp, shp, shp), grid=(dilation, hbn, nb + 1),
        in_specs=_att_specs(hd, v_base, nb, pp) + [
            pl.BlockSpec((pp, SUBLANES, LANES), lambda r, hp, n: (hp, 0, 0)), st_spec, q_out],
        out_specs=(q_out, kv_out, kv_out),
        scratch_shapes=[pltpu.VMEM((ATT_BLK, width), F32), pltpu.VMEM((ATT_BLK, width), F32)],
        compiler_params=pltpu.CompilerParams(dimension_semantics=("parallel", "parallel", "arbitrary")),
    )(qk_r, qk_r, qk_r, v_r, v_r, slopes, _to_residues(stats, dilation), _to_residues(do16, dilation))


def _att_consts(q_gain, k_gain, hd):
    heads = hd // ATT_HEAD_DIM
    gq2 = jnp.broadcast_to(jnp.tile(q_gain, 2)[None], (SUBLANES, LANES))
    gk2 = jnp.broadcast_to(jnp.tile(k_gain, 2)[None], (SUBLANES, LANES))
    sl = 2.0 ** (-8.0 * jnp.arange(1, heads + 1, dtype=F32) / heads)
    slopes = jnp.broadcast_to(jnp.repeat(sl, ATT_HEAD_DIM).reshape(hd // LANES, 1, LANES), (hd // LANES, SUBLANES, LANES))
    return gq2, gk2, slopes


def _attention_core_fwd(qkv, q_gain, k_gain):
    hd = qkv.shape[1] // (3 * len(DIL_PATTERNS))
    gq2, gk2, slopes = _att_consts(q_gain, k_gain, hd)
    outs, lses, views = [], [], []
    for g, (_, dilation) in enumerate(DIL_PATTERNS):
        v_col = (3 * g + 2) * hd
        view = (_att_pre(qkv, g, dilation, gq2, gk2),) + (
            (_to_residues(qkv, 1), v_col) if dilation == 1 else (_to_residues(qkv, dilation, v_col, hd), 0))
        o_g, l_g = _att_group_fwd(view, hd, slopes, g)
        outs.append(o_g)
        lses.append(l_g)
        views.append(view)
    o16, lse_tot = _att_merge(outs, lses)
    return o16, (qkv, views, q_gain, k_gain, o16, lse_tot)


def _attention_core_bwd(res, do):
    qkv, views, q_gain, k_gain, o16, lse_tot = res
    hd = o16.shape[1]
    gq2, gk2, slopes = _att_consts(q_gain, k_gain, hd)
    do16, stats = _att_bwd_prep(do, o16, lse_tot)
    dqkv, dgq, dgk = None, 0.0, 0.0
    for g, view in enumerate(views):
        parts = _att_group_bwd(view, hd, slopes, stats, do16, g)
        dqkv, a, b = _att_post(dqkv, parts, qkv, gq2, gk2, g)
        dgq = dgq + a[0].reshape(-1, ATT_HEAD_DIM).sum(0)
        dgk = dgk + b[0].reshape(-1, ATT_HEAD_DIM).sum(0)
    return dqkv, dgq, dgk


HALO = 8


def _silu(v):
    return v * jax.nn.sigmoid(v)


def _silu_grad(v):
    s = jax.nn.sigmoid(v)
    return s * (1.0 + v * (1.0 - s))


def _halo_rows(dtype):
    return BF16_ROWS if dtype == BF16 else HALO


def _conv_fwd(zx, conv_w, conv_b, d_inner):
    t = zx.shape[0]
    conv_dim = conv_w.shape[1]
    cb = _pick(d_inner, (1024, 512, 256, 128))
    assert conv_dim % cb == 0
    tr = _pick(t, (256, 128))
    off = d_inner // cb
    hx = _halo_rows(zx.dtype)

    def body(x_ref, h_ref, w_ref, b_ref, o_ref):
        i = pl.program_id(1)
        halo = jnp.where(i > 0, h_ref[...].astype(F32), 0.0)
        ext = jnp.concatenate([halo, x_ref[...].astype(F32)], axis=0)
        acc = jnp.broadcast_to(b_ref[...], (tr, cb))
        for k in range(CONV_WIDTH):
            s = CONV_WIDTH - 1 - k
            sh = ext if s == 0 else pltpu.roll(ext, shift=s, axis=0)
            acc = acc + w_ref[k:k + 1, :] * sh[hx:hx + tr]
        o_ref[...] = acc.astype(o_ref.dtype)

    return pl.pallas_call(
        body, name="ssm_conv_fwd", out_shape=jax.ShapeDtypeStruct((t, conv_dim), BF16),
        grid=(conv_dim // cb, t // tr),
        in_specs=[pl.BlockSpec((tr, cb), lambda j, i: (i, off + j)),
                  pl.BlockSpec((hx, cb), lambda j, i: (jnp.maximum(i * (tr // hx) - 1, 0), off + j)),
                  pl.BlockSpec((CONV_WIDTH, cb), lambda j, i: (0, j)),
                  pl.BlockSpec((1, cb), lambda j, i: (0, j))],
        out_specs=pl.BlockSpec((tr, cb), lambda j, i: (i, j)),
        compiler_params=pltpu.CompilerParams(dimension_semantics=("parallel", "parallel")),
    )(zx, zx, conv_w, conv_b.reshape(1, -1))


def _conv_bwd(zx, conv_w, dpre, dzx, d_inner, col0):
    t, width = zx.shape
    conv_dim = dpre.shape[1]
    cb = _pick(conv_dim, (1024, 512, 256, 128))
    assert (d_inner + col0) % cb == 0
    tr = _pick(t, (256, 128))
    off = (d_inner + col0) // cb
    woff = col0 // cb
    nr = t // tr
    hx = _halo_rows(zx.dtype)
    hd = _halo_rows(dpre.dtype)

    def body(x_ref, h_ref, w_ref, d_ref, dn_ref, dzx_in, dx_ref, dw_ref, db_ref):
        i = pl.program_id(1)

        @pl.when(i == 0)
        def _():
            dw_ref[...] = jnp.zeros_like(dw_ref)
            db_ref[...] = jnp.zeros_like(db_ref)

        halo = jnp.where(i > 0, h_ref[...].astype(F32), 0.0)
        ext = jnp.concatenate([halo, x_ref[...].astype(F32)], axis=0)
        d = d_ref[...].astype(F32)
        dext = jnp.concatenate([d, jnp.where(i < nr - 1, dn_ref[...].astype(F32), 0.0)], axis=0)
        dx = jnp.zeros((tr, cb), F32)
        for k in range(CONV_WIDTH):
            s = CONV_WIDTH - 1 - k
            fut = dext if s == 0 else pltpu.roll(dext, shift=tr + hd - s, axis=0)
            dx = dx + w_ref[k:k + 1, :] * fut[:tr]
            past = ext if s == 0 else pltpu.roll(ext, shift=s, axis=0)
            dw_ref[k:k + 1, :] += jnp.sum(d * past[hx:hx + tr], axis=0, keepdims=True)
        dx_ref[...] = dx.astype(dx_ref.dtype)
        db_ref[...] += jnp.sum(d, axis=0, keepdims=True)

    last_halo = t // hd - 1
    return pl.pallas_call(
        body, name=f"ssm_conv_bwd_{col0}",
        out_shape=(jax.ShapeDtypeStruct(dzx.shape, dzx.dtype), jax.ShapeDtypeStruct((CONV_WIDTH, conv_dim), F32),
                   jax.ShapeDtypeStruct((1, conv_dim), F32)),
        grid=(conv_dim // cb, nr),
        in_specs=[pl.BlockSpec((tr, cb), lambda j, i: (i, off + j)),
                  pl.BlockSpec((hx, cb), lambda j, i: (jnp.maximum(i * (tr // hx) - 1, 0), off + j)),
                  pl.BlockSpec((CONV_WIDTH, cb), lambda j, i: (0, woff + j)),
                  pl.BlockSpec((tr, cb), lambda j, i: (i, j)),
                  pl.BlockSpec((hd, cb), lambda j, i: (jnp.minimum((i + 1) * (tr // hd), last_halo), j)),
                  pl.BlockSpec(memory_space=pl.ANY)],
        out_specs=(pl.BlockSpec((tr, cb), lambda j, i: (i, off + j)),
                   pl.BlockSpec((CONV_WIDTH, cb), lambda j, i: (0, j)),
                   pl.BlockSpec((1, cb), lambda j, i: (0, j))),
        input_output_aliases={5: 0},
        compiler_params=pltpu.CompilerParams(dimension_semantics=("parallel", "arbitrary")),
    )(zx, zx, conv_w, dpre, dpre, dzx)


def _eye(n):
    return lax.broadcasted_iota(jnp.int32, (n, n), 0) == lax.broadcasted_iota(jnp.int32, (n, n), 1)


def _row_to_col(row):
    n = row.shape[1]
    return jnp.sum(jnp.where(_eye(n), row, 0.0), axis=1, keepdims=True)


def _col_to_row(col):
    n = col.shape[0]
    return jnp.sum(jnp.where(_eye(n), col, 0.0), axis=0, keepdims=True)


def _pair_lanes(c0, c1):
    lane = lax.broadcasted_iota(jnp.int32, (1, LANES), 1)
    return jnp.where(lane < SSM_HEAD_DIM, c0, c1)


def _ssd_chunk_common(pre_x_ref, pre_b_ref, pre_c_ref, dtr_ref, bias_ref, alog_ref, cs_ref):
    cl = SSD_CHUNK
    hpg = dtr_ref.shape[0]
    x = _silu(pre_x_ref[...].astype(F32))
    b16 = _silu(pre_b_ref[...].astype(F32)).astype(BF16)
    c16 = _silu(pre_c_ref[...].astype(F32)).astype(BF16)
    dt = jax.nn.softplus(dtr_ref[...] + bias_ref[...])
    a = -jnp.exp(alog_ref[...])
    li = lax.broadcasted_iota(jnp.int32, (cl, cl), 0)
    si = lax.broadcasted_iota(jnp.int32, (cl, cl), 1)
    upper = (li <= si).astype(F32)
    cs_ref[0:hpg, :] = jnp.dot(dt * a, upper, precision=lax.Precision.HIGHEST, preferred_element_type=F32)
    cs_ref[hpg:2 * hpg, :] = dt
    g = lax.dot_general(c16, b16, (((1,), (1,)), ((), ())), preferred_element_type=F32)
    return x, b16, c16, dt, a, g, li >= si


def _ssd_fwd(pre, dtT, bias, alog, dskip_lanes, d_inner):
    t = pre.shape[0]
    cl = SSD_CHUNK
    nc = t // cl
    ng = SSM_GROUPS
    hpg = dtT.shape[1]
    gw = hpg * SSM_HEAD_DIM
    assert d_inner == ng * gw and hpg % 2 == 0
    bo = d_inner // SSM_STATE

    def body(px_ref, pb_ref, pc_ref, dtr_ref, bias_ref, alog_ref, dsk_ref, y_ref, st_ref, s_ref, cs_ref):
        c = pl.program_id(1)

        @pl.when(c == 0)
        def _():
            s_ref[...] = jnp.zeros_like(s_ref)

        x, b16, c16, dt, a, g, causal = _ssd_chunk_common(px_ref, pb_ref, pc_ref, dtr_ref, bias_ref, alog_ref, cs_ref)
        st_ref[...] = s_ref[...]
        yoff = lax.dot_general(c16, s_ref[...].astype(BF16), (((1,), (1,)), ((), ())), preferred_element_type=F32)
        xde_parts = []
        for j in range(hpg // 2):
            cols = slice(j * LANES, (j + 1) * LANES)
            xp = x[:, cols]
            dcol, ecol, ocol, ms = [], [], [], []
            for hh in range(2):
                h = 2 * j + hh
                cs_row = cs_ref[h:h + 1, :]
                cs_col = _row_to_col(cs_row)
                dcol.append(_row_to_col(cs_ref[hpg + h:hpg + h + 1, :]))
                ecol.append(jnp.exp(cs_ref[h:h + 1, cl - 1:cl] - cs_col))
                ocol.append(jnp.exp(cs_col))
                lm = jnp.where(causal, jnp.exp(jnp.minimum(cs_col - cs_row, 0.0)), 0.0)
                ms.append((g * lm).astype(BF16))
            xd = xp * _pair_lanes(dcol[0], dcol[1])
            xd16 = xd.astype(BF16)
            yd = _pair_lanes(1.0, 0.0) * jnp.dot(ms[0], xd16, preferred_element_type=F32) \
                + _pair_lanes(0.0, 1.0) * jnp.dot(ms[1], xd16, preferred_element_type=F32)
            y_ref[:, cols] = yd + yoff[:, cols] * _pair_lanes(ocol[0], ocol[1]) + xp * dsk_ref[0:1, cols]
            xde_parts.append((xd * _pair_lanes(ecol[0], ecol[1])).astype(BF16))
        new = lax.dot_general(jnp.concatenate(xde_parts, axis=1), b16, (((0,), (0,)), ((), ())),
                              preferred_element_type=F32)
        for h in range(hpg):
            rows = slice(h * SSM_HEAD_DIM, (h + 1) * SSM_HEAD_DIM)
            s_ref[rows, :] = s_ref[rows, :] * jnp.exp(cs_ref[h:h + 1, cl - 1:cl]) + new[rows, :]

    vec = lambda n: pl.BlockSpec((None, hpg, n), lambda gi, c: (gi, 0, 0))
    return pl.pallas_call(
        body, name="ssd_fwd",
        out_shape=(jax.ShapeDtypeStruct((t, d_inner), F32), jax.ShapeDtypeStruct((ng, nc, gw, SSM_STATE), F32)),
        grid=(ng, nc),
        in_specs=[pl.BlockSpec((cl, gw), lambda gi, c: (c, gi)),
                  pl.BlockSpec((cl, SSM_STATE), lambda gi, c: (c, bo + gi)),
                  pl.BlockSpec((cl, SSM_STATE), lambda gi, c: (c, bo + ng + gi)),
                  pl.BlockSpec((None, hpg, cl), lambda gi, c: (gi, 0, c)),
                  vec(1), vec(1),
                  pl.BlockSpec((1, gw), lambda gi, c: (0, gi))],
        out_specs=(pl.BlockSpec((cl, gw), lambda gi, c: (c, gi)),
                   pl.BlockSpec((None, None, gw, SSM_STATE), lambda gi, c: (gi, c, 0, 0))),
        scratch_shapes=[pltpu.VMEM((gw, SSM_STATE), F32), pltpu.VMEM((2 * hpg, cl), F32)],
        compiler_params=pltpu.CompilerParams(dimension_semantics=("parallel", "arbitrary")),
    )(pre, pre, pre, dtT, bias, alog, dskip_lanes)


def _ssd_bwd(pre, dtT, bias, alog, dskip_lanes, states, dy, d_inner):
    t, conv_dim = pre.shape
    cl = SSD_CHUNK
    nc = t // cl
    ng = SSM_GROUPS
    hpg = dtT.shape[1]
    gw = hpg * SSM_HEAD_DIM
    bo = d_inner // SSM_STATE

    def body(px_ref, pb_ref, pc_ref, dtr_ref, bias_ref, alog_ref, dsk_ref, st_ref, dy_ref,
             dx_ref, db_ref, dc_ref, ddt_ref, acc_ref, dsk_out, ds_ref, cs_ref, dcs_ref):
        c = pl.program_id(1)

        @pl.when(c == 0)
        def _():
            ds_ref[...] = jnp.zeros_like(ds_ref)
            acc_ref[...] = jnp.zeros_like(acc_ref)
            dsk_out[...] = jnp.zeros_like(dsk_out)

        x, b16, c16, dt, a, g, causal = _ssd_chunk_common(px_ref, pb_ref, pc_ref, dtr_ref, bias_ref, alog_ref, cs_ref)
        s_prev = st_ref[...]
        s16 = s_prev.astype(BF16)
        ds = ds_ref[...]
        ds16 = ds.astype(BF16)
        dyv = dy_ref[...]
        yoff = lax.dot_general(c16, s16, (((1,), (1,)), ((), ())), preferred_element_type=F32)
        bds = lax.dot_general(b16, ds16, (((1,), (1,)), ((), ())), preferred_element_type=F32)
        dg = jnp.zeros((cl, cl), F32)
        xde_parts, dye_parts = [], []
        lane = lax.broadcasted_iota(jnp.int32, (1, LANES), 1)
        for j in range(hpg // 2):
            cols = slice(j * LANES, (j + 1) * LANES)
            xp, dyp = x[:, cols], dyv[:, cols]
            dcol, ecol, ocol, lms = [], [], [], []
            for hh in range(2):
                h = 2 * j + hh
                cs_row = cs_ref[h:h + 1, :]
                cs_col = _row_to_col(cs_row)
                dcol.append(_row_to_col(cs_ref[hpg + h:hpg + h + 1, :]))
                ecol.append(jnp.exp(cs_ref[h:h + 1, cl - 1:cl] - cs_col))
                ocol.append(jnp.exp(cs_col))
                lms.append(jnp.where(causal, jnp.exp(jnp.minimum(cs_col - cs_row, 0.0)), 0.0))
            dlanes, elanes, olanes = _pair_lanes(*dcol), _pair_lanes(*ecol), _pair_lanes(*ocol)
            xd = xp * dlanes
            xd16 = xd.astype(BF16)
            xde = xd * elanes
            yoffp = yoff[:, cols] * olanes
            bdsp = bds[:, cols]
            dxd = bdsp * elanes
            for hh in range(2):
                h = 2 * j + hh
                hmask = (lane // SSM_HEAD_DIM) == hh
                dyh16 = jnp.where(hmask, dyp, 0.0).astype(BF16)
                m = g * lms[hh]
                dm = lax.dot_general(dyh16, xd16, (((1,), (1,)), ((), ())), preferred_element_type=F32)
                w = dm * m
                dg = dg + dm * lms[hh]
                dxd = dxd + lax.dot_general(m.astype(BF16), dyh16, (((0,), (0,)), ((), ())),
                                            preferred_element_type=F32)
                term = jnp.sum(jnp.where(hmask, xde * bdsp, 0.0), axis=1, keepdims=True)
                dcs_col = (jnp.sum(w, axis=1, keepdims=True)
                           + jnp.sum(jnp.where(hmask, dyp * yoffp, 0.0), axis=1, keepdims=True) - term)
                rows = slice(h * SSM_HEAD_DIM, (h + 1) * SSM_HEAD_DIM)
                dec = jnp.exp(cs_ref[h:h + 1, cl - 1:cl])
                tail = jnp.sum(term, axis=0, keepdims=True) + dec * jnp.sum(
                    jnp.sum(s_prev[rows, :] * ds[rows, :], axis=1, keepdims=True), axis=0, keepdims=True)
                last = lax.broadcasted_iota(jnp.int32, (1, cl), 1) == cl - 1
                dcs_ref[h:h + 1, :] = _col_to_row(dcs_col) - jnp.sum(w, axis=0, keepdims=True) + jnp.where(last, tail, 0.0)
                dcs_ref[hpg + h:hpg + h + 1, :] = _col_to_row(
                    jnp.sum(jnp.where(hmask, dxd * xp, 0.0), axis=1, keepdims=True))
            dx_act = dxd * dlanes + dyp * dsk_ref[0:1, cols]
            dx_ref[:, cols] = (dx_act * _silu_grad(px_ref[:, cols].astype(F32))).astype(dx_ref.dtype)
            dsk_out[0:1, cols] += jnp.sum(dyp * xp, axis=0, keepdims=True)
            xde_parts.append(xde.astype(BF16))
            dye_parts.append((dyp * olanes).astype(BF16))
        xde16 = jnp.concatenate(xde_parts, axis=1)
        dye16 = jnp.concatenate(dye_parts, axis=1)
        dg16 = dg.astype(BF16)
        dc_act = jnp.dot(dg16, b16, preferred_element_type=F32) + jnp.dot(dye16, s16, preferred_element_type=F32)
        db_act = lax.dot_general(dg16, c16, (((0,), (0,)), ((), ())), preferred_element_type=F32) \
            + jnp.dot(xde16, ds16, preferred_element_type=F32)
        dc_ref[...] = (dc_act * _silu_grad(pc_ref[...].astype(F32))).astype(dc_ref.dtype)
        db_ref[...] = (db_act * _silu_grad(pb_ref[...].astype(F32))).astype(db_ref.dtype)
        ds_new = lax.dot_general(dye16, c16, (((0,), (0,)), ((), ())), preferred_element_type=F32)
        for h in range(hpg):
            rows = slice(h * SSM_HEAD_DIM, (h + 1) * SSM_HEAD_DIM)
            ds_ref[rows, :] = ds[rows, :] * jnp.exp(cs_ref[h:h + 1, cl - 1:cl]) + ds_new[rows, :]
        li = lax.broadcasted_iota(jnp.int32, (cl, cl), 0)
        si = lax.broadcasted_iota(jnp.int32, (cl, cl), 1)
        d_adt = jnp.dot(dcs_ref[0:hpg, :], (li >= si).astype(F32), precision=lax.Precision.HIGHEST,
                        preferred_element_type=F32)
        ddt = d_adt * a + dcs_ref[hpg:2 * hpg, :]
        ddt_raw = ddt * jax.nn.sigmoid(dtr_ref[...] + bias_ref[...])
        ddt_ref[...] = ddt_raw
        acc_ref[0:hpg, :] += d_adt * dt
        acc_ref[hpg:2 * hpg, :] += ddt_raw

    rc = lambda c: nc - 1 - c
    vec = lambda n: pl.BlockSpec((None, hpg, n), lambda gi, c: (gi, 0, 0))
    x_spec = pl.BlockSpec((cl, gw), lambda gi, c: (rc(c), gi))
    b_spec = pl.BlockSpec((cl, SSM_STATE), lambda gi, c: (rc(c), bo + gi))
    c_spec = pl.BlockSpec((cl, SSM_STATE), lambda gi, c: (rc(c), bo + ng + gi))
    dt_spec = pl.BlockSpec((None, hpg, cl), lambda gi, c: (gi, 0, rc(c)))
    return pl.pallas_call(
        body, name="ssd_bwd",
        out_shape=(jax.ShapeDtypeStruct((t, d_inner), BF16), jax.ShapeDtypeStruct((t, ng * SSM_STATE), BF16),
                   jax.ShapeDtypeStruct((t, ng * SSM_STATE), BF16), jax.ShapeDtypeStruct(dtT.shape, F32),
                   jax.ShapeDtypeStruct((ng, 2 * hpg, cl), F32), jax.ShapeDtypeStruct((1, d_inner), F32)),
        grid=(ng, nc),
        in_specs=[x_spec, b_spec, c_spec, dt_spec, vec(1), vec(1),
                  pl.BlockSpec((1, gw), lambda gi, c: (0, gi)),
                  pl.BlockSpec((None, None, gw, SSM_STATE), lambda gi, c: (gi, rc(c), 0, 0)),
                  x_spec],
        out_specs=(x_spec, pl.BlockSpec((cl, SSM_STATE), lambda gi, c: (rc(c), gi)),
                   pl.BlockSpec((cl, SSM_STATE), lambda gi, c: (rc(c), gi)), dt_spec,
                   pl.BlockSpec((None, 2 * hpg, cl), lambda gi, c: (gi, 0, 0)),
                   pl.BlockSpec((1, gw), lambda gi, c: (0, gi))),
        scratch_shapes=[pltpu.VMEM((gw, SSM_STATE), F32), pltpu.VMEM((2 * hpg, cl), F32),
                        pltpu.VMEM((2 * hpg, cl), F32)],
        compiler_params=pltpu.CompilerParams(dimension_semantics=("parallel", "arbitrary")),
    )(pre, pre, pre, dtT, bias, alog, dskip_lanes, states, dy)


def _gate_norm_fwd(y, zx, norm_w, d_inner):
    t = y.shape[0]
    tr = _pick(t, (256, 128))
    gs = d_inner // SSM_GROUPS

    def body(y_ref, z_ref, w_ref, o_ref):
        for gi in range(SSM_GROUPS):
            cols = slice(gi * gs, (gi + 1) * gs)
            v = y_ref[:, cols] * _silu(z_ref[:, cols].astype(F32))
            r = lax.rsqrt(jnp.mean(v * v, axis=-1, keepdims=True) + NORM_EPS)
            o_ref[:, cols] = (v * r * w_ref[0:1, cols]).astype(BF16)

    spec = pl.BlockSpec((tr, d_inner), lambda i: (i, 0))
    return pl.pallas_call(
        body, name="ssm_gate_norm_fwd", out_shape=jax.ShapeDtypeStruct((t, d_inner), BF16), grid=(t // tr,),
        in_specs=[spec, spec, pl.BlockSpec((1, d_inner), lambda i: (0, 0))], out_specs=spec,
        compiler_params=pltpu.CompilerParams(dimension_semantics=("parallel",)),
    )(y, zx, norm_w.reshape(1, -1))


def _gate_norm_bwd(y, zx, norm_w, dout, d_inner):
    t, width = zx.shape
    tr = _pick(t, (256, 128))
    gs = d_inner // SSM_GROUPS

    def body(y_ref, z_ref, w_ref, do_ref, dy_ref, dz_ref, dw_ref):
        @pl.when(pl.program_id(0) == 0)
        def _():
            dw_ref[...] = jnp.zeros_like(dw_ref)

        for gi in range(SSM_GROUPS):
            cols = slice(gi * gs, (gi + 1) * gs)
            yv, zv = y_ref[:, cols], z_ref[:, cols].astype(F32)
            sz = _silu(zv)
            v = yv * sz
            r = lax.rsqrt(jnp.mean(v * v, axis=-1, keepdims=True) + NORM_EPS)
            vhat = v * r
            dn = do_ref[:, cols].astype(F32)
            dw_ref[0:1, cols] += jnp.sum(dn * vhat, axis=0, keepdims=True)
            dvh = dn * w_ref[0:1, cols]
            dv = r * (dvh - vhat * jnp.mean(dvh * vhat, axis=-1, keepdims=True))
            dy_ref[:, cols] = dv * sz
            dz_ref[:, cols] = (dv * yv * _silu_grad(zv)).astype(dz_ref.dtype)

    spec = pl.BlockSpec((tr, d_inner), lambda i: (i, 0))
    wspec = pl.BlockSpec((1, d_inner), lambda i: (0, 0))
    return pl.pallas_call(
        body, name="ssm_gate_norm_bwd",
        out_shape=(jax.ShapeDtypeStruct((t, d_inner), F32), jax.ShapeDtypeStruct((t, width), zx.dtype),
                   jax.ShapeDtypeStruct((1, d_inner), F32)),
        grid=(t // tr,),
        in_specs=[spec, spec, wspec, spec], out_specs=(spec, spec, wspec),
        compiler_params=pltpu.CompilerParams(dimension_semantics=("arbitrary",)),
    )(y, zx, norm_w.reshape(1, -1), dout)


def _ssm_small(dt_raw, dt_bias, a_log, d_skip):
    heads = dt_raw.shape[1]
    hpg = heads // SSM_GROUPS
    dtT = dt_raw.T.reshape(SSM_GROUPS, hpg, -1)
    return (dtT, dt_bias.reshape(SSM_GROUPS, hpg, 1), a_log.reshape(SSM_GROUPS, hpg, 1),
            jnp.repeat(d_skip, SSM_HEAD_DIM).reshape(1, -1))


def _ssm_core_fwd(zx, dt_raw, conv_w, conv_b, dt_bias, a_log, d_skip, norm_w):
    d_inner = norm_w.shape[0]
    pre = _conv_fwd(zx, conv_w, conv_b, d_inner)
    dtT, bias, alog, dsk = _ssm_small(dt_raw, dt_bias, a_log, d_skip)
    y, states = _ssd_fwd(pre, dtT, bias, alog, dsk, d_inner)
    out = _gate_norm_fwd(y, zx, norm_w, d_inner)
    return out, (zx, dt_raw, conv_w, dt_bias, a_log, d_skip, norm_w, pre, y, states)


def _ssm_core_bwd(res, dout):
    zx, dt_raw, conv_w, dt_bias, a_log, d_skip, norm_w, pre, y, states = res
    d_inner = norm_w.shape[0]
    heads = dt_raw.shape[1]
    dy, dzx, dnorm = _gate_norm_bwd(y, zx, norm_w, dout, d_inner)
    dtT, bias, alog, dsk = _ssm_small(dt_raw, dt_bias, a_log, d_skip)
    dx, db, dc, ddtT, acc, dsk_l = _ssd_bwd(pre, dtT, bias, alog, dsk, states, dy, d_inner)
    dws, dbs, col0 = [], [], 0
    for part in (dx, db, dc):
        dzx, dw_part, db_part = _conv_bwd(zx, conv_w, part, dzx, d_inner, col0)
        dws.append(dw_part)
        dbs.append(db_part)
        col0 += part.shape[1]
    dconv_w, dconv_b = jnp.concatenate(dws, axis=1), jnp.concatenate(dbs, axis=1)
    d_dt_raw = ddtT.reshape(heads, -1).T
    hpg = heads // SSM_GROUPS
    da = acc[:, :hpg].sum(-1).reshape(heads)
    d_bias = acc[:, hpg:].sum(-1).reshape(heads)
    d_alog = da * (-jnp.exp(a_log))
    d_dskip = dsk_l.reshape(heads, SSM_HEAD_DIM).sum(-1)
    return dzx, d_dt_raw, dconv_w, dconv_b.reshape(-1), d_bias, d_alog, d_dskip, dnorm.reshape(-1)


ROWS_VMEM_BYTES = 24 * 1024 * 1024


def _rows_call(body, name, ins, outs, acc_outs=(), rows=256):
    t = max(a.shape[0] for a in ins)
    row_bytes = sum(a.shape[1] * jnp.dtype(a.dtype).itemsize for a in list(ins) + list(outs) if a.shape[0] == t)
    if 2 * 2 * rows * row_bytes <= ROWS_VMEM_BYTES:
        rows *= 2
    tr = _pick(t, (rows, 256, 128, 64, 32, 16, 8))

    def spec(a):
        if a.shape[0] == t:
            return pl.BlockSpec((tr, a.shape[1]), lambda i: (i, 0))
        return pl.BlockSpec(a.shape, lambda i: (0, 0))

    return pl.pallas_call(
        body, name=name, out_shape=tuple(outs) + tuple(acc_outs), grid=(t // tr,),
        in_specs=[spec(a) for a in ins],
        out_specs=tuple(spec(a) for a in outs) + tuple(pl.BlockSpec(a.shape, lambda i: (0, 0)) for a in acc_outs),
        compiler_params=pltpu.CompilerParams(dimension_semantics=("arbitrary" if acc_outs else "parallel",)),
    )(*ins)


def _rms_fwd(x, gain, after=None):
    def body(x_ref, g_ref, *rest):
        v = x_ref[...]
        rest[-1][...] = (v * lax.rsqrt(jnp.mean(v * v, axis=-1, keepdims=True) + NORM_EPS) * g_ref[...]).astype(BF16)

    ins = [x, gain.reshape(1, -1)] + ([] if after is None else [after])
    (h,) = _rows_call(body, "rms_fwd", ins, [jax.ShapeDtypeStruct(x.shape, BF16)])
    return h


def _rms_bwd(x, gain, dh, dres, after):
    def body(x_ref, g_ref, dh_ref, dr_ref, *rest):
        dx_ref, dg_ref = rest[-2:]

        @pl.when(pl.program_id(0) == 0)
        def _():
            dg_ref[...] = jnp.zeros_like(dg_ref)

        v = x_ref[...]
        r = lax.rsqrt(jnp.mean(v * v, axis=-1, keepdims=True) + NORM_EPS)
        vhat = v * r
        d = dh_ref[...].astype(F32)
        dg_ref[...] += jnp.sum(d * vhat, axis=0, keepdims=True)
        dvh = d * g_ref[...]
        dx_ref[...] = dr_ref[...] + r * (dvh - vhat * jnp.mean(dvh * vhat, axis=-1, keepdims=True))

    ins = [x, gain.reshape(1, -1), dh, dres] + ([] if after is None else [after])
    dx, dg = _rows_call(body, "rms_bwd", ins, [jax.ShapeDtypeStruct(x.shape, F32)],
                        [jax.ShapeDtypeStruct((1, x.shape[1]), F32)])
    return dx, dg.reshape(gain.shape)


def _swiglu_fwd(gu):
    t, f2 = gu.shape
    f = f2 // 2

    def body(gu_ref, o_ref):
        o_ref[...] = (_silu(gu_ref[:, :f].astype(F32)) * gu_ref[:, f:].astype(F32)).astype(BF16)

    (act,) = _rows_call(body, "swiglu_fwd", [gu], [jax.ShapeDtypeStruct((t, f), BF16)])
    return act


def _swiglu_bwd(gu, dact):
    t, f2 = gu.shape
    f = f2 // 2

    def body(gu_ref, d_ref, o_ref):
        g, u, d = gu_ref[:, :f].astype(F32), gu_ref[:, f:].astype(F32), d_ref[...].astype(F32)
        o_ref[:, :f] = (d * u * _silu_grad(g)).astype(BF16)
        o_ref[:, f:] = (d * _silu(g)).astype(BF16)

    (dgu,) = _rows_call(body, "swiglu_bwd", [gu, dact], [jax.ShapeDtypeStruct((t, f2), BF16)])
    return dgu


def _ple_fwd(x, gl, ple):
    def body(x_ref, g_ref, p_ref, o_ref):
        o_ref[...] = x_ref[...] + jax.nn.sigmoid(g_ref[...].astype(F32)) * p_ref[...].astype(F32)

    (out,) = _rows_call(body, "ple_fwd", [x, gl, ple], [jax.ShapeDtypeStruct(x.shape, F32)])
    return out


def _ple_bwd(gl, ple, dout):
    def body(g_ref, p_ref, d_ref, dg_ref, dp_ref):
        s, d = jax.nn.sigmoid(g_ref[...].astype(F32)), d_ref[...]
        dg_ref[...] = (d * p_ref[...].astype(F32) * s * (1.0 - s)).astype(BF16)
        dp_ref[...] = (d * s).astype(BF16)

    shp = jax.ShapeDtypeStruct(gl.shape, BF16)
    return _rows_call(body, "ple_bwd", [gl, ple, dout], [shp, shp])


def _loss_fwd(y, target):
    inv = 1.0 / y.shape[1]

    def body(y_ref, t_ref, d_ref, l_ref):
        @pl.when(pl.program_id(0) == 0)
        def _():
            l_ref[...] = jnp.zeros_like(l_ref)

        e = y_ref[...] - t_ref[...]
        d_ref[...] = e * inv
        part = jnp.sum(jnp.sum(e * e, axis=1, keepdims=True), axis=0, keepdims=True) * (0.5 * inv)
        l_ref[...] += jnp.broadcast_to(part, l_ref.shape)

    dy, acc = _rows_call(body, "loss_fwd", [y, target], [jax.ShapeDtypeStruct(y.shape, F32)],
                         [jax.ShapeDtypeStruct((SUBLANES, LANES), F32)])
    return acc[0, 0], dy


def local_step(small, fetch, emit, x, p, target):
    depth = small['norm_mix'].shape[0]
    ssm_small = ('ssm_conv_w', 'ssm_conv_b', 'ssm_dt_bias', 'ssm_a_log', 'ssm_d_skip', 'ssm_norm_w')
    saved = []
    for i in range(depth):
        j = i // 2
        s = {'x': x}
        wm, token = fetch(('mix', i), x)
        h = s['h'] = _rms_fwd(x, small['norm_mix'][i], token)
        if i % 2 == 0:
            n_main = wm['ssm_w_in'].shape[0] - small['ssm_dt_bias'].shape[1]
            zx = _mm(h, wm['ssm_w_in'][:n_main], tb=True, out_dtype=BF16, name="ssm_in_fwd")
            more, token = fetch(('out', i), zx)
            wm = {**wm, **more}
            dt_raw = _mm(h, wm['ssm_w_in'][n_main:], tb=True, after=token, name="ssm_dt_fwd")
            y, s['mix'] = _ssm_core_fwd(zx, dt_raw, wm['ssm_conv_w'], *[small[n][j] for n in ssm_small[1:]])
            x = _mm(y, wm['ssm_w_out'], add=x, name="ssm_out_fwd")
        else:
            qkv = _mm(h, wm['att_w_qkv'], tb=True, out_dtype=BF16, name="att_qkv_fwd")
            y, s['mix'] = _attention_core_fwd(qkv, small['att_q_norm'][j], small['att_k_norm'][j])
            x = _mm(y, wm['att_w_o'], add=x, name="att_o_fwd")
        s['wm'], s['y'], s['x1'] = wm, y, x
        wf, token = fetch(('ffn', i), x)
        s['wf'] = wf
        h2 = s['h2'] = _rms_fwd(x, small['norm_ffn'][i], token)
        gu = s['gu'] = _mm(h2, wf['ffn_w_gu'], tb=True, out_dtype=BF16, name="ffn_gu_fwd")
        act = s['act'] = _swiglu_fwd(gu)
        x = s['x2'] = _mm(act, wf['ffn_w_down'], add=x, name="ffn_down_fwd")
        gl = s['gl'] = _mm(x, wf['ple_w_gate'], out_dtype=BF16, name="ple_gate_fwd")
        ple = s['ple'] = _mm(p[i], wf['ple_w_proj'], tb=True, out_dtype=BF16, name="ple_proj_fwd")
        x = _ple_fwd(x, gl, ple)
        saved.append(s)
    loss, dx = _loss_fwd(x, target)

    g = {n: [None] * small[n].shape[0] for n in small}
    for i in reversed(range(depth)):
        j = i // 2
        s = saved[i]
        wm, wf = s['wm'], s['wf']
        gf = {}
        dgl, dple = _ple_bwd(s['gl'], s['ple'], dx)
        gf['ple_w_proj'] = _mm(dple, p[i], ta=True, out_dtype=BF16, name="ple_proj_dw")
        gf['ple_w_gate'] = _mm(s['x2'], dgl, ta=True, out_dtype=BF16, name="ple_gate_dw")
        dx = _mm(dgl, wf['ple_w_gate'], tb=True, add=dx, name="ple_gate_da")
        dact = _mm(dx, wf['ffn_w_down'], tb=True, out_dtype=BF16, name="ffn_down_da")
        gf['ffn_w_down'] = _mm(s['act'], dx, ta=True, out_dtype=BF16, name="ffn_down_dw")
        dgu = _swiglu_bwd(s['gu'], dact)
        dh2 = _mm(dgu, wf['ffn_w_gu'], out_dtype=BF16, name="ffn_gu_da")
        gf['ffn_w_gu'] = _mm(dgu, s['h2'], ta=True, out_dtype=BF16, name="ffn_gu_dw")
        dx, g['norm_ffn'][i] = _rms_bwd(s['x1'], small['norm_ffn'][i], dh2, dx, emit(('ffn', i), gf))
        gm = {}
        if i % 2 == 0:
            n_main = wm['ssm_w_in'].shape[0] - small['ssm_dt_bias'].shape[1]
            dyn = _mm(dx, wm['ssm_w_out'], tb=True, out_dtype=BF16, name="ssm_out_da")
            gm['ssm_w_out'] = _mm(s['y'], dx, ta=True, out_dtype=BF16, name="ssm_out_dw")
            dzx, d_dt, *sg = _ssm_core_bwd(s['mix'], dyn)
            for n, v in zip(ssm_small, sg):
                g[n][j] = v
            dh = _mm(d_dt, wm['ssm_w_in'][n_main:], name="ssm_dt_da")
            dh = _mm(dzx, wm['ssm_w_in'][:n_main], add=dh, out_dtype=BF16, name="ssm_in_da")
            gm['ssm_w_in'] = jnp.concatenate([_mm(dzx, s['h'], ta=True, out_dtype=BF16, name="ssm_in_dw"),
                                              _mm(d_dt, s['h'], ta=True, out_dtype=BF16, name="ssm_dt_dw")], axis=0)
        else:
            do = _mm(dx, wm['att_w_o'], tb=True, name="att_o_da")
            gm['att_w_o'] = _mm(s['y'], dx, ta=True, out_dtype=BF16, name="att_o_dw")
            dqkv, g['att_q_norm'][j], g['att_k_norm'][j] = _attention_core_bwd(s['mix'], do)
            dh = _mm(dqkv, wm['att_w_qkv'], out_dtype=BF16, name="att_qkv_da")
            gm['att_w_qkv'] = _mm(dqkv, s['h'], ta=True, out_dtype=BF16, name="att_qkv_dw")
        dx, g['norm_mix'][i] = _rms_bwd(s['x'], small['norm_mix'][i], dh, dx, emit(('mix', i), gm))
    return loss, dx, {n: jnp.stack(v) for n, v in g.items()}


CHANNEL_WEIGHTS = ('ffn_w_gate', 'ffn_w_up', 'ffn_w_down', 'ple_w_proj', 'ple_w_gate')


def _stages(depth):
    gather, scatter = {}, {}
    for i in range(depth):
        j = i // 2
        if i % 2 == 0:
            gather['mix', i] = [('ssm_w_in', j), ('ssm_conv_w', j)]
            gather['out', i] = [('ssm_w_out', j)]
            scatter['mix', i] = [('ssm_w_in', j), ('ssm_w_out', j)]
        else:
            gather['mix', i] = scatter['mix', i] = [('att_w_qkv', j), ('att_w_o', j)]
        gather['ffn', i] = scatter['ffn', i] = [(n, i) for n in CHANNEL_WEIGHTS]
    return gather, scatter


def _pack_plan(shapes, width, members):
    plan, off = [], 0
    for name, lyr in members:
        _, r, c = shapes[name]
        if name in COL_SHARDED:
            r, c = c, r
        if name == 'ssm_conv_w':
            pr = -(-2 * r * c // width)
        else:
            assert (r * c) % width == 0, (name, r, c)
            pr = r * c // width
        plan.append((name, lyr, r, c, pr, off))
        off += _round_up(pr, BF16_ROWS)
    return plan, off


def _small_plan(shapes):
    plan, off = [], 0
    for name in SMALL:
        n = math.prod(shapes[name])
        plan.append((name, n, off))
        off += n
    return plan, _round_up(off, SUBLANES * LANES)


def kernel(x, p, norm_mix, norm_ffn, ssm_w_in, ssm_conv_w, ssm_conv_b, ssm_dt_bias, ssm_a_log, ssm_d_skip, ssm_norm_w, ssm_w_out, att_w_qkv, att_q_norm, att_k_norm, att_w_o, ffn_w_gate, ffn_w_up, ffn_w_down, ple_w_proj, ple_w_gate, loss_target, m_norm_mix, m_norm_ffn, m_ssm_w_in, m_ssm_conv_w, m_ssm_conv_b, m_ssm_dt_bias, m_ssm_a_log, m_ssm_d_skip, m_ssm_norm_w, m_ssm_w_out, m_att_w_qkv, m_att_q_norm, m_att_k_norm, m_att_w_o, m_ffn_w_gate, m_ffn_w_up, m_ffn_w_down, m_ple_w_proj, m_ple_w_gate, v_norm_mix, v_norm_ffn, v_ssm_w_in, v_ssm_conv_w, v_ssm_conv_b, v_ssm_dt_bias, v_ssm_a_log, v_ssm_d_skip, v_ssm_norm_w, v_ssm_w_out, v_att_w_qkv, v_att_q_norm, v_att_k_norm, v_att_w_o, v_ffn_w_gate, v_ffn_w_up, v_ffn_w_down, v_ple_w_proj, v_ple_w_gate):
    given = dict(locals())
    w_in = {n: given[n] for n in WEIGHTS}
    m_in = {n: given["m_" + n] for n in WEIGHTS}
    v_in = {n: given["v_" + n] for n in WEIGHTS}
    width = x.shape[-1]
    depth = norm_mix.shape[0]

    gather_members, scatter_members = _stages(depth)
    shapes = {n: w_in[n].shape for n in BIG + ('ssm_conv_w',)}
    plans = {key: _pack_plan(shapes, width, members)[0] for key, members in gather_members.items()}
    splans = {key: _pack_plan(shapes, width, members)[0] for key, members in scatter_members.items()}
    order = list(gather_members)

    def pack_weights(stage):
        pieces = []
        for name, layer, r, c, pr, off in plans[stage]:
            blk = w_in[name][layer]
            if name == 'ssm_conv_w':
                blk = lax.bitcast_convert_type(blk.reshape(-1), BF16).reshape(-1)
                blk = jnp.pad(blk, (0, pr * width - blk.shape[0]))
            elif name in COL_SHARDED:
                blk = blk.T
            blk = blk.astype(BF16).reshape(pr, width)
            pieces.append(jnp.pad(blk, ((0, _round_up(pr, BF16_ROWS) - pr), (0, 0))))
        return jnp.concatenate(pieces, axis=0)

    packed = [pack_weights(order[0])]
    pending = [exchange_start(packed[0], True, "gather_start_0")]
    packed += [pack_weights(stage) for stage in order[1:]]

    def fetch(stage, after):
        k = order.index(stage)
        handle, token = pending[k]
        land = exchange_wait(handle, [token] + packed[1:] if k == 0 else after, True, f"gather_wait_{k}")
        token = None
        if k + 1 < len(order):
            pending.append(exchange_start(packed[k + 1], True, f"gather_start_{k + 1}", land))
            token = pending[-1][1]
        got = {}
        for name, layer, r, c, pr, off in plans[stage]:
            piece = land[:, off:off + pr]
            if name == 'ssm_conv_w':
                taps, chans = w_in[name].shape[1:]
                bits = piece.reshape(N_DEV, -1)[:, :2 * taps * chans].reshape(N_DEV, taps * chans, 2)
                piece = lax.bitcast_convert_type(bits, F32).reshape(N_DEV, taps, chans)
                got[name] = piece.transpose(1, 0, 2).reshape(taps, N_DEV * chans)
            else:
                got[name] = piece.reshape(N_DEV * r, c)
        if 'ffn_w_gate' in got:
            got['ffn_w_gu'] = jnp.concatenate([got.pop('ffn_w_gate'), got.pop('ffn_w_up')], axis=0)
        return got, token

    scatters = {}

    def emit(stage, grads):
        grads = dict(grads)
        if 'ffn_w_gu' in grads:
            hidden = grads['ffn_w_gu'].shape[0] // 2
            grads['ffn_w_gate'], grads['ffn_w_up'] = grads['ffn_w_gu'][:hidden], grads['ffn_w_gu'][hidden:]
        pieces = []
        for name, layer, r, c, pr, off in splans[stage]:
            g = grads[name].reshape(N_DEV, pr, width)
            pieces.append(jnp.pad(g, ((0, 0), (0, _round_up(pr, BF16_ROWS) - pr), (0, 0))))
        scatters[stage], token = exchange_start(jnp.concatenate(pieces, axis=1), False,
                                                f"scatter_start_{len(scatters)}")
        return token

    small = {n: w_in[n] for n in SMALL}
    cs = w_in['ssm_conv_w'].shape[2]
    loss_local, gx, gw = local_step(small, fetch, emit, x[0], p[:, 0], loss_target[0])
    loss = lax.psum(loss_local, ("x", "y", "c"))

    parts = {}
    for k, (stage, handle) in enumerate(scatters.items()):
        received = exchange_wait(handle, gx, False, f"scatter_wait_{k}")
        gsum = sum_slots(received, f"sum_grads_{k}")
        for name, layer, r, c, pr, off in splans[stage]:
            g = gsum[off:off + pr].reshape(r, c)
            parts[name, layer] = g.T if name in COL_SHARDED else g
    grads = {n: jnp.stack([parts[n, layer] for layer in range(w_in[n].shape[0])]) for n in BIG}

    splan, stotal = _small_plan({n: gw[n].shape for n in SMALL})
    svec = jnp.concatenate([gw[n].reshape(-1) for n, _, _ in splan])
    svec = jnp.pad(svec, (0, stotal - svec.shape[0])).reshape(stotal // LANES, LANES)
    _, ssum = all_gather_sum_small(svec, "sum_small_grads")
    ssum = ssum.reshape(-1)
    for name, n, off in splan:
        grads[name] = ssum[off:off + n].reshape(gw[name].shape)
    me = _me()
    grads['ssm_conv_w'] = lax.dynamic_slice_in_dim(grads['ssm_conv_w'], me * cs, cs, axis=2)

    delta, new_m, new_v = {}, {}, {}
    for name in BIG:
        shp = w_in[name].shape
        flat = lambda a: a.reshape(-1, shp[-1])
        d, nm, nv = adamw(flat(w_in[name]), flat(grads[name]), flat(m_in[name]), flat(v_in[name]), "adamw_" + name)
        delta[name], new_m[name], new_v[name] = d.reshape(shp), nm.reshape(shp), nv.reshape(shp)
    splan2, stotal2 = _small_plan({n: w_in[n].shape for n in SMALL})

    def pack_small(src):
        vec = jnp.concatenate([src[n].reshape(-1) for n, _, _ in splan2])
        return jnp.pad(vec, (0, stotal2 - vec.shape[0]), constant_values=1.0).reshape(stotal2 // LANES, LANES)

    sd, snm, snv = adamw(pack_small(w_in), pack_small(grads), pack_small(m_in), pack_small(v_in), "adamw_small")
    for name, n, off in splan2:
        shp = w_in[name].shape
        delta[name] = sd.reshape(-1)[off:off + n].reshape(shp)
        new_m[name] = snm.reshape(-1)[off:off + n].reshape(shp)
        new_v[name] = snv.reshape(-1)[off:off + n].reshape(shp)

    return (loss, gx[None], *[grads[n] for n in WEIGHTS], *[delta[n] for n in WEIGHTS],
            *[new_m[n] for n in WEIGHTS], *[new_v[n] for n in WEIGHTS])
```

```python
import functools
import math

import jax
import jax.numpy as jnp
from jax import lax
from jax.experimental import pallas as pl
from jax.experimental.pallas import tpu as pltpu

F32 = jnp.float32
BF16 = jnp.bfloat16
N_DEV = 8
MESH = pl.DeviceIdType.MESH

SSM_HEAD_DIM = 64
SSM_GROUPS = 4
SSM_STATE = 128
CONV_WIDTH = 4
SSD_CHUNK = 128
ATT_HEAD_DIM = 64
DIL_PATTERNS = ((128, 1), (512, 4), (2048, 16))
NORM_EPS = 1e-6
ADAM_LR = 0.001
ADAM_B1 = 0.9
ADAM_B2 = 0.999
ADAM_EPS = 1e-08
ADAM_WD = 0.01
ADAM_STEP = 10

BF16_ROWS = 16
LANES = 128
SUBLANES = 8

WEIGHTS = ['norm_mix', 'norm_ffn', 'ssm_w_in', 'ssm_conv_w', 'ssm_conv_b', 'ssm_dt_bias', 'ssm_a_log', 'ssm_d_skip',
           'ssm_norm_w', 'ssm_w_out', 'att_w_qkv', 'att_q_norm', 'att_k_norm', 'att_w_o', 'ffn_w_gate', 'ffn_w_up',
           'ffn_w_down', 'ple_w_proj', 'ple_w_gate']
COL_SHARDED = ('ssm_w_in', 'att_w_qkv', 'ffn_w_gate', 'ffn_w_up', 'ple_w_proj')
ROW_SHARDED = ('ssm_w_out', 'att_w_o', 'ffn_w_down', 'ple_w_gate')
BIG = COL_SHARDED + ROW_SHARDED
SMALL = ('norm_mix', 'norm_ffn', 'ssm_conv_w', 'ssm_conv_b', 'ssm_dt_bias', 'ssm_a_log', 'ssm_d_skip', 'ssm_norm_w',
         'att_q_norm', 'att_k_norm')


def _pick(n, cands):
    for c in cands:
        if n % c == 0:
            return c
    return n


def _round_up(n, m):
    return -(-n // m) * m


MM_TILES = (1024, 1408, 512, 256, 128)
MM_VMEM_BYTES = 48 * 1024 * 1024


def _mm(a, b, *, ta=False, tb=False, out_dtype=F32, add=None, after=None, name):
    k_dim, m_dim = (a.shape if ta else a.shape[::-1])
    n_dim = b.shape[0] if tb else b.shape[1]
    assert (b.shape[1] if tb else b.shape[0]) == k_dim, (a.shape, b.shape, ta, tb)
    tm = _pick(m_dim, MM_TILES)
    tn = _pick(n_dim, MM_TILES)
    tk = _pick(k_dim, MM_TILES)
    nk = k_dim // tk
    a_spec = pl.BlockSpec((tk, tm), lambda i, j, k: (k, i)) if ta else pl.BlockSpec((tm, tk), lambda i, j, k: (i, k))
    b_spec = pl.BlockSpec((tn, tk), lambda i, j, k: (j, k)) if tb else pl.BlockSpec((tk, tn), lambda i, j, k: (k, j))
    o_spec = pl.BlockSpec((tm, tn), lambda i, j, k: (i, j))
    dims = (((0 if ta else 1,), (1 if tb else 0,)), ((), ()))
    has_add = add is not None
    n_in = 2 + has_add + (after is not None)

    def body(*refs):
        a_ref, b_ref = refs[:2]
        o_ref = refs[n_in]

        def dot():
            return lax.dot_general(a_ref[...].astype(BF16), b_ref[...].astype(BF16), dims,
                                   preferred_element_type=F32)

        def finish(acc):
            if has_add:
                acc = acc + refs[2][...].astype(F32)
            o_ref[...] = acc.astype(o_ref.dtype)

        if nk == 1:
            finish(dot())
            return
        acc_ref = refs[n_in + 1]
        k = pl.program_id(2)

        @pl.when(k == 0)
        def _():
            acc_ref[...] = dot()

        @pl.when((k > 0) & (k < nk - 1))
        def _():
            acc_ref[...] += dot()

        @pl.when(k == nk - 1)
        def _():
            finish(acc_ref[...] + dot())

    return pl.pallas_call(
        body, name=f"{name}_{m_dim}x{n_dim}x{k_dim}",
        out_shape=jax.ShapeDtypeStruct((m_dim, n_dim), out_dtype),
        grid=(m_dim // tm, n_dim // tn, nk),
        in_specs=[a_spec, b_spec] + ([o_spec] if has_add else []) + (
            [] if after is None else [pl.BlockSpec(memory_space=pl.ANY)]),
        out_specs=o_spec,
        scratch_shapes=[] if nk == 1 else [pltpu.VMEM((tm, tn), F32)],
        compiler_params=pltpu.CompilerParams(dimension_semantics=("parallel", "parallel", "arbitrary"),
                                             vmem_limit_bytes=MM_VMEM_BYTES),
    )(*((a, b) + ((add,) if has_add else ()) + (() if after is None else (after,))))


def _me():
    return 4 * lax.axis_index("x") + 2 * lax.axis_index("y") + lax.axis_index("c")


def _peer(j):
    x, y, c = lax.axis_index("x"), lax.axis_index("y"), lax.axis_index("c")
    px = 1 - x if j & 4 else x
    py = 1 - y if j & 2 else y
    pc = 1 - c if j & 1 else c
    return (px, py, pc), 4 * px + 2 * py + pc


def _exchange_body(src_of, dst_ref, send_sems, recv_sems, local_sem):
    me = _me()
    mine = pltpu.make_async_copy(src_of(me), dst_ref.at[me], local_sem)
    mine.start()
    sends = []
    for j in range(1, N_DEV):
        peer, pidx = _peer(j)
        cp = pltpu.make_async_remote_copy(src_ref=src_of(pidx), dst_ref=dst_ref.at[me], send_sem=send_sems.at[j - 1],
                                          recv_sem=recv_sems.at[j - 1], device_id=peer, device_id_type=MESH)
        cp.start()
        sends.append(cp)
    for j in range(1, N_DEV):
        peer, pidx = _peer(j)
        pltpu.make_async_remote_copy(src_ref=src_of(pidx), dst_ref=dst_ref.at[pidx], send_sem=send_sems.at[j - 1],
                                     recv_sem=recv_sems.at[j - 1], device_id=peer, device_id_type=MESH).wait_recv()
    for cp in sends:
        cp.wait_send()
    mine.wait()


_EXCHANGE_SCRATCH = [pltpu.SemaphoreType.DMA((N_DEV - 1,)), pltpu.SemaphoreType.DMA((N_DEV - 1,)),
                     pltpu.SemaphoreType.DMA]


_HBM = pl.BlockSpec(memory_space=pltpu.HBM)
_SEM = pl.BlockSpec(memory_space=pltpu.SEMAPHORE)


def _split_copies(src_ref, gather, land_ref, send_sems, recv_sems):
    me = _me()
    pairs = []
    for j in range(1, N_DEV):
        peer, pidx = _peer(j)

        def make(slot, peer=peer, pidx=pidx, j=j):
            return pltpu.make_async_remote_copy(
                src_ref=src_ref if gather else src_ref.at[pidx], dst_ref=land_ref.at[slot],
                send_sem=send_sems.at[j - 1], recv_sem=recv_sems.at[j - 1], device_id=peer, device_id_type=MESH)

        pairs.append((make(me), make(pidx)))
    return pairs


def exchange_start(src, gather, name, after=None):
    land_shape = ((N_DEV,) + src.shape) if gather else src.shape
    has_after = after is not None

    def body(*refs):
        src_ref, land_ref = refs[:2]
        send_sems, recv_sems = refs[2 + has_after:4 + has_after]
        for send, _ in _split_copies(src_ref, gather, land_ref, send_sems, recv_sems):
            send.start()
        refs[-1][...] = jnp.zeros_like(refs[-1])

    sem = pltpu.SemaphoreType.DMA((N_DEV - 1,))
    send_sems, recv_sems, src_thru, land, token = pl.pallas_call(
        body, name=name,
        out_shape=(sem, sem, pltpu.HBM(src.shape, src.dtype), pltpu.HBM(land_shape, src.dtype),
                   jax.ShapeDtypeStruct((SUBLANES, LANES), F32)),
        in_specs=(_HBM, _HBM) + ((pl.BlockSpec(memory_space=pl.ANY),) if has_after else ()),
        out_specs=(_SEM, _SEM, _HBM, _HBM, pl.BlockSpec(memory_space=pltpu.VMEM)),
        input_output_aliases={0: 2, 1: 3},
        compiler_params=pltpu.CompilerParams(has_side_effects=pltpu.SideEffectType.DATAFLOW_SIDE_EFFECTING),
    )(pltpu.with_memory_space_constraint(src, pltpu.HBM),
      pltpu.with_memory_space_constraint(lax.empty(land_shape, src.dtype), pltpu.HBM),
      *((after,) if has_after else ()))
    return (send_sems, recv_sems, src_thru, land), token


def exchange_wait(handle, after, gather, name):
    send_sems, recv_sems, src_thru, land = handle
    after = tuple(after) if isinstance(after, (tuple, list)) else (after,)

    def body(src_ref, land_ref, send_sems, recv_sems, *rest):
        for _, arrival in _split_copies(src_ref, gather, land_ref, send_sems, recv_sems):
            arrival.wait_send()
            arrival.wait_recv()

    src_done, got = pl.pallas_call(
        body, name=name,
        out_shape=(pltpu.HBM(src_thru.shape, src_thru.dtype), pltpu.HBM(land.shape, land.dtype)),
        in_specs=(_HBM, _HBM, _SEM, _SEM) + (pl.BlockSpec(memory_space=pl.ANY),) * len(after), out_specs=(_HBM, _HBM),
        input_output_aliases={0: 0, 1: 1},
        compiler_params=pltpu.CompilerParams(has_side_effects=pltpu.SideEffectType.DATAFLOW_SIDE_EFFECTING),
    )(src_thru, land, send_sems, recv_sems, *after)
    mine = src_done if gather else lax.dynamic_index_in_dim(src_done, _me(), 0, keepdims=False)
    return lax.dynamic_update_index_in_dim(got, mine, _me(), 0)


def all_gather_sum_small(v, name):
    def body(x_ref, out_ref, sum_ref, send_sems, recv_sems, local_sem):
        _exchange_body(lambda k: x_ref, out_ref, send_sems, recv_sems, local_sem)
        acc = out_ref[0]
        for k in range(1, N_DEV):
            acc = acc + out_ref[k]
        sum_ref[...] = acc

    return pl.pallas_call(
        body, name=name,
        out_shape=(jax.ShapeDtypeStruct((N_DEV,) + v.shape, v.dtype), jax.ShapeDtypeStruct(v.shape, v.dtype)),
        in_specs=[pl.BlockSpec(memory_space=pltpu.VMEM)],
        out_specs=(pl.BlockSpec(memory_space=pltpu.VMEM), pl.BlockSpec(memory_space=pltpu.VMEM)),
        scratch_shapes=list(_EXCHANGE_SCRATCH),
    )(v)


def sum_slots(slots, name):
    _, p_dim, c_dim = slots.shape
    tp = next(tp for tp in range(512, 0, -BF16_ROWS) if p_dim % tp == 0)

    def body(x_ref, o_ref):
        acc = x_ref[0].astype(F32)
        for k in range(1, N_DEV):
            acc = acc + x_ref[k].astype(F32)
        o_ref[...] = acc

    return pl.pallas_call(
        body, name=name,
        out_shape=jax.ShapeDtypeStruct((p_dim, c_dim), F32),
        grid=(p_dim // tp,),
        in_specs=[pl.BlockSpec((N_DEV, tp, c_dim), lambda i: (0, i, 0))],
        out_specs=pl.BlockSpec((tp, c_dim), lambda i: (i, 0)),
        compiler_params=pltpu.CompilerParams(dimension_semantics=("parallel",)),
    )(slots)


def adamw(w, g, m, v, name):
    rows, cols = w.shape
    tr = _pick(rows, (256, 128, 64, 32, 16, 8))

    def body(w_ref, g_ref, m_ref, v_ref, d_ref, nm_ref, nv_ref):
        gv = g_ref[...]
        nm = ADAM_B1 * m_ref[...] + (1.0 - ADAM_B1) * gv
        nv = ADAM_B2 * v_ref[...] + (1.0 - ADAM_B2) * (gv * gv)
        m_hat = nm / (1.0 - ADAM_B1 ** ADAM_STEP)
        v_hat = nv / (1.0 - ADAM_B2 ** ADAM_STEP)
        d_ref[...] = -ADAM_LR * (m_hat / (jnp.sqrt(v_hat) + ADAM_EPS) + ADAM_WD * w_ref[...])
        nm_ref[...] = nm
        nv_ref[...] = nv

    spec = pl.BlockSpec((tr, cols), lambda i: (i, 0))
    shp = jax.ShapeDtypeStruct((rows, cols), F32)
    return pl.pallas_call(
        body, name=name, out_shape=(shp, shp, shp), grid=(rows // tr,),
        in_specs=[spec] * 4, out_specs=(spec,) * 3,
        compiler_params=pltpu.CompilerParams(dimension_semantics=("parallel",)),
    )(w, g, m, v)


ATT_BLK = 128
NEG = -1e30


def _head_sums(v):
    li = lax.broadcasted_iota(jnp.int32, (LANES, LANES), 0) // ATT_HEAD_DIM
    lj = lax.broadcasted_iota(jnp.int32, (LANES, LANES), 1) // ATT_HEAD_DIM
    ones = (li == lj).astype(BF16)
    hi = v.astype(BF16)
    lo = (v - hi.astype(F32)).astype(BF16)
    return jnp.dot(hi, ones, preferred_element_type=F32) + jnp.dot(lo, ones, preferred_element_type=F32)


def _head_col(v, hmask):
    return jnp.max(jnp.where(hmask, v, -jnp.inf), axis=-1, keepdims=True)


def _qk_norm(raw, gain2):
    rstd = lax.rsqrt(_head_sums(raw * raw) * (1.0 / ATT_HEAD_DIM) + NORM_EPS)
    xhat = raw * rstd
    return xhat * gain2, xhat, rstd


def _qk_norm_bwd(dn, xhat, rstd, gain2):
    dxh = dn * gain2
    return rstd * (dxh - xhat * (_head_sums(dxh * xhat) * (1.0 / ATT_HEAD_DIM))), dn * xhat


def _att_mask_bias(n, dilation):
    qi = lax.broadcasted_iota(jnp.int32, (ATT_BLK, 2 * ATT_BLK), 0)
    ki = lax.broadcasted_iota(jnp.int32, (ATT_BLK, 2 * ATT_BLK), 1)
    dist = qi + ATT_BLK - ki
    valid = (dist >= 0) & (dist <= ATT_BLK) & ((n > 0) | (ki >= ATT_BLK))
    return valid, (dilation * dist).astype(F32)


ATT_PAIRS = 8
RELAYOUT_ROWS = 512
RELAYOUT_COLS = 512


def _to_residues(x, dilation, col0=0, cols=None):
    t = x.shape[0]
    cols = x.shape[1] if cols is None else cols
    if dilation == 1 and col0 == 0 and cols == x.shape[1]:
        return x.reshape(1, t, cols)
    tr = _pick(t, (RELAYOUT_ROWS,))
    tc = _pick(cols, (RELAYOUT_COLS, 256, 128))
    per = tr // dilation
    assert tr % dilation == 0 and col0 % tc == 0

    def body(x_ref, o_ref, s_ref):
        for c in range(tc // LANES):
            lanes = slice(c * LANES, (c + 1) * LANES)
            s_ref[c] = x_ref[:, lanes].astype(F32)
            for r in range(dilation):
                o_ref[r, :, lanes] = s_ref[c, pl.ds(r, per, stride=dilation), :].astype(o_ref.dtype)

    return pl.pallas_call(
        body, name=f"to_residues_{dilation}", out_shape=jax.ShapeDtypeStruct((dilation, t // dilation, cols), x.dtype),
        grid=(t // tr, cols // tc),
        in_specs=[pl.BlockSpec((tr, tc), lambda i, j: (i, col0 // tc + j))],
        out_specs=pl.BlockSpec((dilation, per, tc), lambda i, j: (0, i, j)),
        scratch_shapes=[pltpu.VMEM((tc // LANES, tr, LANES), F32)],
        compiler_params=pltpu.CompilerParams(dimension_semantics=("parallel", "parallel")),
    )(x)


def _from_residues(y):
    dilation, lu, cols = y.shape
    t = dilation * lu
    if dilation == 1:
        return y.reshape(t, cols)
    tr = _pick(t, (RELAYOUT_ROWS,))
    tc = _pick(cols, (RELAYOUT_COLS, 256, 128))
    per = tr // dilation

    def body(y_ref, o_ref, s_ref):
        for c in range(tc // LANES):
            lanes = slice(c * LANES, (c + 1) * LANES)
            for r in range(dilation):
                s_ref[c, pl.ds(r, per, stride=dilation), :] = y_ref[r, :, lanes].astype(F32)
            o_ref[:, lanes] = s_ref[c].astype(o_ref.dtype)

    return pl.pallas_call(
        body, name=f"from_residues_{dilation}", out_shape=jax.ShapeDtypeStruct((t, cols), y.dtype),
        grid=(t // tr, cols // tc),
        in_specs=[pl.BlockSpec((dilation, per, tc), lambda i, j: (0, i, j))],
        out_specs=pl.BlockSpec((tr, tc), lambda i, j: (i, j)),
        scratch_shapes=[pltpu.VMEM((tc // LANES, tr, LANES), F32)],
        compiler_params=pltpu.CompilerParams(dimension_semantics=("parallel", "parallel")),
    )(y)


def _att_specs(hd, v_base, nb, pp):
    width = pp * LANES
    assert hd % width == 0 and v_base % width == 0

    def spec(base, shift):
        def imap(r, hp, n):
            row = jnp.minimum(n, nb - 1) if shift == 0 else jnp.maximum(n - 1, 0)
            return (r, row, base // width + hp)
        return pl.BlockSpec((None, ATT_BLK, width), imap)

    return [spec(0, 0), spec(hd, 1), spec(hd, 0), spec(v_base, 1), spec(v_base, 0)]


DELTA_LANE = 64


def _att_pre(qkv, g, dilation, gq2, gk2):
    t, width = qkv.shape
    hd = width // (3 * len(DIL_PATTERNS))
    tr = _pick(t, (RELAYOUT_ROWS,))
    tc = _pick(hd, (RELAYOUT_COLS, 256, 128))
    per = tr // dilation
    assert tr % dilation == 0 and (g * 3 * hd) % tc == 0

    def body(x_ref, gq_ref, gk_ref, o_ref, s_ref):
        gain = jnp.where(pl.program_id(1) * tc < hd, gq_ref[0:1, :], gk_ref[0:1, :])
        for c in range(tc // LANES):
            lanes = slice(c * LANES, (c + 1) * LANES)
            s_ref[c] = _qk_norm(x_ref[:, lanes].astype(F32), gain)[0]
            for r in range(dilation):
                o_ref[r, :, lanes] = s_ref[c, pl.ds(r, per, stride=dilation), :].astype(o_ref.dtype)

    vec_spec = pl.BlockSpec((SUBLANES, LANES), lambda i, j: (0, 0))
    return pl.pallas_call(
        body, name=f"att_pre_g{g}", out_shape=jax.ShapeDtypeStruct((dilation, t // dilation, 2 * hd), qkv.dtype),
        grid=(t // tr, 2 * hd // tc),
        in_specs=[pl.BlockSpec((tr, tc), lambda i, j: (i, g * 3 * hd // tc + j)), vec_spec, vec_spec],
        out_specs=pl.BlockSpec((dilation, per, tc), lambda i, j: (0, i, j)),
        scratch_shapes=[pltpu.VMEM((tc // LANES, tr, LANES), F32)],
        compiler_params=pltpu.CompilerParams(dimension_semantics=("parallel", "parallel")),
    )(qkv, gq2, gk2)


def _att_group_fwd(view, hd, slopes, g):
    qk_r, v_r, v_base = view
    dilation, lu, _ = qk_r.shape
    nb = lu // ATT_BLK
    assert nb * ATT_BLK == lu and hd % LANES == 0
    hpn = hd // LANES
    pp = math.gcd(ATT_PAIRS, hpn)
    scale = 1.0 / math.sqrt(ATT_HEAD_DIM)

    def body(q_ref, kp_ref, kc_ref, vp_ref, vc_ref, sl_ref, o_ref, l_ref):
        n = pl.program_id(2)
        lane = lax.broadcasted_iota(jnp.int32, (1, LANES), 1)
        first = (lane // ATT_HEAD_DIM) == 0
        valid, dist = _att_mask_bias(n, dilation)
        stats = jnp.zeros((ATT_BLK, LANES), F32)
        for pair in range(pp):
            cols = slice(pair * LANES, (pair + 1) * LANES)
            qn = q_ref[:, cols]
            kn16 = jnp.concatenate([kp_ref[:, cols], kc_ref[:, cols]], axis=0)
            v16 = jnp.concatenate([vp_ref[:, cols], vc_ref[:, cols]], axis=0)
            outs = []
            for hh in range(2):
                hmask = (lane // ATT_HEAD_DIM) == hh
                qh = jnp.where(hmask, qn, jnp.zeros_like(qn))
                s = lax.dot_general(qh, kn16, (((1,), (1,)), ((), ())), preferred_element_type=F32) * scale
                slope = _head_col(sl_ref[pair, 0:1, :], hmask)
                logits = jnp.where(valid, s - slope * dist, NEG)
                mx = jnp.max(logits, axis=-1, keepdims=True)
                pexp = jnp.exp(logits - mx)
                den = jnp.sum(pexp, axis=-1, keepdims=True)
                outs.append(jnp.dot(pexp.astype(BF16), v16, preferred_element_type=F32) / den)
                stats = jnp.where(lane == 2 * pair + hh, mx + jnp.log(den), stats)
            o_ref[:, cols] = jnp.where(first, outs[0], outs[1]).astype(BF16)
        l_ref[...] = stats

    out_spec = pl.BlockSpec((None, ATT_BLK, pp * LANES), lambda r, hp, n: (r, n, hp))
    stat_spec = pl.BlockSpec((None, ATT_BLK, LANES), lambda r, hp, n: (r, n, hp))
    o, lse = pl.pallas_call(
        body, name=f"att_fwd_g{g}",
        out_shape=(jax.ShapeDtypeStruct((dilation, lu, hd), BF16),
                   jax.ShapeDtypeStruct((dilation, lu, hpn // pp * LANES), F32)),
        grid=(dilation, hpn // pp, nb),
        in_specs=_att_specs(hd, v_base, nb, pp) + [pl.BlockSpec((pp, SUBLANES, LANES), lambda r, hp, n: (hp, 0, 0))],
        out_specs=(out_spec, stat_spec),
        compiler_params=pltpu.CompilerParams(dimension_semantics=("parallel", "parallel", "arbitrary")),
    )(qk_r, qk_r, qk_r, v_r, v_r, slopes)
    return _from_residues(o), _from_residues(lse)


def _att_merge(outs, lses):
    t, hd = outs[0].shape
    sw = lses[0].shape[1]
    pp = hd // sw
    tr = _pick(t, (512, 256, 128))
    ng = len(outs)

    def body(*refs):
        o_refs, l_refs, o16_ref, lt_ref = refs[:ng], refs[ng:2 * ng], refs[2 * ng], refs[2 * ng + 1]
        lane = lax.broadcasted_iota(jnp.int32, (1, LANES), 1)
        first = (lane // ATT_HEAD_DIM) == 0
        for blk in range(sw // LANES):
            scols = slice(blk * LANES, (blk + 1) * LANES)
            stats = jnp.zeros((tr, LANES), F32)
            for pair in range(pp):
                cols = slice((blk * pp + pair) * LANES, (blk * pp + pair + 1) * LANES)
                weights = []
                for hh in range(2):
                    pick = lane == 2 * pair + hh
                    ls = [_head_col(r[:, scols], pick) for r in l_refs]
                    mx = functools.reduce(jnp.maximum, ls)
                    es = [jnp.exp(l - mx) for l in ls]
                    den = functools.reduce(jnp.add, es)
                    weights.append([e / den for e in es])
                    stats = jnp.where(pick, mx + jnp.log(den), stats)
                acc = jnp.zeros((tr, LANES), F32)
                for gi in range(ng):
                    acc = acc + jnp.where(first, weights[0][gi], weights[1][gi]) * o_refs[gi][:, cols].astype(F32)
                o16_ref[:, cols] = acc.astype(BF16)
            lt_ref[:, scols] = stats

    spec = pl.BlockSpec((tr, hd), lambda i: (i, 0))
    sspec = pl.BlockSpec((tr, sw), lambda i: (i, 0))
    return pl.pallas_call(
        body, name="att_merge",
        out_shape=(jax.ShapeDtypeStruct((t, hd), BF16), jax.ShapeDtypeStruct((t, sw), F32)), grid=(t // tr,),
        in_specs=[spec] * ng + [sspec] * ng, out_specs=(spec, sspec),
        compiler_params=pltpu.CompilerParams(dimension_semantics=("parallel",)),
    )(*outs, *lses)


def _att_bwd_prep(do, o16, lse_tot):
    t, hd = do.shape
    sw = lse_tot.shape[1]
    pp = hd // sw
    tr = _pick(t, (256, 128))

    def body(do_ref, o_ref, l_ref, d16_ref, st_ref):
        lane = lax.broadcasted_iota(jnp.int32, (1, LANES), 1)
        d16_ref[...] = do_ref[...].astype(BF16)
        for blk in range(sw // LANES):
            scols = slice(blk * LANES, (blk + 1) * LANES)
            stats = l_ref[:, scols]
            for pair in range(pp):
                cols = slice((blk * pp + pair) * LANES, (blk * pp + pair + 1) * LANES)
                prod = do_ref[:, cols] * o_ref[:, cols].astype(F32)
                for hh in range(2):
                    hmask = (lane // ATT_HEAD_DIM) == hh
                    delta = jnp.sum(jnp.where(hmask, prod, 0.0), axis=-1, keepdims=True)
                    stats = jnp.where(lane == DELTA_LANE + 2 * pair + hh, delta, stats)
            st_ref[:, scols] = stats

    spec = pl.BlockSpec((tr, hd), lambda i: (i, 0))
    sspec = pl.BlockSpec((tr, sw), lambda i: (i, 0))
    return pl.pallas_call(
        body, name="att_bwd_prep",
        out_shape=(jax.ShapeDtypeStruct((t, hd), BF16), jax.ShapeDtypeStruct((t, sw), F32)), grid=(t // tr,),
        in_specs=[spec, spec, sspec], out_specs=(spec, sspec),
        compiler_params=pltpu.CompilerParams(dimension_semantics=("parallel",)),
    )(do, o16, lse_tot)


def _att_post(buf, parts, qkv, gq2, gk2, g):
    dilation, lu, hd = parts[0].shape
    t = dilation * lu
    tr = _pick(t, (RELAYOUT_ROWS,))
    per = tr // dilation

    def body(*refs):
        raw_ref, gq_ref, gk_ref = refs[3:6]
        o_ref, dgq_ref, dgk_ref, s_ref = refs[-4:]

        @pl.when(pl.program_id(0) == 0)
        def _():
            dgq_ref[...] = jnp.zeros_like(dgq_ref)
            dgk_ref[...] = jnp.zeros_like(dgk_ref)

        for sec, y_ref in enumerate(refs[:3]):
            gsum = jnp.zeros((1, LANES), F32)
            for c in range(hd // LANES):
                lanes = slice(c * LANES, (c + 1) * LANES)
                out_lanes = slice(sec * hd + c * LANES, sec * hd + (c + 1) * LANES)
                for r in range(dilation):
                    s_ref[pl.ds(r, per, stride=dilation), :] = y_ref[r, :, lanes].astype(F32)
                d = s_ref[...]
                if sec < 2:
                    gain = (gq_ref if sec == 0 else gk_ref)[0:1, :]
                    _, xhat, rstd = _qk_norm(raw_ref[:, out_lanes].astype(F32), gain)
                    d, part = _qk_norm_bwd(d, xhat, rstd, gain)
                    gsum = gsum + jnp.sum(part, axis=0, keepdims=True)
                o_ref[:, out_lanes] = d.astype(o_ref.dtype)
            if sec < 2:
                acc = dgq_ref if sec == 0 else dgk_ref
                acc[...] += jnp.broadcast_to(gsum, acc.shape)

    part_spec = pl.BlockSpec((dilation, per, hd), lambda i: (0, i, 0))
    slab_spec = pl.BlockSpec((tr, 3 * hd), lambda i: (i, g))
    vec_spec = pl.BlockSpec((SUBLANES, LANES), lambda i: (0, 0))
    vec_shape = jax.ShapeDtypeStruct((SUBLANES, LANES), F32)
    return pl.pallas_call(
        body, name=f"att_post_g{g}", out_shape=(jax.ShapeDtypeStruct(qkv.shape, qkv.dtype), vec_shape, vec_shape),
        grid=(t // tr,),
        in_specs=[part_spec] * 3 + [slab_spec, vec_spec, vec_spec] + (
            [] if buf is None else [pl.BlockSpec(memory_space=pl.ANY)]),
        out_specs=(slab_spec, vec_spec, vec_spec),
        scratch_shapes=[pltpu.VMEM((tr, LANES), F32)],
        input_output_aliases={} if buf is None else {6: 0},
        compiler_params=pltpu.CompilerParams(dimension_semantics=("arbitrary",)),
    )(*parts, qkv, gq2, gk2, *(() if buf is None else (buf,)))


def _att_group_bwd(view, hd, slopes, stats, do16, g):
    qk_r, v_r, v_base = view
    dilation, lu, _ = qk_r.shape
    nb = lu // ATT_BLK
    hpn = hd // LANES
    pp = math.gcd(ATT_PAIRS, hpn)
    hbn = hpn // pp
    scale = 1.0 / math.sqrt(ATT_HEAD_DIM)

    def body(q_ref, kp_ref, kc_ref, vp_ref, vc_ref, sl_ref, st_ref, do_ref, dq_ref, dk_ref, dv_ref, ck_ref, cv_ref):
        n = pl.program_id(2)
        lane = lax.broadcasted_iota(jnp.int32, (1, LANES), 1)

        @pl.when(n == 0)
        def _():
            ck_ref[...] = jnp.zeros_like(ck_ref)
            cv_ref[...] = jnp.zeros_like(cv_ref)

        @pl.when(n < nb)
        def _():
            valid, dist = _att_mask_bias(n, dilation)
            stats = st_ref[...]
            for pair in range(pp):
                cols = slice(pair * LANES, (pair + 1) * LANES)
                qn = q_ref[:, cols]
                kn16 = jnp.concatenate([kp_ref[:, cols], kc_ref[:, cols]], axis=0)
                v16 = jnp.concatenate([vp_ref[:, cols], vc_ref[:, cols]], axis=0)
                dov = do_ref[:, cols]
                dq_acc = jnp.zeros((ATT_BLK, LANES), F32)
                dk_acc = jnp.zeros((2 * ATT_BLK, LANES), F32)
                dv_acc = jnp.zeros((2 * ATT_BLK, LANES), F32)
                for hh in range(2):
                    hmask = (lane // ATT_HEAD_DIM) == hh
                    qh = jnp.where(hmask, qn, jnp.zeros_like(qn))
                    doh = jnp.where(hmask, dov, jnp.zeros_like(dov))
                    s = lax.dot_general(qh, kn16, (((1,), (1,)), ((), ())), preferred_element_type=F32) * scale
                    slope = _head_col(sl_ref[pair, 0:1, :], hmask)
                    lse = _head_col(stats, lane == 2 * pair + hh)
                    delta = _head_col(stats, lane == DELTA_LANE + 2 * pair + hh)
                    pr = jnp.exp(jnp.where(valid, s - slope * dist - lse, NEG))
                    dp = lax.dot_general(doh, v16, (((1,), (1,)), ((), ())), preferred_element_type=F32)
                    ds = (pr * (dp - delta) * scale).astype(BF16)
                    dq_acc = dq_acc + jnp.where(hmask, jnp.dot(ds, kn16, preferred_element_type=F32), 0.0)
                    dk_acc = dk_acc + lax.dot_general(ds, qh, (((0,), (0,)), ((), ())), preferred_element_type=F32)
                    dv_acc = dv_acc + lax.dot_general(pr.astype(BF16), doh, (((0,), (0,)), ((), ())),
                                                      preferred_element_type=F32)
                dq_ref[:, cols] = dq_acc.astype(dq_ref.dtype)
                dk_ref[:, cols] = (ck_ref[:, cols] + dk_acc[:ATT_BLK]).astype(dk_ref.dtype)
                dv_ref[:, cols] = (cv_ref[:, cols] + dv_acc[:ATT_BLK]).astype(dv_ref.dtype)
                ck_ref[:, cols] = dk_acc[ATT_BLK:]
                cv_ref[:, cols] = dv_acc[ATT_BLK:]

        @pl.when(n == nb)
        def _():
            dk_ref[...] = ck_ref[...].astype(dk_ref.dtype)
            dv_ref[...] = cv_ref[...].astype(dv_ref.dtype)

    width = pp * LANES
    q_out = pl.BlockSpec((None, ATT_BLK, width), lambda r, hp, n: (r, jnp.minimum(n, nb - 1), hp))
    kv_out = pl.BlockSpec((None, ATT_BLK, width), lambda r, hp, n: (r, jnp.maximum(n - 1, 0), hp))
    st_spec = pl.BlockSpec((None, ATT_BLK, LANES), lambda r, hp, n: (r, jnp.minimum(n, nb - 1), hp))
    shp = jax.ShapeDtypeStruct((dilation, lu, hd), BF16)
    return pl.pallas_call(
        body, name=f"att_bwd_g{g}", out_shape=(shp, shp, shp), grid=(dilation, hbn, nb + 1),
        in_specs=_att_specs(hd, v_base, nb, pp) + [
            pl.BlockSpec((pp, SUBLANES, LANES), lambda r, hp, n: (hp, 0, 0)), st_spec, q_out],
        out_specs=(q_out, kv_out, kv_out),
        scratch_shapes=[pltpu.VMEM((ATT_BLK, width), F32), pltpu.VMEM((ATT_BLK, width), F32)],
        compiler_params=pltpu.CompilerParams(dimension_semantics=("parallel", "parallel", "arbitrary")),
    )(qk_r, qk_r, qk_r, v_r, v_r, slopes, _to_residues(stats, dilation), _to_residues(do16, dilation))


def _att_consts(q_gain, k_gain, hd):
    heads = hd // ATT_HEAD_DIM
    gq2 = jnp.broadcast_to(jnp.tile(q_gain, 2)[None], (SUBLANES, LANES))
    gk2 = jnp.broadcast_to(jnp.tile(k_gain, 2)[None], (SUBLANES, LANES))
    sl = 2.0 ** (-8.0 * jnp.arange(1, heads + 1, dtype=F32) / heads)
    slopes = jnp.broadcast_to(jnp.repeat(sl, ATT_HEAD_DIM).reshape(hd // LANES, 1, LANES), (hd // LANES, SUBLANES, LANES))
    return gq2, gk2, slopes


def _attention_core_fwd(qkv, q_gain, k_gain):
    hd = qkv.shape[1] // (3 * len(DIL_PATTERNS))
    gq2, gk2, slopes = _att_consts(q_gain, k_gain, hd)
    outs, lses, views = [], [], []
    for g, (_, dilation) in enumerate(DIL_PATTERNS):
        v_col = (3 * g + 2) * hd
        view = (_att_pre(qkv, g, dilation, gq2, gk2),) + (
            (_to_residues(qkv, 1), v_col) if dilation == 1 else (_to_residues(qkv, dilation, v_col, hd), 0))
        o_g, l_g = _att_group_fwd(view, hd, slopes, g)
        outs.append(o_g)
        lses.append(l_g)
        views.append(view)
    o16, lse_tot = _att_merge(outs, lses)
    return o16, (qkv, views, q_gain, k_gain, o16, lse_tot)


def _attention_core_bwd(res, do):
    qkv, views, q_gain, k_gain, o16, lse_tot = res
    hd = o16.shape[1]
    gq2, gk2, slopes = _att_consts(q_gain, k_gain, hd)
    do16, stats = _att_bwd_prep(do, o16, lse_tot)
    dqkv, dgq, dgk = None, 0.0, 0.0
    for g, view in enumerate(views):
        parts = _att_group_bwd(view, hd, slopes, stats, do16, g)
        dqkv, a, b = _att_post(dqkv, parts, qkv, gq2, gk2, g)
        dgq = dgq + a[0].reshape(-1, ATT_HEAD_DIM).sum(0)
        dgk = dgk + b[0].reshape(-1, ATT_HEAD_DIM).sum(0)
    return dqkv, dgq, dgk


HALO = 8


def _silu(v):
    return v * jax.nn.sigmoid(v)


def _silu_grad(v):
    s = jax.nn.sigmoid(v)
    return s * (1.0 + v * (1.0 - s))


def _halo_rows(dtype):
    return BF16_ROWS if dtype == BF16 else HALO


def _conv_fwd(zx, conv_w, conv_b, d_inner):
    t = zx.shape[0]
    conv_dim = conv_w.shape[1]
    cb = _pick(d_inner, (1024, 512, 256, 128))
    assert conv_dim % cb == 0
    tr = _pick(t, (512, 256, 128))
    off = d_inner // cb
    hx = _halo_rows(zx.dtype)

    def body(x_ref, h_ref, w_ref, b_ref, o_ref):
        i = pl.program_id(1)
        halo = jnp.where(i > 0, h_ref[...].astype(F32), 0.0)
        ext = jnp.concatenate([halo, x_ref[...].astype(F32)], axis=0)
        acc = jnp.broadcast_to(b_ref[...], (tr, cb))
        for k in range(CONV_WIDTH):
            s = CONV_WIDTH - 1 - k
            sh = ext if s == 0 else pltpu.roll(ext, shift=s, axis=0)
            acc = acc + w_ref[k:k + 1, :] * sh[hx:hx + tr]
        o_ref[...] = acc.astype(o_ref.dtype)

    return pl.pallas_call(
        body, name="ssm_conv_fwd", out_shape=jax.ShapeDtypeStruct((t, conv_dim), BF16),
        grid=(conv_dim // cb, t // tr),
        in_specs=[pl.BlockSpec((tr, cb), lambda j, i: (i, off + j)),
                  pl.BlockSpec((hx, cb), lambda j, i: (jnp.maximum(i * (tr // hx) - 1, 0), off + j)),
                  pl.BlockSpec((CONV_WIDTH, cb), lambda j, i: (0, j)),
                  pl.BlockSpec((1, cb), lambda j, i: (0, j))],
        out_specs=pl.BlockSpec((tr, cb), lambda j, i: (i, j)),
        compiler_params=pltpu.CompilerParams(dimension_semantics=("parallel", "parallel")),
    )(zx, zx, conv_w, conv_b.reshape(1, -1))


def _conv_bwd(zx, conv_w, dpre, dzx, d_inner, col0):
    t, width = zx.shape
    conv_dim = dpre.shape[1]
    cb = _pick(conv_dim, (1024, 512, 256, 128))
    assert (d_inner + col0) % cb == 0
    tr = _pick(t, (512, 256, 128))
    off = (d_inner + col0) // cb
    woff = col0 // cb
    nr = t // tr
    hx = _halo_rows(zx.dtype)
    hd = _halo_rows(dpre.dtype)

    def body(x_ref, h_ref, w_ref, d_ref, dn_ref, dzx_in, dx_ref, dw_ref, db_ref):
        i = pl.program_id(1)

        @pl.when(i == 0)
        def _():
            dw_ref[...] = jnp.zeros_like(dw_ref)
            db_ref[...] = jnp.zeros_like(db_ref)

        halo = jnp.where(i > 0, h_ref[...].astype(F32), 0.0)
        ext = jnp.concatenate([halo, x_ref[...].astype(F32)], axis=0)
        d = d_ref[...].astype(F32)
        dext = jnp.concatenate([d, jnp.where(i < nr - 1, dn_ref[...].astype(F32), 0.0)], axis=0)
        dx = jnp.zeros((tr, cb), F32)
        for k in range(CONV_WIDTH):
            s = CONV_WIDTH - 1 - k
            fut = dext if s == 0 else pltpu.roll(dext, shift=tr + hd - s, axis=0)
            dx = dx + w_ref[k:k + 1, :] * fut[:tr]
            past = ext if s == 0 else pltpu.roll(ext, shift=s, axis=0)
            dw_ref[k:k + 1, :] += jnp.sum(d * past[hx:hx + tr], axis=0, keepdims=True)
        dx_ref[...] = dx.astype(dx_ref.dtype)
        db_ref[...] += jnp.sum(d, axis=0, keepdims=True)

    last_halo = t // hd - 1
    return pl.pallas_call(
        body, name=f"ssm_conv_bwd_{col0}",
        out_shape=(jax.ShapeDtypeStruct(dzx.shape, dzx.dtype), jax.ShapeDtypeStruct((CONV_WIDTH, conv_dim), F32),
                   jax.ShapeDtypeStruct((1, conv_dim), F32)),
        grid=(conv_dim // cb, nr),
        in_specs=[pl.BlockSpec((tr, cb), lambda j, i: (i, off + j)),
                  pl.BlockSpec((hx, cb), lambda j, i: (jnp.maximum(i * (tr // hx) - 1, 0), off + j)),
                  pl.BlockSpec((CONV_WIDTH, cb), lambda j, i: (0, woff + j)),
                  pl.BlockSpec((tr, cb), lambda j, i: (i, j)),
                  pl.BlockSpec((hd, cb), lambda j, i: (jnp.minimum((i + 1) * (tr // hd), last_halo), j)),
                  pl.BlockSpec(memory_space=pl.ANY)],
        out_specs=(pl.BlockSpec((tr, cb), lambda j, i: (i, off + j)),
                   pl.BlockSpec((CONV_WIDTH, cb), lambda j, i: (0, j)),
                   pl.BlockSpec((1, cb), lambda j, i: (0, j))),
        input_output_aliases={5: 0},
        compiler_params=pltpu.CompilerParams(dimension_semantics=("parallel", "arbitrary")),
    )(zx, zx, conv_w, dpre, dpre, dzx)


def _eye(n):
    return lax.broadcasted_iota(jnp.int32, (n, n), 0) == lax.broadcasted_iota(jnp.int32, (n, n), 1)


def _row_to_col(row):
    n = row.shape[1]
    return jnp.sum(jnp.where(_eye(n), row, 0.0), axis=1, keepdims=True)


def _col_to_row(col):
    n = col.shape[0]
    return jnp.sum(jnp.where(_eye(n), col, 0.0), axis=0, keepdims=True)


def _pair_lanes(c0, c1):
    lane = lax.broadcasted_iota(jnp.int32, (1, LANES), 1)
    return jnp.where(lane < SSM_HEAD_DIM, c0, c1)


def _ssd_chunk_common(pre_x_ref, pre_b_ref, pre_c_ref, dtr_ref, bias_ref, alog_ref, cs_ref):
    cl = SSD_CHUNK
    hpg = dtr_ref.shape[0]
    x = _silu(pre_x_ref[...].astype(F32))
    b16 = _silu(pre_b_ref[...].astype(F32)).astype(BF16)
    c16 = _silu(pre_c_ref[...].astype(F32)).astype(BF16)
    dt = jax.nn.softplus(dtr_ref[...] + bias_ref[...])
    a = -jnp.exp(alog_ref[...])
    li = lax.broadcasted_iota(jnp.int32, (cl, cl), 0)
    si = lax.broadcasted_iota(jnp.int32, (cl, cl), 1)
    upper = (li <= si).astype(F32)
    cs_ref[0:hpg, :] = jnp.dot(dt * a, upper, precision=lax.Precision.HIGHEST, preferred_element_type=F32)
    cs_ref[hpg:2 * hpg, :] = dt
    g = lax.dot_general(c16, b16, (((1,), (1,)), ((), ())), preferred_element_type=F32)
    return x, b16, c16, dt, a, g, li >= si


def _ssd_fwd(pre, dtT, bias, alog, dskip_lanes, d_inner):
    t = pre.shape[0]
    cl = SSD_CHUNK
    nc = t // cl
    ng = SSM_GROUPS
    hpg = dtT.shape[1]
    gw = hpg * SSM_HEAD_DIM
    assert d_inner == ng * gw and hpg % 2 == 0
    bo = d_inner // SSM_STATE

    def body(px_ref, pb_ref, pc_ref, dtr_ref, bias_ref, alog_ref, dsk_ref, y_ref, st_ref, s_ref, cs_ref):
        c = pl.program_id(1)

        @pl.when(c == 0)
        def _():
            s_ref[...] = jnp.zeros_like(s_ref)

        x, b16, c16, dt, a, g, causal = _ssd_chunk_common(px_ref, pb_ref, pc_ref, dtr_ref, bias_ref, alog_ref, cs_ref)
        st_ref[...] = s_ref[...]
        yoff = lax.dot_general(c16, s_ref[...].astype(BF16), (((1,), (1,)), ((), ())), preferred_element_type=F32)
        xde_parts = []
        for j in range(hpg // 2):
            cols = slice(j * LANES, (j + 1) * LANES)
            xp = x[:, cols]
            dcol, ecol, ocol, ms = [], [], [], []
            for hh in range(2):
                h = 2 * j + hh
                cs_row = cs_ref[h:h + 1, :]
                cs_col = _row_to_col(cs_row)
                dcol.append(_row_to_col(cs_ref[hpg + h:hpg + h + 1, :]))
                ecol.append(jnp.exp(cs_ref[h:h + 1, cl - 1:cl] - cs_col))
                ocol.append(jnp.exp(cs_col))
                lm = jnp.where(causal, jnp.exp(jnp.minimum(cs_col - cs_row, 0.0)), 0.0)
                ms.append((g * lm).astype(BF16))
            xd = xp * _pair_lanes(dcol[0], dcol[1])
            xd16 = xd.astype(BF16)
            yd = _pair_lanes(1.0, 0.0) * jnp.dot(ms[0], xd16, preferred_element_type=F32) \
                + _pair_lanes(0.0, 1.0) * jnp.dot(ms[1], xd16, preferred_element_type=F32)
            y_ref[:, cols] = yd + yoff[:, cols] * _pair_lanes(ocol[0], ocol[1]) + xp * dsk_ref[0:1, cols]
            xde_parts.append((xd * _pair_lanes(ecol[0], ecol[1])).astype(BF16))
        new = lax.dot_general(jnp.concatenate(xde_parts, axis=1), b16, (((0,), (0,)), ((), ())),
                              preferred_element_type=F32)
        for h in range(hpg):
            rows = slice(h * SSM_HEAD_DIM, (h + 1) * SSM_HEAD_DIM)
            s_ref[rows, :] = s_ref[rows, :] * jnp.exp(cs_ref[h:h + 1, cl - 1:cl]) + new[rows, :]

    vec = lambda n: pl.BlockSpec((None, hpg, n), lambda gi, c: (gi, 0, 0))
    return pl.pallas_call(
        body, name="ssd_fwd",
        out_shape=(jax.ShapeDtypeStruct((t, d_inner), F32), jax.ShapeDtypeStruct((ng, nc, gw, SSM_STATE), F32)),
        grid=(ng, nc),
        in_specs=[pl.BlockSpec((cl, gw), lambda gi, c: (c, gi)),
                  pl.BlockSpec((cl, SSM_STATE), lambda gi, c: (c, bo + gi)),
                  pl.BlockSpec((cl, SSM_STATE), lambda gi, c: (c, bo + ng + gi)),
                  pl.BlockSpec((None, hpg, cl), lambda gi, c: (gi, 0, c)),
                  vec(1), vec(1),
                  pl.BlockSpec((1, gw), lambda gi, c: (0, gi))],
        out_specs=(pl.BlockSpec((cl, gw), lambda gi, c: (c, gi)),
                   pl.BlockSpec((None, None, gw, SSM_STATE), lambda gi, c: (gi, c, 0, 0))),
        scratch_shapes=[pltpu.VMEM((gw, SSM_STATE), F32), pltpu.VMEM((2 * hpg, cl), F32)],
        compiler_params=pltpu.CompilerParams(dimension_semantics=("parallel", "arbitrary")),
    )(pre, pre, pre, dtT, bias, alog, dskip_lanes)


def _ssd_bwd(pre, dtT, bias, alog, dskip_lanes, states, dy, d_inner):
    t, conv_dim = pre.shape
    cl = SSD_CHUNK
    nc = t // cl
    ng = SSM_GROUPS
    hpg = dtT.shape[1]
    gw = hpg * SSM_HEAD_DIM
    bo = d_inner // SSM_STATE

    def body(px_ref, pb_ref, pc_ref, dtr_ref, bias_ref, alog_ref, dsk_ref, st_ref, dy_ref,
             dx_ref, db_ref, dc_ref, ddt_ref, acc_ref, dsk_out, ds_ref, cs_ref, dcs_ref):
        c = pl.program_id(1)

        @pl.when(c == 0)
        def _():
            ds_ref[...] = jnp.zeros_like(ds_ref)
            acc_ref[...] = jnp.zeros_like(acc_ref)
            dsk_out[...] = jnp.zeros_like(dsk_out)

        x, b16, c16, dt, a, g, causal = _ssd_chunk_common(px_ref, pb_ref, pc_ref, dtr_ref, bias_ref, alog_ref, cs_ref)
        s_prev = st_ref[...]
        s16 = s_prev.astype(BF16)
        ds = ds_ref[...]
        ds16 = ds.astype(BF16)
        dyv = dy_ref[...]
        yoff = lax.dot_general(c16, s16, (((1,), (1,)), ((), ())), preferred_element_type=F32)
        bds = lax.dot_general(b16, ds16, (((1,), (1,)), ((), ())), preferred_element_type=F32)
        dg = jnp.zeros((cl, cl), F32)
        xde_parts, dye_parts = [], []
        lane = lax.broadcasted_iota(jnp.int32, (1, LANES), 1)
        for j in range(hpg // 2):
            cols = slice(j * LANES, (j + 1) * LANES)
            xp, dyp = x[:, cols], dyv[:, cols]
            dcol, ecol, ocol, lms = [], [], [], []
            for hh in range(2):
                h = 2 * j + hh
                cs_row = cs_ref[h:h + 1, :]
                cs_col = _row_to_col(cs_row)
                dcol.append(_row_to_col(cs_ref[hpg + h:hpg + h + 1, :]))
                ecol.append(jnp.exp(cs_ref[h:h + 1, cl - 1:cl] - cs_col))
                ocol.append(jnp.exp(cs_col))
                lms.append(jnp.where(causal, jnp.exp(jnp.minimum(cs_col - cs_row, 0.0)), 0.0))
            dlanes, elanes, olanes = _pair_lanes(*dcol), _pair_lanes(*ecol), _pair_lanes(*ocol)
            xd = xp * dlanes
            xd16 = xd.astype(BF16)
            xde = xd * elanes
            yoffp = yoff[:, cols] * olanes
            bdsp = bds[:, cols]
            dxd = bdsp * elanes
            for hh in range(2):
                h = 2 * j + hh
                hmask = (lane // SSM_HEAD_DIM) == hh
                dyh16 = jnp.where(hmask, dyp, 0.0).astype(BF16)
                m = g * lms[hh]
                dm = lax.dot_general(dyh16, xd16, (((1,), (1,)), ((), ())), preferred_element_type=F32)
                w = dm * m
                dg = dg + dm * lms[hh]
                dxd = dxd + lax.dot_general(m.astype(BF16), dyh16, (((0,), (0,)), ((), ())),
                                            preferred_element_type=F32)
                term = jnp.sum(jnp.where(hmask, xde * bdsp, 0.0), axis=1, keepdims=True)
                dcs_col = (jnp.sum(w, axis=1, keepdims=True)
                           + jnp.sum(jnp.where(hmask, dyp * yoffp, 0.0), axis=1, keepdims=True) - term)
                rows = slice(h * SSM_HEAD_DIM, (h + 1) * SSM_HEAD_DIM)
                dec = jnp.exp(cs_ref[h:h + 1, cl - 1:cl])
                tail = jnp.sum(term, axis=0, keepdims=True) + dec * jnp.sum(
                    jnp.sum(s_prev[rows, :] * ds[rows, :], axis=1, keepdims=True), axis=0, keepdims=True)
                last = lax.broadcasted_iota(jnp.int32, (1, cl), 1) == cl - 1
                dcs_ref[h:h + 1, :] = _col_to_row(dcs_col) - jnp.sum(w, axis=0, keepdims=True) + jnp.where(last, tail, 0.0)
                dcs_ref[hpg + h:hpg + h + 1, :] = _col_to_row(
                    jnp.sum(jnp.where(hmask, dxd * xp, 0.0), axis=1, keepdims=True))
            dx_act = dxd * dlanes + dyp * dsk_ref[0:1, cols]
            dx_ref[:, cols] = (dx_act * _silu_grad(px_ref[:, cols].astype(F32))).astype(dx_ref.dtype)
            dsk_out[0:1, cols] += jnp.sum(dyp * xp, axis=0, keepdims=True)
            xde_parts.append(xde.astype(BF16))
            dye_parts.append((dyp * olanes).astype(BF16))
        xde16 = jnp.concatenate(xde_parts, axis=1)
        dye16 = jnp.concatenate(dye_parts, axis=1)
        dg16 = dg.astype(BF16)
        dc_act = jnp.dot(dg16, b16, preferred_element_type=F32) + jnp.dot(dye16, s16, preferred_element_type=F32)
        db_act = lax.dot_general(dg16, c16, (((0,), (0,)), ((), ())), preferred_element_type=F32) \
            + jnp.dot(xde16, ds16, preferred_element_type=F32)
        dc_ref[...] = (dc_act * _silu_grad(pc_ref[...].astype(F32))).astype(dc_ref.dtype)
        db_ref[...] = (db_act * _silu_grad(pb_ref[...].astype(F32))).astype(db_ref.dtype)
        ds_new = lax.dot_general(dye16, c16, (((0,), (0,)), ((), ())), preferred_element_type=F32)
        for h in range(hpg):
            rows = slice(h * SSM_HEAD_DIM, (h + 1) * SSM_HEAD_DIM)
            ds_ref[rows, :] = ds[rows, :] * jnp.exp(cs_ref[h:h + 1, cl - 1:cl]) + ds_new[rows, :]
        li = lax.broadcasted_iota(jnp.int32, (cl, cl), 0)
        si = lax.broadcasted_iota(jnp.int32, (cl, cl), 1)
        d_adt = jnp.dot(dcs_ref[0:hpg, :], (li >= si).astype(F32), precision=lax.Precision.HIGHEST,
                        preferred_element_type=F32)
        ddt = d_adt * a + dcs_ref[hpg:2 * hpg, :]
        ddt_raw = ddt * jax.nn.sigmoid(dtr_ref[...] + bias_ref[...])
        ddt_ref[...] = ddt_raw
        acc_ref[0:hpg, :] += d_adt * dt
        acc_ref[hpg:2 * hpg, :] += ddt_raw

    rc = lambda c: nc - 1 - c
    vec = lambda n: pl.BlockSpec((None, hpg, n), lambda gi, c: (gi, 0, 0))
    x_spec = pl.BlockSpec((cl, gw), lambda gi, c: (rc(c), gi))
    b_spec = pl.BlockSpec((cl, SSM_STATE), lambda gi, c: (rc(c), bo + gi))
    c_spec = pl.BlockSpec((cl, SSM_STATE), lambda gi, c: (rc(c), bo + ng + gi))
    dt_spec = pl.BlockSpec((None, hpg, cl), lambda gi, c: (gi, 0, rc(c)))
    return pl.pallas_call(
        body, name="ssd_bwd",
        out_shape=(jax.ShapeDtypeStruct((t, d_inner), BF16), jax.ShapeDtypeStruct((t, ng * SSM_STATE), BF16),
                   jax.ShapeDtypeStruct((t, ng * SSM_STATE), BF16), jax.ShapeDtypeStruct(dtT.shape, F32),
                   jax.ShapeDtypeStruct((ng, 2 * hpg, cl), F32), jax.ShapeDtypeStruct((1, d_inner), F32)),
        grid=(ng, nc),
        in_specs=[x_spec, b_spec, c_spec, dt_spec, vec(1), vec(1),
                  pl.BlockSpec((1, gw), lambda gi, c: (0, gi)),
                  pl.BlockSpec((None, None, gw, SSM_STATE), lambda gi, c: (gi, rc(c), 0, 0)),
                  x_spec],
        out_specs=(x_spec, pl.BlockSpec((cl, SSM_STATE), lambda gi, c: (rc(c), gi)),
                   pl.BlockSpec((cl, SSM_STATE), lambda gi, c: (rc(c), gi)), dt_spec,
                   pl.BlockSpec((None, 2 * hpg, cl), lambda gi, c: (gi, 0, 0)),
                   pl.BlockSpec((1, gw), lambda gi, c: (0, gi))),
        scratch_shapes=[pltpu.VMEM((gw, SSM_STATE), F32), pltpu.VMEM((2 * hpg, cl), F32),
                        pltpu.VMEM((2 * hpg, cl), F32)],
        compiler_params=pltpu.CompilerParams(dimension_semantics=("parallel", "arbitrary")),
    )(pre, pre, pre, dtT, bias, alog, dskip_lanes, states, dy)


def _gate_norm_fwd(y, zx, norm_w, d_inner):
    t = y.shape[0]
    tr = _pick(t, (256, 128))
    gs = d_inner // SSM_GROUPS

    def body(y_ref, z_ref, w_ref, o_ref):
        for gi in range(SSM_GROUPS):
            cols = slice(gi * gs, (gi + 1) * gs)
            v = y_ref[:, cols] * _silu(z_ref[:, cols].astype(F32))
            r = lax.rsqrt(jnp.mean(v * v, axis=-1, keepdims=True) + NORM_EPS)
            o_ref[:, cols] = (v * r * w_ref[0:1, cols]).astype(BF16)

    spec = pl.BlockSpec((tr, d_inner), lambda i: (i, 0))
    return pl.pallas_call(
        body, name="ssm_gate_norm_fwd", out_shape=jax.ShapeDtypeStruct((t, d_inner), BF16), grid=(t // tr,),
        in_specs=[spec, spec, pl.BlockSpec((1, d_inner), lambda i: (0, 0))], out_specs=spec,
        compiler_params=pltpu.CompilerParams(dimension_semantics=("parallel",)),
    )(y, zx, norm_w.reshape(1, -1))


def _gate_norm_bwd(y, zx, norm_w, dout, d_inner):
    t, width = zx.shape
    tr = _pick(t, (256, 128))
    gs = d_inner // SSM_GROUPS

    def body(y_ref, z_ref, w_ref, do_ref, dy_ref, dz_ref, dw_ref):
        @pl.when(pl.program_id(0) == 0)
        def _():
            dw_ref[...] = jnp.zeros_like(dw_ref)

        for gi in range(SSM_GROUPS):
            cols = slice(gi * gs, (gi + 1) * gs)
            yv, zv = y_ref[:, cols], z_ref[:, cols].astype(F32)
            sz = _silu(zv)
            v = yv * sz
            r = lax.rsqrt(jnp.mean(v * v, axis=-1, keepdims=True) + NORM_EPS)
            vhat = v * r
            dn = do_ref[:, cols].astype(F32)
            dw_ref[0:1, cols] += jnp.sum(dn * vhat, axis=0, keepdims=True)
            dvh = dn * w_ref[0:1, cols]
            dv = r * (dvh - vhat * jnp.mean(dvh * vhat, axis=-1, keepdims=True))
            dy_ref[:, cols] = dv * sz
            dz_ref[:, cols] = (dv * yv * _silu_grad(zv)).astype(dz_ref.dtype)

    spec = pl.BlockSpec((tr, d_inner), lambda i: (i, 0))
    wspec = pl.BlockSpec((1, d_inner), lambda i: (0, 0))
    return pl.pallas_call(
        body, name="ssm_gate_norm_bwd",
        out_shape=(jax.ShapeDtypeStruct((t, d_inner), F32), jax.ShapeDtypeStruct((t, width), zx.dtype),
                   jax.ShapeDtypeStruct((1, d_inner), F32)),
        grid=(t // tr,),
        in_specs=[spec, spec, wspec, spec], out_specs=(spec, spec, wspec),
        compiler_params=pltpu.CompilerParams(dimension_semantics=("arbitrary",)),
    )(y, zx, norm_w.reshape(1, -1), dout)


def _ssm_small(dt_raw, dt_bias, a_log, d_skip):
    heads = dt_raw.shape[1]
    hpg = heads // SSM_GROUPS
    dtT = dt_raw.T.reshape(SSM_GROUPS, hpg, -1)
    return (dtT, dt_bias.reshape(SSM_GROUPS, hpg, 1), a_log.reshape(SSM_GROUPS, hpg, 1),
            jnp.repeat(d_skip, SSM_HEAD_DIM).reshape(1, -1))


def _ssm_core_fwd(zx, dt_raw, conv_w, conv_b, dt_bias, a_log, d_skip, norm_w):
    d_inner = norm_w.shape[0]
    pre = _conv_fwd(zx, conv_w, conv_b, d_inner)
    dtT, bias, alog, dsk = _ssm_small(dt_raw, dt_bias, a_log, d_skip)
    y, states = _ssd_fwd(pre, dtT, bias, alog, dsk, d_inner)
    out = _gate_norm_fwd(y, zx, norm_w, d_inner)
    return out, (zx, dt_raw, conv_w, dt_bias, a_log, d_skip, norm_w, pre, y, states)


def _ssm_core_bwd(res, dout):
    zx, dt_raw, conv_w, dt_bias, a_log, d_skip, norm_w, pre, y, states = res
    d_inner = norm_w.shape[0]
    heads = dt_raw.shape[1]
    dy, dzx, dnorm = _gate_norm_bwd(y, zx, norm_w, dout, d_inner)
    dtT, bias, alog, dsk = _ssm_small(dt_raw, dt_bias, a_log, d_skip)
    dx, db, dc, ddtT, acc, dsk_l = _ssd_bwd(pre, dtT, bias, alog, dsk, states, dy, d_inner)
    dws, dbs, col0 = [], [], 0
    for part in (dx, db, dc):
        dzx, dw_part, db_part = _conv_bwd(zx, conv_w, part, dzx, d_inner, col0)
        dws.append(dw_part)
        dbs.append(db_part)
        col0 += part.shape[1]
    dconv_w, dconv_b = jnp.concatenate(dws, axis=1), jnp.concatenate(dbs, axis=1)
    d_dt_raw = ddtT.reshape(heads, -1).T
    hpg = heads // SSM_GROUPS
    da = acc[:, :hpg].sum(-1).reshape(heads)
    d_bias = acc[:, hpg:].sum(-1).reshape(heads)
    d_alog = da * (-jnp.exp(a_log))
    d_dskip = dsk_l.reshape(heads, SSM_HEAD_DIM).sum(-1)
    return dzx, d_dt_raw, dconv_w, dconv_b.reshape(-1), d_bias, d_alog, d_dskip, dnorm.reshape(-1)


ROWS_VMEM_BYTES = 24 * 1024 * 1024


def _rows_call(body, name, ins, outs, acc_outs=(), rows=256):
    t = max(a.shape[0] for a in ins)
    row_bytes = sum(a.shape[1] * jnp.dtype(a.dtype).itemsize for a in list(ins) + list(outs) if a.shape[0] == t)
    if 2 * 2 * rows * row_bytes <= ROWS_VMEM_BYTES:
        rows *= 2
    tr = _pick(t, (rows, 256, 128, 64, 32, 16, 8))

    def spec(a):
        if a.shape[0] == t:
            return pl.BlockSpec((tr, a.shape[1]), lambda i: (i, 0))
        return pl.BlockSpec(a.shape, lambda i: (0, 0))

    return pl.pallas_call(
        body, name=name, out_shape=tuple(outs) + tuple(acc_outs), grid=(t // tr,),
        in_specs=[spec(a) for a in ins],
        out_specs=tuple(spec(a) for a in outs) + tuple(pl.BlockSpec(a.shape, lambda i: (0, 0)) for a in acc_outs),
        compiler_params=pltpu.CompilerParams(dimension_semantics=("arbitrary" if acc_outs else "parallel",)),
    )(*ins)


def _rms_fwd(x, gain, after=None):
    def body(x_ref, g_ref, *rest):
        v = x_ref[...]
        rest[-1][...] = (v * lax.rsqrt(jnp.mean(v * v, axis=-1, keepdims=True) + NORM_EPS) * g_ref[...]).astype(BF16)

    ins = [x, gain.reshape(1, -1)] + ([] if after is None else [after])
    (h,) = _rows_call(body, "rms_fwd", ins, [jax.ShapeDtypeStruct(x.shape, BF16)])
    return h


def _rms_bwd(x, gain, dh, dres, after):
    def body(x_ref, g_ref, dh_ref, dr_ref, *rest):
        dx_ref, dg_ref = rest[-2:]

        @pl.when(pl.program_id(0) == 0)
        def _():
            dg_ref[...] = jnp.zeros_like(dg_ref)

        v = x_ref[...]
        r = lax.rsqrt(jnp.mean(v * v, axis=-1, keepdims=True) + NORM_EPS)
        vhat = v * r
        d = dh_ref[...].astype(F32)
        dg_ref[...] += jnp.sum(d * vhat, axis=0, keepdims=True)
        dvh = d * g_ref[...]
        dx_ref[...] = dr_ref[...] + r * (dvh - vhat * jnp.mean(dvh * vhat, axis=-1, keepdims=True))

    ins = [x, gain.reshape(1, -1), dh, dres] + ([] if after is None else [after])
    dx, dg = _rows_call(body, "rms_bwd", ins, [jax.ShapeDtypeStruct(x.shape, F32)],
                        [jax.ShapeDtypeStruct((1, x.shape[1]), F32)])
    return dx, dg.reshape(gain.shape)


def _swiglu_fwd(gu):
    t, f2 = gu.shape
    f = f2 // 2

    def body(gu_ref, o_ref):
        o_ref[...] = (_silu(gu_ref[:, :f].astype(F32)) * gu_ref[:, f:].astype(F32)).astype(BF16)

    (act,) = _rows_call(body, "swiglu_fwd", [gu], [jax.ShapeDtypeStruct((t, f), BF16)])
    return act


def _swiglu_bwd(gu, dact):
    t, f2 = gu.shape
    f = f2 // 2

    def body(gu_ref, d_ref, o_ref):
        g, u, d = gu_ref[:, :f].astype(F32), gu_ref[:, f:].astype(F32), d_ref[...].astype(F32)
        o_ref[:, :f] = (d * u * _silu_grad(g)).astype(BF16)
        o_ref[:, f:] = (d * _silu(g)).astype(BF16)

    (dgu,) = _rows_call(body, "swiglu_bwd", [gu, dact], [jax.ShapeDtypeStruct((t, f2), BF16)])
    return dgu


def _ple_fwd(x, gl, ple):
    def body(x_ref, g_ref, p_ref, o_ref):
        o_ref[...] = x_ref[...] + jax.nn.sigmoid(g_ref[...].astype(F32)) * p_ref[...].astype(F32)

    (out,) = _rows_call(body, "ple_fwd", [x, gl, ple], [jax.ShapeDtypeStruct(x.shape, F32)])
    return out


def _ple_bwd(gl, ple, dout):
    def body(g_ref, p_ref, d_ref, dg_ref, dp_ref):
        s, d = jax.nn.sigmoid(g_ref[...].astype(F32)), d_ref[...]
        dg_ref[...] = (d * p_ref[...].astype(F32) * s * (1.0 - s)).astype(BF16)
        dp_ref[...] = (d * s).astype(BF16)

    shp = jax.ShapeDtypeStruct(gl.shape, BF16)
    return _rows_call(body, "ple_bwd", [gl, ple, dout], [shp, shp])


def _loss_fwd(y, target):
    inv = 1.0 / y.shape[1]

    def body(y_ref, t_ref, d_ref, l_ref):
        @pl.when(pl.program_id(0) == 0)
        def _():
            l_ref[...] = jnp.zeros_like(l_ref)

        e = y_ref[...] - t_ref[...]
        d_ref[...] = e * inv
        part = jnp.sum(jnp.sum(e * e, axis=1, keepdims=True), axis=0, keepdims=True) * (0.5 * inv)
        l_ref[...] += jnp.broadcast_to(part, l_ref.shape)

    dy, acc = _rows_call(body, "loss_fwd", [y, target], [jax.ShapeDtypeStruct(y.shape, F32)],
                         [jax.ShapeDtypeStruct((SUBLANES, LANES), F32)])
    return acc[0, 0], dy


def local_step(small, fetch, emit, x, p, target):
    depth = small['norm_mix'].shape[0]
    ssm_small = ('ssm_conv_w', 'ssm_conv_b', 'ssm_dt_bias', 'ssm_a_log', 'ssm_d_skip', 'ssm_norm_w')
    saved = []
    for i in range(depth):
        j = i // 2
        s = {'x': x}
        wm, token = fetch(('mix', i), x)
        h = s['h'] = _rms_fwd(x, small['norm_mix'][i], token)
        if i % 2 == 0:
            n_main = wm['ssm_w_in'].shape[0] - small['ssm_dt_bias'].shape[1]
            zx = _mm(h, wm['ssm_w_in'][:n_main], tb=True, out_dtype=BF16, name="ssm_in_fwd")
            more, token = fetch(('out', i), zx)
            wm = {**wm, **more}
            dt_raw = _mm(h, wm['ssm_w_in'][n_main:], tb=True, after=token, name="ssm_dt_fwd")
            y, s['mix'] = _ssm_core_fwd(zx, dt_raw, wm['ssm_conv_w'], *[small[n][j] for n in ssm_small[1:]])
            x = _mm(y, wm['ssm_w_out'], add=x, name="ssm_out_fwd")
        else:
            qkv = _mm(h, wm['att_w_qkv'], tb=True, out_dtype=BF16, name="att_qkv_fwd")
            y, s['mix'] = _attention_core_fwd(qkv, small['att_q_norm'][j], small['att_k_norm'][j])
            x = _mm(y, wm['att_w_o'], add=x, name="att_o_fwd")
        s['wm'], s['y'], s['x1'] = wm, y, x
        wf, token = fetch(('ffn', i), x)
        s['wf'] = wf
        h2 = s['h2'] = _rms_fwd(x, small['norm_ffn'][i], token)
        gu = s['gu'] = _mm(h2, wf['ffn_w_gu'], tb=True, out_dtype=BF16, name="ffn_gu_fwd")
        act = s['act'] = _swiglu_fwd(gu)
        x = s['x2'] = _mm(act, wf['ffn_w_down'], add=x, name="ffn_down_fwd")
        gl = s['gl'] = _mm(x, wf['ple_w_gate'], out_dtype=BF16, name="ple_gate_fwd")
        ple = s['ple'] = _mm(p[i], wf['ple_w_proj'], tb=True, out_dtype=BF16, name="ple_proj_fwd")
        x = _ple_fwd(x, gl, ple)
        saved.append(s)
    loss, dx = _loss_fwd(x, target)

    g = {n: [None] * small[n].shape[0] for n in small}
    for i in reversed(range(depth)):
        j = i // 2
        s = saved[i]
        wm, wf = s['wm'], s['wf']
        gf = {}
        dgl, dple = _ple_bwd(s['gl'], s['ple'], dx)
        gf['ple_w_proj'] = _mm(dple, p[i], ta=True, out_dtype=BF16, name="ple_proj_dw")
        gf['ple_w_gate'] = _mm(s['x2'], dgl, ta=True, out_dtype=BF16, name="ple_gate_dw")
        dx = _mm(dgl, wf['ple_w_gate'], tb=True, add=dx, name="ple_gate_da")
        dact = _mm(dx, wf['ffn_w_down'], tb=True, out_dtype=BF16, name="ffn_down_da")
        gf['ffn_w_down'] = _mm(s['act'], dx, ta=True, out_dtype=BF16, name="ffn_down_dw")
        dgu = _swiglu_bwd(s['gu'], dact)
        dh2 = _mm(dgu, wf['ffn_w_gu'], out_dtype=BF16, name="ffn_gu_da")
        gf['ffn_w_gu'] = _mm(dgu, s['h2'], ta=True, out_dtype=BF16, name="ffn_gu_dw")
        dx, g['norm_ffn'][i] = _rms_bwd(s['x1'], small['norm_ffn'][i], dh2, dx, emit(('ffn', i), gf))
        gm = {}
        if i % 2 == 0:
            n_main = wm['ssm_w_in'].shape[0] - small['ssm_dt_bias'].shape[1]
            dyn = _mm(dx, wm['ssm_w_out'], tb=True, out_dtype=BF16, name="ssm_out_da")
            gm['ssm_w_out'] = _mm(s['y'], dx, ta=True, out_dtype=BF16, name="ssm_out_dw")
            dzx, d_dt, *sg = _ssm_core_bwd(s['mix'], dyn)
            for n, v in zip(ssm_small, sg):
                g[n][j] = v
            dh = _mm(d_dt, wm['ssm_w_in'][n_main:], name="ssm_dt_da")
            dh = _mm(dzx, wm['ssm_w_in'][:n_main], add=dh, out_dtype=BF16, name="ssm_in_da")
            gm['ssm_w_in'] = jnp.concatenate([_mm(dzx, s['h'], ta=True, out_dtype=BF16, name="ssm_in_dw"),
                                              _mm(d_dt, s['h'], ta=True, out_dtype=BF16, name="ssm_dt_dw")], axis=0)
        else:
            do = _mm(dx, wm['att_w_o'], tb=True, name="att_o_da")
            gm['att_w_o'] = _mm(s['y'], dx, ta=True, out_dtype=BF16, name="att_o_dw")
            dqkv, g['att_q_norm'][j], g['att_k_norm'][j] = _attention_core_bwd(s['mix'], do)
            dh = _mm(dqkv, wm['att_w_qkv'], out_dtype=BF16, name="att_qkv_da")
            gm['att_w_qkv'] = _mm(dqkv, s['h'], ta=True, out_dtype=BF16, name="att_qkv_dw")
        dx, g['norm_mix'][i] = _rms_bwd(s['x'], small['norm_mix'][i], dh, dx, emit(('mix', i), gm))
    return loss, dx, {n: jnp.stack(v) for n, v in g.items()}


CHANNEL_WEIGHTS = ('ffn_w_gate', 'ffn_w_up', 'ffn_w_down', 'ple_w_proj', 'ple_w_gate')


def _stages(depth):
    gather, scatter = {}, {}
    for i in range(depth):
        j = i // 2
        if i % 2 == 0:
            gather['mix', i] = [('ssm_w_in', j), ('ssm_conv_w', j)]
            gather['out', i] = [('ssm_w_out', j)]
            scatter['mix', i] = [('ssm_w_in', j), ('ssm_w_out', j)]
        else:
            gather['mix', i] = scatter['mix', i] = [('att_w_qkv', j), ('att_w_o', j)]
        gather['ffn', i] = scatter['ffn', i] = [(n, i) for n in CHANNEL_WEIGHTS]
    return gather, scatter


def _pack_plan(shapes, width, members):
    plan, off = [], 0
    for name, lyr in members:
        _, r, c = shapes[name]
        if name in COL_SHARDED:
            r, c = c, r
        if name == 'ssm_conv_w':
            pr = -(-2 * r * c // width)
        else:
            assert (r * c) % width == 0, (name, r, c)
            pr = r * c // width
        plan.append((name, lyr, r, c, pr, off))
        off += _round_up(pr, BF16_ROWS)
    return plan, off


def _small_plan(shapes):
    plan, off = [], 0
    for name in SMALL:
        n = math.prod(shapes[name])
        plan.append((name, n, off))
        off += n
    return plan, _round_up(off, SUBLANES * LANES)


def kernel(x, p, norm_mix, norm_ffn, ssm_w_in, ssm_conv_w, ssm_conv_b, ssm_dt_bias, ssm_a_log, ssm_d_skip, ssm_norm_w, ssm_w_out, att_w_qkv, att_q_norm, att_k_norm, att_w_o, ffn_w_gate, ffn_w_up, ffn_w_down, ple_w_proj, ple_w_gate, loss_target, m_norm_mix, m_norm_ffn, m_ssm_w_in, m_ssm_conv_w, m_ssm_conv_b, m_ssm_dt_bias, m_ssm_a_log, m_ssm_d_skip, m_ssm_norm_w, m_ssm_w_out, m_att_w_qkv, m_att_q_norm, m_att_k_norm, m_att_w_o, m_ffn_w_gate, m_ffn_w_up, m_ffn_w_down, m_ple_w_proj, m_ple_w_gate, v_norm_mix, v_norm_ffn, v_ssm_w_in, v_ssm_conv_w, v_ssm_conv_b, v_ssm_dt_bias, v_ssm_a_log, v_ssm_d_skip, v_ssm_norm_w, v_ssm_w_out, v_att_w_qkv, v_att_q_norm, v_att_k_norm, v_att_w_o, v_ffn_w_gate, v_ffn_w_up, v_ffn_w_down, v_ple_w_proj, v_ple_w_gate):
    given = dict(locals())
    w_in = {n: given[n] for n in WEIGHTS}
    m_in = {n: given["m_" + n] for n in WEIGHTS}
    v_in = {n: given["v_" + n] for n in WEIGHTS}
    width = x.shape[-1]
    depth = norm_mix.shape[0]

    gather_members, scatter_members = _stages(depth)
    shapes = {n: w_in[n].shape for n in BIG + ('ssm_conv_w',)}
    plans = {key: _pack_plan(shapes, width, members)[0] for key, members in gather_members.items()}
    splans = {key: _pack_plan(shapes, width, members)[0] for key, members in scatter_members.items()}
    order = list(gather_members)

    def pack_weights(stage):
        pieces = []
        for name, layer, r, c, pr, off in plans[stage]:
            blk = w_in[name][layer]
            if name == 'ssm_conv_w':
                blk = lax.bitcast_convert_type(blk.reshape(-1), BF16).reshape(-1)
                blk = jnp.pad(blk, (0, pr * width - blk.shape[0]))
            elif name in COL_SHARDED:
                blk = blk.T
            blk = blk.astype(BF16).reshape(pr, width)
            pieces.append(jnp.pad(blk, ((0, _round_up(pr, BF16_ROWS) - pr), (0, 0))))
        return jnp.concatenate(pieces, axis=0)

    packed = [pack_weights(order[0])]
    pending = [exchange_start(packed[0], True, "gather_start_0")]
    packed += [pack_weights(stage) for stage in order[1:]]

    def fetch(stage, after):
        k = order.index(stage)
        handle, token = pending[k]
        land = exchange_wait(handle, [token] + packed[1:] if k == 0 else after, True, f"gather_wait_{k}")
        token = None
        if k + 1 < len(order):
            pending.append(exchange_start(packed[k + 1], True, f"gather_start_{k + 1}", land))
            token = pending[-1][1]
        got = {}
        for name, layer, r, c, pr, off in plans[stage]:
            piece = land[:, off:off + pr]
            if name == 'ssm_conv_w':
                taps, chans = w_in[name].shape[1:]
                bits = piece.reshape(N_DEV, -1)[:, :2 * taps * chans].reshape(N_DEV, taps * chans, 2)
                piece = lax.bitcast_convert_type(bits, F32).reshape(N_DEV, taps, chans)
                got[name] = piece.transpose(1, 0, 2).reshape(taps, N_DEV * chans)
            else:
                got[name] = piece.reshape(N_DEV * r, c)
        if 'ffn_w_gate' in got:
            got['ffn_w_gu'] = jnp.concatenate([got.pop('ffn_w_gate'), got.pop('ffn_w_up')], axis=0)
        return got, token

    scatters = {}

    def emit(stage, grads):
        grads = dict(grads)
        if 'ffn_w_gu' in grads:
            hidden = grads['ffn_w_gu'].shape[0] // 2
            grads['ffn_w_gate'], grads['ffn_w_up'] = grads['ffn_w_gu'][:hidden], grads['ffn_w_gu'][hidden:]
        pieces = []
        for name, layer, r, c, pr, off in splans[stage]:
            g = grads[name].reshape(N_DEV, pr, width)
            pieces.append(jnp.pad(g, ((0, 0), (0, _round_up(pr, BF16_ROWS) - pr), (0, 0))))
        scatters[stage], token = exchange_start(jnp.concatenate(pieces, axis=1), False,
                                                f"scatter_start_{len(scatters)}")
        return token

    small = {n: w_in[n] for n in SMALL}
    cs = w_in['ssm_conv_w'].shape[2]
    loss_local, gx, gw = local_step(small, fetch, emit, x[0], p[:, 0], loss_target[0])
    loss = lax.psum(loss_local, ("x", "y", "c"))

    parts = {}
    for k, (stage, handle) in enumerate(scatters.items()):
        received = exchange_wait(handle, gx, False, f"scatter_wait_{k}")
        gsum = sum_slots(received, f"sum_grads_{k}")
        for name, layer, r, c, pr, off in splans[stage]:
            g = gsum[off:off + pr].reshape(r, c)
            parts[name, layer] = g.T if name in COL_SHARDED else g
    grads = {n: jnp.stack([parts[n, layer] for layer in range(w_in[n].shape[0])]) for n in BIG}

    splan, stotal = _small_plan({n: gw[n].shape for n in SMALL})
    svec = jnp.concatenate([gw[n].reshape(-1) for n, _, _ in splan])
    svec = jnp.pad(svec, (0, stotal - svec.shape[0])).reshape(stotal // LANES, LANES)
    _, ssum = all_gather_sum_small(svec, "sum_small_grads")
    ssum = ssum.reshape(-1)
    for name, n, off in splan:
        grads[name] = ssum[off:off + n].reshape(gw[name].shape)
    me = _me()
    grads['ssm_conv_w'] = lax.dynamic_slice_in_dim(grads['ssm_conv_w'], me * cs, cs, axis=2)

    delta, new_m, new_v = {}, {}, {}
    for name in BIG:
        shp = w_in[name].shape
        flat = lambda a: a.reshape(-1, shp[-1])
        d, nm, nv = adamw(flat(w_in[name]), flat(grads[name]), flat(m_in[name]), flat(v_in[name]), "adamw_" + name)
        delta[name], new_m[name], new_v[name] = d.reshape(shp), nm.reshape(shp), nv.reshape(shp)
    splan2, stotal2 = _small_plan({n: w_in[n].shape for n in SMALL})

    def pack_small(src):
        vec = jnp.concatenate([src[n].reshape(-1) for n, _, _ in splan2])
        return jnp.pad(vec, (0, stotal2 - vec.shape[0]), constant_values=1.0).reshape(stotal2 // LANES, LANES)

    sd, snm, snv = adamw(pack_small(w_in), pack_small(grads), pack_small(m_in), pack_small(v_in), "adamw_small")
    for name, n, off in splan2:
        shp = w_in[name].shape
        delta[name] = sd.reshape(-1)[off:off + n].reshape(shp)
        new_m[name] = snm.reshape(-1)[off:off + n].reshape(shp)
        new_v[name] = snv.reshape(-1)[off:off + n].reshape(shp)

    return (loss, gx[None], *[grads[n] for n in WEIGHTS], *[delta[n] for n in WEIGHTS],
            *[new_m[n] for n in WEIGHTS], *[new_v[n] for n in WEIGHTS])
```

```python
import functools
import math

import jax
import jax.numpy as jnp
from jax import lax
from jax.experimental import pallas as pl
from jax.experimental.pallas import tpu as pltpu

F32 = jnp.float32
BF16 = jnp.bfloat16
N_DEV = 8
MESH = pl.DeviceIdType.MESH

SSM_HEAD_DIM = 64
SSM_GROUPS = 4
SSM_STATE = 128
CONV_WIDTH = 4
SSD_CHUNK = 128
ATT_HEAD_DIM = 64
DIL_PATTERNS = ((128, 1), (512, 4), (2048, 16))
NORM_EPS = 1e-6
ADAM_LR = 0.001
ADAM_B1 = 0.9
ADAM_B2 = 0.999
ADAM_EPS = 1e-08
ADAM_WD = 0.01
ADAM_STEP = 10

BF16_ROWS = 16
LANES = 128
SUBLANES = 8

WEIGHTS = ['norm_mix', 'norm_ffn', 'ssm_w_in', 'ssm_conv_w', 'ssm_conv_b', 'ssm_dt_bias', 'ssm_a_log', 'ssm_d_skip',
           'ssm_norm_w', 'ssm_w_out', 'att_w_qkv', 'att_q_norm', 'att_k_norm', 'att_w_o', 'ffn_w_gate', 'ffn_w_up',
           'ffn_w_down', 'ple_w_proj', 'ple_w_gate']
COL_SHARDED = ('ssm_w_in', 'att_w_qkv', 'ffn_w_gate', 'ffn_w_up', 'ple_w_proj')
ROW_SHARDED = ('ssm_w_out', 'att_w_o', 'ffn_w_down', 'ple_w_gate')
BIG = COL_SHARDED + ROW_SHARDED
SMALL = ('norm_mix', 'norm_ffn', 'ssm_conv_w', 'ssm_conv_b', 'ssm_dt_bias', 'ssm_a_log', 'ssm_d_skip', 'ssm_norm_w',
         'att_q_norm', 'att_k_norm')


def _pick(n, cands):
    for c in cands:
        if n % c == 0:
            return c
    return n


def _round_up(n, m):
    return -(-n // m) * m


MM_TILES = (1024, 1408, 512, 256, 128)
MM_VMEM_BYTES = 48 * 1024 * 1024


def _mm(a, b, *, ta=False, tb=False, out_dtype=F32, add=None, after=None, name):
    k_dim, m_dim = (a.shape if ta else a.shape[::-1])
    n_dim = b.shape[0] if tb else b.shape[1]
    assert (b.shape[1] if tb else b.shape[0]) == k_dim, (a.shape, b.shape, ta, tb)
    tm = _pick(m_dim, MM_TILES)
    tn = _pick(n_dim, MM_TILES)
    tk = _pick(k_dim, MM_TILES)
    nk = k_dim // tk
    a_spec = pl.BlockSpec((tk, tm), lambda i, j, k: (k, i)) if ta else pl.BlockSpec((tm, tk), lambda i, j, k: (i, k))
    b_spec = pl.BlockSpec((tn, tk), lambda i, j, k: (j, k)) if tb else pl.BlockSpec((tk, tn), lambda i, j, k: (k, j))
    o_spec = pl.BlockSpec((tm, tn), lambda i, j, k: (i, j))
    dims = (((0 if ta else 1,), (1 if tb else 0,)), ((), ()))
    has_add = add is not None
    n_in = 2 + has_add + (after is not None)

    def body(*refs):
        a_ref, b_ref = refs[:2]
        o_ref = refs[n_in]

        def dot():
            return lax.dot_general(a_ref[...].astype(BF16), b_ref[...].astype(BF16), dims,
                                   preferred_element_type=F32)

        def finish(acc):
            if has_add:
                acc = acc + refs[2][...].astype(F32)
            o_ref[...] = acc.astype(o_ref.dtype)

        if nk == 1:
            finish(dot())
            return
        acc_ref = refs[n_in + 1]
        k = pl.program_id(2)

        @pl.when(k == 0)
        def _():
            acc_ref[...] = dot()

        @pl.when((k > 0) & (k < nk - 1))
        def _():
            acc_ref[...] += dot()

        @pl.when(k == nk - 1)
        def _():
            finish(acc_ref[...] + dot())

    return pl.pallas_call(
        body, name=f"{name}_{m_dim}x{n_dim}x{k_dim}",
        out_shape=jax.ShapeDtypeStruct((m_dim, n_dim), out_dtype),
        grid=(m_dim // tm, n_dim // tn, nk),
        in_specs=[a_spec, b_spec] + ([o_spec] if has_add else []) + (
            [] if after is None else [pl.BlockSpec(memory_space=pl.ANY)]),
        out_specs=o_spec,
        scratch_shapes=[] if nk == 1 else [pltpu.VMEM((tm, tn), F32)],
        compiler_params=pltpu.CompilerParams(dimension_semantics=("parallel", "parallel", "arbitrary"),
                                             vmem_limit_bytes=MM_VMEM_BYTES),
    )(*((a, b) + ((add,) if has_add else ()) + (() if after is None else (after,))))


def _me():
    return 4 * lax.axis_index("x") + 2 * lax.axis_index("y") + lax.axis_index("c")


def _peer(j):
    x, y, c = lax.axis_index("x"), lax.axis_index("y"), lax.axis_index("c")
    px = 1 - x if j & 4 else x
    py = 1 - y if j & 2 else y
    pc = 1 - c if j & 1 else c
    return (px, py, pc), 4 * px + 2 * py + pc


def _exchange_body(src_of, dst_ref, send_sems, recv_sems, local_sem):
    me = _me()
    mine = pltpu.make_async_copy(src_of(me), dst_ref.at[me], local_sem)
    mine.start()
    sends = []
    for j in range(1, N_DEV):
        peer, pidx = _peer(j)
        cp = pltpu.make_async_remote_copy(src_ref=src_of(pidx), dst_ref=dst_ref.at[me], send_sem=send_sems.at[j - 1],
                                          recv_sem=recv_sems.at[j - 1], device_id=peer, device_id_type=MESH)
        cp.start()
        sends.append(cp)
    for j in range(1, N_DEV):
        peer, pidx = _peer(j)
        pltpu.make_async_remote_copy(src_ref=src_of(pidx), dst_ref=dst_ref.at[pidx], send_sem=send_sems.at[j - 1],
                                     recv_sem=recv_sems.at[j - 1], device_id=peer, device_id_type=MESH).wait_recv()
    for cp in sends:
        cp.wait_send()
    mine.wait()


_EXCHANGE_SCRATCH = [pltpu.SemaphoreType.DMA((N_DEV - 1,)), pltpu.SemaphoreType.DMA((N_DEV - 1,)),
                     pltpu.SemaphoreType.DMA]


_HBM = pl.BlockSpec(memory_space=pltpu.HBM)
_SEM = pl.BlockSpec(memory_space=pltpu.SEMAPHORE)


def _split_copies(src_ref, gather, land_ref, send_sems, recv_sems):
    me = _me()
    pairs = []
    for j in range(1, N_DEV):
        peer, pidx = _peer(j)

        def make(slot, peer=peer, pidx=pidx, j=j):
            return pltpu.make_async_remote_copy(
                src_ref=src_ref if gather else src_ref.at[pidx], dst_ref=land_ref.at[slot],
                send_sem=send_sems.at[j - 1], recv_sem=recv_sems.at[j - 1], device_id=peer, device_id_type=MESH)

        pairs.append((make(me), make(pidx)))
    return pairs


def exchange_start(src, gather, name, after=None):
    land_shape = ((N_DEV,) + src.shape) if gather else src.shape
    has_after = after is not None

    def body(*refs):
        src_ref, land_ref = refs[:2]
        send_sems, recv_sems = refs[2 + has_after:4 + has_after]
        for send, _ in _split_copies(src_ref, gather, land_ref, send_sems, recv_sems):
            send.start()
        refs[-1][...] = jnp.zeros_like(refs[-1])

    sem = pltpu.SemaphoreType.DMA((N_DEV - 1,))
    send_sems, recv_sems, src_thru, land, token = pl.pallas_call(
        body, name=name,
        out_shape=(sem, sem, pltpu.HBM(src.shape, src.dtype), pltpu.HBM(land_shape, src.dtype),
                   jax.ShapeDtypeStruct((SUBLANES, LANES), F32)),
        in_specs=(_HBM, _HBM) + ((pl.BlockSpec(memory_space=pl.ANY),) if has_after else ()),
        out_specs=(_SEM, _SEM, _HBM, _HBM, pl.BlockSpec(memory_space=pltpu.VMEM)),
        input_output_aliases={0: 2, 1: 3},
        compiler_params=pltpu.CompilerParams(has_side_effects=pltpu.SideEffectType.DATAFLOW_SIDE_EFFECTING),
    )(pltpu.with_memory_space_constraint(src, pltpu.HBM),
      pltpu.with_memory_space_constraint(lax.empty(land_shape, src.dtype), pltpu.HBM),
      *((after,) if has_after else ()))
    return (send_sems, recv_sems, src_thru, land), token


def exchange_wait(handle, after, gather, name):
    send_sems, recv_sems, src_thru, land = handle
    after = tuple(after) if isinstance(after, (tuple, list)) else (after,)

    def body(src_ref, land_ref, send_sems, recv_sems, *rest):
        for _, arrival in _split_copies(src_ref, gather, land_ref, send_sems, recv_sems):
            arrival.wait_send()
            arrival.wait_recv()

    src_done, got = pl.pallas_call(
        body, name=name,
        out_shape=(pltpu.HBM(src_thru.shape, src_thru.dtype), pltpu.HBM(land.shape, land.dtype)),
        in_specs=(_HBM, _HBM, _SEM, _SEM) + (pl.BlockSpec(memory_space=pl.ANY),) * len(after), out_specs=(_HBM, _HBM),
        input_output_aliases={0: 0, 1: 1},
        compiler_params=pltpu.CompilerParams(has_side_effects=pltpu.SideEffectType.DATAFLOW_SIDE_EFFECTING),
    )(src_thru, land, send_sems, recv_sems, *after)
    mine = src_done if gather else lax.dynamic_index_in_dim(src_done, _me(), 0, keepdims=False)
    return lax.dynamic_update_index_in_dim(got, mine, _me(), 0)


def all_gather_sum_small(v, name):
    def body(x_ref, out_ref, sum_ref, send_sems, recv_sems, local_sem):
        _exchange_body(lambda k: x_ref, out_ref, send_sems, recv_sems, local_sem)
        acc = out_ref[0]
        for k in range(1, N_DEV):
            acc = acc + out_ref[k]
        sum_ref[...] = acc

    return pl.pallas_call(
        body, name=name,
        out_shape=(jax.ShapeDtypeStruct((N_DEV,) + v.shape, v.dtype), jax.ShapeDtypeStruct(v.shape, v.dtype)),
        in_specs=[pl.BlockSpec(memory_space=pltpu.VMEM)],
        out_specs=(pl.BlockSpec(memory_space=pltpu.VMEM), pl.BlockSpec(memory_space=pltpu.VMEM)),
        scratch_shapes=list(_EXCHANGE_SCRATCH),
    )(v)


def sum_slots(slots, name):
    _, p_dim, c_dim = slots.shape
    tp = next(tp for tp in range(512, 0, -BF16_ROWS) if p_dim % tp == 0)

    def body(x_ref, o_ref):
        acc = x_ref[0].astype(F32)
        for k in range(1, N_DEV):
            acc = acc + x_ref[k].astype(F32)
        o_ref[...] = acc

    return pl.pallas_call(
        body, name=name,
        out_shape=jax.ShapeDtypeStruct((p_dim, c_dim), F32),
        grid=(p_dim // tp,),
        in_specs=[pl.BlockSpec((N_DEV, tp, c_dim), lambda i: (0, i, 0))],
        out_specs=pl.BlockSpec((tp, c_dim), lambda i: (i, 0)),
        compiler_params=pltpu.CompilerParams(dimension_semantics=("parallel",)),
    )(slots)


def adamw(w, g, m, v, name):
    rows, cols = w.shape
    tr = _pick(rows, (256, 128, 64, 32, 16, 8))

    def body(w_ref, g_ref, m_ref, v_ref, d_ref, nm_ref, nv_ref):
        gv = g_ref[...]
        nm = ADAM_B1 * m_ref[...] + (1.0 - ADAM_B1) * gv
        nv = ADAM_B2 * v_ref[...] + (1.0 - ADAM_B2) * (gv * gv)
        m_hat = nm / (1.0 - ADAM_B1 ** ADAM_STEP)
        v_hat = nv / (1.0 - ADAM_B2 ** ADAM_STEP)
        d_ref[...] = -ADAM_LR * (m_hat / (jnp.sqrt(v_hat) + ADAM_EPS) + ADAM_WD * w_ref[...])
        nm_ref[...] = nm
        nv_ref[...] = nv

    spec = pl.BlockSpec((tr, cols), lambda i: (i, 0))
    shp = jax.ShapeDtypeStruct((rows, cols), F32)
    return pl.pallas_call(
        body, name=name, out_shape=(shp, shp, shp), grid=(rows // tr,),
        in_specs=[spec] * 4, out_specs=(spec,) * 3,
        compiler_params=pltpu.CompilerParams(dimension_semantics=("parallel",)),
    )(w, g, m, v)


ATT_BLK = 128
NEG = -1e30


def _head_sums(v):
    li = lax.broadcasted_iota(jnp.int32, (LANES, LANES), 0) // ATT_HEAD_DIM
    lj = lax.broadcasted_iota(jnp.int32, (LANES, LANES), 1) // ATT_HEAD_DIM
    ones = (li == lj).astype(BF16)
    hi = v.astype(BF16)
    lo = (v - hi.astype(F32)).astype(BF16)
    return jnp.dot(hi, ones, preferred_element_type=F32) + jnp.dot(lo, ones, preferred_element_type=F32)


def _head_col(v, hmask):
    return jnp.max(jnp.where(hmask, v, -jnp.inf), axis=-1, keepdims=True)


def _qk_norm(raw, gain2):
    rstd = lax.rsqrt(_head_sums(raw * raw) * (1.0 / ATT_HEAD_DIM) + NORM_EPS)
    xhat = raw * rstd
    return xhat * gain2, xhat, rstd


def _qk_norm_bwd(dn, xhat, rstd, gain2):
    dxh = dn * gain2
    return rstd * (dxh - xhat * (_head_sums(dxh * xhat) * (1.0 / ATT_HEAD_DIM))), dn * xhat


def _att_mask_bias(n, dilation):
    qi = lax.broadcasted_iota(jnp.int32, (ATT_BLK, 2 * ATT_BLK), 0)
    ki = lax.broadcasted_iota(jnp.int32, (ATT_BLK, 2 * ATT_BLK), 1)
    dist = qi + ATT_BLK - ki
    valid = (dist >= 0) & (dist <= ATT_BLK) & ((n > 0) | (ki >= ATT_BLK))
    return valid, (dilation * dist).astype(F32)


ATT_PAIRS = 8
RELAYOUT_ROWS = 512
RELAYOUT_COLS = 512


def _to_residues(x, dilation, col0=0, cols=None):
    t = x.shape[0]
    cols = x.shape[1] if cols is None else cols
    if dilation == 1 and col0 == 0 and cols == x.shape[1]:
        return x.reshape(1, t, cols)
    tr = _pick(t, (RELAYOUT_ROWS,))
    tc = _pick(cols, (RELAYOUT_COLS, 256, 128))
    per = tr // dilation
    assert tr % dilation == 0 and col0 % tc == 0

    def body(x_ref, o_ref, s_ref):
        for c in range(tc // LANES):
            lanes = slice(c * LANES, (c + 1) * LANES)
            s_ref[c] = x_ref[:, lanes].astype(F32)
            for r in range(dilation):
                o_ref[r, :, lanes] = s_ref[c, pl.ds(r, per, stride=dilation), :].astype(o_ref.dtype)

    return pl.pallas_call(
        body, name=f"to_residues_{dilation}", out_shape=jax.ShapeDtypeStruct((dilation, t // dilation, cols), x.dtype),
        grid=(t // tr, cols // tc),
        in_specs=[pl.BlockSpec((tr, tc), lambda i, j: (i, col0 // tc + j))],
        out_specs=pl.BlockSpec((dilation, per, tc), lambda i, j: (0, i, j)),
        scratch_shapes=[pltpu.VMEM((tc // LANES, tr, LANES), F32)],
        compiler_params=pltpu.CompilerParams(dimension_semantics=("parallel", "parallel")),
    )(x)


def _from_residues(y):
    dilation, lu, cols = y.shape
    t = dilation * lu
    if dilation == 1:
        return y.reshape(t, cols)
    tr = _pick(t, (RELAYOUT_ROWS,))
    tc = _pick(cols, (RELAYOUT_COLS, 256, 128))
    per = tr // dilation

    def body(y_ref, o_ref, s_ref):
        for c in range(tc // LANES):
            lanes = slice(c * LANES, (c + 1) * LANES)
            for r in range(dilation):
                s_ref[c, pl.ds(r, per, stride=dilation), :] = y_ref[r, :, lanes].astype(F32)
            o_ref[:, lanes] = s_ref[c].astype(o_ref.dtype)

    return pl.pallas_call(
        body, name=f"from_residues_{dilation}", out_shape=jax.ShapeDtypeStruct((t, cols), y.dtype),
        grid=(t // tr, cols // tc),
        in_specs=[pl.BlockSpec((dilation, per, tc), lambda i, j: (0, i, j))],
        out_specs=pl.BlockSpec((tr, tc), lambda i, j: (i, j)),
        scratch_shapes=[pltpu.VMEM((tc // LANES, tr, LANES), F32)],
        compiler_params=pltpu.CompilerParams(dimension_semantics=("parallel", "parallel")),
    )(y)


def _att_specs(hd, v_base, nb, pp):
    width = pp * LANES
    assert hd % width == 0 and v_base % width == 0

    def spec(base, shift):
        def imap(r, hp, n):
            row = jnp.minimum(n, nb - 1) if shift == 0 else jnp.maximum(n - 1, 0)
            return (r, row, base // width + hp)
        return pl.BlockSpec((None, ATT_BLK, width), imap)

    return [spec(0, 0), spec(hd, 1), spec(hd, 0), spec(v_base, 1), spec(v_base, 0)]


DELTA_LANE = 64


def _att_pre(qkv, g, dilation, gq2, gk2):
    t, width = qkv.shape
    hd = width // (3 * len(DIL_PATTERNS))
    tr = _pick(t, (RELAYOUT_ROWS,))
    tc = _pick(hd, (RELAYOUT_COLS, 256, 128))
    per = tr // dilation
    assert tr % dilation == 0 and (g * 3 * hd) % tc == 0

    def body(x_ref, gq_ref, gk_ref, o_ref, s_ref):
        gain = jnp.where(pl.program_id(1) * tc < hd, gq_ref[0:1, :], gk_ref[0:1, :])
        for c in range(tc // LANES):
            lanes = slice(c * LANES, (c + 1) * LANES)
            s_ref[c] = _qk_norm(x_ref[:, lanes].astype(F32), gain)[0]
            for r in range(dilation):
                o_ref[r, :, lanes] = s_ref[c, pl.ds(r, per, stride=dilation), :].astype(o_ref.dtype)

    vec_spec = pl.BlockSpec((SUBLANES, LANES), lambda i, j: (0, 0))
    return pl.pallas_call(
        body, name=f"att_pre_g{g}", out_shape=jax.ShapeDtypeStruct((dilation, t // dilation, 2 * hd), qkv.dtype),
        grid=(t // tr, 2 * hd // tc),
        in_specs=[pl.BlockSpec((tr, tc), lambda i, j: (i, g * 3 * hd // tc + j)), vec_spec, vec_spec],
        out_specs=pl.BlockSpec((dilation, per, tc), lambda i, j: (0, i, j)),
        scratch_shapes=[pltpu.VMEM((tc // LANES, tr, LANES), F32)],
        compiler_params=pltpu.CompilerParams(dimension_semantics=("parallel", "parallel")),
    )(qkv, gq2, gk2)


def _att_group_fwd(view, hd, slopes, g):
    qk_r, v_r, v_base = view
    dilation, lu, _ = qk_r.shape
    nb = lu // ATT_BLK
    assert nb * ATT_BLK == lu and hd % LANES == 0
    hpn = hd // LANES
    pp = math.gcd(ATT_PAIRS, hpn)
    scale = 1.0 / math.sqrt(ATT_HEAD_DIM)

    def body(q_ref, kp_ref, kc_ref, vp_ref, vc_ref, sl_ref, o_ref, l_ref):
        n = pl.program_id(2)
        lane = lax.broadcasted_iota(jnp.int32, (1, LANES), 1)
        first = (lane // ATT_HEAD_DIM) == 0
        valid, dist = _att_mask_bias(n, dilation)
        stats = jnp.zeros((ATT_BLK, LANES), F32)
        for pair in range(pp):
            cols = slice(pair * LANES, (pair + 1) * LANES)
            qn = q_ref[:, cols]
            kn16 = jnp.concatenate([kp_ref[:, cols], kc_ref[:, cols]], axis=0)
            v16 = jnp.concatenate([vp_ref[:, cols], vc_ref[:, cols]], axis=0)
            outs = []
            for hh in range(2):
                hmask = (lane // ATT_HEAD_DIM) == hh
                qh = jnp.where(hmask, qn, jnp.zeros_like(qn))
                s = lax.dot_general(qh, kn16, (((1,), (1,)), ((), ())), preferred_element_type=F32) * scale
                slope = _head_col(sl_ref[pair, 0:1, :], hmask)
                logits = jnp.where(valid, s - slope * dist, NEG)
                mx = jnp.max(logits, axis=-1, keepdims=True)
                pexp = jnp.exp(logits - mx)
                den = jnp.sum(pexp, axis=-1, keepdims=True)
                outs.append(jnp.dot(pexp.astype(BF16), v16, preferred_element_type=F32) / den)
                stats = jnp.where(lane == 2 * pair + hh, mx + jnp.log(den), stats)
            o_ref[:, cols] = jnp.where(first, outs[0], outs[1]).astype(BF16)
        l_ref[...] = stats

    out_spec = pl.BlockSpec((None, ATT_BLK, pp * LANES), lambda r, hp, n: (r, n, hp))
    stat_spec = pl.BlockSpec((None, ATT_BLK, LANES), lambda r, hp, n: (r, n, hp))
    o, lse = pl.pallas_call(
        body, name=f"att_fwd_g{g}",
        out_shape=(jax.ShapeDtypeStruct((dilation, lu, hd), BF16),
                   jax.ShapeDtypeStruct((dilation, lu, hpn // pp * LANES), F32)),
        grid=(dilation, hpn // pp, nb),
        in_specs=_att_specs(hd, v_base, nb, pp) + [pl.BlockSpec((pp, SUBLANES, LANES), lambda r, hp, n: (hp, 0, 0))],
        out_specs=(out_spec, stat_spec),
        compiler_params=pltpu.CompilerParams(dimension_semantics=("parallel", "parallel", "arbitrary")),
    )(qk_r, qk_r, qk_r, v_r, v_r, slopes)
    return _from_residues(o), _from_residues(lse)


def _att_merge(outs, lses):
    t, hd = outs[0].shape
    sw = lses[0].shape[1]
    pp = hd // sw
    tr = _pick(t, (512, 256, 128))
    ng = len(outs)

    def body(*refs):
        o_refs, l_refs, o16_ref, lt_ref = refs[:ng], refs[ng:2 * ng], refs[2 * ng], refs[2 * ng + 1]
        lane = lax.broadcasted_iota(jnp.int32, (1, LANES), 1)
        first = (lane // ATT_HEAD_DIM) == 0
        for blk in range(sw // LANES):
            scols = slice(blk * LANES, (blk + 1) * LANES)
            stats = jnp.zeros((tr, LANES), F32)
            for pair in range(pp):
                cols = slice((blk * pp + pair) * LANES, (blk * pp + pair + 1) * LANES)
                weights = []
                for hh in range(2):
                    pick = lane == 2 * pair + hh
                    ls = [_head_col(r[:, scols], pick) for r in l_refs]
                    mx = functools.reduce(jnp.maximum, ls)
                    es = [jnp.exp(l - mx) for l in ls]
                    den = functools.reduce(jnp.add, es)
                    weights.append([e / den for e in es])
                    stats = jnp.where(pick, mx + jnp.log(den), stats)
                acc = jnp.zeros((tr, LANES), F32)
                for gi in range(ng):
                    acc = acc + jnp.where(first, weights[0][gi], weights[1][gi]) * o_refs[gi][:, cols].astype(F32)
                o16_ref[:, cols] = acc.astype(BF16)
            lt_ref[:, scols] = stats

    spec = pl.BlockSpec((tr, hd), lambda i: (i, 0))
    sspec = pl.BlockSpec((tr, sw), lambda i: (i, 0))
    return pl.pallas_call(
        body, name="att_merge",
        out_shape=(jax.ShapeDtypeStruct((t, hd), BF16), jax.ShapeDtypeStruct((t, sw), F32)), grid=(t // tr,),
        in_specs=[spec] * ng + [sspec] * ng, out_specs=(spec, sspec),
        compiler_params=pltpu.CompilerParams(dimension_semantics=("parallel",)),
    )(*outs, *lses)


def _att_bwd_prep(do, o16, lse_tot):
    t, hd = do.shape
    sw = lse_tot.shape[1]
    pp = hd // sw
    tr = _pick(t, (512, 256, 128))

    def body(do_ref, o_ref, l_ref, d16_ref, st_ref):
        lane = lax.broadcasted_iota(jnp.int32, (1, LANES), 1)
        d16_ref[...] = do_ref[...].astype(BF16)
        for blk in range(sw // LANES):
            scols = slice(blk * LANES, (blk + 1) * LANES)
            stats = l_ref[:, scols]
            for pair in range(pp):
                cols = slice((blk * pp + pair) * LANES, (blk * pp + pair + 1) * LANES)
                prod = do_ref[:, cols] * o_ref[:, cols].astype(F32)
                for hh in range(2):
                    hmask = (lane // ATT_HEAD_DIM) == hh
                    delta = jnp.sum(jnp.where(hmask, prod, 0.0), axis=-1, keepdims=True)
                    stats = jnp.where(lane == DELTA_LANE + 2 * pair + hh, delta, stats)
            st_ref[:, scols] = stats

    spec = pl.BlockSpec((tr, hd), lambda i: (i, 0))
    sspec = pl.BlockSpec((tr, sw), lambda i: (i, 0))
    return pl.pallas_call(
        body, name="att_bwd_prep",
        out_shape=(jax.ShapeDtypeStruct((t, hd), BF16), jax.ShapeDtypeStruct((t, sw), F32)), grid=(t // tr,),
        in_specs=[spec, spec, sspec], out_specs=(spec, sspec),
        compiler_params=pltpu.CompilerParams(dimension_semantics=("parallel",)),
    )(do, o16, lse_tot)


def _att_post(buf, parts, qkv, gq2, gk2, g):
    dilation, lu, hd = parts[0].shape
    t = dilation * lu
    tr = _pick(t, (RELAYOUT_ROWS,))
    per = tr // dilation

    def body(*refs):
        raw_ref, gq_ref, gk_ref = refs[3:6]
        o_ref, dgq_ref, dgk_ref, s_ref = refs[-4:]

        @pl.when(pl.program_id(0) == 0)
        def _():
            dgq_ref[...] = jnp.zeros_like(dgq_ref)
            dgk_ref[...] = jnp.zeros_like(dgk_ref)

        for sec, y_ref in enumerate(refs[:3]):
            gsum = jnp.zeros((1, LANES), F32)
            for c in range(hd // LANES):
                lanes = slice(c * LANES, (c + 1) * LANES)
                out_lanes = slice(sec * hd + c * LANES, sec * hd + (c + 1) * LANES)
                for r in range(dilation):
                    s_ref[pl.ds(r, per, stride=dilation), :] = y_ref[r, :, lanes].astype(F32)
                d = s_ref[...]
                if sec < 2:
                    gain = (gq_ref if sec == 0 else gk_ref)[0:1, :]
                    _, xhat, rstd = _qk_norm(raw_ref[:, out_lanes].astype(F32), gain)
                    d, part = _qk_norm_bwd(d, xhat, rstd, gain)
                    gsum = gsum + jnp.sum(part, axis=0, keepdims=True)
                o_ref[:, out_lanes] = d.astype(o_ref.dtype)
            if sec < 2:
                acc = dgq_ref if sec == 0 else dgk_ref
                acc[...] += jnp.broadcast_to(gsum, acc.shape)

    part_spec = pl.BlockSpec((dilation, per, hd), lambda i: (0, i, 0))
    slab_spec = pl.BlockSpec((tr, 3 * hd), lambda i: (i, g))
    vec_spec = pl.BlockSpec((SUBLANES, LANES), lambda i: (0, 0))
    vec_shape = jax.ShapeDtypeStruct((SUBLANES, LANES), F32)
    return pl.pallas_call(
        body, name=f"att_post_g{g}", out_shape=(jax.ShapeDtypeStruct(qkv.shape, qkv.dtype), vec_shape, vec_shape),
        grid=(t // tr,),
        in_specs=[part_spec] * 3 + [slab_spec, vec_spec, vec_spec] + (
            [] if buf is None else [pl.BlockSpec(memory_space=pl.ANY)]),
        out_specs=(slab_spec, vec_spec, vec_spec),
        scratch_shapes=[pltpu.VMEM((tr, LANES), F32)],
        input_output_aliases={} if buf is None else {6: 0},
        compiler_params=pltpu.CompilerParams(dimension_semantics=("arbitrary",)),
    )(*parts, qkv, gq2, gk2, *(() if buf is None else (buf,)))


def _att_group_bwd(view, hd, slopes, stats, do16, g):
    qk_r, v_r, v_base = view
    dilation, lu, _ = qk_r.shape
    nb = lu // ATT_BLK
    hpn = hd // LANES
    pp = math.gcd(ATT_PAIRS, hpn)
    hbn = hpn // pp
    scale = 1.0 / math.sqrt(ATT_HEAD_DIM)

    def body(q_ref, kp_ref, kc_ref, vp_ref, vc_ref, sl_ref, st_ref, do_ref, dq_ref, dk_ref, dv_ref, ck_ref, cv_ref):
        n = pl.program_id(2)
        lane = lax.broadcasted_iota(jnp.int32, (1, LANES), 1)

        @pl.when(n == 0)
        def _():
            ck_ref[...] = jnp.zeros_like(ck_ref)
            cv_ref[...] = jnp.zeros_like(cv_ref)

        @pl.when(n < nb)
        def _():
            valid, dist = _att_mask_bias(n, dilation)
            stats = st_ref[...]
            for pair in range(pp):
                cols = slice(pair * LANES, (pair + 1) * LANES)
                qn = q_ref[:, cols]
                kn16 = jnp.concatenate([kp_ref[:, cols], kc_ref[:, cols]], axis=0)
                v16 = jnp.concatenate([vp_ref[:, cols], vc_ref[:, cols]], axis=0)
                dov = do_ref[:, cols]
                dq_acc = jnp.zeros((ATT_BLK, LANES), F32)
                dk_acc = jnp.zeros((2 * ATT_BLK, LANES), F32)
                dv_acc = jnp.zeros((2 * ATT_BLK, LANES), F32)
                for hh in range(2):
                    hmask = (lane // ATT_HEAD_DIM) == hh
                    qh = jnp.where(hmask, qn, jnp.zeros_like(qn))
                    doh = jnp.where(hmask, dov, jnp.zeros_like(dov))
                    s = lax.dot_general(qh, kn16, (((1,), (1,)), ((), ())), preferred_element_type=F32) * scale
                    slope = _head_col(sl_ref[pair, 0:1, :], hmask)
                    lse = _head_col(stats, lane == 2 * pair + hh)
                    delta = _head_col(stats, lane == DELTA_LANE + 2 * pair + hh)
                    pr = jnp.exp(jnp.where(valid, s - slope * dist - lse, NEG))
                    dp = lax.dot_general(doh, v16, (((1,), (1,)), ((), ())), preferred_element_type=F32)
                    ds = (pr * (dp - delta) * scale).astype(BF16)
                    dq_acc = dq_acc + jnp.where(hmask, jnp.dot(ds, kn16, preferred_element_type=F32), 0.0)
                    dk_acc = dk_acc + lax.dot_general(ds, qh, (((0,), (0,)), ((), ())), preferred_element_type=F32)
                    dv_acc = dv_acc + lax.dot_general(pr.astype(BF16), doh, (((0,), (0,)), ((), ())),
                                                      preferred_element_type=F32)
                dq_ref[:, cols] = dq_acc.astype(dq_ref.dtype)
                dk_ref[:, cols] = (ck_ref[:, cols] + dk_acc[:ATT_BLK]).astype(dk_ref.dtype)
                dv_ref[:, cols] = (cv_ref[:, cols] + dv_acc[:ATT_BLK]).astype(dv_ref.dtype)
                ck_ref[:, cols] = dk_acc[ATT_BLK:]
                cv_ref[:, cols] = dv_acc[ATT_BLK:]

        @pl.when(n == nb)
        def _():
            dk_ref[...] = ck_ref[...].astype(dk_ref.dtype)
            dv_ref[...] = cv_ref[...].astype(dv_ref.dtype)

    width = pp * LANES
    q_out = pl.BlockSpec((None, ATT_BLK, width), lambda r, hp, n: (r, jnp.minimum(n, nb - 1), hp))
    kv_out = pl.BlockSpec((None, ATT_BLK, width), lambda r, hp, n: (r, jnp.maximum(n - 1, 0), hp))
    st_spec = pl.BlockSpec((None, ATT_BLK, LANES), lambda r, hp, n: (r, jnp.minimum(n, nb - 1), hp))
    shp = jax.ShapeDtypeStruct((dilation, lu, hd), BF16)
    return pl.pallas_call(
        body, name=f"att_bwd_g{g}", out_shape=(shp, shp, shp), grid=(dilation, hbn, nb + 1),
        in_specs=_att_specs(hd, v_base, nb, pp) + [
            pl.BlockSpec((pp, SUBLANES, LANES), lambda r, hp, n: (hp, 0, 0)), st_spec, q_out],
        out_specs=(q_out, kv_out, kv_out),
        scratch_shapes=[pltpu.VMEM((ATT_BLK, width), F32), pltpu.VMEM((ATT_BLK, width), F32)],
        compiler_params=pltpu.CompilerParams(dimension_semantics=("parallel", "parallel", "arbitrary")),
    )(qk_r, qk_r, qk_r, v_r, v_r, slopes, _to_residues(stats, dilation), _to_residues(do16, dilation))


def _att_consts(q_gain, k_gain, hd):
    heads = hd // ATT_HEAD_DIM
    gq2 = jnp.broadcast_to(jnp.tile(q_gain, 2)[None], (SUBLANES, LANES))
    gk2 = jnp.broadcast_to(jnp.tile(k_gain, 2)[None], (SUBLANES, LANES))
    sl = 2.0 ** (-8.0 * jnp.arange(1, heads + 1, dtype=F32) / heads)
    slopes = jnp.broadcast_to(jnp.repeat(sl, ATT_HEAD_DIM).reshape(hd // LANES, 1, LANES), (hd // LANES, SUBLANES, LANES))
    return gq2, gk2, slopes


def _attention_core_fwd(qkv, q_gain, k_gain):
    hd = qkv.shape[1] // (3 * len(DIL_PATTERNS))
    gq2, gk2, slopes = _att_consts(q_gain, k_gain, hd)
    outs, lses, views = [], [], []
    for g, (_, dilation) in enumerate(DIL_PATTERNS):
        v_col = (3 * g + 2) * hd
        view = (_att_pre(qkv, g, dilation, gq2, gk2),) + (
            (_to_residues(qkv, 1), v_col) if dilation == 1 else (_to_residues(qkv, dilation, v_col, hd), 0))
        o_g, l_g = _att_group_fwd(view, hd, slopes, g)
        outs.append(o_g)
        lses.append(l_g)
        views.append(view)
    o16, lse_tot = _att_merge(outs, lses)
    return o16, (qkv, views, q_gain, k_gain, o16, lse_tot)


def _attention_core_bwd(res, do):
    qkv, views, q_gain, k_gain, o16, lse_tot = res
    hd = o16.shape[1]
    gq2, gk2, slopes = _att_consts(q_gain, k_gain, hd)
    do16, stats = _att_bwd_prep(do, o16, lse_tot)
    dqkv, dgq, dgk = None, 0.0, 0.0
    for g, view in enumerate(views):
        parts = _att_group_bwd(view, hd, slopes, stats, do16, g)
        dqkv, a, b = _att_post(dqkv, parts, qkv, gq2, gk2, g)
        dgq = dgq + a[0].reshape(-1, ATT_HEAD_DIM).sum(0)
        dgk = dgk + b[0].reshape(-1, ATT_HEAD_DIM).sum(0)
    return dqkv, dgq, dgk


HALO = 8


def _silu(v):
    return v * jax.nn.sigmoid(v)


def _silu_grad(v):
    s = jax.nn.sigmoid(v)
    return s * (1.0 + v * (1.0 - s))


def _halo_rows(dtype):
    return BF16_ROWS if dtype == BF16 else HALO


def _conv_fwd(zx, conv_w, conv_b, d_inner):
    t = zx.shape[0]
    conv_dim = conv_w.shape[1]
    cb = _pick(d_inner, (1024, 512, 256, 128))
    assert conv_dim % cb == 0
    tr = _pick(t, (512, 256, 128))
    off = d_inner // cb
    hx = _halo_rows(zx.dtype)

    def body(x_ref, h_ref, w_ref, b_ref, o_ref):
        i = pl.program_id(1)
        halo = jnp.where(i > 0, h_ref[...].astype(F32), 0.0)
        ext = jnp.concatenate([halo, x_ref[...].astype(F32)], axis=0)
        acc = jnp.broadcast_to(b_ref[...], (tr, cb))
        for k in range(CONV_WIDTH):
            s = CONV_WIDTH - 1 - k
            sh = ext if s == 0 else pltpu.roll(ext, shift=s, axis=0)
            acc = acc + w_ref[k:k + 1, :] * sh[hx:hx + tr]
        o_ref[...] = acc.astype(o_ref.dtype)

    return pl.pallas_call(
        body, name="ssm_conv_fwd", out_shape=jax.ShapeDtypeStruct((t, conv_dim), BF16),
        grid=(conv_dim // cb, t // tr),
        in_specs=[pl.BlockSpec((tr, cb), lambda j, i: (i, off + j)),
                  pl.BlockSpec((hx, cb), lambda j, i: (jnp.maximum(i * (tr // hx) - 1, 0), off + j)),
                  pl.BlockSpec((CONV_WIDTH, cb), lambda j, i: (0, j)),
                  pl.BlockSpec((1, cb), lambda j, i: (0, j))],
        out_specs=pl.BlockSpec((tr, cb), lambda j, i: (i, j)),
        compiler_params=pltpu.CompilerParams(dimension_semantics=("parallel", "parallel")),
    )(zx, zx, conv_w, conv_b.reshape(1, -1))


def _conv_bwd(zx, conv_w, dpre, dzx, d_inner, col0):
    t, width = zx.shape
    conv_dim = dpre.shape[1]
    cb = _pick(conv_dim, (1024, 512, 256, 128))
    assert (d_inner + col0) % cb == 0
    tr = _pick(t, (512, 256, 128))
    off = (d_inner + col0) // cb
    woff = col0 // cb
    nr = t // tr
    hx = _halo_rows(zx.dtype)
    hd = _halo_rows(dpre.dtype)

    def body(x_ref, h_ref, w_ref, d_ref, dn_ref, dzx_in, dx_ref, dw_ref, db_ref):
        i = pl.program_id(1)

        @pl.when(i == 0)
        def _():
            dw_ref[...] = jnp.zeros_like(dw_ref)
            db_ref[...] = jnp.zeros_like(db_ref)

        halo = jnp.where(i > 0, h_ref[...].astype(F32), 0.0)
        ext = jnp.concatenate([halo, x_ref[...].astype(F32)], axis=0)
        d = d_ref[...].astype(F32)
        dext = jnp.concatenate([d, jnp.where(i < nr - 1, dn_ref[...].astype(F32), 0.0)], axis=0)
        dx = jnp.zeros((tr, cb), F32)
        for k in range(CONV_WIDTH):
            s = CONV_WIDTH - 1 - k
            fut = dext if s == 0 else pltpu.roll(dext, shift=tr + hd - s, axis=0)
            dx = dx + w_ref[k:k + 1, :] * fut[:tr]
            past = ext if s == 0 else pltpu.roll(ext, shift=s, axis=0)
            dw_ref[k:k + 1, :] += jnp.sum(d * past[hx:hx + tr], axis=0, keepdims=True)
        dx_ref[...] = dx.astype(dx_ref.dtype)
        db_ref[...] += jnp.sum(d, axis=0, keepdims=True)

    last_halo = t // hd - 1
    return pl.pallas_call(
        body, name=f"ssm_conv_bwd_{col0}",
        out_shape=(jax.ShapeDtypeStruct(dzx.shape, dzx.dtype), jax.ShapeDtypeStruct((CONV_WIDTH, conv_dim), F32),
                   jax.ShapeDtypeStruct((1, conv_dim), F32)),
        grid=(conv_dim // cb, nr),
        in_specs=[pl.BlockSpec((tr, cb), lambda j, i: (i, off + j)),
                  pl.BlockSpec((hx, cb), lambda j, i: (jnp.maximum(i * (tr // hx) - 1, 0), off + j)),
                  pl.BlockSpec((CONV_WIDTH, cb), lambda j, i: (0, woff + j)),
                  pl.BlockSpec((tr, cb), lambda j, i: (i, j)),
                  pl.BlockSpec((hd, cb), lambda j, i: (jnp.minimum((i + 1) * (tr // hd), last_halo), j)),
                  pl.BlockSpec(memory_space=pl.ANY)],
        out_specs=(pl.BlockSpec((tr, cb), lambda j, i: (i, off + j)),
                   pl.BlockSpec((CONV_WIDTH, cb), lambda j, i: (0, j)),
                   pl.BlockSpec((1, cb), lambda j, i: (0, j))),
        input_output_aliases={5: 0},
        compiler_params=pltpu.CompilerParams(dimension_semantics=("parallel", "arbitrary")),
    )(zx, zx, conv_w, dpre, dpre, dzx)


def _eye(n):
    return lax.broadcasted_iota(jnp.int32, (n, n), 0) == lax.broadcasted_iota(jnp.int32, (n, n), 1)


def _row_to_col(row):
    n = row.shape[1]
    return jnp.sum(jnp.where(_eye(n), row, 0.0), axis=1, keepdims=True)


def _col_to_row(col):
    n = col.shape[0]
    return jnp.sum(jnp.where(_eye(n), col, 0.0), axis=0, keepdims=True)


def _pair_lanes(c0, c1):
    lane = lax.broadcasted_iota(jnp.int32, (1, LANES), 1)
    return jnp.where(lane < SSM_HEAD_DIM, c0, c1)


def _ssd_chunk_common(pre_x_ref, pre_b_ref, pre_c_ref, dtr_ref, bias_ref, alog_ref, cs_ref):
    cl = SSD_CHUNK
    hpg = dtr_ref.shape[0]
    x = _silu(pre_x_ref[...].astype(F32))
    b16 = _silu(pre_b_ref[...].astype(F32)).astype(BF16)
    c16 = _silu(pre_c_ref[...].astype(F32)).astype(BF16)
    dt = jax.nn.softplus(dtr_ref[...] + bias_ref[...])
    a = -jnp.exp(alog_ref[...])
    li = lax.broadcasted_iota(jnp.int32, (cl, cl), 0)
    si = lax.broadcasted_iota(jnp.int32, (cl, cl), 1)
    upper = (li <= si).astype(F32)
    cs_ref[0:hpg, :] = jnp.dot(dt * a, upper, precision=lax.Precision.HIGHEST, preferred_element_type=F32)
    cs_ref[hpg:2 * hpg, :] = dt
    g = lax.dot_general(c16, b16, (((1,), (1,)), ((), ())), preferred_element_type=F32)
    return x, b16, c16, dt, a, g, li >= si


def _ssd_fwd(pre, dtT, bias, alog, dskip_lanes, d_inner):
    t = pre.shape[0]
    cl = SSD_CHUNK
    nc = t // cl
    ng = SSM_GROUPS
    hpg = dtT.shape[1]
    gw = hpg * SSM_HEAD_DIM
    assert d_inner == ng * gw and hpg % 2 == 0
    bo = d_inner // SSM_STATE

    def body(px_ref, pb_ref, pc_ref, dtr_ref, bias_ref, alog_ref, dsk_ref, y_ref, st_ref, s_ref, cs_ref):
        c = pl.program_id(1)

        @pl.when(c == 0)
        def _():
            s_ref[...] = jnp.zeros_like(s_ref)

        x, b16, c16, dt, a, g, causal = _ssd_chunk_common(px_ref, pb_ref, pc_ref, dtr_ref, bias_ref, alog_ref, cs_ref)
        st_ref[...] = s_ref[...]
        yoff = lax.dot_general(c16, s_ref[...].astype(BF16), (((1,), (1,)), ((), ())), preferred_element_type=F32)
        xde_parts = []
        for j in range(hpg // 2):
            cols = slice(j * LANES, (j + 1) * LANES)
            xp = x[:, cols]
            dcol, ecol, ocol, ms = [], [], [], []
            for hh in range(2):
                h = 2 * j + hh
                cs_row = cs_ref[h:h + 1, :]
                cs_col = _row_to_col(cs_row)
                dcol.append(_row_to_col(cs_ref[hpg + h:hpg + h + 1, :]))
                ecol.append(jnp.exp(cs_ref[h:h + 1, cl - 1:cl] - cs_col))
                ocol.append(jnp.exp(cs_col))
                lm = jnp.where(causal, jnp.exp(jnp.minimum(cs_col - cs_row, 0.0)), 0.0)
                ms.append((g * lm).astype(BF16))
            xd = xp * _pair_lanes(dcol[0], dcol[1])
            xd16 = xd.astype(BF16)
            yd = _pair_lanes(1.0, 0.0) * jnp.dot(ms[0], xd16, preferred_element_type=F32) \
                + _pair_lanes(0.0, 1.0) * jnp.dot(ms[1], xd16, preferred_element_type=F32)
            y_ref[:, cols] = yd + yoff[:, cols] * _pair_lanes(ocol[0], ocol[1]) + xp * dsk_ref[0:1, cols]
            xde_parts.append((xd * _pair_lanes(ecol[0], ecol[1])).astype(BF16))
        new = lax.dot_general(jnp.concatenate(xde_parts, axis=1), b16, (((0,), (0,)), ((), ())),
                              preferred_element_type=F32)
        for h in range(hpg):
            rows = slice(h * SSM_HEAD_DIM, (h + 1) * SSM_HEAD_DIM)
            s_ref[rows, :] = s_ref[rows, :] * jnp.exp(cs_ref[h:h + 1, cl - 1:cl]) + new[rows, :]

    vec = lambda n: pl.BlockSpec((None, hpg, n), lambda gi, c: (gi, 0, 0))
    return pl.pallas_call(
        body, name="ssd_fwd",
        out_shape=(jax.ShapeDtypeStruct((t, d_inner), F32), jax.ShapeDtypeStruct((ng, nc, gw, SSM_STATE), F32)),
        grid=(ng, nc),
        in_specs=[pl.BlockSpec((cl, gw), lambda gi, c: (c, gi)),
                  pl.BlockSpec((cl, SSM_STATE), lambda gi, c: (c, bo + gi)),
                  pl.BlockSpec((cl, SSM_STATE), lambda gi, c: (c, bo + ng + gi)),
                  pl.BlockSpec((None, hpg, cl), lambda gi, c: (gi, 0, c)),
                  vec(1), vec(1),
                  pl.BlockSpec((1, gw), lambda gi, c: (0, gi))],
        out_specs=(pl.BlockSpec((cl, gw), lambda gi, c: (c, gi)),
                   pl.BlockSpec((None, None, gw, SSM_STATE), lambda gi, c: (gi, c, 0, 0))),
        scratch_shapes=[pltpu.VMEM((gw, SSM_STATE), F32), pltpu.VMEM((2 * hpg, cl), F32)],
        compiler_params=pltpu.CompilerParams(dimension_semantics=("parallel", "arbitrary")),
    )(pre, pre, pre, dtT, bias, alog, dskip_lanes)


def _ssd_bwd(pre, dtT, bias, alog, dskip_lanes, states, dy, d_inner):
    t, conv_dim = pre.shape
    cl = SSD_CHUNK
    nc = t // cl
    ng = SSM_GROUPS
    hpg = dtT.shape[1]
    gw = hpg * SSM_HEAD_DIM
    bo = d_inner // SSM_STATE

    def body(px_ref, pb_ref, pc_ref, dtr_ref, bias_ref, alog_ref, dsk_ref, st_ref, dy_ref,
             dx_ref, db_ref, dc_ref, ddt_ref, acc_ref, dsk_out, ds_ref, cs_ref, dcs_ref):
        c = pl.program_id(1)

        @pl.when(c == 0)
        def _():
            ds_ref[...] = jnp.zeros_like(ds_ref)
            acc_ref[...] = jnp.zeros_like(acc_ref)
            dsk_out[...] = jnp.zeros_like(dsk_out)

        x, b16, c16, dt, a, g, causal = _ssd_chunk_common(px_ref, pb_ref, pc_ref, dtr_ref, bias_ref, alog_ref, cs_ref)
        s_prev = st_ref[...]
        s16 = s_prev.astype(BF16)
        ds = ds_ref[...]
        ds16 = ds.astype(BF16)
        dyv = dy_ref[...]
        yoff = lax.dot_general(c16, s16, (((1,), (1,)), ((), ())), preferred_element_type=F32)
        bds = lax.dot_general(b16, ds16, (((1,), (1,)), ((), ())), preferred_element_type=F32)
        dg = jnp.zeros((cl, cl), F32)
        xde_parts, dye_parts = [], []
        lane = lax.broadcasted_iota(jnp.int32, (1, LANES), 1)
        for j in range(hpg // 2):
            cols = slice(j * LANES, (j + 1) * LANES)
            xp, dyp = x[:, cols], dyv[:, cols]
            dcol, ecol, ocol, lms = [], [], [], []
            for hh in range(2):
                h = 2 * j + hh
                cs_row = cs_ref[h:h + 1, :]
                cs_col = _row_to_col(cs_row)
                dcol.append(_row_to_col(cs_ref[hpg + h:hpg + h + 1, :]))
                ecol.append(jnp.exp(cs_ref[h:h + 1, cl - 1:cl] - cs_col))
                ocol.append(jnp.exp(cs_col))
                lms.append(jnp.where(causal, jnp.exp(jnp.minimum(cs_col - cs_row, 0.0)), 0.0))
            dlanes, elanes, olanes = _pair_lanes(*dcol), _pair_lanes(*ecol), _pair_lanes(*ocol)
            xd = xp * dlanes
            xd16 = xd.astype(BF16)
            xde = xd * elanes
            yoffp = yoff[:, cols] * olanes
            bdsp = bds[:, cols]
            dxd = bdsp * elanes
            for hh in range(2):
                h = 2 * j + hh
                hmask = (lane // SSM_HEAD_DIM) == hh
                dyh16 = jnp.where(hmask, dyp, 0.0).astype(BF16)
                m = g * lms[hh]
                dm = lax.dot_general(dyh16, xd16, (((1,), (1,)), ((), ())), preferred_element_type=F32)
                w = dm * m
                dg = dg + dm * lms[hh]
                dxd = dxd + lax.dot_general(m.astype(BF16), dyh16, (((0,), (0,)), ((), ())),
                                            preferred_element_type=F32)
                term = jnp.sum(jnp.where(hmask, xde * bdsp, 0.0), axis=1, keepdims=True)
                dcs_col = (jnp.sum(w, axis=1, keepdims=True)
                           + jnp.sum(jnp.where(hmask, dyp * yoffp, 0.0), axis=1, keepdims=True) - term)
                rows = slice(h * SSM_HEAD_DIM, (h + 1) * SSM_HEAD_DIM)
                dec = jnp.exp(cs_ref[h:h + 1, cl - 1:cl])
                tail = jnp.sum(term, axis=0, keepdims=True) + dec * jnp.sum(
                    jnp.sum(s_prev[rows, :] * ds[rows, :], axis=1, keepdims=True), axis=0, keepdims=True)
                last = lax.broadcasted_iota(jnp.int32, (1, cl), 1) == cl - 1
                dcs_ref[h:h + 1, :] = _col_to_row(dcs_col) - jnp.sum(w, axis=0, keepdims=True) + jnp.where(last, tail, 0.0)
                dcs_ref[hpg + h:hpg + h + 1, :] = _col_to_row(
                    jnp.sum(jnp.where(hmask, dxd * xp, 0.0), axis=1, keepdims=True))
            dx_act = dxd * dlanes + dyp * dsk_ref[0:1, cols]
            dx_ref[:, cols] = (dx_act * _silu_grad(px_ref[:, cols].astype(F32))).astype(dx_ref.dtype)
            dsk_out[0:1, cols] += jnp.sum(dyp * xp, axis=0, keepdims=True)
            xde_parts.append(xde.astype(BF16))
            dye_parts.append((dyp * olanes).astype(BF16))
        xde16 = jnp.concatenate(xde_parts, axis=1)
        dye16 = jnp.concatenate(dye_parts, axis=1)
        dg16 = dg.astype(BF16)
        dc_act = jnp.dot(dg16, b16, preferred_element_type=F32) + jnp.dot(dye16, s16, preferred_element_type=F32)
        db_act = lax.dot_general(dg16, c16, (((0,), (0,)), ((), ())), preferred_element_type=F32) \
            + jnp.dot(xde16, ds16, preferred_element_type=F32)
        dc_ref[...] = (dc_act * _silu_grad(pc_ref[...].astype(F32))).astype(dc_ref.dtype)
        db_ref[...] = (db_act * _silu_grad(pb_ref[...].astype(F32))).astype(db_ref.dtype)
        ds_new = lax.dot_general(dye16, c16, (((0,), (0,)), ((), ())), preferred_element_type=F32)
        for h in range(hpg):
            rows = slice(h * SSM_HEAD_DIM, (h + 1) * SSM_HEAD_DIM)
            ds_ref[rows, :] = ds[rows, :] * jnp.exp(cs_ref[h:h + 1, cl - 1:cl]) + ds_new[rows, :]
        li = lax.broadcasted_iota(jnp.int32, (cl, cl), 0)
        si = lax.broadcasted_iota(jnp.int32, (cl, cl), 1)
        d_adt = jnp.dot(dcs_ref[0:hpg, :], (li >= si).astype(F32), precision=lax.Precision.HIGHEST,
                        preferred_element_type=F32)
        ddt = d_adt * a + dcs_ref[hpg:2 * hpg, :]
        ddt_raw = ddt * jax.nn.sigmoid(dtr_ref[...] + bias_ref[...])
        ddt_ref[...] = ddt_raw
        acc_ref[0:hpg, :] += d_adt * dt
        acc_ref[hpg:2 * hpg, :] += ddt_raw

    rc = lambda c: nc - 1 - c
    vec = lambda n: pl.BlockSpec((None, hpg, n), lambda gi, c: (gi, 0, 0))
    x_spec = pl.BlockSpec((cl, gw), lambda gi, c: (rc(c), gi))
    b_spec = pl.BlockSpec((cl, SSM_STATE), lambda gi, c: (rc(c), bo + gi))
    c_spec = pl.BlockSpec((cl, SSM_STATE), lambda gi, c: (rc(c), bo + ng + gi))
    dt_spec = pl.BlockSpec((None, hpg, cl), lambda gi, c: (gi, 0, rc(c)))
    return pl.pallas_call(
        body, name="ssd_bwd",
        out_shape=(jax.ShapeDtypeStruct((t, d_inner), BF16), jax.ShapeDtypeStruct((t, ng * SSM_STATE), BF16),
                   jax.ShapeDtypeStruct((t, ng * SSM_STATE), BF16), jax.ShapeDtypeStruct(dtT.shape, F32),
                   jax.ShapeDtypeStruct((ng, 2 * hpg, cl), F32), jax.ShapeDtypeStruct((1, d_inner), F32)),
        grid=(ng, nc),
        in_specs=[x_spec, b_spec, c_spec, dt_spec, vec(1), vec(1),
                  pl.BlockSpec((1, gw), lambda gi, c: (0, gi)),
                  pl.BlockSpec((None, None, gw, SSM_STATE), lambda gi, c: (gi, rc(c), 0, 0)),
                  x_spec],
        out_specs=(x_spec, pl.BlockSpec((cl, SSM_STATE), lambda gi, c: (rc(c), gi)),
                   pl.BlockSpec((cl, SSM_STATE), lambda gi, c: (rc(c), gi)), dt_spec,
                   pl.BlockSpec((None, 2 * hpg, cl), lambda gi, c: (gi, 0, 0)),
                   pl.BlockSpec((1, gw), lambda gi, c: (0, gi))),
        scratch_shapes=[pltpu.VMEM((gw, SSM_STATE), F32), pltpu.VMEM((2 * hpg, cl), F32),
                        pltpu.VMEM((2 * hpg, cl), F32)],
        compiler_params=pltpu.CompilerParams(dimension_semantics=("parallel", "arbitrary")),
    )(pre, pre, pre, dtT, bias, alog, dskip_lanes, states, dy)


def _gate_norm_fwd(y, zx, norm_w, d_inner):
    t = y.shape[0]
    tr = _pick(t, (512, 256, 128))
    gs = d_inner // SSM_GROUPS

    def body(y_ref, z_ref, w_ref, o_ref):
        for gi in range(SSM_GROUPS):
            cols = slice(gi * gs, (gi + 1) * gs)
            v = y_ref[:, cols] * _silu(z_ref[:, cols].astype(F32))
            r = lax.rsqrt(jnp.mean(v * v, axis=-1, keepdims=True) + NORM_EPS)
            o_ref[:, cols] = (v * r * w_ref[0:1, cols]).astype(BF16)

    spec = pl.BlockSpec((tr, d_inner), lambda i: (i, 0))
    return pl.pallas_call(
        body, name="ssm_gate_norm_fwd", out_shape=jax.ShapeDtypeStruct((t, d_inner), BF16), grid=(t // tr,),
        in_specs=[spec, spec, pl.BlockSpec((1, d_inner), lambda i: (0, 0))], out_specs=spec,
        compiler_params=pltpu.CompilerParams(dimension_semantics=("parallel",)),
    )(y, zx, norm_w.reshape(1, -1))


def _gate_norm_bwd(y, zx, norm_w, dout, d_inner):
    t, width = zx.shape
    tr = _pick(t, (256, 128))
    gs = d_inner // SSM_GROUPS

    def body(y_ref, z_ref, w_ref, do_ref, dy_ref, dz_ref, dw_ref):
        @pl.when(pl.program_id(0) == 0)
        def _():
            dw_ref[...] = jnp.zeros_like(dw_ref)

        for gi in range(SSM_GROUPS):
            cols = slice(gi * gs, (gi + 1) * gs)
            yv, zv = y_ref[:, cols], z_ref[:, cols].astype(F32)
            sz = _silu(zv)
            v = yv * sz
            r = lax.rsqrt(jnp.mean(v * v, axis=-1, keepdims=True) + NORM_EPS)
            vhat = v * r
            dn = do_ref[:, cols].astype(F32)
            dw_ref[0:1, cols] += jnp.sum(dn * vhat, axis=0, keepdims=True)
            dvh = dn * w_ref[0:1, cols]
            dv = r * (dvh - vhat * jnp.mean(dvh * vhat, axis=-1, keepdims=True))
            dy_ref[:, cols] = dv * sz
            dz_ref[:, cols] = (dv * yv * _silu_grad(zv)).astype(dz_ref.dtype)

    spec = pl.BlockSpec((tr, d_inner), lambda i: (i, 0))
    wspec = pl.BlockSpec((1, d_inner), lambda i: (0, 0))
    return pl.pallas_call(
        body, name="ssm_gate_norm_bwd",
        out_shape=(jax.ShapeDtypeStruct((t, d_inner), F32), jax.ShapeDtypeStruct((t, width), zx.dtype),
                   jax.ShapeDtypeStruct((1, d_inner), F32)),
        grid=(t // tr,),
        in_specs=[spec, spec, wspec, spec], out_specs=(spec, spec, wspec),
        compiler_params=pltpu.CompilerParams(dimension_semantics=("arbitrary",)),
    )(y, zx, norm_w.reshape(1, -1), dout)


def _ssm_small(dt_raw, dt_bias, a_log, d_skip):
    heads = dt_raw.shape[1]
    hpg = heads // SSM_GROUPS
    dtT = dt_raw.T.reshape(SSM_GROUPS, hpg, -1)
    return (dtT, dt_bias.reshape(SSM_GROUPS, hpg, 1), a_log.reshape(SSM_GROUPS, hpg, 1),
            jnp.repeat(d_skip, SSM_HEAD_DIM).reshape(1, -1))


def _ssm_core_fwd(zx, dt_raw, conv_w, conv_b, dt_bias, a_log, d_skip, norm_w):
    d_inner = norm_w.shape[0]
    pre = _conv_fwd(zx, conv_w, conv_b, d_inner)
    dtT, bias, alog, dsk = _ssm_small(dt_raw, dt_bias, a_log, d_skip)
    y, states = _ssd_fwd(pre, dtT, bias, alog, dsk, d_inner)
    out = _gate_norm_fwd(y, zx, norm_w, d_inner)
    return out, (zx, dt_raw, conv_w, dt_bias, a_log, d_skip, norm_w, pre, y, states)


def _ssm_core_bwd(res, dout):
    zx, dt_raw, conv_w, dt_bias, a_log, d_skip, norm_w, pre, y, states = res
    d_inner = norm_w.shape[0]
    heads = dt_raw.shape[1]
    dy, dzx, dnorm = _gate_norm_bwd(y, zx, norm_w, dout, d_inner)
    dtT, bias, alog, dsk = _ssm_small(dt_raw, dt_bias, a_log, d_skip)
    dx, db, dc, ddtT, acc, dsk_l = _ssd_bwd(pre, dtT, bias, alog, dsk, states, dy, d_inner)
    dws, dbs, col0 = [], [], 0
    for part in (dx, db, dc):
        dzx, dw_part, db_part = _conv_bwd(zx, conv_w, part, dzx, d_inner, col0)
        dws.append(dw_part)
        dbs.append(db_part)
        col0 += part.shape[1]
    dconv_w, dconv_b = jnp.concatenate(dws, axis=1), jnp.concatenate(dbs, axis=1)
    d_dt_raw = ddtT.reshape(heads, -1).T
    hpg = heads // SSM_GROUPS
    da = acc[:, :hpg].sum(-1).reshape(heads)
    d_bias = acc[:, hpg:].sum(-1).reshape(heads)
    d_alog = da * (-jnp.exp(a_log))
    d_dskip = dsk_l.reshape(heads, SSM_HEAD_DIM).sum(-1)
    return dzx, d_dt_raw, dconv_w, dconv_b.reshape(-1), d_bias, d_alog, d_dskip, dnorm.reshape(-1)


ROWS_VMEM_BYTES = 24 * 1024 * 1024


def _rows_call(body, name, ins, outs, acc_outs=(), rows=256):
    t = max(a.shape[0] for a in ins)
    row_bytes = sum(a.shape[1] * jnp.dtype(a.dtype).itemsize for a in list(ins) + list(outs) if a.shape[0] == t)
    if 2 * 2 * rows * row_bytes <= ROWS_VMEM_BYTES:
        rows *= 2
    tr = _pick(t, (rows, 256, 128, 64, 32, 16, 8))

    def spec(a):
        if a.shape[0] == t:
            return pl.BlockSpec((tr, a.shape[1]), lambda i: (i, 0))
        return pl.BlockSpec(a.shape, lambda i: (0, 0))

    return pl.pallas_call(
        body, name=name, out_shape=tuple(outs) + tuple(acc_outs), grid=(t // tr,),
        in_specs=[spec(a) for a in ins],
        out_specs=tuple(spec(a) for a in outs) + tuple(pl.BlockSpec(a.shape, lambda i: (0, 0)) for a in acc_outs),
        compiler_params=pltpu.CompilerParams(dimension_semantics=("arbitrary" if acc_outs else "parallel",)),
    )(*ins)


def _rms_fwd(x, gain, after=None):
    def body(x_ref, g_ref, *rest):
        v = x_ref[...]
        rest[-1][...] = (v * lax.rsqrt(jnp.mean(v * v, axis=-1, keepdims=True) + NORM_EPS) * g_ref[...]).astype(BF16)

    ins = [x, gain.reshape(1, -1)] + ([] if after is None else [after])
    (h,) = _rows_call(body, "rms_fwd", ins, [jax.ShapeDtypeStruct(x.shape, BF16)])
    return h


def _rms_bwd(x, gain, dh, dres, after):
    def body(x_ref, g_ref, dh_ref, dr_ref, *rest):
        dx_ref, dg_ref = rest[-2:]

        @pl.when(pl.program_id(0) == 0)
        def _():
            dg_ref[...] = jnp.zeros_like(dg_ref)

        v = x_ref[...]
        r = lax.rsqrt(jnp.mean(v * v, axis=-1, keepdims=True) + NORM_EPS)
        vhat = v * r
        d = dh_ref[...].astype(F32)
        dg_ref[...] += jnp.sum(d * vhat, axis=0, keepdims=True)
        dvh = d * g_ref[...]
        dx_ref[...] = dr_ref[...] + r * (dvh - vhat * jnp.mean(dvh * vhat, axis=-1, keepdims=True))

    ins = [x, gain.reshape(1, -1), dh, dres] + ([] if after is None else [after])
    dx, dg = _rows_call(body, "rms_bwd", ins, [jax.ShapeDtypeStruct(x.shape, F32)],
                        [jax.ShapeDtypeStruct((1, x.shape[1]), F32)])
    return dx, dg.reshape(gain.shape)


def _swiglu_fwd(gu):
    t, f2 = gu.shape
    f = f2 // 2

    def body(gu_ref, o_ref):
        o_ref[...] = (_silu(gu_ref[:, :f].astype(F32)) * gu_ref[:, f:].astype(F32)).astype(BF16)

    (act,) = _rows_call(body, "swiglu_fwd", [gu], [jax.ShapeDtypeStruct((t, f), BF16)])
    return act


def _swiglu_bwd(gu, dact):
    t, f2 = gu.shape
    f = f2 // 2

    def body(gu_ref, d_ref, o_ref):
        g, u, d = gu_ref[:, :f].astype(F32), gu_ref[:, f:].astype(F32), d_ref[...].astype(F32)
        o_ref[:, :f] = (d * u * _silu_grad(g)).astype(BF16)
        o_ref[:, f:] = (d * _silu(g)).astype(BF16)

    (dgu,) = _rows_call(body, "swiglu_bwd", [gu, dact], [jax.ShapeDtypeStruct((t, f2), BF16)])
    return dgu


def _ple_fwd(x, gl, ple):
    def body(x_ref, g_ref, p_ref, o_ref):
        o_ref[...] = x_ref[...] + jax.nn.sigmoid(g_ref[...].astype(F32)) * p_ref[...].astype(F32)

    (out,) = _rows_call(body, "ple_fwd", [x, gl, ple], [jax.ShapeDtypeStruct(x.shape, F32)])
    return out


def _ple_bwd(gl, ple, dout):
    def body(g_ref, p_ref, d_ref, dg_ref, dp_ref):
        s, d = jax.nn.sigmoid(g_ref[...].astype(F32)), d_ref[...]
        dg_ref[...] = (d * p_ref[...].astype(F32) * s * (1.0 - s)).astype(BF16)
        dp_ref[...] = (d * s).astype(BF16)

    shp = jax.ShapeDtypeStruct(gl.shape, BF16)
    return _rows_call(body, "ple_bwd", [gl, ple, dout], [shp, shp])


def _loss_fwd(y, target):
    inv = 1.0 / y.shape[1]

    def body(y_ref, t_ref, d_ref, l_ref):
        @pl.when(pl.program_id(0) == 0)
        def _():
            l_ref[...] = jnp.zeros_like(l_ref)

        e = y_ref[...] - t_ref[...]
        d_ref[...] = e * inv
        part = jnp.sum(jnp.sum(e * e, axis=1, keepdims=True), axis=0, keepdims=True) * (0.5 * inv)
        l_ref[...] += jnp.broadcast_to(part, l_ref.shape)

    dy, acc = _rows_call(body, "loss_fwd", [y, target], [jax.ShapeDtypeStruct(y.shape, F32)],
                         [jax.ShapeDtypeStruct((SUBLANES, LANES), F32)])
    return acc[0, 0], dy


def local_step(small, fetch, emit, x, p, target):
    depth = small['norm_mix'].shape[0]
    ssm_small = ('ssm_conv_w', 'ssm_conv_b', 'ssm_dt_bias', 'ssm_a_log', 'ssm_d_skip', 'ssm_norm_w')
    saved = []
    for i in range(depth):
        j = i // 2
        s = {'x': x}
        wm, token = fetch(('mix', i), x)
        h = s['h'] = _rms_fwd(x, small['norm_mix'][i], token)
        if i % 2 == 0:
            n_main = wm['ssm_w_in'].shape[0] - small['ssm_dt_bias'].shape[1]
            zx = _mm(h, wm['ssm_w_in'][:n_main], tb=True, out_dtype=BF16, name="ssm_in_fwd")
            more, token = fetch(('out', i), zx)
            wm = {**wm, **more}
            dt_raw = _mm(h, wm['ssm_w_in'][n_main:], tb=True, after=token, name="ssm_dt_fwd")
            y, s['mix'] = _ssm_core_fwd(zx, dt_raw, wm['ssm_conv_w'], *[small[n][j] for n in ssm_small[1:]])
            x = _mm(y, wm['ssm_w_out'], add=x, name="ssm_out_fwd")
        else:
            qkv = _mm(h, wm['att_w_qkv'], tb=True, out_dtype=BF16, name="att_qkv_fwd")
            y, s['mix'] = _attention_core_fwd(qkv, small['att_q_norm'][j], small['att_k_norm'][j])
            x = _mm(y, wm['att_w_o'], add=x, name="att_o_fwd")
        s['wm'], s['y'], s['x1'] = wm, y, x
        wf, token = fetch(('ffn', i), x)
        s['wf'] = wf
        h2 = s['h2'] = _rms_fwd(x, small['norm_ffn'][i], token)
        gu = s['gu'] = _mm(h2, wf['ffn_w_gu'], tb=True, out_dtype=BF16, name="ffn_gu_fwd")
        act = s['act'] = _swiglu_fwd(gu)
        x = s['x2'] = _mm(act, wf['ffn_w_down'], add=x, name="ffn_down_fwd")
        gl = s['gl'] = _mm(x, wf['ple_w_gate'], out_dtype=BF16, name="ple_gate_fwd")
        ple = s['ple'] = _mm(p[i], wf['ple_w_proj'], tb=True, out_dtype=BF16, name="ple_proj_fwd")
        x = _ple_fwd(x, gl, ple)
        saved.append(s)
    loss, dx = _loss_fwd(x, target)

    g = {n: [None] * small[n].shape[0] for n in small}
    for i in reversed(range(depth)):
        j = i // 2
        s = saved[i]
        wm, wf = s['wm'], s['wf']
        gf = {}
        dgl, dple = _ple_bwd(s['gl'], s['ple'], dx)
        gf['ple_w_proj'] = _mm(dple, p[i], ta=True, out_dtype=BF16, name="ple_proj_dw")
        gf['ple_w_gate'] = _mm(s['x2'], dgl, ta=True, out_dtype=BF16, name="ple_gate_dw")
        dx = _mm(dgl, wf['ple_w_gate'], tb=True, add=dx, name="ple_gate_da")
        dact = _mm(dx, wf['ffn_w_down'], tb=True, out_dtype=BF16, name="ffn_down_da")
        gf['ffn_w_down'] = _mm(s['act'], dx, ta=True, out_dtype=BF16, name="ffn_down_dw")
        dgu = _swiglu_bwd(s['gu'], dact)
        dh2 = _mm(dgu, wf['ffn_w_gu'], out_dtype=BF16, name="ffn_gu_da")
        gf['ffn_w_gu'] = _mm(dgu, s['h2'], ta=True, out_dtype=BF16, name="ffn_gu_dw")
        dx, g['norm_ffn'][i] = _rms_bwd(s['x1'], small['norm_ffn'][i], dh2, dx, emit(('ffn', i), gf))
        gm = {}
        if i % 2 == 0:
            n_main = wm['ssm_w_in'].shape[0] - small['ssm_dt_bias'].shape[1]
            dyn = _mm(dx, wm['ssm_w_out'], tb=True, out_dtype=BF16, name="ssm_out_da")
            gm['ssm_w_out'] = _mm(s['y'], dx, ta=True, out_dtype=BF16, name="ssm_out_dw")
            dzx, d_dt, *sg = _ssm_core_bwd(s['mix'], dyn)
            for n, v in zip(ssm_small, sg):
                g[n][j] = v
            dh = _mm(d_dt, wm['ssm_w_in'][n_main:], name="ssm_dt_da")
            dh = _mm(dzx, wm['ssm_w_in'][:n_main], add=dh, out_dtype=BF16, name="ssm_in_da")
            gm['ssm_w_in'] = jnp.concatenate([_mm(dzx, s['h'], ta=True, out_dtype=BF16, name="ssm_in_dw"),
                                              _mm(d_dt, s['h'], ta=True, out_dtype=BF16, name="ssm_dt_dw")], axis=0)
        else:
            do = _mm(dx, wm['att_w_o'], tb=True, name="att_o_da")
            gm['att_w_o'] = _mm(s['y'], dx, ta=True, out_dtype=BF16, name="att_o_dw")
            dqkv, g['att_q_norm'][j], g['att_k_norm'][j] = _attention_core_bwd(s['mix'], do)
            dh = _mm(dqkv, wm['att_w_qkv'], out_dtype=BF16, name="att_qkv_da")
            gm['att_w_qkv'] = _mm(dqkv, s['h'], ta=True, out_dtype=BF16, name="att_qkv_dw")
        dx, g['norm_mix'][i] = _rms_bwd(s['x'], small['norm_mix'][i], dh, dx, emit(('mix', i), gm))
    return loss, dx, {n: jnp.stack(v) for n, v in g.items()}


CHANNEL_WEIGHTS = ('ffn_w_gate', 'ffn_w_up', 'ffn_w_down', 'ple_w_proj', 'ple_w_gate')


def _stages(depth):
    gather, scatter = {}, {}
    for i in range(depth):
        j = i // 2
        if i % 2 == 0:
            gather['mix', i] = [('ssm_w_in', j), ('ssm_conv_w', j)]
            gather['out', i] = [('ssm_w_out', j)]
            scatter['mix', i] = [('ssm_w_in', j), ('ssm_w_out', j)]
        else:
            gather['mix', i] = scatter['mix', i] = [('att_w_qkv', j), ('att_w_o', j)]
        gather['ffn', i] = scatter['ffn', i] = [(n, i) for n in CHANNEL_WEIGHTS]
    return gather, scatter


def _pack_plan(shapes, width, members):
    plan, off = [], 0
    for name, lyr in members:
        _, r, c = shapes[name]
        if name in COL_SHARDED:
            r, c = c, r
        if name == 'ssm_conv_w':
            pr = -(-2 * r * c // width)
        else:
            assert (r * c) % width == 0, (name, r, c)
            pr = r * c // width
        plan.append((name, lyr, r, c, pr, off))
        off += _round_up(pr, BF16_ROWS)
    return plan, off


def _small_plan(shapes):
    plan, off = [], 0
    for name in SMALL:
        n = math.prod(shapes[name])
        plan.append((name, n, off))
        off += n
    return plan, _round_up(off, SUBLANES * LANES)


def kernel(x, p, norm_mix, norm_ffn, ssm_w_in, ssm_conv_w, ssm_conv_b, ssm_dt_bias, ssm_a_log, ssm_d_skip, ssm_norm_w, ssm_w_out, att_w_qkv, att_q_norm, att_k_norm, att_w_o, ffn_w_gate, ffn_w_up, ffn_w_down, ple_w_proj, ple_w_gate, loss_target, m_norm_mix, m_norm_ffn, m_ssm_w_in, m_ssm_conv_w, m_ssm_conv_b, m_ssm_dt_bias, m_ssm_a_log, m_ssm_d_skip, m_ssm_norm_w, m_ssm_w_out, m_att_w_qkv, m_att_q_norm, m_att_k_norm, m_att_w_o, m_ffn_w_gate, m_ffn_w_up, m_ffn_w_down, m_ple_w_proj, m_ple_w_gate, v_norm_mix, v_norm_ffn, v_ssm_w_in, v_ssm_conv_w, v_ssm_conv_b, v_ssm_dt_bias, v_ssm_a_log, v_ssm_d_skip, v_ssm_norm_w, v_ssm_w_out, v_att_w_qkv, v_att_q_norm, v_att_k_norm, v_att_w_o, v_ffn_w_gate, v_ffn_w_up, v_ffn_w_down, v_ple_w_proj, v_ple_w_gate):
    given = dict(locals())
    w_in = {n: given[n] for n in WEIGHTS}
    m_in = {n: given["m_" + n] for n in WEIGHTS}
    v_in = {n: given["v_" + n] for n in WEIGHTS}
    width = x.shape[-1]
    depth = norm_mix.shape[0]

    gather_members, scatter_members = _stages(depth)
    shapes = {n: w_in[n].shape for n in BIG + ('ssm_conv_w',)}
    plans = {key: _pack_plan(shapes, width, members)[0] for key, members in gather_members.items()}
    splans = {key: _pack_plan(shapes, width, members)[0] for key, members in scatter_members.items()}
    order = list(gather_members)

    def pack_weights(stage):
        pieces = []
        for name, layer, r, c, pr, off in plans[stage]:
            blk = w_in[name][layer]
            if name == 'ssm_conv_w':
                blk = lax.bitcast_convert_type(blk.reshape(-1), BF16).reshape(-1)
                blk = jnp.pad(blk, (0, pr * width - blk.shape[0]))
            elif name in COL_SHARDED:
                blk = blk.T
            blk = blk.astype(BF16).reshape(pr, width)
            pieces.append(jnp.pad(blk, ((0, _round_up(pr, BF16_ROWS) - pr), (0, 0))))
        return jnp.concatenate(pieces, axis=0)

    packed = [pack_weights(order[0])]
    pending = [exchange_start(packed[0], True, "gather_start_0")]
    packed += [pack_weights(stage) for stage in order[1:]]

    def fetch(stage, after):
        k = order.index(stage)
        handle, token = pending[k]
        land = exchange_wait(handle, [token] + packed[1:] if k == 0 else after, True, f"gather_wait_{k}")
        token = None
        if k + 1 < len(order):
            pending.append(exchange_start(packed[k + 1], True, f"gather_start_{k + 1}", land))
            token = pending[-1][1]
        got = {}
        for name, layer, r, c, pr, off in plans[stage]:
            piece = land[:, off:off + pr]
            if name == 'ssm_conv_w':
                taps, chans = w_in[name].shape[1:]
                bits = piece.reshape(N_DEV, -1)[:, :2 * taps * chans].reshape(N_DEV, taps * chans, 2)
                piece = lax.bitcast_convert_type(bits, F32).reshape(N_DEV, taps, chans)
                got[name] = piece.transpose(1, 0, 2).reshape(taps, N_DEV * chans)
            else:
                got[name] = piece.reshape(N_DEV * r, c)
        if 'ffn_w_gate' in got:
            got['ffn_w_gu'] = jnp.concatenate([got.pop('ffn_w_gate'), got.pop('ffn_w_up')], axis=0)
        return got, token

    scatters = {}

    def emit(stage, grads):
        grads = dict(grads)
        if 'ffn_w_gu' in grads:
            hidden = grads['ffn_w_gu'].shape[0] // 2
            grads['ffn_w_gate'], grads['ffn_w_up'] = grads['ffn_w_gu'][:hidden], grads['ffn_w_gu'][hidden:]
        pieces = []
        for name, layer, r, c, pr, off in splans[stage]:
            g = grads[name].reshape(N_DEV, pr, width)
            pieces.append(jnp.pad(g, ((0, 0), (0, _round_up(pr, BF16_ROWS) - pr), (0, 0))))
        scatters[stage], token = exchange_start(jnp.concatenate(pieces, axis=1), False,
                                                f"scatter_start_{len(scatters)}")
        return token

    small = {n: w_in[n] for n in SMALL}
    cs = w_in['ssm_conv_w'].shape[2]
    loss_local, gx, gw = local_step(small, fetch, emit, x[0], p[:, 0], loss_target[0])
    loss = lax.psum(loss_local, ("x", "y", "c"))

    parts = {}
    for k, (stage, handle) in enumerate(scatters.items()):
        received = exchange_wait(handle, gx, False, f"scatter_wait_{k}")
        gsum = sum_slots(received, f"sum_grads_{k}")
        for name, layer, r, c, pr, off in splans[stage]:
            g = gsum[off:off + pr].reshape(r, c)
            parts[name, layer] = g.T if name in COL_SHARDED else g
    grads = {n: jnp.stack([parts[n, layer] for layer in range(w_in[n].shape[0])]) for n in BIG}

    splan, stotal = _small_plan({n: gw[n].shape for n in SMALL})
    svec = jnp.concatenate([gw[n].reshape(-1) for n, _, _ in splan])
    svec = jnp.pad(svec, (0, stotal - svec.shape[0])).reshape(stotal // LANES, LANES)
    _, ssum = all_gather_sum_small(svec, "sum_small_grads")
    ssum = ssum.reshape(-1)
    for name, n, off in splan:
        grads[name] = ssum[off:off + n].reshape(gw[name].shape)
    me = _me()
    grads['ssm_conv_w'] = lax.dynamic_slice_in_dim(grads['ssm_conv_w'], me * cs, cs, axis=2)

    delta, new_m, new_v = {}, {}, {}
    for name in BIG:
        shp = w_in[name].shape
        flat = lambda a: a.reshape(-1, shp[-1])
        d, nm, nv = adamw(flat(w_in[name]), flat(grads[name]), flat(m_in[name]), flat(v_in[name]), "adamw_" + name)
        delta[name], new_m[name], new_v[name] = d.reshape(shp), nm.reshape(shp), nv.reshape(shp)
    splan2, stotal2 = _small_plan({n: w_in[n].shape for n in SMALL})

    def pack_small(src):
        vec = jnp.concatenate([src[n].reshape(-1) for n, _, _ in splan2])
        return jnp.pad(vec, (0, stotal2 - vec.shape[0]), constant_values=1.0).reshape(stotal2 // LANES, LANES)

    sd, snm, snv = adamw(pack_small(w_in), pack_small(grads), pack_small(m_in), pack_small(v_in), "adamw_small")
    for name, n, off in splan2:
        shp = w_in[name].shape
        delta[name] = sd.reshape(-1)[off:off + n].reshape(shp)
        new_m[name] = snm.reshape(-1)[off:off + n].reshape(shp)
        new_v[name] = snv.reshape(-1)[off:off + n].reshape(shp)

    return (loss, gx[None], *[grads[n] for n in WEIGHTS], *[delta[n] for n in WEIGHTS],
            *[new_m[n] for n in WEIGHTS], *[new_v[n] for n in WEIGHTS])
```

```python
import functools
import math

import jax
import jax.numpy as jnp
from jax import lax
from jax.experimental import pallas as pl
from jax.experimental.pallas import tpu as pltpu

F32 = jnp.float32
BF16 = jnp.bfloat16
N_DEV = 8
MESH = pl.DeviceIdType.MESH

SSM_HEAD_DIM = 64
SSM_GROUPS = 4
SSM_STATE = 128
CONV_WIDTH = 4
SSD_CHUNK = 128
ATT_HEAD_DIM = 64
DIL_PATTERNS = ((128, 1), (512, 4), (2048, 16))
NORM_EPS = 1e-6
ADAM_LR = 0.001
ADAM_B1 = 0.9
ADAM_B2 = 0.999
ADAM_EPS = 1e-08
ADAM_WD = 0.01
ADAM_STEP = 10

BF16_ROWS = 16
LANES = 128
SUBLANES = 8

WEIGHTS = ['norm_mix', 'norm_ffn', 'ssm_w_in', 'ssm_conv_w', 'ssm_conv_b', 'ssm_dt_bias', 'ssm_a_log', 'ssm_d_skip',
           'ssm_norm_w', 'ssm_w_out', 'att_w_qkv', 'att_q_norm', 'att_k_norm', 'att_w_o', 'ffn_w_gate', 'ffn_w_up',
           'ffn_w_down', 'ple_w_proj', 'ple_w_gate']
COL_SHARDED = ('ssm_w_in', 'att_w_qkv', 'ffn_w_gate', 'ffn_w_up', 'ple_w_proj')
ROW_SHARDED = ('ssm_w_out', 'att_w_o', 'ffn_w_down', 'ple_w_gate')
BIG = COL_SHARDED + ROW_SHARDED
SMALL = ('norm_mix', 'norm_ffn', 'ssm_conv_w', 'ssm_conv_b', 'ssm_dt_bias', 'ssm_a_log', 'ssm_d_skip', 'ssm_norm_w',
         'att_q_norm', 'att_k_norm')


def _pick(n, cands):
    for c in cands:
        if n % c == 0:
            return c
    return n


def _round_up(n, m):
    return -(-n // m) * m


MM_TILES = (1024, 1408, 512, 256, 128)
MM_VMEM_BYTES = 48 * 1024 * 1024


def _mm(a, b, *, ta=False, tb=False, out_dtype=F32, add=None, after=None, name):
    k_dim, m_dim = (a.shape if ta else a.shape[::-1])
    n_dim = b.shape[0] if tb else b.shape[1]
    assert (b.shape[1] if tb else b.shape[0]) == k_dim, (a.shape, b.shape, ta, tb)
    tm = _pick(m_dim, MM_TILES)
    tn = _pick(n_dim, MM_TILES)
    tk = _pick(k_dim, MM_TILES)
    nk = k_dim // tk
    a_spec = pl.BlockSpec((tk, tm), lambda i, j, k: (k, i)) if ta else pl.BlockSpec((tm, tk), lambda i, j, k: (i, k))
    b_spec = pl.BlockSpec((tn, tk), lambda i, j, k: (j, k)) if tb else pl.BlockSpec((tk, tn), lambda i, j, k: (k, j))
    o_spec = pl.BlockSpec((tm, tn), lambda i, j, k: (i, j))
    dims = (((0 if ta else 1,), (1 if tb else 0,)), ((), ()))
    has_add = add is not None
    n_in = 2 + has_add + (after is not None)

    def body(*refs):
        a_ref, b_ref = refs[:2]
        o_ref = refs[n_in]

        def dot():
            return lax.dot_general(a_ref[...].astype(BF16), b_ref[...].astype(BF16), dims,
                                   preferred_element_type=F32)

        def finish(acc):
            if has_add:
                acc = acc + refs[2][...].astype(F32)
            o_ref[...] = acc.astype(o_ref.dtype)

        if nk == 1:
            finish(dot())
            return
        acc_ref = refs[n_in + 1]
        k = pl.program_id(2)

        @pl.when(k == 0)
        def _():
            acc_ref[...] = dot()

        @pl.when((k > 0) & (k < nk - 1))
        def _():
            acc_ref[...] += dot()

        @pl.when(k == nk - 1)
        def _():
            finish(acc_ref[...] + dot())

    return pl.pallas_call(
        body, name=f"{name}_{m_dim}x{n_dim}x{k_dim}",
        out_shape=jax.ShapeDtypeStruct((m_dim, n_dim), out_dtype),
        grid=(m_dim // tm, n_dim // tn, nk),
        in_specs=[a_spec, b_spec] + ([o_spec] if has_add else []) + (
            [] if after is None else [pl.BlockSpec(memory_space=pl.ANY)]),
        out_specs=o_spec,
        scratch_shapes=[] if nk == 1 else [pltpu.VMEM((tm, tn), F32)],
        compiler_params=pltpu.CompilerParams(dimension_semantics=("parallel", "parallel", "arbitrary"),
                                             vmem_limit_bytes=MM_VMEM_BYTES),
    )(*((a, b) + ((add,) if has_add else ()) + (() if after is None else (after,))))


def _me():
    return 4 * lax.axis_index("x") + 2 * lax.axis_index("y") + lax.axis_index("c")


def _peer(j):
    x, y, c = lax.axis_index("x"), lax.axis_index("y"), lax.axis_index("c")
    px = 1 - x if j & 4 else x
    py = 1 - y if j & 2 else y
    pc = 1 - c if j & 1 else c
    return (px, py, pc), 4 * px + 2 * py + pc


def _exchange_body(src_of, dst_ref, send_sems, recv_sems, local_sem):
    me = _me()
    mine = pltpu.make_async_copy(src_of(me), dst_ref.at[me], local_sem)
    mine.start()
    sends = []
    for j in range(1, N_DEV):
        peer, pidx = _peer(j)
        cp = pltpu.make_async_remote_copy(src_ref=src_of(pidx), dst_ref=dst_ref.at[me], send_sem=send_sems.at[j - 1],
                                          recv_sem=recv_sems.at[j - 1], device_id=peer, device_id_type=MESH)
        cp.start()
        sends.append(cp)
    for j in range(1, N_DEV):
        peer, pidx = _peer(j)
        pltpu.make_async_remote_copy(src_ref=src_of(pidx), dst_ref=dst_ref.at[pidx], send_sem=send_sems.at[j - 1],
                                     recv_sem=recv_sems.at[j - 1], device_id=peer, device_id_type=MESH).wait_recv()
    for cp in sends:
        cp.wait_send()
    mine.wait()


_EXCHANGE_SCRATCH = [pltpu.SemaphoreType.DMA((N_DEV - 1,)), pltpu.SemaphoreType.DMA((N_DEV - 1,)),
                     pltpu.SemaphoreType.DMA]


_HBM = pl.BlockSpec(memory_space=pltpu.HBM)
_SEM = pl.BlockSpec(memory_space=pltpu.SEMAPHORE)


def _split_copies(src_ref, gather, land_ref, send_sems, recv_sems):
    me = _me()
    pairs = []
    for j in range(1, N_DEV):
        peer, pidx = _peer(j)

        def make(slot, peer=peer, pidx=pidx, j=j):
            return pltpu.make_async_remote_copy(
                src_ref=src_ref if gather else src_ref.at[pidx], dst_ref=land_ref.at[slot],
                send_sem=send_sems.at[j - 1], recv_sem=recv_sems.at[j - 1], device_id=peer, device_id_type=MESH)

        pairs.append((make(me), make(pidx)))
    return pairs


def exchange_start(src, gather, name, after=None):
    land_shape = ((N_DEV,) + src.shape) if gather else src.shape
    has_after = after is not None

    def body(*refs):
        src_ref, land_ref = refs[:2]
        send_sems, recv_sems = refs[2 + has_after:4 + has_after]
        for send, _ in _split_copies(src_ref, gather, land_ref, send_sems, recv_sems):
            send.start()
        refs[-1][...] = jnp.zeros_like(refs[-1])

    sem = pltpu.SemaphoreType.DMA((N_DEV - 1,))
    send_sems, recv_sems, src_thru, land, token = pl.pallas_call(
        body, name=name,
        out_shape=(sem, sem, pltpu.HBM(src.shape, src.dtype), pltpu.HBM(land_shape, src.dtype),
                   jax.ShapeDtypeStruct((SUBLANES, LANES), F32)),
        in_specs=(_HBM, _HBM) + ((pl.BlockSpec(memory_space=pl.ANY),) if has_after else ()),
        out_specs=(_SEM, _SEM, _HBM, _HBM, pl.BlockSpec(memory_space=pltpu.VMEM)),
        input_output_aliases={0: 2, 1: 3},
        compiler_params=pltpu.CompilerParams(has_side_effects=pltpu.SideEffectType.DATAFLOW_SIDE_EFFECTING),
    )(pltpu.with_memory_space_constraint(src, pltpu.HBM),
      pltpu.with_memory_space_constraint(lax.empty(land_shape, src.dtype), pltpu.HBM),
      *((after,) if has_after else ()))
    return (send_sems, recv_sems, src_thru, land), token


def exchange_wait(handle, after, gather, name):
    send_sems, recv_sems, src_thru, land = handle
    after = tuple(after) if isinstance(after, (tuple, list)) else (after,)

    def body(src_ref, land_ref, send_sems, recv_sems, *rest):
        for _, arrival in _split_copies(src_ref, gather, land_ref, send_sems, recv_sems):
            arrival.wait_send()
            arrival.wait_recv()

    src_done, got = pl.pallas_call(
        body, name=name,
        out_shape=(pltpu.HBM(src_thru.shape, src_thru.dtype), pltpu.HBM(land.shape, land.dtype)),
        in_specs=(_HBM, _HBM, _SEM, _SEM) + (pl.BlockSpec(memory_space=pl.ANY),) * len(after), out_specs=(_HBM, _HBM),
        input_output_aliases={0: 0, 1: 1},
        compiler_params=pltpu.CompilerParams(has_side_effects=pltpu.SideEffectType.DATAFLOW_SIDE_EFFECTING),
    )(src_thru, land, send_sems, recv_sems, *after)
    mine = src_done if gather else lax.dynamic_index_in_dim(src_done, _me(), 0, keepdims=False)
    return lax.dynamic_update_index_in_dim(got, mine, _me(), 0)


def all_gather_sum_small(v, name):
    def body(x_ref, out_ref, sum_ref, send_sems, recv_sems, local_sem):
        _exchange_body(lambda k: x_ref, out_ref, send_sems, recv_sems, local_sem)
        acc = out_ref[0]
        for k in range(1, N_DEV):
            acc = acc + out_ref[k]
        sum_ref[...] = acc

    return pl.pallas_call(
        body, name=name,
        out_shape=(jax.ShapeDtypeStruct((N_DEV,) + v.shape, v.dtype), jax.ShapeDtypeStruct(v.shape, v.dtype)),
        in_specs=[pl.BlockSpec(memory_space=pltpu.VMEM)],
        out_specs=(pl.BlockSpec(memory_space=pltpu.VMEM), pl.BlockSpec(memory_space=pltpu.VMEM)),
        scratch_shapes=list(_EXCHANGE_SCRATCH),
    )(v)


def sum_slots(slots, name):
    _, p_dim, c_dim = slots.shape
    tp = next(tp for tp in range(512, 0, -BF16_ROWS) if p_dim % tp == 0)

    def body(x_ref, o_ref):
        acc = x_ref[0].astype(F32)
        for k in range(1, N_DEV):
            acc = acc + x_ref[k].astype(F32)
        o_ref[...] = acc

    return pl.pallas_call(
        body, name=name,
        out_shape=jax.ShapeDtypeStruct((p_dim, c_dim), F32),
        grid=(p_dim // tp,),
        in_specs=[pl.BlockSpec((N_DEV, tp, c_dim), lambda i: (0, i, 0))],
        out_specs=pl.BlockSpec((tp, c_dim), lambda i: (i, 0)),
        compiler_params=pltpu.CompilerParams(dimension_semantics=("parallel",)),
    )(slots)


def adamw(w, g, m, v, name):
    rows, cols = w.shape
    tr = _pick(rows, (256, 128, 64, 32, 16, 8))

    def body(w_ref, g_ref, m_ref, v_ref, d_ref, nm_ref, nv_ref):
        gv = g_ref[...]
        nm = ADAM_B1 * m_ref[...] + (1.0 - ADAM_B1) * gv
        nv = ADAM_B2 * v_ref[...] + (1.0 - ADAM_B2) * (gv * gv)
        m_hat = nm / (1.0 - ADAM_B1 ** ADAM_STEP)
        v_hat = nv / (1.0 - ADAM_B2 ** ADAM_STEP)
        d_ref[...] = -ADAM_LR * (m_hat / (jnp.sqrt(v_hat) + ADAM_EPS) + ADAM_WD * w_ref[...])
        nm_ref[...] = nm
        nv_ref[...] = nv

    spec = pl.BlockSpec((tr, cols), lambda i: (i, 0))
    shp = jax.ShapeDtypeStruct((rows, cols), F32)
    return pl.pallas_call(
        body, name=name, out_shape=(shp, shp, shp), grid=(rows // tr,),
        in_specs=[spec] * 4, out_specs=(spec,) * 3,
        compiler_params=pltpu.CompilerParams(dimension_semantics=("parallel",)),
    )(w, g, m, v)


ATT_BLK = 128
NEG = -1e30


def _head_sums(v):
    li = lax.broadcasted_iota(jnp.int32, (LANES, LANES), 0) // ATT_HEAD_DIM
    lj = lax.broadcasted_iota(jnp.int32, (LANES, LANES), 1) // ATT_HEAD_DIM
    ones = (li == lj).astype(BF16)
    hi = v.astype(BF16)
    lo = (v - hi.astype(F32)).astype(BF16)
    return jnp.dot(hi, ones, preferred_element_type=F32) + jnp.dot(lo, ones, preferred_element_type=F32)


def _head_col(v, hmask):
    return jnp.max(jnp.where(hmask, v, -jnp.inf), axis=-1, keepdims=True)


def _qk_norm(raw, gain2):
    rstd = lax.rsqrt(_head_sums(raw * raw) * (1.0 / ATT_HEAD_DIM) + NORM_EPS)
    xhat = raw * rstd
    return xhat * gain2, xhat, rstd


def _qk_norm_bwd(dn, xhat, rstd, gain2):
    dxh = dn * gain2
    return rstd * (dxh - xhat * (_head_sums(dxh * xhat) * (1.0 / ATT_HEAD_DIM))), dn * xhat


def _att_mask_bias(n, dilation):
    qi = lax.broadcasted_iota(jnp.int32, (ATT_BLK, 2 * ATT_BLK), 0)
    ki = lax.broadcasted_iota(jnp.int32, (ATT_BLK, 2 * ATT_BLK), 1)
    dist = qi + ATT_BLK - ki
    valid = (dist >= 0) & (dist <= ATT_BLK) & ((n > 0) | (ki >= ATT_BLK))
    return valid, (dilation * dist).astype(F32)


ATT_PAIRS = 8
RELAYOUT_ROWS = 512
RELAYOUT_COLS = 1024


def _to_residues(x, dilation, col0=0, cols=None):
    t = x.shape[0]
    cols = x.shape[1] if cols is None else cols
    if dilation == 1 and col0 == 0 and cols == x.shape[1]:
        return x.reshape(1, t, cols)
    tr = _pick(t, (RELAYOUT_ROWS,))
    tc = _pick(cols, (RELAYOUT_COLS, 256, 128))
    per = tr // dilation
    assert tr % dilation == 0 and col0 % tc == 0

    def body(x_ref, o_ref, s_ref):
        for c in range(tc // LANES):
            lanes = slice(c * LANES, (c + 1) * LANES)
            s_ref[c] = x_ref[:, lanes].astype(F32)
            for r in range(dilation):
                o_ref[r, :, lanes] = s_ref[c, pl.ds(r, per, stride=dilation), :].astype(o_ref.dtype)

    return pl.pallas_call(
        body, name=f"to_residues_{dilation}", out_shape=jax.ShapeDtypeStruct((dilation, t // dilation, cols), x.dtype),
        grid=(t // tr, cols // tc),
        in_specs=[pl.BlockSpec((tr, tc), lambda i, j: (i, col0 // tc + j))],
        out_specs=pl.BlockSpec((dilation, per, tc), lambda i, j: (0, i, j)),
        scratch_shapes=[pltpu.VMEM((tc // LANES, tr, LANES), F32)],
        compiler_params=pltpu.CompilerParams(dimension_semantics=("parallel", "parallel")),
    )(x)


def _from_residues(y):
    dilation, lu, cols = y.shape
    t = dilation * lu
    if dilation == 1:
        return y.reshape(t, cols)
    tr = _pick(t, (RELAYOUT_ROWS,))
    tc = _pick(cols, (RELAYOUT_COLS, 256, 128))
    per = tr // dilation

    def body(y_ref, o_ref, s_ref):
        for c in range(tc // LANES):
            lanes = slice(c * LANES, (c + 1) * LANES)
            for r in range(dilation):
                s_ref[c, pl.ds(r, per, stride=dilation), :] = y_ref[r, :, lanes].astype(F32)
            o_ref[:, lanes] = s_ref[c].astype(o_ref.dtype)

    return pl.pallas_call(
        body, name=f"from_residues_{dilation}", out_shape=jax.ShapeDtypeStruct((t, cols), y.dtype),
        grid=(t // tr, cols // tc),
        in_specs=[pl.BlockSpec((dilation, per, tc), lambda i, j: (0, i, j))],
        out_specs=pl.BlockSpec((tr, tc), lambda i, j: (i, j)),
        scratch_shapes=[pltpu.VMEM((tc // LANES, tr, LANES), F32)],
        compiler_params=pltpu.CompilerParams(dimension_semantics=("parallel", "parallel")),
    )(y)


def _att_specs(hd, v_base, nb, pp):
    width = pp * LANES
    assert hd % width == 0 and v_base % width == 0

    def spec(base, shift):
        def imap(r, hp, n):
            row = jnp.minimum(n, nb - 1) if shift == 0 else jnp.maximum(n - 1, 0)
            return (r, row, base // width + hp)
        return pl.BlockSpec((None, ATT_BLK, width), imap)

    return [spec(0, 0), spec(hd, 1), spec(hd, 0), spec(v_base, 1), spec(v_base, 0)]


DELTA_LANE = 64


def _att_pre(qkv, g, dilation, gq2, gk2):
    t, width = qkv.shape
    hd = width // (3 * len(DIL_PATTERNS))
    tr = _pick(t, (RELAYOUT_ROWS,))
    tc = _pick(hd, (RELAYOUT_COLS, 256, 128))
    per = tr // dilation
    assert tr % dilation == 0 and (g * 3 * hd) % tc == 0

    def body(x_ref, gq_ref, gk_ref, o_ref, s_ref):
        gain = jnp.where(pl.program_id(1) * tc < hd, gq_ref[0:1, :], gk_ref[0:1, :])
        for c in range(tc // LANES):
            lanes = slice(c * LANES, (c + 1) * LANES)
            s_ref[c] = _qk_norm(x_ref[:, lanes].astype(F32), gain)[0]
            for r in range(dilation):
                o_ref[r, :, lanes] = s_ref[c, pl.ds(r, per, stride=dilation), :].astype(o_ref.dtype)

    vec_spec = pl.BlockSpec((SUBLANES, LANES), lambda i, j: (0, 0))
    return pl.pallas_call(
        body, name=f"att_pre_g{g}", out_shape=jax.ShapeDtypeStruct((dilation, t // dilation, 2 * hd), qkv.dtype),
        grid=(t // tr, 2 * hd // tc),
        in_specs=[pl.BlockSpec((tr, tc), lambda i, j: (i, g * 3 * hd // tc + j)), vec_spec, vec_spec],
        out_specs=pl.BlockSpec((dilation, per, tc), lambda i, j: (0, i, j)),
        scratch_shapes=[pltpu.VMEM((tc // LANES, tr, LANES), F32)],
        compiler_params=pltpu.CompilerParams(dimension_semantics=("parallel", "parallel")),
    )(qkv, gq2, gk2)


def _att_group_fwd(view, hd, slopes, g):
    qk_r, v_r, v_base = view
    dilation, lu, _ = qk_r.shape
    nb = lu // ATT_BLK
    assert nb * ATT_BLK == lu and hd % LANES == 0
    hpn = hd // LANES
    pp = math.gcd(ATT_PAIRS, hpn)
    scale = 1.0 / math.sqrt(ATT_HEAD_DIM)

    def body(q_ref, kp_ref, kc_ref, vp_ref, vc_ref, sl_ref, o_ref, l_ref):
        n = pl.program_id(2)
        lane = lax.broadcasted_iota(jnp.int32, (1, LANES), 1)
        first = (lane // ATT_HEAD_DIM) == 0
        valid, dist = _att_mask_bias(n, dilation)
        stats = jnp.zeros((ATT_BLK, LANES), F32)
        for pair in range(pp):
            cols = slice(pair * LANES, (pair + 1) * LANES)
            qn = q_ref[:, cols]
            kn16 = jnp.concatenate([kp_ref[:, cols], kc_ref[:, cols]], axis=0)
            v16 = jnp.concatenate([vp_ref[:, cols], vc_ref[:, cols]], axis=0)
            outs = []
            for hh in range(2):
                hmask = (lane // ATT_HEAD_DIM) == hh
                qh = jnp.where(hmask, qn, jnp.zeros_like(qn))
                s = lax.dot_general(qh, kn16, (((1,), (1,)), ((), ())), preferred_element_type=F32) * scale
                slope = _head_col(sl_ref[pair, 0:1, :], hmask)
                logits = jnp.where(valid, s - slope * dist, NEG)
                mx = jnp.max(logits, axis=-1, keepdims=True)
                pexp = jnp.exp(logits - mx)
                den = jnp.sum(pexp, axis=-1, keepdims=True)
                outs.append(jnp.dot(pexp.astype(BF16), v16, preferred_element_type=F32) / den)
                stats = jnp.where(lane == 2 * pair + hh, mx + jnp.log(den), stats)
            o_ref[:, cols] = jnp.where(first, outs[0], outs[1]).astype(BF16)
        l_ref[...] = stats

    out_spec = pl.BlockSpec((None, ATT_BLK, pp * LANES), lambda r, hp, n: (r, n, hp))
    stat_spec = pl.BlockSpec((None, ATT_BLK, LANES), lambda r, hp, n: (r, n, hp))
    o, lse = pl.pallas_call(
        body, name=f"att_fwd_g{g}",
        out_shape=(jax.ShapeDtypeStruct((dilation, lu, hd), BF16),
                   jax.ShapeDtypeStruct((dilation, lu, hpn // pp * LANES), F32)),
        grid=(dilation, hpn // pp, nb),
        in_specs=_att_specs(hd, v_base, nb, pp) + [pl.BlockSpec((pp, SUBLANES, LANES), lambda r, hp, n: (hp, 0, 0))],
        out_specs=(out_spec, stat_spec),
        compiler_params=pltpu.CompilerParams(dimension_semantics=("parallel", "parallel", "arbitrary")),
    )(qk_r, qk_r, qk_r, v_r, v_r, slopes)
    return _from_residues(o), _from_residues(lse)


def _att_merge(outs, lses):
    t, hd = outs[0].shape
    sw = lses[0].shape[1]
    pp = hd // sw
    tr = _pick(t, (512, 256, 128))
    ng = len(outs)

    def body(*refs):
        o_refs, l_refs, o16_ref, lt_ref = refs[:ng], refs[ng:2 * ng], refs[2 * ng], refs[2 * ng + 1]
        lane = lax.broadcasted_iota(jnp.int32, (1, LANES), 1)
        first = (lane // ATT_HEAD_DIM) == 0
        for blk in range(sw // LANES):
            scols = slice(blk * LANES, (blk + 1) * LANES)
            stats = jnp.zeros((tr, LANES), F32)
            for pair in range(pp):
                cols = slice((blk * pp + pair) * LANES, (blk * pp + pair + 1) * LANES)
                weights = []
                for hh in range(2):
                    pick = lane == 2 * pair + hh
                    ls = [_head_col(r[:, scols], pick) for r in l_refs]
                    mx = functools.reduce(jnp.maximum, ls)
                    es = [jnp.exp(l - mx) for l in ls]
                    den = functools.reduce(jnp.add, es)
                    weights.append([e / den for e in es])
                    stats = jnp.where(pick, mx + jnp.log(den), stats)
                acc = jnp.zeros((tr, LANES), F32)
                for gi in range(ng):
                    acc = acc + jnp.where(first, weights[0][gi], weights[1][gi]) * o_refs[gi][:, cols].astype(F32)
                o16_ref[:, cols] = acc.astype(BF16)
            lt_ref[:, scols] = stats

    spec = pl.BlockSpec((tr, hd), lambda i: (i, 0))
    sspec = pl.BlockSpec((tr, sw), lambda i: (i, 0))
    return pl.pallas_call(
        body, name="att_merge",
        out_shape=(jax.ShapeDtypeStruct((t, hd), BF16), jax.ShapeDtypeStruct((t, sw), F32)), grid=(t // tr,),
        in_specs=[spec] * ng + [sspec] * ng, out_specs=(spec, sspec),
        compiler_params=pltpu.CompilerParams(dimension_semantics=("parallel",)),
    )(*outs, *lses)


def _att_bwd_prep(do, o16, lse_tot):
    t, hd = do.shape
    sw = lse_tot.shape[1]
    pp = hd // sw
    tr = _pick(t, (256, 128))

    def body(do_ref, o_ref, l_ref, d16_ref, st_ref):
        lane = lax.broadcasted_iota(jnp.int32, (1, LANES), 1)
        d16_ref[...] = do_ref[...].astype(BF16)
        for blk in range(sw // LANES):
            scols = slice(blk * LANES, (blk + 1) * LANES)
            stats = l_ref[:, scols]
            for pair in range(pp):
                cols = slice((blk * pp + pair) * LANES, (blk * pp + pair + 1) * LANES)
                prod = do_ref[:, cols] * o_ref[:, cols].astype(F32)
                for hh in range(2):
                    hmask = (lane // ATT_HEAD_DIM) == hh
                    delta = jnp.sum(jnp.where(hmask, prod, 0.0), axis=-1, keepdims=True)
                    stats = jnp.where(lane == DELTA_LANE + 2 * pair + hh, delta, stats)
            st_ref[:, scols] = stats

    spec = pl.BlockSpec((tr, hd), lambda i: (i, 0))
    sspec = pl.BlockSpec((tr, sw), lambda i: (i, 0))
    return pl.pallas_call(
        body, name="att_bwd_prep",
        out_shape=(jax.ShapeDtypeStruct((t, hd), BF16), jax.ShapeDtypeStruct((t, sw), F32)), grid=(t // tr,),
        in_specs=[spec, spec, sspec], out_specs=(spec, sspec),
        compiler_params=pltpu.CompilerParams(dimension_semantics=("parallel",)),
    )(do, o16, lse_tot)


def _att_post(buf, parts, qkv, gq2, gk2, g):
    dilation, lu, hd = parts[0].shape
    t = dilation * lu
    tr = _pick(t, (RELAYOUT_ROWS,))
    per = tr // dilation

    def body(*refs):
        raw_ref, gq_ref, gk_ref = refs[3:6]
        o_ref, dgq_ref, dgk_ref, s_ref = refs[-4:]

        @pl.when(pl.program_id(0) == 0)
        def _():
            dgq_ref[...] = jnp.zeros_like(dgq_ref)
            dgk_ref[...] = jnp.zeros_like(dgk_ref)

        for sec, y_ref in enumerate(refs[:3]):
            gsum = jnp.zeros((1, LANES), F32)
            for c in range(hd // LANES):
                lanes = slice(c * LANES, (c + 1) * LANES)
                out_lanes = slice(sec * hd + c * LANES, sec * hd + (c + 1) * LANES)
                for r in range(dilation):
                    s_ref[pl.ds(r, per, stride=dilation), :] = y_ref[r, :, lanes].astype(F32)
                d = s_ref[...]
                if sec < 2:
                    gain = (gq_ref if sec == 0 else gk_ref)[0:1, :]
                    _, xhat, rstd = _qk_norm(raw_ref[:, out_lanes].astype(F32), gain)
                    d, part = _qk_norm_bwd(d, xhat, rstd, gain)
                    gsum = gsum + jnp.sum(part, axis=0, keepdims=True)
                o_ref[:, out_lanes] = d.astype(o_ref.dtype)
            if sec < 2:
                acc = dgq_ref if sec == 0 else dgk_ref
                acc[...] += jnp.broadcast_to(gsum, acc.shape)

    part_spec = pl.BlockSpec((dilation, per, hd), lambda i: (0, i, 0))
    slab_spec = pl.BlockSpec((tr, 3 * hd), lambda i: (i, g))
    vec_spec = pl.BlockSpec((SUBLANES, LANES), lambda i: (0, 0))
    vec_shape = jax.ShapeDtypeStruct((SUBLANES, LANES), F32)
    return pl.pallas_call(
        body, name=f"att_post_g{g}", out_shape=(jax.ShapeDtypeStruct(qkv.shape, qkv.dtype), vec_shape, vec_shape),
        grid=(t // tr,),
        in_specs=[part_spec] * 3 + [slab_spec, vec_spec, vec_spec] + (
            [] if buf is None else [pl.BlockSpec(memory_space=pl.ANY)]),
        out_specs=(slab_spec, vec_spec, vec_spec),
        scratch_shapes=[pltpu.VMEM((tr, LANES), F32)],
        input_output_aliases={} if buf is None else {6: 0},
        compiler_params=pltpu.CompilerParams(dimension_semantics=("arbitrary",)),
    )(*parts, qkv, gq2, gk2, *(() if buf is None else (buf,)))


def _att_group_bwd(view, hd, slopes, stats, do16, g):
    qk_r, v_r, v_base = view
    dilation, lu, _ = qk_r.shape
    nb = lu // ATT_BLK
    hpn = hd // LANES
    pp = math.gcd(ATT_PAIRS, hpn)
    hbn = hpn // pp
    scale = 1.0 / math.sqrt(ATT_HEAD_DIM)

    def body(q_ref, kp_ref, kc_ref, vp_ref, vc_ref, sl_ref, st_ref, do_ref, dq_ref, dk_ref, dv_ref, ck_ref, cv_ref):
        n = pl.program_id(2)
        lane = lax.broadcasted_iota(jnp.int32, (1, LANES), 1)

        @pl.when(n == 0)
        def _():
            ck_ref[...] = jnp.zeros_like(ck_ref)
            cv_ref[...] = jnp.zeros_like(cv_ref)

        @pl.when(n < nb)
        def _():
            valid, dist = _att_mask_bias(n, dilation)
            stats = st_ref[...]
            for pair in range(pp):
                cols = slice(pair * LANES, (pair + 1) * LANES)
                qn = q_ref[:, cols]
                kn16 = jnp.concatenate([kp_ref[:, cols], kc_ref[:, cols]], axis=0)
                v16 = jnp.concatenate([vp_ref[:, cols], vc_ref[:, cols]], axis=0)
                dov = do_ref[:, cols]
                dq_acc = jnp.zeros((ATT_BLK, LANES), F32)
                dk_acc = jnp.zeros((2 * ATT_BLK, LANES), F32)
                dv_acc = jnp.zeros((2 * ATT_BLK, LANES), F32)
                for hh in range(2):
                    hmask = (lane // ATT_HEAD_DIM) == hh
                    qh = jnp.where(hmask, qn, jnp.zeros_like(qn))
                    doh = jnp.where(hmask, dov, jnp.zeros_like(dov))
                    s = lax.dot_general(qh, kn16, (((1,), (1,)), ((), ())), preferred_element_type=F32) * scale
                    slope = _head_col(sl_ref[pair, 0:1, :], hmask)
                    lse = _head_col(stats, lane == 2 * pair + hh)
                    delta = _head_col(stats, lane == DELTA_LANE + 2 * pair + hh)
                    pr = jnp.exp(jnp.where(valid, s - slope * dist - lse, NEG))
                    dp = lax.dot_general(doh, v16, (((1,), (1,)), ((), ())), preferred_element_type=F32)
                    ds = (pr * (dp - delta) * scale).astype(BF16)
                    dq_acc = dq_acc + jnp.where(hmask, jnp.dot(ds, kn16, preferred_element_type=F32), 0.0)
                    dk_acc = dk_acc + lax.dot_general(ds, qh, (((0,), (0,)), ((), ())), preferred_element_type=F32)
                    dv_acc = dv_acc + lax.dot_general(pr.astype(BF16), doh, (((0,), (0,)), ((), ())),
                                                      preferred_element_type=F32)
                dq_ref[:, cols] = dq_acc.astype(dq_ref.dtype)
                dk_ref[:, cols] = (ck_ref[:, cols] + dk_acc[:ATT_BLK]).astype(dk_ref.dtype)
                dv_ref[:, cols] = (cv_ref[:, cols] + dv_acc[:ATT_BLK]).astype(dv_ref.dtype)
                ck_ref[:, cols] = dk_acc[ATT_BLK:]
                cv_ref[:, cols] = dv_acc[ATT_BLK:]

        @pl.when(n == nb)
        def _():
            dk_ref[...] = ck_ref[...].astype(dk_ref.dtype)
            dv_ref[...] = cv_ref[...].astype(dv_ref.dtype)

    width = pp * LANES
    q_out = pl.BlockSpec((None, ATT_BLK, width), lambda r, hp, n: (r, jnp.minimum(n, nb - 1), hp))
    kv_out = pl.BlockSpec((None, ATT_BLK, width), lambda r, hp, n: (r, jnp.maximum(n - 1, 0), hp))
    st_spec = pl.BlockSpec((None, ATT_BLK, LANES), lambda r, hp, n: (r, jnp.minimum(n, nb - 1), hp))
    shp = jax.ShapeDtypeStruct((dilation, lu, hd), BF16)
    return pl.pallas_call(
        body, name=f"att_bwd_g{g}", out_shape=(shp, shp, shp), grid=(dilation, hbn, nb + 1),
        in_specs=_att_specs(hd, v_base, nb, pp) + [
            pl.BlockSpec((pp, SUBLANES, LANES), lambda r, hp, n: (hp, 0, 0)), st_spec, q_out],
        out_specs=(q_out, kv_out, kv_out),
        scratch_shapes=[pltpu.VMEM((ATT_BLK, width), F32), pltpu.VMEM((ATT_BLK, width), F32)],
        compiler_params=pltpu.CompilerParams(dimension_semantics=("parallel", "parallel", "arbitrary")),
    )(qk_r, qk_r, qk_r, v_r, v_r, slopes, _to_residues(stats, dilation), _to_residues(do16, dilation))


def _att_consts(q_gain, k_gain, hd):
    heads = hd // ATT_HEAD_DIM
    gq2 = jnp.broadcast_to(jnp.tile(q_gain, 2)[None], (SUBLANES, LANES))
    gk2 = jnp.broadcast_to(jnp.tile(k_gain, 2)[None], (SUBLANES, LANES))
    sl = 2.0 ** (-8.0 * jnp.arange(1, heads + 1, dtype=F32) / heads)
    slopes = jnp.broadcast_to(jnp.repeat(sl, ATT_HEAD_DIM).reshape(hd // LANES, 1, LANES), (hd // LANES, SUBLANES, LANES))
    return gq2, gk2, slopes


def _attention_core_fwd(qkv, q_gain, k_gain):
    hd = qkv.shape[1] // (3 * len(DIL_PATTERNS))
    gq2, gk2, slopes = _att_consts(q_gain, k_gain, hd)
    outs, lses, views = [], [], []
    for g, (_, dilation) in enumerate(DIL_PATTERNS):
        v_col = (3 * g + 2) * hd
        view = (_att_pre(qkv, g, dilation, gq2, gk2),) + (
            (_to_residues(qkv, 1), v_col) if dilation == 1 else (_to_residues(qkv, dilation, v_col, hd), 0))
        o_g, l_g = _att_group_fwd(view, hd, slopes, g)
        outs.append(o_g)
        lses.append(l_g)
        views.append(view)
    o16, lse_tot = _att_merge(outs, lses)
    return o16, (qkv, views, q_gain, k_gain, o16, lse_tot)


def _attention_core_bwd(res, do):
    qkv, views, q_gain, k_gain, o16, lse_tot = res
    hd = o16.shape[1]
    gq2, gk2, slopes = _att_consts(q_gain, k_gain, hd)
    do16, stats = _att_bwd_prep(do, o16, lse_tot)
    dqkv, dgq, dgk = None, 0.0, 0.0
    for g, view in enumerate(views):
        parts = _att_group_bwd(view, hd, slopes, stats, do16, g)
        dqkv, a, b = _att_post(dqkv, parts, qkv, gq2, gk2, g)
        dgq = dgq + a[0].reshape(-1, ATT_HEAD_DIM).sum(0)
        dgk = dgk + b[0].reshape(-1, ATT_HEAD_DIM).sum(0)
    return dqkv, dgq, dgk


HALO = 8


def _silu(v):
    return v * jax.nn.sigmoid(v)


def _silu_grad(v):
    s = jax.nn.sigmoid(v)
    return s * (1.0 + v * (1.0 - s))


def _halo_rows(dtype):
    return BF16_ROWS if dtype == BF16 else HALO


def _conv_fwd(zx, conv_w, conv_b, d_inner):
    t = zx.shape[0]
    conv_dim = conv_w.shape[1]
    cb = _pick(d_inner, (1024, 512, 256, 128))
    assert conv_dim % cb == 0
    tr = _pick(t, (512, 256, 128))
    off = d_inner // cb
    hx = _halo_rows(zx.dtype)

    def body(x_ref, h_ref, w_ref, b_ref, o_ref):
        i = pl.program_id(1)
        halo = jnp.where(i > 0, h_ref[...].astype(F32), 0.0)
        ext = jnp.concatenate([halo, x_ref[...].astype(F32)], axis=0)
        acc = jnp.broadcast_to(b_ref[...], (tr, cb))
        for k in range(CONV_WIDTH):
            s = CONV_WIDTH - 1 - k
            sh = ext if s == 0 else pltpu.roll(ext, shift=s, axis=0)
            acc = acc + w_ref[k:k + 1, :] * sh[hx:hx + tr]
        o_ref[...] = acc.astype(o_ref.dtype)

    return pl.pallas_call(
        body, name="ssm_conv_fwd", out_shape=jax.ShapeDtypeStruct((t, conv_dim), BF16),
        grid=(conv_dim // cb, t // tr),
        in_specs=[pl.BlockSpec((tr, cb), lambda j, i: (i, off + j)),
                  pl.BlockSpec((hx, cb), lambda j, i: (jnp.maximum(i * (tr // hx) - 1, 0), off + j)),
                  pl.BlockSpec((CONV_WIDTH, cb), lambda j, i: (0, j)),
                  pl.BlockSpec((1, cb), lambda j, i: (0, j))],
        out_specs=pl.BlockSpec((tr, cb), lambda j, i: (i, j)),
        compiler_params=pltpu.CompilerParams(dimension_semantics=("parallel", "parallel")),
    )(zx, zx, conv_w, conv_b.reshape(1, -1))


def _conv_bwd(zx, conv_w, dpre, dzx, d_inner, col0):
    t, width = zx.shape
    conv_dim = dpre.shape[1]
    cb = _pick(conv_dim, (1024, 512, 256, 128))
    assert (d_inner + col0) % cb == 0
    tr = _pick(t, (512, 256, 128))
    off = (d_inner + col0) // cb
    woff = col0 // cb
    nr = t // tr
    hx = _halo_rows(zx.dtype)
    hd = _halo_rows(dpre.dtype)

    def body(x_ref, h_ref, w_ref, d_ref, dn_ref, dzx_in, dx_ref, dw_ref, db_ref):
        i = pl.program_id(1)

        @pl.when(i == 0)
        def _():
            dw_ref[...] = jnp.zeros_like(dw_ref)
            db_ref[...] = jnp.zeros_like(db_ref)

        halo = jnp.where(i > 0, h_ref[...].astype(F32), 0.0)
        ext = jnp.concatenate([halo, x_ref[...].astype(F32)], axis=0)
        d = d_ref[...].astype(F32)
        dext = jnp.concatenate([d, jnp.where(i < nr - 1, dn_ref[...].astype(F32), 0.0)], axis=0)
        dx = jnp.zeros((tr, cb), F32)
        for k in range(CONV_WIDTH):
            s = CONV_WIDTH - 1 - k
            fut = dext if s == 0 else pltpu.roll(dext, shift=tr + hd - s, axis=0)
            dx = dx + w_ref[k:k + 1, :] * fut[:tr]
            past = ext if s == 0 else pltpu.roll(ext, shift=s, axis=0)
            dw_ref[k:k + 1, :] += jnp.sum(d * past[hx:hx + tr], axis=0, keepdims=True)
        dx_ref[...] = dx.astype(dx_ref.dtype)
        db_ref[...] += jnp.sum(d, axis=0, keepdims=True)

    last_halo = t // hd - 1
    return pl.pallas_call(
        body, name=f"ssm_conv_bwd_{col0}",
        out_shape=(jax.ShapeDtypeStruct(dzx.shape, dzx.dtype), jax.ShapeDtypeStruct((CONV_WIDTH, conv_dim), F32),
                   jax.ShapeDtypeStruct((1, conv_dim), F32)),
        grid=(conv_dim // cb, nr),
        in_specs=[pl.BlockSpec((tr, cb), lambda j, i: (i, off + j)),
                  pl.BlockSpec((hx, cb), lambda j, i: (jnp.maximum(i * (tr // hx) - 1, 0), off + j)),
                  pl.BlockSpec((CONV_WIDTH, cb), lambda j, i: (0, woff + j)),
                  pl.BlockSpec((tr, cb), lambda j, i: (i, j)),
                  pl.BlockSpec((hd, cb), lambda j, i: (jnp.minimum((i + 1) * (tr // hd), last_halo), j)),
                  pl.BlockSpec(memory_space=pl.ANY)],
        out_specs=(pl.BlockSpec((tr, cb), lambda j, i: (i, off + j)),
                   pl.BlockSpec((CONV_WIDTH, cb), lambda j, i: (0, j)),
                   pl.BlockSpec((1, cb), lambda j, i: (0, j))),
        input_output_aliases={5: 0},
        compiler_params=pltpu.CompilerParams(dimension_semantics=("parallel", "arbitrary")),
    )(zx, zx, conv_w, dpre, dpre, dzx)


def _eye(n):
    return lax.broadcasted_iota(jnp.int32, (n, n), 0) == lax.broadcasted_iota(jnp.int32, (n, n), 1)


def _row_to_col(row):
    n = row.shape[1]
    return jnp.sum(jnp.where(_eye(n), row, 0.0), axis=1, keepdims=True)


def _col_to_row(col):
    n = col.shape[0]
    return jnp.sum(jnp.where(_eye(n), col, 0.0), axis=0, keepdims=True)


def _pair_lanes(c0, c1):
    lane = lax.broadcasted_iota(jnp.int32, (1, LANES), 1)
    return jnp.where(lane < SSM_HEAD_DIM, c0, c1)


def _ssd_chunk_common(pre_x_ref, pre_b_ref, pre_c_ref, dtr_ref, bias_ref, alog_ref, cs_ref):
    cl = SSD_CHUNK
    hpg = dtr_ref.shape[0]
    x = _silu(pre_x_ref[...].astype(F32))
    b16 = _silu(pre_b_ref[...].astype(F32)).astype(BF16)
    c16 = _silu(pre_c_ref[...].astype(F32)).astype(BF16)
    dt = jax.nn.softplus(dtr_ref[...] + bias_ref[...])
    a = -jnp.exp(alog_ref[...])
    li = lax.broadcasted_iota(jnp.int32, (cl, cl), 0)
    si = lax.broadcasted_iota(jnp.int32, (cl, cl), 1)
    upper = (li <= si).astype(F32)
    cs_ref[0:hpg, :] = jnp.dot(dt * a, upper, precision=lax.Precision.HIGHEST, preferred_element_type=F32)
    cs_ref[hpg:2 * hpg, :] = dt
    g = lax.dot_general(c16, b16, (((1,), (1,)), ((), ())), preferred_element_type=F32)
    return x, b16, c16, dt, a, g, li >= si


def _ssd_fwd(pre, dtT, bias, alog, dskip_lanes, d_inner):
    t = pre.shape[0]
    cl = SSD_CHUNK
    nc = t // cl
    ng = SSM_GROUPS
    hpg = dtT.shape[1]
    gw = hpg * SSM_HEAD_DIM
    assert d_inner == ng * gw and hpg % 2 == 0
    bo = d_inner // SSM_STATE

    def body(px_ref, pb_ref, pc_ref, dtr_ref, bias_ref, alog_ref, dsk_ref, y_ref, st_ref, s_ref, cs_ref):
        c = pl.program_id(1)

        @pl.when(c == 0)
        def _():
            s_ref[...] = jnp.zeros_like(s_ref)

        x, b16, c16, dt, a, g, causal = _ssd_chunk_common(px_ref, pb_ref, pc_ref, dtr_ref, bias_ref, alog_ref, cs_ref)
        st_ref[...] = s_ref[...]
        yoff = lax.dot_general(c16, s_ref[...].astype(BF16), (((1,), (1,)), ((), ())), preferred_element_type=F32)
        xde_parts = []
        for j in range(hpg // 2):
            cols = slice(j * LANES, (j + 1) * LANES)
            xp = x[:, cols]
            dcol, ecol, ocol, ms = [], [], [], []
            for hh in range(2):
                h = 2 * j + hh
                cs_row = cs_ref[h:h + 1, :]
                cs_col = _row_to_col(cs_row)
                dcol.append(_row_to_col(cs_ref[hpg + h:hpg + h + 1, :]))
                ecol.append(jnp.exp(cs_ref[h:h + 1, cl - 1:cl] - cs_col))
                ocol.append(jnp.exp(cs_col))
                lm = jnp.where(causal, jnp.exp(jnp.minimum(cs_col - cs_row, 0.0)), 0.0)
                ms.append((g * lm).astype(BF16))
            xd = xp * _pair_lanes(dcol[0], dcol[1])
            xd16 = xd.astype(BF16)
            yd = _pair_lanes(1.0, 0.0) * jnp.dot(ms[0], xd16, preferred_element_type=F32) \
                + _pair_lanes(0.0, 1.0) * jnp.dot(ms[1], xd16, preferred_element_type=F32)
            y_ref[:, cols] = yd + yoff[:, cols] * _pair_lanes(ocol[0], ocol[1]) + xp * dsk_ref[0:1, cols]
            xde_parts.append((xd * _pair_lanes(ecol[0], ecol[1])).astype(BF16))
        new = lax.dot_general(jnp.concatenate(xde_parts, axis=1), b16, (((0,), (0,)), ((), ())),
                              preferred_element_type=F32)
        for h in range(hpg):
            rows = slice(h * SSM_HEAD_DIM, (h + 1) * SSM_HEAD_DIM)
            s_ref[rows, :] = s_ref[rows, :] * jnp.exp(cs_ref[h:h + 1, cl - 1:cl]) + new[rows, :]

    vec = lambda n: pl.BlockSpec((None, hpg, n), lambda gi, c: (gi, 0, 0))
    return pl.pallas_call(
        body, name="ssd_fwd",
        out_shape=(jax.ShapeDtypeStruct((t, d_inner), F32), jax.ShapeDtypeStruct((ng, nc, gw, SSM_STATE), F32)),
        grid=(ng, nc),
        in_specs=[pl.BlockSpec((cl, gw), lambda gi, c: (c, gi)),
                  pl.BlockSpec((cl, SSM_STATE), lambda gi, c: (c, bo + gi)),
                  pl.BlockSpec((cl, SSM_STATE), lambda gi, c: (c, bo + ng + gi)),
                  pl.BlockSpec((None, hpg, cl), lambda gi, c: (gi, 0, c)),
                  vec(1), vec(1),
                  pl.BlockSpec((1, gw), lambda gi, c: (0, gi))],
        out_specs=(pl.BlockSpec((cl, gw), lambda gi, c: (c, gi)),
                   pl.BlockSpec((None, None, gw, SSM_STATE), lambda gi, c: (gi, c, 0, 0))),
        scratch_shapes=[pltpu.VMEM((gw, SSM_STATE), F32), pltpu.VMEM((2 * hpg, cl), F32)],
        compiler_params=pltpu.CompilerParams(dimension_semantics=("parallel", "arbitrary")),
    )(pre, pre, pre, dtT, bias, alog, dskip_lanes)


def _ssd_bwd(pre, dtT, bias, alog, dskip_lanes, states, dy, d_inner):
    t, conv_dim = pre.shape
    cl = SSD_CHUNK
    nc = t // cl
    ng = SSM_GROUPS
    hpg = dtT.shape[1]
    gw = hpg * SSM_HEAD_DIM
    bo = d_inner // SSM_STATE

    def body(px_ref, pb_ref, pc_ref, dtr_ref, bias_ref, alog_ref, dsk_ref, st_ref, dy_ref,
             dx_ref, db_ref, dc_ref, ddt_ref, acc_ref, dsk_out, ds_ref, cs_ref, dcs_ref):
        c = pl.program_id(1)

        @pl.when(c == 0)
        def _():
            ds_ref[...] = jnp.zeros_like(ds_ref)
            acc_ref[...] = jnp.zeros_like(acc_ref)
            dsk_out[...] = jnp.zeros_like(dsk_out)

        x, b16, c16, dt, a, g, causal = _ssd_chunk_common(px_ref, pb_ref, pc_ref, dtr_ref, bias_ref, alog_ref, cs_ref)
        s_prev = st_ref[...]
        s16 = s_prev.astype(BF16)
        ds = ds_ref[...]
        ds16 = ds.astype(BF16)
        dyv = dy_ref[...]
        yoff = lax.dot_general(c16, s16, (((1,), (1,)), ((), ())), preferred_element_type=F32)
        bds = lax.dot_general(b16, ds16, (((1,), (1,)), ((), ())), preferred_element_type=F32)
        dg = jnp.zeros((cl, cl), F32)
        xde_parts, dye_parts = [], []
        lane = lax.broadcasted_iota(jnp.int32, (1, LANES), 1)
        for j in range(hpg // 2):
            cols = slice(j * LANES, (j + 1) * LANES)
            xp, dyp = x[:, cols], dyv[:, cols]
            dcol, ecol, ocol, lms = [], [], [], []
            for hh in range(2):
                h = 2 * j + hh
                cs_row = cs_ref[h:h + 1, :]
                cs_col = _row_to_col(cs_row)
                dcol.append(_row_to_col(cs_ref[hpg + h:hpg + h + 1, :]))
                ecol.append(jnp.exp(cs_ref[h:h + 1, cl - 1:cl] - cs_col))
                ocol.append(jnp.exp(cs_col))
                lms.append(jnp.where(causal, jnp.exp(jnp.minimum(cs_col - cs_row, 0.0)), 0.0))
            dlanes, elanes, olanes = _pair_lanes(*dcol), _pair_lanes(*ecol), _pair_lanes(*ocol)
            xd = xp * dlanes
            xd16 = xd.astype(BF16)
            xde = xd * elanes
            yoffp = yoff[:, cols] * olanes
            bdsp = bds[:, cols]
            dxd = bdsp * elanes
            for hh in range(2):
                h = 2 * j + hh
                hmask = (lane // SSM_HEAD_DIM) == hh
                dyh16 = jnp.where(hmask, dyp, 0.0).astype(BF16)
                m = g * lms[hh]
                dm = lax.dot_general(dyh16, xd16, (((1,), (1,)), ((), ())), preferred_element_type=F32)
                w = dm * m
                dg = dg + dm * lms[hh]
                dxd = dxd + lax.dot_general(m.astype(BF16), dyh16, (((0,), (0,)), ((), ())),
                                            preferred_element_type=F32)
                term = jnp.sum(jnp.where(hmask, xde * bdsp, 0.0), axis=1, keepdims=True)
                dcs_col = (jnp.sum(w, axis=1, keepdims=True)
                           + jnp.sum(jnp.where(hmask, dyp * yoffp, 0.0), axis=1, keepdims=True) - term)
                rows = slice(h * SSM_HEAD_DIM, (h + 1) * SSM_HEAD_DIM)
                dec = jnp.exp(cs_ref[h:h + 1, cl - 1:cl])
                tail = jnp.sum(term, axis=0, keepdims=True) + dec * jnp.sum(
                    jnp.sum(s_prev[rows, :] * ds[rows, :], axis=1, keepdims=True), axis=0, keepdims=True)
                last = lax.broadcasted_iota(jnp.int32, (1, cl), 1) == cl - 1
                dcs_ref[h:h + 1, :] = _col_to_row(dcs_col) - jnp.sum(w, axis=0, keepdims=True) + jnp.where(last, tail, 0.0)
                dcs_ref[hpg + h:hpg + h + 1, :] = _col_to_row(
                    jnp.sum(jnp.where(hmask, dxd * xp, 0.0), axis=1, keepdims=True))
            dx_act = dxd * dlanes + dyp * dsk_ref[0:1, cols]
            dx_ref[:, cols] = (dx_act * _silu_grad(px_ref[:, cols].astype(F32))).astype(dx_ref.dtype)
            dsk_out[0:1, cols] += jnp.sum(dyp * xp, axis=0, keepdims=True)
            xde_parts.append(xde.astype(BF16))
            dye_parts.append((dyp * olanes).astype(BF16))
        xde16 = jnp.concatenate(xde_parts, axis=1)
        dye16 = jnp.concatenate(dye_parts, axis=1)
        dg16 = dg.astype(BF16)
        dc_act = jnp.dot(dg16, b16, preferred_element_type=F32) + jnp.dot(dye16, s16, preferred_element_type=F32)
        db_act = lax.dot_general(dg16, c16, (((0,), (0,)), ((), ())), preferred_element_type=F32) \
            + jnp.dot(xde16, ds16, preferred_element_type=F32)
        dc_ref[...] = (dc_act * _silu_grad(pc_ref[...].astype(F32))).astype(dc_ref.dtype)
        db_ref[...] = (db_act * _silu_grad(pb_ref[...].astype(F32))).astype(db_ref.dtype)
        ds_new = lax.dot_general(dye16, c16, (((0,), (0,)), ((), ())), preferred_element_type=F32)
        for h in range(hpg):
            rows = slice(h * SSM_HEAD_DIM, (h + 1) * SSM_HEAD_DIM)
            ds_ref[rows, :] = ds[rows, :] * jnp.exp(cs_ref[h:h + 1, cl - 1:cl]) + ds_new[rows, :]
        li = lax.broadcasted_iota(jnp.int32, (cl, cl), 0)
        si = lax.broadcasted_iota(jnp.int32, (cl, cl), 1)
        d_adt = jnp.dot(dcs_ref[0:hpg, :], (li >= si).astype(F32), precision=lax.Precision.HIGHEST,
                        preferred_element_type=F32)
        ddt = d_adt * a + dcs_ref[hpg:2 * hpg, :]
        ddt_raw = ddt * jax.nn.sigmoid(dtr_ref[...] + bias_ref[...])
        ddt_ref[...] = ddt_raw
        acc_ref[0:hpg, :] += d_adt * dt
        acc_ref[hpg:2 * hpg, :] += ddt_raw

    rc = lambda c: nc - 1 - c
    vec = lambda n: pl.BlockSpec((None, hpg, n), lambda gi, c: (gi, 0, 0))
    x_spec = pl.BlockSpec((cl, gw), lambda gi, c: (rc(c), gi))
    b_spec = pl.BlockSpec((cl, SSM_STATE), lambda gi, c: (rc(c), bo + gi))
    c_spec = pl.BlockSpec((cl, SSM_STATE), lambda gi, c: (rc(c), bo + ng + gi))
    dt_spec = pl.BlockSpec((None, hpg, cl), lambda gi, c: (gi, 0, rc(c)))
    return pl.pallas_call(
        body, name="ssd_bwd",
        out_shape=(jax.ShapeDtypeStruct((t, d_inner), BF16), jax.ShapeDtypeStruct((t, ng * SSM_STATE), BF16),
                   jax.ShapeDtypeStruct((t, ng * SSM_STATE), BF16), jax.ShapeDtypeStruct(dtT.shape, F32),
                   jax.ShapeDtypeStruct((ng, 2 * hpg, cl), F32), jax.ShapeDtypeStruct((1, d_inner), F32)),
        grid=(ng, nc),
        in_specs=[x_spec, b_spec, c_spec, dt_spec, vec(1), vec(1),
                  pl.BlockSpec((1, gw), lambda gi, c: (0, gi)),
                  pl.BlockSpec((None, None, gw, SSM_STATE), lambda gi, c: (gi, rc(c), 0, 0)),
                  x_spec],
        out_specs=(x_spec, pl.BlockSpec((cl, SSM_STATE), lambda gi, c: (rc(c), gi)),
                   pl.BlockSpec((cl, SSM_STATE), lambda gi, c: (rc(c), gi)), dt_spec,
                   pl.BlockSpec((None, 2 * hpg, cl), lambda gi, c: (gi, 0, 0)),
                   pl.BlockSpec((1, gw), lambda gi, c: (0, gi))),
        scratch_shapes=[pltpu.VMEM((gw, SSM_STATE), F32), pltpu.VMEM((2 * hpg, cl), F32),
                        pltpu.VMEM((2 * hpg, cl), F32)],
        compiler_params=pltpu.CompilerParams(dimension_semantics=("parallel", "arbitrary")),
    )(pre, pre, pre, dtT, bias, alog, dskip_lanes, states, dy)


def _gate_norm_fwd(y, zx, norm_w, d_inner):
    t = y.shape[0]
    tr = _pick(t, (256, 128))
    gs = d_inner // SSM_GROUPS

    def body(y_ref, z_ref, w_ref, o_ref):
        for gi in range(SSM_GROUPS):
            cols = slice(gi * gs, (gi + 1) * gs)
            v = y_ref[:, cols] * _silu(z_ref[:, cols].astype(F32))
            r = lax.rsqrt(jnp.mean(v * v, axis=-1, keepdims=True) + NORM_EPS)
            o_ref[:, cols] = (v * r * w_ref[0:1, cols]).astype(BF16)

    spec = pl.BlockSpec((tr, d_inner), lambda i: (i, 0))
    return pl.pallas_call(
        body, name="ssm_gate_norm_fwd", out_shape=jax.ShapeDtypeStruct((t, d_inner), BF16), grid=(t // tr,),
        in_specs=[spec, spec, pl.BlockSpec((1, d_inner), lambda i: (0, 0))], out_specs=spec,
        compiler_params=pltpu.CompilerParams(dimension_semantics=("parallel",)),
    )(y, zx, norm_w.reshape(1, -1))


def _gate_norm_bwd(y, zx, norm_w, dout, d_inner):
    t, width = zx.shape
    tr = _pick(t, (256, 128))
    gs = d_inner // SSM_GROUPS

    def body(y_ref, z_ref, w_ref, do_ref, dy_ref, dz_ref, dw_ref):
        @pl.when(pl.program_id(0) == 0)
        def _():
            dw_ref[...] = jnp.zeros_like(dw_ref)

        for gi in range(SSM_GROUPS):
            cols = slice(gi * gs, (gi + 1) * gs)
            yv, zv = y_ref[:, cols], z_ref[:, cols].astype(F32)
            sz = _silu(zv)
            v = yv * sz
            r = lax.rsqrt(jnp.mean(v * v, axis=-1, keepdims=True) + NORM_EPS)
            vhat = v * r
            dn = do_ref[:, cols].astype(F32)
            dw_ref[0:1, cols] += jnp.sum(dn * vhat, axis=0, keepdims=True)
            dvh = dn * w_ref[0:1, cols]
            dv = r * (dvh - vhat * jnp.mean(dvh * vhat, axis=-1, keepdims=True))
            dy_ref[:, cols] = dv * sz
            dz_ref[:, cols] = (dv * yv * _silu_grad(zv)).astype(dz_ref.dtype)

    spec = pl.BlockSpec((tr, d_inner), lambda i: (i, 0))
    wspec = pl.BlockSpec((1, d_inner), lambda i: (0, 0))
    return pl.pallas_call(
        body, name="ssm_gate_norm_bwd",
        out_shape=(jax.ShapeDtypeStruct((t, d_inner), F32), jax.ShapeDtypeStruct((t, width), zx.dtype),
                   jax.ShapeDtypeStruct((1, d_inner), F32)),
        grid=(t // tr,),
        in_specs=[spec, spec, wspec, spec], out_specs=(spec, spec, wspec),
        compiler_params=pltpu.CompilerParams(dimension_semantics=("arbitrary",)),
    )(y, zx, norm_w.reshape(1, -1), dout)


def _ssm_small(dt_raw, dt_bias, a_log, d_skip):
    heads = dt_raw.shape[1]
    hpg = heads // SSM_GROUPS
    dtT = dt_raw.T.reshape(SSM_GROUPS, hpg, -1)
    return (dtT, dt_bias.reshape(SSM_GROUPS, hpg, 1), a_log.reshape(SSM_GROUPS, hpg, 1),
            jnp.repeat(d_skip, SSM_HEAD_DIM).reshape(1, -1))


def _ssm_core_fwd(zx, dt_raw, conv_w, conv_b, dt_bias, a_log, d_skip, norm_w):
    d_inner = norm_w.shape[0]
    pre = _conv_fwd(zx, conv_w, conv_b, d_inner)
    dtT, bias, alog, dsk = _ssm_small(dt_raw, dt_bias, a_log, d_skip)
    y, states = _ssd_fwd(pre, dtT, bias, alog, dsk, d_inner)
    out = _gate_norm_fwd(y, zx, norm_w, d_inner)
    return out, (zx, dt_raw, conv_w, dt_bias, a_log, d_skip, norm_w, pre, y, states)


def _ssm_core_bwd(res, dout):
    zx, dt_raw, conv_w, dt_bias, a_log, d_skip, norm_w, pre, y, states = res
    d_inner = norm_w.shape[0]
    heads = dt_raw.shape[1]
    dy, dzx, dnorm = _gate_norm_bwd(y, zx, norm_w, dout, d_inner)
    dtT, bias, alog, dsk = _ssm_small(dt_raw, dt_bias, a_log, d_skip)
    dx, db, dc, ddtT, acc, dsk_l = _ssd_bwd(pre, dtT, bias, alog, dsk, states, dy, d_inner)
    dws, dbs, col0 = [], [], 0
    for part in (dx, db, dc):
        dzx, dw_part, db_part = _conv_bwd(zx, conv_w, part, dzx, d_inner, col0)
        dws.append(dw_part)
        dbs.append(db_part)
        col0 += part.shape[1]
    dconv_w, dconv_b = jnp.concatenate(dws, axis=1), jnp.concatenate(dbs, axis=1)
    d_dt_raw = ddtT.reshape(heads, -1).T
    hpg = heads // SSM_GROUPS
    da = acc[:, :hpg].sum(-1).reshape(heads)
    d_bias = acc[:, hpg:].sum(-1).reshape(heads)
    d_alog = da * (-jnp.exp(a_log))
    d_dskip = dsk_l.reshape(heads, SSM_HEAD_DIM).sum(-1)
    return dzx, d_dt_raw, dconv_w, dconv_b.reshape(-1), d_bias, d_alog, d_dskip, dnorm.reshape(-1)


ROWS_VMEM_BYTES = 24 * 1024 * 1024


def _rows_call(body, name, ins, outs, acc_outs=(), rows=256):
    t = max(a.shape[0] for a in ins)
    row_bytes = sum(a.shape[1] * jnp.dtype(a.dtype).itemsize for a in list(ins) + list(outs) if a.shape[0] == t)
    if 2 * 2 * rows * row_bytes <= ROWS_VMEM_BYTES:
        rows *= 2
    tr = _pick(t, (rows, 256, 128, 64, 32, 16, 8))

    def spec(a):
        if a.shape[0] == t:
            return pl.BlockSpec((tr, a.shape[1]), lambda i: (i, 0))
        return pl.BlockSpec(a.shape, lambda i: (0, 0))

    return pl.pallas_call(
        body, name=name, out_shape=tuple(outs) + tuple(acc_outs), grid=(t // tr,),
        in_specs=[spec(a) for a in ins],
        out_specs=tuple(spec(a) for a in outs) + tuple(pl.BlockSpec(a.shape, lambda i: (0, 0)) for a in acc_outs),
        compiler_params=pltpu.CompilerParams(dimension_semantics=("arbitrary" if acc_outs else "parallel",)),
    )(*ins)


def _rms_fwd(x, gain, after=None):
    def body(x_ref, g_ref, *rest):
        v = x_ref[...]
        rest[-1][...] = (v * lax.rsqrt(jnp.mean(v * v, axis=-1, keepdims=True) + NORM_EPS) * g_ref[...]).astype(BF16)

    ins = [x, gain.reshape(1, -1)] + ([] if after is None else [after])
    (h,) = _rows_call(body, "rms_fwd", ins, [jax.ShapeDtypeStruct(x.shape, BF16)])
    return h


def _rms_bwd(x, gain, dh, dres, after):
    def body(x_ref, g_ref, dh_ref, dr_ref, *rest):
        dx_ref, dg_ref = rest[-2:]

        @pl.when(pl.program_id(0) == 0)
        def _():
            dg_ref[...] = jnp.zeros_like(dg_ref)

        v = x_ref[...]
        r = lax.rsqrt(jnp.mean(v * v, axis=-1, keepdims=True) + NORM_EPS)
        vhat = v * r
        d = dh_ref[...].astype(F32)
        dg_ref[...] += jnp.sum(d * vhat, axis=0, keepdims=True)
        dvh = d * g_ref[...]
        dx_ref[...] = dr_ref[...] + r * (dvh - vhat * jnp.mean(dvh * vhat, axis=-1, keepdims=True))

    ins = [x, gain.reshape(1, -1), dh, dres] + ([] if after is None else [after])
    dx, dg = _rows_call(body, "rms_bwd", ins, [jax.ShapeDtypeStruct(x.shape, F32)],
                        [jax.ShapeDtypeStruct((1, x.shape[1]), F32)])
    return dx, dg.reshape(gain.shape)


def _swiglu_fwd(gu):
    t, f2 = gu.shape
    f = f2 // 2

    def body(gu_ref, o_ref):
        o_ref[...] = (_silu(gu_ref[:, :f].astype(F32)) * gu_ref[:, f:].astype(F32)).astype(BF16)

    (act,) = _rows_call(body, "swiglu_fwd", [gu], [jax.ShapeDtypeStruct((t, f), BF16)])
    return act


def _swiglu_bwd(gu, dact):
    t, f2 = gu.shape
    f = f2 // 2

    def body(gu_ref, d_ref, o_ref):
        g, u, d = gu_ref[:, :f].astype(F32), gu_ref[:, f:].astype(F32), d_ref[...].astype(F32)
        o_ref[:, :f] = (d * u * _silu_grad(g)).astype(BF16)
        o_ref[:, f:] = (d * _silu(g)).astype(BF16)

    (dgu,) = _rows_call(body, "swiglu_bwd", [gu, dact], [jax.ShapeDtypeStruct((t, f2), BF16)])
    return dgu


def _ple_fwd(x, gl, ple):
    def body(x_ref, g_ref, p_ref, o_ref):
        o_ref[...] = x_ref[...] + jax.nn.sigmoid(g_ref[...].astype(F32)) * p_ref[...].astype(F32)

    (out,) = _rows_call(body, "ple_fwd", [x, gl, ple], [jax.ShapeDtypeStruct(x.shape, F32)])
    return out


def _ple_bwd(gl, ple, dout):
    def body(g_ref, p_ref, d_ref, dg_ref, dp_ref):
        s, d = jax.nn.sigmoid(g_ref[...].astype(F32)), d_ref[...]
        dg_ref[...] = (d * p_ref[...].astype(F32) * s * (1.0 - s)).astype(BF16)
        dp_ref[...] = (d * s).astype(BF16)

    shp = jax.ShapeDtypeStruct(gl.shape, BF16)
    return _rows_call(body, "ple_bwd", [gl, ple, dout], [shp, shp])


def _loss_fwd(y, target):
    inv = 1.0 / y.shape[1]

    def body(y_ref, t_ref, d_ref, l_ref):
        @pl.when(pl.program_id(0) == 0)
        def _():
            l_ref[...] = jnp.zeros_like(l_ref)

        e = y_ref[...] - t_ref[...]
        d_ref[...] = e * inv
        part = jnp.sum(jnp.sum(e * e, axis=1, keepdims=True), axis=0, keepdims=True) * (0.5 * inv)
        l_ref[...] += jnp.broadcast_to(part, l_ref.shape)

    dy, acc = _rows_call(body, "loss_fwd", [y, target], [jax.ShapeDtypeStruct(y.shape, F32)],
                         [jax.ShapeDtypeStruct((SUBLANES, LANES), F32)])
    return acc[0, 0], dy


def local_step(small, fetch, emit, x, p, target):
    depth = small['norm_mix'].shape[0]
    ssm_small = ('ssm_conv_w', 'ssm_conv_b', 'ssm_dt_bias', 'ssm_a_log', 'ssm_d_skip', 'ssm_norm_w')
    saved = []
    for i in range(depth):
        j = i // 2
        s = {'x': x}
        wm, token = fetch(('mix', i), x)
        h = s['h'] = _rms_fwd(x, small['norm_mix'][i], token)
        if i % 2 == 0:
            n_main = wm['ssm_w_in'].shape[0] - small['ssm_dt_bias'].shape[1]
            zx = _mm(h, wm['ssm_w_in'][:n_main], tb=True, out_dtype=BF16, name="ssm_in_fwd")
            more, token = fetch(('out', i), zx)
            wm = {**wm, **more}
            dt_raw = _mm(h, wm['ssm_w_in'][n_main:], tb=True, after=token, name="ssm_dt_fwd")
            y, s['mix'] = _ssm_core_fwd(zx, dt_raw, wm['ssm_conv_w'], *[small[n][j] for n in ssm_small[1:]])
            x = _mm(y, wm['ssm_w_out'], add=x, name="ssm_out_fwd")
        else:
            qkv = _mm(h, wm['att_w_qkv'], tb=True, out_dtype=BF16, name="att_qkv_fwd")
            y, s['mix'] = _attention_core_fwd(qkv, small['att_q_norm'][j], small['att_k_norm'][j])
            x = _mm(y, wm['att_w_o'], add=x, name="att_o_fwd")
        s['wm'], s['y'], s['x1'] = wm, y, x
        wf, token = fetch(('ffn', i), x)
        s['wf'] = wf
        h2 = s['h2'] = _rms_fwd(x, small['norm_ffn'][i], token)
        gu = s['gu'] = _mm(h2, wf['ffn_w_gu'], tb=True, out_dtype=BF16, name="ffn_gu_fwd")
        act = s['act'] = _swiglu_fwd(gu)
        x = s['x2'] = _mm(act, wf['ffn_w_down'], add=x, name="ffn_down_fwd")
        gl = s['gl'] = _mm(x, wf['ple_w_gate'], out_dtype=BF16, name="ple_gate_fwd")
        ple = s['ple'] = _mm(p[i], wf['ple_w_proj'], tb=True, out_dtype=BF16, name="ple_proj_fwd")
        x = _ple_fwd(x, gl, ple)
        saved.append(s)
    loss, dx = _loss_fwd(x, target)

    g = {n: [None] * small[n].shape[0] for n in small}
    for i in reversed(range(depth)):
        j = i // 2
        s = saved[i]
        wm, wf = s['wm'], s['wf']
        gf = {}
        dgl, dple = _ple_bwd(s['gl'], s['ple'], dx)
        gf['ple_w_proj'] = _mm(dple, p[i], ta=True, out_dtype=BF16, name="ple_proj_dw")
        gf['ple_w_gate'] = _mm(s['x2'], dgl, ta=True, out_dtype=BF16, name="ple_gate_dw")
        dx = _mm(dgl, wf['ple_w_gate'], tb=True, add=dx, name="ple_gate_da")
        dact = _mm(dx, wf['ffn_w_down'], tb=True, out_dtype=BF16, name="ffn_down_da")
        gf['ffn_w_down'] = _mm(s['act'], dx, ta=True, out_dtype=BF16, name="ffn_down_dw")
        dgu = _swiglu_bwd(s['gu'], dact)
        dh2 = _mm(dgu, wf['ffn_w_gu'], out_dtype=BF16, name="ffn_gu_da")
        gf['ffn_w_gu'] = _mm(dgu, s['h2'], ta=True, out_dtype=BF16, name="ffn_gu_dw")
        dx, g['norm_ffn'][i] = _rms_bwd(s['x1'], small['norm_ffn'][i], dh2, dx, emit(('ffn', i), gf))
        gm = {}
        if i % 2 == 0:
            n_main = wm['ssm_w_in'].shape[0] - small['ssm_dt_bias'].shape[1]
            dyn = _mm(dx, wm['ssm_w_out'], tb=True, out_dtype=BF16, name="ssm_out_da")
            gm['ssm_w_out'] = _mm(s['y'], dx, ta=True, out_dtype=BF16, name="ssm_out_dw")
            dzx, d_dt, *sg = _ssm_core_bwd(s['mix'], dyn)
            for n, v in zip(ssm_small, sg):
                g[n][j] = v
            dh = _mm(d_dt, wm['ssm_w_in'][n_main:], name="ssm_dt_da")
            dh = _mm(dzx, wm['ssm_w_in'][:n_main], add=dh, out_dtype=BF16, name="ssm_in_da")
            gm['ssm_w_in'] = jnp.concatenate([_mm(dzx, s['h'], ta=True, out_dtype=BF16, name="ssm_in_dw"),
                                              _mm(d_dt, s['h'], ta=True, out_dtype=BF16, name="ssm_dt_dw")], axis=0)
        else:
            do = _mm(dx, wm['att_w_o'], tb=True, name="att_o_da")
            gm['att_w_o'] = _mm(s['y'], dx, ta=True, out_dtype=BF16, name="att_o_dw")
            dqkv, g['att_q_norm'][j], g['att_k_norm'][j] = _attention_core_bwd(s['mix'], do)
            dh = _mm(dqkv, wm['att_w_qkv'], out_dtype=BF16, name="att_qkv_da")
            gm['att_w_qkv'] = _mm(dqkv, s['h'], ta=True, out_dtype=BF16, name="att_qkv_dw")
        dx, g['norm_mix'][i] = _rms_bwd(s['x'], small['norm_mix'][i], dh, dx, emit(('mix', i), gm))
    return loss, dx, {n: jnp.stack(v) for n, v in g.items()}


CHANNEL_WEIGHTS = ('ffn_w_gate', 'ffn_w_up', 'ffn_w_down', 'ple_w_proj', 'ple_w_gate')


def _stages(depth):
    gather, scatter = {}, {}
    for i in range(depth):
        j = i // 2
        if i % 2 == 0:
            gather['mix', i] = [('ssm_w_in', j), ('ssm_conv_w', j)]
            gather['out', i] = [('ssm_w_out', j)]
            scatter['mix', i] = [('ssm_w_in', j), ('ssm_w_out', j)]
        else:
            gather['mix', i] = scatter['mix', i] = [('att_w_qkv', j), ('att_w_o', j)]
        gather['ffn', i] = scatter['ffn', i] = [(n, i) for n in CHANNEL_WEIGHTS]
    return gather, scatter


def _pack_plan(shapes, width, members):
    plan, off = [], 0
    for name, lyr in members:
        _, r, c = shapes[name]
        if name in COL_SHARDED:
            r, c = c, r
        if name == 'ssm_conv_w':
            pr = -(-2 * r * c // width)
        else:
            assert (r * c) % width == 0, (name, r, c)
            pr = r * c // width
        plan.append((name, lyr, r, c, pr, off))
        off += _round_up(pr, BF16_ROWS)
    return plan, off


def _small_plan(shapes):
    plan, off = [], 0
    for name in SMALL:
        n = math.prod(shapes[name])
        plan.append((name, n, off))
        off += n
    return plan, _round_up(off, SUBLANES * LANES)


def kernel(x, p, norm_mix, norm_ffn, ssm_w_in, ssm_conv_w, ssm_conv_b, ssm_dt_bias, ssm_a_log, ssm_d_skip, ssm_norm_w, ssm_w_out, att_w_qkv, att_q_norm, att_k_norm, att_w_o, ffn_w_gate, ffn_w_up, ffn_w_down, ple_w_proj, ple_w_gate, loss_target, m_norm_mix, m_norm_ffn, m_ssm_w_in, m_ssm_conv_w, m_ssm_conv_b, m_ssm_dt_bias, m_ssm_a_log, m_ssm_d_skip, m_ssm_norm_w, m_ssm_w_out, m_att_w_qkv, m_att_q_norm, m_att_k_norm, m_att_w_o, m_ffn_w_gate, m_ffn_w_up, m_ffn_w_down, m_ple_w_proj, m_ple_w_gate, v_norm_mix, v_norm_ffn, v_ssm_w_in, v_ssm_conv_w, v_ssm_conv_b, v_ssm_dt_bias, v_ssm_a_log, v_ssm_d_skip, v_ssm_norm_w, v_ssm_w_out, v_att_w_qkv, v_att_q_norm, v_att_k_norm, v_att_w_o, v_ffn_w_gate, v_ffn_w_up, v_ffn_w_down, v_ple_w_proj, v_ple_w_gate):
    given = dict(locals())
    w_in = {n: given[n] for n in WEIGHTS}
    m_in = {n: given["m_" + n] for n in WEIGHTS}
    v_in = {n: given["v_" + n] for n in WEIGHTS}
    width = x.shape[-1]
    depth = norm_mix.shape[0]

    gather_members, scatter_members = _stages(depth)
    shapes = {n: w_in[n].shape for n in BIG + ('ssm_conv_w',)}
    plans = {key: _pack_plan(shapes, width, members)[0] for key, members in gather_members.items()}
    splans = {key: _pack_plan(shapes, width, members)[0] for key, members in scatter_members.items()}
    order = list(gather_members)

    def pack_weights(stage):
        pieces = []
        for name, layer, r, c, pr, off in plans[stage]:
            blk = w_in[name][layer]
            if name == 'ssm_conv_w':
                blk = lax.bitcast_convert_type(blk.reshape(-1), BF16).reshape(-1)
                blk = jnp.pad(blk, (0, pr * width - blk.shape[0]))
            elif name in COL_SHARDED:
                blk = blk.T
            blk = blk.astype(BF16).reshape(pr, width)
            pieces.append(jnp.pad(blk, ((0, _round_up(pr, BF16_ROWS) - pr), (0, 0))))
        return jnp.concatenate(pieces, axis=0)

    packed = [pack_weights(order[0])]
    pending = [exchange_start(packed[0], True, "gather_start_0")]
    packed += [pack_weights(stage) for stage in order[1:]]

    def fetch(stage, after):
        k = order.index(stage)
        handle, token = pending[k]
        land = exchange_wait(handle, [token] + packed[1:] if k == 0 else after, True, f"gather_wait_{k}")
        token = None
        if k + 1 < len(order):
            pending.append(exchange_start(packed[k + 1], True, f"gather_start_{k + 1}", land))
            token = pending[-1][1]
        got = {}
        for name, layer, r, c, pr, off in plans[stage]:
            piece = land[:, off:off + pr]
            if name == 'ssm_conv_w':
                taps, chans = w_in[name].shape[1:]
                bits = piece.reshape(N_DEV, -1)[:, :2 * taps * chans].reshape(N_DEV, taps * chans, 2)
                piece = lax.bitcast_convert_type(bits, F32).reshape(N_DEV, taps, chans)
                got[name] = piece.transpose(1, 0, 2).reshape(taps, N_DEV * chans)
            else:
                got[name] = piece.reshape(N_DEV * r, c)
        if 'ffn_w_gate' in got:
            got['ffn_w_gu'] = jnp.concatenate([got.pop('ffn_w_gate'), got.pop('ffn_w_up')], axis=0)
        return got, token

    scatters = {}

    def emit(stage, grads):
        grads = dict(grads)
        if 'ffn_w_gu' in grads:
            hidden = grads['ffn_w_gu'].shape[0] // 2
            grads['ffn_w_gate'], grads['ffn_w_up'] = grads['ffn_w_gu'][:hidden], grads['ffn_w_gu'][hidden:]
        pieces = []
        for name, layer, r, c, pr, off in splans[stage]:
            g = grads[name].reshape(N_DEV, pr, width)
            pieces.append(jnp.pad(g, ((0, 0), (0, _round_up(pr, BF16_ROWS) - pr), (0, 0))))
        scatters[stage], token = exchange_start(jnp.concatenate(pieces, axis=1), False,
                                                f"scatter_start_{len(scatters)}")
        return token

    small = {n: w_in[n] for n in SMALL}
    cs = w_in['ssm_conv_w'].shape[2]
    loss_local, gx, gw = local_step(small, fetch, emit, x[0], p[:, 0], loss_target[0])
    loss = lax.psum(loss_local, ("x", "y", "c"))

    parts = {}
    for k, (stage, handle) in enumerate(scatters.items()):
        received = exchange_wait(handle, gx, False, f"scatter_wait_{k}")
        gsum = sum_slots(received, f"sum_grads_{k}")
        for name, layer, r, c, pr, off in splans[stage]:
            g = gsum[off:off + pr].reshape(r, c)
            parts[name, layer] = g.T if name in COL_SHARDED else g
    grads = {n: jnp.stack([parts[n, layer] for layer in range(w_in[n].shape[0])]) for n in BIG}

    splan, stotal = _small_plan({n: gw[n].shape for n in SMALL})
    svec = jnp.concatenate([gw[n].reshape(-1) for n, _, _ in splan])
    svec = jnp.pad(svec, (0, stotal - svec.shape[0])).reshape(stotal // LANES, LANES)
    _, ssum = all_gather_sum_small(svec, "sum_small_grads")
    ssum = ssum.reshape(-1)
    for name, n, off in splan:
        grads[name] = ssum[off:off + n].reshape(gw[name].shape)
    me = _me()
    grads['ssm_conv_w'] = lax.dynamic_slice_in_dim(grads['ssm_conv_w'], me * cs, cs, axis=2)

    delta, new_m, new_v = {}, {}, {}
    for name in BIG:
        shp = w_in[name].shape
        flat = lambda a: a.reshape(-1, shp[-1])
        d, nm, nv = adamw(flat(w_in[name]), flat(grads[name]), flat(m_in[name]), flat(v_in[name]), "adamw_" + name)
        delta[name], new_m[name], new_v[name] = d.reshape(shp), nm.reshape(shp), nv.reshape(shp)
    splan2, stotal2 = _small_plan({n: w_in[n].shape for n in SMALL})

    def pack_small(src):
        vec = jnp.concatenate([src[n].reshape(-1) for n, _, _ in splan2])
        return jnp.pad(vec, (0, stotal2 - vec.shape[0]), constant_values=1.0).reshape(stotal2 // LANES, LANES)

    sd, snm, snv = adamw(pack_small(w_in), pack_small(grads), pack_small(m_in), pack_small(v_in), "adamw_small")
    for name, n, off in splan2:
        shp = w_in[name].shape
        delta[name] = sd.reshape(-1)[off:off + n].reshape(shp)
        new_m[name] = snm.reshape(-1)[off:off + n].reshape(shp)
        new_v[name] = snv.reshape(-1)[off:off + n].reshape(shp)

    return (loss, gx[None], *[grads[n] for n in WEIGHTS], *[delta[n] for n in WEIGHTS],
            *[new_m[n] for n in WEIGHTS], *[new_v[n] for n in WEIGHTS])
```
